```python
import jax, jax.numpy as jnp
from jax import lax
import numpy as np

D_MODEL = 1024
BATCH = 8
SEQ = 8192
DEPTH = 2

CONV_CHANNELS = 512
CONV_GROUPS = 8
CONV_WIDTH = 3
N_Q_HEADS = 8
N_KV_HEADS = 2
HEAD_DIM = 64
ATTN_WIDTH = N_Q_HEADS * HEAD_DIM
WINDOW = 128
BLOCK = 128
MIX_WIDTH = CONV_CHANNELS + ATTN_WIDTH
IN_COLS = 3 * CONV_CHANNELS + (N_Q_HEADS + 2 * N_KV_HEADS) * HEAD_DIM
D_FF = -((-8 * D_MODEL) // (3 * 256)) * 256
EPS = 1e-6
NEG_INF = -1e30

kernel_name = "hymba_style_conv_swa_sink_hybrid"


def rms_norm(x, g):
    xf = x.astype(jnp.float32)
    y = xf * lax.rsqrt(jnp.mean(xf * xf, axis=-1, keepdims=True) + EPS)
    return (y * g.astype(jnp.float32)).astype(x.dtype)


def short_gated_conv(b_gate, c_gate, h, conv_w):
    seq = h.shape[1]
    u = c_gate * h
    up = jnp.pad(u, ((0, 0), (CONV_WIDTH - 1, 0), (0, 0)))
    y = conv_w[0] * up[:, 0:seq]
    for tap in range(1, CONV_WIDTH):
        y = y + conv_w[tap] * up[:, tap:tap + seq]
    return b_gate * y


def band_keys(t, nb):
    b = t.shape[0]
    tb = t.reshape(b, nb, BLOCK, t.shape[2], t.shape[3])
    prev = jnp.pad(tb[:, :-1], ((0, 0), (1, 0), (0, 0), (0, 0), (0, 0)))
    return jnp.concatenate([prev, tb], axis=2)


def sliding_window_attention_with_sinks(q, k, v, sinks):
    b, seq = q.shape[0], q.shape[1]
    nb = seq // BLOCK
    grp = N_Q_HEADS // N_KV_HEADS
    qb = q.reshape(b, nb, BLOCK, N_KV_HEADS, grp, HEAD_DIM)
    kb = band_keys(k, nb)
    vb = band_keys(v, nb)
    scale = HEAD_DIM ** -0.5
    s = jnp.einsum('bnqhgd,bnkhd->bnhgqk', qb, kb).astype(jnp.float32) * scale
    qpos = jnp.arange(nb)[:, None] * BLOCK + jnp.arange(BLOCK)[None, :]
    kpos = (jnp.arange(nb)[:, None] - 1) * BLOCK + jnp.arange(2 * BLOCK)[None, :]
    diff = qpos[:, :, None] - kpos[:, None, :]
    valid = (diff >= 0) & (diff < WINDOW) & (kpos[:, None, :] >= 0)
    s = jnp.where(valid[None, :, None, None], s, NEG_INF)
    sink = sinks.astype(jnp.float32).reshape(N_KV_HEADS, grp)[None, None, :, :, None, None]
    m = jnp.maximum(jnp.max(s, axis=-1, keepdims=True), sink)
    p = jnp.exp(s - m)
    denom = jnp.sum(p, axis=-1, keepdims=True) + jnp.exp(sink - m)
    probs = (p / denom).astype(v.dtype)
    o = jnp.einsum('bnhgqk,bnkhd->bnqhgd', probs, vb)
    return o.reshape(b, seq, ATTN_WIDTH)


def _fwd_setup_inputs(seed: int = 0) -> dict:
    key = jax.random.key(seed)
    ks = jax.random.split(key, 16)
    f32 = jnp.float32

    def gain(k, shape):
        return 1.0 + 0.02 * jax.random.normal(k, shape, f32)

    return {
        "x": jax.random.normal(ks[0], (BATCH, SEQ, D_MODEL), f32),
        "norm1_g": gain(ks[1], (DEPTH, D_MODEL)),
        "w_in": jax.random.normal(ks[2], (DEPTH, D_MODEL, IN_COLS), f32) * D_MODEL ** -0.5,
        "conv_w": jax.random.normal(ks[3], (DEPTH, CONV_WIDTH, CONV_CHANNELS), f32) * CONV_WIDTH ** -0.5,
        "q_norm_g": gain(ks[4], (DEPTH, HEAD_DIM)),
        "k_norm_g": gain(ks[5], (DEPTH, HEAD_DIM)),
        "sinks": 0.5 * jax.random.normal(ks[6], (DEPTH, N_Q_HEADS), f32),
        "conv_out_g": gain(ks[7], (DEPTH, CONV_CHANNELS)),
        "attn_out_g": gain(ks[8], (DEPTH, ATTN_WIDTH)),
        "w_o": jax.random.normal(ks[9], (DEPTH, MIX_WIDTH, D_MODEL), f32) * MIX_WIDTH ** -0.5,
        "norm2_g": gain(ks[10], (DEPTH, D_MODEL)),
        "w_gate": jax.random.normal(ks[11], (DEPTH, D_MODEL, D_FF), f32) * D_MODEL ** -0.5,
        "w_up": jax.random.normal(ks[12], (DEPTH, D_MODEL, D_FF), f32) * D_MODEL ** -0.5,
        "w_down": jax.random.normal(ks[13], (DEPTH, D_FF, D_MODEL), f32) * D_FF ** -0.5,
    }


def _fwd_reference(x, norm1_g, w_in, conv_w, q_norm_g, k_norm_g, sinks, conv_out_g,
              attn_out_g, w_o, norm2_g, w_gate, w_up, w_down):
    b, seq = x.shape[0], x.shape[1]
    c = CONV_CHANNELS
    o_q = 3 * c
    o_k = o_q + ATTN_WIDTH
    o_v = o_k + N_KV_HEADS * HEAD_DIM
    for l in range(DEPTH):
        h = rms_norm(x, norm1_g[l])
        proj = h @ w_in[l]
        b_gate = proj[..., 0:c]
        c_gate = proj[..., c:2 * c]
        hc = proj[..., 2 * c:3 * c]
        q = proj[..., o_q:o_k].reshape(b, seq, N_Q_HEADS, HEAD_DIM)
        k = proj[..., o_k:o_v].reshape(b, seq, N_KV_HEADS, HEAD_DIM)
        v = proj[..., o_v:].reshape(b, seq, N_KV_HEADS, HEAD_DIM)

        conv_out = short_gated_conv(b_gate, c_gate, hc, conv_w[l])

        q = rms_norm(q, q_norm_g[l])
        k = rms_norm(k, k_norm_g[l])
        attn_out = sliding_window_attention_with_sinks(q, k, v, sinks[l])

        mix = jnp.concatenate([rms_norm(conv_out, conv_out_g[l]),
                               rms_norm(attn_out, attn_out_g[l])], axis=-1)
        x = x + mix @ w_o[l]

        h2 = rms_norm(x, norm2_g[l])
        x = x + (jax.nn.silu(h2 @ w_gate[l]) * (h2 @ w_up[l])) @ w_down[l]
    return x


import jax as _jax
import jax.numpy as _jnp

TWIN_FORMAT = 'train_step'
FWD_PARAMS = ['x', 'norm1_g', 'w_in', 'conv_w', 'q_norm_g', 'k_norm_g', 'sinks', 'conv_out_g', 'attn_out_g', 'w_o', 'norm2_g', 'w_gate', 'w_up', 'w_down']
TWIN_WEIGHTS = ['norm1_g', 'w_in', 'conv_w', 'q_norm_g', 'k_norm_g', 'sinks', 'conv_out_g', 'attn_out_g', 'w_o', 'norm2_g', 'w_gate', 'w_up', 'w_down']
TWIN_DIFF_INPUT = 'x'
TWIN_INPUTS = ['x', 'norm1_g', 'w_in', 'conv_w', 'q_norm_g', 'k_norm_g', 'sinks', 'conv_out_g', 'attn_out_g', 'w_o', 'norm2_g', 'w_gate', 'w_up', 'w_down', 'loss_target', 'm_norm1_g', 'm_w_in', 'm_conv_w', 'm_q_norm_g', 'm_k_norm_g', 'm_sinks', 'm_conv_out_g', 'm_attn_out_g', 'm_w_o', 'm_norm2_g', 'm_w_gate', 'm_w_up', 'm_w_down', 'v_norm1_g', 'v_w_in', 'v_conv_w', 'v_q_norm_g', 'v_k_norm_g', 'v_sinks', 'v_conv_out_g', 'v_attn_out_g', 'v_w_o', 'v_norm2_g', 'v_w_gate', 'v_w_up', 'v_w_down']
TWIN_OUTPUTS = ['loss', 'grad_x', 'grad_norm1_g', 'grad_w_in', 'grad_conv_w', 'grad_q_norm_g', 'grad_k_norm_g', 'grad_sinks', 'grad_conv_out_g', 'grad_attn_out_g', 'grad_w_o', 'grad_norm2_g', 'grad_w_gate', 'grad_w_up', 'grad_w_down', 'delta_norm1_g', 'delta_w_in', 'delta_conv_w', 'delta_q_norm_g', 'delta_k_norm_g', 'delta_sinks', 'delta_conv_out_g', 'delta_attn_out_g', 'delta_w_o', 'delta_norm2_g', 'delta_w_gate', 'delta_w_up', 'delta_w_down', 'new_m_norm1_g', 'new_m_w_in', 'new_m_conv_w', 'new_m_q_norm_g', 'new_m_k_norm_g', 'new_m_sinks', 'new_m_conv_out_g', 'new_m_attn_out_g', 'new_m_w_o', 'new_m_norm2_g', 'new_m_w_gate', 'new_m_w_up', 'new_m_w_down', 'new_v_norm1_g', 'new_v_w_in', 'new_v_conv_w', 'new_v_q_norm_g', 'new_v_k_norm_g', 'new_v_sinks', 'new_v_conv_out_g', 'new_v_attn_out_g', 'new_v_w_o', 'new_v_norm2_g', 'new_v_w_gate', 'new_v_w_up', 'new_v_w_down']
TWIN_LEAF_KINDS = {'loss': 'loss', 'grad_x': 'grad_x', 'grad_norm1_g': 'grad_w', 'grad_w_in': 'grad_w', 'grad_conv_w': 'grad_w', 'grad_q_norm_g': 'grad_w', 'grad_k_norm_g': 'grad_w', 'grad_sinks': 'grad_w', 'grad_conv_out_g': 'grad_w', 'grad_attn_out_g': 'grad_w', 'grad_w_o': 'grad_w', 'grad_norm2_g': 'grad_w', 'grad_w_gate': 'grad_w', 'grad_w_up': 'grad_w', 'grad_w_down': 'grad_w', 'delta_norm1_g': 'delta_w', 'delta_w_in': 'delta_w', 'delta_conv_w': 'delta_w', 'delta_q_norm_g': 'delta_w', 'delta_k_norm_g': 'delta_w', 'delta_sinks': 'delta_w', 'delta_conv_out_g': 'delta_w', 'delta_attn_out_g': 'delta_w', 'delta_w_o': 'delta_w', 'delta_norm2_g': 'delta_w', 'delta_w_gate': 'delta_w', 'delta_w_up': 'delta_w', 'delta_w_down': 'delta_w', 'new_m_norm1_g': 'new_m', 'new_m_w_in': 'new_m', 'new_m_conv_w': 'new_m', 'new_m_q_norm_g': 'new_m', 'new_m_k_norm_g': 'new_m', 'new_m_sinks': 'new_m', 'new_m_conv_out_g': 'new_m', 'new_m_attn_out_g': 'new_m', 'new_m_w_o': 'new_m', 'new_m_norm2_g': 'new_m', 'new_m_w_gate': 'new_m', 'new_m_w_up': 'new_m', 'new_m_w_down': 'new_m', 'new_v_norm1_g': 'new_v', 'new_v_w_in': 'new_v', 'new_v_conv_w': 'new_v', 'new_v_q_norm_g': 'new_v', 'new_v_k_norm_g': 'new_v', 'new_v_sinks': 'new_v', 'new_v_conv_out_g': 'new_v', 'new_v_attn_out_g': 'new_v', 'new_v_w_o': 'new_v', 'new_v_norm2_g': 'new_v', 'new_v_w_gate': 'new_v', 'new_v_w_up': 'new_v', 'new_v_w_down': 'new_v'}


def _forward(args):
    return _fwd_reference(*[args[k] for k in FWD_PARAMS])


def _output_shape():
    def fwd():
        inp = _fwd_setup_inputs(0)
        return _fwd_reference(*[inp[k] for k in FWD_PARAMS])
    out = _jax.eval_shape(fwd)
    return out.shape, out.dtype

N_MICROBATCH = 1
ADAM_LR = 0.001
ADAM_B1 = 0.9
ADAM_B2 = 0.999
ADAM_EPS = 1e-08
ADAM_WD = 0.01
ADAM_STEP = 10
PER_EXAMPLE_BATCH_AXIS = {'x': 0, 'loss_target': 0}
SHARED_INPUTS = []
_WEIGHT_DTYPES = {'norm1_g': _jnp.float32, 'w_in': _jnp.float32, 'conv_w': _jnp.float32, 'q_norm_g': _jnp.float32, 'k_norm_g': _jnp.float32, 'sinks': _jnp.float32, 'conv_out_g': _jnp.float32, 'attn_out_g': _jnp.float32, 'w_o': _jnp.float32, 'norm2_g': _jnp.float32, 'w_gate': _jnp.float32, 'w_up': _jnp.float32, 'w_down': _jnp.float32}
MOMENT_SCALE = {'norm1_g': 2.452862e+00, 'w_in': 1.667798e+00, 'conv_w': 1.486588e+00, 'q_norm_g': 2.120614e+00, 'k_norm_g': 2.106821e+00, 'sinks': 4.934952e-01, 'conv_out_g': 8.090827e+01, 'attn_out_g': 6.311039e+01, 'w_o': 2.965492e+00, 'norm2_g': 4.931356e+01, 'w_gate': 4.252473e-01, 'w_up': 4.997630e-01, 'w_down': 7.890675e-01}


def _to_microbatches(a, axis):
    t = _jnp.moveaxis(a, axis, 0)
    t = t.reshape((N_MICROBATCH, t.shape[0] // N_MICROBATCH) + t.shape[1:])
    return _jnp.moveaxis(t, 1, axis + 1)


def setup_inputs(seed: int = 0) -> dict:
    inp = _fwd_setup_inputs(seed)
    key = _jax.random.fold_in(_jax.random.key(seed), 7919)
    shape, _ = _output_shape()
    out = dict(inp)
    out["loss_target"] = _jax.random.normal(_jax.random.fold_in(key, 0), shape, _jnp.float32)
    for i, name in enumerate(TWIN_WEIGHTS):
        w = inp[name].astype(_jnp.float32)
        if MOMENT_SCALE is None:
            s = _jnp.sqrt(_jnp.mean(_jnp.square(w)) + 1e-30)
        else:
            s = MOMENT_SCALE[name]
        km, kv = _jax.random.split(_jax.random.fold_in(key, i + 1))
        out[name] = w
        out["m_" + name] = s * _jax.random.normal(km, w.shape, _jnp.float32)
        out["v_" + name] = (s * s) * _jax.random.uniform(kv, w.shape, _jnp.float32, 0.5, 1.5)
    if N_MICROBATCH > 1:
        for name, axis in PER_EXAMPLE_BATCH_AXIS.items():
            out[name] = _to_microbatches(out[name], axis)
    return {'x': out['x'], 'norm1_g': out['norm1_g'], 'w_in': out['w_in'], 'conv_w': out['conv_w'], 'q_norm_g': out['q_norm_g'], 'k_norm_g': out['k_norm_g'], 'sinks': out['sinks'], 'conv_out_g': out['conv_out_g'], 'attn_out_g': out['attn_out_g'], 'w_o': out['w_o'], 'norm2_g': out['norm2_g'], 'w_gate': out['w_gate'], 'w_up': out['w_up'], 'w_down': out['w_down'], 'loss_target': out['loss_target'], 'm_norm1_g': out['m_norm1_g'], 'm_w_in': out['m_w_in'], 'm_conv_w': out['m_conv_w'], 'm_q_norm_g': out['m_q_norm_g'], 'm_k_norm_g': out['m_k_norm_g'], 'm_sinks': out['m_sinks'], 'm_conv_out_g': out['m_conv_out_g'], 'm_attn_out_g': out['m_attn_out_g'], 'm_w_o': out['m_w_o'], 'm_norm2_g': out['m_norm2_g'], 'm_w_gate': out['m_w_gate'], 'm_w_up': out['m_w_up'], 'm_w_down': out['m_w_down'], 'v_norm1_g': out['v_norm1_g'], 'v_w_in': out['v_w_in'], 'v_conv_w': out['v_conv_w'], 'v_q_norm_g': out['v_q_norm_g'], 'v_k_norm_g': out['v_k_norm_g'], 'v_sinks': out['v_sinks'], 'v_conv_out_g': out['v_conv_out_g'], 'v_attn_out_g': out['v_attn_out_g'], 'v_w_o': out['v_w_o'], 'v_norm2_g': out['v_norm2_g'], 'v_w_gate': out['v_w_gate'], 'v_w_up': out['v_w_up'], 'v_w_down': out['v_w_down']}


def _loss(weights, diff, rest, loss_target):
    with _jax.named_scope("forward"):
        args = {**rest, TWIN_DIFF_INPUT: diff, **{k: w.astype(_WEIGHT_DTYPES[k]) for k, w in weights.items()}}
        y = _forward(args)
    with _jax.named_scope("loss_head"):
        err = _jnp.square(y.astype(_jnp.float32) - loss_target)
        return 0.5 * _jnp.sum(_jnp.mean(err, axis=-1)) if err.ndim else 0.5 * err


def _adamw(w, g, m, v):
    m = ADAM_B1 * m + (1.0 - ADAM_B1) * g
    v = ADAM_B2 * v + (1.0 - ADAM_B2) * _jnp.square(g)
    m_hat = m / (1.0 - ADAM_B1 ** ADAM_STEP)
    v_hat = v / (1.0 - ADAM_B2 ** ADAM_STEP)
    delta = -ADAM_LR * (m_hat / (_jnp.sqrt(v_hat) + ADAM_EPS) + ADAM_WD * w)
    return delta, m, v


def reference(x, norm1_g, w_in, conv_w, q_norm_g, k_norm_g, sinks, conv_out_g, attn_out_g, w_o, norm2_g, w_gate, w_up, w_down, loss_target, m_norm1_g, m_w_in, m_conv_w, m_q_norm_g, m_k_norm_g, m_sinks, m_conv_out_g, m_attn_out_g, m_w_o, m_norm2_g, m_w_gate, m_w_up, m_w_down, v_norm1_g, v_w_in, v_conv_w, v_q_norm_g, v_k_norm_g, v_sinks, v_conv_out_g, v_attn_out_g, v_w_o, v_norm2_g, v_w_gate, v_w_up, v_w_down):
    given = dict(x=x, norm1_g=norm1_g, w_in=w_in, conv_w=conv_w, q_norm_g=q_norm_g, k_norm_g=k_norm_g, sinks=sinks, conv_out_g=conv_out_g, attn_out_g=attn_out_g, w_o=w_o, norm2_g=norm2_g, w_gate=w_gate, w_up=w_up, w_down=w_down, loss_target=loss_target, m_norm1_g=m_norm1_g, m_w_in=m_w_in, m_conv_w=m_conv_w, m_q_norm_g=m_q_norm_g, m_k_norm_g=m_k_norm_g, m_sinks=m_sinks, m_conv_out_g=m_conv_out_g, m_attn_out_g=m_attn_out_g, m_w_o=m_w_o, m_norm2_g=m_norm2_g, m_w_gate=m_w_gate, m_w_up=m_w_up, m_w_down=m_w_down, v_norm1_g=v_norm1_g, v_w_in=v_w_in, v_conv_w=v_conv_w, v_q_norm_g=v_q_norm_g, v_k_norm_g=v_k_norm_g, v_sinks=v_sinks, v_conv_out_g=v_conv_out_g, v_attn_out_g=v_attn_out_g, v_w_o=v_w_o, v_norm2_g=v_norm2_g, v_w_gate=v_w_gate, v_w_up=v_w_up, v_w_down=v_w_down)
    weights = {n: given[n] for n in TWIN_WEIGHTS}
    shared = {n: given[n] for n in SHARED_INPUTS}
    per_example = {n: given[n] for n in ['x']}
    grad_fn = _jax.value_and_grad(_loss, argnums=(0, 1))

    def one_microbatch(ex, loss_target):
        ex = dict(ex)
        diff = ex.pop(TWIN_DIFF_INPUT)
        return grad_fn(weights, diff, {**shared, **ex}, loss_target)

    if N_MICROBATCH == 1:
        loss, (grad_w, grad_x) = one_microbatch(per_example, given["loss_target"])
    else:
        def body(carry, xs):
            loss_sum, grad_sum = carry
            l_k, (gw_k, gx_k) = one_microbatch(xs[0], xs[1])
            with _jax.named_scope("update"):
                return (loss_sum + l_k, _jax.tree.map(_jnp.add, grad_sum, gw_k)), gx_k

        init = (_jnp.zeros((), _jnp.float32), _jax.tree.map(_jnp.zeros_like, weights))
        (loss, grad_w), grad_x = _jax.lax.scan(body, init, (per_example, given["loss_target"]))
    with _jax.named_scope("update"):
        delta_w, new_m, new_v = {}, {}, {}
        for n in TWIN_WEIGHTS:
            delta_w[n], new_m[n], new_v[n] = _adamw(weights[n], grad_w[n], given["m_" + n], given["v_" + n])
    return (loss, grad_x, *[grad_w[n] for n in TWIN_WEIGHTS], *[delta_w[n] for n in TWIN_WEIGHTS],
            *[new_m[n] for n in TWIN_WEIGHTS], *[new_v[n] for n in TWIN_WEIGHTS])
```

```python
import functools

import jax
import jax.numpy as jnp
from jax import lax
from jax.experimental import pallas as pl
from jax.experimental.pallas import tpu as pltpu

F32 = jnp.float32
BF16 = jnp.bfloat16

D_MODEL = 1024
CONV_CH = 512
ATTN_W = 512
N_Q = 8
N_KV = 2
GRP = N_Q // N_KV
HEAD = 64
IN_COLS = 2304
D_FF = 2816
BLK = 128
O_Q = 3 * CONV_CH
O_K = O_Q + ATTN_W
O_V = O_K + N_KV * HEAD
EPS = 1e-6
NEG_INF = -1e30
SCALE = HEAD ** -0.5
N_DEV = 8
DEPTH = 2

ADAM_LR = 0.001
ADAM_B1 = 0.9
ADAM_B2 = 0.999
ADAM_EPS = 1e-08
ADAM_WD = 0.01
ADAM_STEP = 10

VMEM_LIMIT = 56 * 1024 * 1024
TM = 512
MESH_T = pl.DeviceIdType.MESH

ROWS_IN, ROWS_O, ROWS_FF = 288, 128, 352
PACK_ROWS = ROWS_IN + ROWS_O + 3 * ROWS_FF


def _cparams(sem):
    return pltpu.CompilerParams(dimension_semantics=sem, vmem_limit_bytes=VMEM_LIMIT)


def _dot(a, b):
    return jnp.dot(a, b, preferred_element_type=F32)


def _dot_nt(a, b):
    return lax.dot_general(a, b, (((1,), (1,)), ((), ())), preferred_element_type=F32)


def _dot_tn(a, b):
    return lax.dot_general(a, b, (((0,), (0,)), ((), ())), preferred_element_type=F32)


def _rms(v):
    return lax.rsqrt(jnp.mean(v * v, axis=-1, keepdims=True) + EPS)


def _rms_bwd(dyv, xh, r, g):
    dxh = dyv * g
    return r * (dxh - xh * jnp.mean(dxh * xh, axis=-1, keepdims=True))


def norm_proj(x, g, w, name):
    S, N = x.shape[0], w.shape[1]

    def body(x_ref, g_ref, w_ref, h_ref, p_ref):
        xf = x_ref[...]
        h = ((xf * _rms(xf)) * g_ref[...]).astype(BF16)
        h_ref[...] = h
        p_ref[...] = _dot(h, w_ref[...]).astype(BF16)

    return pl.pallas_call(
        body, name=name, grid=(S // TM,),
        in_specs=[pl.BlockSpec((TM, D_MODEL), lambda i: (i, 0)),
                  pl.BlockSpec((1, D_MODEL), lambda i: (0, 0)),
                  pl.BlockSpec((D_MODEL, N), lambda i: (0, 0))],
        out_specs=[pl.BlockSpec((TM, D_MODEL), lambda i: (i, 0)),
                   pl.BlockSpec((TM, N), lambda i: (i, 0))],
        out_shape=[jax.ShapeDtypeStruct((S, D_MODEL), BF16), jax.ShapeDtypeStruct((S, N), BF16)],
        compiler_params=_cparams(("parallel",)),
    )(x, g, w)


def _band_mask(first_block):
    qi = lax.broadcasted_iota(jnp.int32, (GRP * BLK, 2 * BLK), 0) & (BLK - 1)
    kj = lax.broadcasted_iota(jnp.int32, (GRP * BLK, 2 * BLK), 1)
    diff = qi + BLK - kj
    valid = (diff >= 0) & (diff < BLK)
    return valid & ((kj >= BLK) | jnp.logical_not(first_block))


def _sink_col(sinks_ref, h):
    return jnp.concatenate([jnp.full((BLK, 1), sinks_ref[GRP * h + g], F32) for g in range(GRP)], axis=0)


def _softmax_with_sink(s, sink):
    m = jnp.maximum(jnp.max(s, axis=-1, keepdims=True), sink)
    p = jnp.exp(s - m)
    es = jnp.exp(sink - m)
    denom = jnp.sum(p, axis=-1, keepdims=True) + es
    return p / denom, es / denom


def mixer_fwd(proj, sinks, conv_w, qg, kg, cog, aog, name):
    S = proj.shape[0]
    nb = S // BLK

    def body(sinks_ref, p_ref, kvp_ref, cw_ref, qg_ref, kg_ref, cog_ref, aog_ref,
             mix_ref, y_ref, ao_ref, ucar, ao_scr):
        n = pl.program_id(0)

        @pl.when(n == 0)
        def _():
            ucar[...] = jnp.zeros_like(ucar)

        bg = p_ref[:, 0:CONV_CH].astype(F32)
        u = p_ref[:, CONV_CH:2 * CONV_CH].astype(F32) * p_ref[:, 2 * CONV_CH:3 * CONV_CH].astype(F32)
        row = lax.broadcasted_iota(jnp.int32, (BLK, CONV_CH), 0)
        prev = ucar[...]
        u1 = jnp.where(row == 0, prev[7:8, :], pltpu.roll(u, 1, 0))
        u2 = jnp.where(row == 0, prev[6:7, :], jnp.where(row == 1, prev[7:8, :], pltpu.roll(u, 2, 0)))
        ucar[...] = u[BLK - 8:BLK, :]
        y = cw_ref[0:1, :] * u2 + cw_ref[1:2, :] * u1 + cw_ref[2:3, :] * u
        y_ref[...] = y.astype(BF16)
        co = bg * y
        mix_ref[:, 0:CONV_CH] = ((co * _rms(co)) * cog_ref[...]).astype(BF16)

        valid = _band_mask(n == 0)
        for h in range(N_KV):
            kb = jnp.concatenate([kvp_ref[:, HEAD * h:HEAD * (h + 1)],
                                  p_ref[:, O_K + HEAD * h:O_K + HEAD * (h + 1)]], axis=0).astype(F32)
            kn = ((kb * _rms(kb)) * kg_ref[...]).astype(BF16)
            vb = jnp.concatenate([kvp_ref[:, N_KV * HEAD + HEAD * h:N_KV * HEAD + HEAD * (h + 1)],
                                  p_ref[:, O_V + HEAD * h:O_V + HEAD * (h + 1)]], axis=0)
            q = jnp.concatenate([p_ref[:, O_Q + HEAD * (GRP * h + g):O_Q + HEAD * (GRP * h + g + 1)]
                                 for g in range(GRP)], axis=0).astype(F32)
            qn = ((q * _rms(q)) * qg_ref[...]).astype(BF16)
            s = jnp.where(valid, _dot_nt(qn, kn) * SCALE, NEG_INF)
            pn, _ = _softmax_with_sink(s, _sink_col(sinks_ref, h))
            o = _dot(pn.astype(BF16), vb)
            for g in range(GRP):
                hq = GRP * h + g
                ao_scr[:, HEAD * hq:HEAD * (hq + 1)] = o[BLK * g:BLK * (g + 1), :]
        ao = ao_scr[...]
        ao_ref[...] = ao.astype(BF16)
        mix_ref[:, CONV_CH:] = ((ao * _rms(ao)) * aog_ref[...]).astype(BF16)

    small = lambda shape: pl.BlockSpec(shape, lambda n: (0, 0))
    return pl.pallas_call(
        body, name=name, grid=(nb,),
        in_specs=[pl.BlockSpec(memory_space=pltpu.SMEM),
                  pl.BlockSpec((BLK, IN_COLS), lambda n: (n, 0)),
                  pl.BlockSpec((BLK, 2 * N_KV * HEAD), lambda n: (jnp.maximum(n - 1, 0), O_K // (2 * N_KV * HEAD))),
                  small((3, CONV_CH)), small((1, HEAD)), small((1, HEAD)), small((1, CONV_CH)), small((1, ATTN_W))],
        out_specs=[pl.BlockSpec((BLK, D_MODEL), lambda n: (n, 0)),
                   pl.BlockSpec((BLK, CONV_CH), lambda n: (n, 0)),
                   pl.BlockSpec((BLK, ATTN_W), lambda n: (n, 0))],
        out_shape=[jax.ShapeDtypeStruct((S, D_MODEL), BF16), jax.ShapeDtypeStruct((S, CONV_CH), BF16),
                   jax.ShapeDtypeStruct((S, ATTN_W), BF16)],
        scratch_shapes=[pltpu.VMEM((8, CONV_CH), F32), pltpu.VMEM((BLK, ATTN_W), F32)],
        compiler_params=_cparams(("arbitrary",)),
    )(sinks, proj, proj, conv_w, qg, kg, cog, aog)


def matmul_residual(a, w, res, name):
    S, K = a.shape
    N = w.shape[1]

    def body(a_ref, w_ref, r_ref, o_ref):
        o_ref[...] = r_ref[...] + _dot(a_ref[...], w_ref[...])

    return pl.pallas_call(
        body, name=name, grid=(S // TM,),
        in_specs=[pl.BlockSpec((TM, K), lambda i: (i, 0)), pl.BlockSpec((K, N), lambda i: (0, 0)),
                  pl.BlockSpec((TM, N), lambda i: (i, 0))],
        out_specs=pl.BlockSpec((TM, N), lambda i: (i, 0)),
        out_shape=jax.ShapeDtypeStruct((S, N), F32),
        compiler_params=_cparams(("parallel",)),
    )(a, w, res)


TF = 1408


def ffn_fwd(x1, g, wg, wu, wd, name):
    S = x1.shape[0]
    nj = D_FF // TF

    def body(x_ref, g_ref, wg_ref, wu_ref, wd_ref, h2_ref, a_ref, b_ref, o_ref, acc):
        j = pl.program_id(1)

        @pl.when(j == 0)
        def _():
            xf = x_ref[...]
            h2_ref[...] = ((xf * _rms(xf)) * g_ref[...]).astype(BF16)
            acc[...] = xf

        h2 = h2_ref[...]
        a = _dot(h2, wg_ref[...])
        b = _dot(h2, wu_ref[...])
        a_ref[...] = a.astype(BF16)
        b_ref[...] = b.astype(BF16)
        f = (a * (1.0 / (1.0 + jnp.exp(-a)))) * b
        acc[...] += _dot(f.astype(BF16), wd_ref[...])

        @pl.when(j == nj - 1)
        def _():
            o_ref[...] = acc[...]

    return pl.pallas_call(
        body, name=name, grid=(S // TM, nj),
        in_specs=[pl.BlockSpec((TM, D_MODEL), lambda i, j: (i, 0)), pl.BlockSpec((1, D_MODEL), lambda i, j: (0, 0)),
                  pl.BlockSpec((D_MODEL, TF), lambda i, j: (0, j)), pl.BlockSpec((D_MODEL, TF), lambda i, j: (0, j)),
                  pl.BlockSpec((TF, D_MODEL), lambda i, j: (j, 0))],
        out_specs=[pl.BlockSpec((TM, D_MODEL), lambda i, j: (i, 0)), pl.BlockSpec((TM, TF), lambda i, j: (i, j)),
                   pl.BlockSpec((TM, TF), lambda i, j: (i, j)), pl.BlockSpec((TM, D_MODEL), lambda i, j: (i, 0))],
        out_shape=[jax.ShapeDtypeStruct((S, D_MODEL), BF16), jax.ShapeDtypeStruct((S, D_FF), BF16),
                   jax.ShapeDtypeStruct((S, D_FF), BF16), jax.ShapeDtypeStruct((S, D_MODEL), F32)],
        scratch_shapes=[pltpu.VMEM((TM, D_MODEL), F32)],
        compiler_params=_cparams(("parallel", "arbitrary")),
    )(x1, g, wg, wu, wd)


def loss_head(yv, tgt, name):
    S = yv.shape[0]
    ni = S // TM

    def body(y_ref, t_ref, l_ref, d_ref, acc):
        i = pl.program_id(0)

        @pl.when(i == 0)
        def _():
            acc[...] = jnp.zeros_like(acc)

        e = y_ref[...] - t_ref[...]
        d_ref[...] = e * (1.0 / D_MODEL)
        acc[...] += jnp.sum(e * e, axis=0, keepdims=True)

        @pl.when(i == ni - 1)
        def _():
            l_ref[...] = jnp.full((1, 128), jnp.sum(acc[...]) * (0.5 / D_MODEL), F32)

    return pl.pallas_call(
        body, name=name, grid=(ni,),
        in_specs=[pl.BlockSpec((TM, D_MODEL), lambda i: (i, 0)), pl.BlockSpec((TM, D_MODEL), lambda i: (i, 0))],
        out_specs=[pl.BlockSpec((1, 128), lambda i: (0, 0)), pl.BlockSpec((TM, D_MODEL), lambda i: (i, 0))],
        out_shape=[jax.ShapeDtypeStruct((1, 128), F32), jax.ShapeDtypeStruct((S, D_MODEL), F32)],
        scratch_shapes=[pltpu.VMEM((1, D_MODEL), F32)],
        compiler_params=_cparams(("arbitrary",)),
    )(yv, tgt)


def ffn_bwd(dx2, a, b, x1, g, wg, wu, wd, name):
    S = dx2.shape[0]
    nj = D_FF // TF

    def body(dx_ref, a_ref, b_ref, x_ref, g_ref, wg_ref, wu_ref, wd_ref, da_ref, db_ref, dx1_ref, dg_ref, dxb, acc):
        i, j = pl.program_id(0), pl.program_id(1)

        @pl.when((i == 0) & (j == 0))
        def _():
            dg_ref[...] = jnp.zeros_like(dg_ref)

        @pl.when(j == 0)
        def _():
            dxb[...] = dx_ref[...].astype(BF16)
            acc[...] = jnp.zeros_like(acc)

        df = _dot_nt(dxb[...], wd_ref[...])
        av = a_ref[...].astype(F32)
        bv = b_ref[...].astype(F32)
        sg = 1.0 / (1.0 + jnp.exp(-av))
        da = ((df * bv) * (sg * (1.0 + av * (1.0 - sg)))).astype(BF16)
        db = (df * (av * sg)).astype(BF16)
        da_ref[...] = da
        db_ref[...] = db
        acc[...] += _dot_nt(da, wg_ref[...]) + _dot_nt(db, wu_ref[...])

        @pl.when(j == nj - 1)
        def _():
            xf = x_ref[...]
            r = _rms(xf)
            xh = xf * r
            dh = acc[...]
            dg_ref[...] += jnp.sum(dh * xh, axis=0, keepdims=True)
            dx1_ref[...] = dx_ref[...] + _rms_bwd(dh, xh, r, g_ref[...])

    return pl.pallas_call(
        body, name=name, grid=(S // TM, nj),
        in_specs=[pl.BlockSpec((TM, D_MODEL), lambda i, j: (i, 0)),
                  pl.BlockSpec((TM, TF), lambda i, j: (i, j)), pl.BlockSpec((TM, TF), lambda i, j: (i, j)),
                  pl.BlockSpec((TM, D_MODEL), lambda i, j: (i, 0)), pl.BlockSpec((1, D_MODEL), lambda i, j: (0, 0)),
                  pl.BlockSpec((D_MODEL, TF), lambda i, j: (0, j)), pl.BlockSpec((D_MODEL, TF), lambda i, j: (0, j)),
                  pl.BlockSpec((TF, D_MODEL), lambda i, j: (j, 0))],
        out_specs=[pl.BlockSpec((TM, TF), lambda i, j: (i, j)), pl.BlockSpec((TM, TF), lambda i, j: (i, j)),
                   pl.BlockSpec((TM, D_MODEL), lambda i, j: (i, 0)), pl.BlockSpec((1, D_MODEL), lambda i, j: (0, 0))],
        out_shape=[jax.ShapeDtypeStruct((S, D_FF), BF16), jax.ShapeDtypeStruct((S, D_FF), BF16),
                   jax.ShapeDtypeStruct((S, D_MODEL), F32), jax.ShapeDtypeStruct((1, D_MODEL), F32)],
        scratch_shapes=[pltpu.VMEM((TM, D_MODEL), BF16), pltpu.VMEM((TM, D_MODEL), F32)],
        compiler_params=_cparams(("arbitrary", "arbitrary")),
    )(dx2, a, b, x1, g, wg, wu, wd)


def grad_weight(lhs, rhs, tm, tn, name, swiglu=False):
    S = rhs[0].shape[0]
    M = lhs[0].shape[1]
    N = rhs[0].shape[1]
    tk = 512
    nk = S // tk
    n_l, n_r = len(lhs), len(rhs)

    def body(*refs):
        l_refs, r_refs = refs[:n_l], refs[n_l:n_l + n_r]
        o_refs, accs = refs[n_l + n_r:n_l + 2 * n_r], refs[n_l + 2 * n_r:]
        k = pl.program_id(2)
        if swiglu:
            av = l_refs[0][...].astype(F32)
            lv = ((av * (1.0 / (1.0 + jnp.exp(-av)))) * l_refs[1][...].astype(F32)).astype(BF16)
        else:
            lv = l_refs[0][...]
        for r_ref, o_ref, acc in zip(r_refs, o_refs, accs):
            part = _dot_tn(lv, r_ref[...].astype(BF16))

            @pl.when(k == 0)
            def _():
                acc[...] = part

            @pl.when(k > 0)
            def _():
                acc[...] += part

            @pl.when(k == nk - 1)
            def _():
                o_ref[...] = acc[...]

    return pl.pallas_call(
        body, name=name, grid=(M // tm, N // tn, nk),
        in_specs=[pl.BlockSpec((tk, tm), lambda i, j, k: (k, i)) for _ in lhs]
        + [pl.BlockSpec((tk, tn), lambda i, j, k: (k, j)) for _ in rhs],
        out_specs=[pl.BlockSpec((tm, tn), lambda i, j, k: (i, j)) for _ in rhs],
        out_shape=[jax.ShapeDtypeStruct((M, N), F32) for _ in rhs],
        scratch_shapes=[pltpu.VMEM((tm, tn), F32) for _ in rhs],
        compiler_params=_cparams(("parallel", "parallel", "arbitrary")),
    )(*lhs, *rhs)


def matmul_nt(a, w, name):
    S, K = a.shape
    N = w.shape[0]

    def body(a_ref, w_ref, o_ref):
        o_ref[...] = _dot_nt(a_ref[...].astype(BF16), w_ref[...]).astype(BF16)

    return pl.pallas_call(
        body, name=name, grid=(S // TM,),
        in_specs=[pl.BlockSpec((TM, K), lambda i: (i, 0)), pl.BlockSpec((N, K), lambda i: (0, 0))],
        out_specs=pl.BlockSpec((TM, N), lambda i: (i, 0)),
        out_shape=jax.ShapeDtypeStruct((S, N), BF16),
        compiler_params=_cparams(("parallel",)),
    )(a, w)


def proj_bwd(dproj, w, x, g, dres, name):
    S, N = dproj.shape

    def body(dp_ref, w_ref, x_ref, g_ref, dr_ref, dx_ref, dg_ref):
        i = pl.program_id(0)

        @pl.when(i == 0)
        def _():
            dg_ref[...] = jnp.zeros_like(dg_ref)

        dh = _dot_nt(dp_ref[...], w_ref[...])
        xf = x_ref[...]
        r = _rms(xf)
        xh = xf * r
        dg_ref[...] += jnp.sum(dh * xh, axis=0, keepdims=True)
        dx_ref[...] = dr_ref[...] + _rms_bwd(dh, xh, r, g_ref[...])

    return pl.pallas_call(
        body, name=name, grid=(S // TM,),
        in_specs=[pl.BlockSpec((TM, N), lambda i: (i, 0)), pl.BlockSpec((D_MODEL, N), lambda i: (0, 0)),
                  pl.BlockSpec((TM, D_MODEL), lambda i: (i, 0)), pl.BlockSpec((1, D_MODEL), lambda i: (0, 0)),
                  pl.BlockSpec((TM, D_MODEL), lambda i: (i, 0))],
        out_specs=[pl.BlockSpec((TM, D_MODEL), lambda i: (i, 0)), pl.BlockSpec((1, D_MODEL), lambda i: (0, 0))],
        out_shape=[jax.ShapeDtypeStruct((S, D_MODEL), F32), jax.ShapeDtypeStruct((1, D_MODEL), F32)],
        compiler_params=_cparams(("arbitrary",)),
    )(dproj, w, x, g, dres)


def mixer_bwd(proj, y, ao, dmix, sinks, conv_w, qg, kg, cog, aog, name):
    S = proj.shape[0]
    nb = S // BLK
    KV_W = 2 * N_KV * HEAD

    def body(sinks_ref, p_ref, kvp_ref, y_ref, ao_ref, dm_ref, cw_ref, qg_ref, kg_ref, cog_ref, aog_ref,
             dp_ref, dcw_ref, dqg_ref, dkg_ref, dsk_ref, dcog_ref, daog_ref,
             dycar, kvcar, dao_scr, kvnow):
        step = pl.program_id(0)
        n = nb - 1 - step

        @pl.when(step == 0)
        def _():
            dycar[...] = jnp.zeros_like(dycar)
            kvcar[...] = jnp.zeros_like(kvcar)
            dcw_ref[...] = jnp.zeros_like(dcw_ref)
            dqg_ref[...] = jnp.zeros_like(dqg_ref)
            dkg_ref[...] = jnp.zeros_like(dkg_ref)
            dsk_ref[...] = jnp.zeros_like(dsk_ref)
            dcog_ref[...] = jnp.zeros_like(dcog_ref)
            daog_ref[...] = jnp.zeros_like(daog_ref)

        bg = p_ref[:, 0:CONV_CH].astype(F32)
        cg = p_ref[:, CONV_CH:2 * CONV_CH].astype(F32)
        hc = p_ref[:, 2 * CONV_CH:3 * CONV_CH].astype(F32)
        yv = y_ref[...].astype(F32)
        dmc = dm_ref[:, 0:CONV_CH].astype(F32)
        co = bg * yv
        rc = _rms(co)
        ch = co * rc
        dcog_ref[...] += jnp.sum(dmc * ch, axis=0, keepdims=True)
        dco = _rms_bwd(dmc, ch, rc, cog_ref[...])
        dp_ref[:, 0:CONV_CH] = (dco * yv).astype(BF16)
        dy = dco * bg
        row = lax.broadcasted_iota(jnp.int32, (BLK, CONV_CH), 0)
        nxt = dycar[...]
        dy1 = jnp.where(row == BLK - 1, nxt[0:1, :], pltpu.roll(dy, BLK - 1, 0))
        dy2 = jnp.where(row == BLK - 2, nxt[0:1, :], jnp.where(row == BLK - 1, nxt[1:2, :], pltpu.roll(dy, BLK - 2, 0)))
        dycar[...] = dy[0:8, :]
        du = cw_ref[2:3, :] * dy + cw_ref[1:2, :] * dy1 + cw_ref[0:1, :] * dy2
        dp_ref[:, CONV_CH:2 * CONV_CH] = (du * hc).astype(BF16)
        dp_ref[:, 2 * CONV_CH:3 * CONV_CH] = (du * cg).astype(BF16)
        u = cg * hc
        dcw_ref[0:1, :] += jnp.sum(dy2 * u, axis=0, keepdims=True)
        dcw_ref[1:2, :] += jnp.sum(dy1 * u, axis=0, keepdims=True)
        dcw_ref[2:3, :] += jnp.sum(dy * u, axis=0, keepdims=True)

        aov = ao_ref[...].astype(F32)
        dma = dm_ref[:, CONV_CH:].astype(F32)
        ra = _rms(aov)
        ah = aov * ra
        daog_ref[...] += jnp.sum(dma * ah, axis=0, keepdims=True)
        dao_scr[...] = _rms_bwd(dma, ah, ra, aog_ref[...])

        valid = _band_mask(n == 0)
        lane8 = lax.broadcasted_iota(jnp.int32, (1, N_Q), 1)
        dsk = jnp.zeros((1, N_Q), F32)
        for h in range(N_KV):
            kb = jnp.concatenate([kvp_ref[:, HEAD * h:HEAD * (h + 1)],
                                  p_ref[:, O_K + HEAD * h:O_K + HEAD * (h + 1)]], axis=0).astype(F32)
            rk = _rms(kb)
            kh = kb * rk
            kn = (kh * kg_ref[...]).astype(BF16)
            vb = jnp.concatenate([kvp_ref[:, N_KV * HEAD + HEAD * h:N_KV * HEAD + HEAD * (h + 1)],
                                  p_ref[:, O_V + HEAD * h:O_V + HEAD * (h + 1)]], axis=0)
            q = jnp.concatenate([p_ref[:, O_Q + HEAD * (GRP * h + g):O_Q + HEAD * (GRP * h + g + 1)]
                                 for g in range(GRP)], axis=0).astype(F32)
            rq = _rms(q)
            qh = q * rq
            qn = (qh * qg_ref[...]).astype(BF16)
            s = jnp.where(valid, _dot_nt(qn, kn) * SCALE, NEG_INF)
            pn, psink = _softmax_with_sink(s, _sink_col(sinks_ref, h))
            do = jnp.concatenate([dao_scr[:, HEAD * (GRP * h + g):HEAD * (GRP * h + g + 1)]
                                  for g in range(GRP)], axis=0).astype(BF16)
            dv = _dot_tn(pn.astype(BF16), do)
            dpr = _dot_nt(do, vb)
            delta = jnp.sum(pn * dpr, axis=-1, keepdims=True)
            ds = ((pn * (dpr - delta)) * SCALE).astype(BF16)
            sk = psink * delta
            for g in range(GRP):
                dsk = dsk - jnp.where(lane8 == GRP * h + g, jnp.sum(sk[BLK * g:BLK * (g + 1), :]), 0.0)
            dqn = _dot(ds, kn)
            dkn = _dot_tn(ds, qn)
            dqg_ref[...] += jnp.sum(dqn * qh, axis=0, keepdims=True)
            dq = _rms_bwd(dqn, qh, rq, qg_ref[...]).astype(BF16)
            for g in range(GRP):
                hq = GRP * h + g
                dp_ref[:, O_Q + HEAD * hq:O_Q + HEAD * (hq + 1)] = dq[BLK * g:BLK * (g + 1), :]
            dkg_ref[...] += jnp.sum(dkn * kh, axis=0, keepdims=True)
            dk = _rms_bwd(dkn, kh, rk, kg_ref[...])
            kc, vc = slice(HEAD * h, HEAD * (h + 1)), slice(N_KV * HEAD + HEAD * h, N_KV * HEAD + HEAD * (h + 1))
            kvnow[:, kc] = kvcar[:, kc] + dk[BLK:, :]
            kvnow[:, vc] = kvcar[:, vc] + dv[BLK:, :]
            kvcar[:, kc] = dk[:BLK, :]
            kvcar[:, vc] = dv[:BLK, :]
        dp_ref[:, O_K:] = kvnow[...].astype(BF16)
        dsk_ref[...] += dsk

    small = lambda shape: pl.BlockSpec(shape, lambda s: (0, 0))
    blk = lambda w: pl.BlockSpec((BLK, w), lambda s: (nb - 1 - s, 0))
    return pl.pallas_call(
        body, name=name, grid=(nb,),
        in_specs=[pl.BlockSpec(memory_space=pltpu.SMEM),
                  blk(IN_COLS),
                  pl.BlockSpec((BLK, KV_W), lambda s: (jnp.maximum(nb - 2 - s, 0), O_K // KV_W)),
                  blk(CONV_CH), blk(ATTN_W), blk(D_MODEL),
                  small((3, CONV_CH)), small((1, HEAD)), small((1, HEAD)), small((1, CONV_CH)), small((1, ATTN_W))],
        out_specs=[blk(IN_COLS), small((3, CONV_CH)), small((1, HEAD)), small((1, HEAD)), small((1, N_Q)),
                   small((1, CONV_CH)), small((1, ATTN_W))],
        out_shape=[jax.ShapeDtypeStruct((S, IN_COLS), BF16), jax.ShapeDtypeStruct((3, CONV_CH), F32),
                   jax.ShapeDtypeStruct((1, HEAD), F32), jax.ShapeDtypeStruct((1, HEAD), F32),
                   jax.ShapeDtypeStruct((1, N_Q), F32), jax.ShapeDtypeStruct((1, CONV_CH), F32),
                   jax.ShapeDtypeStruct((1, ATTN_W), F32)],
        scratch_shapes=[pltpu.VMEM((8, CONV_CH), F32), pltpu.VMEM((BLK, KV_W), F32),
                        pltpu.VMEM((BLK, ATTN_W), F32), pltpu.VMEM((BLK, KV_W), F32)],
        compiler_params=_cparams(("arbitrary",)),
    )(sinks, proj, proj, y, ao, dmix, conv_w, qg, kg, cog, aog)


def local_step(x, tgt, big, small):
    saved = []
    for l in range(DEPTH):
        w_in, w_o, w_g, w_u, w_d = big[l]
        sp = small[l]
        h, proj = norm_proj(x, sp["norm1_g"], w_in, f"norm_proj{l}")
        mix, y, ao = mixer_fwd(proj, sp["sinks"], sp["conv_w"], sp["q_norm_g"], sp["k_norm_g"],
                               sp["conv_out_g"], sp["attn_out_g"], f"mixer_fwd{l}")
        x1 = matmul_residual(mix, w_o, x, f"out_proj{l}")
        h2, a, b, x2 = ffn_fwd(x1, sp["norm2_g"], w_g, w_u, w_d, f"ffn_fwd{l}")
        saved.append((x, h, proj, mix, y, ao, x1, h2, a, b))
        x = x2
    loss_row, dx = loss_head(x, tgt, "loss_head")

    gbig, gsmall = [None] * DEPTH, [None] * DEPTH
    for l in reversed(range(DEPTH)):
        w_in, w_o, w_g, w_u, w_d = big[l]
        sp = small[l]
        x0, h, proj, mix, y, ao, x1, h2, a, b = saved[l]
        (d_wd,) = grad_weight((a, b), [dx], TF, D_MODEL, f"grad_w_down{l}", swiglu=True)
        da, db, dx1, d_g2 = ffn_bwd(dx, a, b, x1, sp["norm2_g"], w_g, w_u, w_d, f"ffn_bwd{l}")
        d_wg, d_wu = grad_weight((h2,), [da, db], D_MODEL, TF, f"grad_w_gate_up{l}")
        (d_wo,) = grad_weight((mix,), [dx1], D_MODEL, D_MODEL, f"grad_w_o{l}")
        dmix = matmul_nt(dx1, w_o, f"out_proj_bwd{l}")
        dproj, d_cw, d_qg, d_kg, d_sk, d_cog, d_aog = mixer_bwd(
            proj, y, ao, dmix, sp["sinks"], sp["conv_w"], sp["q_norm_g"], sp["k_norm_g"],
            sp["conv_out_g"], sp["attn_out_g"], f"mixer_bwd{l}")
        (d_win,) = grad_weight((h,), [dproj], D_MODEL, 768, f"grad_w_in{l}")
        dx, d_g1 = proj_bwd(dproj, w_in, x0, sp["norm1_g"], dx1, f"proj_bwd{l}")
        gbig[l] = (d_win, d_wo, d_wg, d_wu, d_wd)
        gsmall[l] = dict(norm1_g=d_g1, conv_w=d_cw, q_norm_g=d_qg, k_norm_g=d_kg, sinks=d_sk,
                         conv_out_g=d_cog, attn_out_g=d_aog, norm2_g=d_g2)
    return loss_row, dx, gbig, gsmall


ANY = pl.BlockSpec(memory_space=pl.ANY)
OTHER_CHIPS = ((1, 0), (0, 1), (1, 1))


def _position():
    return lax.axis_index("x"), lax.axis_index("y"), lax.axis_index("c")


def all_gather(shard, name):
    R, W = shard.shape

    def body(x_ref, out_ref, send_sems, recv_sems, local_sem):
        x, y, c = _position()
        me, sibling = (x, y, c), (x, y, 1 - c)
        chips = [(x ^ mx, y ^ my) for mx, my in OTHER_CHIPS]

        def slab(px, py, pc):
            return out_ref.at[4 * px + 2 * py + pc]

        def copy(k, block, to, src=None):
            return pltpu.make_async_remote_copy(
                src_ref=slab(*block) if src is None else src, dst_ref=slab(*block),
                send_sem=send_sems.at[k], recv_sem=recv_sems.at[k], device_id=to, device_id_type=MESH_T)

        mine = pltpu.make_async_copy(x_ref, slab(*me), local_sem)
        mine.start()
        first = [copy(0, me, sibling, src=x_ref)]
        first += [copy(1 + j, me, (*chip, c), src=x_ref) for j, chip in enumerate(chips)]
        for cp in first:
            cp.start()
        passed = [copy(4 + j, (*chip, c), sibling) for j, chip in enumerate(chips)]
        for j, chip in enumerate(chips):
            copy(1 + j, (*chip, c), me).wait_recv()
            passed[j].start()
        copy(0, sibling, me).wait_recv()
        for j, chip in enumerate(chips):
            copy(4 + j, (*chip, 1 - c), me).wait_recv()
        for cp in first + passed:
            cp.wait_send()
        mine.wait()

    return pl.pallas_call(
        body, name=name, in_specs=[ANY], out_specs=ANY,
        out_shape=jax.ShapeDtypeStruct((N_DEV, R, W), shard.dtype),
        scratch_shapes=[pltpu.SemaphoreType.DMA((7,)), pltpu.SemaphoreType.DMA((7,)), pltpu.SemaphoreType.DMA(())],
    )(shard)


def exchange_siblings(g8, name):
    _, R, W = g8.shape

    def body(g_ref, out_ref, send_sems, recv_sems):
        x, y, c = _position()
        copies = [pltpu.make_async_remote_copy(
            src_ref=g_ref.at[2 * k + (1 - c)], dst_ref=out_ref.at[k], send_sem=send_sems.at[k],
            recv_sem=recv_sems.at[k], device_id=(x, y, 1 - c), device_id_type=MESH_T) for k in range(4)]
        for cp in copies:
            cp.start()
        for cp in copies:
            cp.wait()

    return pl.pallas_call(
        body, name=name, in_specs=[ANY], out_specs=ANY,
        out_shape=jax.ShapeDtypeStruct((4, R, W), g8.dtype),
        scratch_shapes=[pltpu.SemaphoreType.DMA((4,)), pltpu.SemaphoreType.DMA((4,))],
    )(g8)


def exchange_chips(pair, name):
    _, R, W = pair.shape

    def body(p_ref, out_ref, send_sems, recv_sems):
        x, y, c = _position()
        copies = [pltpu.make_async_remote_copy(
            src_ref=p_ref.at[2 * (x ^ mx) + (y ^ my)], dst_ref=out_ref.at[j], send_sem=send_sems.at[j],
            recv_sem=recv_sems.at[j], device_id=(x ^ mx, y ^ my, c), device_id_type=MESH_T)
            for j, (mx, my) in enumerate(OTHER_CHIPS)]
        for cp in copies:
            cp.start()
        for cp in copies:
            cp.wait()

    return pl.pallas_call(
        body, name=name, in_specs=[ANY], out_specs=ANY,
        out_shape=jax.ShapeDtypeStruct((3, R, W), pair.dtype),
        scratch_shapes=[pltpu.SemaphoreType.DMA((3,)), pltpu.SemaphoreType.DMA((3,))],
    )(pair)


ROW_TILE = 368


def pair_add(g8, recv, c_arr, name):
    _, R, W = g8.shape

    def body(c_ref, g_ref, r_ref, o_ref):
        o_ref[...] = g_ref[...] + r_ref[...]

    return pl.pallas_call(
        body, name=name,
        grid_spec=pltpu.PrefetchScalarGridSpec(
            num_scalar_prefetch=1, grid=(4, R // ROW_TILE),
            in_specs=[pl.BlockSpec((1, ROW_TILE, W), lambda k, r, c: (2 * k + c[0], r, 0)),
                      pl.BlockSpec((1, ROW_TILE, W), lambda k, r, c: (k, r, 0))],
            out_specs=pl.BlockSpec((1, ROW_TILE, W), lambda k, r, c: (k, r, 0))),
        out_shape=jax.ShapeDtypeStruct((4, R, W), F32),
        compiler_params=_cparams(("parallel", "parallel")),
    )(c_arr, g8, recv)


def chip_sum(pair, recv, chip_arr, name):
    _, R, W = pair.shape

    def body(k_ref, p_ref, r_ref, o_ref):
        o_ref[...] = ((p_ref[0] + r_ref[0]) + r_ref[1]) + r_ref[2]

    return pl.pallas_call(
        body, name=name,
        grid_spec=pltpu.PrefetchScalarGridSpec(
            num_scalar_prefetch=1, grid=(R // ROW_TILE,),
            in_specs=[pl.BlockSpec((1, ROW_TILE, W), lambda r, k: (k[0], r, 0)),
                      pl.BlockSpec((3, ROW_TILE, W), lambda r, k: (0, r, 0))],
            out_specs=pl.BlockSpec((ROW_TILE, W), lambda r, k: (r, 0))),
        out_shape=jax.ShapeDtypeStruct((R, W), F32),
        compiler_params=_cparams(("parallel",)),
    )(chip_arr, pair, recv)


def all_reduce_small(v, name):
    R, W = v.shape

    def body(v_ref, o_ref, recv, send_sems, recv_sems):
        x, y, c = _position()
        me = 4 * x + 2 * y + c
        copies = []
        for r in range(1, N_DEV):
            to = (x ^ ((r >> 2) & 1), y ^ ((r >> 1) & 1), c ^ (r & 1))
            copies.append(pltpu.make_async_remote_copy(
                src_ref=v_ref, dst_ref=recv.at[me], send_sem=send_sems.at[r - 1], recv_sem=recv_sems.at[r - 1],
                device_id=to, device_id_type=MESH_T))
        for cp in copies:
            cp.start()
        recv[pl.ds(me, 1)] = v_ref[...][None]
        for cp in copies:
            cp.wait()
        acc = recv[0]
        for s in range(1, N_DEV):
            acc = acc + recv[s]
        o_ref[...] = acc

    return pl.pallas_call(
        body, name=name,
        in_specs=[pl.BlockSpec(memory_space=pltpu.VMEM)], out_specs=pl.BlockSpec(memory_space=pltpu.VMEM),
        out_shape=jax.ShapeDtypeStruct((R, W), F32),
        scratch_shapes=[pltpu.VMEM((N_DEV, R, W), F32), pltpu.SemaphoreType.DMA((N_DEV - 1,)),
                        pltpu.SemaphoreType.DMA((N_DEV - 1,))],
    )(v)


def _row_tile(rows):
    if rows <= 512:
        return rows
    return max(t for t in range(8, 513, 8) if rows % t == 0)


def adamw(w, g, m, v, name):
    R, W = w.shape
    tr = _row_tile(R)

    def body(w_ref, g_ref, m_ref, v_ref, d_ref, mo_ref, vo_ref):
        gv = g_ref[...]
        mn = ADAM_B1 * m_ref[...] + (1.0 - ADAM_B1) * gv
        vn = ADAM_B2 * v_ref[...] + (1.0 - ADAM_B2) * (gv * gv)
        m_hat = mn / (1.0 - ADAM_B1 ** ADAM_STEP)
        v_hat = vn / (1.0 - ADAM_B2 ** ADAM_STEP)
        d_ref[...] = -ADAM_LR * (m_hat / (jnp.sqrt(v_hat) + ADAM_EPS) + ADAM_WD * w_ref[...])
        mo_ref[...] = mn
        vo_ref[...] = vn

    spec = pl.BlockSpec((tr, W), lambda i: (i, 0))
    return pl.pallas_call(
        body, name=name, grid=(R // tr,), in_specs=[spec] * 4, out_specs=[spec] * 3,
        out_shape=[jax.ShapeDtypeStruct((R, W), F32)] * 3,
        compiler_params=_cparams(("parallel",)),
    )(w, g, m, v)


def _pack_shards(w_in, w_o, w_gate, w_up, w_down):
    return jnp.concatenate([w_in.reshape(ROWS_IN, D_MODEL), w_o, w_gate.reshape(ROWS_FF, D_MODEL),
                            w_up.reshape(ROWS_FF, D_MODEL), w_down], axis=0)


def _unpack_shards(p):
    o1, o2, o3, o4 = ROWS_IN, ROWS_IN + ROWS_O, ROWS_IN + ROWS_O + ROWS_FF, ROWS_IN + ROWS_O + 2 * ROWS_FF
    return (p[:o1].reshape(D_MODEL, ROWS_IN), p[o1:o2], p[o2:o3].reshape(D_MODEL, ROWS_FF),
            p[o3:o4].reshape(D_MODEL, ROWS_FF), p[o4:])


def _columns_from_shards(g, cols):
    return g.reshape(N_DEV, D_MODEL, cols).transpose(1, 0, 2).reshape(D_MODEL, N_DEV * cols)


def _shards_from_columns(w, cols):
    return w.reshape(D_MODEL, N_DEV, cols).transpose(1, 0, 2).reshape(N_DEV, cols, D_MODEL)


def _unpack_gathered(g):
    o1, o2, o3, o4 = ROWS_IN, ROWS_IN + ROWS_O, ROWS_IN + ROWS_O + ROWS_FF, ROWS_IN + ROWS_O + 2 * ROWS_FF
    return (_columns_from_shards(g[:, :o1], ROWS_IN), g[:, o1:o2].reshape(D_MODEL, D_MODEL),
            _columns_from_shards(g[:, o2:o3], ROWS_FF), _columns_from_shards(g[:, o3:o4], ROWS_FF),
            g[:, o4:].reshape(D_FF, D_MODEL))


def _pack_full(d_win, d_wo, d_wg, d_wu, d_wd):
    return jnp.concatenate([_shards_from_columns(d_win, ROWS_IN), d_wo.reshape(N_DEV, ROWS_O, D_MODEL),
                            _shards_from_columns(d_wg, ROWS_FF), _shards_from_columns(d_wu, ROWS_FF),
                            d_wd.reshape(N_DEV, ROWS_FF, D_MODEL)], axis=1)


SMALL_NAMES = ("norm1_g", "q_norm_g", "k_norm_g", "sinks", "conv_out_g", "attn_out_g", "norm2_g", "conv_w")
SMALL_SIZES = (D_MODEL, HEAD, HEAD, N_Q, CONV_CH, ATTN_W, D_MODEL, 3 * CONV_CH)
SMALL_ROWS = 80


def kernel(x, norm1_g, w_in, conv_w, q_norm_g, k_norm_g, sinks, conv_out_g, attn_out_g, w_o, norm2_g, w_gate, w_up, w_down, loss_target, m_norm1_g, m_w_in, m_conv_w, m_q_norm_g, m_k_norm_g, m_sinks, m_conv_out_g, m_attn_out_g, m_w_o, m_norm2_g, m_w_gate, m_w_up, m_w_down, v_norm1_g, v_w_in, v_conv_w, v_q_norm_g, v_k_norm_g, v_sinks, v_conv_out_g, v_attn_out_g, v_w_o, v_norm2_g, v_w_gate, v_w_up, v_w_down):
    xi, yi, ci = _position()
    me = 4 * xi + 2 * yi + ci
    c_arr = jnp.reshape(ci, (1,)).astype(jnp.int32)
    chip_arr = jnp.reshape(2 * xi + yi, (1,)).astype(jnp.int32)

    packed = jnp.concatenate([_pack_shards(w_in[l], w_o[l], w_gate[l], w_up[l], w_down[l]) for l in range(DEPTH)],
                             axis=0).astype(BF16)
    gathered = all_gather(packed, "all_gather_weights")
    big = [_unpack_gathered(gathered[:, l * PACK_ROWS:(l + 1) * PACK_ROWS]) for l in range(DEPTH)]
    conv_place = lax.dynamic_update_slice(jnp.zeros((DEPTH, 3, CONV_CH), F32), conv_w, (0, 0, me * HEAD))
    conv_full = all_reduce_small(conv_place.reshape(DEPTH * 3 * CONV_CH // 128, 128), "gather_conv_w")
    conv_full = conv_full.reshape(DEPTH, 3, CONV_CH)
    small = [dict(norm1_g=norm1_g[l][None], conv_w=conv_full[l], q_norm_g=q_norm_g[l][None],
                  k_norm_g=k_norm_g[l][None], sinks=sinks[l], conv_out_g=conv_out_g[l][None],
                  attn_out_g=attn_out_g[l][None], norm2_g=norm2_g[l][None]) for l in range(DEPTH)]

    loss_row, dx, gbig, gsmall = local_step(x[0], loss_target[0], big, small)
    loss = lax.psum(loss_row[0, 0], ("x", "y", "c"))
    grad_x = dx[None]

    flat = jnp.concatenate([gsmall[l][n].reshape(-1) for l in range(DEPTH) for n in SMALL_NAMES])
    flat = jnp.pad(flat, (0, SMALL_ROWS * 128 - flat.shape[0])).reshape(SMALL_ROWS, 128)
    flat = all_reduce_small(flat, "all_reduce_small_grads").reshape(-1)
    gs = {n: [] for n in SMALL_NAMES}
    off = 0
    for l in range(DEPTH):
        for n, size in zip(SMALL_NAMES, SMALL_SIZES):
            gs[n].append(flat[off:off + size])
            off += size
    gs = {n: jnp.stack(v) for n, v in gs.items()}
    g_conv = lax.dynamic_slice(gs["conv_w"].reshape(DEPTH, 3, CONV_CH), (0, 0, me * HEAD), (DEPTH, 3, HEAD))

    reduced = []
    for l in reversed(range(DEPTH)):
        g8 = _pack_full(*gbig[l])
        from_sibling = exchange_siblings(g8, f"exchange_siblings{l}")
        pair = pair_add(g8, from_sibling, c_arr, f"pair_add{l}")
        from_chips = exchange_chips(pair, f"exchange_chips{l}")
        reduced.append(_unpack_shards(chip_sum(pair, from_chips, chip_arr, f"chip_sum{l}")))
    reduced = reduced[::-1]
    g_win, g_wo, g_wg, g_wu, g_wd = (jnp.stack([reduced[l][i] for l in range(DEPTH)]) for i in range(5))

    grads = dict(norm1_g=gs["norm1_g"], w_in=g_win, conv_w=g_conv, q_norm_g=gs["q_norm_g"], k_norm_g=gs["k_norm_g"],
                 sinks=gs["sinks"], conv_out_g=gs["conv_out_g"], attn_out_g=gs["attn_out_g"], w_o=g_wo,
                 norm2_g=gs["norm2_g"], w_gate=g_wg, w_up=g_wu, w_down=g_wd)
    params = dict(norm1_g=(norm1_g, m_norm1_g, v_norm1_g), w_in=(w_in, m_w_in, v_w_in),
                  conv_w=(conv_w, m_conv_w, v_conv_w), q_norm_g=(q_norm_g, m_q_norm_g, v_q_norm_g),
                  k_norm_g=(k_norm_g, m_k_norm_g, v_k_norm_g), sinks=(sinks, m_sinks, v_sinks),
                  conv_out_g=(conv_out_g, m_conv_out_g, v_conv_out_g),
                  attn_out_g=(attn_out_g, m_attn_out_g, v_attn_out_g), w_o=(w_o, m_w_o, v_w_o),
                  norm2_g=(norm2_g, m_norm2_g, v_norm2_g), w_gate=(w_gate, m_w_gate, v_w_gate),
                  w_up=(w_up, m_w_up, v_w_up), w_down=(w_down, m_w_down, v_w_down))
    names = ("norm1_g", "w_in", "conv_w", "q_norm_g", "k_norm_g", "sinks", "conv_out_g", "attn_out_g", "w_o",
             "norm2_g", "w_gate", "w_up", "w_down")

    deltas, new_m, new_v = [], [], []
    for n in names:
        w, m, v = params[n]
        two_d = (-1, w.shape[-1])
        d, mn, vn = adamw(w.reshape(two_d), grads[n].reshape(two_d), m.reshape(two_d), v.reshape(two_d), f"adamw_{n}")
        deltas.append(d.reshape(w.shape))
        new_m.append(mn.reshape(w.shape))
        new_v.append(vn.reshape(w.shape))
    return (loss, grad_x, *[grads[n].reshape(params[n][0].shape) for n in names], *deltas, *new_m, *new_v)
```

```python
import functools

import jax
import jax.numpy as jnp
from jax import lax
from jax.experimental import pallas as pl
from jax.experimental.pallas import tpu as pltpu

F32 = jnp.float32
BF16 = jnp.bfloat16

D_MODEL = 1024
CONV_CH = 512
ATTN_W = 512
N_Q = 8
N_KV = 2
GRP = N_Q // N_KV
HEAD = 64
IN_COLS = 2304
D_FF = 2816
BLK = 128
O_Q = 3 * CONV_CH
O_K = O_Q + ATTN_W
O_V = O_K + N_KV * HEAD
EPS = 1e-6
NEG_INF = -1e30
SCALE = HEAD ** -0.5
N_DEV = 8
DEPTH = 2

ADAM_LR = 0.001
ADAM_B1 = 0.9
ADAM_B2 = 0.999
ADAM_EPS = 1e-08
ADAM_WD = 0.01
ADAM_STEP = 10

VMEM_LIMIT = 56 * 1024 * 1024
TM = 512
MESH_T = pl.DeviceIdType.MESH

ROWS_IN, ROWS_O, ROWS_FF = 288, 128, 352
PACK_ROWS = ROWS_IN + ROWS_O + 3 * ROWS_FF


def _cparams(sem):
    return pltpu.CompilerParams(dimension_semantics=sem, vmem_limit_bytes=VMEM_LIMIT)


def _dot(a, b):
    return jnp.dot(a, b, preferred_element_type=F32)


def _dot_nt(a, b):
    return lax.dot_general(a, b, (((1,), (1,)), ((), ())), preferred_element_type=F32)


def _dot_tn(a, b):
    return lax.dot_general(a, b, (((0,), (0,)), ((), ())), preferred_element_type=F32)


def _rms(v):
    return lax.rsqrt(jnp.mean(v * v, axis=-1, keepdims=True) + EPS)


def _rms_bwd(dyv, xh, r, g):
    dxh = dyv * g
    return r * (dxh - xh * jnp.mean(dxh * xh, axis=-1, keepdims=True))


def norm_proj(x, g, w, name):
    S, N = x.shape[0], w.shape[1]

    def body(x_ref, g_ref, w_ref, h_ref, p_ref):
        xf = x_ref[...]
        h = ((xf * _rms(xf)) * g_ref[...]).astype(BF16)
        h_ref[...] = h
        p_ref[...] = _dot(h, w_ref[...]).astype(BF16)

    return pl.pallas_call(
        body, name=name, grid=(S // TM,),
        in_specs=[pl.BlockSpec((TM, D_MODEL), lambda i: (i, 0)),
                  pl.BlockSpec((1, D_MODEL), lambda i: (0, 0)),
                  pl.BlockSpec((D_MODEL, N), lambda i: (0, 0))],
        out_specs=[pl.BlockSpec((TM, D_MODEL), lambda i: (i, 0)),
                   pl.BlockSpec((TM, N), lambda i: (i, 0))],
        out_shape=[jax.ShapeDtypeStruct((S, D_MODEL), BF16), jax.ShapeDtypeStruct((S, N), BF16)],
        compiler_params=_cparams(("parallel",)),
    )(x, g, w)


def _band_mask(first_block):
    qi = lax.broadcasted_iota(jnp.int32, (GRP * BLK, 2 * BLK), 0) & (BLK - 1)
    kj = lax.broadcasted_iota(jnp.int32, (GRP * BLK, 2 * BLK), 1)
    diff = qi + BLK - kj
    valid = (diff >= 0) & (diff < BLK)
    return valid & ((kj >= BLK) | jnp.logical_not(first_block))


def _sink_col(sinks_ref, h):
    return jnp.concatenate([jnp.full((BLK, 1), sinks_ref[GRP * h + g], F32) for g in range(GRP)], axis=0)


def _softmax_with_sink(s, sink):
    m = jnp.maximum(jnp.max(s, axis=-1, keepdims=True), sink)
    p = jnp.exp(s - m)
    es = jnp.exp(sink - m)
    denom = jnp.sum(p, axis=-1, keepdims=True) + es
    return p / denom, es / denom


def mixer_fwd(proj, sinks, conv_w, qg, kg, cog, aog, name):
    S = proj.shape[0]
    nb = S // BLK

    def body(sinks_ref, p_ref, kvp_ref, cw_ref, qg_ref, kg_ref, cog_ref, aog_ref,
             mix_ref, y_ref, ao_ref, ucar, ao_scr):
        n = pl.program_id(0)

        @pl.when(n == 0)
        def _():
            ucar[...] = jnp.zeros_like(ucar)

        bg = p_ref[:, 0:CONV_CH].astype(F32)
        u = p_ref[:, CONV_CH:2 * CONV_CH].astype(F32) * p_ref[:, 2 * CONV_CH:3 * CONV_CH].astype(F32)
        row = lax.broadcasted_iota(jnp.int32, (BLK, CONV_CH), 0)
        prev = ucar[...]
        u1 = jnp.where(row == 0, prev[7:8, :], pltpu.roll(u, 1, 0))
        u2 = jnp.where(row == 0, prev[6:7, :], jnp.where(row == 1, prev[7:8, :], pltpu.roll(u, 2, 0)))
        ucar[...] = u[BLK - 8:BLK, :]
        y = cw_ref[0:1, :] * u2 + cw_ref[1:2, :] * u1 + cw_ref[2:3, :] * u
        y_ref[...] = y.astype(BF16)
        co = bg * y
        mix_ref[:, 0:CONV_CH] = ((co * _rms(co)) * cog_ref[...]).astype(BF16)

        valid = _band_mask(n == 0)
        for h in range(N_KV):
            kb = jnp.concatenate([kvp_ref[:, HEAD * h:HEAD * (h + 1)],
                                  p_ref[:, O_K + HEAD * h:O_K + HEAD * (h + 1)]], axis=0).astype(F32)
            kn = ((kb * _rms(kb)) * kg_ref[...]).astype(BF16)
            vb = jnp.concatenate([kvp_ref[:, N_KV * HEAD + HEAD * h:N_KV * HEAD + HEAD * (h + 1)],
                                  p_ref[:, O_V + HEAD * h:O_V + HEAD * (h + 1)]], axis=0)
            q = jnp.concatenate([p_ref[:, O_Q + HEAD * (GRP * h + g):O_Q + HEAD * (GRP * h + g + 1)]
                                 for g in range(GRP)], axis=0).astype(F32)
            qn = ((q * _rms(q)) * qg_ref[...]).astype(BF16)
            s = jnp.where(valid, _dot_nt(qn, kn) * SCALE, NEG_INF)
            pn, _ = _softmax_with_sink(s, _sink_col(sinks_ref, h))
            o = _dot(pn.astype(BF16), vb)
            for g in range(GRP):
                hq = GRP * h + g
                ao_scr[:, HEAD * hq:HEAD * (hq + 1)] = o[BLK * g:BLK * (g + 1), :]
        ao = ao_scr[...]
        ao_ref[...] = ao.astype(BF16)
        mix_ref[:, CONV_CH:] = ((ao * _rms(ao)) * aog_ref[...]).astype(BF16)

    small = lambda shape: pl.BlockSpec(shape, lambda n: (0, 0))
    return pl.pallas_call(
        body, name=name, grid=(nb,),
        in_specs=[pl.BlockSpec(memory_space=pltpu.SMEM),
                  pl.BlockSpec((BLK, IN_COLS), lambda n: (n, 0)),
                  pl.BlockSpec((BLK, 2 * N_KV * HEAD), lambda n: (jnp.maximum(n - 1, 0), O_K // (2 * N_KV * HEAD))),
                  small((3, CONV_CH)), small((1, HEAD)), small((1, HEAD)), small((1, CONV_CH)), small((1, ATTN_W))],
        out_specs=[pl.BlockSpec((BLK, D_MODEL), lambda n: (n, 0)),
                   pl.BlockSpec((BLK, CONV_CH), lambda n: (n, 0)),
                   pl.BlockSpec((BLK, ATTN_W), lambda n: (n, 0))],
        out_shape=[jax.ShapeDtypeStruct((S, D_MODEL), BF16), jax.ShapeDtypeStruct((S, CONV_CH), BF16),
                   jax.ShapeDtypeStruct((S, ATTN_W), BF16)],
        scratch_shapes=[pltpu.VMEM((8, CONV_CH), F32), pltpu.VMEM((BLK, ATTN_W), F32)],
        compiler_params=_cparams(("arbitrary",)),
    )(sinks, proj, proj, conv_w, qg, kg, cog, aog)


def matmul_residual(a, w, res, name):
    S, K = a.shape
    N = w.shape[1]

    def body(a_ref, w_ref, r_ref, o_ref):
        o_ref[...] = r_ref[...] + _dot(a_ref[...], w_ref[...])

    return pl.pallas_call(
        body, name=name, grid=(S // TM,),
        in_specs=[pl.BlockSpec((TM, K), lambda i: (i, 0)), pl.BlockSpec((K, N), lambda i: (0, 0)),
                  pl.BlockSpec((TM, N), lambda i: (i, 0))],
        out_specs=pl.BlockSpec((TM, N), lambda i: (i, 0)),
        out_shape=jax.ShapeDtypeStruct((S, N), F32),
        compiler_params=_cparams(("parallel",)),
    )(a, w, res)


TF = 1408


def ffn_fwd(x1, g, wg, wu, wd, name):
    S = x1.shape[0]
    nj = D_FF // TF

    def body(x_ref, g_ref, wg_ref, wu_ref, wd_ref, h2_ref, a_ref, b_ref, o_ref, acc):
        j = pl.program_id(1)

        @pl.when(j == 0)
        def _():
            xf = x_ref[...]
            h2_ref[...] = ((xf * _rms(xf)) * g_ref[...]).astype(BF16)
            acc[...] = xf

        h2 = h2_ref[...]
        a = _dot(h2, wg_ref[...])
        b = _dot(h2, wu_ref[...])
        a_ref[...] = a.astype(BF16)
        b_ref[...] = b.astype(BF16)
        f = (a * (1.0 / (1.0 + jnp.exp(-a)))) * b
        acc[...] += _dot(f.astype(BF16), wd_ref[...])

        @pl.when(j == nj - 1)
        def _():
            o_ref[...] = acc[...]

    return pl.pallas_call(
        body, name=name, grid=(S // TM, nj),
        in_specs=[pl.BlockSpec((TM, D_MODEL), lambda i, j: (i, 0)), pl.BlockSpec((1, D_MODEL), lambda i, j: (0, 0)),
                  pl.BlockSpec((D_MODEL, TF), lambda i, j: (0, j)), pl.BlockSpec((D_MODEL, TF), lambda i, j: (0, j)),
                  pl.BlockSpec((TF, D_MODEL), lambda i, j: (j, 0))],
        out_specs=[pl.BlockSpec((TM, D_MODEL), lambda i, j: (i, 0)), pl.BlockSpec((TM, TF), lambda i, j: (i, j)),
                   pl.BlockSpec((TM, TF), lambda i, j: (i, j)), pl.BlockSpec((TM, D_MODEL), lambda i, j: (i, 0))],
        out_shape=[jax.ShapeDtypeStruct((S, D_MODEL), BF16), jax.ShapeDtypeStruct((S, D_FF), BF16),
                   jax.ShapeDtypeStruct((S, D_FF), BF16), jax.ShapeDtypeStruct((S, D_MODEL), F32)],
        scratch_shapes=[pltpu.VMEM((TM, D_MODEL), F32)],
        compiler_params=_cparams(("parallel", "arbitrary")),
    )(x1, g, wg, wu, wd)


def loss_head(yv, tgt, name):
    S = yv.shape[0]
    ni = S // TM

    def body(y_ref, t_ref, l_ref, d_ref, acc):
        i = pl.program_id(0)

        @pl.when(i == 0)
        def _():
            acc[...] = jnp.zeros_like(acc)

        e = y_ref[...] - t_ref[...]
        d_ref[...] = e * (1.0 / D_MODEL)
        acc[...] += jnp.sum(e * e, axis=0, keepdims=True)

        @pl.when(i == ni - 1)
        def _():
            l_ref[...] = jnp.full((1, 128), jnp.sum(acc[...]) * (0.5 / D_MODEL), F32)

    return pl.pallas_call(
        body, name=name, grid=(ni,),
        in_specs=[pl.BlockSpec((TM, D_MODEL), lambda i: (i, 0)), pl.BlockSpec((TM, D_MODEL), lambda i: (i, 0))],
        out_specs=[pl.BlockSpec((1, 128), lambda i: (0, 0)), pl.BlockSpec((TM, D_MODEL), lambda i: (i, 0))],
        out_shape=[jax.ShapeDtypeStruct((1, 128), F32), jax.ShapeDtypeStruct((S, D_MODEL), F32)],
        scratch_shapes=[pltpu.VMEM((1, D_MODEL), F32)],
        compiler_params=_cparams(("arbitrary",)),
    )(yv, tgt)


def ffn_bwd(dx2, a, b, x1, g, wg, wu, wd, name):
    S = dx2.shape[0]
    nj = D_FF // TF

    def body(dx_ref, a_ref, b_ref, x_ref, g_ref, wg_ref, wu_ref, wd_ref, da_ref, db_ref, dx1_ref, dg_ref, dxb, acc):
        i, j = pl.program_id(0), pl.program_id(1)

        @pl.when((i == 0) & (j == 0))
        def _():
            dg_ref[...] = jnp.zeros_like(dg_ref)

        @pl.when(j == 0)
        def _():
            dxb[...] = dx_ref[...].astype(BF16)
            acc[...] = jnp.zeros_like(acc)

        df = _dot_nt(dxb[...], wd_ref[...])
        av = a_ref[...].astype(F32)
        bv = b_ref[...].astype(F32)
        sg = 1.0 / (1.0 + jnp.exp(-av))
        da = ((df * bv) * (sg * (1.0 + av * (1.0 - sg)))).astype(BF16)
        db = (df * (av * sg)).astype(BF16)
        da_ref[...] = da
        db_ref[...] = db
        acc[...] += _dot_nt(da, wg_ref[...]) + _dot_nt(db, wu_ref[...])

        @pl.when(j == nj - 1)
        def _():
            xf = x_ref[...]
            r = _rms(xf)
            xh = xf * r
            dh = acc[...]
            dg_ref[...] += jnp.sum(dh * xh, axis=0, keepdims=True)
            dx1_ref[...] = dx_ref[...] + _rms_bwd(dh, xh, r, g_ref[...])

    return pl.pallas_call(
        body, name=name, grid=(S // TM, nj),
        in_specs=[pl.BlockSpec((TM, D_MODEL), lambda i, j: (i, 0)),
                  pl.BlockSpec((TM, TF), lambda i, j: (i, j)), pl.BlockSpec((TM, TF), lambda i, j: (i, j)),
                  pl.BlockSpec((TM, D_MODEL), lambda i, j: (i, 0)), pl.BlockSpec((1, D_MODEL), lambda i, j: (0, 0)),
                  pl.BlockSpec((D_MODEL, TF), lambda i, j: (0, j)), pl.BlockSpec((D_MODEL, TF), lambda i, j: (0, j)),
                  pl.BlockSpec((TF, D_MODEL), lambda i, j: (j, 0))],
        out_specs=[pl.BlockSpec((TM, TF), lambda i, j: (i, j)), pl.BlockSpec((TM, TF), lambda i, j: (i, j)),
                   pl.BlockSpec((TM, D_MODEL), lambda i, j: (i, 0)), pl.BlockSpec((1, D_MODEL), lambda i, j: (0, 0))],
        out_shape=[jax.ShapeDtypeStruct((S, D_FF), BF16), jax.ShapeDtypeStruct((S, D_FF), BF16),
                   jax.ShapeDtypeStruct((S, D_MODEL), F32), jax.ShapeDtypeStruct((1, D_MODEL), F32)],
        scratch_shapes=[pltpu.VMEM((TM, D_MODEL), BF16), pltpu.VMEM((TM, D_MODEL), F32)],
        compiler_params=_cparams(("arbitrary", "arbitrary")),
    )(dx2, a, b, x1, g, wg, wu, wd)


def grad_weight(lhs, rhs, tm, tn, name, swiglu=False, slab_cols=None):
    S = rhs[0].shape[0]
    M = lhs[0].shape[1]
    N = rhs[0].shape[1]
    tk = 512
    nk = S // tk
    n_l, n_r = len(lhs), len(rhs)
    if slab_cols is None:
        out_spec = pl.BlockSpec((tm, tn), lambda i, j, k: (i, j))
        out_shape = jax.ShapeDtypeStruct((M, N), BF16)
    else:
        out_spec = pl.BlockSpec((tn // slab_cols, tm, slab_cols), lambda i, j, k: (j, i, 0))
        out_shape = jax.ShapeDtypeStruct((N // slab_cols, M, slab_cols), BF16)

    def body(*refs):
        l_refs, r_refs = refs[:n_l], refs[n_l:n_l + n_r]
        o_refs, accs = refs[n_l + n_r:n_l + 2 * n_r], refs[n_l + 2 * n_r:]
        k = pl.program_id(2)
        if swiglu:
            av = l_refs[0][...].astype(F32)
            lv = ((av * (1.0 / (1.0 + jnp.exp(-av)))) * l_refs[1][...].astype(F32)).astype(BF16)
        else:
            lv = l_refs[0][...]
        for r_ref, o_ref, acc in zip(r_refs, o_refs, accs):
            part = _dot_tn(lv, r_ref[...].astype(BF16))

            @pl.when(k == 0)
            def _():
                acc[...] = part

            @pl.when(k > 0)
            def _():
                acc[...] += part

            @pl.when(k == nk - 1)
            def _():
                if slab_cols is None:
                    o_ref[...] = acc[...].astype(BF16)
                else:
                    for s in range(tn // slab_cols):
                        o_ref[s] = acc[:, slab_cols * s:slab_cols * (s + 1)].astype(BF16)

    return pl.pallas_call(
        body, name=name, grid=(M // tm, N // tn, nk),
        in_specs=[pl.BlockSpec((tk, tm), lambda i, j, k: (k, i)) for _ in lhs]
        + [pl.BlockSpec((tk, tn), lambda i, j, k: (k, j)) for _ in rhs],
        out_specs=[out_spec for _ in rhs],
        out_shape=[out_shape for _ in rhs],
        scratch_shapes=[pltpu.VMEM((tm, tn), F32) for _ in rhs],
        compiler_params=_cparams(("parallel", "parallel", "arbitrary")),
    )(*lhs, *rhs)


def matmul_nt(a, w, name):
    S, K = a.shape
    N = w.shape[0]

    def body(a_ref, w_ref, o_ref):
        o_ref[...] = _dot_nt(a_ref[...].astype(BF16), w_ref[...]).astype(BF16)

    return pl.pallas_call(
        body, name=name, grid=(S // TM,),
        in_specs=[pl.BlockSpec((TM, K), lambda i: (i, 0)), pl.BlockSpec((N, K), lambda i: (0, 0))],
        out_specs=pl.BlockSpec((TM, N), lambda i: (i, 0)),
        out_shape=jax.ShapeDtypeStruct((S, N), BF16),
        compiler_params=_cparams(("parallel",)),
    )(a, w)


def proj_bwd(dproj, w, x, g, dres, name):
    S, N = dproj.shape

    def body(dp_ref, w_ref, x_ref, g_ref, dr_ref, dx_ref, dg_ref):
        i = pl.program_id(0)

        @pl.when(i == 0)
        def _():
            dg_ref[...] = jnp.zeros_like(dg_ref)

        dh = _dot_nt(dp_ref[...], w_ref[...])
        xf = x_ref[...]
        r = _rms(xf)
        xh = xf * r
        dg_ref[...] += jnp.sum(dh * xh, axis=0, keepdims=True)
        dx_ref[...] = dr_ref[...] + _rms_bwd(dh, xh, r, g_ref[...])

    return pl.pallas_call(
        body, name=name, grid=(S // TM,),
        in_specs=[pl.BlockSpec((TM, N), lambda i: (i, 0)), pl.BlockSpec((D_MODEL, N), lambda i: (0, 0)),
                  pl.BlockSpec((TM, D_MODEL), lambda i: (i, 0)), pl.BlockSpec((1, D_MODEL), lambda i: (0, 0)),
                  pl.BlockSpec((TM, D_MODEL), lambda i: (i, 0))],
        out_specs=[pl.BlockSpec((TM, D_MODEL), lambda i: (i, 0)), pl.BlockSpec((1, D_MODEL), lambda i: (0, 0))],
        out_shape=[jax.ShapeDtypeStruct((S, D_MODEL), F32), jax.ShapeDtypeStruct((1, D_MODEL), F32)],
        compiler_params=_cparams(("arbitrary",)),
    )(dproj, w, x, g, dres)


def mixer_bwd(proj, y, ao, dmix, sinks, conv_w, qg, kg, cog, aog, name):
    S = proj.shape[0]
    nb = S // BLK
    KV_W = 2 * N_KV * HEAD

    def body(sinks_ref, p_ref, kvp_ref, y_ref, ao_ref, dm_ref, cw_ref, qg_ref, kg_ref, cog_ref, aog_ref,
             dp_ref, dcw_ref, dqg_ref, dkg_ref, dsk_ref, dcog_ref, daog_ref,
             dycar, kvcar, dao_scr, kvnow):
        step = pl.program_id(0)
        n = nb - 1 - step

        @pl.when(step == 0)
        def _():
            dycar[...] = jnp.zeros_like(dycar)
            kvcar[...] = jnp.zeros_like(kvcar)
            dcw_ref[...] = jnp.zeros_like(dcw_ref)
            dqg_ref[...] = jnp.zeros_like(dqg_ref)
            dkg_ref[...] = jnp.zeros_like(dkg_ref)
            dsk_ref[...] = jnp.zeros_like(dsk_ref)
            dcog_ref[...] = jnp.zeros_like(dcog_ref)
            daog_ref[...] = jnp.zeros_like(daog_ref)

        bg = p_ref[:, 0:CONV_CH].astype(F32)
        cg = p_ref[:, CONV_CH:2 * CONV_CH].astype(F32)
        hc = p_ref[:, 2 * CONV_CH:3 * CONV_CH].astype(F32)
        yv = y_ref[...].astype(F32)
        dmc = dm_ref[:, 0:CONV_CH].astype(F32)
        co = bg * yv
        rc = _rms(co)
        ch = co * rc
        dcog_ref[...] += jnp.sum(dmc * ch, axis=0, keepdims=True)
        dco = _rms_bwd(dmc, ch, rc, cog_ref[...])
        dp_ref[:, 0:CONV_CH] = (dco * yv).astype(BF16)
        dy = dco * bg
        row = lax.broadcasted_iota(jnp.int32, (BLK, CONV_CH), 0)
        nxt = dycar[...]
        dy1 = jnp.where(row == BLK - 1, nxt[0:1, :], pltpu.roll(dy, BLK - 1, 0))
        dy2 = jnp.where(row == BLK - 2, nxt[0:1, :], jnp.where(row == BLK - 1, nxt[1:2, :], pltpu.roll(dy, BLK - 2, 0)))
        dycar[...] = dy[0:8, :]
        du = cw_ref[2:3, :] * dy + cw_ref[1:2, :] * dy1 + cw_ref[0:1, :] * dy2
        dp_ref[:, CONV_CH:2 * CONV_CH] = (du * hc).astype(BF16)
        dp_ref[:, 2 * CONV_CH:3 * CONV_CH] = (du * cg).astype(BF16)
        u = cg * hc
        dcw_ref[0:1, :] += jnp.sum(dy2 * u, axis=0, keepdims=True)
        dcw_ref[1:2, :] += jnp.sum(dy1 * u, axis=0, keepdims=True)
        dcw_ref[2:3, :] += jnp.sum(dy * u, axis=0, keepdims=True)

        aov = ao_ref[...].astype(F32)
        dma = dm_ref[:, CONV_CH:].astype(F32)
        ra = _rms(aov)
        ah = aov * ra
        daog_ref[...] += jnp.sum(dma * ah, axis=0, keepdims=True)
        dao_scr[...] = _rms_bwd(dma, ah, ra, aog_ref[...])

        valid = _band_mask(n == 0)
        lane8 = lax.broadcasted_iota(jnp.int32, (1, N_Q), 1)
        dsk = jnp.zeros((1, N_Q), F32)
        for h in range(N_KV):
            kb = jnp.concatenate([kvp_ref[:, HEAD * h:HEAD * (h + 1)],
                                  p_ref[:, O_K + HEAD * h:O_K + HEAD * (h + 1)]], axis=0).astype(F32)
            rk = _rms(kb)
            kh = kb * rk
            kn = (kh * kg_ref[...]).astype(BF16)
            vb = jnp.concatenate([kvp_ref[:, N_KV * HEAD + HEAD * h:N_KV * HEAD + HEAD * (h + 1)],
                                  p_ref[:, O_V + HEAD * h:O_V + HEAD * (h + 1)]], axis=0)
            q = jnp.concatenate([p_ref[:, O_Q + HEAD * (GRP * h + g):O_Q + HEAD * (GRP * h + g + 1)]
                                 for g in range(GRP)], axis=0).astype(F32)
            rq = _rms(q)
            qh = q * rq
            qn = (qh * qg_ref[...]).astype(BF16)
            s = jnp.where(valid, _dot_nt(qn, kn) * SCALE, NEG_INF)
            pn, psink = _softmax_with_sink(s, _sink_col(sinks_ref, h))
            do = jnp.concatenate([dao_scr[:, HEAD * (GRP * h + g):HEAD * (GRP * h + g + 1)]
                                  for g in range(GRP)], axis=0).astype(BF16)
            dv = _dot_tn(pn.astype(BF16), do)
            dpr = _dot_nt(do, vb)
            delta = jnp.sum(pn * dpr, axis=-1, keepdims=True)
            ds = ((pn * (dpr - delta)) * SCALE).astype(BF16)
            sk = psink * delta
            for g in range(GRP):
                dsk = dsk - jnp.where(lane8 == GRP * h + g, jnp.sum(sk[BLK * g:BLK * (g + 1), :]), 0.0)
            dqn = _dot(ds, kn)
            dkn = _dot_tn(ds, qn)
            dqg_ref[...] += jnp.sum(dqn * qh, axis=0, keepdims=True)
            dq = _rms_bwd(dqn, qh, rq, qg_ref[...]).astype(BF16)
            for g in range(GRP):
                hq = GRP * h + g
                dp_ref[:, O_Q + HEAD * hq:O_Q + HEAD * (hq + 1)] = dq[BLK * g:BLK * (g + 1), :]
            dkg_ref[...] += jnp.sum(dkn * kh, axis=0, keepdims=True)
            dk = _rms_bwd(dkn, kh, rk, kg_ref[...])
            kc, vc = slice(HEAD * h, HEAD * (h + 1)), slice(N_KV * HEAD + HEAD * h, N_KV * HEAD + HEAD * (h + 1))
            kvnow[:, kc] = kvcar[:, kc] + dk[BLK:, :]
            kvnow[:, vc] = kvcar[:, vc] + dv[BLK:, :]
            kvcar[:, kc] = dk[:BLK, :]
            kvcar[:, vc] = dv[:BLK, :]
        dp_ref[:, O_K:] = kvnow[...].astype(BF16)
        dsk_ref[...] += dsk

    small = lambda shape: pl.BlockSpec(shape, lambda s: (0, 0))
    blk = lambda w: pl.BlockSpec((BLK, w), lambda s: (nb - 1 - s, 0))
    return pl.pallas_call(
        body, name=name, grid=(nb,),
        in_specs=[pl.BlockSpec(memory_space=pltpu.SMEM),
                  blk(IN_COLS),
                  pl.BlockSpec((BLK, KV_W), lambda s: (jnp.maximum(nb - 2 - s, 0), O_K // KV_W)),
                  blk(CONV_CH), blk(ATTN_W), blk(D_MODEL),
                  small((3, CONV_CH)), small((1, HEAD)), small((1, HEAD)), small((1, CONV_CH)), small((1, ATTN_W))],
        out_specs=[blk(IN_COLS), small((3, CONV_CH)), small((1, HEAD)), small((1, HEAD)), small((1, N_Q)),
                   small((1, CONV_CH)), small((1, ATTN_W))],
        out_shape=[jax.ShapeDtypeStruct((S, IN_COLS), BF16), jax.ShapeDtypeStruct((3, CONV_CH), F32),
                   jax.ShapeDtypeStruct((1, HEAD), F32), jax.ShapeDtypeStruct((1, HEAD), F32),
                   jax.ShapeDtypeStruct((1, N_Q), F32), jax.ShapeDtypeStruct((1, CONV_CH), F32),
                   jax.ShapeDtypeStruct((1, ATTN_W), F32)],
        scratch_shapes=[pltpu.VMEM((8, CONV_CH), F32), pltpu.VMEM((BLK, KV_W), F32),
                        pltpu.VMEM((BLK, ATTN_W), F32), pltpu.VMEM((BLK, KV_W), F32)],
        compiler_params=_cparams(("arbitrary",)),
    )(sinks, proj, proj, y, ao, dmix, conv_w, qg, kg, cog, aog)


ANY = pl.BlockSpec(memory_space=pl.ANY)
HBM = pl.BlockSpec(memory_space=pltpu.HBM)
SEM = pl.BlockSpec(memory_space=pltpu.SEMAPHORE)
OTHER_CHIPS = ((1, 0), (0, 1), (1, 1))


def _position():
    return lax.axis_index("x"), lax.axis_index("y"), lax.axis_index("c")


def _tie(value, token):
    return lax.optimization_barrier((value, token))[0]


def all_gather(shards, name):
    n = len(shards)

    def body(*refs):
        x_refs, out_refs = refs[:n], refs[n:2 * n]
        send_sems, recv_sems, local_sems = refs[2 * n:]
        x, y, c = _position()
        me, sibling = (x, y, c), (x, y, 1 - c)
        chips = [(x ^ mx, y ^ my) for mx, my in OTHER_CHIPS]

        def slab(a, px, py, pc):
            return out_refs[a].at[4 * px + 2 * py + pc]

        def copy(a, k, block, to, src=None):
            return pltpu.make_async_remote_copy(
                src_ref=slab(a, *block) if src is None else src, dst_ref=slab(a, *block),
                send_sem=send_sems.at[7 * a + k], recv_sem=recv_sems.at[7 * a + k], device_id=to, device_id_type=MESH_T)

        mine = [pltpu.make_async_copy(x_refs[a], slab(a, *me), local_sems.at[a]) for a in range(n)]
        for cp in mine:
            cp.start()
        first = []
        for a in range(n):
            first.append(copy(a, 0, me, sibling, src=x_refs[a]))
            first += [copy(a, 1 + j, me, (*chip, c), src=x_refs[a]) for j, chip in enumerate(chips)]
        for cp in first:
            cp.start()
        passed = []
        for a in range(n):
            for j, chip in enumerate(chips):
                copy(a, 1 + j, (*chip, c), me).wait_recv()
                passed.append(copy(a, 4 + j, (*chip, c), sibling))
                passed[-1].start()
        for a in range(n):
            copy(a, 0, sibling, me).wait_recv()
            for j, chip in enumerate(chips):
                copy(a, 4 + j, (*chip, 1 - c), me).wait_recv()
        for cp in first + passed:
            cp.wait_send()
        for cp in mine:
            cp.wait()

    return pl.pallas_call(
        body, name=name, in_specs=[ANY] * n, out_specs=[ANY] * n,
        out_shape=[jax.ShapeDtypeStruct((N_DEV, *s.shape), s.dtype) for s in shards],
        scratch_shapes=[pltpu.SemaphoreType.DMA((7 * n,)), pltpu.SemaphoreType.DMA((7 * n,)),
                        pltpu.SemaphoreType.DMA((n,))],
    )(*shards)


class SplitCopy:
    def __init__(self, name, arrays, n_copies, plan):
        n = len(arrays)
        self.name, self.n, self.n_copies, self.plan = name, n, n_copies, plan

        def body(*refs):
            in_refs, send_sems, recv_sems, token = refs[:n], refs[n], refs[n + 1], refs[2 * n + 2]
            for k, (src, dst, to) in enumerate(plan(_position(), in_refs)):
                pltpu.make_async_remote_copy(src_ref=src, dst_ref=dst, send_sem=send_sems.at[k],
                                             recv_sem=recv_sems.at[k], device_id=to, device_id_type=MESH_T).start()
            token[...] = jnp.zeros_like(token)

        outs = pl.pallas_call(
            body, name=name + "_start",
            out_shape=(pltpu.SemaphoreType.DMA((n_copies,)), pltpu.SemaphoreType.DMA((n_copies,)),
                       *[pltpu.HBM(a.shape, a.dtype) for a in arrays], jax.ShapeDtypeStruct((8, 128), F32)),
            in_specs=[HBM] * n,
            out_specs=(SEM, SEM, *[HBM] * n, pl.BlockSpec(memory_space=pltpu.VMEM)),
            input_output_aliases={i: 2 + i for i in range(n)},
            compiler_params=pltpu.CompilerParams(has_side_effects=pltpu.SideEffectType.DATAFLOW_SIDE_EFFECTING),
        )(*[pltpu.with_memory_space_constraint(a, pltpu.HBM) for a in arrays])
        self.send_sems, self.recv_sems = outs[0], outs[1]
        self.arrays, self.token = list(outs[2:2 + n]), outs[2 + n]

    def wait(self, after):
        n, plan = self.n, self.plan

        def body(*refs):
            in_refs, send_sems, recv_sems = refs[:n], refs[n], refs[n + 1]
            for k, (src, dst, to) in enumerate(plan(_position(), in_refs)):
                cp = pltpu.make_async_remote_copy(src_ref=src, dst_ref=dst, send_sem=send_sems.at[k],
                                                  recv_sem=recv_sems.at[k], device_id=to, device_id_type=MESH_T)
                cp.wait_send()
                cp.wait_recv()

        outs = pl.pallas_call(
            body, name=self.name + "_wait",
            out_shape=tuple(pltpu.HBM(a.shape, a.dtype) for a in self.arrays),
            in_specs=[HBM] * n + [SEM, SEM, ANY], out_specs=tuple([HBM] * n),
            input_output_aliases={i: i for i in range(n)},
            compiler_params=pltpu.CompilerParams(has_side_effects=pltpu.SideEffectType.DATAFLOW_SIDE_EFFECTING),
        )(*self.arrays, self.send_sems, self.recv_sems, after)
        return list(outs)


def gather_start(shards, me, name, after=None):
    n = len(shards)
    lands = [lax.dynamic_update_slice(lax.empty((N_DEV, *s.shape), s.dtype), s[None], (me, 0, 0)) for s in shards]
    if after is not None:
        shards = [_tie(shards[0], after)] + list(shards[1:])

    def plan(pos, refs):
        x, y, c = pos
        return [(refs[a], refs[n + a].at[4 * x + 2 * y + c], (x ^ mx, y ^ my, c))
                for a in range(n) for mx, my in OTHER_CHIPS]

    return SplitCopy(name, list(shards) + lands, 3 * n, plan)


def sibling_start(lands, name):
    n = len(lands)

    def plan(pos, refs):
        x, y, c = pos
        return [(refs[a].at[2 * q + c], refs[a].at[2 * q + c], (x, y, 1 - c)) for a in range(n) for q in range(4)]

    return SplitCopy(name, list(lands), 4 * n, plan)


def scatter_start(slabs, name):
    n = len(slabs)
    lands = [lax.empty((N_DEV - 1, *g.shape[1:]), g.dtype) for g in slabs]

    def plan(pos, refs):
        x, y, c = pos
        copies = []
        for a in range(n):
            for r in range(1, N_DEV):
                px, py, pc = x ^ ((r >> 2) & 1), y ^ ((r >> 1) & 1), c ^ (r & 1)
                copies.append((refs[a].at[4 * px + 2 * py + pc], refs[n + a].at[r - 1], (px, py, pc)))
        return copies

    return SplitCopy(name, list(slabs) + lands, (N_DEV - 1) * n, plan)


def columns_from_slabs(g, name):
    n, R, c = g.shape
    tr = 256

    def body(g_ref, o_ref):
        for d in range(n):
            o_ref[:, c * d:c * (d + 1)] = g_ref[d]

    return pl.pallas_call(
        body, name=name, grid=(R // tr,),
        in_specs=[pl.BlockSpec((n, tr, c), lambda i: (0, i, 0))],
        out_specs=pl.BlockSpec((tr, n * c), lambda i: (i, 0)),
        out_shape=jax.ShapeDtypeStruct((R, n * c), g.dtype),
        compiler_params=_cparams(("parallel",)),
    )(g)


def all_reduce_small(v, name):
    R, W = v.shape

    def body(v_ref, o_ref, recv, send_sems, recv_sems):
        x, y, c = _position()
        me = 4 * x + 2 * y + c
        copies = []
        for r in range(1, N_DEV):
            to = (x ^ ((r >> 2) & 1), y ^ ((r >> 1) & 1), c ^ (r & 1))
            copies.append(pltpu.make_async_remote_copy(
                src_ref=v_ref, dst_ref=recv.at[me], send_sem=send_sems.at[r - 1], recv_sem=recv_sems.at[r - 1],
                device_id=to, device_id_type=MESH_T))
        for cp in copies:
            cp.start()
        recv[pl.ds(me, 1)] = v_ref[...][None]
        for cp in copies:
            cp.wait()
        acc = recv[0]
        for s in range(1, N_DEV):
            acc = acc + recv[s]
        o_ref[...] = acc

    return pl.pallas_call(
        body, name=name,
        in_specs=[pl.BlockSpec(memory_space=pltpu.VMEM)], out_specs=pl.BlockSpec(memory_space=pltpu.VMEM),
        out_shape=jax.ShapeDtypeStruct((R, W), F32),
        scratch_shapes=[pltpu.VMEM((N_DEV, R, W), F32), pltpu.SemaphoreType.DMA((N_DEV - 1,)),
                        pltpu.SemaphoreType.DMA((N_DEV - 1,))],
    )(v)


def _row_tile(rows):
    if rows <= 512:
        return rows
    return max(t for t in range(8, 513, 8) if rows % t == 0)


def _adamw_update(w, g, m, v):
    mn = ADAM_B1 * m + (1.0 - ADAM_B1) * g
    vn = ADAM_B2 * v + (1.0 - ADAM_B2) * (g * g)
    m_hat = mn / (1.0 - ADAM_B1 ** ADAM_STEP)
    v_hat = vn / (1.0 - ADAM_B2 ** ADAM_STEP)
    return -ADAM_LR * (m_hat / (jnp.sqrt(v_hat) + ADAM_EPS) + ADAM_WD * w), mn, vn


def adamw(w, g, m, v, name):
    R, W = w.shape
    tr = _row_tile(R)

    def body(w_ref, g_ref, m_ref, v_ref, d_ref, mo_ref, vo_ref):
        d_ref[...], mo_ref[...], vo_ref[...] = _adamw_update(w_ref[...], g_ref[...], m_ref[...], v_ref[...])

    spec = pl.BlockSpec((tr, W), lambda i: (i, 0))
    return pl.pallas_call(
        body, name=name, grid=(R // tr,), in_specs=[spec] * 4, out_specs=[spec] * 3,
        out_shape=[jax.ShapeDtypeStruct((R, W), F32)] * 3,
        compiler_params=_cparams(("parallel",)),
    )(w, g, m, v)


def reduce_adamw(slabs, land, w, m, v, me_arr, name):
    _, R, W = slabs.shape
    tr = _row_tile(R)

    def body(me_ref, s_ref, l_ref, w_ref, m_ref, v_ref, g_ref, d_ref, mo_ref, vo_ref):
        g = s_ref[0].astype(F32)
        for r in range(N_DEV - 1):
            g = g + l_ref[r].astype(F32)
        g_ref[...] = g
        d_ref[...], mo_ref[...], vo_ref[...] = _adamw_update(w_ref[...], g, m_ref[...], v_ref[...])

    spec = pl.BlockSpec((tr, W), lambda i, me: (i, 0))
    return pl.pallas_call(
        body, name=name,
        grid_spec=pltpu.PrefetchScalarGridSpec(
            num_scalar_prefetch=1, grid=(R // tr,),
            in_specs=[pl.BlockSpec((1, tr, W), lambda i, me: (me[0], i, 0)),
                      pl.BlockSpec((N_DEV - 1, tr, W), lambda i, me: (0, i, 0)), spec, spec, spec],
            out_specs=[spec] * 4),
        out_shape=[jax.ShapeDtypeStruct((R, W), F32)] * 4,
        compiler_params=_cparams(("parallel",)),
    )(me_arr, slabs, land, w, m, v)


SMALL_NAMES = ("norm1_g", "q_norm_g", "k_norm_g", "sinks", "conv_out_g", "attn_out_g", "norm2_g", "conv_w")
SMALL_SIZES = (D_MODEL, HEAD, HEAD, N_Q, CONV_CH, ATTN_W, D_MODEL, 3 * CONV_CH)
SMALL_ROWS = 80


def kernel(x, norm1_g, w_in, conv_w, q_norm_g, k_norm_g, sinks, conv_out_g, attn_out_g, w_o, norm2_g, w_gate, w_up, w_down, loss_target, m_norm1_g, m_w_in, m_conv_w, m_q_norm_g, m_k_norm_g, m_sinks, m_conv_out_g, m_attn_out_g, m_w_o, m_norm2_g, m_w_gate, m_w_up, m_w_down, v_norm1_g, v_w_in, v_conv_w, v_q_norm_g, v_k_norm_g, v_sinks, v_conv_out_g, v_attn_out_g, v_w_o, v_norm2_g, v_w_gate, v_w_up, v_w_down):
    xi, yi, ci = _position()
    me = 4 * xi + 2 * yi + ci
    me_arr = jnp.reshape(me, (1,)).astype(jnp.int32)
    xs, tgt = x[0], loss_target[0]
    bf = lambda a: a.astype(BF16)

    g_in0, g_o0 = all_gather([bf(w_in[0]), bf(w_o[0])], "gather_first")
    ag_ffn0 = gather_start([bf(w_gate[0]), bf(w_up[0]), bf(w_down[0])], me, "gather_ffn0")
    ag_l1 = gather_start([bf(w_in[1]), bf(w_o[1]), bf(w_gate[1]), bf(w_up[1]), bf(w_down[1])], me, "gather_layer1",
                         after=ag_ffn0.token)
    conv_place = lax.dynamic_update_slice(jnp.zeros((DEPTH, 3, CONV_CH), F32), conv_w, (0, 0, me * HEAD))
    conv_full = all_reduce_small(_tie(conv_place, ag_l1.token).reshape(DEPTH * 3 * CONV_CH // 128, 128),
                                 "gather_conv_w").reshape(DEPTH, 3, CONV_CH)
    small = [dict(norm1_g=norm1_g[l][None], conv_w=conv_full[l], q_norm_g=q_norm_g[l][None],
                  k_norm_g=k_norm_g[l][None], sinks=sinks[l], conv_out_g=conv_out_g[l][None],
                  attn_out_g=attn_out_g[l][None], norm2_g=norm2_g[l][None]) for l in range(DEPTH)]
    weights = [dict(w_in=columns_from_slabs(g_in0, "layout_w_in0"), w_o=g_o0.reshape(D_MODEL, D_MODEL)), {}]

    saved = []
    xl = xs
    for l in range(DEPTH):
        sp, wl = small[l], weights[l]
        h, proj = norm_proj(xl, sp["norm1_g"], wl["w_in"], f"norm_proj{l}")
        mix, y, ao = mixer_fwd(proj, sp["sinks"], sp["conv_w"], sp["q_norm_g"], sp["k_norm_g"],
                               sp["conv_out_g"], sp["attn_out_g"], f"mixer_fwd{l}")
        if l == 0:
            pass_ffn0 = sibling_start(ag_ffn0.wait(mix)[3:], "pass_ffn0")
            mix = _tie(mix, pass_ffn0.token)
        x1 = matmul_residual(mix, wl["w_o"], xl, f"out_proj{l}")
        if l == 0:
            g_gate, g_up, g_down = pass_ffn0.wait(x1)
            wl.update(w_gate=columns_from_slabs(g_gate, "layout_w_gate0"),
                      w_up=columns_from_slabs(g_up, "layout_w_up0"), w_down=g_down.reshape(D_FF, D_MODEL))
            pass_l1 = sibling_start(ag_l1.wait(x1)[5:], "pass_layer1")
            x1 = _tie(x1, pass_l1.token)
        h2, a, b, x2 = ffn_fwd(x1, sp["norm2_g"], wl["w_gate"], wl["w_up"], wl["w_down"], f"ffn_fwd{l}")
        if l == 0:
            g_in, g_o, g_gate, g_up, g_down = pass_l1.wait(x2)
            weights[1] = dict(w_in=columns_from_slabs(g_in, "layout_w_in1"), w_o=g_o.reshape(D_MODEL, D_MODEL),
                              w_gate=columns_from_slabs(g_gate, "layout_w_gate1"),
                              w_up=columns_from_slabs(g_up, "layout_w_up1"), w_down=g_down.reshape(D_FF, D_MODEL))
        saved.append((xl, h, proj, mix, y, ao, x1, h2, a, b))
        xl = x2
    loss_row, dx = loss_head(xl, tgt, "loss_head")
    loss = lax.psum(loss_row[0, 0], ("x", "y", "c"))

    shard = dict(w_in=(w_in, m_w_in, v_w_in), w_o=(w_o, m_w_o, v_w_o), w_gate=(w_gate, m_w_gate, v_w_gate),
                 w_up=(w_up, m_w_up, v_w_up), w_down=(w_down, m_w_down, v_w_down))
    stepped = {n: [None] * DEPTH for n in shard}
    gsmall = [None] * DEPTH

    def finish(sc, names, after, l):
        arrays = sc.wait(after)
        k = len(names)
        for i, n in enumerate(names):
            w, m, v = shard[n]
            stepped[n][l] = reduce_adamw(arrays[i], arrays[k + i], w[l], m[l], v[l], me_arr, f"reduce_adamw_{n}{l}")

    for l in reversed(range(DEPTH)):
        sp, wl = small[l], weights[l]
        x0, h, proj, mix, y, ao, x1, h2, a, b = saved[l]
        (d_wd,) = grad_weight((a, b), [dx], TF, D_MODEL, f"grad_w_down{l}", swiglu=True)
        sc_down = scatter_start([d_wd.reshape(N_DEV, ROWS_FF, D_MODEL)], f"scatter_w_down{l}")
        da, db, dx1, d_g2 = ffn_bwd(_tie(dx, sc_down.token), a, b, x1, sp["norm2_g"], wl["w_gate"], wl["w_up"],
                                    wl["w_down"], f"ffn_bwd{l}")
        d_wg, d_wu = grad_weight((h2,), [da, db], D_MODEL, TF, f"grad_w_gate_up{l}", slab_cols=ROWS_FF)
        sc_gu = scatter_start([d_wg, d_wu], f"scatter_w_gate_up{l}")
        (d_wo,) = grad_weight((_tie(mix, sc_gu.token),), [dx1], D_MODEL, D_MODEL, f"grad_w_o{l}")
        sc_o = scatter_start([d_wo.reshape(N_DEV, ROWS_O, D_MODEL)], f"scatter_w_o{l}")
        dmix = matmul_nt(_tie(dx1, sc_o.token), wl["w_o"], f"out_proj_bwd{l}")
        finish(sc_down, ["w_down"], dmix, l)
        dproj, d_cw, d_qg, d_kg, d_sk, d_cog, d_aog = mixer_bwd(
            proj, y, ao, dmix, sp["sinks"], sp["conv_w"], sp["q_norm_g"], sp["k_norm_g"],
            sp["conv_out_g"], sp["attn_out_g"], f"mixer_bwd{l}")
        finish(sc_gu, ["w_gate", "w_up"], dproj, l)
        finish(sc_o, ["w_o"], dproj, l)
        (d_win,) = grad_weight((h,), [dproj], D_MODEL, IN_COLS // 2, f"grad_w_in{l}", slab_cols=ROWS_IN)
        sc_in = scatter_start([d_win], f"scatter_w_in{l}")
        dx, d_g1 = proj_bwd(_tie(dproj, sc_in.token), wl["w_in"], x0, sp["norm1_g"], dx1, f"proj_bwd{l}")
        finish(sc_in, ["w_in"], dx, l)
        gsmall[l] = dict(norm1_g=d_g1, conv_w=d_cw, q_norm_g=d_qg, k_norm_g=d_kg, sinks=d_sk,
                         conv_out_g=d_cog, attn_out_g=d_aog, norm2_g=d_g2)
    grad_x = dx[None]

    flat = jnp.concatenate([gsmall[l][n].reshape(-1) for l in range(DEPTH) for n in SMALL_NAMES])
    flat = jnp.pad(flat, (0, SMALL_ROWS * 128 - flat.shape[0])).reshape(SMALL_ROWS, 128)
    flat = all_reduce_small(flat, "all_reduce_small_grads").reshape(-1)
    gs = {n: [] for n in SMALL_NAMES}
    off = 0
    for l in range(DEPTH):
        for n, size in zip(SMALL_NAMES, SMALL_SIZES):
            gs[n].append(flat[off:off + size])
            off += size
    gs = {n: jnp.stack(v) for n, v in gs.items()}
    g_conv = lax.dynamic_slice(gs["conv_w"].reshape(DEPTH, 3, CONV_CH), (0, 0, me * HEAD), (DEPTH, 3, HEAD))

    gs["conv_w"] = g_conv
    params = dict(norm1_g=(norm1_g, m_norm1_g, v_norm1_g), conv_w=(conv_w, m_conv_w, v_conv_w),
                  q_norm_g=(q_norm_g, m_q_norm_g, v_q_norm_g), k_norm_g=(k_norm_g, m_k_norm_g, v_k_norm_g),
                  sinks=(sinks, m_sinks, v_sinks), conv_out_g=(conv_out_g, m_conv_out_g, v_conv_out_g),
                  attn_out_g=(attn_out_g, m_attn_out_g, v_attn_out_g), norm2_g=(norm2_g, m_norm2_g, v_norm2_g))
    names = ("norm1_g", "w_in", "conv_w", "q_norm_g", "k_norm_g", "sinks", "conv_out_g", "attn_out_g", "w_o",
             "norm2_g", "w_gate", "w_up", "w_down")

    out = {}
    for n in names:
        if n in shard:
            out[n] = tuple(jnp.stack([stepped[n][l][i] for l in range(DEPTH)]) for i in range(4))
        else:
            w, m, v = params[n]
            two_d = (-1, w.shape[-1])
            d, mn, vn = adamw(w.reshape(two_d), gs[n].reshape(two_d), m.reshape(two_d), v.reshape(two_d), f"adamw_{n}")
            out[n] = (gs[n].reshape(w.shape), d.reshape(w.shape), mn.reshape(w.shape), vn.reshape(w.shape))
    return (loss, grad_x, *[out[n][i] for i in range(4) for n in names])
```

```python
import functools

import jax
import jax.numpy as jnp
from jax import lax
from jax.experimental import pallas as pl
from jax.experimental.pallas import tpu as pltpu

F32 = jnp.float32
BF16 = jnp.bfloat16

D_MODEL = 1024
CONV_CH = 512
ATTN_W = 512
N_Q = 8
N_KV = 2
GRP = N_Q // N_KV
HEAD = 64
IN_COLS = 2304
D_FF = 2816
BLK = 128
O_Q = 3 * CONV_CH
O_K = O_Q + ATTN_W
O_V = O_K + N_KV * HEAD
EPS = 1e-6
NEG_INF = -1e30
SCALE = HEAD ** -0.5
N_DEV = 8
DEPTH = 2

ADAM_LR = 0.001
ADAM_B1 = 0.9
ADAM_B2 = 0.999
ADAM_EPS = 1e-08
ADAM_WD = 0.01
ADAM_STEP = 10

VMEM_LIMIT = 56 * 1024 * 1024
TM = 512
MESH_T = pl.DeviceIdType.MESH

ROWS_IN, ROWS_O, ROWS_FF = IN_COLS // N_DEV, D_MODEL // N_DEV, D_FF // N_DEV


ANY = pl.BlockSpec(memory_space=pl.ANY)
HBM = pl.BlockSpec(memory_space=pltpu.HBM)
SEM = pl.BlockSpec(memory_space=pltpu.SEMAPHORE)


def _cparams(sem):
    return pltpu.CompilerParams(dimension_semantics=sem, vmem_limit_bytes=VMEM_LIMIT)


def _pallas_after(dep, body, *args, in_specs, **kw):
    if dep is None:
        return pl.pallas_call(body, in_specs=in_specs, **kw)(*args)

    def after_dep(dep_ref, *refs):
        body(*refs)

    return pl.pallas_call(after_dep, in_specs=[ANY, *in_specs], **kw)(dep, *args)


def _dot(a, b):
    return jnp.dot(a, b, preferred_element_type=F32)


def _dot_nt(a, b):
    return lax.dot_general(a, b, (((1,), (1,)), ((), ())), preferred_element_type=F32)


def _dot_tn(a, b):
    return lax.dot_general(a, b, (((0,), (0,)), ((), ())), preferred_element_type=F32)


def _rms(v):
    return lax.rsqrt(jnp.mean(v * v, axis=-1, keepdims=True) + EPS)


def _rms_bwd(dyv, xh, r, g):
    dxh = dyv * g
    return r * (dxh - xh * jnp.mean(dxh * xh, axis=-1, keepdims=True))


def norm_proj(x, g, w, name, dep=None):
    S, N = x.shape[0], w.shape[1]

    def body(x_ref, g_ref, w_ref, h_ref, p_ref):
        xf = x_ref[...]
        h = ((xf * _rms(xf)) * g_ref[...]).astype(BF16)
        h_ref[...] = h
        p_ref[...] = _dot(h, w_ref[...]).astype(BF16)

    return _pallas_after(
        dep, body, x, g, w, name=name, grid=(S // TM,),
        in_specs=[pl.BlockSpec((TM, D_MODEL), lambda i: (i, 0)),
                  pl.BlockSpec((1, D_MODEL), lambda i: (0, 0)),
                  pl.BlockSpec((D_MODEL, N), lambda i: (0, 0))],
        out_specs=[pl.BlockSpec((TM, D_MODEL), lambda i: (i, 0)),
                   pl.BlockSpec((TM, N), lambda i: (i, 0))],
        out_shape=[jax.ShapeDtypeStruct((S, D_MODEL), BF16), jax.ShapeDtypeStruct((S, N), BF16)],
        compiler_params=_cparams(("parallel",)),
    )


def _band_mask(first_block):
    qi = lax.broadcasted_iota(jnp.int32, (GRP * BLK, 2 * BLK), 0) & (BLK - 1)
    kj = lax.broadcasted_iota(jnp.int32, (GRP * BLK, 2 * BLK), 1)
    diff = qi + BLK - kj
    valid = (diff >= 0) & (diff < BLK)
    return valid & ((kj >= BLK) | jnp.logical_not(first_block))


def _sink_col(sinks_ref, h):
    return jnp.concatenate([jnp.full((BLK, 1), sinks_ref[GRP * h + g], F32) for g in range(GRP)], axis=0)


def _softmax_with_sink(s, sink):
    m = jnp.maximum(jnp.max(s, axis=-1, keepdims=True), sink)
    p = jnp.exp(s - m)
    es = jnp.exp(sink - m)
    denom = jnp.sum(p, axis=-1, keepdims=True) + es
    return p / denom, es / denom


def mixer_fwd(proj, sinks, conv_w, qg, kg, cog, aog, name):
    S = proj.shape[0]
    nb = S // BLK

    def body(sinks_ref, p_ref, kvp_ref, cw_ref, qg_ref, kg_ref, cog_ref, aog_ref,
             mix_ref, y_ref, ao_ref, ucar, ao_scr):
        n = pl.program_id(0)

        @pl.when(n == 0)
        def _():
            ucar[...] = jnp.zeros_like(ucar)

        bg = p_ref[:, 0:CONV_CH].astype(F32)
        u = p_ref[:, CONV_CH:2 * CONV_CH].astype(F32) * p_ref[:, 2 * CONV_CH:3 * CONV_CH].astype(F32)
        row = lax.broadcasted_iota(jnp.int32, (BLK, CONV_CH), 0)
        prev = ucar[...]
        u1 = jnp.where(row == 0, prev[7:8, :], pltpu.roll(u, 1, 0))
        u2 = jnp.where(row == 0, prev[6:7, :], jnp.where(row == 1, prev[7:8, :], pltpu.roll(u, 2, 0)))
        ucar[...] = u[BLK - 8:BLK, :]
        y = cw_ref[0:1, :] * u2 + cw_ref[1:2, :] * u1 + cw_ref[2:3, :] * u
        y_ref[...] = y.astype(BF16)
        co = bg * y
        mix_ref[:, 0:CONV_CH] = ((co * _rms(co)) * cog_ref[...]).astype(BF16)

        valid = _band_mask(n == 0)
        for h in range(N_KV):
            kb = jnp.concatenate([kvp_ref[:, HEAD * h:HEAD * (h + 1)],
                                  p_ref[:, O_K + HEAD * h:O_K + HEAD * (h + 1)]], axis=0).astype(F32)
            kn = ((kb * _rms(kb)) * kg_ref[...]).astype(BF16)
            vb = jnp.concatenate([kvp_ref[:, N_KV * HEAD + HEAD * h:N_KV * HEAD + HEAD * (h + 1)],
                                  p_ref[:, O_V + HEAD * h:O_V + HEAD * (h + 1)]], axis=0)
            q = jnp.concatenate([p_ref[:, O_Q + HEAD * (GRP * h + g):O_Q + HEAD * (GRP * h + g + 1)]
                                 for g in range(GRP)], axis=0).astype(F32)
            qn = ((q * _rms(q)) * qg_ref[...]).astype(BF16)
            s = jnp.where(valid, _dot_nt(qn, kn) * SCALE, NEG_INF)
            pn, _ = _softmax_with_sink(s, _sink_col(sinks_ref, h))
            o = _dot(pn.astype(BF16), vb)
            for g in range(GRP):
                hq = GRP * h + g
                ao_scr[:, HEAD * hq:HEAD * (hq + 1)] = o[BLK * g:BLK * (g + 1), :]
        ao = ao_scr[...]
        ao_ref[...] = ao.astype(BF16)
        mix_ref[:, CONV_CH:] = ((ao * _rms(ao)) * aog_ref[...]).astype(BF16)

    small = lambda shape: pl.BlockSpec(shape, lambda n: (0, 0))
    return pl.pallas_call(
        body, name=name, grid=(nb,),
        in_specs=[pl.BlockSpec(memory_space=pltpu.SMEM),
                  pl.BlockSpec((BLK, IN_COLS), lambda n: (n, 0)),
                  pl.BlockSpec((BLK, 2 * N_KV * HEAD), lambda n: (jnp.maximum(n - 1, 0), O_K // (2 * N_KV * HEAD))),
                  small((3, CONV_CH)), small((1, HEAD)), small((1, HEAD)), small((1, CONV_CH)), small((1, ATTN_W))],
        out_specs=[pl.BlockSpec((BLK, D_MODEL), lambda n: (n, 0)),
                   pl.BlockSpec((BLK, CONV_CH), lambda n: (n, 0)),
                   pl.BlockSpec((BLK, ATTN_W), lambda n: (n, 0))],
        out_shape=[jax.ShapeDtypeStruct((S, D_MODEL), BF16), jax.ShapeDtypeStruct((S, CONV_CH), BF16),
                   jax.ShapeDtypeStruct((S, ATTN_W), BF16)],
        scratch_shapes=[pltpu.VMEM((8, CONV_CH), F32), pltpu.VMEM((BLK, ATTN_W), F32)],
        compiler_params=_cparams(("arbitrary",)),
    )(sinks, proj, proj, conv_w, qg, kg, cog, aog)


def matmul_residual(a, w, res, name, dep=None):
    S, K = a.shape
    N = w.shape[1]

    def body(a_ref, w_ref, r_ref, o_ref):
        o_ref[...] = r_ref[...] + _dot(a_ref[...], w_ref[...])

    return _pallas_after(
        dep, body, a, w, res, name=name, grid=(S // TM,),
        in_specs=[pl.BlockSpec((TM, K), lambda i: (i, 0)), pl.BlockSpec((K, N), lambda i: (0, 0)),
                  pl.BlockSpec((TM, N), lambda i: (i, 0))],
        out_specs=pl.BlockSpec((TM, N), lambda i: (i, 0)),
        out_shape=jax.ShapeDtypeStruct((S, N), F32),
        compiler_params=_cparams(("parallel",)),
    )


TF = 1408


def ffn_fwd(x1, g, wg, wu, wd, name, dep=None):
    S = x1.shape[0]
    nj = D_FF // TF

    def body(x_ref, g_ref, wg_ref, wu_ref, wd_ref, h2_ref, a_ref, b_ref, o_ref, acc):
        j = pl.program_id(1)

        @pl.when(j == 0)
        def _():
            xf = x_ref[...]
            h2_ref[...] = ((xf * _rms(xf)) * g_ref[...]).astype(BF16)
            acc[...] = xf

        h2 = h2_ref[...]
        a = _dot(h2, wg_ref[...])
        b = _dot(h2, wu_ref[...])
        a_ref[...] = a.astype(BF16)
        b_ref[...] = b.astype(BF16)
        f = (a * (1.0 / (1.0 + jnp.exp(-a)))) * b
        acc[...] += _dot(f.astype(BF16), wd_ref[...])

        @pl.when(j == nj - 1)
        def _():
            o_ref[...] = acc[...]

    return _pallas_after(
        dep, body, x1, g, wg, wu, wd, name=name, grid=(S // TM, nj),
        in_specs=[pl.BlockSpec((TM, D_MODEL), lambda i, j: (i, 0)), pl.BlockSpec((1, D_MODEL), lambda i, j: (0, 0)),
                  pl.BlockSpec((D_MODEL, TF), lambda i, j: (0, j)), pl.BlockSpec((D_MODEL, TF), lambda i, j: (0, j)),
                  pl.BlockSpec((TF, D_MODEL), lambda i, j: (j, 0))],
        out_specs=[pl.BlockSpec((TM, D_MODEL), lambda i, j: (i, 0)), pl.BlockSpec((TM, TF), lambda i, j: (i, j)),
                   pl.BlockSpec((TM, TF), lambda i, j: (i, j)), pl.BlockSpec((TM, D_MODEL), lambda i, j: (i, 0))],
        out_shape=[jax.ShapeDtypeStruct((S, D_MODEL), BF16), jax.ShapeDtypeStruct((S, D_FF), BF16),
                   jax.ShapeDtypeStruct((S, D_FF), BF16), jax.ShapeDtypeStruct((S, D_MODEL), F32)],
        scratch_shapes=[pltpu.VMEM((TM, D_MODEL), F32)],
        compiler_params=_cparams(("parallel", "arbitrary")),
    )


def loss_head(yv, tgt, name):
    S = yv.shape[0]
    ni = S // TM

    def body(y_ref, t_ref, l_ref, d_ref, acc):
        i = pl.program_id(0)

        @pl.when(i == 0)
        def _():
            acc[...] = jnp.zeros_like(acc)

        e = y_ref[...] - t_ref[...]
        d_ref[...] = e * (1.0 / D_MODEL)
        acc[...] += jnp.sum(e * e, axis=0, keepdims=True)

        @pl.when(i == ni - 1)
        def _():
            l_ref[...] = jnp.full((1, 128), jnp.sum(acc[...]) * (0.5 / D_MODEL), F32)

    return pl.pallas_call(
        body, name=name, grid=(ni,),
        in_specs=[pl.BlockSpec((TM, D_MODEL), lambda i: (i, 0)), pl.BlockSpec((TM, D_MODEL), lambda i: (i, 0))],
        out_specs=[pl.BlockSpec((1, 128), lambda i: (0, 0)), pl.BlockSpec((TM, D_MODEL), lambda i: (i, 0))],
        out_shape=[jax.ShapeDtypeStruct((1, 128), F32), jax.ShapeDtypeStruct((S, D_MODEL), F32)],
        scratch_shapes=[pltpu.VMEM((1, D_MODEL), F32)],
        compiler_params=_cparams(("arbitrary",)),
    )(yv, tgt)


def ffn_bwd(dx2, a, b, x1, g, wg, wu, wd, name, dep=None):
    S = dx2.shape[0]
    nj = D_FF // TF

    def body(dx_ref, a_ref, b_ref, x_ref, g_ref, wg_ref, wu_ref, wd_ref, da_ref, db_ref, dx1_ref, dg_ref, dxb, acc):
        i, j = pl.program_id(0), pl.program_id(1)

        @pl.when((i == 0) & (j == 0))
        def _():
            dg_ref[...] = jnp.zeros_like(dg_ref)

        @pl.when(j == 0)
        def _():
            dxb[...] = dx_ref[...].astype(BF16)
            acc[...] = jnp.zeros_like(acc)

        df = _dot_nt(dxb[...], wd_ref[...])
        av = a_ref[...].astype(F32)
        bv = b_ref[...].astype(F32)
        sg = 1.0 / (1.0 + jnp.exp(-av))
        da = ((df * bv) * (sg * (1.0 + av * (1.0 - sg)))).astype(BF16)
        db = (df * (av * sg)).astype(BF16)
        da_ref[...] = da
        db_ref[...] = db
        acc[...] += _dot_nt(da, wg_ref[...]) + _dot_nt(db, wu_ref[...])

        @pl.when(j == nj - 1)
        def _():
            xf = x_ref[...]
            r = _rms(xf)
            xh = xf * r
            dh = acc[...]
            dg_ref[...] += jnp.sum(dh * xh, axis=0, keepdims=True)
            dx1_ref[...] = dx_ref[...] + _rms_bwd(dh, xh, r, g_ref[...])

    return _pallas_after(
        dep, body, dx2, a, b, x1, g, wg, wu, wd, name=name, grid=(S // TM, nj),
        in_specs=[pl.BlockSpec((TM, D_MODEL), lambda i, j: (i, 0)),
                  pl.BlockSpec((TM, TF), lambda i, j: (i, j)), pl.BlockSpec((TM, TF), lambda i, j: (i, j)),
                  pl.BlockSpec((TM, D_MODEL), lambda i, j: (i, 0)), pl.BlockSpec((1, D_MODEL), lambda i, j: (0, 0)),
                  pl.BlockSpec((D_MODEL, TF), lambda i, j: (0, j)), pl.BlockSpec((D_MODEL, TF), lambda i, j: (0, j)),
                  pl.BlockSpec((TF, D_MODEL), lambda i, j: (j, 0))],
        out_specs=[pl.BlockSpec((TM, TF), lambda i, j: (i, j)), pl.BlockSpec((TM, TF), lambda i, j: (i, j)),
                   pl.BlockSpec((TM, D_MODEL), lambda i, j: (i, 0)), pl.BlockSpec((1, D_MODEL), lambda i, j: (0, 0))],
        out_shape=[jax.ShapeDtypeStruct((S, D_FF), BF16), jax.ShapeDtypeStruct((S, D_FF), BF16),
                   jax.ShapeDtypeStruct((S, D_MODEL), F32), jax.ShapeDtypeStruct((1, D_MODEL), F32)],
        scratch_shapes=[pltpu.VMEM((TM, D_MODEL), BF16), pltpu.VMEM((TM, D_MODEL), F32)],
        compiler_params=_cparams(("arbitrary", "arbitrary")),
    )


def grad_weight(lhs, rhs, tm, tn, name, swiglu=False, slab_cols=None, dep=None):
    S = rhs[0].shape[0]
    M = lhs[0].shape[1]
    N = rhs[0].shape[1]
    tk = 512
    nk = S // tk
    n_l, n_r = len(lhs), len(rhs)
    if slab_cols is None:
        out_spec = pl.BlockSpec((tm, tn), lambda i, j, k: (i, j))
        out_shape = jax.ShapeDtypeStruct((M, N), BF16)
    else:
        out_spec = pl.BlockSpec((tn // slab_cols, tm, slab_cols), lambda i, j, k: (j, i, 0))
        out_shape = jax.ShapeDtypeStruct((N // slab_cols, M, slab_cols), BF16)

    def body(*refs):
        l_refs, r_refs = refs[:n_l], refs[n_l:n_l + n_r]
        o_refs, accs = refs[n_l + n_r:n_l + 2 * n_r], refs[n_l + 2 * n_r:]
        k = pl.program_id(2)
        if swiglu:
            av = l_refs[0][...].astype(F32)
            lv = ((av * (1.0 / (1.0 + jnp.exp(-av)))) * l_refs[1][...].astype(F32)).astype(BF16)
        else:
            lv = l_refs[0][...]
        for r_ref, o_ref, acc in zip(r_refs, o_refs, accs):
            part = _dot_tn(lv, r_ref[...].astype(BF16))

            @pl.when(k == 0)
            def _():
                acc[...] = part

            @pl.when(k > 0)
            def _():
                acc[...] += part

            @pl.when(k == nk - 1)
            def _():
                if slab_cols is None:
                    o_ref[...] = acc[...].astype(BF16)
                else:
                    for s in range(tn // slab_cols):
                        o_ref[s] = acc[:, slab_cols * s:slab_cols * (s + 1)].astype(BF16)

    return _pallas_after(
        dep, body, *lhs, *rhs, name=name, grid=(M // tm, N // tn, nk),
        in_specs=[pl.BlockSpec((tk, tm), lambda i, j, k: (k, i)) for _ in lhs]
        + [pl.BlockSpec((tk, tn), lambda i, j, k: (k, j)) for _ in rhs],
        out_specs=[out_spec for _ in rhs],
        out_shape=[out_shape for _ in rhs],
        scratch_shapes=[pltpu.VMEM((tm, tn), F32) for _ in rhs],
        compiler_params=_cparams(("parallel", "parallel", "arbitrary")),
    )


def matmul_nt(a, w, name, dep=None):
    S, K = a.shape
    N = w.shape[0]

    def body(a_ref, w_ref, o_ref):
        o_ref[...] = _dot_nt(a_ref[...].astype(BF16), w_ref[...]).astype(BF16)

    return _pallas_after(
        dep, body, a, w, name=name, grid=(S // TM,),
        in_specs=[pl.BlockSpec((TM, K), lambda i: (i, 0)), pl.BlockSpec((N, K), lambda i: (0, 0))],
        out_specs=pl.BlockSpec((TM, N), lambda i: (i, 0)),
        out_shape=jax.ShapeDtypeStruct((S, N), BF16),
        compiler_params=_cparams(("parallel",)),
    )


def proj_bwd(dproj, w, x, g, dres, name, dep=None):
    S, N = dproj.shape

    def body(dp_ref, w_ref, x_ref, g_ref, dr_ref, dx_ref, dg_ref):
        i = pl.program_id(0)

        @pl.when(i == 0)
        def _():
            dg_ref[...] = jnp.zeros_like(dg_ref)

        dh = _dot_nt(dp_ref[...], w_ref[...])
        xf = x_ref[...]
        r = _rms(xf)
        xh = xf * r
        dg_ref[...] += jnp.sum(dh * xh, axis=0, keepdims=True)
        dx_ref[...] = dr_ref[...] + _rms_bwd(dh, xh, r, g_ref[...])

    return _pallas_after(
        dep, body, dproj, w, x, g, dres, name=name, grid=(S // TM,),
        in_specs=[pl.BlockSpec((TM, N), lambda i: (i, 0)), pl.BlockSpec((D_MODEL, N), lambda i: (0, 0)),
                  pl.BlockSpec((TM, D_MODEL), lambda i: (i, 0)), pl.BlockSpec((1, D_MODEL), lambda i: (0, 0)),
                  pl.BlockSpec((TM, D_MODEL), lambda i: (i, 0))],
        out_specs=[pl.BlockSpec((TM, D_MODEL), lambda i: (i, 0)), pl.BlockSpec((1, D_MODEL), lambda i: (0, 0))],
        out_shape=[jax.ShapeDtypeStruct((S, D_MODEL), F32), jax.ShapeDtypeStruct((1, D_MODEL), F32)],
        compiler_params=_cparams(("arbitrary",)),
    )


def mixer_bwd(proj, y, ao, dmix, sinks, conv_w, qg, kg, cog, aog, name):
    S = proj.shape[0]
    nb = S // BLK
    KV_W = 2 * N_KV * HEAD

    def body(sinks_ref, p_ref, kvp_ref, y_ref, ao_ref, dm_ref, cw_ref, qg_ref, kg_ref, cog_ref, aog_ref,
             dp_ref, dcw_ref, dqg_ref, dkg_ref, dsk_ref, dcog_ref, daog_ref,
             dycar, kvcar, dao_scr, kvnow):
        step = pl.program_id(0)
        n = nb - 1 - step

        @pl.when(step == 0)
        def _():
            dycar[...] = jnp.zeros_like(dycar)
            kvcar[...] = jnp.zeros_like(kvcar)
            dcw_ref[...] = jnp.zeros_like(dcw_ref)
            dqg_ref[...] = jnp.zeros_like(dqg_ref)
            dkg_ref[...] = jnp.zeros_like(dkg_ref)
            dsk_ref[...] = jnp.zeros_like(dsk_ref)
            dcog_ref[...] = jnp.zeros_like(dcog_ref)
            daog_ref[...] = jnp.zeros_like(daog_ref)

        bg = p_ref[:, 0:CONV_CH].astype(F32)
        cg = p_ref[:, CONV_CH:2 * CONV_CH].astype(F32)
        hc = p_ref[:, 2 * CONV_CH:3 * CONV_CH].astype(F32)
        yv = y_ref[...].astype(F32)
        dmc = dm_ref[:, 0:CONV_CH].astype(F32)
        co = bg * yv
        rc = _rms(co)
        ch = co * rc
        dcog_ref[...] += jnp.sum(dmc * ch, axis=0, keepdims=True)
        dco = _rms_bwd(dmc, ch, rc, cog_ref[...])
        dp_ref[:, 0:CONV_CH] = (dco * yv).astype(BF16)
        dy = dco * bg
        row = lax.broadcasted_iota(jnp.int32, (BLK, CONV_CH), 0)
        nxt = dycar[...]
        dy1 = jnp.where(row == BLK - 1, nxt[0:1, :], pltpu.roll(dy, BLK - 1, 0))
        dy2 = jnp.where(row == BLK - 2, nxt[0:1, :], jnp.where(row == BLK - 1, nxt[1:2, :], pltpu.roll(dy, BLK - 2, 0)))
        dycar[...] = dy[0:8, :]
        du = cw_ref[2:3, :] * dy + cw_ref[1:2, :] * dy1 + cw_ref[0:1, :] * dy2
        dp_ref[:, CONV_CH:2 * CONV_CH] = (du * hc).astype(BF16)
        dp_ref[:, 2 * CONV_CH:3 * CONV_CH] = (du * cg).astype(BF16)
        u = cg * hc
        dcw_ref[0:1, :] += jnp.sum(dy2 * u, axis=0, keepdims=True)
        dcw_ref[1:2, :] += jnp.sum(dy1 * u, axis=0, keepdims=True)
        dcw_ref[2:3, :] += jnp.sum(dy * u, axis=0, keepdims=True)

        aov = ao_ref[...].astype(F32)
        dma = dm_ref[:, CONV_CH:].astype(F32)
        ra = _rms(aov)
        ah = aov * ra
        daog_ref[...] += jnp.sum(dma * ah, axis=0, keepdims=True)
        dao_scr[...] = _rms_bwd(dma, ah, ra, aog_ref[...])

        valid = _band_mask(n == 0)
        lane8 = lax.broadcasted_iota(jnp.int32, (1, N_Q), 1)
        dsk = jnp.zeros((1, N_Q), F32)
        for h in range(N_KV):
            kb = jnp.concatenate([kvp_ref[:, HEAD * h:HEAD * (h + 1)],
                                  p_ref[:, O_K + HEAD * h:O_K + HEAD * (h + 1)]], axis=0).astype(F32)
            rk = _rms(kb)
            kh = kb * rk
            kn = (kh * kg_ref[...]).astype(BF16)
            vb = jnp.concatenate([kvp_ref[:, N_KV * HEAD + HEAD * h:N_KV * HEAD + HEAD * (h + 1)],
                                  p_ref[:, O_V + HEAD * h:O_V + HEAD * (h + 1)]], axis=0)
            q = jnp.concatenate([p_ref[:, O_Q + HEAD * (GRP * h + g):O_Q + HEAD * (GRP * h + g + 1)]
                                 for g in range(GRP)], axis=0).astype(F32)
            rq = _rms(q)
            qh = q * rq
            qn = (qh * qg_ref[...]).astype(BF16)
            s = jnp.where(valid, _dot_nt(qn, kn) * SCALE, NEG_INF)
            pn, psink = _softmax_with_sink(s, _sink_col(sinks_ref, h))
            do = jnp.concatenate([dao_scr[:, HEAD * (GRP * h + g):HEAD * (GRP * h + g + 1)]
                                  for g in range(GRP)], axis=0).astype(BF16)
            dv = _dot_tn(pn.astype(BF16), do)
            dpr = _dot_nt(do, vb)
            delta = jnp.sum(pn * dpr, axis=-1, keepdims=True)
            ds = ((pn * (dpr - delta)) * SCALE).astype(BF16)
            sk = psink * delta
            for g in range(GRP):
                dsk = dsk - jnp.where(lane8 == GRP * h + g, jnp.sum(sk[BLK * g:BLK * (g + 1), :]), 0.0)
            dqn = _dot(ds, kn)
            dkn = _dot_tn(ds, qn)
            dqg_ref[...] += jnp.sum(dqn * qh, axis=0, keepdims=True)
            dq = _rms_bwd(dqn, qh, rq, qg_ref[...]).astype(BF16)
            for g in range(GRP):
                hq = GRP * h + g
                dp_ref[:, O_Q + HEAD * hq:O_Q + HEAD * (hq + 1)] = dq[BLK * g:BLK * (g + 1), :]
            dkg_ref[...] += jnp.sum(dkn * kh, axis=0, keepdims=True)
            dk = _rms_bwd(dkn, kh, rk, kg_ref[...])
            kc, vc = slice(HEAD * h, HEAD * (h + 1)), slice(N_KV * HEAD + HEAD * h, N_KV * HEAD + HEAD * (h + 1))
            kvnow[:, kc] = kvcar[:, kc] + dk[BLK:, :]
            kvnow[:, vc] = kvcar[:, vc] + dv[BLK:, :]
            kvcar[:, kc] = dk[:BLK, :]
            kvcar[:, vc] = dv[:BLK, :]
        dp_ref[:, O_K:] = kvnow[...].astype(BF16)
        dsk_ref[...] += dsk

    small = lambda shape: pl.BlockSpec(shape, lambda s: (0, 0))
    blk = lambda w: pl.BlockSpec((BLK, w), lambda s: (nb - 1 - s, 0))
    return pl.pallas_call(
        body, name=name, grid=(nb,),
        in_specs=[pl.BlockSpec(memory_space=pltpu.SMEM),
                  blk(IN_COLS),
                  pl.BlockSpec((BLK, KV_W), lambda s: (jnp.maximum(nb - 2 - s, 0), O_K // KV_W)),
                  blk(CONV_CH), blk(ATTN_W), blk(D_MODEL),
                  small((3, CONV_CH)), small((1, HEAD)), small((1, HEAD)), small((1, CONV_CH)), small((1, ATTN_W))],
        out_specs=[blk(IN_COLS), small((3, CONV_CH)), small((1, HEAD)), small((1, HEAD)), small((1, N_Q)),
                   small((1, CONV_CH)), small((1, ATTN_W))],
        out_shape=[jax.ShapeDtypeStruct((S, IN_COLS), BF16), jax.ShapeDtypeStruct((3, CONV_CH), F32),
                   jax.ShapeDtypeStruct((1, HEAD), F32), jax.ShapeDtypeStruct((1, HEAD), F32),
                   jax.ShapeDtypeStruct((1, N_Q), F32), jax.ShapeDtypeStruct((1, CONV_CH), F32),
                   jax.ShapeDtypeStruct((1, ATTN_W), F32)],
        scratch_shapes=[pltpu.VMEM((8, CONV_CH), F32), pltpu.VMEM((BLK, KV_W), F32),
                        pltpu.VMEM((BLK, ATTN_W), F32), pltpu.VMEM((BLK, KV_W), F32)],
        compiler_params=_cparams(("arbitrary",)),
    )(sinks, proj, proj, y, ao, dmix, conv_w, qg, kg, cog, aog)


OTHER_CHIPS = ((1, 0), (0, 1), (1, 1))


def _position():
    return lax.axis_index("x"), lax.axis_index("y"), lax.axis_index("c")


def all_gather(shards, name):
    n = len(shards)

    def body(*refs):
        x_refs, out_refs = refs[:n], refs[n:2 * n]
        send_sems, recv_sems, local_sems = refs[2 * n:]
        x, y, c = _position()
        me, sibling = (x, y, c), (x, y, 1 - c)
        chips = [(x ^ mx, y ^ my) for mx, my in OTHER_CHIPS]

        def slab(a, px, py, pc):
            return out_refs[a].at[4 * px + 2 * py + pc]

        def copy(a, k, block, to, src=None):
            return pltpu.make_async_remote_copy(
                src_ref=slab(a, *block) if src is None else src, dst_ref=slab(a, *block),
                send_sem=send_sems.at[7 * a + k], recv_sem=recv_sems.at[7 * a + k], device_id=to, device_id_type=MESH_T)

        mine = [pltpu.make_async_copy(x_refs[a], slab(a, *me), local_sems.at[a]) for a in range(n)]
        for cp in mine:
            cp.start()
        first = []
        for a in range(n):
            first.append(copy(a, 0, me, sibling, src=x_refs[a]))
            first += [copy(a, 1 + j, me, (*chip, c), src=x_refs[a]) for j, chip in enumerate(chips)]
        for cp in first:
            cp.start()
        passed = []
        for a in range(n):
            for j, chip in enumerate(chips):
                copy(a, 1 + j, (*chip, c), me).wait_recv()
                passed.append(copy(a, 4 + j, (*chip, c), sibling))
                passed[-1].start()
        for a in range(n):
            copy(a, 0, sibling, me).wait_recv()
            for j, chip in enumerate(chips):
                copy(a, 4 + j, (*chip, 1 - c), me).wait_recv()
        for cp in first + passed:
            cp.wait_send()
        for cp in mine:
            cp.wait()

    return pl.pallas_call(
        body, name=name, in_specs=[ANY] * n, out_specs=[ANY] * n,
        out_shape=[jax.ShapeDtypeStruct((N_DEV, *s.shape), s.dtype) for s in shards],
        scratch_shapes=[pltpu.SemaphoreType.DMA((7 * n,)), pltpu.SemaphoreType.DMA((7 * n,)),
                        pltpu.SemaphoreType.DMA((n,))],
    )(*shards)


class SplitCopy:
    def __init__(self, name, arrays, n_copies, plan, after=None):
        n = len(arrays)
        self.name, self.n, self.n_copies, self.plan = name, n, n_copies, plan
        extra = [] if after is None else [after]

        def body(*refs):
            in_refs = refs[:n]
            send_sems, recv_sems = refs[n + len(extra)], refs[n + len(extra) + 1]
            token = refs[2 * n + len(extra) + 2]
            for k, (src, dst, to) in enumerate(plan(_position(), in_refs)):
                pltpu.make_async_remote_copy(src_ref=src, dst_ref=dst, send_sem=send_sems.at[k],
                                             recv_sem=recv_sems.at[k], device_id=to, device_id_type=MESH_T).start()
            token[...] = jnp.zeros_like(token)

        outs = pl.pallas_call(
            body, name=name + "_start",
            out_shape=(pltpu.SemaphoreType.DMA((n_copies,)), pltpu.SemaphoreType.DMA((n_copies,)),
                       *[pltpu.HBM(a.shape, a.dtype) for a in arrays], jax.ShapeDtypeStruct((8, 128), F32)),
            in_specs=[HBM] * n + [ANY] * len(extra),
            out_specs=(SEM, SEM, *[HBM] * n, pl.BlockSpec(memory_space=pltpu.VMEM)),
            input_output_aliases={i: 2 + i for i in range(n)},
            compiler_params=pltpu.CompilerParams(has_side_effects=pltpu.SideEffectType.DATAFLOW_SIDE_EFFECTING),
        )(*[pltpu.with_memory_space_constraint(a, pltpu.HBM) for a in arrays], *extra)
        self.send_sems, self.recv_sems = outs[0], outs[1]
        self.arrays, self.token = list(outs[2:2 + n]), outs[2 + n]

    def wait(self, after):
        n, plan = self.n, self.plan

        def body(*refs):
            in_refs, send_sems, recv_sems = refs[:n], refs[n], refs[n + 1]
            for k, (src, dst, to) in enumerate(plan(_position(), in_refs)):
                cp = pltpu.make_async_remote_copy(src_ref=src, dst_ref=dst, send_sem=send_sems.at[k],
                                                  recv_sem=recv_sems.at[k], device_id=to, device_id_type=MESH_T)
                cp.wait_send()
                cp.wait_recv()

        outs = pl.pallas_call(
            body, name=self.name + "_wait",
            out_shape=tuple(pltpu.HBM(a.shape, a.dtype) for a in self.arrays),
            in_specs=[HBM] * n + [SEM, SEM, ANY], out_specs=tuple([HBM] * n),
            input_output_aliases={i: i for i in range(n)},
            compiler_params=pltpu.CompilerParams(has_side_effects=pltpu.SideEffectType.DATAFLOW_SIDE_EFFECTING),
        )(*self.arrays, self.send_sems, self.recv_sems, after)
        return list(outs)


def gather_start(shards, me, name, after=None):
    n = len(shards)
    lands = [lax.dynamic_update_slice(lax.empty((N_DEV, *s.shape), s.dtype), s[None], (me, 0, 0)) for s in shards]

    def plan(pos, refs):
        x, y, c = pos
        return [(refs[a], refs[n + a].at[4 * x + 2 * y + c], (x ^ mx, y ^ my, c))
                for a in range(n) for mx, my in OTHER_CHIPS]

    return SplitCopy(name, list(shards) + lands, 3 * n, plan, after=after)


def sibling_start(lands, name):
    n = len(lands)

    def plan(pos, refs):
        x, y, c = pos
        return [(refs[a].at[2 * q + c], refs[a].at[2 * q + c], (x, y, 1 - c)) for a in range(n) for q in range(4)]

    return SplitCopy(name, list(lands), 4 * n, plan)


def scatter_start(slabs, name):
    n = len(slabs)
    lands = [lax.empty((N_DEV - 1, *g.shape[1:]), g.dtype) for g in slabs]

    def plan(pos, refs):
        x, y, c = pos
        copies = []
        for a in range(n):
            for r in range(1, N_DEV):
                px, py, pc = x ^ ((r >> 2) & 1), y ^ ((r >> 1) & 1), c ^ (r & 1)
                copies.append((refs[a].at[4 * px + 2 * py + pc], refs[n + a].at[r - 1], (px, py, pc)))
        return copies

    return SplitCopy(name, list(slabs) + lands, (N_DEV - 1) * n, plan)


def columns_from_slabs(g, name):
    n, R, c = g.shape
    tr = 256

    def body(g_ref, o_ref):
        for d in range(n):
            o_ref[:, c * d:c * (d + 1)] = g_ref[d]

    return pl.pallas_call(
        body, name=name, grid=(R // tr,),
        in_specs=[pl.BlockSpec((n, tr, c), lambda i: (0, i, 0))],
        out_specs=pl.BlockSpec((tr, n * c), lambda i: (i, 0)),
        out_shape=jax.ShapeDtypeStruct((R, n * c), g.dtype),
        compiler_params=_cparams(("parallel",)),
    )(g)


def all_reduce_small(v, name, dep=None):
    R, W = v.shape

    def body(v_ref, o_ref, recv, send_sems, recv_sems):
        x, y, c = _position()
        me = 4 * x + 2 * y + c
        copies = []
        for r in range(1, N_DEV):
            to = (x ^ ((r >> 2) & 1), y ^ ((r >> 1) & 1), c ^ (r & 1))
            copies.append(pltpu.make_async_remote_copy(
                src_ref=v_ref, dst_ref=recv.at[me], send_sem=send_sems.at[r - 1], recv_sem=recv_sems.at[r - 1],
                device_id=to, device_id_type=MESH_T))
        for cp in copies:
            cp.start()
        recv[pl.ds(me, 1)] = v_ref[...][None]
        for cp in copies:
            cp.wait()
        acc = recv[0]
        for s in range(1, N_DEV):
            acc = acc + recv[s]
        o_ref[...] = acc

    return _pallas_after(
        dep, body, v, name=name,
        in_specs=[pl.BlockSpec(memory_space=pltpu.VMEM)], out_specs=pl.BlockSpec(memory_space=pltpu.VMEM),
        out_shape=jax.ShapeDtypeStruct((R, W), F32),
        scratch_shapes=[pltpu.VMEM((N_DEV, R, W), F32), pltpu.SemaphoreType.DMA((N_DEV - 1,)),
                        pltpu.SemaphoreType.DMA((N_DEV - 1,))],
    )


def _row_tile(rows):
    if rows <= 512:
        return rows
    return max(t for t in range(8, 513, 8) if rows % t == 0)


def _adamw_update(w, g, m, v):
    mn = ADAM_B1 * m + (1.0 - ADAM_B1) * g
    vn = ADAM_B2 * v + (1.0 - ADAM_B2) * (g * g)
    m_hat = mn / (1.0 - ADAM_B1 ** ADAM_STEP)
    v_hat = vn / (1.0 - ADAM_B2 ** ADAM_STEP)
    return -ADAM_LR * (m_hat / (jnp.sqrt(v_hat) + ADAM_EPS) + ADAM_WD * w), mn, vn


def adamw(w, g, m, v, name):
    R, W = w.shape
    tr = _row_tile(R)

    def body(w_ref, g_ref, m_ref, v_ref, d_ref, mo_ref, vo_ref):
        d_ref[...], mo_ref[...], vo_ref[...] = _adamw_update(w_ref[...], g_ref[...], m_ref[...], v_ref[...])

    spec = pl.BlockSpec((tr, W), lambda i: (i, 0))
    return pl.pallas_call(
        body, name=name, grid=(R // tr,), in_specs=[spec] * 4, out_specs=[spec] * 3,
        out_shape=[jax.ShapeDtypeStruct((R, W), F32)] * 3,
        compiler_params=_cparams(("parallel",)),
    )(w, g, m, v)


def reduce_adamw(slabs, land, w, m, v, me_arr, name):
    _, R, W = slabs.shape
    tr = _row_tile(R)

    def body(me_ref, s_ref, l_ref, w_ref, m_ref, v_ref, g_ref, d_ref, mo_ref, vo_ref):
        g = s_ref[0].astype(F32)
        for r in range(N_DEV - 1):
            g = g + l_ref[r].astype(F32)
        g_ref[...] = g
        d_ref[...], mo_ref[...], vo_ref[...] = _adamw_update(w_ref[...], g, m_ref[...], v_ref[...])

    spec = pl.BlockSpec((tr, W), lambda i, me: (i, 0))
    return pl.pallas_call(
        body, name=name,
        grid_spec=pltpu.PrefetchScalarGridSpec(
            num_scalar_prefetch=1, grid=(R // tr,),
            in_specs=[pl.BlockSpec((1, tr, W), lambda i, me: (me[0], i, 0)),
                      pl.BlockSpec((N_DEV - 1, tr, W), lambda i, me: (0, i, 0)), spec, spec, spec],
            out_specs=[spec] * 4),
        out_shape=[jax.ShapeDtypeStruct((R, W), F32)] * 4,
        compiler_params=_cparams(("parallel",)),
    )(me_arr, slabs, land, w, m, v)


SMALL_NAMES = ("norm1_g", "q_norm_g", "k_norm_g", "sinks", "conv_out_g", "attn_out_g", "norm2_g", "conv_w")
SMALL_SIZES = (D_MODEL, HEAD, HEAD, N_Q, CONV_CH, ATTN_W, D_MODEL, 3 * CONV_CH)
SMALL_ROWS = 80


def kernel(x, norm1_g, w_in, conv_w, q_norm_g, k_norm_g, sinks, conv_out_g, attn_out_g, w_o, norm2_g, w_gate, w_up, w_down, loss_target, m_norm1_g, m_w_in, m_conv_w, m_q_norm_g, m_k_norm_g, m_sinks, m_conv_out_g, m_attn_out_g, m_w_o, m_norm2_g, m_w_gate, m_w_up, m_w_down, v_norm1_g, v_w_in, v_conv_w, v_q_norm_g, v_k_norm_g, v_sinks, v_conv_out_g, v_attn_out_g, v_w_o, v_norm2_g, v_w_gate, v_w_up, v_w_down):
    xi, yi, ci = _position()
    me = 4 * xi + 2 * yi + ci
    me_arr = jnp.reshape(me, (1,)).astype(jnp.int32)
    xs, tgt = x[0], loss_target[0]
    bf = lambda a: a.astype(BF16)

    g_in0, g_o0 = all_gather([bf(w_in[0]), bf(w_o[0])], "gather_first")
    ag_ffn0 = gather_start([bf(w_gate[0]), bf(w_up[0]), bf(w_down[0])], me, "gather_ffn0")
    ag_l1 = gather_start([bf(w_in[1]), bf(w_o[1]), bf(w_gate[1]), bf(w_up[1]), bf(w_down[1])], me, "gather_layer1",
                         after=ag_ffn0.token)
    conv_place = lax.dynamic_update_slice(jnp.zeros((DEPTH, 3, CONV_CH), F32), conv_w, (0, 0, me * HEAD))
    conv_full = all_reduce_small(conv_place.reshape(DEPTH * 3 * CONV_CH // 128, 128), "gather_conv_w",
                                 dep=ag_l1.token).reshape(DEPTH, 3, CONV_CH)
    small = [dict(norm1_g=norm1_g[l][None], conv_w=conv_full[l], q_norm_g=q_norm_g[l][None],
                  k_norm_g=k_norm_g[l][None], sinks=sinks[l], conv_out_g=conv_out_g[l][None],
                  attn_out_g=attn_out_g[l][None], norm2_g=norm2_g[l][None]) for l in range(DEPTH)]
    weights = [dict(w_in=columns_from_slabs(g_in0, "layout_w_in0"), w_o=g_o0.reshape(D_MODEL, D_MODEL)), {}]

    saved = []
    xl = xs
    for l in range(DEPTH):
        sp, wl = small[l], weights[l]
        h, proj = norm_proj(xl, sp["norm1_g"], wl["w_in"], f"norm_proj{l}", dep=ag_l1.token if l == 0 else None)
        mix, y, ao = mixer_fwd(proj, sp["sinks"], sp["conv_w"], sp["q_norm_g"], sp["k_norm_g"],
                               sp["conv_out_g"], sp["attn_out_g"], f"mixer_fwd{l}")
        dep = None
        if l == 0:
            pass_ffn0 = sibling_start(ag_ffn0.wait(mix)[3:], "pass_ffn0")
            dep = pass_ffn0.token
        x1 = matmul_residual(mix, wl["w_o"], xl, f"out_proj{l}", dep=dep)
        if l == 0:
            g_gate, g_up, g_down = pass_ffn0.wait(x1)
            wl.update(w_gate=columns_from_slabs(g_gate, "layout_w_gate0"),
                      w_up=columns_from_slabs(g_up, "layout_w_up0"), w_down=g_down.reshape(D_FF, D_MODEL))
            pass_l1 = sibling_start(ag_l1.wait(x1)[5:], "pass_layer1")
            dep = pass_l1.token
        h2, a, b, x2 = ffn_fwd(x1, sp["norm2_g"], wl["w_gate"], wl["w_up"], wl["w_down"], f"ffn_fwd{l}", dep=dep)
        if l == 0:
            g_in, g_o, g_gate, g_up, g_down = pass_l1.wait(x2)
            weights[1] = dict(w_in=columns_from_slabs(g_in, "layout_w_in1"), w_o=g_o.reshape(D_MODEL, D_MODEL),
                              w_gate=columns_from_slabs(g_gate, "layout_w_gate1"),
                              w_up=columns_from_slabs(g_up, "layout_w_up1"), w_down=g_down.reshape(D_FF, D_MODEL))
        saved.append((xl, h, proj, mix, y, ao, x1, h2, a, b))
        xl = x2
    loss_row, dx = loss_head(xl, tgt, "loss_head")
    loss = lax.psum(loss_row[0, 0], ("x", "y", "c"))

    shard = dict(w_in=(w_in, m_w_in, v_w_in), w_o=(w_o, m_w_o, v_w_o), w_gate=(w_gate, m_w_gate, v_w_gate),
                 w_up=(w_up, m_w_up, v_w_up), w_down=(w_down, m_w_down, v_w_down))
    stepped = {n: [None] * DEPTH for n in shard}
    gsmall = [None] * DEPTH

    def finish(sc, names, after, l):
        arrays = sc.wait(after)
        k = len(names)
        for i, n in enumerate(names):
            w, m, v = shard[n]
            stepped[n][l] = reduce_adamw(arrays[i], arrays[k + i], w[l], m[l], v[l], me_arr, f"reduce_adamw_{n}{l}")

    for l in reversed(range(DEPTH)):
        sp, wl = small[l], weights[l]
        x0, h, proj, mix, y, ao, x1, h2, a, b = saved[l]
        (d_wd,) = grad_weight((a, b), [dx], TF, D_MODEL, f"grad_w_down{l}", swiglu=True)
        sc_down = scatter_start([d_wd.reshape(N_DEV, ROWS_FF, D_MODEL)], f"scatter_w_down{l}")
        da, db, dx1, d_g2 = ffn_bwd(dx, a, b, x1, sp["norm2_g"], wl["w_gate"], wl["w_up"], wl["w_down"],
                                    f"ffn_bwd{l}", dep=sc_down.token)
        d_wg, d_wu = grad_weight((h2,), [da, db], D_MODEL, TF, f"grad_w_gate_up{l}", slab_cols=ROWS_FF)
        sc_gu = scatter_start([d_wg, d_wu], f"scatter_w_gate_up{l}")
        (d_wo,) = grad_weight((mix,), [dx1], D_MODEL, D_MODEL, f"grad_w_o{l}", dep=sc_gu.token)
        sc_o = scatter_start([d_wo.reshape(N_DEV, ROWS_O, D_MODEL)], f"scatter_w_o{l}")
        dmix = matmul_nt(dx1, wl["w_o"], f"out_proj_bwd{l}", dep=sc_o.token)
        finish(sc_down, ["w_down"], dmix, l)
        dproj, d_cw, d_qg, d_kg, d_sk, d_cog, d_aog = mixer_bwd(
            proj, y, ao, dmix, sp["sinks"], sp["conv_w"], sp["q_norm_g"], sp["k_norm_g"],
            sp["conv_out_g"], sp["attn_out_g"], f"mixer_bwd{l}")
        finish(sc_gu, ["w_gate", "w_up"], dproj, l)
        finish(sc_o, ["w_o"], dproj, l)
        (d_win,) = grad_weight((h,), [dproj], D_MODEL, IN_COLS // 2, f"grad_w_in{l}", slab_cols=ROWS_IN)
        sc_in = scatter_start([d_win], f"scatter_w_in{l}")
        dx, d_g1 = proj_bwd(dproj, wl["w_in"], x0, sp["norm1_g"], dx1, f"proj_bwd{l}", dep=sc_in.token)
        finish(sc_in, ["w_in"], dx, l)
        gsmall[l] = dict(norm1_g=d_g1, conv_w=d_cw, q_norm_g=d_qg, k_norm_g=d_kg, sinks=d_sk,
                         conv_out_g=d_cog, attn_out_g=d_aog, norm2_g=d_g2)
    grad_x = dx[None]

    flat = jnp.concatenate([gsmall[l][n].reshape(-1) for l in range(DEPTH) for n in SMALL_NAMES])
    flat = jnp.pad(flat, (0, SMALL_ROWS * 128 - flat.shape[0])).reshape(SMALL_ROWS, 128)
    flat = all_reduce_small(flat, "all_reduce_small_grads").reshape(-1)
    gs = {n: [] for n in SMALL_NAMES}
    off = 0
    for l in range(DEPTH):
        for n, size in zip(SMALL_NAMES, SMALL_SIZES):
            gs[n].append(flat[off:off + size])
            off += size
    gs = {n: jnp.stack(v) for n, v in gs.items()}
    g_conv = lax.dynamic_slice(gs["conv_w"].reshape(DEPTH, 3, CONV_CH), (0, 0, me * HEAD), (DEPTH, 3, HEAD))

    gs["conv_w"] = g_conv
    params = dict(norm1_g=(norm1_g, m_norm1_g, v_norm1_g), conv_w=(conv_w, m_conv_w, v_conv_w),
                  q_norm_g=(q_norm_g, m_q_norm_g, v_q_norm_g), k_norm_g=(k_norm_g, m_k_norm_g, v_k_norm_g),
                  sinks=(sinks, m_sinks, v_sinks), conv_out_g=(conv_out_g, m_conv_out_g, v_conv_out_g),
                  attn_out_g=(attn_out_g, m_attn_out_g, v_attn_out_g), norm2_g=(norm2_g, m_norm2_g, v_norm2_g))
    names = ("norm1_g", "w_in", "conv_w", "q_norm_g", "k_norm_g", "sinks", "conv_out_g", "attn_out_g", "w_o",
             "norm2_g", "w_gate", "w_up", "w_down")

    out = {}
    for n in names:
        if n in shard:
            out[n] = tuple(jnp.stack([stepped[n][l][i] for l in range(DEPTH)]) for i in range(4))
        else:
            w, m, v = params[n]
            two_d = (-1, w.shape[-1])
            d, mn, vn = adamw(w.reshape(two_d), gs[n].reshape(two_d), m.reshape(two_d), v.reshape(two_d), f"adamw_{n}")
            out[n] = (gs[n].reshape(w.shape), d.reshape(w.shape), mn.reshape(w.shape), vn.reshape(w.shape))
    return (loss, grad_x, *[out[n][i] for i in range(4) for n in names])
```

```python
import functools

import jax
import jax.numpy as jnp
from jax import lax
from jax.experimental import pallas as pl
from jax.experimental.pallas import tpu as pltpu

F32 = jnp.float32
BF16 = jnp.bfloat16

D_MODEL = 1024
CONV_CH = 512
ATTN_W = 512
N_Q = 8
N_KV = 2
GRP = N_Q // N_KV
HEAD = 64
IN_COLS = 2304
D_FF = 2816
BLK = 128
O_Q = 3 * CONV_CH
O_K = O_Q + ATTN_W
O_V = O_K + N_KV * HEAD
EPS = 1e-6
NEG_INF = -1e30
SCALE = HEAD ** -0.5
N_DEV = 8
DEPTH = 2

ADAM_LR = 0.001
ADAM_B1 = 0.9
ADAM_B2 = 0.999
ADAM_EPS = 1e-08
ADAM_WD = 0.01
ADAM_STEP = 10

VMEM_LIMIT = 56 * 1024 * 1024
TM = 512
MESH_T = pl.DeviceIdType.MESH

ROWS_IN, ROWS_O, ROWS_FF = IN_COLS // N_DEV, D_MODEL // N_DEV, D_FF // N_DEV


ANY = pl.BlockSpec(memory_space=pl.ANY)
HBM = pl.BlockSpec(memory_space=pltpu.HBM)
SEM = pl.BlockSpec(memory_space=pltpu.SEMAPHORE)


def _cparams(sem):
    return pltpu.CompilerParams(dimension_semantics=sem, vmem_limit_bytes=VMEM_LIMIT)


def _pallas_after(dep, body, *args, in_specs, **kw):
    if dep is None:
        return pl.pallas_call(body, in_specs=in_specs, **kw)(*args)

    def after_dep(dep_ref, *refs):
        body(*refs)

    return pl.pallas_call(after_dep, in_specs=[ANY, *in_specs], **kw)(dep, *args)


def _dot(a, b):
    return jnp.dot(a, b, preferred_element_type=F32)


def _dot_nt(a, b):
    return lax.dot_general(a, b, (((1,), (1,)), ((), ())), preferred_element_type=F32)


def _dot_tn(a, b):
    return lax.dot_general(a, b, (((0,), (0,)), ((), ())), preferred_element_type=F32)


def _rms(v):
    return lax.rsqrt(jnp.mean(v * v, axis=-1, keepdims=True) + EPS)


def _rms_bwd(dyv, xh, r, g):
    dxh = dyv * g
    return r * (dxh - xh * jnp.mean(dxh * xh, axis=-1, keepdims=True))


def norm_proj(x, g, w, name, dep=None):
    S, N = x.shape[0], w.shape[1]

    def body(x_ref, g_ref, w_ref, h_ref, p_ref):
        xf = x_ref[...]
        h = ((xf * _rms(xf)) * g_ref[...]).astype(BF16)
        h_ref[...] = h
        p_ref[...] = _dot(h, w_ref[...]).astype(BF16)

    return _pallas_after(
        dep, body, x, g, w, name=name, grid=(S // TM,),
        in_specs=[pl.BlockSpec((TM, D_MODEL), lambda i: (i, 0)),
                  pl.BlockSpec((1, D_MODEL), lambda i: (0, 0)),
                  pl.BlockSpec((D_MODEL, N), lambda i: (0, 0))],
        out_specs=[pl.BlockSpec((TM, D_MODEL), lambda i: (i, 0)),
                   pl.BlockSpec((TM, N), lambda i: (i, 0))],
        out_shape=[jax.ShapeDtypeStruct((S, D_MODEL), BF16), jax.ShapeDtypeStruct((S, N), BF16)],
        compiler_params=_cparams(("parallel",)),
    )


def band_bias():
    qi = lax.broadcasted_iota(jnp.int32, (BLK, 2 * BLK), 0)
    kj = lax.broadcasted_iota(jnp.int32, (BLK, 2 * BLK), 1)
    diff = qi + BLK - kj
    valid = (diff >= 0) & (diff < BLK)
    return jnp.stack([jnp.where(valid & (kj >= BLK), 0.0, NEG_INF), jnp.where(valid, 0.0, NEG_INF)]).astype(F32)


def _softmax_with_sink(s, sink):
    m = jnp.maximum(jnp.max(s, axis=-1, keepdims=True), sink)
    p = jnp.exp(s - m)
    es = jnp.exp(sink - m)
    inv = 1.0 / (jnp.sum(p, axis=-1, keepdims=True) + es)
    return p * inv, es * inv


def _kv_band(p_ref, kvp_ref, h):
    kb = jnp.concatenate([kvp_ref[:, HEAD * h:HEAD * (h + 1)],
                          p_ref[:, O_K + HEAD * h:O_K + HEAD * (h + 1)]], axis=0).astype(F32)
    vb = jnp.concatenate([kvp_ref[:, N_KV * HEAD + HEAD * h:N_KV * HEAD + HEAD * (h + 1)],
                          p_ref[:, O_V + HEAD * h:O_V + HEAD * (h + 1)]], axis=0)
    return kb, vb


def mixer_fwd(proj, bias, sinks, conv_w, qg, kg, cog, aog, name):
    S = proj.shape[0]
    nb = S // BLK

    def body(sinks_ref, p_ref, kvp_ref, bias_ref, cw_ref, qg_ref, kg_ref, cog_ref, aog_ref,
             mix_ref, y_ref, ao_ref, ucar, ao_scr):
        n = pl.program_id(0)

        @pl.when(n == 0)
        def _():
            ucar[...] = jnp.zeros_like(ucar)

        bg = p_ref[:, 0:CONV_CH].astype(F32)
        u = p_ref[:, CONV_CH:2 * CONV_CH].astype(F32) * p_ref[:, 2 * CONV_CH:3 * CONV_CH].astype(F32)
        row = lax.broadcasted_iota(jnp.int32, (BLK, CONV_CH), 0)
        prev = ucar[...]
        u1 = jnp.where(row == 0, prev[7:8, :], pltpu.roll(u, 1, 0))
        u2 = jnp.where(row == 0, prev[6:7, :], jnp.where(row == 1, prev[7:8, :], pltpu.roll(u, 2, 0)))
        ucar[...] = u[BLK - 8:BLK, :]
        y = cw_ref[0:1, :] * u2 + cw_ref[1:2, :] * u1 + cw_ref[2:3, :] * u
        y_ref[...] = y.astype(BF16)
        co = bg * y
        mix_ref[:, 0:CONV_CH] = ((co * _rms(co)) * cog_ref[...]).astype(BF16)

        q_gain = qg_ref[...] * SCALE
        for h in range(N_KV):
            kb, vb = _kv_band(p_ref, kvp_ref, h)
            kn = ((kb * _rms(kb)) * kg_ref[...]).astype(BF16)
            for hq in range(GRP * h, GRP * (h + 1)):
                q = p_ref[:, O_Q + HEAD * hq:O_Q + HEAD * (hq + 1)].astype(F32)
                qn = ((q * _rms(q)) * q_gain).astype(BF16)
                pn, _ = _softmax_with_sink(_dot_nt(qn, kn) + bias_ref[0], sinks_ref[hq])
                ao_scr[:, HEAD * hq:HEAD * (hq + 1)] = _dot(pn.astype(BF16), vb)
        ao = ao_scr[...]
        ao_ref[...] = ao.astype(BF16)
        mix_ref[:, CONV_CH:] = ((ao * _rms(ao)) * aog_ref[...]).astype(BF16)

    small = lambda shape: pl.BlockSpec(shape, lambda n: (0, 0))
    return pl.pallas_call(
        body, name=name, grid=(nb,),
        in_specs=[pl.BlockSpec(memory_space=pltpu.SMEM),
                  pl.BlockSpec((BLK, IN_COLS), lambda n: (n, 0)),
                  pl.BlockSpec((BLK, 2 * N_KV * HEAD), lambda n: (jnp.maximum(n - 1, 0), O_K // (2 * N_KV * HEAD))),
                  pl.BlockSpec((1, BLK, 2 * BLK), lambda n: (jnp.minimum(n, 1), 0, 0)),
                  small((3, CONV_CH)), small((1, HEAD)), small((1, HEAD)), small((1, CONV_CH)), small((1, ATTN_W))],
        out_specs=[pl.BlockSpec((BLK, D_MODEL), lambda n: (n, 0)),
                   pl.BlockSpec((BLK, CONV_CH), lambda n: (n, 0)),
                   pl.BlockSpec((BLK, ATTN_W), lambda n: (n, 0))],
        out_shape=[jax.ShapeDtypeStruct((S, D_MODEL), BF16), jax.ShapeDtypeStruct((S, CONV_CH), BF16),
                   jax.ShapeDtypeStruct((S, ATTN_W), BF16)],
        scratch_shapes=[pltpu.VMEM((8, CONV_CH), F32), pltpu.VMEM((BLK, ATTN_W), F32)],
        compiler_params=_cparams(("arbitrary",)),
    )(sinks, proj, proj, bias, conv_w, qg, kg, cog, aog)


def matmul_residual(a, w, res, name, dep=None):
    S, K = a.shape
    N = w.shape[1]

    def body(a_ref, w_ref, r_ref, o_ref):
        o_ref[...] = r_ref[...] + _dot(a_ref[...], w_ref[...])

    return _pallas_after(
        dep, body, a, w, res, name=name, grid=(S // TM,),
        in_specs=[pl.BlockSpec((TM, K), lambda i: (i, 0)), pl.BlockSpec((K, N), lambda i: (0, 0)),
                  pl.BlockSpec((TM, N), lambda i: (i, 0))],
        out_specs=pl.BlockSpec((TM, N), lambda i: (i, 0)),
        out_shape=jax.ShapeDtypeStruct((S, N), F32),
        compiler_params=_cparams(("parallel",)),
    )


TF = 1408


def ffn_fwd(x1, g, wg, wu, wd, name, dep=None):
    S = x1.shape[0]
    nj = D_FF // TF

    def body(x_ref, g_ref, wg_ref, wu_ref, wd_ref, h2_ref, a_ref, b_ref, o_ref, acc):
        j = pl.program_id(1)

        @pl.when(j == 0)
        def _():
            xf = x_ref[...]
            h2_ref[...] = ((xf * _rms(xf)) * g_ref[...]).astype(BF16)
            acc[...] = xf

        h2 = h2_ref[...]
        a = _dot(h2, wg_ref[...])
        b = _dot(h2, wu_ref[...])
        a_ref[...] = a.astype(BF16)
        b_ref[...] = b.astype(BF16)
        f = (a * (1.0 / (1.0 + jnp.exp(-a)))) * b
        acc[...] += _dot(f.astype(BF16), wd_ref[...])

        @pl.when(j == nj - 1)
        def _():
            o_ref[...] = acc[...]

    return _pallas_after(
        dep, body, x1, g, wg, wu, wd, name=name, grid=(S // TM, nj),
        in_specs=[pl.BlockSpec((TM, D_MODEL), lambda i, j: (i, 0)), pl.BlockSpec((1, D_MODEL), lambda i, j: (0, 0)),
                  pl.BlockSpec((D_MODEL, TF), lambda i, j: (0, j)), pl.BlockSpec((D_MODEL, TF), lambda i, j: (0, j)),
                  pl.BlockSpec((TF, D_MODEL), lambda i, j: (j, 0))],
        out_specs=[pl.BlockSpec((TM, D_MODEL), lambda i, j: (i, 0)), pl.BlockSpec((TM, TF), lambda i, j: (i, j)),
                   pl.BlockSpec((TM, TF), lambda i, j: (i, j)), pl.BlockSpec((TM, D_MODEL), lambda i, j: (i, 0))],
        out_shape=[jax.ShapeDtypeStruct((S, D_MODEL), BF16), jax.ShapeDtypeStruct((S, D_FF), BF16),
                   jax.ShapeDtypeStruct((S, D_FF), BF16), jax.ShapeDtypeStruct((S, D_MODEL), F32)],
        scratch_shapes=[pltpu.VMEM((TM, D_MODEL), F32)],
        compiler_params=_cparams(("parallel", "arbitrary")),
    )


def loss_head(yv, tgt, name):
    S = yv.shape[0]
    ni = S // TM

    def body(y_ref, t_ref, l_ref, d_ref, acc):
        i = pl.program_id(0)

        @pl.when(i == 0)
        def _():
            acc[...] = jnp.zeros_like(acc)

        e = y_ref[...] - t_ref[...]
        d_ref[...] = e * (1.0 / D_MODEL)
        acc[...] += jnp.sum(e * e, axis=0, keepdims=True)

        @pl.when(i == ni - 1)
        def _():
            l_ref[...] = jnp.full((1, 128), jnp.sum(acc[...]) * (0.5 / D_MODEL), F32)

    return pl.pallas_call(
        body, name=name, grid=(ni,),
        in_specs=[pl.BlockSpec((TM, D_MODEL), lambda i: (i, 0)), pl.BlockSpec((TM, D_MODEL), lambda i: (i, 0))],
        out_specs=[pl.BlockSpec((1, 128), lambda i: (0, 0)), pl.BlockSpec((TM, D_MODEL), lambda i: (i, 0))],
        out_shape=[jax.ShapeDtypeStruct((1, 128), F32), jax.ShapeDtypeStruct((S, D_MODEL), F32)],
        scratch_shapes=[pltpu.VMEM((1, D_MODEL), F32)],
        compiler_params=_cparams(("arbitrary",)),
    )(yv, tgt)


def ffn_bwd(dx2, a, b, x1, g, wg, wu, wd, name, dep=None):
    S = dx2.shape[0]
    nj = D_FF // TF

    def body(dx_ref, a_ref, b_ref, x_ref, g_ref, wg_ref, wu_ref, wd_ref, da_ref, db_ref, dx1_ref, dg_ref, dxb, acc):
        i, j = pl.program_id(0), pl.program_id(1)

        @pl.when((i == 0) & (j == 0))
        def _():
            dg_ref[...] = jnp.zeros_like(dg_ref)

        @pl.when(j == 0)
        def _():
            dxb[...] = dx_ref[...].astype(BF16)
            acc[...] = jnp.zeros_like(acc)

        df = _dot_nt(dxb[...], wd_ref[...])
        av = a_ref[...].astype(F32)
        bv = b_ref[...].astype(F32)
        sg = 1.0 / (1.0 + jnp.exp(-av))
        da = ((df * bv) * (sg * (1.0 + av * (1.0 - sg)))).astype(BF16)
        db = (df * (av * sg)).astype(BF16)
        da_ref[...] = da
        db_ref[...] = db
        acc[...] += _dot_nt(da, wg_ref[...]) + _dot_nt(db, wu_ref[...])

        @pl.when(j == nj - 1)
        def _():
            xf = x_ref[...]
            r = _rms(xf)
            xh = xf * r
            dh = acc[...]
            dg_ref[...] += jnp.sum(dh * xh, axis=0, keepdims=True)
            dx1_ref[...] = dx_ref[...] + _rms_bwd(dh, xh, r, g_ref[...])

    return _pallas_after(
        dep, body, dx2, a, b, x1, g, wg, wu, wd, name=name, grid=(S // TM, nj),
        in_specs=[pl.BlockSpec((TM, D_MODEL), lambda i, j: (i, 0)),
                  pl.BlockSpec((TM, TF), lambda i, j: (i, j)), pl.BlockSpec((TM, TF), lambda i, j: (i, j)),
                  pl.BlockSpec((TM, D_MODEL), lambda i, j: (i, 0)), pl.BlockSpec((1, D_MODEL), lambda i, j: (0, 0)),
                  pl.BlockSpec((D_MODEL, TF), lambda i, j: (0, j)), pl.BlockSpec((D_MODEL, TF), lambda i, j: (0, j)),
                  pl.BlockSpec((TF, D_MODEL), lambda i, j: (j, 0))],
        out_specs=[pl.BlockSpec((TM, TF), lambda i, j: (i, j)), pl.BlockSpec((TM, TF), lambda i, j: (i, j)),
                   pl.BlockSpec((TM, D_MODEL), lambda i, j: (i, 0)), pl.BlockSpec((1, D_MODEL), lambda i, j: (0, 0))],
        out_shape=[jax.ShapeDtypeStruct((S, D_FF), BF16), jax.ShapeDtypeStruct((S, D_FF), BF16),
                   jax.ShapeDtypeStruct((S, D_MODEL), F32), jax.ShapeDtypeStruct((1, D_MODEL), F32)],
        scratch_shapes=[pltpu.VMEM((TM, D_MODEL), BF16), pltpu.VMEM((TM, D_MODEL), F32)],
        compiler_params=_cparams(("arbitrary", "arbitrary")),
    )


def grad_weight(lhs, rhs, tm, tn, name, swiglu=False, slab_cols=None, dep=None):
    S = rhs[0].shape[0]
    M = lhs[0].shape[1]
    N = rhs[0].shape[1]
    tk = 512
    nk = S // tk
    n_l, n_r = len(lhs), len(rhs)
    if slab_cols is None:
        out_spec = pl.BlockSpec((tm, tn), lambda i, j, k: (i, j))
        out_shape = jax.ShapeDtypeStruct((M, N), BF16)
    else:
        out_spec = pl.BlockSpec((tn // slab_cols, tm, slab_cols), lambda i, j, k: (j, i, 0))
        out_shape = jax.ShapeDtypeStruct((N // slab_cols, M, slab_cols), BF16)

    def body(*refs):
        l_refs, r_refs = refs[:n_l], refs[n_l:n_l + n_r]
        o_refs, accs = refs[n_l + n_r:n_l + 2 * n_r], refs[n_l + 2 * n_r:]
        k = pl.program_id(2)
        if swiglu:
            av = l_refs[0][...].astype(F32)
            lv = ((av * (1.0 / (1.0 + jnp.exp(-av)))) * l_refs[1][...].astype(F32)).astype(BF16)
        else:
            lv = l_refs[0][...]
        for r_ref, o_ref, acc in zip(r_refs, o_refs, accs):
            part = _dot_tn(lv, r_ref[...].astype(BF16))

            @pl.when(k == 0)
            def _():
                acc[...] = part

            @pl.when(k > 0)
            def _():
                acc[...] += part

            @pl.when(k == nk - 1)
            def _():
                if slab_cols is None:
                    o_ref[...] = acc[...].astype(BF16)
                else:
                    for s in range(tn // slab_cols):
                        o_ref[s] = acc[:, slab_cols * s:slab_cols * (s + 1)].astype(BF16)

    return _pallas_after(
        dep, body, *lhs, *rhs, name=name, grid=(M // tm, N // tn, nk),
        in_specs=[pl.BlockSpec((tk, tm), lambda i, j, k: (k, i)) for _ in lhs]
        + [pl.BlockSpec((tk, tn), lambda i, j, k: (k, j)) for _ in rhs],
        out_specs=[out_spec for _ in rhs],
        out_shape=[out_shape for _ in rhs],
        scratch_shapes=[pltpu.VMEM((tm, tn), F32) for _ in rhs],
        compiler_params=_cparams(("parallel", "parallel", "arbitrary")),
    )


def matmul_nt(a, w, name, dep=None):
    S, K = a.shape
    N = w.shape[0]

    def body(a_ref, w_ref, o_ref):
        o_ref[...] = _dot_nt(a_ref[...].astype(BF16), w_ref[...]).astype(BF16)

    return _pallas_after(
        dep, body, a, w, name=name, grid=(S // TM,),
        in_specs=[pl.BlockSpec((TM, K), lambda i: (i, 0)), pl.BlockSpec((N, K), lambda i: (0, 0))],
        out_specs=pl.BlockSpec((TM, N), lambda i: (i, 0)),
        out_shape=jax.ShapeDtypeStruct((S, N), BF16),
        compiler_params=_cparams(("parallel",)),
    )


def proj_bwd(dproj, w, x, g, dres, name, dep=None):
    S, N = dproj.shape

    def body(dp_ref, w_ref, x_ref, g_ref, dr_ref, dx_ref, dg_ref):
        i = pl.program_id(0)

        @pl.when(i == 0)
        def _():
            dg_ref[...] = jnp.zeros_like(dg_ref)

        dh = _dot_nt(dp_ref[...], w_ref[...])
        xf = x_ref[...]
        r = _rms(xf)
        xh = xf * r
        dg_ref[...] += jnp.sum(dh * xh, axis=0, keepdims=True)
        dx_ref[...] = dr_ref[...] + _rms_bwd(dh, xh, r, g_ref[...])

    return _pallas_after(
        dep, body, dproj, w, x, g, dres, name=name, grid=(S // TM,),
        in_specs=[pl.BlockSpec((TM, N), lambda i: (i, 0)), pl.BlockSpec((D_MODEL, N), lambda i: (0, 0)),
                  pl.BlockSpec((TM, D_MODEL), lambda i: (i, 0)), pl.BlockSpec((1, D_MODEL), lambda i: (0, 0)),
                  pl.BlockSpec((TM, D_MODEL), lambda i: (i, 0))],
        out_specs=[pl.BlockSpec((TM, D_MODEL), lambda i: (i, 0)), pl.BlockSpec((1, D_MODEL), lambda i: (0, 0))],
        out_shape=[jax.ShapeDtypeStruct((S, D_MODEL), F32), jax.ShapeDtypeStruct((1, D_MODEL), F32)],
        compiler_params=_cparams(("arbitrary",)),
    )


def mixer_bwd(proj, bias, y, ao, dmix, sinks, conv_w, qg, kg, cog, aog, name):
    S = proj.shape[0]
    nb = S // BLK
    KV_W = 2 * N_KV * HEAD

    def body(sinks_ref, p_ref, kvp_ref, bias_ref, y_ref, ao_ref, dm_ref, cw_ref, qg_ref, kg_ref, cog_ref, aog_ref,
             dp_ref, dcw_ref, dqg_ref, dkg_ref, dsk_ref, dcog_ref, daog_ref,
             dycar, kvcar, dao_scr, kvnow):
        step = pl.program_id(0)

        @pl.when(step == 0)
        def _():
            dycar[...] = jnp.zeros_like(dycar)
            kvcar[...] = jnp.zeros_like(kvcar)
            dcw_ref[...] = jnp.zeros_like(dcw_ref)
            dqg_ref[...] = jnp.zeros_like(dqg_ref)
            dkg_ref[...] = jnp.zeros_like(dkg_ref)
            dsk_ref[...] = jnp.zeros_like(dsk_ref)
            dcog_ref[...] = jnp.zeros_like(dcog_ref)
            daog_ref[...] = jnp.zeros_like(daog_ref)

        bg = p_ref[:, 0:CONV_CH].astype(F32)
        cg = p_ref[:, CONV_CH:2 * CONV_CH].astype(F32)
        hc = p_ref[:, 2 * CONV_CH:3 * CONV_CH].astype(F32)
        yv = y_ref[...].astype(F32)
        dmc = dm_ref[:, 0:CONV_CH].astype(F32)
        co = bg * yv
        rc = _rms(co)
        ch = co * rc
        dcog_ref[...] += jnp.sum(dmc * ch, axis=0, keepdims=True)
        dco = _rms_bwd(dmc, ch, rc, cog_ref[...])
        dp_ref[:, 0:CONV_CH] = (dco * yv).astype(BF16)
        dy = dco * bg
        row = lax.broadcasted_iota(jnp.int32, (BLK, CONV_CH), 0)
        nxt = dycar[...]
        dy1 = jnp.where(row == BLK - 1, nxt[0:1, :], pltpu.roll(dy, BLK - 1, 0))
        dy2 = jnp.where(row == BLK - 2, nxt[0:1, :], jnp.where(row == BLK - 1, nxt[1:2, :], pltpu.roll(dy, BLK - 2, 0)))
        dycar[...] = dy[0:8, :]
        du = cw_ref[2:3, :] * dy + cw_ref[1:2, :] * dy1 + cw_ref[0:1, :] * dy2
        dp_ref[:, CONV_CH:2 * CONV_CH] = (du * hc).astype(BF16)
        dp_ref[:, 2 * CONV_CH:3 * CONV_CH] = (du * cg).astype(BF16)
        u = cg * hc
        dcw_ref[0:1, :] += jnp.sum(dy2 * u, axis=0, keepdims=True)
        dcw_ref[1:2, :] += jnp.sum(dy1 * u, axis=0, keepdims=True)
        dcw_ref[2:3, :] += jnp.sum(dy * u, axis=0, keepdims=True)

        aov = ao_ref[...].astype(F32)
        dma = dm_ref[:, CONV_CH:].astype(F32)
        ra = _rms(aov)
        ah = aov * ra
        daog_ref[...] += jnp.sum(dma * ah, axis=0, keepdims=True)
        dao_scr[...] = _rms_bwd(dma, ah, ra, aog_ref[...])

        q_gain = qg_ref[...] * SCALE
        lane8 = lax.broadcasted_iota(jnp.int32, (1, N_Q), 1)
        dsk = jnp.zeros((1, N_Q), F32)
        dqg = jnp.zeros((1, HEAD), F32)
        for h in range(N_KV):
            kb, vb = _kv_band(p_ref, kvp_ref, h)
            rk = _rms(kb)
            kh = kb * rk
            kn = (kh * kg_ref[...]).astype(BF16)
            dv = jnp.zeros((2 * BLK, HEAD), F32)
            dkn = jnp.zeros((2 * BLK, HEAD), F32)
            for hq in range(GRP * h, GRP * (h + 1)):
                cols = slice(HEAD * hq, HEAD * (hq + 1))
                q = p_ref[:, O_Q + HEAD * hq:O_Q + HEAD * (hq + 1)].astype(F32)
                rq = _rms(q)
                qh = q * rq
                qn = (qh * q_gain).astype(BF16)
                pn, psink = _softmax_with_sink(_dot_nt(qn, kn) + bias_ref[0], sinks_ref[hq])
                dof = dao_scr[:, cols]
                do = dof.astype(BF16)
                dv = dv + _dot_tn(pn.astype(BF16), do)
                dpr = _dot_nt(do, vb)
                delta = jnp.sum(dof * ao_ref[:, cols].astype(F32), axis=-1, keepdims=True)
                ds = (pn * (dpr - delta)).astype(BF16)
                dsk = dsk - jnp.where(lane8 == hq, jnp.sum(psink * delta), 0.0)
                dqn = _dot(ds, kn)
                dkn = dkn + _dot_tn(ds, qn)
                dqg = dqg + jnp.sum(dqn * qh, axis=0, keepdims=True)
                dp_ref[:, O_Q + HEAD * hq:O_Q + HEAD * (hq + 1)] = _rms_bwd(dqn, qh, rq, q_gain).astype(BF16)
            dkg_ref[...] += jnp.sum(dkn * kh, axis=0, keepdims=True)
            dk = _rms_bwd(dkn, kh, rk, kg_ref[...])
            kc, vc = slice(HEAD * h, HEAD * (h + 1)), slice(N_KV * HEAD + HEAD * h, N_KV * HEAD + HEAD * (h + 1))
            kvnow[:, kc] = kvcar[:, kc] + dk[BLK:, :]
            kvnow[:, vc] = kvcar[:, vc] + dv[BLK:, :]
            kvcar[:, kc] = dk[:BLK, :]
            kvcar[:, vc] = dv[:BLK, :]
        dp_ref[:, O_K:] = kvnow[...].astype(BF16)
        dsk_ref[...] += dsk
        dqg_ref[...] += dqg * SCALE

    small = lambda shape: pl.BlockSpec(shape, lambda s: (0, 0))
    blk = lambda w: pl.BlockSpec((BLK, w), lambda s: (nb - 1 - s, 0))
    return pl.pallas_call(
        body, name=name, grid=(nb,),
        in_specs=[pl.BlockSpec(memory_space=pltpu.SMEM),
                  blk(IN_COLS),
                  pl.BlockSpec((BLK, KV_W), lambda s: (jnp.maximum(nb - 2 - s, 0), O_K // KV_W)),
                  pl.BlockSpec((1, BLK, 2 * BLK), lambda s: (jnp.minimum(nb - 1 - s, 1), 0, 0)),
                  blk(CONV_CH), blk(ATTN_W), blk(D_MODEL),
                  small((3, CONV_CH)), small((1, HEAD)), small((1, HEAD)), small((1, CONV_CH)), small((1, ATTN_W))],
        out_specs=[blk(IN_COLS), small((3, CONV_CH)), small((1, HEAD)), small((1, HEAD)), small((1, N_Q)),
                   small((1, CONV_CH)), small((1, ATTN_W))],
        out_shape=[jax.ShapeDtypeStruct((S, IN_COLS), BF16), jax.ShapeDtypeStruct((3, CONV_CH), F32),
                   jax.ShapeDtypeStruct((1, HEAD), F32), jax.ShapeDtypeStruct((1, HEAD), F32),
                   jax.ShapeDtypeStruct((1, N_Q), F32), jax.ShapeDtypeStruct((1, CONV_CH), F32),
                   jax.ShapeDtypeStruct((1, ATTN_W), F32)],
        scratch_shapes=[pltpu.VMEM((8, CONV_CH), F32), pltpu.VMEM((BLK, KV_W), F32),
                        pltpu.VMEM((BLK, ATTN_W), F32), pltpu.VMEM((BLK, KV_W), F32)],
        compiler_params=_cparams(("arbitrary",)),
    )(sinks, proj, proj, bias, y, ao, dmix, conv_w, qg, kg, cog, aog)


OTHER_CHIPS = ((1, 0), (0, 1), (1, 1))


def _position():
    return lax.axis_index("x"), lax.axis_index("y"), lax.axis_index("c")


def all_gather(shards, name):
    n = len(shards)

    def body(*refs):
        x_refs, out_refs = refs[:n], refs[n:2 * n]
        send_sems, recv_sems, local_sems = refs[2 * n:]
        x, y, c = _position()
        me, sibling = (x, y, c), (x, y, 1 - c)
        chips = [(x ^ mx, y ^ my) for mx, my in OTHER_CHIPS]

        def slab(a, px, py, pc):
            return out_refs[a].at[4 * px + 2 * py + pc]

        def copy(a, k, block, to, src=None):
            return pltpu.make_async_remote_copy(
                src_ref=slab(a, *block) if src is None else src, dst_ref=slab(a, *block),
                send_sem=send_sems.at[7 * a + k], recv_sem=recv_sems.at[7 * a + k], device_id=to, device_id_type=MESH_T)

        mine = [pltpu.make_async_copy(x_refs[a], slab(a, *me), local_sems.at[a]) for a in range(n)]
        for cp in mine:
            cp.start()
        first = []
        for a in range(n):
            first.append(copy(a, 0, me, sibling, src=x_refs[a]))
            first += [copy(a, 1 + j, me, (*chip, c), src=x_refs[a]) for j, chip in enumerate(chips)]
        for cp in first:
            cp.start()
        passed = []
        for a in range(n):
            for j, chip in enumerate(chips):
                copy(a, 1 + j, (*chip, c), me).wait_recv()
                passed.append(copy(a, 4 + j, (*chip, c), sibling))
                passed[-1].start()
        for a in range(n):
            copy(a, 0, sibling, me).wait_recv()
            for j, chip in enumerate(chips):
                copy(a, 4 + j, (*chip, 1 - c), me).wait_recv()
        for cp in first + passed:
            cp.wait_send()
        for cp in mine:
            cp.wait()

    return pl.pallas_call(
        body, name=name, in_specs=[ANY] * n, out_specs=[ANY] * n,
        out_shape=[jax.ShapeDtypeStruct((N_DEV, *s.shape), s.dtype) for s in shards],
        scratch_shapes=[pltpu.SemaphoreType.DMA((7 * n,)), pltpu.SemaphoreType.DMA((7 * n,)),
                        pltpu.SemaphoreType.DMA((n,))],
    )(*shards)


class SplitCopy:
    def __init__(self, name, arrays, n_copies, plan, after=None):
        n = len(arrays)
        self.name, self.n, self.n_copies, self.plan = name, n, n_copies, plan
        extra = [] if after is None else [after]

        def body(*refs):
            in_refs = refs[:n]
            send_sems, recv_sems = refs[n + len(extra)], refs[n + len(extra) + 1]
            token = refs[2 * n + len(extra) + 2]
            for k, (src, dst, to) in enumerate(plan(_position(), in_refs)):
                pltpu.make_async_remote_copy(src_ref=src, dst_ref=dst, send_sem=send_sems.at[k],
                                             recv_sem=recv_sems.at[k], device_id=to, device_id_type=MESH_T).start()
            token[...] = jnp.zeros_like(token)

        outs = pl.pallas_call(
            body, name=name + "_start",
            out_shape=(pltpu.SemaphoreType.DMA((n_copies,)), pltpu.SemaphoreType.DMA((n_copies,)),
                       *[pltpu.HBM(a.shape, a.dtype) for a in arrays], jax.ShapeDtypeStruct((8, 128), F32)),
            in_specs=[HBM] * n + [ANY] * len(extra),
            out_specs=(SEM, SEM, *[HBM] * n, pl.BlockSpec(memory_space=pltpu.VMEM)),
            input_output_aliases={i: 2 + i for i in range(n)},
            compiler_params=pltpu.CompilerParams(has_side_effects=pltpu.SideEffectType.DATAFLOW_SIDE_EFFECTING),
        )(*[pltpu.with_memory_space_constraint(a, pltpu.HBM) for a in arrays], *extra)
        self.send_sems, self.recv_sems = outs[0], outs[1]
        self.arrays, self.token = list(outs[2:2 + n]), outs[2 + n]

    def wait(self, after):
        n, plan = self.n, self.plan

        def body(*refs):
            in_refs, send_sems, recv_sems = refs[:n], refs[n], refs[n + 1]
            for k, (src, dst, to) in enumerate(plan(_position(), in_refs)):
                cp = pltpu.make_async_remote_copy(src_ref=src, dst_ref=dst, send_sem=send_sems.at[k],
                                                  recv_sem=recv_sems.at[k], device_id=to, device_id_type=MESH_T)
                cp.wait_send()
                cp.wait_recv()

        outs = pl.pallas_call(
            body, name=self.name + "_wait",
            out_shape=tuple(pltpu.HBM(a.shape, a.dtype) for a in self.arrays),
            in_specs=[HBM] * n + [SEM, SEM, ANY], out_specs=tuple([HBM] * n),
            input_output_aliases={i: i for i in range(n)},
            compiler_params=pltpu.CompilerParams(has_side_effects=pltpu.SideEffectType.DATAFLOW_SIDE_EFFECTING),
        )(*self.arrays, self.send_sems, self.recv_sems, after)
        return list(outs)


def gather_start(shards, me, name, after=None):
    n = len(shards)
    lands = [lax.dynamic_update_slice(lax.empty((N_DEV, *s.shape), s.dtype), s[None], (me, 0, 0)) for s in shards]

    def plan(pos, refs):
        x, y, c = pos
        return [(refs[a], refs[n + a].at[4 * x + 2 * y + c], (x ^ mx, y ^ my, c))
                for a in range(n) for mx, my in OTHER_CHIPS]

    return SplitCopy(name, list(shards) + lands, 3 * n, plan, after=after)


def sibling_start(lands, name):
    n = len(lands)

    def plan(pos, refs):
        x, y, c = pos
        return [(refs[a].at[2 * q + c], refs[a].at[2 * q + c], (x, y, 1 - c)) for a in range(n) for q in range(4)]

    return SplitCopy(name, list(lands), 4 * n, plan)


def scatter_start(slabs, name):
    n = len(slabs)
    lands = [lax.empty((N_DEV - 1, *g.shape[1:]), g.dtype) for g in slabs]

    def plan(pos, refs):
        x, y, c = pos
        copies = []
        for a in range(n):
            for r in range(1, N_DEV):
                px, py, pc = x ^ ((r >> 2) & 1), y ^ ((r >> 1) & 1), c ^ (r & 1)
                copies.append((refs[a].at[4 * px + 2 * py + pc], refs[n + a].at[r - 1], (px, py, pc)))
        return copies

    return SplitCopy(name, list(slabs) + lands, (N_DEV - 1) * n, plan)


def columns_from_slabs(g, name):
    n, R, c = g.shape
    tr = 256

    def body(g_ref, o_ref):
        for d in range(n):
            o_ref[:, c * d:c * (d + 1)] = g_ref[d]

    return pl.pallas_call(
        body, name=name, grid=(R // tr,),
        in_specs=[pl.BlockSpec((n, tr, c), lambda i: (0, i, 0))],
        out_specs=pl.BlockSpec((tr, n * c), lambda i: (i, 0)),
        out_shape=jax.ShapeDtypeStruct((R, n * c), g.dtype),
        compiler_params=_cparams(("parallel",)),
    )(g)


def all_reduce_small(v, name, dep=None):
    R, W = v.shape

    def body(v_ref, o_ref, recv, send_sems, recv_sems):
        x, y, c = _position()
        me = 4 * x + 2 * y + c
        copies = []
        for r in range(1, N_DEV):
            to = (x ^ ((r >> 2) & 1), y ^ ((r >> 1) & 1), c ^ (r & 1))
            copies.append(pltpu.make_async_remote_copy(
                src_ref=v_ref, dst_ref=recv.at[me], send_sem=send_sems.at[r - 1], recv_sem=recv_sems.at[r - 1],
                device_id=to, device_id_type=MESH_T))
        for cp in copies:
            cp.start()
        recv[pl.ds(me, 1)] = v_ref[...][None]
        for cp in copies:
            cp.wait()
        acc = recv[0]
        for s in range(1, N_DEV):
            acc = acc + recv[s]
        o_ref[...] = acc

    return _pallas_after(
        dep, body, v, name=name,
        in_specs=[pl.BlockSpec(memory_space=pltpu.VMEM)], out_specs=pl.BlockSpec(memory_space=pltpu.VMEM),
        out_shape=jax.ShapeDtypeStruct((R, W), F32),
        scratch_shapes=[pltpu.VMEM((N_DEV, R, W), F32), pltpu.SemaphoreType.DMA((N_DEV - 1,)),
                        pltpu.SemaphoreType.DMA((N_DEV - 1,))],
    )


def _row_tile(rows):
    if rows <= 512:
        return rows
    return max(t for t in range(8, 513, 8) if rows % t == 0)


def _adamw_update(w, g, m, v):
    mn = ADAM_B1 * m + (1.0 - ADAM_B1) * g
    vn = ADAM_B2 * v + (1.0 - ADAM_B2) * (g * g)
    m_hat = mn / (1.0 - ADAM_B1 ** ADAM_STEP)
    v_hat = vn / (1.0 - ADAM_B2 ** ADAM_STEP)
    return -ADAM_LR * (m_hat / (jnp.sqrt(v_hat) + ADAM_EPS) + ADAM_WD * w), mn, vn


def adamw(w, g, m, v, name):
    R, W = w.shape
    tr = _row_tile(R)

    def body(w_ref, g_ref, m_ref, v_ref, d_ref, mo_ref, vo_ref):
        d_ref[...], mo_ref[...], vo_ref[...] = _adamw_update(w_ref[...], g_ref[...], m_ref[...], v_ref[...])

    spec = pl.BlockSpec((tr, W), lambda i: (i, 0))
    return pl.pallas_call(
        body, name=name, grid=(R // tr,), in_specs=[spec] * 4, out_specs=[spec] * 3,
        out_shape=[jax.ShapeDtypeStruct((R, W), F32)] * 3,
        compiler_params=_cparams(("parallel",)),
    )(w, g, m, v)


def reduce_adamw(slabs, land, w, m, v, me_arr, name):
    _, R, W = slabs.shape
    tr = _row_tile(R)

    def body(me_ref, s_ref, l_ref, w_ref, m_ref, v_ref, g_ref, d_ref, mo_ref, vo_ref):
        g = s_ref[0].astype(F32)
        for r in range(N_DEV - 1):
            g = g + l_ref[r].astype(F32)
        g_ref[...] = g
        d_ref[...], mo_ref[...], vo_ref[...] = _adamw_update(w_ref[...], g, m_ref[...], v_ref[...])

    spec = pl.BlockSpec((tr, W), lambda i, me: (i, 0))
    return pl.pallas_call(
        body, name=name,
        grid_spec=pltpu.PrefetchScalarGridSpec(
            num_scalar_prefetch=1, grid=(R // tr,),
            in_specs=[pl.BlockSpec((1, tr, W), lambda i, me: (me[0], i, 0)),
                      pl.BlockSpec((N_DEV - 1, tr, W), lambda i, me: (0, i, 0)), spec, spec, spec],
            out_specs=[spec] * 4),
        out_shape=[jax.ShapeDtypeStruct((R, W), F32)] * 4,
        compiler_params=_cparams(("parallel",)),
    )(me_arr, slabs, land, w, m, v)


SMALL_NAMES = ("norm1_g", "q_norm_g", "k_norm_g", "sinks", "conv_out_g", "attn_out_g", "norm2_g", "conv_w")
SMALL_SIZES = (D_MODEL, HEAD, HEAD, N_Q, CONV_CH, ATTN_W, D_MODEL, 3 * CONV_CH)
SMALL_ROWS = 80


def kernel(x, norm1_g, w_in, conv_w, q_norm_g, k_norm_g, sinks, conv_out_g, attn_out_g, w_o, norm2_g, w_gate, w_up, w_down, loss_target, m_norm1_g, m_w_in, m_conv_w, m_q_norm_g, m_k_norm_g, m_sinks, m_conv_out_g, m_attn_out_g, m_w_o, m_norm2_g, m_w_gate, m_w_up, m_w_down, v_norm1_g, v_w_in, v_conv_w, v_q_norm_g, v_k_norm_g, v_sinks, v_conv_out_g, v_attn_out_g, v_w_o, v_norm2_g, v_w_gate, v_w_up, v_w_down):
    xi, yi, ci = _position()
    me = 4 * xi + 2 * yi + ci
    me_arr = jnp.reshape(me, (1,)).astype(jnp.int32)
    xs, tgt = x[0], loss_target[0]
    bf = lambda a: a.astype(BF16)
    bias = band_bias()

    g_in0, g_o0 = all_gather([bf(w_in[0]), bf(w_o[0])], "gather_first")
    ag_ffn0 = gather_start([bf(w_gate[0]), bf(w_up[0]), bf(w_down[0])], me, "gather_ffn0", after=g_in0)
    ag_l1 = gather_start([bf(w_in[1]), bf(w_o[1]), bf(w_gate[1]), bf(w_up[1]), bf(w_down[1])], me, "gather_layer1",
                         after=ag_ffn0.token)
    conv_place = lax.dynamic_update_slice(jnp.zeros((DEPTH, 3, CONV_CH), F32), conv_w, (0, 0, me * HEAD))
    conv_full = all_reduce_small(conv_place.reshape(DEPTH * 3 * CONV_CH // 128, 128), "gather_conv_w",
                                 dep=ag_l1.token).reshape(DEPTH, 3, CONV_CH)
    small = [dict(norm1_g=norm1_g[l][None], conv_w=conv_full[l], q_norm_g=q_norm_g[l][None],
                  k_norm_g=k_norm_g[l][None], sinks=sinks[l], conv_out_g=conv_out_g[l][None],
                  attn_out_g=attn_out_g[l][None], norm2_g=norm2_g[l][None]) for l in range(DEPTH)]
    weights = [dict(w_in=columns_from_slabs(g_in0, "layout_w_in0"), w_o=g_o0.reshape(D_MODEL, D_MODEL)), {}]

    saved = []
    xl = xs
    for l in range(DEPTH):
        sp, wl = small[l], weights[l]
        h, proj = norm_proj(xl, sp["norm1_g"], wl["w_in"], f"norm_proj{l}", dep=ag_l1.token if l == 0 else None)
        mix, y, ao = mixer_fwd(proj, bias, sp["sinks"], sp["conv_w"], sp["q_norm_g"], sp["k_norm_g"],
                               sp["conv_out_g"], sp["attn_out_g"], f"mixer_fwd{l}")
        dep = None
        if l == 0:
            pass_ffn0 = sibling_start(ag_ffn0.wait(mix)[3:], "pass_ffn0")
            dep = pass_ffn0.token
        x1 = matmul_residual(mix, wl["w_o"], xl, f"out_proj{l}", dep=dep)
        if l == 0:
            g_gate, g_up, g_down = pass_ffn0.wait(x1)
            wl.update(w_gate=columns_from_slabs(g_gate, "layout_w_gate0"),
                      w_up=columns_from_slabs(g_up, "layout_w_up0"), w_down=g_down.reshape(D_FF, D_MODEL))
            pass_l1 = sibling_start(ag_l1.wait(x1)[5:], "pass_layer1")
            dep = pass_l1.token
        h2, a, b, x2 = ffn_fwd(x1, sp["norm2_g"], wl["w_gate"], wl["w_up"], wl["w_down"], f"ffn_fwd{l}", dep=dep)
        if l == 0:
            g_in, g_o, g_gate, g_up, g_down = pass_l1.wait(x2)
            weights[1] = dict(w_in=columns_from_slabs(g_in, "layout_w_in1"), w_o=g_o.reshape(D_MODEL, D_MODEL),
                              w_gate=columns_from_slabs(g_gate, "layout_w_gate1"),
                              w_up=columns_from_slabs(g_up, "layout_w_up1"), w_down=g_down.reshape(D_FF, D_MODEL))
        saved.append((xl, h, proj, mix, y, ao, x1, h2, a, b))
        xl = x2
    loss_row, dx = loss_head(xl, tgt, "loss_head")

    shard = dict(w_in=(w_in, m_w_in, v_w_in), w_o=(w_o, m_w_o, v_w_o), w_gate=(w_gate, m_w_gate, v_w_gate),
                 w_up=(w_up, m_w_up, v_w_up), w_down=(w_down, m_w_down, v_w_down))
    stepped = {n: [None] * DEPTH for n in shard}
    gsmall = [None] * DEPTH

    def finish(sc, names, after, l):
        arrays = sc.wait(after)
        k = len(names)
        for i, n in enumerate(names):
            w, m, v = shard[n]
            stepped[n][l] = reduce_adamw(arrays[i], arrays[k + i], w[l], m[l], v[l], me_arr, f"reduce_adamw_{n}{l}")

    for l in reversed(range(DEPTH)):
        sp, wl = small[l], weights[l]
        x0, h, proj, mix, y, ao, x1, h2, a, b = saved[l]
        (d_wd,) = grad_weight((a, b), [dx], TF, D_MODEL, f"grad_w_down{l}", swiglu=True)
        sc_down = scatter_start([d_wd.reshape(N_DEV, ROWS_FF, D_MODEL)], f"scatter_w_down{l}")
        da, db, dx1, d_g2 = ffn_bwd(dx, a, b, x1, sp["norm2_g"], wl["w_gate"], wl["w_up"], wl["w_down"],
                                    f"ffn_bwd{l}", dep=sc_down.token)
        d_wg, d_wu = grad_weight((h2,), [da, db], D_MODEL, TF, f"grad_w_gate_up{l}", slab_cols=ROWS_FF)
        sc_gu = scatter_start([d_wg, d_wu], f"scatter_w_gate_up{l}")
        (d_wo,) = grad_weight((mix,), [dx1], D_MODEL, D_MODEL, f"grad_w_o{l}", dep=sc_gu.token)
        sc_o = scatter_start([d_wo.reshape(N_DEV, ROWS_O, D_MODEL)], f"scatter_w_o{l}")
        dmix = matmul_nt(dx1, wl["w_o"], f"out_proj_bwd{l}", dep=sc_o.token)
        finish(sc_down, ["w_down"], dmix, l)
        dproj, d_cw, d_qg, d_kg, d_sk, d_cog, d_aog = mixer_bwd(
            proj, bias, y, ao, dmix, sp["sinks"], sp["conv_w"], sp["q_norm_g"], sp["k_norm_g"],
            sp["conv_out_g"], sp["attn_out_g"], f"mixer_bwd{l}")
        finish(sc_gu, ["w_gate", "w_up"], dproj, l)
        finish(sc_o, ["w_o"], dproj, l)
        (d_win,) = grad_weight((h,), [dproj], D_MODEL, IN_COLS // 2, f"grad_w_in{l}", slab_cols=ROWS_IN)
        sc_in = scatter_start([d_win], f"scatter_w_in{l}")
        dx, d_g1 = proj_bwd(dproj, wl["w_in"], x0, sp["norm1_g"], dx1, f"proj_bwd{l}", dep=sc_in.token)
        finish(sc_in, ["w_in"], dx, l)
        gsmall[l] = dict(norm1_g=d_g1, conv_w=d_cw, q_norm_g=d_qg, k_norm_g=d_kg, sinks=d_sk,
                         conv_out_g=d_cog, attn_out_g=d_aog, norm2_g=d_g2)
    grad_x = dx[None]

    flat = jnp.concatenate([loss_row[0, 0:1]] + [gsmall[l][n].reshape(-1) for l in range(DEPTH) for n in SMALL_NAMES])
    flat = jnp.pad(flat, (0, SMALL_ROWS * 128 - flat.shape[0])).reshape(SMALL_ROWS, 128)
    flat = all_reduce_small(flat, "all_reduce_small_grads").reshape(-1)
    loss = flat[0]
    gs = {n: [] for n in SMALL_NAMES}
    off = 1
    for l in range(DEPTH):
        for n, size in zip(SMALL_NAMES, SMALL_SIZES):
            gs[n].append(flat[off:off + size])
            off += size
    gs = {n: jnp.stack(v) for n, v in gs.items()}
    g_conv = lax.dynamic_slice(gs["conv_w"].reshape(DEPTH, 3, CONV_CH), (0, 0, me * HEAD), (DEPTH, 3, HEAD))

    gs["conv_w"] = g_conv
    params = dict(norm1_g=(norm1_g, m_norm1_g, v_norm1_g), conv_w=(conv_w, m_conv_w, v_conv_w),
                  q_norm_g=(q_norm_g, m_q_norm_g, v_q_norm_g), k_norm_g=(k_norm_g, m_k_norm_g, v_k_norm_g),
                  sinks=(sinks, m_sinks, v_sinks), conv_out_g=(conv_out_g, m_conv_out_g, v_conv_out_g),
                  attn_out_g=(attn_out_g, m_attn_out_g, v_attn_out_g), norm2_g=(norm2_g, m_norm2_g, v_norm2_g))
    names = ("norm1_g", "w_in", "conv_w", "q_norm_g", "k_norm_g", "sinks", "conv_out_g", "attn_out_g", "w_o",
             "norm2_g", "w_gate", "w_up", "w_down")

    out = {}
    for n in names:
        if n in shard:
            out[n] = tuple(jnp.stack([stepped[n][l][i] for l in range(DEPTH)]) for i in range(4))
        else:
            w, m, v = params[n]
            two_d = (-1, w.shape[-1])
            d, mn, vn = adamw(w.reshape(two_d), gs[n].reshape(two_d), m.reshape(two_d), v.reshape(two_d), f"adamw_{n}")
            out[n] = (gs[n].reshape(w.shape), d.reshape(w.shape), mn.reshape(w.shape), vn.reshape(w.shape))
    return (loss, grad_x, *[out[n][i] for i in range(4) for n in names])
```

```python
import functools

import jax
import jax.numpy as jnp
from jax import lax
from jax.experimental import pallas as pl
from jax.experimental.pallas import tpu as pltpu

F32 = jnp.float32
BF16 = jnp.bfloat16

D_MODEL = 1024
CONV_CH = 512
ATTN_W = 512
N_Q = 8
N_KV = 2
GRP = N_Q // N_KV
HEAD = 64
IN_COLS = 2304
D_FF = 2816
BLK = 128
O_Q = 3 * CONV_CH
O_K = O_Q + ATTN_W
O_V = O_K + N_KV * HEAD
EPS = 1e-6
NEG_INF = -1e30
SCALE = HEAD ** -0.5
N_DEV = 8
DEPTH = 2

ADAM_LR = 0.001
ADAM_B1 = 0.9
ADAM_B2 = 0.999
ADAM_EPS = 1e-08
ADAM_WD = 0.01
ADAM_STEP = 10

VMEM_LIMIT = 56 * 1024 * 1024
TM = 512
MESH_T = pl.DeviceIdType.MESH

ROWS_IN, ROWS_O, ROWS_FF = IN_COLS // N_DEV, D_MODEL // N_DEV, D_FF // N_DEV


ANY = pl.BlockSpec(memory_space=pl.ANY)
HBM = pl.BlockSpec(memory_space=pltpu.HBM)
SEM = pl.BlockSpec(memory_space=pltpu.SEMAPHORE)


def _cparams(sem):
    return pltpu.CompilerParams(dimension_semantics=sem, vmem_limit_bytes=VMEM_LIMIT)


def _pallas_after(dep, body, *args, in_specs, **kw):
    if dep is None:
        return pl.pallas_call(body, in_specs=in_specs, **kw)(*args)

    def after_dep(dep_ref, *refs):
        body(*refs)

    return pl.pallas_call(after_dep, in_specs=[ANY, *in_specs], **kw)(dep, *args)


def _dot(a, b):
    return jnp.dot(a, b, preferred_element_type=F32)


def _dot_nt(a, b):
    return lax.dot_general(a, b, (((1,), (1,)), ((), ())), preferred_element_type=F32)


def _dot_tn(a, b):
    return lax.dot_general(a, b, (((0,), (0,)), ((), ())), preferred_element_type=F32)


LANES = 128


def _row_reduce(v, op, reduce):
    w = v.shape[-1]
    if w > LANES and w % LANES == 0:
        acc = v[:, 0:LANES]
        for c in range(1, w // LANES):
            acc = op(acc, v[:, LANES * c:LANES * (c + 1)])
        v = acc
    return reduce(v, axis=-1, keepdims=True)


def _row_sum(v):
    return _row_reduce(v, jnp.add, jnp.sum)


def _row_mean(v):
    return _row_sum(v) * (1.0 / v.shape[-1])


def _rms(v):
    return lax.rsqrt(_row_mean(v * v) + EPS)


def _rms_bwd(dyv, xh, r, g):
    dxh = dyv * g
    return r * (dxh - xh * _row_mean(dxh * xh))


def norm_proj(x, g, w, name, dep=None):
    S, N = x.shape[0], w.shape[1]

    def body(x_ref, g_ref, w_ref, h_ref, p_ref):
        xf = x_ref[...]
        h = ((xf * _rms(xf)) * g_ref[...]).astype(BF16)
        h_ref[...] = h
        p_ref[...] = _dot(h, w_ref[...]).astype(BF16)

    return _pallas_after(
        dep, body, x, g, w, name=name, grid=(S // TM,),
        in_specs=[pl.BlockSpec((TM, D_MODEL), lambda i: (i, 0)),
                  pl.BlockSpec((1, D_MODEL), lambda i: (0, 0)),
                  pl.BlockSpec((D_MODEL, N), lambda i: (0, 0))],
        out_specs=[pl.BlockSpec((TM, D_MODEL), lambda i: (i, 0)),
                   pl.BlockSpec((TM, N), lambda i: (i, 0))],
        out_shape=[jax.ShapeDtypeStruct((S, D_MODEL), BF16), jax.ShapeDtypeStruct((S, N), BF16)],
        compiler_params=_cparams(("parallel",)),
    )


def band_bias():
    qi = lax.broadcasted_iota(jnp.int32, (BLK, 2 * BLK), 0)
    kj = lax.broadcasted_iota(jnp.int32, (BLK, 2 * BLK), 1)
    diff = qi + BLK - kj
    valid = (diff >= 0) & (diff < BLK)
    return jnp.stack([jnp.where(valid & (kj >= BLK), 0.0, NEG_INF), jnp.where(valid, 0.0, NEG_INF)]).astype(F32)


def _softmax_with_sink(s, sink):
    m = jnp.maximum(_row_reduce(s, jnp.maximum, jnp.max), sink)
    p = jnp.exp(s - m)
    es = jnp.exp(sink - m)
    inv = 1.0 / (_row_sum(p) + es)
    return p * inv, es * inv


PAIR = 2 * HEAD


def _low_half():
    return lax.broadcasted_iota(jnp.int32, (1, PAIR), 1) < HEAD


def _half_sums(v, low):
    return (jnp.sum(jnp.where(low, v, 0.0), axis=-1, keepdims=True),
            jnp.sum(jnp.where(low, 0.0, v), axis=-1, keepdims=True))


def _pair_mean(v, low):
    e, o = _half_sums(v, low)
    return jnp.where(low, e, o) * (1.0 / HEAD)


def _pair_rms(v, low):
    return lax.rsqrt(_pair_mean(v * v, low) + EPS)


def _one_head_in_both_halves(pair, low):
    swapped = pltpu.roll(pair, HEAD, 1)
    return jnp.where(low, pair, swapped), jnp.where(low, swapped, pair)


def _kv_pairs(p_ref, kvp_ref):
    k = jnp.concatenate([kvp_ref[:, 0:PAIR], p_ref[:, O_K:O_K + PAIR]], axis=0).astype(F32)
    v = jnp.concatenate([kvp_ref[:, PAIR:2 * PAIR], p_ref[:, O_V:O_V + PAIR]], axis=0)
    return k, v


def mixer_fwd(proj, bias, sinks, conv_w, qg, kg, cog, aog, name):
    S = proj.shape[0]
    nb = S // BLK

    def body(sinks_ref, p_ref, kvp_ref, bias_ref, cw_ref, qg_ref, kg_ref, cog_ref, aog_ref,
             mix_ref, y_ref, ao_ref, ucar):
        n = pl.program_id(0)

        @pl.when(n == 0)
        def _():
            ucar[...] = jnp.zeros_like(ucar)

        bg = p_ref[:, 0:CONV_CH].astype(F32)
        u = p_ref[:, CONV_CH:2 * CONV_CH].astype(F32) * p_ref[:, 2 * CONV_CH:3 * CONV_CH].astype(F32)
        row = lax.broadcasted_iota(jnp.int32, (BLK, CONV_CH), 0)
        prev = ucar[...]
        u1 = jnp.where(row == 0, prev[7:8, :], pltpu.roll(u, 1, 0))
        u2 = jnp.where(row == 0, prev[6:7, :], jnp.where(row == 1, prev[7:8, :], pltpu.roll(u, 2, 0)))
        ucar[...] = u[BLK - 8:BLK, :]
        y = cw_ref[0:1, :] * u2 + cw_ref[1:2, :] * u1 + cw_ref[2:3, :] * u
        y_ref[...] = y.astype(BF16)
        co = bg * y
        mix_ref[:, 0:CONV_CH] = ((co * _rms(co)) * cog_ref[...]).astype(BF16)

        low = _low_half()
        q_gain = qg_ref[...] * SCALE
        k, v = _kv_pairs(p_ref, kvp_ref)
        kn = ((k * _pair_rms(k, low)) * kg_ref[...]).astype(BF16)
        k_of = _one_head_in_both_halves(kn, low)
        v_of = _one_head_in_both_halves(v, low)
        scores = []
        for pair in range(N_Q // 2):
            q = p_ref[:, O_Q + PAIR * pair:O_Q + PAIR * (pair + 1)].astype(F32)
            qn = ((q * _pair_rms(q, low)) * q_gain).astype(BF16)
            for half in range(2):
                q_one = jnp.where(low if half == 0 else jnp.logical_not(low), qn, jnp.zeros_like(qn))
                scores.append(_dot_nt(q_one, k_of[pair // (GRP // 2)]) + bias_ref[0])
        probs = [_softmax_with_sink(s, sinks_ref[hq])[0].astype(BF16) for hq, s in enumerate(scores)]
        o = [_dot(pn, v_of[hq // GRP]) for hq, pn in enumerate(probs)]
        ao = jnp.concatenate([jnp.where(low, o[2 * pair], o[2 * pair + 1]) for pair in range(N_Q // 2)], axis=1)
        ao_ref[...] = ao.astype(BF16)
        mix_ref[:, CONV_CH:] = ((ao * _rms(ao)) * aog_ref[...]).astype(BF16)

    small = lambda shape: pl.BlockSpec(shape, lambda n: (0, 0))
    return pl.pallas_call(
        body, name=name, grid=(nb,),
        in_specs=[pl.BlockSpec(memory_space=pltpu.SMEM),
                  pl.BlockSpec((BLK, IN_COLS), lambda n: (n, 0)),
                  pl.BlockSpec((BLK, 2 * N_KV * HEAD), lambda n: (jnp.maximum(n - 1, 0), O_K // (2 * N_KV * HEAD))),
                  pl.BlockSpec((1, BLK, 2 * BLK), lambda n: (jnp.minimum(n, 1), 0, 0)),
                  small((3, CONV_CH)), small((1, PAIR)), small((1, PAIR)), small((1, CONV_CH)), small((1, ATTN_W))],
        out_specs=[pl.BlockSpec((BLK, D_MODEL), lambda n: (n, 0)),
                   pl.BlockSpec((BLK, CONV_CH), lambda n: (n, 0)),
                   pl.BlockSpec((BLK, ATTN_W), lambda n: (n, 0))],
        out_shape=[jax.ShapeDtypeStruct((S, D_MODEL), BF16), jax.ShapeDtypeStruct((S, CONV_CH), BF16),
                   jax.ShapeDtypeStruct((S, ATTN_W), BF16)],
        scratch_shapes=[pltpu.VMEM((8, CONV_CH), F32)],
        compiler_params=_cparams(("arbitrary",)),
    )(sinks, proj, proj, bias, conv_w, qg, kg, cog, aog)


def matmul_residual(a, w, res, name, dep=None):
    S, K = a.shape
    N = w.shape[1]

    def body(a_ref, w_ref, r_ref, o_ref):
        o_ref[...] = r_ref[...] + _dot(a_ref[...], w_ref[...])

    return _pallas_after(
        dep, body, a, w, res, name=name, grid=(S // TM,),
        in_specs=[pl.BlockSpec((TM, K), lambda i: (i, 0)), pl.BlockSpec((K, N), lambda i: (0, 0)),
                  pl.BlockSpec((TM, N), lambda i: (i, 0))],
        out_specs=pl.BlockSpec((TM, N), lambda i: (i, 0)),
        out_shape=jax.ShapeDtypeStruct((S, N), F32),
        compiler_params=_cparams(("parallel",)),
    )


TF = 1408


def ffn_fwd(x1, g, wg, wu, wd, name, dep=None):
    S = x1.shape[0]
    nj = D_FF // TF

    def body(x_ref, g_ref, wg_ref, wu_ref, wd_ref, h2_ref, a_ref, b_ref, o_ref, acc):
        j = pl.program_id(1)

        @pl.when(j == 0)
        def _():
            xf = x_ref[...]
            h2_ref[...] = ((xf * _rms(xf)) * g_ref[...]).astype(BF16)
            acc[...] = xf

        h2 = h2_ref[...]
        a = _dot(h2, wg_ref[...])
        b = _dot(h2, wu_ref[...])
        a_ref[...] = a.astype(BF16)
        b_ref[...] = b.astype(BF16)
        f = (a * (1.0 / (1.0 + jnp.exp(-a)))) * b
        acc[...] += _dot(f.astype(BF16), wd_ref[...])

        @pl.when(j == nj - 1)
        def _():
            o_ref[...] = acc[...]

    return _pallas_after(
        dep, body, x1, g, wg, wu, wd, name=name, grid=(S // TM, nj),
        in_specs=[pl.BlockSpec((TM, D_MODEL), lambda i, j: (i, 0)), pl.BlockSpec((1, D_MODEL), lambda i, j: (0, 0)),
                  pl.BlockSpec((D_MODEL, TF), lambda i, j: (0, j)), pl.BlockSpec((D_MODEL, TF), lambda i, j: (0, j)),
                  pl.BlockSpec((TF, D_MODEL), lambda i, j: (j, 0))],
        out_specs=[pl.BlockSpec((TM, D_MODEL), lambda i, j: (i, 0)), pl.BlockSpec((TM, TF), lambda i, j: (i, j)),
                   pl.BlockSpec((TM, TF), lambda i, j: (i, j)), pl.BlockSpec((TM, D_MODEL), lambda i, j: (i, 0))],
        out_shape=[jax.ShapeDtypeStruct((S, D_MODEL), BF16), jax.ShapeDtypeStruct((S, D_FF), BF16),
                   jax.ShapeDtypeStruct((S, D_FF), BF16), jax.ShapeDtypeStruct((S, D_MODEL), F32)],
        scratch_shapes=[pltpu.VMEM((TM, D_MODEL), F32)],
        compiler_params=_cparams(("parallel", "arbitrary")),
    )


def loss_head(yv, tgt, name):
    S = yv.shape[0]
    ni = S // TM

    def body(y_ref, t_ref, l_ref, d_ref, acc):
        i = pl.program_id(0)

        @pl.when(i == 0)
        def _():
            acc[...] = jnp.zeros_like(acc)

        e = y_ref[...] - t_ref[...]
        d_ref[...] = e * (1.0 / D_MODEL)
        acc[...] += jnp.sum(e * e, axis=0, keepdims=True)

        @pl.when(i == ni - 1)
        def _():
            l_ref[...] = jnp.full((1, 128), jnp.sum(acc[...]) * (0.5 / D_MODEL), F32)

    return pl.pallas_call(
        body, name=name, grid=(ni,),
        in_specs=[pl.BlockSpec((TM, D_MODEL), lambda i: (i, 0)), pl.BlockSpec((TM, D_MODEL), lambda i: (i, 0))],
        out_specs=[pl.BlockSpec((1, 128), lambda i: (0, 0)), pl.BlockSpec((TM, D_MODEL), lambda i: (i, 0))],
        out_shape=[jax.ShapeDtypeStruct((1, 128), F32), jax.ShapeDtypeStruct((S, D_MODEL), F32)],
        scratch_shapes=[pltpu.VMEM((1, D_MODEL), F32)],
        compiler_params=_cparams(("arbitrary",)),
    )(yv, tgt)


def ffn_bwd(dx2, a, b, x1, g, wg, wu, wd, name, dep=None):
    S = dx2.shape[0]
    nj = D_FF // TF

    def body(dx_ref, a_ref, b_ref, x_ref, g_ref, wg_ref, wu_ref, wd_ref, da_ref, db_ref, dx1_ref, dg_ref, dxb, acc):
        i, j = pl.program_id(0), pl.program_id(1)

        @pl.when((i == 0) & (j == 0))
        def _():
            dg_ref[...] = jnp.zeros_like(dg_ref)

        @pl.when(j == 0)
        def _():
            dxb[...] = dx_ref[...].astype(BF16)
            acc[...] = jnp.zeros_like(acc)

        df = _dot_nt(dxb[...], wd_ref[...])
        av = a_ref[...].astype(F32)
        bv = b_ref[...].astype(F32)
        sg = 1.0 / (1.0 + jnp.exp(-av))
        da = ((df * bv) * (sg * (1.0 + av * (1.0 - sg)))).astype(BF16)
        db = (df * (av * sg)).astype(BF16)
        da_ref[...] = da
        db_ref[...] = db
        acc[...] += _dot_nt(da, wg_ref[...]) + _dot_nt(db, wu_ref[...])

        @pl.when(j == nj - 1)
        def _():
            xf = x_ref[...]
            r = _rms(xf)
            xh = xf * r
            dh = acc[...]
            dg_ref[...] += jnp.sum(dh * xh, axis=0, keepdims=True)
            dx1_ref[...] = dx_ref[...] + _rms_bwd(dh, xh, r, g_ref[...])

    return _pallas_after(
        dep, body, dx2, a, b, x1, g, wg, wu, wd, name=name, grid=(S // TM, nj),
        in_specs=[pl.BlockSpec((TM, D_MODEL), lambda i, j: (i, 0)),
                  pl.BlockSpec((TM, TF), lambda i, j: (i, j)), pl.BlockSpec((TM, TF), lambda i, j: (i, j)),
                  pl.BlockSpec((TM, D_MODEL), lambda i, j: (i, 0)), pl.BlockSpec((1, D_MODEL), lambda i, j: (0, 0)),
                  pl.BlockSpec((D_MODEL, TF), lambda i, j: (0, j)), pl.BlockSpec((D_MODEL, TF), lambda i, j: (0, j)),
                  pl.BlockSpec((TF, D_MODEL), lambda i, j: (j, 0))],
        out_specs=[pl.BlockSpec((TM, TF), lambda i, j: (i, j)), pl.BlockSpec((TM, TF), lambda i, j: (i, j)),
                   pl.BlockSpec((TM, D_MODEL), lambda i, j: (i, 0)), pl.BlockSpec((1, D_MODEL), lambda i, j: (0, 0))],
        out_shape=[jax.ShapeDtypeStruct((S, D_FF), BF16), jax.ShapeDtypeStruct((S, D_FF), BF16),
                   jax.ShapeDtypeStruct((S, D_MODEL), F32), jax.ShapeDtypeStruct((1, D_MODEL), F32)],
        scratch_shapes=[pltpu.VMEM((TM, D_MODEL), BF16), pltpu.VMEM((TM, D_MODEL), F32)],
        compiler_params=_cparams(("arbitrary", "arbitrary")),
    )


def grad_weight(lhs, rhs, tm, tn, name, swiglu=False, slab_cols=None, dep=None):
    S = rhs[0].shape[0]
    M = lhs[0].shape[1]
    N = rhs[0].shape[1]
    tk = 512
    nk = S // tk
    n_l, n_r = len(lhs), len(rhs)
    if slab_cols is None:
        out_spec = pl.BlockSpec((tm, tn), lambda i, j, k: (i, j))
        out_shape = jax.ShapeDtypeStruct((M, N), BF16)
    else:
        out_spec = pl.BlockSpec((tn // slab_cols, tm, slab_cols), lambda i, j, k: (j, i, 0))
        out_shape = jax.ShapeDtypeStruct((N // slab_cols, M, slab_cols), BF16)

    def body(*refs):
        l_refs, r_refs = refs[:n_l], refs[n_l:n_l + n_r]
        o_refs, accs = refs[n_l + n_r:n_l + 2 * n_r], refs[n_l + 2 * n_r:]
        k = pl.program_id(2)
        if swiglu:
            av = l_refs[0][...].astype(F32)
            lv = ((av * (1.0 / (1.0 + jnp.exp(-av)))) * l_refs[1][...].astype(F32)).astype(BF16)
        else:
            lv = l_refs[0][...]
        for r_ref, o_ref, acc in zip(r_refs, o_refs, accs):
            part = _dot_tn(lv, r_ref[...].astype(BF16))

            @pl.when(k == 0)
            def _():
                acc[...] = part

            @pl.when(k > 0)
            def _():
                acc[...] += part

            @pl.when(k == nk - 1)
            def _():
                if slab_cols is None:
                    o_ref[...] = acc[...].astype(BF16)
                else:
                    for s in range(tn // slab_cols):
                        o_ref[s] = acc[:, slab_cols * s:slab_cols * (s + 1)].astype(BF16)

    return _pallas_after(
        dep, body, *lhs, *rhs, name=name, grid=(M // tm, N // tn, nk),
        in_specs=[pl.BlockSpec((tk, tm), lambda i, j, k: (k, i)) for _ in lhs]
        + [pl.BlockSpec((tk, tn), lambda i, j, k: (k, j)) for _ in rhs],
        out_specs=[out_spec for _ in rhs],
        out_shape=[out_shape for _ in rhs],
        scratch_shapes=[pltpu.VMEM((tm, tn), F32) for _ in rhs],
        compiler_params=_cparams(("parallel", "parallel", "arbitrary")),
    )


def matmul_nt(a, w, name, dep=None):
    S, K = a.shape
    N = w.shape[0]

    def body(a_ref, w_ref, o_ref):
        o_ref[...] = _dot_nt(a_ref[...].astype(BF16), w_ref[...]).astype(BF16)

    return _pallas_after(
        dep, body, a, w, name=name, grid=(S // TM,),
        in_specs=[pl.BlockSpec((TM, K), lambda i: (i, 0)), pl.BlockSpec((N, K), lambda i: (0, 0))],
        out_specs=pl.BlockSpec((TM, N), lambda i: (i, 0)),
        out_shape=jax.ShapeDtypeStruct((S, N), BF16),
        compiler_params=_cparams(("parallel",)),
    )


def proj_bwd(dproj, w, x, g, dres, name, dep=None):
    S, N = dproj.shape

    def body(dp_ref, w_ref, x_ref, g_ref, dr_ref, dx_ref, dg_ref):
        i = pl.program_id(0)

        @pl.when(i == 0)
        def _():
            dg_ref[...] = jnp.zeros_like(dg_ref)

        dh = _dot_nt(dp_ref[...], w_ref[...])
        xf = x_ref[...]
        r = _rms(xf)
        xh = xf * r
        dg_ref[...] += jnp.sum(dh * xh, axis=0, keepdims=True)
        dx_ref[...] = dr_ref[...] + _rms_bwd(dh, xh, r, g_ref[...])

    return _pallas_after(
        dep, body, dproj, w, x, g, dres, name=name, grid=(S // TM,),
        in_specs=[pl.BlockSpec((TM, N), lambda i: (i, 0)), pl.BlockSpec((D_MODEL, N), lambda i: (0, 0)),
                  pl.BlockSpec((TM, D_MODEL), lambda i: (i, 0)), pl.BlockSpec((1, D_MODEL), lambda i: (0, 0)),
                  pl.BlockSpec((TM, D_MODEL), lambda i: (i, 0))],
        out_specs=[pl.BlockSpec((TM, D_MODEL), lambda i: (i, 0)), pl.BlockSpec((1, D_MODEL), lambda i: (0, 0))],
        out_shape=[jax.ShapeDtypeStruct((S, D_MODEL), F32), jax.ShapeDtypeStruct((1, D_MODEL), F32)],
        compiler_params=_cparams(("arbitrary",)),
    )


def mixer_bwd(proj, bias, y, ao, dmix, sinks, conv_w, qg, kg, cog, aog, name):
    S = proj.shape[0]
    nb = S // BLK
    KV_W = 2 * N_KV * HEAD

    def body(sinks_ref, p_ref, kvp_ref, bias_ref, y_ref, ao_ref, dm_ref, cw_ref, qg_ref, kg_ref, cog_ref, aog_ref,
             dp_ref, dcw_ref, dqg_ref, dkg_ref, dsk_ref, dcog_ref, daog_ref,
             dycar, kcar, vcar):
        step = pl.program_id(0)

        @pl.when(step == 0)
        def _():
            dycar[...] = jnp.zeros_like(dycar)
            kcar[...] = jnp.zeros_like(kcar)
            vcar[...] = jnp.zeros_like(vcar)
            dcw_ref[...] = jnp.zeros_like(dcw_ref)
            dqg_ref[...] = jnp.zeros_like(dqg_ref)
            dkg_ref[...] = jnp.zeros_like(dkg_ref)
            dsk_ref[...] = jnp.zeros_like(dsk_ref)
            dcog_ref[...] = jnp.zeros_like(dcog_ref)
            daog_ref[...] = jnp.zeros_like(daog_ref)

        bg = p_ref[:, 0:CONV_CH].astype(F32)
        cg = p_ref[:, CONV_CH:2 * CONV_CH].astype(F32)
        hc = p_ref[:, 2 * CONV_CH:3 * CONV_CH].astype(F32)
        yv = y_ref[...].astype(F32)
        dmc = dm_ref[:, 0:CONV_CH].astype(F32)
        co = bg * yv
        rc = _rms(co)
        ch = co * rc
        dcog_ref[...] += jnp.sum(dmc * ch, axis=0, keepdims=True)
        dco = _rms_bwd(dmc, ch, rc, cog_ref[...])
        dp_ref[:, 0:CONV_CH] = (dco * yv).astype(BF16)
        dy = dco * bg
        row = lax.broadcasted_iota(jnp.int32, (BLK, CONV_CH), 0)
        nxt = dycar[...]
        dy1 = jnp.where(row == BLK - 1, nxt[0:1, :], pltpu.roll(dy, BLK - 1, 0))
        dy2 = jnp.where(row == BLK - 2, nxt[0:1, :], jnp.where(row == BLK - 1, nxt[1:2, :], pltpu.roll(dy, BLK - 2, 0)))
        dycar[...] = dy[0:8, :]
        du = cw_ref[2:3, :] * dy + cw_ref[1:2, :] * dy1 + cw_ref[0:1, :] * dy2
        dp_ref[:, CONV_CH:2 * CONV_CH] = (du * hc).astype(BF16)
        dp_ref[:, 2 * CONV_CH:3 * CONV_CH] = (du * cg).astype(BF16)
        u = cg * hc
        dcw_ref[0:1, :] += jnp.sum(dy2 * u, axis=0, keepdims=True)
        dcw_ref[1:2, :] += jnp.sum(dy1 * u, axis=0, keepdims=True)
        dcw_ref[2:3, :] += jnp.sum(dy * u, axis=0, keepdims=True)

        aov = ao_ref[...].astype(F32)
        dma = dm_ref[:, CONV_CH:].astype(F32)
        ra = _rms(aov)
        ah = aov * ra
        daog_ref[...] += jnp.sum(dma * ah, axis=0, keepdims=True)
        dao = _rms_bwd(dma, ah, ra, aog_ref[...])

        low = _low_half()
        high = jnp.logical_not(low)
        q_gain = qg_ref[...] * SCALE
        k, v = _kv_pairs(p_ref, kvp_ref)
        rk = _pair_rms(k, low)
        kh = k * rk
        k_of = _one_head_in_both_halves((kh * kg_ref[...]).astype(BF16), low)
        v_of = _one_head_in_both_halves(v, low)
        lane8 = lax.broadcasted_iota(jnp.int32, (1, N_Q), 1)
        dsk = jnp.zeros((1, N_Q), F32)
        dqg = jnp.zeros((1, PAIR), F32)
        dv_t = [jnp.zeros((PAIR, 2 * BLK), F32) for _ in range(N_KV)]
        dkn_t = [jnp.zeros((PAIR, 2 * BLK), F32) for _ in range(N_KV)]
        rq, qh, q_one, do_one, delta = [], [], [], [], []
        for pair in range(N_Q // 2):
            cols = slice(PAIR * pair, PAIR * (pair + 1))
            q = p_ref[:, O_Q + PAIR * pair:O_Q + PAIR * (pair + 1)].astype(F32)
            rq.append(_pair_rms(q, low))
            qh.append(q * rq[pair])
            qn = (qh[pair] * q_gain).astype(BF16)
            do = dao[:, cols]
            do_b = do.astype(BF16)
            delta += _half_sums(do * aov[:, cols], low)
            for mine in (low, high):
                q_one.append(jnp.where(mine, qn, jnp.zeros_like(qn)))
                do_one.append(jnp.where(mine, do_b, jnp.zeros_like(do_b)))
        heads = range(N_Q)
        scores = [_dot_nt(q_one[hq], k_of[hq // GRP]) + bias_ref[0] for hq in heads]
        dprobs = [_dot_nt(do_one[hq], v_of[hq // GRP]) for hq in heads]
        probs = [_softmax_with_sink(scores[hq], sinks_ref[hq]) for hq in heads]
        ds = [(probs[hq][0] * (dprobs[hq] - delta[hq])).astype(BF16) for hq in heads]
        for hq in heads:
            dsk = dsk - jnp.where(lane8 == hq, jnp.sum(probs[hq][1] * delta[hq]), 0.0)
            dv_t[hq // GRP] = dv_t[hq // GRP] + _dot_tn(do_one[hq], probs[hq][0].astype(BF16))
            dkn_t[hq // GRP] = dkn_t[hq // GRP] + _dot_tn(q_one[hq], ds[hq])
        dqn_of = [_dot(ds[hq], k_of[hq // GRP]) for hq in heads]
        for pair in range(N_Q // 2):
            dqn = jnp.where(low, dqn_of[2 * pair], dqn_of[2 * pair + 1])
            dqg = dqg + jnp.sum(dqn * qh[pair], axis=0, keepdims=True)
            dqh = dqn * q_gain
            dp_ref[:, O_Q + PAIR * pair:O_Q + PAIR * (pair + 1)] = (
                rq[pair] * (dqh - qh[pair] * _pair_mean(dqh * qh[pair], low))).astype(BF16)

        def untranspose(parts):
            return jnp.concatenate([t[:HEAD] + t[HEAD:] for t in parts], axis=0).T

        dv = untranspose(dv_t)
        dkn = untranspose(dkn_t)
        dkg_ref[...] += jnp.sum(dkn * kh, axis=0, keepdims=True)
        dkh = dkn * kg_ref[...]
        dk = rk * (dkh - kh * _pair_mean(dkh * kh, low))
        dp_ref[:, O_K:O_V] = (kcar[...] + dk[BLK:, :]).astype(BF16)
        dp_ref[:, O_V:] = (vcar[...] + dv[BLK:, :]).astype(BF16)
        kcar[...] = dk[:BLK, :]
        vcar[...] = dv[:BLK, :]
        dsk_ref[...] += dsk
        dqg_ref[...] += dqg * SCALE

    small = lambda shape: pl.BlockSpec(shape, lambda s: (0, 0))
    blk = lambda w: pl.BlockSpec((BLK, w), lambda s: (nb - 1 - s, 0))
    return pl.pallas_call(
        body, name=name, grid=(nb,),
        in_specs=[pl.BlockSpec(memory_space=pltpu.SMEM),
                  blk(IN_COLS),
                  pl.BlockSpec((BLK, KV_W), lambda s: (jnp.maximum(nb - 2 - s, 0), O_K // KV_W)),
                  pl.BlockSpec((1, BLK, 2 * BLK), lambda s: (jnp.minimum(nb - 1 - s, 1), 0, 0)),
                  blk(CONV_CH), blk(ATTN_W), blk(D_MODEL),
                  small((3, CONV_CH)), small((1, PAIR)), small((1, PAIR)), small((1, CONV_CH)), small((1, ATTN_W))],
        out_specs=[blk(IN_COLS), small((3, CONV_CH)), small((1, PAIR)), small((1, PAIR)), small((1, N_Q)),
                   small((1, CONV_CH)), small((1, ATTN_W))],
        out_shape=[jax.ShapeDtypeStruct((S, IN_COLS), BF16), jax.ShapeDtypeStruct((3, CONV_CH), F32),
                   jax.ShapeDtypeStruct((1, PAIR), F32), jax.ShapeDtypeStruct((1, PAIR), F32),
                   jax.ShapeDtypeStruct((1, N_Q), F32), jax.ShapeDtypeStruct((1, CONV_CH), F32),
                   jax.ShapeDtypeStruct((1, ATTN_W), F32)],
        scratch_shapes=[pltpu.VMEM((8, CONV_CH), F32), pltpu.VMEM((BLK, PAIR), F32), pltpu.VMEM((BLK, PAIR), F32)],
        compiler_params=_cparams(("arbitrary",)),
    )(sinks, proj, proj, bias, y, ao, dmix, conv_w, qg, kg, cog, aog)


OTHER_CHIPS = ((1, 0), (0, 1), (1, 1))


def _position():
    return lax.axis_index("x"), lax.axis_index("y"), lax.axis_index("c")


def all_gather(shards, name):
    n = len(shards)

    def body(*refs):
        x_refs, out_refs = refs[:n], refs[n:2 * n]
        send_sems, recv_sems, local_sems = refs[2 * n:]
        x, y, c = _position()
        me, sibling = (x, y, c), (x, y, 1 - c)
        chips = [(x ^ mx, y ^ my) for mx, my in OTHER_CHIPS]

        def slab(a, px, py, pc):
            return out_refs[a].at[4 * px + 2 * py + pc]

        def copy(a, k, block, to, src=None):
            return pltpu.make_async_remote_copy(
                src_ref=slab(a, *block) if src is None else src, dst_ref=slab(a, *block),
                send_sem=send_sems.at[7 * a + k], recv_sem=recv_sems.at[7 * a + k], device_id=to, device_id_type=MESH_T)

        mine = [pltpu.make_async_copy(x_refs[a], slab(a, *me), local_sems.at[a]) for a in range(n)]
        for cp in mine:
            cp.start()
        first = []
        for a in range(n):
            first.append(copy(a, 0, me, sibling, src=x_refs[a]))
            first += [copy(a, 1 + j, me, (*chip, c), src=x_refs[a]) for j, chip in enumerate(chips)]
        for cp in first:
            cp.start()
        passed = []
        for a in range(n):
            for j, chip in enumerate(chips):
                copy(a, 1 + j, (*chip, c), me).wait_recv()
                passed.append(copy(a, 4 + j, (*chip, c), sibling))
                passed[-1].start()
        for a in range(n):
            copy(a, 0, sibling, me).wait_recv()
            for j, chip in enumerate(chips):
                copy(a, 4 + j, (*chip, 1 - c), me).wait_recv()
        for cp in first + passed:
            cp.wait_send()
        for cp in mine:
            cp.wait()

    return pl.pallas_call(
        body, name=name, in_specs=[ANY] * n, out_specs=[ANY] * n,
        out_shape=[jax.ShapeDtypeStruct((N_DEV, *s.shape), s.dtype) for s in shards],
        scratch_shapes=[pltpu.SemaphoreType.DMA((7 * n,)), pltpu.SemaphoreType.DMA((7 * n,)),
                        pltpu.SemaphoreType.DMA((n,))],
    )(*shards)


class SplitCopy:
    def __init__(self, name, arrays, n_copies, plan, after=None):
        n = len(arrays)
        self.name, self.n, self.n_copies, self.plan = name, n, n_copies, plan
        extra = [] if after is None else [after]

        def body(*refs):
            in_refs = refs[:n]
            send_sems, recv_sems = refs[n + len(extra)], refs[n + len(extra) + 1]
            token = refs[2 * n + len(extra) + 2]
            for k, (src, dst, to) in enumerate(plan(_position(), in_refs)):
                pltpu.make_async_remote_copy(src_ref=src, dst_ref=dst, send_sem=send_sems.at[k],
                                             recv_sem=recv_sems.at[k], device_id=to, device_id_type=MESH_T).start()
            token[...] = jnp.zeros_like(token)

        outs = pl.pallas_call(
            body, name=name + "_start",
            out_shape=(pltpu.SemaphoreType.DMA((n_copies,)), pltpu.SemaphoreType.DMA((n_copies,)),
                       *[pltpu.HBM(a.shape, a.dtype) for a in arrays], jax.ShapeDtypeStruct((8, 128), F32)),
            in_specs=[HBM] * n + [ANY] * len(extra),
            out_specs=(SEM, SEM, *[HBM] * n, pl.BlockSpec(memory_space=pltpu.VMEM)),
            input_output_aliases={i: 2 + i for i in range(n)},
            compiler_params=pltpu.CompilerParams(has_side_effects=pltpu.SideEffectType.DATAFLOW_SIDE_EFFECTING),
        )(*[pltpu.with_memory_space_constraint(a, pltpu.HBM) for a in arrays], *extra)
        self.send_sems, self.recv_sems = outs[0], outs[1]
        self.arrays, self.token = list(outs[2:2 + n]), outs[2 + n]

    def wait(self, after):
        n, plan = self.n, self.plan

        def body(*refs):
            in_refs, send_sems, recv_sems = refs[:n], refs[n], refs[n + 1]
            for k, (src, dst, to) in enumerate(plan(_position(), in_refs)):
                cp = pltpu.make_async_remote_copy(src_ref=src, dst_ref=dst, send_sem=send_sems.at[k],
                                                  recv_sem=recv_sems.at[k], device_id=to, device_id_type=MESH_T)
                cp.wait_send()
                cp.wait_recv()

        outs = pl.pallas_call(
            body, name=self.name + "_wait",
            out_shape=tuple(pltpu.HBM(a.shape, a.dtype) for a in self.arrays),
            in_specs=[HBM] * n + [SEM, SEM, ANY], out_specs=tuple([HBM] * n),
            input_output_aliases={i: i for i in range(n)},
            compiler_params=pltpu.CompilerParams(has_side_effects=pltpu.SideEffectType.DATAFLOW_SIDE_EFFECTING),
        )(*self.arrays, self.send_sems, self.recv_sems, after)
        return list(outs)


def gather_start(shards, me, name, after=None):
    n = len(shards)
    lands = [lax.dynamic_update_slice(lax.empty((N_DEV, *s.shape), s.dtype), s[None], (me, 0, 0)) for s in shards]

    def plan(pos, refs):
        x, y, c = pos
        return [(refs[a], refs[n + a].at[4 * x + 2 * y + c], (x ^ mx, y ^ my, c))
                for a in range(n) for mx, my in OTHER_CHIPS]

    return SplitCopy(name, list(shards) + lands, 3 * n, plan, after=after)


def sibling_start(lands, name):
    n = len(lands)

    def plan(pos, refs):
        x, y, c = pos
        return [(refs[a].at[2 * q + c], refs[a].at[2 * q + c], (x, y, 1 - c)) for a in range(n) for q in range(4)]

    return SplitCopy(name, list(lands), 4 * n, plan)


def scatter_start(slabs, name):
    n = len(slabs)
    lands = [lax.empty((N_DEV - 1, *g.shape[1:]), g.dtype) for g in slabs]

    def plan(pos, refs):
        x, y, c = pos
        copies = []
        for a in range(n):
            for r in range(1, N_DEV):
                px, py, pc = x ^ ((r >> 2) & 1), y ^ ((r >> 1) & 1), c ^ (r & 1)
                copies.append((refs[a].at[4 * px + 2 * py + pc], refs[n + a].at[r - 1], (px, py, pc)))
        return copies

    return SplitCopy(name, list(slabs) + lands, (N_DEV - 1) * n, plan)


def columns_from_slabs(g, name):
    n, R, c = g.shape
    tr = 256

    def body(g_ref, o_ref):
        for d in range(n):
            o_ref[:, c * d:c * (d + 1)] = g_ref[d]

    return pl.pallas_call(
        body, name=name, grid=(R // tr,),
        in_specs=[pl.BlockSpec((n, tr, c), lambda i: (0, i, 0))],
        out_specs=pl.BlockSpec((tr, n * c), lambda i: (i, 0)),
        out_shape=jax.ShapeDtypeStruct((R, n * c), g.dtype),
        compiler_params=_cparams(("parallel",)),
    )(g)


def all_reduce_small(v, name, dep=None):
    R, W = v.shape

    def body(v_ref, o_ref, recv, send_sems, recv_sems):
        x, y, c = _position()
        me = 4 * x + 2 * y + c
        copies = []
        for r in range(1, N_DEV):
            to = (x ^ ((r >> 2) & 1), y ^ ((r >> 1) & 1), c ^ (r & 1))
            copies.append(pltpu.make_async_remote_copy(
                src_ref=v_ref, dst_ref=recv.at[me], send_sem=send_sems.at[r - 1], recv_sem=recv_sems.at[r - 1],
                device_id=to, device_id_type=MESH_T))
        for cp in copies:
            cp.start()
        recv[pl.ds(me, 1)] = v_ref[...][None]
        for cp in copies:
            cp.wait()
        acc = recv[0]
        for s in range(1, N_DEV):
            acc = acc + recv[s]
        o_ref[...] = acc

    return _pallas_after(
        dep, body, v, name=name,
        in_specs=[pl.BlockSpec(memory_space=pltpu.VMEM)], out_specs=pl.BlockSpec(memory_space=pltpu.VMEM),
        out_shape=jax.ShapeDtypeStruct((R, W), F32),
        scratch_shapes=[pltpu.VMEM((N_DEV, R, W), F32), pltpu.SemaphoreType.DMA((N_DEV - 1,)),
                        pltpu.SemaphoreType.DMA((N_DEV - 1,))],
    )


def _row_tile(rows):
    if rows <= 512:
        return rows
    return max(t for t in range(8, 513, 8) if rows % t == 0)


def _adamw_update(w, g, m, v):
    mn = ADAM_B1 * m + (1.0 - ADAM_B1) * g
    vn = ADAM_B2 * v + (1.0 - ADAM_B2) * (g * g)
    m_hat = mn / (1.0 - ADAM_B1 ** ADAM_STEP)
    v_hat = vn / (1.0 - ADAM_B2 ** ADAM_STEP)
    return -ADAM_LR * (m_hat / (jnp.sqrt(v_hat) + ADAM_EPS) + ADAM_WD * w), mn, vn


def adamw(w, g, m, v, name):
    R, W = w.shape
    tr = _row_tile(R)

    def body(w_ref, g_ref, m_ref, v_ref, d_ref, mo_ref, vo_ref):
        d_ref[...], mo_ref[...], vo_ref[...] = _adamw_update(w_ref[...], g_ref[...], m_ref[...], v_ref[...])

    spec = pl.BlockSpec((tr, W), lambda i: (i, 0))
    return pl.pallas_call(
        body, name=name, grid=(R // tr,), in_specs=[spec] * 4, out_specs=[spec] * 3,
        out_shape=[jax.ShapeDtypeStruct((R, W), F32)] * 3,
        compiler_params=_cparams(("parallel",)),
    )(w, g, m, v)


def reduce_adamw(slabs, land, w, m, v, me_arr, name):
    _, R, W = slabs.shape
    tr = _row_tile(R)

    def body(me_ref, s_ref, l_ref, w_ref, m_ref, v_ref, g_ref, d_ref, mo_ref, vo_ref):
        g = s_ref[0].astype(F32)
        for r in range(N_DEV - 1):
            g = g + l_ref[r].astype(F32)
        g_ref[...] = g
        d_ref[...], mo_ref[...], vo_ref[...] = _adamw_update(w_ref[...], g, m_ref[...], v_ref[...])

    spec = pl.BlockSpec((tr, W), lambda i, me: (i, 0))
    return pl.pallas_call(
        body, name=name,
        grid_spec=pltpu.PrefetchScalarGridSpec(
            num_scalar_prefetch=1, grid=(R // tr,),
            in_specs=[pl.BlockSpec((1, tr, W), lambda i, me: (me[0], i, 0)),
                      pl.BlockSpec((N_DEV - 1, tr, W), lambda i, me: (0, i, 0)), spec, spec, spec],
            out_specs=[spec] * 4),
        out_shape=[jax.ShapeDtypeStruct((R, W), F32)] * 4,
        compiler_params=_cparams(("parallel",)),
    )(me_arr, slabs, land, w, m, v)


SMALL_NAMES = ("norm1_g", "q_norm_g", "k_norm_g", "sinks", "conv_out_g", "attn_out_g", "norm2_g", "conv_w")
SMALL_SIZES = (D_MODEL, HEAD, HEAD, N_Q, CONV_CH, ATTN_W, D_MODEL, 3 * CONV_CH)
SMALL_ROWS = 80


def kernel(x, norm1_g, w_in, conv_w, q_norm_g, k_norm_g, sinks, conv_out_g, attn_out_g, w_o, norm2_g, w_gate, w_up, w_down, loss_target, m_norm1_g, m_w_in, m_conv_w, m_q_norm_g, m_k_norm_g, m_sinks, m_conv_out_g, m_attn_out_g, m_w_o, m_norm2_g, m_w_gate, m_w_up, m_w_down, v_norm1_g, v_w_in, v_conv_w, v_q_norm_g, v_k_norm_g, v_sinks, v_conv_out_g, v_attn_out_g, v_w_o, v_norm2_g, v_w_gate, v_w_up, v_w_down):
    xi, yi, ci = _position()
    me = 4 * xi + 2 * yi + ci
    me_arr = jnp.reshape(me, (1,)).astype(jnp.int32)
    xs, tgt = x[0], loss_target[0]
    bf = lambda a: a.astype(BF16)
    bias = band_bias()

    g_in0, g_o0, g_conv = all_gather([bf(w_in[0]), bf(w_o[0]), conv_w.reshape(DEPTH * 3, HEAD)], "gather_first")
    ag_ffn0 = gather_start([bf(w_gate[0]), bf(w_up[0]), bf(w_down[0])], me, "gather_ffn0", after=g_in0)
    ag_l1 = gather_start([bf(w_in[1]), bf(w_o[1]), bf(w_gate[1]), bf(w_up[1]), bf(w_down[1])], me, "gather_layer1",
                         after=ag_ffn0.token)
    conv_full = g_conv.reshape(N_DEV, DEPTH, 3, HEAD).transpose(1, 2, 0, 3).reshape(DEPTH, 3, CONV_CH)
    pair_gain = lambda g: jnp.tile(g[None], (1, 2))
    small = [dict(norm1_g=norm1_g[l][None], conv_w=conv_full[l], q_norm_g=pair_gain(q_norm_g[l]),
                  k_norm_g=pair_gain(k_norm_g[l]), sinks=sinks[l], conv_out_g=conv_out_g[l][None],
                  attn_out_g=attn_out_g[l][None], norm2_g=norm2_g[l][None]) for l in range(DEPTH)]
    weights = [dict(w_in=columns_from_slabs(g_in0, "layout_w_in0"), w_o=g_o0.reshape(D_MODEL, D_MODEL)), {}]

    saved = []
    xl = xs
    for l in range(DEPTH):
        sp, wl = small[l], weights[l]
        h, proj = norm_proj(xl, sp["norm1_g"], wl["w_in"], f"norm_proj{l}", dep=ag_l1.token if l == 0 else None)
        mix, y, ao = mixer_fwd(proj, bias, sp["sinks"], sp["conv_w"], sp["q_norm_g"], sp["k_norm_g"],
                               sp["conv_out_g"], sp["attn_out_g"], f"mixer_fwd{l}")
        dep = None
        if l == 0:
            pass_ffn0 = sibling_start(ag_ffn0.wait(mix)[3:], "pass_ffn0")
            dep = pass_ffn0.token
        x1 = matmul_residual(mix, wl["w_o"], xl, f"out_proj{l}", dep=dep)
        if l == 0:
            g_gate, g_up, g_down = pass_ffn0.wait(x1)
            wl.update(w_gate=columns_from_slabs(g_gate, "layout_w_gate0"),
                      w_up=columns_from_slabs(g_up, "layout_w_up0"), w_down=g_down.reshape(D_FF, D_MODEL))
            pass_l1 = sibling_start(ag_l1.wait(x1)[5:], "pass_layer1")
            dep = pass_l1.token
        h2, a, b, x2 = ffn_fwd(x1, sp["norm2_g"], wl["w_gate"], wl["w_up"], wl["w_down"], f"ffn_fwd{l}", dep=dep)
        if l == 0:
            g_in, g_o, g_gate, g_up, g_down = pass_l1.wait(x2)
            weights[1] = dict(w_in=columns_from_slabs(g_in, "layout_w_in1"), w_o=g_o.reshape(D_MODEL, D_MODEL),
                              w_gate=columns_from_slabs(g_gate, "layout_w_gate1"),
                              w_up=columns_from_slabs(g_up, "layout_w_up1"), w_down=g_down.reshape(D_FF, D_MODEL))
        saved.append((xl, h, proj, mix, y, ao, x1, h2, a, b))
        xl = x2
    loss_row, dx = loss_head(xl, tgt, "loss_head")

    shard = dict(w_in=(w_in, m_w_in, v_w_in), w_o=(w_o, m_w_o, v_w_o), w_gate=(w_gate, m_w_gate, v_w_gate),
                 w_up=(w_up, m_w_up, v_w_up), w_down=(w_down, m_w_down, v_w_down))
    stepped = {n: [None] * DEPTH for n in shard}
    gsmall = [None] * DEPTH

    def finish(sc, names, after, l):
        arrays = sc.wait(after)
        k = len(names)
        for i, n in enumerate(names):
            w, m, v = shard[n]
            stepped[n][l] = reduce_adamw(arrays[i], arrays[k + i], w[l], m[l], v[l], me_arr, f"reduce_adamw_{n}{l}")

    for l in reversed(range(DEPTH)):
        sp, wl = small[l], weights[l]
        x0, h, proj, mix, y, ao, x1, h2, a, b = saved[l]
        (d_wd,) = grad_weight((a, b), [dx], TF, D_MODEL, f"grad_w_down{l}", swiglu=True)
        sc_down = scatter_start([d_wd.reshape(N_DEV, ROWS_FF, D_MODEL)], f"scatter_w_down{l}")
        da, db, dx1, d_g2 = ffn_bwd(dx, a, b, x1, sp["norm2_g"], wl["w_gate"], wl["w_up"], wl["w_down"],
                                    f"ffn_bwd{l}", dep=sc_down.token)
        d_wg, d_wu = grad_weight((h2,), [da, db], D_MODEL, TF, f"grad_w_gate_up{l}", slab_cols=ROWS_FF)
        sc_gu = scatter_start([d_wg, d_wu], f"scatter_w_gate_up{l}")
        (d_wo,) = grad_weight((mix,), [dx1], D_MODEL, D_MODEL, f"grad_w_o{l}", dep=sc_gu.token)
        sc_o = scatter_start([d_wo.reshape(N_DEV, ROWS_O, D_MODEL)], f"scatter_w_o{l}")
        dmix = matmul_nt(dx1, wl["w_o"], f"out_proj_bwd{l}", dep=sc_o.token)
        finish(sc_down, ["w_down"], dmix, l)
        dproj, d_cw, d_qg, d_kg, d_sk, d_cog, d_aog = mixer_bwd(
            proj, bias, y, ao, dmix, sp["sinks"], sp["conv_w"], sp["q_norm_g"], sp["k_norm_g"],
            sp["conv_out_g"], sp["attn_out_g"], f"mixer_bwd{l}")
        finish(sc_gu, ["w_gate", "w_up"], dproj, l)
        finish(sc_o, ["w_o"], dproj, l)
        (d_win,) = grad_weight((h,), [dproj], D_MODEL, IN_COLS // 2, f"grad_w_in{l}", slab_cols=ROWS_IN)
        sc_in = scatter_start([d_win], f"scatter_w_in{l}")
        dx, d_g1 = proj_bwd(dproj, wl["w_in"], x0, sp["norm1_g"], dx1, f"proj_bwd{l}", dep=sc_in.token)
        finish(sc_in, ["w_in"], dx, l)
        both_heads = lambda d: d[:, :HEAD] + d[:, HEAD:]
        gsmall[l] = dict(norm1_g=d_g1, conv_w=d_cw, q_norm_g=both_heads(d_qg), k_norm_g=both_heads(d_kg), sinks=d_sk,
                         conv_out_g=d_cog, attn_out_g=d_aog, norm2_g=d_g2)
    grad_x = dx[None]

    flat = jnp.concatenate([loss_row[0, 0:1]] + [gsmall[l][n].reshape(-1) for l in range(DEPTH) for n in SMALL_NAMES])
    flat = jnp.pad(flat, (0, SMALL_ROWS * 128 - flat.shape[0])).reshape(SMALL_ROWS, 128)
    flat = all_reduce_small(flat, "all_reduce_small_grads").reshape(-1)
    loss = flat[0]
    gs = {n: [] for n in SMALL_NAMES}
    off = 1
    for l in range(DEPTH):
        for n, size in zip(SMALL_NAMES, SMALL_SIZES):
            gs[n].append(flat[off:off + size])
            off += size
    gs = {n: jnp.stack(v) for n, v in gs.items()}
    g_conv = lax.dynamic_slice(gs["conv_w"].reshape(DEPTH, 3, CONV_CH), (0, 0, me * HEAD), (DEPTH, 3, HEAD))

    gs["conv_w"] = g_conv
    params = dict(norm1_g=(norm1_g, m_norm1_g, v_norm1_g), conv_w=(conv_w, m_conv_w, v_conv_w),
                  q_norm_g=(q_norm_g, m_q_norm_g, v_q_norm_g), k_norm_g=(k_norm_g, m_k_norm_g, v_k_norm_g),
                  sinks=(sinks, m_sinks, v_sinks), conv_out_g=(conv_out_g, m_conv_out_g, v_conv_out_g),
                  attn_out_g=(attn_out_g, m_attn_out_g, v_attn_out_g), norm2_g=(norm2_g, m_norm2_g, v_norm2_g))
    names = ("norm1_g", "w_in", "conv_w", "q_norm_g", "k_norm_g", "sinks", "conv_out_g", "attn_out_g", "w_o",
             "norm2_g", "w_gate", "w_up", "w_down")

    out = {}
    for n in names:
        if n in shard:
            out[n] = tuple(jnp.stack([stepped[n][l][i] for l in range(DEPTH)]) for i in range(4))
        else:
            w, m, v = params[n]
            two_d = (-1, w.shape[-1])
            d, mn, vn = adamw(w.reshape(two_d), gs[n].reshape(two_d), m.reshape(two_d), v.reshape(two_d), f"adamw_{n}")
            out[n] = (gs[n].reshape(w.shape), d.reshape(w.shape), mn.reshape(w.shape), vn.reshape(w.shape))
    return (loss, grad_x, *[out[n][i] for i in range(4) for n in names])
```

```python
import functools

import jax
import jax.numpy as jnp
from jax import lax
from jax.experimental import pallas as pl
from jax.experimental.pallas import tpu as pltpu

F32 = jnp.float32
BF16 = jnp.bfloat16

D_MODEL = 1024
CONV_CH = 512
ATTN_W = 512
N_Q = 8
N_KV = 2
GRP = N_Q // N_KV
HEAD = 64
IN_COLS = 2304
D_FF = 2816
BLK = 128
O_Q = 3 * CONV_CH
O_K = O_Q + ATTN_W
O_V = O_K + N_KV * HEAD
EPS = 1e-6
NEG_INF = -1e30
SCALE = HEAD ** -0.5
N_DEV = 8
DEPTH = 2

ADAM_LR = 0.001
ADAM_B1 = 0.9
ADAM_B2 = 0.999
ADAM_EPS = 1e-08
ADAM_WD = 0.01
ADAM_STEP = 10

VMEM_LIMIT = 56 * 1024 * 1024
TM = 512
MESH_T = pl.DeviceIdType.MESH

ROWS_IN, ROWS_O, ROWS_FF = IN_COLS // N_DEV, D_MODEL // N_DEV, D_FF // N_DEV


ANY = pl.BlockSpec(memory_space=pl.ANY)
HBM = pl.BlockSpec(memory_space=pltpu.HBM)
SEM = pl.BlockSpec(memory_space=pltpu.SEMAPHORE)


def _cparams(sem):
    return pltpu.CompilerParams(dimension_semantics=sem, vmem_limit_bytes=VMEM_LIMIT)


def _pallas_after(dep, body, *args, in_specs, **kw):
    if dep is None:
        return pl.pallas_call(body, in_specs=in_specs, **kw)(*args)

    def after_dep(dep_ref, *refs):
        body(*refs)

    return pl.pallas_call(after_dep, in_specs=[ANY, *in_specs], **kw)(dep, *args)


def _dot(a, b):
    return jnp.dot(a, b, preferred_element_type=F32)


def _dot_nt(a, b):
    return lax.dot_general(a, b, (((1,), (1,)), ((), ())), preferred_element_type=F32)


def _dot_tn(a, b):
    return lax.dot_general(a, b, (((0,), (0,)), ((), ())), preferred_element_type=F32)


LANES = 128


def _row_reduce(v, op, reduce):
    w = v.shape[-1]
    if w > LANES and w % LANES == 0:
        acc = v[:, 0:LANES]
        for c in range(1, w // LANES):
            acc = op(acc, v[:, LANES * c:LANES * (c + 1)])
        v = acc
    return reduce(v, axis=-1, keepdims=True)


def _row_sum(v):
    return _row_reduce(v, jnp.add, jnp.sum)


def _row_mean(v):
    return _row_sum(v) * (1.0 / v.shape[-1])


def _rms(v):
    return lax.rsqrt(_row_mean(v * v) + EPS)


def _sigmoid(v):
    return 0.5 * jnp.tanh(0.5 * v) + 0.5


def _rms_bwd(dyv, xh, r, g):
    dxh = dyv * g
    return r * (dxh - xh * _row_mean(dxh * xh))


def norm_proj(x, g, w, name, dep=None):
    S, N = x.shape[0], w.shape[1]

    def body(x_ref, g_ref, w_ref, h_ref, p_ref):
        xf = x_ref[...]
        h = ((xf * _rms(xf)) * g_ref[...]).astype(BF16)
        h_ref[...] = h
        p_ref[...] = _dot(h, w_ref[...]).astype(BF16)

    return _pallas_after(
        dep, body, x, g, w, name=name, grid=(S // TM,),
        in_specs=[pl.BlockSpec((TM, D_MODEL), lambda i: (i, 0)),
                  pl.BlockSpec((1, D_MODEL), lambda i: (0, 0)),
                  pl.BlockSpec((D_MODEL, N), lambda i: (0, 0))],
        out_specs=[pl.BlockSpec((TM, D_MODEL), lambda i: (i, 0)),
                   pl.BlockSpec((TM, N), lambda i: (i, 0))],
        out_shape=[jax.ShapeDtypeStruct((S, D_MODEL), BF16), jax.ShapeDtypeStruct((S, N), BF16)],
        compiler_params=_cparams(("parallel",)),
    )


def band_bias():
    qi = lax.broadcasted_iota(jnp.int32, (BLK, 2 * BLK), 0)
    kj = lax.broadcasted_iota(jnp.int32, (BLK, 2 * BLK), 1)
    diff = qi + BLK - kj
    valid = (diff >= 0) & (diff < BLK)
    return jnp.stack([jnp.where(valid & (kj >= BLK), 0.0, NEG_INF), jnp.where(valid, 0.0, NEG_INF)]).astype(F32)


def _softmax_with_sink(s, sink):
    m = jnp.maximum(_row_reduce(s, jnp.maximum, jnp.max), sink)
    p = jnp.exp(s - m)
    es = jnp.exp(sink - m)
    inv = 1.0 / (_row_sum(p) + es)
    return p * inv, es * inv


PAIR = 2 * HEAD


def _low_half():
    return lax.broadcasted_iota(jnp.int32, (1, PAIR), 1) < HEAD


def _half_sums(v, low):
    return (jnp.sum(jnp.where(low, v, 0.0), axis=-1, keepdims=True),
            jnp.sum(jnp.where(low, 0.0, v), axis=-1, keepdims=True))


def _pair_mean(v, low):
    e, o = _half_sums(v, low)
    return jnp.where(low, e, o) * (1.0 / HEAD)


def _pair_rms(v, low):
    return lax.rsqrt(_pair_mean(v * v, low) + EPS)


def _one_head_in_both_halves(pair, low):
    swapped = pltpu.roll(pair, HEAD, 1)
    return jnp.where(low, pair, swapped), jnp.where(low, swapped, pair)


def _kv_pairs(p_ref, kvp_ref):
    k = jnp.concatenate([kvp_ref[:, 0:PAIR], p_ref[:, O_K:O_K + PAIR]], axis=0).astype(F32)
    v = jnp.concatenate([kvp_ref[:, PAIR:2 * PAIR], p_ref[:, O_V:O_V + PAIR]], axis=0)
    return k, v


def mixer_fwd(proj, bias, sinks, conv_w, qg, kg, cog, aog, name):
    S = proj.shape[0]
    nb = S // BLK

    def body(sinks_ref, p_ref, kvp_ref, bias_ref, cw_ref, qg_ref, kg_ref, cog_ref, aog_ref,
             mix_ref, y_ref, ao_ref, ucar):
        n = pl.program_id(0)

        @pl.when(n == 0)
        def _():
            ucar[...] = jnp.zeros_like(ucar)

        bg = p_ref[:, 0:CONV_CH].astype(F32)
        u = p_ref[:, CONV_CH:2 * CONV_CH].astype(F32) * p_ref[:, 2 * CONV_CH:3 * CONV_CH].astype(F32)
        row = lax.broadcasted_iota(jnp.int32, (BLK, CONV_CH), 0)
        prev = ucar[...]
        u1 = jnp.where(row == 0, prev[7:8, :], pltpu.roll(u, 1, 0))
        u2 = jnp.where(row == 0, prev[6:7, :], jnp.where(row == 1, prev[7:8, :], pltpu.roll(u, 2, 0)))
        ucar[...] = u[BLK - 8:BLK, :]
        y = cw_ref[0:1, :] * u2 + cw_ref[1:2, :] * u1 + cw_ref[2:3, :] * u
        y_ref[...] = y.astype(BF16)
        co = bg * y
        mix_ref[:, 0:CONV_CH] = ((co * _rms(co)) * cog_ref[...]).astype(BF16)

        low = _low_half()
        q_gain = qg_ref[...] * SCALE
        k, v = _kv_pairs(p_ref, kvp_ref)
        kn = ((k * _pair_rms(k, low)) * kg_ref[...]).astype(BF16)
        k_of = _one_head_in_both_halves(kn, low)
        v_of = _one_head_in_both_halves(v, low)
        scores = []
        for pair in range(N_Q // 2):
            q = p_ref[:, O_Q + PAIR * pair:O_Q + PAIR * (pair + 1)].astype(F32)
            qn = ((q * _pair_rms(q, low)) * q_gain).astype(BF16)
            for half in range(2):
                q_one = jnp.where(low if half == 0 else jnp.logical_not(low), qn, jnp.zeros_like(qn))
                scores.append(_dot_nt(q_one, k_of[pair // (GRP // 2)]) + bias_ref[0])
        probs = [_softmax_with_sink(s, sinks_ref[hq])[0].astype(BF16) for hq, s in enumerate(scores)]
        o = [_dot(pn, v_of[hq // GRP]) for hq, pn in enumerate(probs)]
        ao = jnp.concatenate([jnp.where(low, o[2 * pair], o[2 * pair + 1]) for pair in range(N_Q // 2)], axis=1)
        ao_ref[...] = ao.astype(BF16)
        mix_ref[:, CONV_CH:] = ((ao * _rms(ao)) * aog_ref[...]).astype(BF16)

    small = lambda shape: pl.BlockSpec(shape, lambda n: (0, 0))
    return pl.pallas_call(
        body, name=name, grid=(nb,),
        in_specs=[pl.BlockSpec(memory_space=pltpu.SMEM),
                  pl.BlockSpec((BLK, IN_COLS), lambda n: (n, 0)),
                  pl.BlockSpec((BLK, 2 * N_KV * HEAD), lambda n: (jnp.maximum(n - 1, 0), O_K // (2 * N_KV * HEAD))),
                  pl.BlockSpec((1, BLK, 2 * BLK), lambda n: (jnp.minimum(n, 1), 0, 0)),
                  small((3, CONV_CH)), small((1, PAIR)), small((1, PAIR)), small((1, CONV_CH)), small((1, ATTN_W))],
        out_specs=[pl.BlockSpec((BLK, D_MODEL), lambda n: (n, 0)),
                   pl.BlockSpec((BLK, CONV_CH), lambda n: (n, 0)),
                   pl.BlockSpec((BLK, ATTN_W), lambda n: (n, 0))],
        out_shape=[jax.ShapeDtypeStruct((S, D_MODEL), BF16), jax.ShapeDtypeStruct((S, CONV_CH), BF16),
                   jax.ShapeDtypeStruct((S, ATTN_W), BF16)],
        scratch_shapes=[pltpu.VMEM((8, CONV_CH), F32)],
        compiler_params=_cparams(("arbitrary",)),
    )(sinks, proj, proj, bias, conv_w, qg, kg, cog, aog)


def matmul_residual(a, w, res, name, dep=None):
    S, K = a.shape
    N = w.shape[1]

    def body(a_ref, w_ref, r_ref, o_ref):
        o_ref[...] = r_ref[...] + _dot(a_ref[...], w_ref[...])

    return _pallas_after(
        dep, body, a, w, res, name=name, grid=(S // TM,),
        in_specs=[pl.BlockSpec((TM, K), lambda i: (i, 0)), pl.BlockSpec((K, N), lambda i: (0, 0)),
                  pl.BlockSpec((TM, N), lambda i: (i, 0))],
        out_specs=pl.BlockSpec((TM, N), lambda i: (i, 0)),
        out_shape=jax.ShapeDtypeStruct((S, N), F32),
        compiler_params=_cparams(("parallel",)),
    )


TF = 1408


def ffn_fwd(x1, g, wg, wu, wd, name, dep=None, tgt=None):
    S = x1.shape[0]
    ni, nj = S // TM, D_FF // TF
    with_loss = tgt is not None

    def body(x_ref, g_ref, wg_ref, wu_ref, wd_ref, *rest):
        if with_loss:
            t_ref, h2_ref, a_ref, b_ref, o_ref, l_ref, acc, sq = rest
        else:
            h2_ref, a_ref, b_ref, o_ref, acc = rest
        i, j = pl.program_id(0), pl.program_id(1)

        @pl.when(j == 0)
        def _():
            xf = x_ref[...]
            h2_ref[...] = ((xf * _rms(xf)) * g_ref[...]).astype(BF16)
            acc[...] = xf

        h2 = h2_ref[...]
        a = _dot(h2, wg_ref[...])
        b = _dot(h2, wu_ref[...])
        a_ref[...] = a.astype(BF16)
        b_ref[...] = b.astype(BF16)
        f = (a * _sigmoid(a)) * b
        acc[...] += _dot(f.astype(BF16), wd_ref[...])

        @pl.when(j == nj - 1)
        def _():
            if not with_loss:
                o_ref[...] = acc[...]
                return
            e = acc[...] - t_ref[...]
            o_ref[...] = e * (1.0 / D_MODEL)
            col = jnp.sum(e * e, axis=0, keepdims=True)

            @pl.when(i == 0)
            def _():
                sq[...] = col

            @pl.when(i > 0)
            def _():
                sq[...] += col

            @pl.when(i == ni - 1)
            def _():
                l_ref[...] = jnp.full((1, 128), jnp.sum(sq[...]) * (0.5 / D_MODEL), F32)

    row = pl.BlockSpec((TM, D_MODEL), lambda i, j: (i, 0))
    tile = pl.BlockSpec((TM, TF), lambda i, j: (i, j))
    return _pallas_after(
        dep, body, x1, g, wg, wu, wd, *([tgt] if with_loss else []), name=name, grid=(ni, nj),
        in_specs=[row, pl.BlockSpec((1, D_MODEL), lambda i, j: (0, 0)),
                  pl.BlockSpec((D_MODEL, TF), lambda i, j: (0, j)), pl.BlockSpec((D_MODEL, TF), lambda i, j: (0, j)),
                  pl.BlockSpec((TF, D_MODEL), lambda i, j: (j, 0))] + ([row] if with_loss else []),
        out_specs=[row, tile, tile, row] + ([pl.BlockSpec((1, 128), lambda i, j: (0, 0))] if with_loss else []),
        out_shape=[jax.ShapeDtypeStruct((S, D_MODEL), BF16), jax.ShapeDtypeStruct((S, D_FF), BF16),
                   jax.ShapeDtypeStruct((S, D_FF), BF16), jax.ShapeDtypeStruct((S, D_MODEL), F32)]
        + ([jax.ShapeDtypeStruct((1, 128), F32)] if with_loss else []),
        scratch_shapes=[pltpu.VMEM((TM, D_MODEL), F32)] + ([pltpu.VMEM((1, D_MODEL), F32)] if with_loss else []),
        compiler_params=_cparams(("arbitrary", "arbitrary") if with_loss else ("parallel", "arbitrary")),
    )


def ffn_bwd(dx2, a, b, x1, g, wg, wu, wd, name, dep=None):
    S = dx2.shape[0]
    nj = D_FF // TF

    def body(dx_ref, a_ref, b_ref, x_ref, g_ref, wg_ref, wu_ref, wd_ref, da_ref, db_ref, dx1_ref, dg_ref, dxb, acc):
        i, j = pl.program_id(0), pl.program_id(1)

        @pl.when((i == 0) & (j == 0))
        def _():
            dg_ref[...] = jnp.zeros_like(dg_ref)

        @pl.when(j == 0)
        def _():
            dxb[...] = dx_ref[...].astype(BF16)
            acc[...] = jnp.zeros_like(acc)

        df = _dot_nt(dxb[...], wd_ref[...])
        av = a_ref[...].astype(F32)
        bv = b_ref[...].astype(F32)
        sg = _sigmoid(av)
        da =((df * bv) * (sg * (1.0 + av * (1.0 - sg)))).astype(BF16)
        db = (df * (av * sg)).astype(BF16)
        da_ref[...] = da
        db_ref[...] = db
        acc[...] += _dot_nt(da, wg_ref[...]) + _dot_nt(db, wu_ref[...])

        @pl.when(j == nj - 1)
        def _():
            xf = x_ref[...]
            r = _rms(xf)
            xh = xf * r
            dh = acc[...]
            dg_ref[...] += jnp.sum(dh * xh, axis=0, keepdims=True)
            dx1_ref[...] = dx_ref[...] + _rms_bwd(dh, xh, r, g_ref[...])

    return _pallas_after(
        dep, body, dx2, a, b, x1, g, wg, wu, wd, name=name, grid=(S // TM, nj),
        in_specs=[pl.BlockSpec((TM, D_MODEL), lambda i, j: (i, 0)),
                  pl.BlockSpec((TM, TF), lambda i, j: (i, j)), pl.BlockSpec((TM, TF), lambda i, j: (i, j)),
                  pl.BlockSpec((TM, D_MODEL), lambda i, j: (i, 0)), pl.BlockSpec((1, D_MODEL), lambda i, j: (0, 0)),
                  pl.BlockSpec((D_MODEL, TF), lambda i, j: (0, j)), pl.BlockSpec((D_MODEL, TF), lambda i, j: (0, j)),
                  pl.BlockSpec((TF, D_MODEL), lambda i, j: (j, 0))],
        out_specs=[pl.BlockSpec((TM, TF), lambda i, j: (i, j)), pl.BlockSpec((TM, TF), lambda i, j: (i, j)),
                   pl.BlockSpec((TM, D_MODEL), lambda i, j: (i, 0)), pl.BlockSpec((1, D_MODEL), lambda i, j: (0, 0))],
        out_shape=[jax.ShapeDtypeStruct((S, D_FF), BF16), jax.ShapeDtypeStruct((S, D_FF), BF16),
                   jax.ShapeDtypeStruct((S, D_MODEL), F32), jax.ShapeDtypeStruct((1, D_MODEL), F32)],
        scratch_shapes=[pltpu.VMEM((TM, D_MODEL), BF16), pltpu.VMEM((TM, D_MODEL), F32)],
        compiler_params=_cparams(("arbitrary", "arbitrary")),
    )


def grad_weight(lhs, rhs, tm, tn, name, swiglu=False, slab_cols=None, dep=None):
    S = rhs[0].shape[0]
    M = lhs[0].shape[1]
    N = rhs[0].shape[1]
    tk = 1024
    nk = S // tk
    n_l, n_r = len(lhs), len(rhs)
    if slab_cols is None:
        out_spec = pl.BlockSpec((tm, tn), lambda i, j, k: (i, j))
        out_shape = jax.ShapeDtypeStruct((M, N), BF16)
    else:
        out_spec = pl.BlockSpec((tn // slab_cols, tm, slab_cols), lambda i, j, k: (j, i, 0))
        out_shape = jax.ShapeDtypeStruct((N // slab_cols, M, slab_cols), BF16)

    def body(*refs):
        l_refs, r_refs = refs[:n_l], refs[n_l:n_l + n_r]
        o_refs, accs = refs[n_l + n_r:n_l + 2 * n_r], refs[n_l + 2 * n_r:]
        k = pl.program_id(2)
        if swiglu:
            av = l_refs[0][...].astype(F32)
            lv = ((av * _sigmoid(av)) * l_refs[1][...].astype(F32)).astype(BF16)
        else:
            lv = l_refs[0][...]
        for r_ref, o_ref, acc in zip(r_refs, o_refs, accs):
            part = _dot_tn(lv, r_ref[...].astype(BF16))

            @pl.when(k == 0)
            def _():
                acc[...] = part

            @pl.when(k > 0)
            def _():
                acc[...] += part

            @pl.when(k == nk - 1)
            def _():
                if slab_cols is None:
                    o_ref[...] = acc[...].astype(BF16)
                else:
                    for s in range(tn // slab_cols):
                        o_ref[s] = acc[:, slab_cols * s:slab_cols * (s + 1)].astype(BF16)

    return _pallas_after(
        dep, body, *lhs, *rhs, name=name, grid=(M // tm, N // tn, nk),
        in_specs=[pl.BlockSpec((tk, tm), lambda i, j, k: (k, i)) for _ in lhs]
        + [pl.BlockSpec((tk, tn), lambda i, j, k: (k, j)) for _ in rhs],
        out_specs=[out_spec for _ in rhs],
        out_shape=[out_shape for _ in rhs],
        scratch_shapes=[pltpu.VMEM((tm, tn), F32) for _ in rhs],
        compiler_params=_cparams(("parallel", "parallel", "arbitrary")),
    )


def matmul_nt(a, w, name, dep=None):
    S, K = a.shape
    N = w.shape[0]

    def body(a_ref, w_ref, o_ref):
        o_ref[...] = _dot_nt(a_ref[...].astype(BF16), w_ref[...]).astype(BF16)

    return _pallas_after(
        dep, body, a, w, name=name, grid=(S // TM,),
        in_specs=[pl.BlockSpec((TM, K), lambda i: (i, 0)), pl.BlockSpec((N, K), lambda i: (0, 0))],
        out_specs=pl.BlockSpec((TM, N), lambda i: (i, 0)),
        out_shape=jax.ShapeDtypeStruct((S, N), BF16),
        compiler_params=_cparams(("parallel",)),
    )


def proj_bwd(dproj, w, x, g, dres, name, dep=None):
    S, N = dproj.shape

    def body(dp_ref, w_ref, x_ref, g_ref, dr_ref, dx_ref, dg_ref):
        i = pl.program_id(0)

        @pl.when(i == 0)
        def _():
            dg_ref[...] = jnp.zeros_like(dg_ref)

        dh = _dot_nt(dp_ref[...], w_ref[...])
        xf = x_ref[...]
        r = _rms(xf)
        xh = xf * r
        dg_ref[...] += jnp.sum(dh * xh, axis=0, keepdims=True)
        dx_ref[...] = dr_ref[...] + _rms_bwd(dh, xh, r, g_ref[...])

    return _pallas_after(
        dep, body, dproj, w, x, g, dres, name=name, grid=(S // TM,),
        in_specs=[pl.BlockSpec((TM, N), lambda i: (i, 0)), pl.BlockSpec((D_MODEL, N), lambda i: (0, 0)),
                  pl.BlockSpec((TM, D_MODEL), lambda i: (i, 0)), pl.BlockSpec((1, D_MODEL), lambda i: (0, 0)),
                  pl.BlockSpec((TM, D_MODEL), lambda i: (i, 0))],
        out_specs=[pl.BlockSpec((TM, D_MODEL), lambda i: (i, 0)), pl.BlockSpec((1, D_MODEL), lambda i: (0, 0))],
        out_shape=[jax.ShapeDtypeStruct((S, D_MODEL), F32), jax.ShapeDtypeStruct((1, D_MODEL), F32)],
        compiler_params=_cparams(("arbitrary",)),
    )


def mixer_bwd(proj, bias, y, ao, dmix, sinks, conv_w, qg, kg, cog, aog, name):
    S = proj.shape[0]
    nb = S // BLK
    KV_W = 2 * N_KV * HEAD

    def body(sinks_ref, p_ref, kvp_ref, bias_ref, y_ref, ao_ref, dm_ref, cw_ref, qg_ref, kg_ref, cog_ref, aog_ref,
             dp_ref, dcw_ref, dqg_ref, dkg_ref, dsk_ref, dcog_ref, daog_ref,
             dycar, kcar, vcar):
        step = pl.program_id(0)

        @pl.when(step == 0)
        def _():
            dycar[...] = jnp.zeros_like(dycar)
            kcar[...] = jnp.zeros_like(kcar)
            vcar[...] = jnp.zeros_like(vcar)
            dcw_ref[...] = jnp.zeros_like(dcw_ref)
            dqg_ref[...] = jnp.zeros_like(dqg_ref)
            dkg_ref[...] = jnp.zeros_like(dkg_ref)
            dsk_ref[...] = jnp.zeros_like(dsk_ref)
            dcog_ref[...] = jnp.zeros_like(dcog_ref)
            daog_ref[...] = jnp.zeros_like(daog_ref)

        bg = p_ref[:, 0:CONV_CH].astype(F32)
        cg = p_ref[:, CONV_CH:2 * CONV_CH].astype(F32)
        hc = p_ref[:, 2 * CONV_CH:3 * CONV_CH].astype(F32)
        yv = y_ref[...].astype(F32)
        dmc = dm_ref[:, 0:CONV_CH].astype(F32)
        co = bg * yv
        rc = _rms(co)
        ch = co * rc
        dcog_ref[...] += jnp.sum(dmc * ch, axis=0, keepdims=True)
        dco = _rms_bwd(dmc, ch, rc, cog_ref[...])
        dp_ref[:, 0:CONV_CH] = (dco * yv).astype(BF16)
        dy = dco * bg
        row = lax.broadcasted_iota(jnp.int32, (BLK, CONV_CH), 0)
        nxt = dycar[...]
        dy1 = jnp.where(row == BLK - 1, nxt[0:1, :], pltpu.roll(dy, BLK - 1, 0))
        dy2 = jnp.where(row == BLK - 2, nxt[0:1, :], jnp.where(row == BLK - 1, nxt[1:2, :], pltpu.roll(dy, BLK - 2, 0)))
        dycar[...] = dy[0:8, :]
        du = cw_ref[2:3, :] * dy + cw_ref[1:2, :] * dy1 + cw_ref[0:1, :] * dy2
        dp_ref[:, CONV_CH:2 * CONV_CH] = (du * hc).astype(BF16)
        dp_ref[:, 2 * CONV_CH:3 * CONV_CH] = (du * cg).astype(BF16)
        u = cg * hc
        dcw_ref[0:1, :] += jnp.sum(dy2 * u, axis=0, keepdims=True)
        dcw_ref[1:2, :] += jnp.sum(dy1 * u, axis=0, keepdims=True)
        dcw_ref[2:3, :] += jnp.sum(dy * u, axis=0, keepdims=True)

        aov = ao_ref[...].astype(F32)
        dma = dm_ref[:, CONV_CH:].astype(F32)
        ra = _rms(aov)
        ah = aov * ra
        daog_ref[...] += jnp.sum(dma * ah, axis=0, keepdims=True)
        dao = _rms_bwd(dma, ah, ra, aog_ref[...])

        low = _low_half()
        high = jnp.logical_not(low)
        q_gain = qg_ref[...] * SCALE
        k, v = _kv_pairs(p_ref, kvp_ref)
        rk = _pair_rms(k, low)
        kh = k * rk
        k_of = _one_head_in_both_halves((kh * kg_ref[...]).astype(BF16), low)
        v_of = _one_head_in_both_halves(v, low)
        lane8 = lax.broadcasted_iota(jnp.int32, (1, N_Q), 1)
        dsk = jnp.zeros((1, N_Q), F32)
        dqg = jnp.zeros((1, PAIR), F32)
        dv_t = [jnp.zeros((PAIR, 2 * BLK), F32) for _ in range(N_KV)]
        dkn_t = [jnp.zeros((PAIR, 2 * BLK), F32) for _ in range(N_KV)]
        rq, qh, q_one, do_one, delta = [], [], [], [], []
        for pair in range(N_Q // 2):
            cols = slice(PAIR * pair, PAIR * (pair + 1))
            q = p_ref[:, O_Q + PAIR * pair:O_Q + PAIR * (pair + 1)].astype(F32)
            rq.append(_pair_rms(q, low))
            qh.append(q * rq[pair])
            qn = (qh[pair] * q_gain).astype(BF16)
            do = dao[:, cols]
            do_b = do.astype(BF16)
            delta += _half_sums(do * aov[:, cols], low)
            for mine in (low, high):
                q_one.append(jnp.where(mine, qn, jnp.zeros_like(qn)))
                do_one.append(jnp.where(mine, do_b, jnp.zeros_like(do_b)))
        heads = range(N_Q)
        scores = [_dot_nt(q_one[hq], k_of[hq // GRP]) + bias_ref[0] for hq in heads]
        dprobs = [_dot_nt(do_one[hq], v_of[hq // GRP]) for hq in heads]
        probs = [_softmax_with_sink(scores[hq], sinks_ref[hq]) for hq in heads]
        ds = [(probs[hq][0] * (dprobs[hq] - delta[hq])).astype(BF16) for hq in heads]
        for hq in heads:
            dsk = dsk - jnp.where(lane8 == hq, jnp.sum(probs[hq][1] * delta[hq]), 0.0)
            dv_t[hq // GRP] = dv_t[hq // GRP] + _dot_tn(do_one[hq], probs[hq][0].astype(BF16))
            dkn_t[hq // GRP] = dkn_t[hq // GRP] + _dot_tn(q_one[hq], ds[hq])
        dqn_of = [_dot(ds[hq], k_of[hq // GRP]) for hq in heads]
        for pair in range(N_Q // 2):
            dqn = jnp.where(low, dqn_of[2 * pair], dqn_of[2 * pair + 1])
            dqg = dqg + jnp.sum(dqn * qh[pair], axis=0, keepdims=True)
            dqh = dqn * q_gain
            dp_ref[:, O_Q + PAIR * pair:O_Q + PAIR * (pair + 1)] = (
                rq[pair] * (dqh - qh[pair] * _pair_mean(dqh * qh[pair], low))).astype(BF16)

        def untranspose(parts):
            return jnp.concatenate([t[:HEAD] + t[HEAD:] for t in parts], axis=0).T

        dv = untranspose(dv_t)
        dkn = untranspose(dkn_t)
        dkg_ref[...] += jnp.sum(dkn * kh, axis=0, keepdims=True)
        dkh = dkn * kg_ref[...]
        dk = rk * (dkh - kh * _pair_mean(dkh * kh, low))
        dp_ref[:, O_K:O_V] = (kcar[...] + dk[BLK:, :]).astype(BF16)
        dp_ref[:, O_V:] = (vcar[...] + dv[BLK:, :]).astype(BF16)
        kcar[...] = dk[:BLK, :]
        vcar[...] = dv[:BLK, :]
        dsk_ref[...] += dsk
        dqg_ref[...] += dqg * SCALE

    small = lambda shape: pl.BlockSpec(shape, lambda s: (0, 0))
    blk = lambda w: pl.BlockSpec((BLK, w), lambda s: (nb - 1 - s, 0))
    return pl.pallas_call(
        body, name=name, grid=(nb,),
        in_specs=[pl.BlockSpec(memory_space=pltpu.SMEM),
                  blk(IN_COLS),
                  pl.BlockSpec((BLK, KV_W), lambda s: (jnp.maximum(nb - 2 - s, 0), O_K // KV_W)),
                  pl.BlockSpec((1, BLK, 2 * BLK), lambda s: (jnp.minimum(nb - 1 - s, 1), 0, 0)),
                  blk(CONV_CH), blk(ATTN_W), blk(D_MODEL),
                  small((3, CONV_CH)), small((1, PAIR)), small((1, PAIR)), small((1, CONV_CH)), small((1, ATTN_W))],
        out_specs=[blk(IN_COLS), small((3, CONV_CH)), small((1, PAIR)), small((1, PAIR)), small((1, N_Q)),
                   small((1, CONV_CH)), small((1, ATTN_W))],
        out_shape=[jax.ShapeDtypeStruct((S, IN_COLS), BF16), jax.ShapeDtypeStruct((3, CONV_CH), F32),
                   jax.ShapeDtypeStruct((1, PAIR), F32), jax.ShapeDtypeStruct((1, PAIR), F32),
                   jax.ShapeDtypeStruct((1, N_Q), F32), jax.ShapeDtypeStruct((1, CONV_CH), F32),
                   jax.ShapeDtypeStruct((1, ATTN_W), F32)],
        scratch_shapes=[pltpu.VMEM((8, CONV_CH), F32), pltpu.VMEM((BLK, PAIR), F32), pltpu.VMEM((BLK, PAIR), F32)],
        compiler_params=_cparams(("arbitrary",)),
    )(sinks, proj, proj, bias, y, ao, dmix, conv_w, qg, kg, cog, aog)


OTHER_CHIPS = ((1, 0), (0, 1), (1, 1))


def _position():
    return lax.axis_index("x"), lax.axis_index("y"), lax.axis_index("c")


def all_gather(shards, name):
    n = len(shards)

    def body(*refs):
        x_refs, out_refs = refs[:n], refs[n:2 * n]
        send_sems, recv_sems, local_sems = refs[2 * n:]
        x, y, c = _position()
        me, sibling = (x, y, c), (x, y, 1 - c)
        chips = [(x ^ mx, y ^ my) for mx, my in OTHER_CHIPS]

        def slab(a, px, py, pc):
            return out_refs[a].at[4 * px + 2 * py + pc]

        def copy(a, k, block, to, src=None):
            return pltpu.make_async_remote_copy(
                src_ref=slab(a, *block) if src is None else src, dst_ref=slab(a, *block),
                send_sem=send_sems.at[7 * a + k], recv_sem=recv_sems.at[7 * a + k], device_id=to, device_id_type=MESH_T)

        mine = [pltpu.make_async_copy(x_refs[a], slab(a, *me), local_sems.at[a]) for a in range(n)]
        for cp in mine:
            cp.start()
        first = []
        for a in range(n):
            first.append(copy(a, 0, me, sibling, src=x_refs[a]))
            first += [copy(a, 1 + j, me, (*chip, c), src=x_refs[a]) for j, chip in enumerate(chips)]
        for cp in first:
            cp.start()
        passed = []
        for a in range(n):
            for j, chip in enumerate(chips):
                copy(a, 1 + j, (*chip, c), me).wait_recv()
                passed.append(copy(a, 4 + j, (*chip, c), sibling))
                passed[-1].start()
        for a in range(n):
            copy(a, 0, sibling, me).wait_recv()
            for j, chip in enumerate(chips):
                copy(a, 4 + j, (*chip, 1 - c), me).wait_recv()
        for cp in first + passed:
            cp.wait_send()
        for cp in mine:
            cp.wait()

    return pl.pallas_call(
        body, name=name, in_specs=[ANY] * n, out_specs=[ANY] * n,
        out_shape=[jax.ShapeDtypeStruct((N_DEV, *s.shape), s.dtype) for s in shards],
        scratch_shapes=[pltpu.SemaphoreType.DMA((7 * n,)), pltpu.SemaphoreType.DMA((7 * n,)),
                        pltpu.SemaphoreType.DMA((n,))],
    )(*shards)


class SplitCopy:
    def __init__(self, name, arrays, n_copies, plan, after=None):
        n = len(arrays)
        self.name, self.n, self.n_copies, self.plan = name, n, n_copies, plan
        extra = [] if after is None else [after]

        def body(*refs):
            in_refs = refs[:n]
            send_sems, recv_sems = refs[n + len(extra)], refs[n + len(extra) + 1]
            token = refs[2 * n + len(extra) + 2]
            for k, (src, dst, to) in enumerate(plan(_position(), in_refs)):
                pltpu.make_async_remote_copy(src_ref=src, dst_ref=dst, send_sem=send_sems.at[k],
                                             recv_sem=recv_sems.at[k], device_id=to, device_id_type=MESH_T).start()
            token[...] = jnp.zeros_like(token)

        outs = pl.pallas_call(
            body, name=name + "_start",
            out_shape=(pltpu.SemaphoreType.DMA((n_copies,)), pltpu.SemaphoreType.DMA((n_copies,)),
                       *[pltpu.HBM(a.shape, a.dtype) for a in arrays], jax.ShapeDtypeStruct((8, 128), F32)),
            in_specs=[HBM] * n + [ANY] * len(extra),
            out_specs=(SEM, SEM, *[HBM] * n, pl.BlockSpec(memory_space=pltpu.VMEM)),
            input_output_aliases={i: 2 + i for i in range(n)},
            compiler_params=pltpu.CompilerParams(has_side_effects=pltpu.SideEffectType.DATAFLOW_SIDE_EFFECTING),
        )(*[pltpu.with_memory_space_constraint(a, pltpu.HBM) for a in arrays], *extra)
        self.send_sems, self.recv_sems = outs[0], outs[1]
        self.arrays, self.token = list(outs[2:2 + n]), outs[2 + n]

    def wait(self, after):
        n, plan = self.n, self.plan

        def body(*refs):
            in_refs, send_sems, recv_sems = refs[:n], refs[n], refs[n + 1]
            for k, (src, dst, to) in enumerate(plan(_position(), in_refs)):
                cp = pltpu.make_async_remote_copy(src_ref=src, dst_ref=dst, send_sem=send_sems.at[k],
                                                  recv_sem=recv_sems.at[k], device_id=to, device_id_type=MESH_T)
                cp.wait_send()
                cp.wait_recv()

        outs = pl.pallas_call(
            body, name=self.name + "_wait",
            out_shape=tuple(pltpu.HBM(a.shape, a.dtype) for a in self.arrays),
            in_specs=[HBM] * n + [SEM, SEM, ANY], out_specs=tuple([HBM] * n),
            input_output_aliases={i: i for i in range(n)},
            compiler_params=pltpu.CompilerParams(has_side_effects=pltpu.SideEffectType.DATAFLOW_SIDE_EFFECTING),
        )(*self.arrays, self.send_sems, self.recv_sems, after)
        return list(outs)


def gather_start(shards, me, name, after=None):
    n = len(shards)
    lands = [lax.dynamic_update_slice(lax.empty((N_DEV, *s.shape), s.dtype), s[None], (me, 0, 0)) for s in shards]

    def plan(pos, refs):
        x, y, c = pos
        return [(refs[a], refs[n + a].at[4 * x + 2 * y + c], (x ^ mx, y ^ my, c))
                for a in range(n) for mx, my in OTHER_CHIPS]

    return SplitCopy(name, list(shards) + lands, 3 * n, plan, after=after)


def sibling_start(lands, name):
    n = len(lands)

    def plan(pos, refs):
        x, y, c = pos
        return [(refs[a].at[2 * q + c], refs[a].at[2 * q + c], (x, y, 1 - c)) for a in range(n) for q in range(4)]

    return SplitCopy(name, list(lands), 4 * n, plan)


def scatter_start(slabs, name):
    n = len(slabs)
    lands = [lax.empty((N_DEV - 1, *g.shape[1:]), g.dtype) for g in slabs]

    def plan(pos, refs):
        x, y, c = pos
        copies = []
        for a in range(n):
            for r in range(1, N_DEV):
                px, py, pc = x ^ ((r >> 2) & 1), y ^ ((r >> 1) & 1), c ^ (r & 1)
                copies.append((refs[a].at[4 * px + 2 * py + pc], refs[n + a].at[r - 1], (px, py, pc)))
        return copies

    return SplitCopy(name, list(slabs) + lands, (N_DEV - 1) * n, plan)


def columns_from_slabs(g, name):
    n, R, c = g.shape
    tr = 256

    def body(g_ref, o_ref):
        for d in range(n):
            o_ref[:, c * d:c * (d + 1)] = g_ref[d]

    return pl.pallas_call(
        body, name=name, grid=(R // tr,),
        in_specs=[pl.BlockSpec((n, tr, c), lambda i: (0, i, 0))],
        out_specs=pl.BlockSpec((tr, n * c), lambda i: (i, 0)),
        out_shape=jax.ShapeDtypeStruct((R, n * c), g.dtype),
        compiler_params=_cparams(("parallel",)),
    )(g)


def all_reduce_small(v, name, dep=None):
    R, W = v.shape

    def body(v_ref, o_ref, recv, send_sems, recv_sems):
        x, y, c = _position()
        me = 4 * x + 2 * y + c
        copies = []
        for r in range(1, N_DEV):
            to = (x ^ ((r >> 2) & 1), y ^ ((r >> 1) & 1), c ^ (r & 1))
            copies.append(pltpu.make_async_remote_copy(
                src_ref=v_ref, dst_ref=recv.at[me], send_sem=send_sems.at[r - 1], recv_sem=recv_sems.at[r - 1],
                device_id=to, device_id_type=MESH_T))
        for cp in copies:
            cp.start()
        recv[pl.ds(me, 1)] = v_ref[...][None]
        for cp in copies:
            cp.wait()
        acc = recv[0]
        for s in range(1, N_DEV):
            acc = acc + recv[s]
        o_ref[...] = acc

    return _pallas_after(
        dep, body, v, name=name,
        in_specs=[pl.BlockSpec(memory_space=pltpu.VMEM)], out_specs=pl.BlockSpec(memory_space=pltpu.VMEM),
        out_shape=jax.ShapeDtypeStruct((R, W), F32),
        scratch_shapes=[pltpu.VMEM((N_DEV, R, W), F32), pltpu.SemaphoreType.DMA((N_DEV - 1,)),
                        pltpu.SemaphoreType.DMA((N_DEV - 1,))],
    )


def _row_tile(rows):
    if rows <= 512:
        return rows
    return max(t for t in range(8, 513, 8) if rows % t == 0)


def _adamw_update(w, g, m, v):
    mn = ADAM_B1 * m + (1.0 - ADAM_B1) * g
    vn = ADAM_B2 * v + (1.0 - ADAM_B2) * (g * g)
    m_hat = mn / (1.0 - ADAM_B1 ** ADAM_STEP)
    v_hat = vn / (1.0 - ADAM_B2 ** ADAM_STEP)
    return -ADAM_LR * (m_hat / (jnp.sqrt(v_hat) + ADAM_EPS) + ADAM_WD * w), mn, vn


def adamw(w, g, m, v, name):
    R, W = w.shape
    tr = _row_tile(R)

    def body(w_ref, g_ref, m_ref, v_ref, d_ref, mo_ref, vo_ref):
        d_ref[...], mo_ref[...], vo_ref[...] = _adamw_update(w_ref[...], g_ref[...], m_ref[...], v_ref[...])

    spec = pl.BlockSpec((tr, W), lambda i: (i, 0))
    return pl.pallas_call(
        body, name=name, grid=(R // tr,), in_specs=[spec] * 4, out_specs=[spec] * 3,
        out_shape=[jax.ShapeDtypeStruct((R, W), F32)] * 3,
        compiler_params=_cparams(("parallel",)),
    )(w, g, m, v)


def reduce_adamw(slabs, land, w, m, v, layer, me_arr, name, others=None):
    _, R, W = slabs.shape
    tr = _row_tile(R)
    n_other = 0 if others is None else 4

    def body(me_ref, s_ref, l_ref, w_ref, m_ref, v_ref, *rest):
        g_ref, d_ref, mo_ref, vo_ref = rest[n_other:]
        g = s_ref[0].astype(F32)
        for r in range(N_DEV - 1):
            g = g + l_ref[r].astype(F32)
        g_ref[0] = g
        d_ref[0], mo_ref[0], vo_ref[0] = _adamw_update(w_ref[0], g, m_ref[0], v_ref[0])

    spec = pl.BlockSpec((1, tr, W), lambda i, me: (layer, i, 0))
    return pl.pallas_call(
        body, name=name,
        grid_spec=pltpu.PrefetchScalarGridSpec(
            num_scalar_prefetch=1, grid=(R // tr,),
            in_specs=[pl.BlockSpec((1, tr, W), lambda i, me: (me[0], i, 0)),
                      pl.BlockSpec((N_DEV - 1, tr, W), lambda i, me: (0, i, 0)), spec, spec, spec] + [ANY] * n_other,
            out_specs=[spec] * 4),
        out_shape=[jax.ShapeDtypeStruct((DEPTH, R, W), F32)] * 4,
        input_output_aliases={6 + i: i for i in range(n_other)},
        compiler_params=_cparams(("parallel",)),
    )(me_arr, slabs, land, w, m, v, *([] if others is None else others))


SMALL_NAMES = ("norm1_g", "q_norm_g", "k_norm_g", "sinks", "conv_out_g", "attn_out_g", "norm2_g", "conv_w")
SMALL_SIZES = (D_MODEL, HEAD, HEAD, N_Q, CONV_CH, ATTN_W, D_MODEL, 3 * CONV_CH)
SMALL_ROWS = 80


def kernel(x, norm1_g, w_in, conv_w, q_norm_g, k_norm_g, sinks, conv_out_g, attn_out_g, w_o, norm2_g, w_gate, w_up, w_down, loss_target, m_norm1_g, m_w_in, m_conv_w, m_q_norm_g, m_k_norm_g, m_sinks, m_conv_out_g, m_attn_out_g, m_w_o, m_norm2_g, m_w_gate, m_w_up, m_w_down, v_norm1_g, v_w_in, v_conv_w, v_q_norm_g, v_k_norm_g, v_sinks, v_conv_out_g, v_attn_out_g, v_w_o, v_norm2_g, v_w_gate, v_w_up, v_w_down):
    xi, yi, ci = _position()
    me = 4 * xi + 2 * yi + ci
    me_arr = jnp.reshape(me, (1,)).astype(jnp.int32)
    xs, tgt = x[0], loss_target[0]
    bf = lambda a: a.astype(BF16)
    bias = band_bias()

    g_in0, g_o0, g_conv = all_gather([bf(w_in[0]), bf(w_o[0]), conv_w.reshape(DEPTH * 3, HEAD)], "gather_first")
    ag_ffn0 = gather_start([bf(w_gate[0]), bf(w_up[0]), bf(w_down[0])], me, "gather_ffn0", after=g_in0)
    ag_mix1 = gather_start([bf(w_in[1]), bf(w_o[1])], me, "gather_mix1", after=ag_ffn0.token)
    ag_ffn1 = gather_start([bf(w_gate[1]), bf(w_up[1]), bf(w_down[1])], me, "gather_ffn1", after=ag_mix1.token)
    conv_full = g_conv.reshape(N_DEV, DEPTH, 3, HEAD).transpose(1, 2, 0, 3).reshape(DEPTH, 3, CONV_CH)
    pair_gain = lambda g: jnp.tile(g[None], (1, 2))
    small = [dict(norm1_g=norm1_g[l][None], conv_w=conv_full[l], q_norm_g=pair_gain(q_norm_g[l]),
                  k_norm_g=pair_gain(k_norm_g[l]), sinks=sinks[l], conv_out_g=conv_out_g[l][None],
                  attn_out_g=attn_out_g[l][None], norm2_g=norm2_g[l][None]) for l in range(DEPTH)]
    weights = [dict(w_in=columns_from_slabs(g_in0, "layout_w_in0"), w_o=g_o0.reshape(D_MODEL, D_MODEL)), {}]

    def ffn_weights(g_gate, g_up, g_down, l):
        return dict(w_gate=columns_from_slabs(g_gate, f"layout_w_gate{l}"),
                    w_up=columns_from_slabs(g_up, f"layout_w_up{l}"), w_down=g_down.reshape(D_FF, D_MODEL))

    saved = []
    xl = xs
    for l in range(DEPTH):
        sp, wl = small[l], weights[l]
        h, proj = norm_proj(xl, sp["norm1_g"], wl["w_in"], f"norm_proj{l}",
                            dep=ag_ffn1.token if l == 0 else pass_ffn1.token)
        mix, y, ao = mixer_fwd(proj, bias, sp["sinks"], sp["conv_w"], sp["q_norm_g"], sp["k_norm_g"],
                               sp["conv_out_g"], sp["attn_out_g"], f"mixer_fwd{l}")
        dep = None
        if l == 0:
            pass_ffn0 = sibling_start(ag_ffn0.wait(mix)[3:], "pass_ffn0")
            dep = pass_ffn0.token
        x1 = matmul_residual(mix, wl["w_o"], xl, f"out_proj{l}", dep=dep)
        if l == 0:
            wl.update(ffn_weights(*pass_ffn0.wait(x1), 0))
            pass_mix1 = sibling_start(ag_mix1.wait(x1)[2:], "pass_mix1")
            h2, a, b, x2 = ffn_fwd(x1, sp["norm2_g"], wl["w_gate"], wl["w_up"], wl["w_down"], "ffn_fwd0",
                                   dep=pass_mix1.token)
            g_in, g_o = pass_mix1.wait(x2)
            weights[1] = dict(w_in=columns_from_slabs(g_in, "layout_w_in1"), w_o=g_o.reshape(D_MODEL, D_MODEL))
            pass_ffn1 = sibling_start(ag_ffn1.wait(x2)[3:], "pass_ffn1")
        else:
            wl.update(ffn_weights(*pass_ffn1.wait(x1), 1))
            h2, a, b, dx, loss_row = ffn_fwd(x1, sp["norm2_g"], wl["w_gate"], wl["w_up"], wl["w_down"], "ffn_fwd1",
                                             tgt=tgt)
            x2 = None
        saved.append((xl, h, proj, mix, y, ao, x1, h2, a, b))
        xl = x2

    shard = dict(w_in=(w_in, m_w_in, v_w_in), w_o=(w_o, m_w_o, v_w_o), w_gate=(w_gate, m_w_gate, v_w_gate),
                 w_up=(w_up, m_w_up, v_w_up), w_down=(w_down, m_w_down, v_w_down))
    stepped = {n: None for n in shard}
    gsmall = [None] * DEPTH

    def finish(sc, names, after, l):
        arrays = sc.wait(after)
        k = len(names)
        for i, n in enumerate(names):
            w, m, v = shard[n]
            stepped[n] = reduce_adamw(arrays[i], arrays[k + i], w, m, v, l, me_arr, f"reduce_adamw_{n}{l}",
                                      others=stepped[n])

    for l in reversed(range(DEPTH)):
        sp, wl = small[l], weights[l]
        x0, h, proj, mix, y, ao, x1, h2, a, b = saved[l]
        (d_wd,) = grad_weight((a, b), [dx], TF, D_MODEL, f"grad_w_down{l}", swiglu=True)
        sc_down = scatter_start([d_wd.reshape(N_DEV, ROWS_FF, D_MODEL)], f"scatter_w_down{l}")
        da, db, dx1, d_g2 = ffn_bwd(dx, a, b, x1, sp["norm2_g"], wl["w_gate"], wl["w_up"], wl["w_down"],
                                    f"ffn_bwd{l}", dep=sc_down.token)
        d_wg, d_wu = grad_weight((h2,), [da, db], D_MODEL, TF, f"grad_w_gate_up{l}", slab_cols=ROWS_FF)
        sc_gu = scatter_start([d_wg, d_wu], f"scatter_w_gate_up{l}")
        (d_wo,) = grad_weight((mix,), [dx1], D_MODEL, D_MODEL, f"grad_w_o{l}", dep=sc_gu.token)
        sc_o = scatter_start([d_wo.reshape(N_DEV, ROWS_O, D_MODEL)], f"scatter_w_o{l}")
        dmix = matmul_nt(dx1, wl["w_o"], f"out_proj_bwd{l}", dep=sc_o.token)
        finish(sc_down, ["w_down"], dmix, l)
        dproj, d_cw, d_qg, d_kg, d_sk, d_cog, d_aog = mixer_bwd(
            proj, bias, y, ao, dmix, sp["sinks"], sp["conv_w"], sp["q_norm_g"], sp["k_norm_g"],
            sp["conv_out_g"], sp["attn_out_g"], f"mixer_bwd{l}")
        finish(sc_gu, ["w_gate", "w_up"], dproj, l)
        finish(sc_o, ["w_o"], dproj, l)
        (d_win,) = grad_weight((h,), [dproj], D_MODEL, IN_COLS // 2, f"grad_w_in{l}", slab_cols=ROWS_IN)
        sc_in = scatter_start([d_win], f"scatter_w_in{l}")
        dx, d_g1 = proj_bwd(dproj, wl["w_in"], x0, sp["norm1_g"], dx1, f"proj_bwd{l}", dep=sc_in.token)
        finish(sc_in, ["w_in"], dx, l)
        both_heads = lambda d: d[:, :HEAD] + d[:, HEAD:]
        gsmall[l] = dict(norm1_g=d_g1, conv_w=d_cw, q_norm_g=both_heads(d_qg), k_norm_g=both_heads(d_kg), sinks=d_sk,
                         conv_out_g=d_cog, attn_out_g=d_aog, norm2_g=d_g2)
    grad_x = dx[None]

    flat = jnp.concatenate([loss_row[0, 0:1]] + [gsmall[l][n].reshape(-1) for l in range(DEPTH) for n in SMALL_NAMES])
    flat = jnp.pad(flat, (0, SMALL_ROWS * 128 - flat.shape[0])).reshape(SMALL_ROWS, 128)
    flat = all_reduce_small(flat, "all_reduce_small_grads").reshape(-1)
    loss = flat[0]
    gs = {n: [] for n in SMALL_NAMES}
    off = 1
    for l in range(DEPTH):
        for n, size in zip(SMALL_NAMES, SMALL_SIZES):
            gs[n].append(flat[off:off + size])
            off += size
    gs = {n: jnp.stack(v) for n, v in gs.items()}
    g_conv = lax.dynamic_slice(gs["conv_w"].reshape(DEPTH, 3, CONV_CH), (0, 0, me * HEAD), (DEPTH, 3, HEAD))

    gs["conv_w"] = g_conv
    params = dict(norm1_g=(norm1_g, m_norm1_g, v_norm1_g), conv_w=(conv_w, m_conv_w, v_conv_w),
                  q_norm_g=(q_norm_g, m_q_norm_g, v_q_norm_g), k_norm_g=(k_norm_g, m_k_norm_g, v_k_norm_g),
                  sinks=(sinks, m_sinks, v_sinks), conv_out_g=(conv_out_g, m_conv_out_g, v_conv_out_g),
                  attn_out_g=(attn_out_g, m_attn_out_g, v_attn_out_g), norm2_g=(norm2_g, m_norm2_g, v_norm2_g))
    names = ("norm1_g", "w_in", "conv_w", "q_norm_g", "k_norm_g", "sinks", "conv_out_g", "attn_out_g", "w_o",
             "norm2_g", "w_gate", "w_up", "w_down")

    out = {}
    for n in names:
        if n in shard:
            out[n] = stepped[n]
        else:
            w, m, v = params[n]
            two_d = (-1, w.shape[-1])
            d, mn, vn = adamw(w.reshape(two_d), gs[n].reshape(two_d), m.reshape(two_d), v.reshape(two_d), f"adamw_{n}")
            out[n] = (gs[n].reshape(w.shape), d.reshape(w.shape), mn.reshape(w.shape), vn.reshape(w.shape))
    return (loss, grad_x, *[out[n][i] for i in range(4) for n in names])
```

```python
import functools

import jax
import jax.numpy as jnp
from jax import lax
from jax.experimental import pallas as pl
from jax.experimental.pallas import tpu as pltpu

F32 = jnp.float32
BF16 = jnp.bfloat16

D_MODEL = 1024
CONV_CH = 512
ATTN_W = 512
N_Q = 8
N_KV = 2
GRP = N_Q // N_KV
HEAD = 64
IN_COLS = 2304
D_FF = 2816
BLK = 128
O_Q = 3 * CONV_CH
O_K = O_Q + ATTN_W
O_V = O_K + N_KV * HEAD
EPS = 1e-6
NEG_INF = -1e30
SCALE = HEAD ** -0.5
N_DEV = 8
DEPTH = 2

ADAM_LR = 0.001
ADAM_B1 = 0.9
ADAM_B2 = 0.999
ADAM_EPS = 1e-08
ADAM_WD = 0.01
ADAM_STEP = 10

VMEM_LIMIT = 56 * 1024 * 1024
TM = 512
MESH_T = pl.DeviceIdType.MESH

ROWS_IN, ROWS_O, ROWS_FF = IN_COLS // N_DEV, D_MODEL // N_DEV, D_FF // N_DEV


ANY = pl.BlockSpec(memory_space=pl.ANY)
HBM = pl.BlockSpec(memory_space=pltpu.HBM)
SEM = pl.BlockSpec(memory_space=pltpu.SEMAPHORE)


def _cparams(sem):
    return pltpu.CompilerParams(dimension_semantics=sem, vmem_limit_bytes=VMEM_LIMIT)


def _pallas_after(dep, body, *args, in_specs, **kw):
    if dep is None:
        return pl.pallas_call(body, in_specs=in_specs, **kw)(*args)

    def after_dep(dep_ref, *refs):
        body(*refs)

    return pl.pallas_call(after_dep, in_specs=[ANY, *in_specs], **kw)(dep, *args)


def _dot(a, b):
    return jnp.dot(a, b, preferred_element_type=F32)


def _dot_nt(a, b):
    return lax.dot_general(a, b, (((1,), (1,)), ((), ())), preferred_element_type=F32)


def _dot_tn(a, b):
    return lax.dot_general(a, b, (((0,), (0,)), ((), ())), preferred_element_type=F32)


LANES = 128


def _row_reduce(v, op, reduce):
    w = v.shape[-1]
    if w > LANES and w % LANES == 0:
        acc = v[:, 0:LANES]
        for c in range(1, w // LANES):
            acc = op(acc, v[:, LANES * c:LANES * (c + 1)])
        v = acc
    return reduce(v, axis=-1, keepdims=True)


def _row_sum(v):
    return _row_reduce(v, jnp.add, jnp.sum)


def _row_mean(v):
    return _row_sum(v) * (1.0 / v.shape[-1])


def _rms(v):
    return lax.rsqrt(_row_mean(v * v) + EPS)


def _sigmoid(v):
    return 0.5 * jnp.tanh(0.5 * v) + 0.5


def _rms_bwd(dyv, xh, r, g):
    dxh = dyv * g
    return r * (dxh - xh * _row_mean(dxh * xh))


def norm_proj(x, g, w_t, name, dep=None):
    S, N = x.shape[0], w_t.shape[0]

    def body(x_ref, g_ref, w_ref, h_ref, p_ref):
        xf = x_ref[...]
        h = ((xf * _rms(xf)) * g_ref[...]).astype(BF16)
        h_ref[...] = h
        p_ref[...] = _dot_nt(h, w_ref[...]).astype(BF16)

    return _pallas_after(
        dep, body, x, g, w_t, name=name, grid=(S // TM,),
        in_specs=[pl.BlockSpec((TM, D_MODEL), lambda i: (i, 0)),
                  pl.BlockSpec((1, D_MODEL), lambda i: (0, 0)),
                  pl.BlockSpec((N, D_MODEL), lambda i: (0, 0))],
        out_specs=[pl.BlockSpec((TM, D_MODEL), lambda i: (i, 0)),
                   pl.BlockSpec((TM, N), lambda i: (i, 0))],
        out_shape=[jax.ShapeDtypeStruct((S, D_MODEL), BF16), jax.ShapeDtypeStruct((S, N), BF16)],
        compiler_params=_cparams(("parallel",)),
    )


def band_bias():
    qi = lax.broadcasted_iota(jnp.int32, (BLK, 2 * BLK), 0)
    kj = lax.broadcasted_iota(jnp.int32, (BLK, 2 * BLK), 1)
    diff = qi + BLK - kj
    valid = (diff >= 0) & (diff < BLK)
    return jnp.stack([jnp.where(valid & (kj >= BLK), 0.0, NEG_INF), jnp.where(valid, 0.0, NEG_INF)]).astype(F32)


def _softmax_with_sink(s, sink):
    m = jnp.maximum(_row_reduce(s, jnp.maximum, jnp.max), sink)
    p = jnp.exp(s - m)
    es = jnp.exp(sink - m)
    inv = 1.0 / (_row_sum(p) + es)
    return p * inv, es * inv


PAIR = 2 * HEAD


def _low_half():
    return lax.broadcasted_iota(jnp.int32, (1, PAIR), 1) < HEAD


def _half_sums(v, low):
    return (jnp.sum(jnp.where(low, v, 0.0), axis=-1, keepdims=True),
            jnp.sum(jnp.where(low, 0.0, v), axis=-1, keepdims=True))


def _pair_mean(v, low):
    e, o = _half_sums(v, low)
    return jnp.where(low, e, o) * (1.0 / HEAD)


def _pair_rms(v, low):
    return lax.rsqrt(_pair_mean(v * v, low) + EPS)


def _one_head_in_both_halves(pair, low):
    swapped = pltpu.roll(pair, HEAD, 1)
    return jnp.where(low, pair, swapped), jnp.where(low, swapped, pair)


def _kv_pairs(p_ref, kvp_ref):
    k = jnp.concatenate([kvp_ref[:, 0:PAIR], p_ref[:, O_K:O_K + PAIR]], axis=0).astype(F32)
    v = jnp.concatenate([kvp_ref[:, PAIR:2 * PAIR], p_ref[:, O_V:O_V + PAIR]], axis=0)
    return k, v


def mixer_fwd(proj, bias, sinks, conv_w, qg, kg, cog, aog, name):
    S = proj.shape[0]
    nb = S // BLK

    def body(sinks_ref, p_ref, kvp_ref, bias_ref, cw_ref, qg_ref, kg_ref, cog_ref, aog_ref,
             mix_ref, y_ref, ao_ref, ucar):
        n = pl.program_id(0)

        @pl.when(n == 0)
        def _():
            ucar[...] = jnp.zeros_like(ucar)

        bg = p_ref[:, 0:CONV_CH].astype(F32)
        u = p_ref[:, CONV_CH:2 * CONV_CH].astype(F32) * p_ref[:, 2 * CONV_CH:3 * CONV_CH].astype(F32)
        row = lax.broadcasted_iota(jnp.int32, (BLK, CONV_CH), 0)
        prev = ucar[...]
        u1 = jnp.where(row == 0, prev[7:8, :], pltpu.roll(u, 1, 0))
        u2 = jnp.where(row == 0, prev[6:7, :], jnp.where(row == 1, prev[7:8, :], pltpu.roll(u, 2, 0)))
        ucar[...] = u[BLK - 8:BLK, :]
        y = cw_ref[0:1, :] * u2 + cw_ref[1:2, :] * u1 + cw_ref[2:3, :] * u
        y_ref[...] = y.astype(BF16)
        co = bg * y
        mix_ref[:, 0:CONV_CH] = ((co * _rms(co)) * cog_ref[...]).astype(BF16)

        low = _low_half()
        q_gain = qg_ref[...] * SCALE
        k, v = _kv_pairs(p_ref, kvp_ref)
        kn = ((k * _pair_rms(k, low)) * kg_ref[...]).astype(BF16)
        k_of = _one_head_in_both_halves(kn, low)
        v_of = _one_head_in_both_halves(v, low)
        scores = []
        for pair in range(N_Q // 2):
            q = p_ref[:, O_Q + PAIR * pair:O_Q + PAIR * (pair + 1)].astype(F32)
            qn = ((q * _pair_rms(q, low)) * q_gain).astype(BF16)
            for half in range(2):
                q_one = jnp.where(low if half == 0 else jnp.logical_not(low), qn, jnp.zeros_like(qn))
                scores.append(_dot_nt(q_one, k_of[pair // (GRP // 2)]) + bias_ref[0])
        probs = [_softmax_with_sink(s, sinks_ref[hq])[0].astype(BF16) for hq, s in enumerate(scores)]
        o = [_dot(pn, v_of[hq // GRP]) for hq, pn in enumerate(probs)]
        ao = jnp.concatenate([jnp.where(low, o[2 * pair], o[2 * pair + 1]) for pair in range(N_Q // 2)], axis=1)
        ao_ref[...] = ao.astype(BF16)
        mix_ref[:, CONV_CH:] = ((ao * _rms(ao)) * aog_ref[...]).astype(BF16)

    small = lambda shape: pl.BlockSpec(shape, lambda n: (0, 0))
    return pl.pallas_call(
        body, name=name, grid=(nb,),
        in_specs=[pl.BlockSpec(memory_space=pltpu.SMEM),
                  pl.BlockSpec((BLK, IN_COLS), lambda n: (n, 0)),
                  pl.BlockSpec((BLK, 2 * N_KV * HEAD), lambda n: (jnp.maximum(n - 1, 0), O_K // (2 * N_KV * HEAD))),
                  pl.BlockSpec((1, BLK, 2 * BLK), lambda n: (jnp.minimum(n, 1), 0, 0)),
                  small((3, CONV_CH)), small((1, PAIR)), small((1, PAIR)), small((1, CONV_CH)), small((1, ATTN_W))],
        out_specs=[pl.BlockSpec((BLK, D_MODEL), lambda n: (n, 0)),
                   pl.BlockSpec((BLK, CONV_CH), lambda n: (n, 0)),
                   pl.BlockSpec((BLK, ATTN_W), lambda n: (n, 0))],
        out_shape=[jax.ShapeDtypeStruct((S, D_MODEL), BF16), jax.ShapeDtypeStruct((S, CONV_CH), BF16),
                   jax.ShapeDtypeStruct((S, ATTN_W), BF16)],
        scratch_shapes=[pltpu.VMEM((8, CONV_CH), F32)],
        compiler_params=_cparams(("arbitrary",)),
    )(sinks, proj, proj, bias, conv_w, qg, kg, cog, aog)


def matmul_residual(a, w, res, name, dep=None):
    S, K = a.shape
    N = w.shape[1]

    def body(a_ref, w_ref, r_ref, o_ref):
        o_ref[...] = r_ref[...] + _dot(a_ref[...], w_ref[...])

    return _pallas_after(
        dep, body, a, w, res, name=name, grid=(S // TM,),
        in_specs=[pl.BlockSpec((TM, K), lambda i: (i, 0)), pl.BlockSpec((K, N), lambda i: (0, 0)),
                  pl.BlockSpec((TM, N), lambda i: (i, 0))],
        out_specs=pl.BlockSpec((TM, N), lambda i: (i, 0)),
        out_shape=jax.ShapeDtypeStruct((S, N), F32),
        compiler_params=_cparams(("parallel",)),
    )


TF = 1408


def ffn_fwd(x1, g, wg, wu, wd, name, dep=None, tgt=None):
    S = x1.shape[0]
    ni, nj = S // TM, D_FF // TF
    with_loss = tgt is not None

    def body(x_ref, g_ref, wg_ref, wu_ref, wd_ref, *rest):
        if with_loss:
            t_ref, h2_ref, a_ref, b_ref, o_ref, l_ref, acc, sq = rest
        else:
            h2_ref, a_ref, b_ref, o_ref, acc = rest
        i, j = pl.program_id(0), pl.program_id(1)

        @pl.when(j == 0)
        def _():
            xf = x_ref[...]
            h2_ref[...] = ((xf * _rms(xf)) * g_ref[...]).astype(BF16)
            acc[...] = xf

        h2 = h2_ref[...]
        a = _dot_nt(h2, wg_ref[...])
        b = _dot_nt(h2, wu_ref[...])
        a_ref[...] = a.astype(BF16)
        b_ref[...] = b.astype(BF16)
        f = (a * _sigmoid(a)) * b
        acc[...] += _dot(f.astype(BF16), wd_ref[...])

        @pl.when(j == nj - 1)
        def _():
            if not with_loss:
                o_ref[...] = acc[...]
                return
            e = acc[...] - t_ref[...]
            o_ref[...] = e * (1.0 / D_MODEL)
            col = jnp.sum(e * e, axis=0, keepdims=True)

            @pl.when(i == 0)
            def _():
                sq[...] = col

            @pl.when(i > 0)
            def _():
                sq[...] += col

            @pl.when(i == ni - 1)
            def _():
                l_ref[...] = jnp.full((1, 128), jnp.sum(sq[...]) * (0.5 / D_MODEL), F32)

    row = pl.BlockSpec((TM, D_MODEL), lambda i, j: (i, 0))
    tile = pl.BlockSpec((TM, TF), lambda i, j: (i, j))
    return _pallas_after(
        dep, body, x1, g, wg, wu, wd, *([tgt] if with_loss else []), name=name, grid=(ni, nj),
        in_specs=[row, pl.BlockSpec((1, D_MODEL), lambda i, j: (0, 0))]
        + [pl.BlockSpec((TF, D_MODEL), lambda i, j: (j, 0))] * 3 + ([row] if with_loss else []),
        out_specs=[row, tile, tile, row] + ([pl.BlockSpec((1, 128), lambda i, j: (0, 0))] if with_loss else []),
        out_shape=[jax.ShapeDtypeStruct((S, D_MODEL), BF16), jax.ShapeDtypeStruct((S, D_FF), BF16),
                   jax.ShapeDtypeStruct((S, D_FF), BF16), jax.ShapeDtypeStruct((S, D_MODEL), F32)]
        + ([jax.ShapeDtypeStruct((1, 128), F32)] if with_loss else []),
        scratch_shapes=[pltpu.VMEM((TM, D_MODEL), F32)] + ([pltpu.VMEM((1, D_MODEL), F32)] if with_loss else []),
        compiler_params=_cparams(("arbitrary", "arbitrary") if with_loss else ("parallel", "arbitrary")),
    )


def ffn_bwd(dx2, a, b, x1, g, wg, wu, wd, name, dep=None):
    S = dx2.shape[0]
    nj = D_FF // TF

    def body(dx_ref, a_ref, b_ref, x_ref, g_ref, wg_ref, wu_ref, wd_ref, da_ref, db_ref, dx1_ref, dg_ref, dxb, acc):
        i, j = pl.program_id(0), pl.program_id(1)

        @pl.when((i == 0) & (j == 0))
        def _():
            dg_ref[...] = jnp.zeros_like(dg_ref)

        @pl.when(j == 0)
        def _():
            dxb[...] = dx_ref[...].astype(BF16)
            acc[...] = jnp.zeros_like(acc)

        df = _dot_nt(dxb[...], wd_ref[...])
        av = a_ref[...].astype(F32)
        bv = b_ref[...].astype(F32)
        sg = _sigmoid(av)
        da =((df * bv) * (sg * (1.0 + av * (1.0 - sg)))).astype(BF16)
        db = (df * (av * sg)).astype(BF16)
        da_ref[...] = da
        db_ref[...] = db
        acc[...] += _dot(da, wg_ref[...]) + _dot(db, wu_ref[...])

        @pl.when(j == nj - 1)
        def _():
            xf = x_ref[...]
            r = _rms(xf)
            xh = xf * r
            dh = acc[...]
            dg_ref[...] += jnp.sum(dh * xh, axis=0, keepdims=True)
            dx1_ref[...] = dx_ref[...] + _rms_bwd(dh, xh, r, g_ref[...])

    return _pallas_after(
        dep, body, dx2, a, b, x1, g, wg, wu, wd, name=name, grid=(S // TM, nj),
        in_specs=[pl.BlockSpec((TM, D_MODEL), lambda i, j: (i, 0)),
                  pl.BlockSpec((TM, TF), lambda i, j: (i, j)), pl.BlockSpec((TM, TF), lambda i, j: (i, j)),
                  pl.BlockSpec((TM, D_MODEL), lambda i, j: (i, 0)), pl.BlockSpec((1, D_MODEL), lambda i, j: (0, 0))]
        + [pl.BlockSpec((TF, D_MODEL), lambda i, j: (j, 0))] * 3,
        out_specs=[pl.BlockSpec((TM, TF), lambda i, j: (i, j)), pl.BlockSpec((TM, TF), lambda i, j: (i, j)),
                   pl.BlockSpec((TM, D_MODEL), lambda i, j: (i, 0)), pl.BlockSpec((1, D_MODEL), lambda i, j: (0, 0))],
        out_shape=[jax.ShapeDtypeStruct((S, D_FF), BF16), jax.ShapeDtypeStruct((S, D_FF), BF16),
                   jax.ShapeDtypeStruct((S, D_MODEL), F32), jax.ShapeDtypeStruct((1, D_MODEL), F32)],
        scratch_shapes=[pltpu.VMEM((TM, D_MODEL), BF16), pltpu.VMEM((TM, D_MODEL), F32)],
        compiler_params=_cparams(("arbitrary", "arbitrary")),
    )


def grad_weight(lhs, rhs, tm, name, swiglu=False, dep=None):
    S, N = rhs.shape
    M = lhs[0].shape[1]
    tk = 1024
    nk = S // tk
    n_l = len(lhs)
    n_o = 1 if swiglu else n_l

    def body(*refs):
        l_refs, r_ref = refs[:n_l], refs[n_l]
        o_refs, accs = refs[n_l + 1:n_l + 1 + n_o], refs[n_l + 1 + n_o:]
        k = pl.program_id(1)
        rv = r_ref[...].astype(BF16)
        if swiglu:
            av = l_refs[0][...].astype(F32)
            lvs = [((av * _sigmoid(av)) * l_refs[1][...].astype(F32)).astype(BF16)]
        else:
            lvs = [l_ref[...] for l_ref in l_refs]
        for lv, o_ref, acc in zip(lvs, o_refs, accs):
            part = _dot_tn(lv, rv)

            @pl.when(k == 0)
            def _():
                acc[...] = part

            @pl.when(k > 0)
            def _():
                acc[...] += part

            @pl.when(k == nk - 1)
            def _():
                o_ref[...] = acc[...].astype(BF16)

    return _pallas_after(
        dep, body, *lhs, rhs, name=name, grid=(M // tm, nk),
        in_specs=[pl.BlockSpec((tk, tm), lambda i, k: (k, i)) for _ in lhs] + [pl.BlockSpec((tk, N), lambda i, k: (k, 0))],
        out_specs=[pl.BlockSpec((tm, N), lambda i, k: (i, 0))] * n_o,
        out_shape=[jax.ShapeDtypeStruct((M, N), BF16)] * n_o,
        scratch_shapes=[pltpu.VMEM((tm, N), F32)] * n_o,
        compiler_params=_cparams(("parallel", "arbitrary")),
    )


def matmul_nt(a, w, name, dep=None):
    S, K = a.shape
    N = w.shape[0]

    def body(a_ref, w_ref, o_ref):
        o_ref[...] = _dot_nt(a_ref[...].astype(BF16), w_ref[...]).astype(BF16)

    return _pallas_after(
        dep, body, a, w, name=name, grid=(S // TM,),
        in_specs=[pl.BlockSpec((TM, K), lambda i: (i, 0)), pl.BlockSpec((N, K), lambda i: (0, 0))],
        out_specs=pl.BlockSpec((TM, N), lambda i: (i, 0)),
        out_shape=jax.ShapeDtypeStruct((S, N), BF16),
        compiler_params=_cparams(("parallel",)),
    )


def proj_bwd(dproj, w, x, g, dres, name, dep=None):
    S, N = dproj.shape

    def body(dp_ref, w_ref, x_ref, g_ref, dr_ref, dx_ref, dg_ref):
        i = pl.program_id(0)

        @pl.when(i == 0)
        def _():
            dg_ref[...] = jnp.zeros_like(dg_ref)

        dh = _dot(dp_ref[...], w_ref[...])
        xf = x_ref[...]
        r = _rms(xf)
        xh = xf * r
        dg_ref[...] += jnp.sum(dh * xh, axis=0, keepdims=True)
        dx_ref[...] = dr_ref[...] + _rms_bwd(dh, xh, r, g_ref[...])

    return _pallas_after(
        dep, body, dproj, w, x, g, dres, name=name, grid=(S // TM,),
        in_specs=[pl.BlockSpec((TM, N), lambda i: (i, 0)), pl.BlockSpec((N, D_MODEL), lambda i: (0, 0)),
                  pl.BlockSpec((TM, D_MODEL), lambda i: (i, 0)), pl.BlockSpec((1, D_MODEL), lambda i: (0, 0)),
                  pl.BlockSpec((TM, D_MODEL), lambda i: (i, 0))],
        out_specs=[pl.BlockSpec((TM, D_MODEL), lambda i: (i, 0)), pl.BlockSpec((1, D_MODEL), lambda i: (0, 0))],
        out_shape=[jax.ShapeDtypeStruct((S, D_MODEL), F32), jax.ShapeDtypeStruct((1, D_MODEL), F32)],
        compiler_params=_cparams(("arbitrary",)),
    )


def mixer_bwd(proj, bias, y, ao, dmix, sinks, conv_w, qg, kg, cog, aog, name):
    S = proj.shape[0]
    nb = S // BLK
    KV_W = 2 * N_KV * HEAD

    def body(sinks_ref, p_ref, kvp_ref, bias_ref, y_ref, ao_ref, dm_ref, cw_ref, qg_ref, kg_ref, cog_ref, aog_ref,
             dp_ref, dcw_ref, dqg_ref, dkg_ref, dsk_ref, dcog_ref, daog_ref,
             dycar, kcar, vcar):
        step = pl.program_id(0)

        @pl.when(step == 0)
        def _():
            dycar[...] = jnp.zeros_like(dycar)
            kcar[...] = jnp.zeros_like(kcar)
            vcar[...] = jnp.zeros_like(vcar)
            dcw_ref[...] = jnp.zeros_like(dcw_ref)
            dqg_ref[...] = jnp.zeros_like(dqg_ref)
            dkg_ref[...] = jnp.zeros_like(dkg_ref)
            dsk_ref[...] = jnp.zeros_like(dsk_ref)
            dcog_ref[...] = jnp.zeros_like(dcog_ref)
            daog_ref[...] = jnp.zeros_like(daog_ref)

        bg = p_ref[:, 0:CONV_CH].astype(F32)
        cg = p_ref[:, CONV_CH:2 * CONV_CH].astype(F32)
        hc = p_ref[:, 2 * CONV_CH:3 * CONV_CH].astype(F32)
        yv = y_ref[...].astype(F32)
        dmc = dm_ref[:, 0:CONV_CH].astype(F32)
        co = bg * yv
        rc = _rms(co)
        ch = co * rc
        dcog_ref[...] += jnp.sum(dmc * ch, axis=0, keepdims=True)
        dco = _rms_bwd(dmc, ch, rc, cog_ref[...])
        dp_ref[:, 0:CONV_CH] = (dco * yv).astype(BF16)
        dy = dco * bg
        row = lax.broadcasted_iota(jnp.int32, (BLK, CONV_CH), 0)
        nxt = dycar[...]
        dy1 = jnp.where(row == BLK - 1, nxt[0:1, :], pltpu.roll(dy, BLK - 1, 0))
        dy2 = jnp.where(row == BLK - 2, nxt[0:1, :], jnp.where(row == BLK - 1, nxt[1:2, :], pltpu.roll(dy, BLK - 2, 0)))
        dycar[...] = dy[0:8, :]
        du = cw_ref[2:3, :] * dy + cw_ref[1:2, :] * dy1 + cw_ref[0:1, :] * dy2
        dp_ref[:, CONV_CH:2 * CONV_CH] = (du * hc).astype(BF16)
        dp_ref[:, 2 * CONV_CH:3 * CONV_CH] = (du * cg).astype(BF16)
        u = cg * hc
        dcw_ref[0:1, :] += jnp.sum(dy2 * u, axis=0, keepdims=True)
        dcw_ref[1:2, :] += jnp.sum(dy1 * u, axis=0, keepdims=True)
        dcw_ref[2:3, :] += jnp.sum(dy * u, axis=0, keepdims=True)

        aov = ao_ref[...].astype(F32)
        dma = dm_ref[:, CONV_CH:].astype(F32)
        ra = _rms(aov)
        ah = aov * ra
        daog_ref[...] += jnp.sum(dma * ah, axis=0, keepdims=True)
        dao = _rms_bwd(dma, ah, ra, aog_ref[...])

        low = _low_half()
        high = jnp.logical_not(low)
        q_gain = qg_ref[...] * SCALE
        k, v = _kv_pairs(p_ref, kvp_ref)
        rk = _pair_rms(k, low)
        kh = k * rk
        k_of = _one_head_in_both_halves((kh * kg_ref[...]).astype(BF16), low)
        v_of = _one_head_in_both_halves(v, low)
        lane8 = lax.broadcasted_iota(jnp.int32, (1, N_Q), 1)
        dsk = jnp.zeros((1, N_Q), F32)
        dqg = jnp.zeros((1, PAIR), F32)
        dv_t = [jnp.zeros((PAIR, 2 * BLK), F32) for _ in range(N_KV)]
        dkn_t = [jnp.zeros((PAIR, 2 * BLK), F32) for _ in range(N_KV)]
        rq, qh, q_one, do_one, delta = [], [], [], [], []
        for pair in range(N_Q // 2):
            cols = slice(PAIR * pair, PAIR * (pair + 1))
            q = p_ref[:, O_Q + PAIR * pair:O_Q + PAIR * (pair + 1)].astype(F32)
            rq.append(_pair_rms(q, low))
            qh.append(q * rq[pair])
            qn = (qh[pair] * q_gain).astype(BF16)
            do = dao[:, cols]
            do_b = do.astype(BF16)
            delta += _half_sums(do * aov[:, cols], low)
            for mine in (low, high):
                q_one.append(jnp.where(mine, qn, jnp.zeros_like(qn)))
                do_one.append(jnp.where(mine, do_b, jnp.zeros_like(do_b)))
        heads = range(N_Q)
        scores = [_dot_nt(q_one[hq], k_of[hq // GRP]) + bias_ref[0] for hq in heads]
        dprobs = [_dot_nt(do_one[hq], v_of[hq // GRP]) for hq in heads]
        probs = [_softmax_with_sink(scores[hq], sinks_ref[hq]) for hq in heads]
        ds = [(probs[hq][0] * (dprobs[hq] - delta[hq])).astype(BF16) for hq in heads]
        for hq in heads:
            dsk = dsk - jnp.where(lane8 == hq, jnp.sum(probs[hq][1] * delta[hq]), 0.0)
            dv_t[hq // GRP] = dv_t[hq // GRP] + _dot_tn(do_one[hq], probs[hq][0].astype(BF16))
            dkn_t[hq // GRP] = dkn_t[hq // GRP] + _dot_tn(q_one[hq], ds[hq])
        dqn_of = [_dot(ds[hq], k_of[hq // GRP]) for hq in heads]
        for pair in range(N_Q // 2):
            dqn = jnp.where(low, dqn_of[2 * pair], dqn_of[2 * pair + 1])
            dqg = dqg + jnp.sum(dqn * qh[pair], axis=0, keepdims=True)
            dqh = dqn * q_gain
            dp_ref[:, O_Q + PAIR * pair:O_Q + PAIR * (pair + 1)] = (
                rq[pair] * (dqh - qh[pair] * _pair_mean(dqh * qh[pair], low))).astype(BF16)

        def untranspose(parts):
            return jnp.concatenate([t[:HEAD] + t[HEAD:] for t in parts], axis=0).T

        dv = untranspose(dv_t)
        dkn = untranspose(dkn_t)
        dkg_ref[...] += jnp.sum(dkn * kh, axis=0, keepdims=True)
        dkh = dkn * kg_ref[...]
        dk = rk * (dkh - kh * _pair_mean(dkh * kh, low))
        dp_ref[:, O_K:O_V] = (kcar[...] + dk[BLK:, :]).astype(BF16)
        dp_ref[:, O_V:] = (vcar[...] + dv[BLK:, :]).astype(BF16)
        kcar[...] = dk[:BLK, :]
        vcar[...] = dv[:BLK, :]
        dsk_ref[...] += dsk
        dqg_ref[...] += dqg * SCALE

    small = lambda shape: pl.BlockSpec(shape, lambda s: (0, 0))
    blk = lambda w: pl.BlockSpec((BLK, w), lambda s: (nb - 1 - s, 0))
    return pl.pallas_call(
        body, name=name, grid=(nb,),
        in_specs=[pl.BlockSpec(memory_space=pltpu.SMEM),
                  blk(IN_COLS),
                  pl.BlockSpec((BLK, KV_W), lambda s: (jnp.maximum(nb - 2 - s, 0), O_K // KV_W)),
                  pl.BlockSpec((1, BLK, 2 * BLK), lambda s: (jnp.minimum(nb - 1 - s, 1), 0, 0)),
                  blk(CONV_CH), blk(ATTN_W), blk(D_MODEL),
                  small((3, CONV_CH)), small((1, PAIR)), small((1, PAIR)), small((1, CONV_CH)), small((1, ATTN_W))],
        out_specs=[blk(IN_COLS), small((3, CONV_CH)), small((1, PAIR)), small((1, PAIR)), small((1, N_Q)),
                   small((1, CONV_CH)), small((1, ATTN_W))],
        out_shape=[jax.ShapeDtypeStruct((S, IN_COLS), BF16), jax.ShapeDtypeStruct((3, CONV_CH), F32),
                   jax.ShapeDtypeStruct((1, PAIR), F32), jax.ShapeDtypeStruct((1, PAIR), F32),
                   jax.ShapeDtypeStruct((1, N_Q), F32), jax.ShapeDtypeStruct((1, CONV_CH), F32),
                   jax.ShapeDtypeStruct((1, ATTN_W), F32)],
        scratch_shapes=[pltpu.VMEM((8, CONV_CH), F32), pltpu.VMEM((BLK, PAIR), F32), pltpu.VMEM((BLK, PAIR), F32)],
        compiler_params=_cparams(("arbitrary",)),
    )(sinks, proj, proj, bias, y, ao, dmix, conv_w, qg, kg, cog, aog)


OTHER_CHIPS = ((1, 0), (0, 1), (1, 1))


def _position():
    return lax.axis_index("x"), lax.axis_index("y"), lax.axis_index("c")


def all_gather(shards, name):
    n = len(shards)

    def body(*refs):
        x_refs, out_refs = refs[:n], refs[n:2 * n]
        send_sems, recv_sems, local_sems = refs[2 * n:]
        x, y, c = _position()
        me, sibling = (x, y, c), (x, y, 1 - c)
        chips = [(x ^ mx, y ^ my) for mx, my in OTHER_CHIPS]

        def slab(a, px, py, pc):
            return out_refs[a].at[4 * px + 2 * py + pc]

        def copy(a, k, block, to, src=None):
            return pltpu.make_async_remote_copy(
                src_ref=slab(a, *block) if src is None else src, dst_ref=slab(a, *block),
                send_sem=send_sems.at[7 * a + k], recv_sem=recv_sems.at[7 * a + k], device_id=to, device_id_type=MESH_T)

        mine = [pltpu.make_async_copy(x_refs[a], slab(a, *me), local_sems.at[a]) for a in range(n)]
        for cp in mine:
            cp.start()
        first = []
        for a in range(n):
            first.append(copy(a, 0, me, sibling, src=x_refs[a]))
            first += [copy(a, 1 + j, me, (*chip, c), src=x_refs[a]) for j, chip in enumerate(chips)]
        for cp in first:
            cp.start()
        passed = []
        for a in range(n):
            for j, chip in enumerate(chips):
                copy(a, 1 + j, (*chip, c), me).wait_recv()
                passed.append(copy(a, 4 + j, (*chip, c), sibling))
                passed[-1].start()
        for a in range(n):
            copy(a, 0, sibling, me).wait_recv()
            for j, chip in enumerate(chips):
                copy(a, 4 + j, (*chip, 1 - c), me).wait_recv()
        for cp in first + passed:
            cp.wait_send()
        for cp in mine:
            cp.wait()

    return pl.pallas_call(
        body, name=name, in_specs=[ANY] * n, out_specs=[ANY] * n,
        out_shape=[jax.ShapeDtypeStruct((N_DEV, *s.shape), s.dtype) for s in shards],
        scratch_shapes=[pltpu.SemaphoreType.DMA((7 * n,)), pltpu.SemaphoreType.DMA((7 * n,)),
                        pltpu.SemaphoreType.DMA((n,))],
    )(*shards)


class SplitCopy:
    def __init__(self, name, arrays, n_copies, plan, after=None):
        n = len(arrays)
        self.name, self.n, self.n_copies, self.plan = name, n, n_copies, plan
        extra = [] if after is None else [after]

        def body(*refs):
            in_refs = refs[:n]
            send_sems, recv_sems = refs[n + len(extra)], refs[n + len(extra) + 1]
            token = refs[2 * n + len(extra) + 2]
            for k, (src, dst, to) in enumerate(plan(_position(), in_refs)):
                pltpu.make_async_remote_copy(src_ref=src, dst_ref=dst, send_sem=send_sems.at[k],
                                             recv_sem=recv_sems.at[k], device_id=to, device_id_type=MESH_T).start()
            token[...] = jnp.zeros_like(token)

        outs = pl.pallas_call(
            body, name=name + "_start",
            out_shape=(pltpu.SemaphoreType.DMA((n_copies,)), pltpu.SemaphoreType.DMA((n_copies,)),
                       *[pltpu.HBM(a.shape, a.dtype) for a in arrays], jax.ShapeDtypeStruct((8, 128), F32)),
            in_specs=[HBM] * n + [ANY] * len(extra),
            out_specs=(SEM, SEM, *[HBM] * n, pl.BlockSpec(memory_space=pltpu.VMEM)),
            input_output_aliases={i: 2 + i for i in range(n)},
            compiler_params=pltpu.CompilerParams(has_side_effects=pltpu.SideEffectType.DATAFLOW_SIDE_EFFECTING),
        )(*[pltpu.with_memory_space_constraint(a, pltpu.HBM) for a in arrays], *extra)
        self.send_sems, self.recv_sems = outs[0], outs[1]
        self.arrays, self.token = list(outs[2:2 + n]), outs[2 + n]

    def wait(self, after):
        n, plan = self.n, self.plan

        def body(*refs):
            in_refs, send_sems, recv_sems = refs[:n], refs[n], refs[n + 1]
            for k, (src, dst, to) in enumerate(plan(_position(), in_refs)):
                cp = pltpu.make_async_remote_copy(src_ref=src, dst_ref=dst, send_sem=send_sems.at[k],
                                                  recv_sem=recv_sems.at[k], device_id=to, device_id_type=MESH_T)
                cp.wait_send()
                cp.wait_recv()

        outs = pl.pallas_call(
            body, name=self.name + "_wait",
            out_shape=tuple(pltpu.HBM(a.shape, a.dtype) for a in self.arrays),
            in_specs=[HBM] * n + [SEM, SEM, ANY], out_specs=tuple([HBM] * n),
            input_output_aliases={i: i for i in range(n)},
            compiler_params=pltpu.CompilerParams(has_side_effects=pltpu.SideEffectType.DATAFLOW_SIDE_EFFECTING),
        )(*self.arrays, self.send_sems, self.recv_sems, after)
        return list(outs)


def gather_start(shards, me, name, after=None):
    n = len(shards)
    lands = [lax.dynamic_update_slice(lax.empty((N_DEV, *s.shape), s.dtype), s[None], (me, 0, 0)) for s in shards]

    def plan(pos, refs):
        x, y, c = pos
        return [(refs[a], refs[n + a].at[4 * x + 2 * y + c], (x ^ mx, y ^ my, c))
                for a in range(n) for mx, my in OTHER_CHIPS]

    return SplitCopy(name, list(shards) + lands, 3 * n, plan, after=after)


def sibling_start(lands, name):
    n = len(lands)

    def plan(pos, refs):
        x, y, c = pos
        return [(refs[a].at[2 * q + c], refs[a].at[2 * q + c], (x, y, 1 - c)) for a in range(n) for q in range(4)]

    return SplitCopy(name, list(lands), 4 * n, plan)


def scatter_start(slabs, name):
    n = len(slabs)
    lands = [lax.empty((N_DEV - 1, *g.shape[1:]), g.dtype) for g in slabs]

    def plan(pos, refs):
        x, y, c = pos
        copies = []
        for a in range(n):
            for r in range(1, N_DEV):
                px, py, pc = x ^ ((r >> 2) & 1), y ^ ((r >> 1) & 1), c ^ (r & 1)
                copies.append((refs[a].at[4 * px + 2 * py + pc], refs[n + a].at[r - 1], (px, py, pc)))
        return copies

    return SplitCopy(name, list(slabs) + lands, (N_DEV - 1) * n, plan)


def all_reduce_small(v, name, dep=None):
    R, W = v.shape

    def body(v_ref, o_ref, recv, send_sems, recv_sems):
        x, y, c = _position()
        me = 4 * x + 2 * y + c
        copies = []
        for r in range(1, N_DEV):
            to = (x ^ ((r >> 2) & 1), y ^ ((r >> 1) & 1), c ^ (r & 1))
            copies.append(pltpu.make_async_remote_copy(
                src_ref=v_ref, dst_ref=recv.at[me], send_sem=send_sems.at[r - 1], recv_sem=recv_sems.at[r - 1],
                device_id=to, device_id_type=MESH_T))
        for cp in copies:
            cp.start()
        recv[pl.ds(me, 1)] = v_ref[...][None]
        for cp in copies:
            cp.wait()
        acc = recv[0]
        for s in range(1, N_DEV):
            acc = acc + recv[s]
        o_ref[...] = acc

    return _pallas_after(
        dep, body, v, name=name,
        in_specs=[pl.BlockSpec(memory_space=pltpu.VMEM)], out_specs=pl.BlockSpec(memory_space=pltpu.VMEM),
        out_shape=jax.ShapeDtypeStruct((R, W), F32),
        scratch_shapes=[pltpu.VMEM((N_DEV, R, W), F32), pltpu.SemaphoreType.DMA((N_DEV - 1,)),
                        pltpu.SemaphoreType.DMA((N_DEV - 1,))],
    )


def _row_tile(rows):
    if rows <= 512:
        return rows
    return max(t for t in range(8, 513, 8) if rows % t == 0)


def _adamw_update(w, g, m, v):
    mn = ADAM_B1 * m + (1.0 - ADAM_B1) * g
    vn = ADAM_B2 * v + (1.0 - ADAM_B2) * (g * g)
    m_hat = mn / (1.0 - ADAM_B1 ** ADAM_STEP)
    v_hat = vn / (1.0 - ADAM_B2 ** ADAM_STEP)
    return -ADAM_LR * (m_hat / (jnp.sqrt(v_hat) + ADAM_EPS) + ADAM_WD * w), mn, vn


def adamw(w, g, m, v, name):
    R, W = w.shape
    tr = _row_tile(R)

    def body(w_ref, g_ref, m_ref, v_ref, d_ref, mo_ref, vo_ref):
        d_ref[...], mo_ref[...], vo_ref[...] = _adamw_update(w_ref[...], g_ref[...], m_ref[...], v_ref[...])

    spec = pl.BlockSpec((tr, W), lambda i: (i, 0))
    return pl.pallas_call(
        body, name=name, grid=(R // tr,), in_specs=[spec] * 4, out_specs=[spec] * 3,
        out_shape=[jax.ShapeDtypeStruct((R, W), F32)] * 3,
        compiler_params=_cparams(("parallel",)),
    )(w, g, m, v)


def reduce_adamw(slabs, land, w, m, v, layer, me_arr, name, others=None):
    _, R, W = slabs.shape
    tr = _row_tile(R)
    n_other = 0 if others is None else 4

    def body(me_ref, s_ref, l_ref, w_ref, m_ref, v_ref, *rest):
        g_ref, d_ref, mo_ref, vo_ref = rest[n_other:]
        g = s_ref[0].astype(F32)
        for r in range(N_DEV - 1):
            g = g + l_ref[r].astype(F32)
        g_ref[0] = g
        d_ref[0], mo_ref[0], vo_ref[0] = _adamw_update(w_ref[0], g, m_ref[0], v_ref[0])

    spec = pl.BlockSpec((1, tr, W), lambda i, me: (layer, i, 0))
    return pl.pallas_call(
        body, name=name,
        grid_spec=pltpu.PrefetchScalarGridSpec(
            num_scalar_prefetch=1, grid=(R // tr,),
            in_specs=[pl.BlockSpec((1, tr, W), lambda i, me: (me[0], i, 0)),
                      pl.BlockSpec((N_DEV - 1, tr, W), lambda i, me: (0, i, 0)), spec, spec, spec] + [ANY] * n_other,
            out_specs=[spec] * 4),
        out_shape=[jax.ShapeDtypeStruct((DEPTH, R, W), F32)] * 4,
        input_output_aliases={6 + i: i for i in range(n_other)},
        compiler_params=_cparams(("parallel",)),
    )(me_arr, slabs, land, w, m, v, *([] if others is None else others))


SMALL_NAMES = ("norm1_g", "q_norm_g", "k_norm_g", "sinks", "conv_out_g", "attn_out_g", "norm2_g", "conv_w")
SMALL_SIZES = (D_MODEL, HEAD, HEAD, N_Q, CONV_CH, ATTN_W, D_MODEL, 3 * CONV_CH)
SMALL_ROWS = 80


def kernel(x, norm1_g, w_in, conv_w, q_norm_g, k_norm_g, sinks, conv_out_g, attn_out_g, w_o, norm2_g, w_gate, w_up, w_down, loss_target, m_norm1_g, m_w_in, m_conv_w, m_q_norm_g, m_k_norm_g, m_sinks, m_conv_out_g, m_attn_out_g, m_w_o, m_norm2_g, m_w_gate, m_w_up, m_w_down, v_norm1_g, v_w_in, v_conv_w, v_q_norm_g, v_k_norm_g, v_sinks, v_conv_out_g, v_attn_out_g, v_w_o, v_norm2_g, v_w_gate, v_w_up, v_w_down):
    xi, yi, ci = _position()
    me = 4 * xi + 2 * yi + ci
    me_arr = jnp.reshape(me, (1,)).astype(jnp.int32)
    xs, tgt = x[0], loss_target[0]
    bf = lambda a: a.astype(BF16)
    bias = band_bias()

    t = lambda a: jnp.swapaxes(a, 1, 2)
    shard = dict(w_in=(t(w_in), t(m_w_in), t(v_w_in)), w_o=(w_o, m_w_o, v_w_o),
                 w_gate=(t(w_gate), t(m_w_gate), t(v_w_gate)), w_up=(t(w_up), t(m_w_up), t(v_w_up)),
                 w_down=(w_down, m_w_down, v_w_down))
    wb = {n: bf(shard[n][0]) for n in shard}

    g_in0, g_o0, g_conv = all_gather([wb["w_in"][0], wb["w_o"][0], conv_w.reshape(DEPTH * 3, HEAD)], "gather_first")
    ag_ffn0 = gather_start([wb["w_gate"][0], wb["w_up"][0], wb["w_down"][0]], me, "gather_ffn0", after=g_in0)
    ag_mix1 = gather_start([wb["w_in"][1], wb["w_o"][1]], me, "gather_mix1", after=ag_ffn0.token)
    ag_ffn1 = gather_start([wb["w_gate"][1], wb["w_up"][1], wb["w_down"][1]], me, "gather_ffn1", after=ag_mix1.token)
    conv_full = g_conv.reshape(N_DEV, DEPTH, 3, HEAD).transpose(1, 2, 0, 3).reshape(DEPTH, 3, CONV_CH)
    pair_gain = lambda g: jnp.tile(g[None], (1, 2))
    small = [dict(norm1_g=norm1_g[l][None], conv_w=conv_full[l], q_norm_g=pair_gain(q_norm_g[l]),
                  k_norm_g=pair_gain(k_norm_g[l]), sinks=sinks[l], conv_out_g=conv_out_g[l][None],
                  attn_out_g=attn_out_g[l][None], norm2_g=norm2_g[l][None]) for l in range(DEPTH)]
    whole = lambda g: g.reshape(-1, D_MODEL)
    weights = [dict(w_in=whole(g_in0), w_o=whole(g_o0)), {}]

    def ffn_weights(g_gate, g_up, g_down):
        return dict(w_gate=whole(g_gate), w_up=whole(g_up), w_down=whole(g_down))

    saved = []
    xl = xs
    for l in range(DEPTH):
        sp, wl = small[l], weights[l]
        h, proj = norm_proj(xl, sp["norm1_g"], wl["w_in"], f"norm_proj{l}",
                            dep=ag_ffn1.token if l == 0 else pass_ffn1.token)
        mix, y, ao = mixer_fwd(proj, bias, sp["sinks"], sp["conv_w"], sp["q_norm_g"], sp["k_norm_g"],
                               sp["conv_out_g"], sp["attn_out_g"], f"mixer_fwd{l}")
        dep = None
        if l == 0:
            pass_ffn0 = sibling_start(ag_ffn0.wait(mix)[3:], "pass_ffn0")
            dep = pass_ffn0.token
        x1 = matmul_residual(mix, wl["w_o"], xl, f"out_proj{l}", dep=dep)
        if l == 0:
            wl.update(ffn_weights(*pass_ffn0.wait(x1)))
            pass_mix1 = sibling_start(ag_mix1.wait(x1)[2:], "pass_mix1")
            h2, a, b, x2 = ffn_fwd(x1, sp["norm2_g"], wl["w_gate"], wl["w_up"], wl["w_down"], "ffn_fwd0",
                                   dep=pass_mix1.token)
            g_in, g_o = pass_mix1.wait(x2)
            weights[1] = dict(w_in=whole(g_in), w_o=whole(g_o))
            pass_ffn1 = sibling_start(ag_ffn1.wait(x2)[3:], "pass_ffn1")
        else:
            wl.update(ffn_weights(*pass_ffn1.wait(x1)))
            h2, a, b, dx, loss_row = ffn_fwd(x1, sp["norm2_g"], wl["w_gate"], wl["w_up"], wl["w_down"], "ffn_fwd1",
                                             tgt=tgt)
            x2 = None
        saved.append((xl, h, proj, mix, y, ao, x1, h2, a, b))
        xl = x2

    stepped = {n: None for n in shard}
    slabs = lambda d: d.reshape(N_DEV, -1, D_MODEL)
    gsmall = [None] * DEPTH

    def finish(sc, names, after, l):
        arrays = sc.wait(after)
        k = len(names)
        for i, n in enumerate(names):
            w, m, v = shard[n]
            stepped[n] = reduce_adamw(arrays[i], arrays[k + i], w, m, v, l, me_arr, f"reduce_adamw_{n}{l}",
                                      others=stepped[n])

    for l in reversed(range(DEPTH)):
        sp, wl = small[l], weights[l]
        x0, h, proj, mix, y, ao, x1, h2, a, b = saved[l]
        (d_wd,) = grad_weight((a, b), dx, TF, f"grad_w_down{l}", swiglu=True)
        sc_down = scatter_start([slabs(d_wd)], f"scatter_w_down{l}")
        da, db, dx1, d_g2 = ffn_bwd(dx, a, b, x1, sp["norm2_g"], wl["w_gate"], wl["w_up"], wl["w_down"],
                                    f"ffn_bwd{l}", dep=sc_down.token)
        d_wg, d_wu = grad_weight((da, db), h2, TF, f"grad_w_gate_up{l}")
        sc_gu = scatter_start([slabs(d_wg), slabs(d_wu)], f"scatter_w_gate_up{l}")
        (d_wo,) = grad_weight((mix,), dx1, D_MODEL, f"grad_w_o{l}", dep=sc_gu.token)
        sc_o = scatter_start([slabs(d_wo)], f"scatter_w_o{l}")
        dmix = matmul_nt(dx1, wl["w_o"], f"out_proj_bwd{l}", dep=sc_o.token)
        finish(sc_down, ["w_down"], dmix, l)
        dproj, d_cw, d_qg, d_kg, d_sk, d_cog, d_aog = mixer_bwd(
            proj, bias, y, ao, dmix, sp["sinks"], sp["conv_w"], sp["q_norm_g"], sp["k_norm_g"],
            sp["conv_out_g"], sp["attn_out_g"], f"mixer_bwd{l}")
        finish(sc_gu, ["w_gate", "w_up"], dproj, l)
        finish(sc_o, ["w_o"], dproj, l)
        (d_win,) = grad_weight((dproj,), h, IN_COLS // 2, f"grad_w_in{l}")
        sc_in = scatter_start([slabs(d_win)], f"scatter_w_in{l}")
        dx, d_g1 = proj_bwd(dproj, wl["w_in"], x0, sp["norm1_g"], dx1, f"proj_bwd{l}", dep=sc_in.token)
        finish(sc_in, ["w_in"], dx, l)
        both_heads = lambda d: d[:, :HEAD] + d[:, HEAD:]
        gsmall[l] = dict(norm1_g=d_g1, conv_w=d_cw, q_norm_g=both_heads(d_qg), k_norm_g=both_heads(d_kg), sinks=d_sk,
                         conv_out_g=d_cog, attn_out_g=d_aog, norm2_g=d_g2)
    grad_x = dx[None]

    flat = jnp.concatenate([loss_row[0, 0:1]] + [gsmall[l][n].reshape(-1) for l in range(DEPTH) for n in SMALL_NAMES])
    flat = jnp.pad(flat, (0, SMALL_ROWS * 128 - flat.shape[0])).reshape(SMALL_ROWS, 128)
    flat = all_reduce_small(flat, "all_reduce_small_grads").reshape(-1)
    loss = flat[0]
    gs = {n: [] for n in SMALL_NAMES}
    off = 1
    for l in range(DEPTH):
        for n, size in zip(SMALL_NAMES, SMALL_SIZES):
            gs[n].append(flat[off:off + size])
            off += size
    gs = {n: jnp.stack(v) for n, v in gs.items()}
    g_conv = lax.dynamic_slice(gs["conv_w"].reshape(DEPTH, 3, CONV_CH), (0, 0, me * HEAD), (DEPTH, 3, HEAD))

    gs["conv_w"] = g_conv
    params = dict(norm1_g=(norm1_g, m_norm1_g, v_norm1_g), conv_w=(conv_w, m_conv_w, v_conv_w),
                  q_norm_g=(q_norm_g, m_q_norm_g, v_q_norm_g), k_norm_g=(k_norm_g, m_k_norm_g, v_k_norm_g),
                  sinks=(sinks, m_sinks, v_sinks), conv_out_g=(conv_out_g, m_conv_out_g, v_conv_out_g),
                  attn_out_g=(attn_out_g, m_attn_out_g, v_attn_out_g), norm2_g=(norm2_g, m_norm2_g, v_norm2_g))
    names = ("norm1_g", "w_in", "conv_w", "q_norm_g", "k_norm_g", "sinks", "conv_out_g", "attn_out_g", "w_o",
             "norm2_g", "w_gate", "w_up", "w_down")

    out = {}
    for n in names:
        if n in shard:
            out[n] = tuple(t(r) for r in stepped[n]) if n in ("w_in", "w_gate", "w_up") else stepped[n]
        else:
            w, m, v = params[n]
            two_d = (-1, w.shape[-1])
            d, mn, vn = adamw(w.reshape(two_d), gs[n].reshape(two_d), m.reshape(two_d), v.reshape(two_d), f"adamw_{n}")
            out[n] = (gs[n].reshape(w.shape), d.reshape(w.shape), mn.reshape(w.shape), vn.reshape(w.shape))
    return (loss, grad_x, *[out[n][i] for i in range(4) for n in names])
```

```python
import functools

import jax
import jax.numpy as jnp
from jax import lax
from jax.experimental import pallas as pl
from jax.experimental.pallas import tpu as pltpu

F32 = jnp.float32
BF16 = jnp.bfloat16

D_MODEL = 1024
CONV_CH = 512
ATTN_W = 512
N_Q = 8
N_KV = 2
GRP = N_Q // N_KV
HEAD = 64
IN_COLS = 2304
D_FF = 2816
BLK = 128
O_Q = 3 * CONV_CH
O_K = O_Q + ATTN_W
O_V = O_K + N_KV * HEAD
EPS = 1e-6
NEG_INF = -1e30
SCALE = HEAD ** -0.5
N_DEV = 8
DEPTH = 2

ADAM_LR = 0.001
ADAM_B1 = 0.9
ADAM_B2 = 0.999
ADAM_EPS = 1e-08
ADAM_WD = 0.01
ADAM_STEP = 10

VMEM_LIMIT = 56 * 1024 * 1024
TM = 512
MESH_T = pl.DeviceIdType.MESH

ROWS_IN, ROWS_O, ROWS_FF = IN_COLS // N_DEV, D_MODEL // N_DEV, D_FF // N_DEV


ANY = pl.BlockSpec(memory_space=pl.ANY)
HBM = pl.BlockSpec(memory_space=pltpu.HBM)
SEM = pl.BlockSpec(memory_space=pltpu.SEMAPHORE)


def _cparams(sem):
    return pltpu.CompilerParams(dimension_semantics=sem, vmem_limit_bytes=VMEM_LIMIT)


def _pallas_after(dep, body, *args, in_specs, **kw):
    if dep is None:
        return pl.pallas_call(body, in_specs=in_specs, **kw)(*args)

    def after_dep(dep_ref, *refs):
        body(*refs)

    return pl.pallas_call(after_dep, in_specs=[ANY, *in_specs], **kw)(dep, *args)


def _dot(a, b):
    return jnp.dot(a, b, preferred_element_type=F32)


def _dot_nt(a, b):
    return lax.dot_general(a, b, (((1,), (1,)), ((), ())), preferred_element_type=F32)


def _dot_tn(a, b):
    return lax.dot_general(a, b, (((0,), (0,)), ((), ())), preferred_element_type=F32)


LANES = 128


def _row_reduce(v, op, reduce):
    w = v.shape[-1]
    if w > LANES and w % LANES == 0:
        acc = v[:, 0:LANES]
        for c in range(1, w // LANES):
            acc = op(acc, v[:, LANES * c:LANES * (c + 1)])
        v = acc
    return reduce(v, axis=-1, keepdims=True)


def _row_sum(v):
    return _row_reduce(v, jnp.add, jnp.sum)


def _row_mean(v):
    return _row_sum(v) * (1.0 / v.shape[-1])


def _rms(v):
    return lax.rsqrt(_row_mean(v * v) + EPS)


def _sigmoid(v):
    return 1.0 / (1.0 + jnp.exp(-v))


def _rms_bwd(dyv, xh, r, g):
    dxh = dyv * g
    return r * (dxh - xh * _row_mean(dxh * xh))


def norm_proj(x, g, w_t, name, dep=None):
    S, N = x.shape[0], w_t.shape[0]

    def body(x_ref, g_ref, w_ref, h_ref, p_ref):
        xf = x_ref[...]
        h = ((xf * _rms(xf)) * g_ref[...]).astype(BF16)
        h_ref[...] = h
        p_ref[...] = _dot_nt(h, w_ref[...]).astype(BF16)

    return _pallas_after(
        dep, body, x, g, w_t, name=name, grid=(S // TM,),
        in_specs=[pl.BlockSpec((TM, D_MODEL), lambda i: (i, 0)),
                  pl.BlockSpec((1, D_MODEL), lambda i: (0, 0)),
                  pl.BlockSpec((N, D_MODEL), lambda i: (0, 0))],
        out_specs=[pl.BlockSpec((TM, D_MODEL), lambda i: (i, 0)),
                   pl.BlockSpec((TM, N), lambda i: (i, 0))],
        out_shape=[jax.ShapeDtypeStruct((S, D_MODEL), BF16), jax.ShapeDtypeStruct((S, N), BF16)],
        compiler_params=_cparams(("parallel",)),
    )


def band_bias():
    qi = lax.broadcasted_iota(jnp.int32, (BLK, 2 * BLK), 0)
    kj = lax.broadcasted_iota(jnp.int32, (BLK, 2 * BLK), 1)
    diff = qi + BLK - kj
    valid = (diff >= 0) & (diff < BLK)
    return jnp.stack([jnp.where(valid & (kj >= BLK), 0.0, NEG_INF), jnp.where(valid, 0.0, NEG_INF)]).astype(F32)


def _softmax_with_sink(s, sink):
    m = jnp.maximum(_row_reduce(s, jnp.maximum, jnp.max), sink)
    p = jnp.exp(s - m)
    es = jnp.exp(sink - m)
    inv = 1.0 / (_row_sum(p) + es)
    return p * inv, es * inv


PAIR = 2 * HEAD


def _low_half():
    return lax.broadcasted_iota(jnp.int32, (1, PAIR), 1) < HEAD


def _half_sums(v, low):
    return (jnp.sum(jnp.where(low, v, 0.0), axis=-1, keepdims=True),
            jnp.sum(jnp.where(low, 0.0, v), axis=-1, keepdims=True))


def _pair_mean(v, low):
    e, o = _half_sums(v, low)
    return jnp.where(low, e, o) * (1.0 / HEAD)


def _pair_rms(v, low):
    return lax.rsqrt(_pair_mean(v * v, low) + EPS)


def _one_head_in_both_halves(pair, low):
    swapped = pltpu.roll(pair, HEAD, 1)
    return jnp.where(low, pair, swapped), jnp.where(low, swapped, pair)


def _kv_pairs(p_ref, kvp_ref):
    k = jnp.concatenate([kvp_ref[:, 0:PAIR], p_ref[:, O_K:O_K + PAIR]], axis=0).astype(F32)
    v = jnp.concatenate([kvp_ref[:, PAIR:2 * PAIR], p_ref[:, O_V:O_V + PAIR]], axis=0)
    return k, v


def mixer_fwd(proj, bias, sinks, conv_w, qg, kg, cog, aog, name, residual=None, w_o=None):
    S = proj.shape[0]
    nb = S // BLK
    project = w_o is not None

    def body(sinks_ref, p_ref, kvp_ref, bias_ref, cw_ref, qg_ref, kg_ref, cog_ref, aog_ref, *rest):
        if project:
            x_ref, wo_ref, mix_ref, y_ref, ao_ref, x1_ref, ucar = rest
        else:
            mix_ref, y_ref, ao_ref, ucar = rest
        n = pl.program_id(0)

        @pl.when(n == 0)
        def _():
            ucar[...] = jnp.zeros_like(ucar)

        bg = p_ref[:, 0:CONV_CH].astype(F32)
        u = p_ref[:, CONV_CH:2 * CONV_CH].astype(F32) * p_ref[:, 2 * CONV_CH:3 * CONV_CH].astype(F32)
        row = lax.broadcasted_iota(jnp.int32, (BLK, CONV_CH), 0)
        prev = ucar[...]
        u1 = jnp.where(row == 0, prev[7:8, :], pltpu.roll(u, 1, 0))
        u2 = jnp.where(row == 0, prev[6:7, :], jnp.where(row == 1, prev[7:8, :], pltpu.roll(u, 2, 0)))
        ucar[...] = u[BLK - 8:BLK, :]
        y = cw_ref[0:1, :] * u2 + cw_ref[1:2, :] * u1 + cw_ref[2:3, :] * u
        y_ref[...] = y.astype(BF16)
        co = bg * y
        mix_conv = ((co * _rms(co)) * cog_ref[...]).astype(BF16)
        mix_ref[:, 0:CONV_CH] = mix_conv
        if project:
            x1 = x_ref[...] + _dot(mix_conv, wo_ref[0:CONV_CH, :])

        low = _low_half()
        q_gain = qg_ref[...] * SCALE
        k, v = _kv_pairs(p_ref, kvp_ref)
        kn = ((k * _pair_rms(k, low)) * kg_ref[...]).astype(BF16)
        k_of = _one_head_in_both_halves(kn, low)
        v_of = _one_head_in_both_halves(v, low)
        scores = []
        for pair in range(N_Q // 2):
            q = p_ref[:, O_Q + PAIR * pair:O_Q + PAIR * (pair + 1)].astype(F32)
            qn = ((q * _pair_rms(q, low)) * q_gain).astype(BF16)
            for half in range(2):
                q_one = jnp.where(low if half == 0 else jnp.logical_not(low), qn, jnp.zeros_like(qn))
                scores.append(_dot_nt(q_one, k_of[pair // (GRP // 2)]) + bias_ref[0])
        probs = [_softmax_with_sink(s, sinks_ref[hq])[0].astype(BF16) for hq, s in enumerate(scores)]
        o = [_dot(pn, v_of[hq // GRP]) for hq, pn in enumerate(probs)]
        ao = jnp.concatenate([jnp.where(low, o[2 * pair], o[2 * pair + 1]) for pair in range(N_Q // 2)], axis=1)
        ao_ref[...] = ao.astype(BF16)
        mix_attn = ((ao * _rms(ao)) * aog_ref[...]).astype(BF16)
        mix_ref[:, CONV_CH:] = mix_attn
        if project:
            x1_ref[...] = x1 + _dot(mix_attn, wo_ref[CONV_CH:, :])

    small = lambda shape: pl.BlockSpec(shape, lambda n: (0, 0))
    rows = lambda w: pl.BlockSpec((BLK, w), lambda n: (n, 0))
    return pl.pallas_call(
        body, name=name, grid=(nb,),
        in_specs=[pl.BlockSpec(memory_space=pltpu.SMEM),
                  rows(IN_COLS),
                  pl.BlockSpec((BLK, 2 * N_KV * HEAD), lambda n: (jnp.maximum(n - 1, 0), O_K // (2 * N_KV * HEAD))),
                  pl.BlockSpec((1, BLK, 2 * BLK), lambda n: (jnp.minimum(n, 1), 0, 0)),
                  small((3, CONV_CH)), small((1, PAIR)), small((1, PAIR)), small((1, CONV_CH)), small((1, ATTN_W))]
        + ([rows(D_MODEL), small((D_MODEL, D_MODEL))] if project else []),
        out_specs=[rows(D_MODEL), rows(CONV_CH), rows(ATTN_W)] + ([rows(D_MODEL)] if project else []),
        out_shape=[jax.ShapeDtypeStruct((S, D_MODEL), BF16), jax.ShapeDtypeStruct((S, CONV_CH), BF16),
                   jax.ShapeDtypeStruct((S, ATTN_W), BF16)]
        + ([jax.ShapeDtypeStruct((S, D_MODEL), F32)] if project else []),
        scratch_shapes=[pltpu.VMEM((8, CONV_CH), F32)],
        compiler_params=_cparams(("arbitrary",)),
    )(sinks, proj, proj, bias, conv_w, qg, kg, cog, aog, *([residual, w_o] if project else []))


def matmul_residual(a, w, res, name, dep=None):
    S, K = a.shape
    N = w.shape[1]

    def body(a_ref, w_ref, r_ref, o_ref):
        o_ref[...] = r_ref[...] + _dot(a_ref[...], w_ref[...])

    return _pallas_after(
        dep, body, a, w, res, name=name, grid=(S // TM,),
        in_specs=[pl.BlockSpec((TM, K), lambda i: (i, 0)), pl.BlockSpec((K, N), lambda i: (0, 0)),
                  pl.BlockSpec((TM, N), lambda i: (i, 0))],
        out_specs=pl.BlockSpec((TM, N), lambda i: (i, 0)),
        out_shape=jax.ShapeDtypeStruct((S, N), F32),
        compiler_params=_cparams(("parallel",)),
    )


TF = 1408


def ffn_fwd(x1, g, wg, wu, wd, name, dep=None, tgt=None):
    S = x1.shape[0]
    ni, nj = S // TM, D_FF // TF
    with_loss = tgt is not None

    def body(x_ref, g_ref, wg_ref, wu_ref, wd_ref, *rest):
        if with_loss:
            t_ref, h2_ref, a_ref, b_ref, o_ref, l_ref, acc, sq = rest
        else:
            h2_ref, a_ref, b_ref, o_ref, acc = rest
        i, j = pl.program_id(0), pl.program_id(1)

        @pl.when(j == 0)
        def _():
            xf = x_ref[...]
            h2_ref[...] = ((xf * _rms(xf)) * g_ref[...]).astype(BF16)
            acc[...] = xf

        h2 = h2_ref[...]
        a = _dot_nt(h2, wg_ref[...])
        b = _dot_nt(h2, wu_ref[...])
        a_ref[...] = a.astype(BF16)
        b_ref[...] = b.astype(BF16)
        f = (a * _sigmoid(a)) * b
        acc[...] += _dot(f.astype(BF16), wd_ref[...])

        @pl.when(j == nj - 1)
        def _():
            if not with_loss:
                o_ref[...] = acc[...]
                return
            e = acc[...] - t_ref[...]
            o_ref[...] = e * (1.0 / D_MODEL)
            col = jnp.sum(e * e, axis=0, keepdims=True)

            @pl.when(i == 0)
            def _():
                sq[...] = col

            @pl.when(i > 0)
            def _():
                sq[...] += col

            @pl.when(i == ni - 1)
            def _():
                l_ref[...] = jnp.full((1, 128), jnp.sum(sq[...]) * (0.5 / D_MODEL), F32)

    row = pl.BlockSpec((TM, D_MODEL), lambda i, j: (i, 0))
    tile = pl.BlockSpec((TM, TF), lambda i, j: (i, j))
    return _pallas_after(
        dep, body, x1, g, wg, wu, wd, *([tgt] if with_loss else []), name=name, grid=(ni, nj),
        in_specs=[row, pl.BlockSpec((1, D_MODEL), lambda i, j: (0, 0))]
        + [pl.BlockSpec((TF, D_MODEL), lambda i, j: (j, 0))] * 3 + ([row] if with_loss else []),
        out_specs=[row, tile, tile, row] + ([pl.BlockSpec((1, 128), lambda i, j: (0, 0))] if with_loss else []),
        out_shape=[jax.ShapeDtypeStruct((S, D_MODEL), BF16), jax.ShapeDtypeStruct((S, D_FF), BF16),
                   jax.ShapeDtypeStruct((S, D_FF), BF16), jax.ShapeDtypeStruct((S, D_MODEL), F32)]
        + ([jax.ShapeDtypeStruct((1, 128), F32)] if with_loss else []),
        scratch_shapes=[pltpu.VMEM((TM, D_MODEL), F32)] + ([pltpu.VMEM((1, D_MODEL), F32)] if with_loss else []),
        compiler_params=_cparams(("arbitrary", "arbitrary") if with_loss else ("parallel", "arbitrary")),
    )


def ffn_bwd(dx2, a, b, x1, g, wg, wu, wd, name, dep=None):
    S = dx2.shape[0]
    nj = D_FF // TF

    def body(dx_ref, a_ref, b_ref, x_ref, g_ref, wg_ref, wu_ref, wd_ref, da_ref, db_ref, dx1_ref, dg_ref, dxb, acc):
        i, j = pl.program_id(0), pl.program_id(1)

        @pl.when((i == 0) & (j == 0))
        def _():
            dg_ref[...] = jnp.zeros_like(dg_ref)

        @pl.when(j == 0)
        def _():
            dxb[...] = dx_ref[...].astype(BF16)
            acc[...] = jnp.zeros_like(acc)

        df = _dot_nt(dxb[...], wd_ref[...])
        av = a_ref[...].astype(F32)
        bv = b_ref[...].astype(F32)
        sg = _sigmoid(av)
        da =((df * bv) * (sg * (1.0 + av * (1.0 - sg)))).astype(BF16)
        db = (df * (av * sg)).astype(BF16)
        da_ref[...] = da
        db_ref[...] = db
        acc[...] += _dot(da, wg_ref[...]) + _dot(db, wu_ref[...])

        @pl.when(j == nj - 1)
        def _():
            xf = x_ref[...]
            r = _rms(xf)
            xh = xf * r
            dh = acc[...]
            dg_ref[...] += jnp.sum(dh * xh, axis=0, keepdims=True)
            dx1_ref[...] = dx_ref[...] + _rms_bwd(dh, xh, r, g_ref[...])

    return _pallas_after(
        dep, body, dx2, a, b, x1, g, wg, wu, wd, name=name, grid=(S // TM, nj),
        in_specs=[pl.BlockSpec((TM, D_MODEL), lambda i, j: (i, 0)),
                  pl.BlockSpec((TM, TF), lambda i, j: (i, j)), pl.BlockSpec((TM, TF), lambda i, j: (i, j)),
                  pl.BlockSpec((TM, D_MODEL), lambda i, j: (i, 0)), pl.BlockSpec((1, D_MODEL), lambda i, j: (0, 0))]
        + [pl.BlockSpec((TF, D_MODEL), lambda i, j: (j, 0))] * 3,
        out_specs=[pl.BlockSpec((TM, TF), lambda i, j: (i, j)), pl.BlockSpec((TM, TF), lambda i, j: (i, j)),
                   pl.BlockSpec((TM, D_MODEL), lambda i, j: (i, 0)), pl.BlockSpec((1, D_MODEL), lambda i, j: (0, 0))],
        out_shape=[jax.ShapeDtypeStruct((S, D_FF), BF16), jax.ShapeDtypeStruct((S, D_FF), BF16),
                   jax.ShapeDtypeStruct((S, D_MODEL), F32), jax.ShapeDtypeStruct((1, D_MODEL), F32)],
        scratch_shapes=[pltpu.VMEM((TM, D_MODEL), BF16), pltpu.VMEM((TM, D_MODEL), F32)],
        compiler_params=_cparams(("arbitrary", "arbitrary")),
    )


def grad_weight(lhs, rhs, tm, name, swiglu=False, dep=None):
    S, N = rhs.shape
    M = lhs[0].shape[1]
    tk = 1024
    nk = S // tk
    n_l = len(lhs)
    n_o = 1 if swiglu else n_l

    def body(*refs):
        l_refs, r_ref = refs[:n_l], refs[n_l]
        o_refs, accs = refs[n_l + 1:n_l + 1 + n_o], refs[n_l + 1 + n_o:]
        k = pl.program_id(1)
        rv = r_ref[...].astype(BF16)
        if swiglu:
            av = l_refs[0][...].astype(F32)
            lvs = [((av * _sigmoid(av)) * l_refs[1][...].astype(F32)).astype(BF16)]
        else:
            lvs = [l_ref[...] for l_ref in l_refs]
        for lv, o_ref, acc in zip(lvs, o_refs, accs):
            part = _dot_tn(lv, rv)

            @pl.when(k == 0)
            def _():
                acc[...] = part

            @pl.when(k > 0)
            def _():
                acc[...] += part

            @pl.when(k == nk - 1)
            def _():
                o_ref[...] = acc[...].astype(BF16)

    return _pallas_after(
        dep, body, *lhs, rhs, name=name, grid=(M // tm, nk),
        in_specs=[pl.BlockSpec((tk, tm), lambda i, k: (k, i)) for _ in lhs] + [pl.BlockSpec((tk, N), lambda i, k: (k, 0))],
        out_specs=[pl.BlockSpec((tm, N), lambda i, k: (i, 0))] * n_o,
        out_shape=[jax.ShapeDtypeStruct((M, N), BF16)] * n_o,
        scratch_shapes=[pltpu.VMEM((tm, N), F32)] * n_o,
        compiler_params=_cparams(("parallel", "arbitrary")),
    )


def matmul_nt(a, w, name, dep=None):
    S, K = a.shape
    N = w.shape[0]

    def body(a_ref, w_ref, o_ref):
        o_ref[...] = _dot_nt(a_ref[...].astype(BF16), w_ref[...]).astype(BF16)

    return _pallas_after(
        dep, body, a, w, name=name, grid=(S // TM,),
        in_specs=[pl.BlockSpec((TM, K), lambda i: (i, 0)), pl.BlockSpec((N, K), lambda i: (0, 0))],
        out_specs=pl.BlockSpec((TM, N), lambda i: (i, 0)),
        out_shape=jax.ShapeDtypeStruct((S, N), BF16),
        compiler_params=_cparams(("parallel",)),
    )


def proj_bwd(dproj, w, x, g, dres, name, dep=None):
    S, N = dproj.shape

    def body(dp_ref, w_ref, x_ref, g_ref, dr_ref, dx_ref, dg_ref):
        i = pl.program_id(0)

        @pl.when(i == 0)
        def _():
            dg_ref[...] = jnp.zeros_like(dg_ref)

        dh = _dot(dp_ref[...], w_ref[...])
        xf = x_ref[...]
        r = _rms(xf)
        xh = xf * r
        dg_ref[...] += jnp.sum(dh * xh, axis=0, keepdims=True)
        dx_ref[...] = dr_ref[...] + _rms_bwd(dh, xh, r, g_ref[...])

    return _pallas_after(
        dep, body, dproj, w, x, g, dres, name=name, grid=(S // TM,),
        in_specs=[pl.BlockSpec((TM, N), lambda i: (i, 0)), pl.BlockSpec((N, D_MODEL), lambda i: (0, 0)),
                  pl.BlockSpec((TM, D_MODEL), lambda i: (i, 0)), pl.BlockSpec((1, D_MODEL), lambda i: (0, 0)),
                  pl.BlockSpec((TM, D_MODEL), lambda i: (i, 0))],
        out_specs=[pl.BlockSpec((TM, D_MODEL), lambda i: (i, 0)), pl.BlockSpec((1, D_MODEL), lambda i: (0, 0))],
        out_shape=[jax.ShapeDtypeStruct((S, D_MODEL), F32), jax.ShapeDtypeStruct((1, D_MODEL), F32)],
        compiler_params=_cparams(("arbitrary",)),
    )


def mixer_bwd(proj, bias, y, ao, dx1, w_o, sinks, conv_w, qg, kg, cog, aog, name, dep=None):
    S = proj.shape[0]
    nb = S // BLK
    KV_W = 2 * N_KV * HEAD

    def body(sinks_ref, p_ref, kvp_ref, bias_ref, y_ref, ao_ref, dx_first_ref, dx_next_ref, wo_ref, cw_ref, qg_ref,
             kg_ref, cog_ref, aog_ref, dp_ref, dcw_ref, dqg_ref, dkg_ref, dsk_ref, dcog_ref, daog_ref,
             dycar, kcar, vcar, dmix_scr):
        step = pl.program_id(0)

        @pl.when(step == 0)
        def _():
            dmix_scr[...] = _dot_nt(dx_first_ref[...].astype(BF16), wo_ref[...])
            dycar[...] = jnp.zeros_like(dycar)
            kcar[...] = jnp.zeros_like(kcar)
            vcar[...] = jnp.zeros_like(vcar)
            dcw_ref[...] = jnp.zeros_like(dcw_ref)
            dqg_ref[...] = jnp.zeros_like(dqg_ref)
            dkg_ref[...] = jnp.zeros_like(dkg_ref)
            dsk_ref[...] = jnp.zeros_like(dsk_ref)
            dcog_ref[...] = jnp.zeros_like(dcog_ref)
            daog_ref[...] = jnp.zeros_like(daog_ref)

        dma = dmix_scr[:, CONV_CH:]

        aov = ao_ref[...].astype(F32)
        ra = _rms(aov)
        ah = aov * ra
        daog_ref[...] += jnp.sum(dma * ah, axis=0, keepdims=True)
        dao = _rms_bwd(dma, ah, ra, aog_ref[...])

        low = _low_half()
        high = jnp.logical_not(low)
        q_gain = qg_ref[...] * SCALE
        k, v = _kv_pairs(p_ref, kvp_ref)
        rk = _pair_rms(k, low)
        kh = k * rk
        k_of = _one_head_in_both_halves((kh * kg_ref[...]).astype(BF16), low)
        v_of = _one_head_in_both_halves(v, low)
        lane8 = lax.broadcasted_iota(jnp.int32, (1, N_Q), 1)
        dsk = jnp.zeros((1, N_Q), F32)
        dqg = jnp.zeros((1, PAIR), F32)
        dv_t = [jnp.zeros((PAIR, 2 * BLK), F32) for _ in range(N_KV)]
        dkn_t = [jnp.zeros((PAIR, 2 * BLK), F32) for _ in range(N_KV)]
        rq, qh, q_one, do_one, delta = [], [], [], [], []
        for pair in range(N_Q // 2):
            cols = slice(PAIR * pair, PAIR * (pair + 1))
            q = p_ref[:, O_Q + PAIR * pair:O_Q + PAIR * (pair + 1)].astype(F32)
            rq.append(_pair_rms(q, low))
            qh.append(q * rq[pair])
            qn = (qh[pair] * q_gain).astype(BF16)
            do = dao[:, cols]
            do_b = do.astype(BF16)
            delta += _half_sums(do * aov[:, cols], low)
            for mine in (low, high):
                q_one.append(jnp.where(mine, qn, jnp.zeros_like(qn)))
                do_one.append(jnp.where(mine, do_b, jnp.zeros_like(do_b)))
        heads = range(N_Q)
        scores = [_dot_nt(q_one[hq], k_of[hq // GRP]) + bias_ref[0] for hq in heads]
        dprobs = [_dot_nt(do_one[hq], v_of[hq // GRP]) for hq in heads]
        probs = [_softmax_with_sink(scores[hq], sinks_ref[hq]) for hq in heads]
        ds = [(probs[hq][0] * (dprobs[hq] - delta[hq])).astype(BF16) for hq in heads]
        for hq in heads:
            dsk = dsk - jnp.where(lane8 == hq, jnp.sum(probs[hq][1] * delta[hq]), 0.0)
            dv_t[hq // GRP] = dv_t[hq // GRP] + _dot_tn(do_one[hq], probs[hq][0].astype(BF16))
            dkn_t[hq // GRP] = dkn_t[hq // GRP] + _dot_tn(q_one[hq], ds[hq])
        dqn_of = [_dot(ds[hq], k_of[hq // GRP]) for hq in heads]
        for pair in range(N_Q // 2):
            dqn = jnp.where(low, dqn_of[2 * pair], dqn_of[2 * pair + 1])
            dqg = dqg + jnp.sum(dqn * qh[pair], axis=0, keepdims=True)
            dqh = dqn * q_gain
            dp_ref[:, O_Q + PAIR * pair:O_Q + PAIR * (pair + 1)] = (
                rq[pair] * (dqh - qh[pair] * _pair_mean(dqh * qh[pair], low))).astype(BF16)

        def untranspose(parts):
            return jnp.concatenate([t[:HEAD] + t[HEAD:] for t in parts], axis=0).T

        dv = untranspose(dv_t)
        dkn = untranspose(dkn_t)
        dkg_ref[...] += jnp.sum(dkn * kh, axis=0, keepdims=True)
        dkh = dkn * kg_ref[...]
        dk = rk * (dkh - kh * _pair_mean(dkh * kh, low))
        dp_ref[:, O_K:O_V] = (kcar[...] + dk[BLK:, :]).astype(BF16)
        dp_ref[:, O_V:] = (vcar[...] + dv[BLK:, :]).astype(BF16)
        kcar[...] = dk[:BLK, :]
        vcar[...] = dv[:BLK, :]
        dsk_ref[...] += dsk
        dqg_ref[...] += dqg * SCALE

        bg = p_ref[:, 0:CONV_CH].astype(F32)
        cg = p_ref[:, CONV_CH:2 * CONV_CH].astype(F32)
        hc = p_ref[:, 2 * CONV_CH:3 * CONV_CH].astype(F32)
        yv = y_ref[...].astype(F32)
        dmc = dmix_scr[:, 0:CONV_CH]
        co = bg * yv
        rc = _rms(co)
        ch = co * rc
        dcog_ref[...] += jnp.sum(dmc * ch, axis=0, keepdims=True)
        dco = _rms_bwd(dmc, ch, rc, cog_ref[...])
        dp_ref[:, 0:CONV_CH] = (dco * yv).astype(BF16)
        dy = dco * bg
        row = lax.broadcasted_iota(jnp.int32, (BLK, CONV_CH), 0)
        nxt = dycar[...]
        dy1 = jnp.where(row == BLK - 1, nxt[0:1, :], pltpu.roll(dy, BLK - 1, 0))
        dy2 = jnp.where(row == BLK - 2, nxt[0:1, :], jnp.where(row == BLK - 1, nxt[1:2, :], pltpu.roll(dy, BLK - 2, 0)))
        dycar[...] = dy[0:8, :]
        du = cw_ref[2:3, :] * dy + cw_ref[1:2, :] * dy1 + cw_ref[0:1, :] * dy2
        dp_ref[:, CONV_CH:2 * CONV_CH] = (du * hc).astype(BF16)
        dp_ref[:, 2 * CONV_CH:3 * CONV_CH] = (du * cg).astype(BF16)
        u = cg * hc
        dcw_ref[0:1, :] += jnp.sum(dy2 * u, axis=0, keepdims=True)
        dcw_ref[1:2, :] += jnp.sum(dy1 * u, axis=0, keepdims=True)
        dcw_ref[2:3, :] += jnp.sum(dy * u, axis=0, keepdims=True)

        dmix_scr[...] = _dot_nt(dx_next_ref[...].astype(BF16), wo_ref[...])

    small = lambda shape: pl.BlockSpec(shape, lambda s: (0, 0))
    blk = lambda w: pl.BlockSpec((BLK, w), lambda s: (nb - 1 - s, 0))
    return _pallas_after(
        dep, body, sinks, proj, proj, bias, y, ao, dx1, dx1, w_o, conv_w, qg, kg, cog, aog, name=name, grid=(nb,),
        in_specs=[pl.BlockSpec(memory_space=pltpu.SMEM),
                  blk(IN_COLS),
                  pl.BlockSpec((BLK, KV_W), lambda s: (jnp.maximum(nb - 2 - s, 0), O_K // KV_W)),
                  pl.BlockSpec((1, BLK, 2 * BLK), lambda s: (jnp.minimum(nb - 1 - s, 1), 0, 0)),
                  blk(CONV_CH), blk(ATTN_W),
                  pl.BlockSpec((BLK, D_MODEL), lambda s: (nb - 1, 0)),
                  pl.BlockSpec((BLK, D_MODEL), lambda s: (jnp.maximum(nb - 2 - s, 0), 0)),
                  small((D_MODEL, D_MODEL)),
                  small((3, CONV_CH)), small((1, PAIR)), small((1, PAIR)), small((1, CONV_CH)), small((1, ATTN_W))],
        out_specs=[blk(IN_COLS), small((3, CONV_CH)), small((1, PAIR)), small((1, PAIR)), small((1, N_Q)),
                   small((1, CONV_CH)), small((1, ATTN_W))],
        out_shape=[jax.ShapeDtypeStruct((S, IN_COLS), BF16), jax.ShapeDtypeStruct((3, CONV_CH), F32),
                   jax.ShapeDtypeStruct((1, PAIR), F32), jax.ShapeDtypeStruct((1, PAIR), F32),
                   jax.ShapeDtypeStruct((1, N_Q), F32), jax.ShapeDtypeStruct((1, CONV_CH), F32),
                   jax.ShapeDtypeStruct((1, ATTN_W), F32)],
        scratch_shapes=[pltpu.VMEM((8, CONV_CH), F32), pltpu.VMEM((BLK, PAIR), F32), pltpu.VMEM((BLK, PAIR), F32),
                        pltpu.VMEM((BLK, D_MODEL), F32)],
        compiler_params=_cparams(("arbitrary",)),
    )


OTHER_CHIPS = ((1, 0), (0, 1), (1, 1))


def _position():
    return lax.axis_index("x"), lax.axis_index("y"), lax.axis_index("c")


def all_gather(shards, name):
    n = len(shards)

    def body(*refs):
        x_refs, out_refs = refs[:n], refs[n:2 * n]
        send_sems, recv_sems, local_sems = refs[2 * n:]
        x, y, c = _position()
        me, sibling = (x, y, c), (x, y, 1 - c)
        chips = [(x ^ mx, y ^ my) for mx, my in OTHER_CHIPS]

        def slab(a, px, py, pc):
            return out_refs[a].at[4 * px + 2 * py + pc]

        def copy(a, k, block, to, src=None):
            return pltpu.make_async_remote_copy(
                src_ref=slab(a, *block) if src is None else src, dst_ref=slab(a, *block),
                send_sem=send_sems.at[7 * a + k], recv_sem=recv_sems.at[7 * a + k], device_id=to, device_id_type=MESH_T)

        mine = [pltpu.make_async_copy(x_refs[a], slab(a, *me), local_sems.at[a]) for a in range(n)]
        for cp in mine:
            cp.start()
        first = []
        for a in range(n):
            first.append(copy(a, 0, me, sibling, src=x_refs[a]))
            first += [copy(a, 1 + j, me, (*chip, c), src=x_refs[a]) for j, chip in enumerate(chips)]
        for cp in first:
            cp.start()
        passed = []
        for a in range(n):
            for j, chip in enumerate(chips):
                copy(a, 1 + j, (*chip, c), me).wait_recv()
                passed.append(copy(a, 4 + j, (*chip, c), sibling))
                passed[-1].start()
        for a in range(n):
            copy(a, 0, sibling, me).wait_recv()
            for j, chip in enumerate(chips):
                copy(a, 4 + j, (*chip, 1 - c), me).wait_recv()
        for cp in first + passed:
            cp.wait_send()
        for cp in mine:
            cp.wait()

    return pl.pallas_call(
        body, name=name, in_specs=[ANY] * n, out_specs=[ANY] * n,
        out_shape=[jax.ShapeDtypeStruct((N_DEV, *s.shape), s.dtype) for s in shards],
        scratch_shapes=[pltpu.SemaphoreType.DMA((7 * n,)), pltpu.SemaphoreType.DMA((7 * n,)),
                        pltpu.SemaphoreType.DMA((n,))],
    )(*shards)


class SplitCopy:
    def __init__(self, name, arrays, n_copies, plan, after=None):
        n = len(arrays)
        self.name, self.n, self.n_copies, self.plan = name, n, n_copies, plan
        extra = [] if after is None else [after]

        def body(*refs):
            in_refs = refs[:n]
            send_sems, recv_sems = refs[n + len(extra)], refs[n + len(extra) + 1]
            token = refs[2 * n + len(extra) + 2]
            for k, (src, dst, to) in enumerate(plan(_position(), in_refs)):
                pltpu.make_async_remote_copy(src_ref=src, dst_ref=dst, send_sem=send_sems.at[k],
                                             recv_sem=recv_sems.at[k], device_id=to, device_id_type=MESH_T).start()
            token[...] = jnp.zeros_like(token)

        outs = pl.pallas_call(
            body, name=name + "_start",
            out_shape=(pltpu.SemaphoreType.DMA((n_copies,)), pltpu.SemaphoreType.DMA((n_copies,)),
                       *[pltpu.HBM(a.shape, a.dtype) for a in arrays], jax.ShapeDtypeStruct((8, 128), F32)),
            in_specs=[HBM] * n + [ANY] * len(extra),
            out_specs=(SEM, SEM, *[HBM] * n, pl.BlockSpec(memory_space=pltpu.VMEM)),
            input_output_aliases={i: 2 + i for i in range(n)},
            compiler_params=pltpu.CompilerParams(has_side_effects=pltpu.SideEffectType.DATAFLOW_SIDE_EFFECTING),
        )(*[pltpu.with_memory_space_constraint(a, pltpu.HBM) for a in arrays], *extra)
        self.send_sems, self.recv_sems = outs[0], outs[1]
        self.arrays, self.token = list(outs[2:2 + n]), outs[2 + n]

    def wait(self, after):
        n, plan = self.n, self.plan

        def body(*refs):
            in_refs, send_sems, recv_sems = refs[:n], refs[n], refs[n + 1]
            for k, (src, dst, to) in enumerate(plan(_position(), in_refs)):
                cp = pltpu.make_async_remote_copy(src_ref=src, dst_ref=dst, send_sem=send_sems.at[k],
                                                  recv_sem=recv_sems.at[k], device_id=to, device_id_type=MESH_T)
                cp.wait_send()
                cp.wait_recv()

        outs = pl.pallas_call(
            body, name=self.name + "_wait",
            out_shape=tuple(pltpu.HBM(a.shape, a.dtype) for a in self.arrays),
            in_specs=[HBM] * n + [SEM, SEM, ANY], out_specs=tuple([HBM] * n),
            input_output_aliases={i: i for i in range(n)},
            compiler_params=pltpu.CompilerParams(has_side_effects=pltpu.SideEffectType.DATAFLOW_SIDE_EFFECTING),
        )(*self.arrays, self.send_sems, self.recv_sems, after)
        return list(outs)


def gather_start(shards, me, name, after=None):
    n = len(shards)
    lands = [lax.dynamic_update_slice(lax.empty((N_DEV, *s.shape), s.dtype), s[None], (me, 0, 0)) for s in shards]

    def plan(pos, refs):
        x, y, c = pos
        return [(refs[a], refs[n + a].at[4 * x + 2 * y + c], (x ^ mx, y ^ my, c))
                for a in range(n) for mx, my in OTHER_CHIPS]

    return SplitCopy(name, list(shards) + lands, 3 * n, plan, after=after)


def sibling_start(lands, name):
    n = len(lands)

    def plan(pos, refs):
        x, y, c = pos
        return [(refs[a].at[2 * q + c], refs[a].at[2 * q + c], (x, y, 1 - c)) for a in range(n) for q in range(4)]

    return SplitCopy(name, list(lands), 4 * n, plan)


def scatter_start(slabs, name):
    n = len(slabs)
    lands = [lax.empty((N_DEV - 1, *g.shape[1:]), g.dtype) for g in slabs]

    def plan(pos, refs):
        x, y, c = pos
        copies = []
        for a in range(n):
            for r in range(1, N_DEV):
                px, py, pc = x ^ ((r >> 2) & 1), y ^ ((r >> 1) & 1), c ^ (r & 1)
                copies.append((refs[a].at[4 * px + 2 * py + pc], refs[n + a].at[r - 1], (px, py, pc)))
        return copies

    return SplitCopy(name, list(slabs) + lands, (N_DEV - 1) * n, plan)


def all_reduce_small(v, name, dep=None):
    R, W = v.shape

    def body(v_ref, o_ref, recv, send_sems, recv_sems):
        x, y, c = _position()
        me = 4 * x + 2 * y + c
        copies = []
        for r in range(1, N_DEV):
            to = (x ^ ((r >> 2) & 1), y ^ ((r >> 1) & 1), c ^ (r & 1))
            copies.append(pltpu.make_async_remote_copy(
                src_ref=v_ref, dst_ref=recv.at[me], send_sem=send_sems.at[r - 1], recv_sem=recv_sems.at[r - 1],
                device_id=to, device_id_type=MESH_T))
        for cp in copies:
            cp.start()
        recv[pl.ds(me, 1)] = v_ref[...][None]
        for cp in copies:
            cp.wait()
        acc = recv[0]
        for s in range(1, N_DEV):
            acc = acc + recv[s]
        o_ref[...] = acc

    return _pallas_after(
        dep, body, v, name=name,
        in_specs=[pl.BlockSpec(memory_space=pltpu.VMEM)], out_specs=pl.BlockSpec(memory_space=pltpu.VMEM),
        out_shape=jax.ShapeDtypeStruct((R, W), F32),
        scratch_shapes=[pltpu.VMEM((N_DEV, R, W), F32), pltpu.SemaphoreType.DMA((N_DEV - 1,)),
                        pltpu.SemaphoreType.DMA((N_DEV - 1,))],
    )


def _row_tile(rows):
    if rows <= 512:
        return rows
    return max(t for t in range(8, 513, 8) if rows % t == 0)


def _adamw_update(w, g, m, v):
    mn = ADAM_B1 * m + (1.0 - ADAM_B1) * g
    vn = ADAM_B2 * v + (1.0 - ADAM_B2) * (g * g)
    m_hat = mn / (1.0 - ADAM_B1 ** ADAM_STEP)
    v_hat = vn / (1.0 - ADAM_B2 ** ADAM_STEP)
    return -ADAM_LR * (m_hat / (jnp.sqrt(v_hat) + ADAM_EPS) + ADAM_WD * w), mn, vn


def adamw(w, g, m, v, name):
    R, W = w.shape
    tr = _row_tile(R)

    def body(w_ref, g_ref, m_ref, v_ref, d_ref, mo_ref, vo_ref):
        d_ref[...], mo_ref[...], vo_ref[...] = _adamw_update(w_ref[...], g_ref[...], m_ref[...], v_ref[...])

    spec = pl.BlockSpec((tr, W), lambda i: (i, 0))
    return pl.pallas_call(
        body, name=name, grid=(R // tr,), in_specs=[spec] * 4, out_specs=[spec] * 3,
        out_shape=[jax.ShapeDtypeStruct((R, W), F32)] * 3,
        compiler_params=_cparams(("parallel",)),
    )(w, g, m, v)


def reduce_adamw(slabs, land, w, m, v, layer, me_arr, name, others=None):
    _, R, W = slabs.shape
    tr = _row_tile(R)
    n_other = 0 if others is None else 4

    def body(me_ref, s_ref, l_ref, w_ref, m_ref, v_ref, *rest):
        g_ref, d_ref, mo_ref, vo_ref = rest[n_other:]
        g = s_ref[0].astype(F32)
        for r in range(N_DEV - 1):
            g = g + l_ref[r].astype(F32)
        g_ref[0] = g
        d_ref[0], mo_ref[0], vo_ref[0] = _adamw_update(w_ref[0], g, m_ref[0], v_ref[0])

    spec = pl.BlockSpec((1, tr, W), lambda i, me: (layer, i, 0))
    return pl.pallas_call(
        body, name=name,
        grid_spec=pltpu.PrefetchScalarGridSpec(
            num_scalar_prefetch=1, grid=(R // tr,),
            in_specs=[pl.BlockSpec((1, tr, W), lambda i, me: (me[0], i, 0)),
                      pl.BlockSpec((N_DEV - 1, tr, W), lambda i, me: (0, i, 0)), spec, spec, spec] + [ANY] * n_other,
            out_specs=[spec] * 4),
        out_shape=[jax.ShapeDtypeStruct((DEPTH, R, W), F32)] * 4,
        input_output_aliases={6 + i: i for i in range(n_other)},
        compiler_params=_cparams(("parallel",)),
    )(me_arr, slabs, land, w, m, v, *([] if others is None else others))


SMALL_NAMES = ("norm1_g", "q_norm_g", "k_norm_g", "sinks", "conv_out_g", "attn_out_g", "norm2_g", "conv_w")
SMALL_SIZES = (D_MODEL, HEAD, HEAD, N_Q, CONV_CH, ATTN_W, D_MODEL, 3 * CONV_CH)
SMALL_ROWS = 80


def kernel(x, norm1_g, w_in, conv_w, q_norm_g, k_norm_g, sinks, conv_out_g, attn_out_g, w_o, norm2_g, w_gate, w_up, w_down, loss_target, m_norm1_g, m_w_in, m_conv_w, m_q_norm_g, m_k_norm_g, m_sinks, m_conv_out_g, m_attn_out_g, m_w_o, m_norm2_g, m_w_gate, m_w_up, m_w_down, v_norm1_g, v_w_in, v_conv_w, v_q_norm_g, v_k_norm_g, v_sinks, v_conv_out_g, v_attn_out_g, v_w_o, v_norm2_g, v_w_gate, v_w_up, v_w_down):
    xi, yi, ci = _position()
    me = 4 * xi + 2 * yi + ci
    me_arr = jnp.reshape(me, (1,)).astype(jnp.int32)
    xs, tgt = x[0], loss_target[0]
    bf = lambda a: a.astype(BF16)
    bias = band_bias()

    t = lambda a: jnp.swapaxes(a, 1, 2)
    shard = dict(w_in=(t(w_in), t(m_w_in), t(v_w_in)), w_o=(w_o, m_w_o, v_w_o),
                 w_gate=(t(w_gate), t(m_w_gate), t(v_w_gate)), w_up=(t(w_up), t(m_w_up), t(v_w_up)),
                 w_down=(w_down, m_w_down, v_w_down))
    wb = {n: bf(shard[n][0]) for n in shard}

    g_in0, g_o0, g_conv = all_gather([wb["w_in"][0], wb["w_o"][0], conv_w.reshape(DEPTH * 3, HEAD)], "gather_first")
    ag_ffn0 = gather_start([wb["w_gate"][0], wb["w_up"][0], wb["w_down"][0]], me, "gather_ffn0", after=g_in0)
    ag_mix1 = gather_start([wb["w_in"][1], wb["w_o"][1]], me, "gather_mix1", after=ag_ffn0.token)
    ag_ffn1 = gather_start([wb["w_gate"][1], wb["w_up"][1], wb["w_down"][1]], me, "gather_ffn1", after=ag_mix1.token)
    conv_full = g_conv.reshape(N_DEV, DEPTH, 3, HEAD).transpose(1, 2, 0, 3).reshape(DEPTH, 3, CONV_CH)
    pair_gain = lambda g: jnp.tile(g[None], (1, 2))
    small = [dict(norm1_g=norm1_g[l][None], conv_w=conv_full[l], q_norm_g=pair_gain(q_norm_g[l]),
                  k_norm_g=pair_gain(k_norm_g[l]), sinks=sinks[l], conv_out_g=conv_out_g[l][None],
                  attn_out_g=attn_out_g[l][None], norm2_g=norm2_g[l][None]) for l in range(DEPTH)]
    whole = lambda g: g.reshape(-1, D_MODEL)
    weights = [dict(w_in=whole(g_in0), w_o=whole(g_o0)), {}]

    def ffn_weights(g_gate, g_up, g_down):
        return dict(w_gate=whole(g_gate), w_up=whole(g_up), w_down=whole(g_down))

    saved = []
    xl = xs
    for l in range(DEPTH):
        sp, wl = small[l], weights[l]
        h, proj = norm_proj(xl, sp["norm1_g"], wl["w_in"], f"norm_proj{l}",
                            dep=ag_ffn1.token if l == 0 else pass_ffn1.token)
        mixer_args = (proj, bias, sp["sinks"], sp["conv_w"], sp["q_norm_g"], sp["k_norm_g"], sp["conv_out_g"],
                      sp["attn_out_g"], f"mixer_fwd{l}")
        if l == 0:
            mix, y, ao = mixer_fwd(*mixer_args)
            pass_ffn0 = sibling_start(ag_ffn0.wait(mix)[3:], "pass_ffn0")
            x1 = matmul_residual(mix, wl["w_o"], xl, "out_proj0", dep=pass_ffn0.token)
            wl.update(ffn_weights(*pass_ffn0.wait(x1)))
            pass_mix1 = sibling_start(ag_mix1.wait(x1)[2:], "pass_mix1")
            h2, a, b, x2 = ffn_fwd(x1, sp["norm2_g"], wl["w_gate"], wl["w_up"], wl["w_down"], "ffn_fwd0",
                                   dep=pass_mix1.token)
            g_in, g_o = pass_mix1.wait(x2)
            weights[1] = dict(w_in=whole(g_in), w_o=whole(g_o))
            pass_ffn1 = sibling_start(ag_ffn1.wait(x2)[3:], "pass_ffn1")
        else:
            mix, y, ao, x1 = mixer_fwd(*mixer_args, residual=xl, w_o=wl["w_o"])
            wl.update(ffn_weights(*pass_ffn1.wait(x1)))
            h2, a, b, dx, loss_row = ffn_fwd(x1, sp["norm2_g"], wl["w_gate"], wl["w_up"], wl["w_down"], "ffn_fwd1",
                                             tgt=tgt)
            x2 = None
        saved.append((xl, h, proj, mix, y, ao, x1, h2, a, b))
        xl = x2

    stepped = {n: None for n in shard}
    slabs = lambda d: d.reshape(N_DEV, -1, D_MODEL)
    gsmall = [None] * DEPTH

    def finish(sc, names, after, l):
        arrays = sc.wait(after)
        k = len(names)
        for i, n in enumerate(names):
            w, m, v = shard[n]
            stepped[n] = reduce_adamw(arrays[i], arrays[k + i], w, m, v, l, me_arr, f"reduce_adamw_{n}{l}",
                                      others=stepped[n])

    for l in reversed(range(DEPTH)):
        sp, wl = small[l], weights[l]
        x0, h, proj, mix, y, ao, x1, h2, a, b = saved[l]
        (d_wd,) = grad_weight((a, b), dx, TF, f"grad_w_down{l}", swiglu=True)
        sc_down = scatter_start([slabs(d_wd)], f"scatter_w_down{l}")
        da, db, dx1, d_g2 = ffn_bwd(dx, a, b, x1, sp["norm2_g"], wl["w_gate"], wl["w_up"], wl["w_down"],
                                    f"ffn_bwd{l}", dep=sc_down.token)
        d_wg, d_wu = grad_weight((da, db), h2, TF, f"grad_w_gate_up{l}")
        sc_gu = scatter_start([slabs(d_wg), slabs(d_wu)], f"scatter_w_gate_up{l}")
        (d_wo,) = grad_weight((mix,), dx1, D_MODEL, f"grad_w_o{l}", dep=sc_gu.token)
        sc_o = scatter_start([slabs(d_wo)], f"scatter_w_o{l}")
        dproj, d_cw, d_qg, d_kg, d_sk, d_cog, d_aog = mixer_bwd(
            proj, bias, y, ao, dx1, wl["w_o"], sp["sinks"], sp["conv_w"], sp["q_norm_g"], sp["k_norm_g"],
            sp["conv_out_g"], sp["attn_out_g"], f"mixer_bwd{l}", dep=sc_o.token)
        finish(sc_down, ["w_down"], dproj, l)
        finish(sc_gu, ["w_gate", "w_up"], dproj, l)
        finish(sc_o, ["w_o"], dproj, l)
        (d_win,) = grad_weight((dproj,), h, IN_COLS // 2, f"grad_w_in{l}")
        sc_in = scatter_start([slabs(d_win)], f"scatter_w_in{l}")
        dx, d_g1 = proj_bwd(dproj, wl["w_in"], x0, sp["norm1_g"], dx1, f"proj_bwd{l}", dep=sc_in.token)
        finish(sc_in, ["w_in"], dx, l)
        both_heads = lambda d: d[:, :HEAD] + d[:, HEAD:]
        gsmall[l] = dict(norm1_g=d_g1, conv_w=d_cw, q_norm_g=both_heads(d_qg), k_norm_g=both_heads(d_kg), sinks=d_sk,
                         conv_out_g=d_cog, attn_out_g=d_aog, norm2_g=d_g2)
    grad_x = dx[None]

    flat = jnp.concatenate([loss_row[0, 0:1]] + [gsmall[l][n].reshape(-1) for l in range(DEPTH) for n in SMALL_NAMES])
    flat = jnp.pad(flat, (0, SMALL_ROWS * 128 - flat.shape[0])).reshape(SMALL_ROWS, 128)
    flat = all_reduce_small(flat, "all_reduce_small_grads").reshape(-1)
    loss = flat[0]
    gs = {n: [] for n in SMALL_NAMES}
    off = 1
    for l in range(DEPTH):
        for n, size in zip(SMALL_NAMES, SMALL_SIZES):
            gs[n].append(flat[off:off + size])
            off += size
    gs = {n: jnp.stack(v) for n, v in gs.items()}
    g_conv = lax.dynamic_slice(gs["conv_w"].reshape(DEPTH, 3, CONV_CH), (0, 0, me * HEAD), (DEPTH, 3, HEAD))

    gs["conv_w"] = g_conv
    params = dict(norm1_g=(norm1_g, m_norm1_g, v_norm1_g), conv_w=(conv_w, m_conv_w, v_conv_w),
                  q_norm_g=(q_norm_g, m_q_norm_g, v_q_norm_g), k_norm_g=(k_norm_g, m_k_norm_g, v_k_norm_g),
                  sinks=(sinks, m_sinks, v_sinks), conv_out_g=(conv_out_g, m_conv_out_g, v_conv_out_g),
                  attn_out_g=(attn_out_g, m_attn_out_g, v_attn_out_g), norm2_g=(norm2_g, m_norm2_g, v_norm2_g))
    names = ("norm1_g", "w_in", "conv_w", "q_norm_g", "k_norm_g", "sinks", "conv_out_g", "attn_out_g", "w_o",
             "norm2_g", "w_gate", "w_up", "w_down")

    out = {}
    for n in names:
        if n in shard:
            out[n] = tuple(t(r) for r in stepped[n]) if n in ("w_in", "w_gate", "w_up") else stepped[n]
        else:
            w, m, v = params[n]
            two_d = (-1, w.shape[-1])
            d, mn, vn = adamw(w.reshape(two_d), gs[n].reshape(two_d), m.reshape(two_d), v.reshape(two_d), f"adamw_{n}")
            out[n] = (gs[n].reshape(w.shape), d.reshape(w.shape), mn.reshape(w.shape), vn.reshape(w.shape))
    return (loss, grad_x, *[out[n][i] for i in range(4) for n in names])
```

```python
import functools

import jax
import jax.numpy as jnp
from jax import lax
from jax.experimental import pallas as pl
from jax.experimental.pallas import tpu as pltpu

F32 = jnp.float32
BF16 = jnp.bfloat16

D_MODEL = 1024
CONV_CH = 512
ATTN_W = 512
N_Q = 8
N_KV = 2
GRP = N_Q // N_KV
HEAD = 64
IN_COLS = 2304
D_FF = 2816
BLK = 128
O_Q = 3 * CONV_CH
O_K = O_Q + ATTN_W
O_V = O_K + N_KV * HEAD
EPS = 1e-6
NEG_INF = -1e30
SCALE = HEAD ** -0.5
N_DEV = 8
DEPTH = 2

ADAM_LR = 0.001
ADAM_B1 = 0.9
ADAM_B2 = 0.999
ADAM_EPS = 1e-08
ADAM_WD = 0.01
ADAM_STEP = 10

VMEM_LIMIT = 56 * 1024 * 1024
TM = 512
MESH_T = pl.DeviceIdType.MESH

ROWS_IN, ROWS_O, ROWS_FF = IN_COLS // N_DEV, D_MODEL // N_DEV, D_FF // N_DEV


ANY = pl.BlockSpec(memory_space=pl.ANY)
HBM = pl.BlockSpec(memory_space=pltpu.HBM)
SEM = pl.BlockSpec(memory_space=pltpu.SEMAPHORE)


def _cparams(sem):
    return pltpu.CompilerParams(dimension_semantics=sem, vmem_limit_bytes=VMEM_LIMIT)


def _pallas_after(dep, body, *args, in_specs, **kw):
    if dep is None:
        return pl.pallas_call(body, in_specs=in_specs, **kw)(*args)

    def after_dep(dep_ref, *refs):
        body(*refs)

    return pl.pallas_call(after_dep, in_specs=[ANY, *in_specs], **kw)(dep, *args)


def _dot(a, b):
    return jnp.dot(a, b, preferred_element_type=F32)


def _dot_nt(a, b):
    return lax.dot_general(a, b, (((1,), (1,)), ((), ())), preferred_element_type=F32)


def _dot_tn(a, b):
    return lax.dot_general(a, b, (((0,), (0,)), ((), ())), preferred_element_type=F32)


LANES = 128


def _row_reduce(v, op, reduce):
    w = v.shape[-1]
    if w > LANES and w % LANES == 0:
        acc = v[:, 0:LANES]
        for c in range(1, w // LANES):
            acc = op(acc, v[:, LANES * c:LANES * (c + 1)])
        v = acc
    return reduce(v, axis=-1, keepdims=True)


def _row_sum(v):
    return _row_reduce(v, jnp.add, jnp.sum)


def _row_mean(v):
    return _row_sum(v) * (1.0 / v.shape[-1])


def _rms(v):
    return lax.rsqrt(_row_mean(v * v) + EPS)


def _sigmoid(v):
    return 1.0 / (1.0 + jnp.exp(-v))


def _rms_bwd(dyv, xh, r, g):
    dxh = dyv * g
    return r * (dxh - xh * _row_mean(dxh * xh))


def norm_proj(x, g, w_t, name, dep=None):
    S, N = x.shape[0], w_t.shape[0]

    def body(x_ref, g_ref, w_ref, h_ref, p_ref):
        xf = x_ref[...]
        h = ((xf * _rms(xf)) * g_ref[...]).astype(BF16)
        h_ref[...] = h
        p_ref[...] = _dot_nt(h, w_ref[...]).astype(BF16)

    return _pallas_after(
        dep, body, x, g, w_t, name=name, grid=(S // TM,),
        in_specs=[pl.BlockSpec((TM, D_MODEL), lambda i: (i, 0)),
                  pl.BlockSpec((1, D_MODEL), lambda i: (0, 0)),
                  pl.BlockSpec((N, D_MODEL), lambda i: (0, 0))],
        out_specs=[pl.BlockSpec((TM, D_MODEL), lambda i: (i, 0)),
                   pl.BlockSpec((TM, N), lambda i: (i, 0))],
        out_shape=[jax.ShapeDtypeStruct((S, D_MODEL), BF16), jax.ShapeDtypeStruct((S, N), BF16)],
        compiler_params=_cparams(("parallel",)),
    )


def band_bias():
    qi = lax.broadcasted_iota(jnp.int32, (BLK, 2 * BLK), 0)
    kj = lax.broadcasted_iota(jnp.int32, (BLK, 2 * BLK), 1)
    diff = qi + BLK - kj
    valid = (diff >= 0) & (diff < BLK)
    return jnp.stack([jnp.where(valid & (kj >= BLK), 0.0, NEG_INF), jnp.where(valid, 0.0, NEG_INF)]).astype(F32)


def _softmax_with_sink(s, sink):
    m = jnp.maximum(_row_reduce(s, jnp.maximum, jnp.max), sink)
    p = jnp.exp(s - m)
    es = jnp.exp(sink - m)
    inv = 1.0 / (_row_sum(p) + es)
    return p * inv, es * inv


PAIR = 2 * HEAD


def _low_half():
    return lax.broadcasted_iota(jnp.int32, (1, PAIR), 1) < HEAD


def _half_sums(v, low):
    return (jnp.sum(jnp.where(low, v, 0.0), axis=-1, keepdims=True),
            jnp.sum(jnp.where(low, 0.0, v), axis=-1, keepdims=True))


def _pair_mean(v, low):
    e, o = _half_sums(v, low)
    return jnp.where(low, e, o) * (1.0 / HEAD)


def _pair_rms(v, low):
    return lax.rsqrt(_pair_mean(v * v, low) + EPS)


def _one_head_in_both_halves(pair, low):
    swapped = pltpu.roll(pair, HEAD, 1)
    return jnp.where(low, pair, swapped), jnp.where(low, swapped, pair)


def _kv_pairs(p_ref, kvp_ref):
    k = jnp.concatenate([kvp_ref[:, 0:PAIR], p_ref[:, O_K:O_K + PAIR]], axis=0).astype(F32)
    v = jnp.concatenate([kvp_ref[:, PAIR:2 * PAIR], p_ref[:, O_V:O_V + PAIR]], axis=0)
    return k, v


def mixer_fwd(proj, bias, sinks, conv_w, qg, kg, cog, aog, name, residual=None, w_o=None):
    S = proj.shape[0]
    nb = S // BLK
    project = w_o is not None

    def body(sinks_ref, p_ref, kvp_ref, bias_ref, cw_ref, qg_ref, kg_ref, cog_ref, aog_ref, *rest):
        if project:
            x_ref, wo_ref, mix_ref, y_ref, ao_ref, x1_ref, ucar = rest
        else:
            mix_ref, y_ref, ao_ref, ucar = rest
        n = pl.program_id(0)

        @pl.when(n == 0)
        def _():
            ucar[...] = jnp.zeros_like(ucar)

        bg = p_ref[:, 0:CONV_CH].astype(F32)
        u = p_ref[:, CONV_CH:2 * CONV_CH].astype(F32) * p_ref[:, 2 * CONV_CH:3 * CONV_CH].astype(F32)
        row = lax.broadcasted_iota(jnp.int32, (BLK, CONV_CH), 0)
        prev = ucar[...]
        u1 = jnp.where(row == 0, prev[7:8, :], pltpu.roll(u, 1, 0))
        u2 = jnp.where(row == 0, prev[6:7, :], jnp.where(row == 1, prev[7:8, :], pltpu.roll(u, 2, 0)))
        ucar[...] = u[BLK - 8:BLK, :]
        y = cw_ref[0:1, :] * u2 + cw_ref[1:2, :] * u1 + cw_ref[2:3, :] * u
        y_ref[...] = y.astype(BF16)
        co = bg * y
        mix_conv = ((co * _rms(co)) * cog_ref[...]).astype(BF16)
        mix_ref[:, 0:CONV_CH] = mix_conv
        if project:
            x1 = x_ref[...] + _dot(mix_conv, wo_ref[0:CONV_CH, :])

        low = _low_half()
        q_gain = qg_ref[...] * SCALE
        k, v = _kv_pairs(p_ref, kvp_ref)
        kn = ((k * _pair_rms(k, low)) * kg_ref[...]).astype(BF16)
        k_of = _one_head_in_both_halves(kn, low)
        v_of = _one_head_in_both_halves(v, low)
        scores = []
        for pair in range(N_Q // 2):
            q = p_ref[:, O_Q + PAIR * pair:O_Q + PAIR * (pair + 1)].astype(F32)
            qn = ((q * _pair_rms(q, low)) * q_gain).astype(BF16)
            for half in range(2):
                q_one = jnp.where(low if half == 0 else jnp.logical_not(low), qn, jnp.zeros_like(qn))
                scores.append(_dot_nt(q_one, k_of[pair // (GRP // 2)]) + bias_ref[0])
        probs = [_softmax_with_sink(s, sinks_ref[hq])[0].astype(BF16) for hq, s in enumerate(scores)]
        o = [_dot(pn, v_of[hq // GRP]) for hq, pn in enumerate(probs)]
        ao = jnp.concatenate([jnp.where(low, o[2 * pair], o[2 * pair + 1]) for pair in range(N_Q // 2)], axis=1)
        ao_ref[...] = ao.astype(BF16)
        mix_attn = ((ao * _rms(ao)) * aog_ref[...]).astype(BF16)
        mix_ref[:, CONV_CH:] = mix_attn
        if project:
            x1_ref[...] = x1 + _dot(mix_attn, wo_ref[CONV_CH:, :])

    small = lambda shape: pl.BlockSpec(shape, lambda n: (0, 0))
    rows = lambda w: pl.BlockSpec((BLK, w), lambda n: (n, 0))
    return pl.pallas_call(
        body, name=name, grid=(nb,),
        in_specs=[pl.BlockSpec(memory_space=pltpu.SMEM),
                  rows(IN_COLS),
                  pl.BlockSpec((BLK, 2 * N_KV * HEAD), lambda n: (jnp.maximum(n - 1, 0), O_K // (2 * N_KV * HEAD))),
                  pl.BlockSpec((1, BLK, 2 * BLK), lambda n: (jnp.minimum(n, 1), 0, 0)),
                  small((3, CONV_CH)), small((1, PAIR)), small((1, PAIR)), small((1, CONV_CH)), small((1, ATTN_W))]
        + ([rows(D_MODEL), small((D_MODEL, D_MODEL))] if project else []),
        out_specs=[rows(D_MODEL), rows(CONV_CH), rows(ATTN_W)] + ([rows(D_MODEL)] if project else []),
        out_shape=[jax.ShapeDtypeStruct((S, D_MODEL), BF16), jax.ShapeDtypeStruct((S, CONV_CH), BF16),
                   jax.ShapeDtypeStruct((S, ATTN_W), BF16)]
        + ([jax.ShapeDtypeStruct((S, D_MODEL), F32)] if project else []),
        scratch_shapes=[pltpu.VMEM((8, CONV_CH), F32)],
        compiler_params=_cparams(("arbitrary",)),
    )(sinks, proj, proj, bias, conv_w, qg, kg, cog, aog, *([residual, w_o] if project else []))


def matmul_residual(a, w, res, name, dep=None):
    S, K = a.shape
    N = w.shape[1]

    def body(a_ref, w_ref, r_ref, o_ref):
        o_ref[...] = r_ref[...] + _dot(a_ref[...], w_ref[...])

    return _pallas_after(
        dep, body, a, w, res, name=name, grid=(S // TM,),
        in_specs=[pl.BlockSpec((TM, K), lambda i: (i, 0)), pl.BlockSpec((K, N), lambda i: (0, 0)),
                  pl.BlockSpec((TM, N), lambda i: (i, 0))],
        out_specs=pl.BlockSpec((TM, N), lambda i: (i, 0)),
        out_shape=jax.ShapeDtypeStruct((S, N), F32),
        compiler_params=_cparams(("parallel",)),
    )


TF = 1408


def ffn_fwd(x1, g, wg, wu, wd, name, dep=None, tgt=None):
    S = x1.shape[0]
    ni, nj = S // TM, D_FF // TF
    with_loss = tgt is not None

    def body(x_ref, g_ref, wg_ref, wu_ref, wd_ref, *rest):
        if with_loss:
            t_ref, h2_ref, a_ref, b_ref, f_ref, o_ref, l_ref, acc, sq = rest
        else:
            h2_ref, a_ref, b_ref, f_ref, o_ref, acc = rest
        i, j = pl.program_id(0), pl.program_id(1)

        @pl.when(j == 0)
        def _():
            xf = x_ref[...]
            h2_ref[...] = ((xf * _rms(xf)) * g_ref[...]).astype(BF16)
            acc[...] = xf

        h2 = h2_ref[...]
        a = _dot_nt(h2, wg_ref[...])
        b = _dot_nt(h2, wu_ref[...])
        a_ref[...] = a.astype(BF16)
        b_ref[...] = b.astype(BF16)
        f = ((a * _sigmoid(a)) * b).astype(BF16)
        f_ref[...] = f
        acc[...] += _dot(f, wd_ref[...])

        @pl.when(j == nj - 1)
        def _():
            if not with_loss:
                o_ref[...] = acc[...]
                return
            e = acc[...] - t_ref[...]
            o_ref[...] = e * (1.0 / D_MODEL)
            col = jnp.sum(e * e, axis=0, keepdims=True)

            @pl.when(i == 0)
            def _():
                sq[...] = col

            @pl.when(i > 0)
            def _():
                sq[...] += col

            @pl.when(i == ni - 1)
            def _():
                l_ref[...] = jnp.full((1, 128), jnp.sum(sq[...]) * (0.5 / D_MODEL), F32)

    row = pl.BlockSpec((TM, D_MODEL), lambda i, j: (i, 0))
    tile = pl.BlockSpec((TM, TF), lambda i, j: (i, j))
    return _pallas_after(
        dep, body, x1, g, wg, wu, wd, *([tgt] if with_loss else []), name=name, grid=(ni, nj),
        in_specs=[row, pl.BlockSpec((1, D_MODEL), lambda i, j: (0, 0))]
        + [pl.BlockSpec((TF, D_MODEL), lambda i, j: (j, 0))] * 3 + ([row] if with_loss else []),
        out_specs=[row, tile, tile, tile, row] + ([pl.BlockSpec((1, 128), lambda i, j: (0, 0))] if with_loss else []),
        out_shape=[jax.ShapeDtypeStruct((S, D_MODEL), BF16)] + [jax.ShapeDtypeStruct((S, D_FF), BF16)] * 3
        + [jax.ShapeDtypeStruct((S, D_MODEL), F32)]
        + ([jax.ShapeDtypeStruct((1, 128), F32)] if with_loss else []),
        scratch_shapes=[pltpu.VMEM((TM, D_MODEL), F32)] + ([pltpu.VMEM((1, D_MODEL), F32)] if with_loss else []),
        compiler_params=_cparams(("arbitrary", "arbitrary") if with_loss else ("parallel", "arbitrary")),
    )


def ffn_bwd(dx2, a, b, x1, g, wg, wu, wd, name, dep=None):
    S = dx2.shape[0]
    nj = D_FF // TF

    def body(dx_ref, a_ref, b_ref, x_ref, g_ref, wg_ref, wu_ref, wd_ref, da_ref, db_ref, dx1_ref, dg_ref, dxb, acc):
        i, j = pl.program_id(0), pl.program_id(1)

        @pl.when((i == 0) & (j == 0))
        def _():
            dg_ref[...] = jnp.zeros_like(dg_ref)

        @pl.when(j == 0)
        def _():
            dxb[...] = dx_ref[...].astype(BF16)
            acc[...] = jnp.zeros_like(acc)

        df = _dot_nt(dxb[...], wd_ref[...])
        av = a_ref[...].astype(F32)
        bv = b_ref[...].astype(F32)
        sg = _sigmoid(av)
        da =((df * bv) * (sg * (1.0 + av * (1.0 - sg)))).astype(BF16)
        db = (df * (av * sg)).astype(BF16)
        da_ref[...] = da
        db_ref[...] = db
        acc[...] += _dot(da, wg_ref[...]) + _dot(db, wu_ref[...])

        @pl.when(j == nj - 1)
        def _():
            xf = x_ref[...]
            r = _rms(xf)
            xh = xf * r
            dh = acc[...]
            dg_ref[...] += jnp.sum(dh * xh, axis=0, keepdims=True)
            dx1_ref[...] = dx_ref[...] + _rms_bwd(dh, xh, r, g_ref[...])

    return _pallas_after(
        dep, body, dx2, a, b, x1, g, wg, wu, wd, name=name, grid=(S // TM, nj),
        in_specs=[pl.BlockSpec((TM, D_MODEL), lambda i, j: (i, 0)),
                  pl.BlockSpec((TM, TF), lambda i, j: (i, j)), pl.BlockSpec((TM, TF), lambda i, j: (i, j)),
                  pl.BlockSpec((TM, D_MODEL), lambda i, j: (i, 0)), pl.BlockSpec((1, D_MODEL), lambda i, j: (0, 0))]
        + [pl.BlockSpec((TF, D_MODEL), lambda i, j: (j, 0))] * 3,
        out_specs=[pl.BlockSpec((TM, TF), lambda i, j: (i, j)), pl.BlockSpec((TM, TF), lambda i, j: (i, j)),
                   pl.BlockSpec((TM, D_MODEL), lambda i, j: (i, 0)), pl.BlockSpec((1, D_MODEL), lambda i, j: (0, 0))],
        out_shape=[jax.ShapeDtypeStruct((S, D_FF), BF16), jax.ShapeDtypeStruct((S, D_FF), BF16),
                   jax.ShapeDtypeStruct((S, D_MODEL), F32), jax.ShapeDtypeStruct((1, D_MODEL), F32)],
        scratch_shapes=[pltpu.VMEM((TM, D_MODEL), BF16), pltpu.VMEM((TM, D_MODEL), F32)],
        compiler_params=_cparams(("arbitrary", "arbitrary")),
    )


def grad_weight(lhs, rhs, tm, name, dep=None):
    S, N = rhs.shape
    M = lhs[0].shape[1]
    tk = 1024
    nk = S // tk
    n_l = n_o = len(lhs)

    def body(*refs):
        l_refs, r_ref = refs[:n_l], refs[n_l]
        o_refs, accs = refs[n_l + 1:n_l + 1 + n_o], refs[n_l + 1 + n_o:]
        k = pl.program_id(1)
        rv = r_ref[...].astype(BF16)
        for l_ref, o_ref, acc in zip(l_refs, o_refs, accs):
            part = _dot_tn(l_ref[...], rv)

            @pl.when(k == 0)
            def _():
                acc[...] = part

            @pl.when(k > 0)
            def _():
                acc[...] += part

            @pl.when(k == nk - 1)
            def _():
                o_ref[...] = acc[...].astype(BF16)

    return _pallas_after(
        dep, body, *lhs, rhs, name=name, grid=(M // tm, nk),
        in_specs=[pl.BlockSpec((tk, tm), lambda i, k: (k, i)) for _ in lhs] + [pl.BlockSpec((tk, N), lambda i, k: (k, 0))],
        out_specs=[pl.BlockSpec((tm, N), lambda i, k: (i, 0))] * n_o,
        out_shape=[jax.ShapeDtypeStruct((M, N), BF16)] * n_o,
        scratch_shapes=[pltpu.VMEM((tm, N), F32)] * n_o,
        compiler_params=_cparams(("parallel", "arbitrary")),
    )


def proj_bwd(dproj, w, x, g, dres, name, dep=None):
    S, N = dproj.shape

    def body(dp_ref, w_ref, x_ref, g_ref, dr_ref, dx_ref, dg_ref):
        i = pl.program_id(0)

        @pl.when(i == 0)
        def _():
            dg_ref[...] = jnp.zeros_like(dg_ref)

        dh = _dot(dp_ref[...], w_ref[...])
        xf = x_ref[...]
        r = _rms(xf)
        xh = xf * r
        dg_ref[...] += jnp.sum(dh * xh, axis=0, keepdims=True)
        dx_ref[...] = dr_ref[...] + _rms_bwd(dh, xh, r, g_ref[...])

    return _pallas_after(
        dep, body, dproj, w, x, g, dres, name=name, grid=(S // TM,),
        in_specs=[pl.BlockSpec((TM, N), lambda i: (i, 0)), pl.BlockSpec((N, D_MODEL), lambda i: (0, 0)),
                  pl.BlockSpec((TM, D_MODEL), lambda i: (i, 0)), pl.BlockSpec((1, D_MODEL), lambda i: (0, 0)),
                  pl.BlockSpec((TM, D_MODEL), lambda i: (i, 0))],
        out_specs=[pl.BlockSpec((TM, D_MODEL), lambda i: (i, 0)), pl.BlockSpec((1, D_MODEL), lambda i: (0, 0))],
        out_shape=[jax.ShapeDtypeStruct((S, D_MODEL), F32), jax.ShapeDtypeStruct((1, D_MODEL), F32)],
        compiler_params=_cparams(("arbitrary",)),
    )


def mixer_bwd(proj, bias, y, ao, dx1, w_o, sinks, conv_w, qg, kg, cog, aog, name, dep=None):
    S = proj.shape[0]
    nb = S // BLK
    KV_W = 2 * N_KV * HEAD

    def body(sinks_ref, p_ref, kvp_ref, bias_ref, y_ref, ao_ref, dx_first_ref, dx_next_ref, wo_ref, cw_ref, qg_ref,
             kg_ref, cog_ref, aog_ref, dp_ref, dcw_ref, dqg_ref, dkg_ref, dsk_ref, dcog_ref, daog_ref,
             dycar, kcar, vcar, dmix_scr):
        step = pl.program_id(0)

        @pl.when(step == 0)
        def _():
            dmix_scr[...] = _dot_nt(dx_first_ref[...].astype(BF16), wo_ref[...])
            dycar[...] = jnp.zeros_like(dycar)
            kcar[...] = jnp.zeros_like(kcar)
            vcar[...] = jnp.zeros_like(vcar)
            dcw_ref[...] = jnp.zeros_like(dcw_ref)
            dqg_ref[...] = jnp.zeros_like(dqg_ref)
            dkg_ref[...] = jnp.zeros_like(dkg_ref)
            dsk_ref[...] = jnp.zeros_like(dsk_ref)
            dcog_ref[...] = jnp.zeros_like(dcog_ref)
            daog_ref[...] = jnp.zeros_like(daog_ref)

        dma = dmix_scr[:, CONV_CH:]

        aov = ao_ref[...].astype(F32)
        ra = _rms(aov)
        ah = aov * ra
        daog_ref[...] += jnp.sum(dma * ah, axis=0, keepdims=True)
        dao = _rms_bwd(dma, ah, ra, aog_ref[...])

        low = _low_half()
        high = jnp.logical_not(low)
        q_gain = qg_ref[...] * SCALE
        k, v = _kv_pairs(p_ref, kvp_ref)
        rk = _pair_rms(k, low)
        kh = k * rk
        k_of = _one_head_in_both_halves((kh * kg_ref[...]).astype(BF16), low)
        v_of = _one_head_in_both_halves(v, low)
        lane8 = lax.broadcasted_iota(jnp.int32, (1, N_Q), 1)
        dsk = jnp.zeros((1, N_Q), F32)
        dqg = jnp.zeros((1, PAIR), F32)
        dv_t = [jnp.zeros((PAIR, 2 * BLK), F32) for _ in range(N_KV)]
        dkn_t = [jnp.zeros((PAIR, 2 * BLK), F32) for _ in range(N_KV)]
        rq, qh, q_one, do_one, delta = [], [], [], [], []
        for pair in range(N_Q // 2):
            cols = slice(PAIR * pair, PAIR * (pair + 1))
            q = p_ref[:, O_Q + PAIR * pair:O_Q + PAIR * (pair + 1)].astype(F32)
            rq.append(_pair_rms(q, low))
            qh.append(q * rq[pair])
            qn = (qh[pair] * q_gain).astype(BF16)
            do = dao[:, cols]
            do_b = do.astype(BF16)
            delta += _half_sums(do * aov[:, cols], low)
            for mine in (low, high):
                q_one.append(jnp.where(mine, qn, jnp.zeros_like(qn)))
                do_one.append(jnp.where(mine, do_b, jnp.zeros_like(do_b)))
        heads = range(N_Q)
        scores = [_dot_nt(q_one[hq], k_of[hq // GRP]) + bias_ref[0] for hq in heads]
        dprobs = [_dot_nt(do_one[hq], v_of[hq // GRP]) for hq in heads]
        probs = [_softmax_with_sink(scores[hq], sinks_ref[hq]) for hq in heads]
        ds = [(probs[hq][0] * (dprobs[hq] - delta[hq])).astype(BF16) for hq in heads]
        for hq in heads:
            dsk = dsk - jnp.where(lane8 == hq, jnp.sum(probs[hq][1] * delta[hq]), 0.0)
            dv_t[hq // GRP] = dv_t[hq // GRP] + _dot_tn(do_one[hq], probs[hq][0].astype(BF16))
            dkn_t[hq // GRP] = dkn_t[hq // GRP] + _dot_tn(q_one[hq], ds[hq])
        dqn_of = [_dot(ds[hq], k_of[hq // GRP]) for hq in heads]
        for pair in range(N_Q // 2):
            dqn = jnp.where(low, dqn_of[2 * pair], dqn_of[2 * pair + 1])
            dqg = dqg + jnp.sum(dqn * qh[pair], axis=0, keepdims=True)
            dqh = dqn * q_gain
            dp_ref[:, O_Q + PAIR * pair:O_Q + PAIR * (pair + 1)] = (
                rq[pair] * (dqh - qh[pair] * _pair_mean(dqh * qh[pair], low))).astype(BF16)

        def untranspose(parts):
            return jnp.concatenate([t[:HEAD] + t[HEAD:] for t in parts], axis=0).T

        dv = untranspose(dv_t)
        dkn = untranspose(dkn_t)
        dkg_ref[...] += jnp.sum(dkn * kh, axis=0, keepdims=True)
        dkh = dkn * kg_ref[...]
        dk = rk * (dkh - kh * _pair_mean(dkh * kh, low))
        dp_ref[:, O_K:O_V] = (kcar[...] + dk[BLK:, :]).astype(BF16)
        dp_ref[:, O_V:] = (vcar[...] + dv[BLK:, :]).astype(BF16)
        kcar[...] = dk[:BLK, :]
        vcar[...] = dv[:BLK, :]
        dsk_ref[...] += dsk
        dqg_ref[...] += dqg * SCALE

        bg = p_ref[:, 0:CONV_CH].astype(F32)
        cg = p_ref[:, CONV_CH:2 * CONV_CH].astype(F32)
        hc = p_ref[:, 2 * CONV_CH:3 * CONV_CH].astype(F32)
        yv = y_ref[...].astype(F32)
        dmc = dmix_scr[:, 0:CONV_CH]
        co = bg * yv
        rc = _rms(co)
        ch = co * rc
        dcog_ref[...] += jnp.sum(dmc * ch, axis=0, keepdims=True)
        dco = _rms_bwd(dmc, ch, rc, cog_ref[...])
        dp_ref[:, 0:CONV_CH] = (dco * yv).astype(BF16)
        dy = dco * bg
        row = lax.broadcasted_iota(jnp.int32, (BLK, CONV_CH), 0)
        nxt = dycar[...]
        dy1 = jnp.where(row == BLK - 1, nxt[0:1, :], pltpu.roll(dy, BLK - 1, 0))
        dy2 = jnp.where(row == BLK - 2, nxt[0:1, :], jnp.where(row == BLK - 1, nxt[1:2, :], pltpu.roll(dy, BLK - 2, 0)))
        dycar[...] = dy[0:8, :]
        du = cw_ref[2:3, :] * dy + cw_ref[1:2, :] * dy1 + cw_ref[0:1, :] * dy2
        dp_ref[:, CONV_CH:2 * CONV_CH] = (du * hc).astype(BF16)
        dp_ref[:, 2 * CONV_CH:3 * CONV_CH] = (du * cg).astype(BF16)
        u = cg * hc
        dcw_ref[0:1, :] += jnp.sum(dy2 * u, axis=0, keepdims=True)
        dcw_ref[1:2, :] += jnp.sum(dy1 * u, axis=0, keepdims=True)
        dcw_ref[2:3, :] += jnp.sum(dy * u, axis=0, keepdims=True)

        dmix_scr[...] = _dot_nt(dx_next_ref[...].astype(BF16), wo_ref[...])

    small = lambda shape: pl.BlockSpec(shape, lambda s: (0, 0))
    blk = lambda w: pl.BlockSpec((BLK, w), lambda s: (nb - 1 - s, 0))
    return _pallas_after(
        dep, body, sinks, proj, proj, bias, y, ao, dx1, dx1, w_o, conv_w, qg, kg, cog, aog, name=name, grid=(nb,),
        in_specs=[pl.BlockSpec(memory_space=pltpu.SMEM),
                  blk(IN_COLS),
                  pl.BlockSpec((BLK, KV_W), lambda s: (jnp.maximum(nb - 2 - s, 0), O_K // KV_W)),
                  pl.BlockSpec((1, BLK, 2 * BLK), lambda s: (jnp.minimum(nb - 1 - s, 1), 0, 0)),
                  blk(CONV_CH), blk(ATTN_W),
                  pl.BlockSpec((BLK, D_MODEL), lambda s: (nb - 1, 0)),
                  pl.BlockSpec((BLK, D_MODEL), lambda s: (jnp.maximum(nb - 2 - s, 0), 0)),
                  small((D_MODEL, D_MODEL)),
                  small((3, CONV_CH)), small((1, PAIR)), small((1, PAIR)), small((1, CONV_CH)), small((1, ATTN_W))],
        out_specs=[blk(IN_COLS), small((3, CONV_CH)), small((1, PAIR)), small((1, PAIR)), small((1, N_Q)),
                   small((1, CONV_CH)), small((1, ATTN_W))],
        out_shape=[jax.ShapeDtypeStruct((S, IN_COLS), BF16), jax.ShapeDtypeStruct((3, CONV_CH), F32),
                   jax.ShapeDtypeStruct((1, PAIR), F32), jax.ShapeDtypeStruct((1, PAIR), F32),
                   jax.ShapeDtypeStruct((1, N_Q), F32), jax.ShapeDtypeStruct((1, CONV_CH), F32),
                   jax.ShapeDtypeStruct((1, ATTN_W), F32)],
        scratch_shapes=[pltpu.VMEM((8, CONV_CH), F32), pltpu.VMEM((BLK, PAIR), F32), pltpu.VMEM((BLK, PAIR), F32),
                        pltpu.VMEM((BLK, D_MODEL), F32)],
        compiler_params=_cparams(("arbitrary",)),
    )


OTHER_CHIPS = ((1, 0), (0, 1), (1, 1))


def _position():
    return lax.axis_index("x"), lax.axis_index("y"), lax.axis_index("c")


def all_gather(shards, name):
    n = len(shards)

    def body(*refs):
        x_refs, out_refs = refs[:n], refs[n:2 * n]
        send_sems, recv_sems, local_sems = refs[2 * n:]
        x, y, c = _position()
        me, sibling = (x, y, c), (x, y, 1 - c)
        chips = [(x ^ mx, y ^ my) for mx, my in OTHER_CHIPS]

        def slab(a, px, py, pc):
            return out_refs[a].at[4 * px + 2 * py + pc]

        def copy(a, k, block, to, src=None):
            return pltpu.make_async_remote_copy(
                src_ref=slab(a, *block) if src is None else src, dst_ref=slab(a, *block),
                send_sem=send_sems.at[7 * a + k], recv_sem=recv_sems.at[7 * a + k], device_id=to, device_id_type=MESH_T)

        mine = [pltpu.make_async_copy(x_refs[a], slab(a, *me), local_sems.at[a]) for a in range(n)]
        for cp in mine:
            cp.start()
        first = []
        for a in range(n):
            first.append(copy(a, 0, me, sibling, src=x_refs[a]))
            first += [copy(a, 1 + j, me, (*chip, c), src=x_refs[a]) for j, chip in enumerate(chips)]
        for cp in first:
            cp.start()
        passed = []
        for a in range(n):
            for j, chip in enumerate(chips):
                copy(a, 1 + j, (*chip, c), me).wait_recv()
                passed.append(copy(a, 4 + j, (*chip, c), sibling))
                passed[-1].start()
        for a in range(n):
            copy(a, 0, sibling, me).wait_recv()
            for j, chip in enumerate(chips):
                copy(a, 4 + j, (*chip, 1 - c), me).wait_recv()
        for cp in first + passed:
            cp.wait_send()
        for cp in mine:
            cp.wait()

    return pl.pallas_call(
        body, name=name, in_specs=[ANY] * n, out_specs=[ANY] * n,
        out_shape=[jax.ShapeDtypeStruct((N_DEV, *s.shape), s.dtype) for s in shards],
        scratch_shapes=[pltpu.SemaphoreType.DMA((7 * n,)), pltpu.SemaphoreType.DMA((7 * n,)),
                        pltpu.SemaphoreType.DMA((n,))],
    )(*shards)


class SplitCopy:
    def __init__(self, name, arrays, n_copies, plan, after=None):
        n = len(arrays)
        self.name, self.n, self.n_copies, self.plan = name, n, n_copies, plan
        extra = [] if after is None else [after]

        def body(*refs):
            in_refs = refs[:n]
            send_sems, recv_sems = refs[n + len(extra)], refs[n + len(extra) + 1]
            token = refs[2 * n + len(extra) + 2]
            for k, (src, dst, to) in enumerate(plan(_position(), in_refs)):
                pltpu.make_async_remote_copy(src_ref=src, dst_ref=dst, send_sem=send_sems.at[k],
                                             recv_sem=recv_sems.at[k], device_id=to, device_id_type=MESH_T).start()
            token[...] = jnp.zeros_like(token)

        outs = pl.pallas_call(
            body, name=name + "_start",
            out_shape=(pltpu.SemaphoreType.DMA((n_copies,)), pltpu.SemaphoreType.DMA((n_copies,)),
                       *[pltpu.HBM(a.shape, a.dtype) for a in arrays], jax.ShapeDtypeStruct((8, 128), F32)),
            in_specs=[HBM] * n + [ANY] * len(extra),
            out_specs=(SEM, SEM, *[HBM] * n, pl.BlockSpec(memory_space=pltpu.VMEM)),
            input_output_aliases={i: 2 + i for i in range(n)},
            compiler_params=pltpu.CompilerParams(has_side_effects=pltpu.SideEffectType.DATAFLOW_SIDE_EFFECTING),
        )(*[pltpu.with_memory_space_constraint(a, pltpu.HBM) for a in arrays], *extra)
        self.send_sems, self.recv_sems = outs[0], outs[1]
        self.arrays, self.token = list(outs[2:2 + n]), outs[2 + n]

    def wait(self, after):
        n, plan = self.n, self.plan

        def body(*refs):
            in_refs, send_sems, recv_sems = refs[:n], refs[n], refs[n + 1]
            for k, (src, dst, to) in enumerate(plan(_position(), in_refs)):
                cp = pltpu.make_async_remote_copy(src_ref=src, dst_ref=dst, send_sem=send_sems.at[k],
                                                  recv_sem=recv_sems.at[k], device_id=to, device_id_type=MESH_T)
                cp.wait_send()
                cp.wait_recv()

        outs = pl.pallas_call(
            body, name=self.name + "_wait",
            out_shape=tuple(pltpu.HBM(a.shape, a.dtype) for a in self.arrays),
            in_specs=[HBM] * n + [SEM, SEM, ANY], out_specs=tuple([HBM] * n),
            input_output_aliases={i: i for i in range(n)},
            compiler_params=pltpu.CompilerParams(has_side_effects=pltpu.SideEffectType.DATAFLOW_SIDE_EFFECTING),
        )(*self.arrays, self.send_sems, self.recv_sems, after)
        return list(outs)


def gather_start(shards, me, name, after=None):
    n = len(shards)
    lands = [lax.dynamic_update_slice(lax.empty((N_DEV, *s.shape), s.dtype), s[None], (me, 0, 0)) for s in shards]

    def plan(pos, refs):
        x, y, c = pos
        return [(refs[a], refs[n + a].at[4 * x + 2 * y + c], (x ^ mx, y ^ my, c))
                for a in range(n) for mx, my in OTHER_CHIPS]

    return SplitCopy(name, list(shards) + lands, 3 * n, plan, after=after)


def sibling_start(lands, name):
    n = len(lands)

    def plan(pos, refs):
        x, y, c = pos
        return [(refs[a].at[2 * q + c], refs[a].at[2 * q + c], (x, y, 1 - c)) for a in range(n) for q in range(4)]

    return SplitCopy(name, list(lands), 4 * n, plan)


def scatter_start(slabs, name):
    n = len(slabs)
    lands = [lax.empty((N_DEV - 1, *g.shape[1:]), g.dtype) for g in slabs]

    def plan(pos, refs):
        x, y, c = pos
        copies = []
        for a in range(n):
            for r in range(1, N_DEV):
                px, py, pc = x ^ ((r >> 2) & 1), y ^ ((r >> 1) & 1), c ^ (r & 1)
                copies.append((refs[a].at[4 * px + 2 * py + pc], refs[n + a].at[r - 1], (px, py, pc)))
        return copies

    return SplitCopy(name, list(slabs) + lands, (N_DEV - 1) * n, plan)


def all_reduce_small(v, name, dep=None):
    R, W = v.shape

    def body(v_ref, o_ref, recv, send_sems, recv_sems):
        x, y, c = _position()
        me = 4 * x + 2 * y + c
        copies = []
        for r in range(1, N_DEV):
            to = (x ^ ((r >> 2) & 1), y ^ ((r >> 1) & 1), c ^ (r & 1))
            copies.append(pltpu.make_async_remote_copy(
                src_ref=v_ref, dst_ref=recv.at[me], send_sem=send_sems.at[r - 1], recv_sem=recv_sems.at[r - 1],
                device_id=to, device_id_type=MESH_T))
        for cp in copies:
            cp.start()
        recv[pl.ds(me, 1)] = v_ref[...][None]
        for cp in copies:
            cp.wait()
        acc = recv[0]
        for s in range(1, N_DEV):
            acc = acc + recv[s]
        o_ref[...] = acc

    return _pallas_after(
        dep, body, v, name=name,
        in_specs=[pl.BlockSpec(memory_space=pltpu.VMEM)], out_specs=pl.BlockSpec(memory_space=pltpu.VMEM),
        out_shape=jax.ShapeDtypeStruct((R, W), F32),
        scratch_shapes=[pltpu.VMEM((N_DEV, R, W), F32), pltpu.SemaphoreType.DMA((N_DEV - 1,)),
                        pltpu.SemaphoreType.DMA((N_DEV - 1,))],
    )


def _row_tile(rows):
    if rows <= 512:
        return rows
    return max(t for t in range(8, 513, 8) if rows % t == 0)


def _adamw_update(w, g, m, v):
    mn = ADAM_B1 * m + (1.0 - ADAM_B1) * g
    vn = ADAM_B2 * v + (1.0 - ADAM_B2) * (g * g)
    m_hat = mn / (1.0 - ADAM_B1 ** ADAM_STEP)
    v_hat = vn / (1.0 - ADAM_B2 ** ADAM_STEP)
    return -ADAM_LR * (m_hat / (jnp.sqrt(v_hat) + ADAM_EPS) + ADAM_WD * w), mn, vn


def adamw(w, g, m, v, name):
    R, W = w.shape
    tr = _row_tile(R)

    def body(w_ref, g_ref, m_ref, v_ref, d_ref, mo_ref, vo_ref):
        d_ref[...], mo_ref[...], vo_ref[...] = _adamw_update(w_ref[...], g_ref[...], m_ref[...], v_ref[...])

    spec = pl.BlockSpec((tr, W), lambda i: (i, 0))
    return pl.pallas_call(
        body, name=name, grid=(R // tr,), in_specs=[spec] * 4, out_specs=[spec] * 3,
        out_shape=[jax.ShapeDtypeStruct((R, W), F32)] * 3,
        compiler_params=_cparams(("parallel",)),
    )(w, g, m, v)


def reduce_adamw(slabs, land, w, m, v, layer, me_arr, name, others=None):
    _, R, W = slabs.shape
    tr = _row_tile(R)
    n_other = 0 if others is None else 4

    def body(me_ref, s_ref, l_ref, w_ref, m_ref, v_ref, *rest):
        g_ref, d_ref, mo_ref, vo_ref = rest[n_other:]
        g = s_ref[0].astype(F32)
        for r in range(N_DEV - 1):
            g = g + l_ref[r].astype(F32)
        g_ref[0] = g
        d_ref[0], mo_ref[0], vo_ref[0] = _adamw_update(w_ref[0], g, m_ref[0], v_ref[0])

    spec = pl.BlockSpec((1, tr, W), lambda i, me: (layer, i, 0))
    return pl.pallas_call(
        body, name=name,
        grid_spec=pltpu.PrefetchScalarGridSpec(
            num_scalar_prefetch=1, grid=(R // tr,),
            in_specs=[pl.BlockSpec((1, tr, W), lambda i, me: (me[0], i, 0)),
                      pl.BlockSpec((N_DEV - 1, tr, W), lambda i, me: (0, i, 0)), spec, spec, spec] + [ANY] * n_other,
            out_specs=[spec] * 4),
        out_shape=[jax.ShapeDtypeStruct((DEPTH, R, W), F32)] * 4,
        input_output_aliases={6 + i: i for i in range(n_other)},
        compiler_params=_cparams(("parallel",)),
    )(me_arr, slabs, land, w, m, v, *([] if others is None else others))


SMALL_NAMES = ("norm1_g", "q_norm_g", "k_norm_g", "sinks", "conv_out_g", "attn_out_g", "norm2_g", "conv_w")
SMALL_SIZES = (D_MODEL, HEAD, HEAD, N_Q, CONV_CH, ATTN_W, D_MODEL, 3 * CONV_CH)
SMALL_ROWS = 80


def kernel(x, norm1_g, w_in, conv_w, q_norm_g, k_norm_g, sinks, conv_out_g, attn_out_g, w_o, norm2_g, w_gate, w_up, w_down, loss_target, m_norm1_g, m_w_in, m_conv_w, m_q_norm_g, m_k_norm_g, m_sinks, m_conv_out_g, m_attn_out_g, m_w_o, m_norm2_g, m_w_gate, m_w_up, m_w_down, v_norm1_g, v_w_in, v_conv_w, v_q_norm_g, v_k_norm_g, v_sinks, v_conv_out_g, v_attn_out_g, v_w_o, v_norm2_g, v_w_gate, v_w_up, v_w_down):
    xi, yi, ci = _position()
    me = 4 * xi + 2 * yi + ci
    me_arr = jnp.reshape(me, (1,)).astype(jnp.int32)
    xs, tgt = x[0], loss_target[0]
    bf = lambda a: a.astype(BF16)
    bias = band_bias()

    t = lambda a: jnp.swapaxes(a, 1, 2)
    shard = dict(w_in=(t(w_in), t(m_w_in), t(v_w_in)), w_o=(w_o, m_w_o, v_w_o),
                 w_gate=(t(w_gate), t(m_w_gate), t(v_w_gate)), w_up=(t(w_up), t(m_w_up), t(v_w_up)),
                 w_down=(w_down, m_w_down, v_w_down))
    wb = {n: bf(shard[n][0]) for n in shard}

    g_in0, g_o0, g_conv = all_gather([wb["w_in"][0], wb["w_o"][0], conv_w.reshape(DEPTH * 3, HEAD)], "gather_first")
    ag_ffn0 = gather_start([wb["w_gate"][0], wb["w_up"][0], wb["w_down"][0]], me, "gather_ffn0", after=g_in0)
    ag_mix1 = gather_start([wb["w_in"][1], wb["w_o"][1]], me, "gather_mix1", after=ag_ffn0.token)
    ag_ffn1 = gather_start([wb["w_gate"][1], wb["w_up"][1], wb["w_down"][1]], me, "gather_ffn1", after=ag_mix1.token)
    conv_full = g_conv.reshape(N_DEV, DEPTH, 3, HEAD).transpose(1, 2, 0, 3).reshape(DEPTH, 3, CONV_CH)
    pair_gain = lambda g: jnp.tile(g[None], (1, 2))
    small = [dict(norm1_g=norm1_g[l][None], conv_w=conv_full[l], q_norm_g=pair_gain(q_norm_g[l]),
                  k_norm_g=pair_gain(k_norm_g[l]), sinks=sinks[l], conv_out_g=conv_out_g[l][None],
                  attn_out_g=attn_out_g[l][None], norm2_g=norm2_g[l][None]) for l in range(DEPTH)]
    whole = lambda g: g.reshape(-1, D_MODEL)
    weights = [dict(w_in=whole(g_in0), w_o=whole(g_o0)), {}]

    def ffn_weights(g_gate, g_up, g_down):
        return dict(w_gate=whole(g_gate), w_up=whole(g_up), w_down=whole(g_down))

    saved = []
    xl = xs
    for l in range(DEPTH):
        sp, wl = small[l], weights[l]
        h, proj = norm_proj(xl, sp["norm1_g"], wl["w_in"], f"norm_proj{l}",
                            dep=ag_ffn1.token if l == 0 else pass_ffn1.token)
        mixer_args = (proj, bias, sp["sinks"], sp["conv_w"], sp["q_norm_g"], sp["k_norm_g"], sp["conv_out_g"],
                      sp["attn_out_g"], f"mixer_fwd{l}")
        if l == 0:
            mix, y, ao = mixer_fwd(*mixer_args)
            pass_ffn0 = sibling_start(ag_ffn0.wait(mix)[3:], "pass_ffn0")
            x1 = matmul_residual(mix, wl["w_o"], xl, "out_proj0", dep=pass_ffn0.token)
            wl.update(ffn_weights(*pass_ffn0.wait(x1)))
            pass_mix1 = sibling_start(ag_mix1.wait(x1)[2:], "pass_mix1")
            h2, a, b, f, x2 = ffn_fwd(x1, sp["norm2_g"], wl["w_gate"], wl["w_up"], wl["w_down"], "ffn_fwd0",
                                      dep=pass_mix1.token)
            g_in, g_o = pass_mix1.wait(x2)
            weights[1] = dict(w_in=whole(g_in), w_o=whole(g_o))
            pass_ffn1 = sibling_start(ag_ffn1.wait(x2)[3:], "pass_ffn1")
        else:
            mix, y, ao, x1 = mixer_fwd(*mixer_args, residual=xl, w_o=wl["w_o"])
            wl.update(ffn_weights(*pass_ffn1.wait(x1)))
            h2, a, b, f, dx, loss_row = ffn_fwd(x1, sp["norm2_g"], wl["w_gate"], wl["w_up"], wl["w_down"], "ffn_fwd1",
                                                tgt=tgt)
            x2 = None
        saved.append((xl, h, proj, mix, y, ao, x1, h2, a, b, f))
        xl = x2

    stepped = {n: None for n in shard}
    slabs = lambda d: d.reshape(N_DEV, -1, D_MODEL)
    gsmall = [None] * DEPTH

    def finish(sc, names, after, l):
        arrays = sc.wait(after)
        k = len(names)
        for i, n in enumerate(names):
            w, m, v = shard[n]
            stepped[n] = reduce_adamw(arrays[i], arrays[k + i], w, m, v, l, me_arr, f"reduce_adamw_{n}{l}",
                                      others=stepped[n])

    for l in reversed(range(DEPTH)):
        sp, wl = small[l], weights[l]
        x0, h, proj, mix, y, ao, x1, h2, a, b, f = saved[l]
        (d_wd,) = grad_weight((f,), dx, TF, f"grad_w_down{l}")
        sc_down = scatter_start([slabs(d_wd)], f"scatter_w_down{l}")
        da, db, dx1, d_g2 = ffn_bwd(dx, a, b, x1, sp["norm2_g"], wl["w_gate"], wl["w_up"], wl["w_down"],
                                    f"ffn_bwd{l}", dep=sc_down.token)
        d_wg, d_wu = grad_weight((da, db), h2, TF, f"grad_w_gate_up{l}")
        (d_wo,) = grad_weight((mix,), dx1, D_MODEL, f"grad_w_o{l}")
        sc_rest = scatter_start([slabs(d_wg), slabs(d_wu), slabs(d_wo)], f"scatter_w_gate_up_o{l}")
        dproj, d_cw, d_qg, d_kg, d_sk, d_cog, d_aog = mixer_bwd(
            proj, bias, y, ao, dx1, wl["w_o"], sp["sinks"], sp["conv_w"], sp["q_norm_g"], sp["k_norm_g"],
            sp["conv_out_g"], sp["attn_out_g"], f"mixer_bwd{l}", dep=sc_rest.token)
        finish(sc_down, ["w_down"], dproj, l)
        finish(sc_rest, ["w_gate", "w_up", "w_o"], dproj, l)
        (d_win,) = grad_weight((dproj,), h, IN_COLS // 2, f"grad_w_in{l}")
        sc_in = scatter_start([slabs(d_win)], f"scatter_w_in{l}")
        dx, d_g1 = proj_bwd(dproj, wl["w_in"], x0, sp["norm1_g"], dx1, f"proj_bwd{l}", dep=sc_in.token)
        finish(sc_in, ["w_in"], dx, l)
        both_heads = lambda d: d[:, :HEAD] + d[:, HEAD:]
        gsmall[l] = dict(norm1_g=d_g1, conv_w=d_cw, q_norm_g=both_heads(d_qg), k_norm_g=both_heads(d_kg), sinks=d_sk,
                         conv_out_g=d_cog, attn_out_g=d_aog, norm2_g=d_g2)
    grad_x = dx[None]

    flat = jnp.concatenate([loss_row[0, 0:1]] + [gsmall[l][n].reshape(-1) for l in range(DEPTH) for n in SMALL_NAMES])
    flat = jnp.pad(flat, (0, SMALL_ROWS * 128 - flat.shape[0])).reshape(SMALL_ROWS, 128)
    flat = all_reduce_small(flat, "all_reduce_small_grads").reshape(-1)
    loss = flat[0]
    gs = {n: [] for n in SMALL_NAMES}
    off = 1
    for l in range(DEPTH):
        for n, size in zip(SMALL_NAMES, SMALL_SIZES):
            gs[n].append(flat[off:off + size])
            off += size
    gs = {n: jnp.stack(v) for n, v in gs.items()}
    g_conv = lax.dynamic_slice(gs["conv_w"].reshape(DEPTH, 3, CONV_CH), (0, 0, me * HEAD), (DEPTH, 3, HEAD))

    gs["conv_w"] = g_conv
    params = dict(norm1_g=(norm1_g, m_norm1_g, v_norm1_g), conv_w=(conv_w, m_conv_w, v_conv_w),
                  q_norm_g=(q_norm_g, m_q_norm_g, v_q_norm_g), k_norm_g=(k_norm_g, m_k_norm_g, v_k_norm_g),
                  sinks=(sinks, m_sinks, v_sinks), conv_out_g=(conv_out_g, m_conv_out_g, v_conv_out_g),
                  attn_out_g=(attn_out_g, m_attn_out_g, v_attn_out_g), norm2_g=(norm2_g, m_norm2_g, v_norm2_g))
    names = ("norm1_g", "w_in", "conv_w", "q_norm_g", "k_norm_g", "sinks", "conv_out_g", "attn_out_g", "w_o",
             "norm2_g", "w_gate", "w_up", "w_down")

    out = {}
    for n in names:
        if n in shard:
            out[n] = tuple(t(r) for r in stepped[n]) if n in ("w_in", "w_gate", "w_up") else stepped[n]
        else:
            w, m, v = params[n]
            two_d = (-1, w.shape[-1])
            d, mn, vn = adamw(w.reshape(two_d), gs[n].reshape(two_d), m.reshape(two_d), v.reshape(two_d), f"adamw_{n}")
            out[n] = (gs[n].reshape(w.shape), d.reshape(w.shape), mn.reshape(w.shape), vn.reshape(w.shape))
    return (loss, grad_x, *[out[n][i] for i in range(4) for n in names])
```

```python
import functools

import jax
import jax.numpy as jnp
from jax import lax
from jax.experimental import pallas as pl
from jax.experimental.pallas import tpu as pltpu

F32 = jnp.float32
BF16 = jnp.bfloat16

D_MODEL = 1024
CONV_CH = 512
ATTN_W = 512
N_Q = 8
N_KV = 2
GRP = N_Q // N_KV
HEAD = 64
IN_COLS = 2304
D_FF = 2816
BLK = 128
O_Q = 3 * CONV_CH
O_K = O_Q + ATTN_W
O_V = O_K + N_KV * HEAD
EPS = 1e-6
NEG_INF = -1e30
SCALE = HEAD ** -0.5
N_DEV = 8
DEPTH = 2

ADAM_LR = 0.001
ADAM_B1 = 0.9
ADAM_B2 = 0.999
ADAM_EPS = 1e-08
ADAM_WD = 0.01
ADAM_STEP = 10

VMEM_LIMIT = 56 * 1024 * 1024
TM = 512
MESH_T = pl.DeviceIdType.MESH

ROWS_IN, ROWS_O, ROWS_FF = IN_COLS // N_DEV, D_MODEL // N_DEV, D_FF // N_DEV


ANY = pl.BlockSpec(memory_space=pl.ANY)
HBM = pl.BlockSpec(memory_space=pltpu.HBM)
SEM = pl.BlockSpec(memory_space=pltpu.SEMAPHORE)


def _cparams(sem):
    return pltpu.CompilerParams(dimension_semantics=sem, vmem_limit_bytes=VMEM_LIMIT)


def _pallas_after(dep, body, *args, in_specs, **kw):
    if dep is None:
        return pl.pallas_call(body, in_specs=in_specs, **kw)(*args)

    def after_dep(dep_ref, *refs):
        body(*refs)

    return pl.pallas_call(after_dep, in_specs=[ANY, *in_specs], **kw)(dep, *args)


def _dot(a, b):
    return jnp.dot(a, b, preferred_element_type=F32)


def _dot_nt(a, b):
    return lax.dot_general(a, b, (((1,), (1,)), ((), ())), preferred_element_type=F32)


def _dot_tn(a, b):
    return lax.dot_general(a, b, (((0,), (0,)), ((), ())), preferred_element_type=F32)


LANES = 128


def _row_reduce(v, op, reduce):
    w = v.shape[-1]
    if w > LANES and w % LANES == 0:
        acc = v[:, 0:LANES]
        for c in range(1, w // LANES):
            acc = op(acc, v[:, LANES * c:LANES * (c + 1)])
        v = acc
    return reduce(v, axis=-1, keepdims=True)


def _row_sum(v):
    return _row_reduce(v, jnp.add, jnp.sum)


def _row_mean(v):
    return _row_sum(v) * (1.0 / v.shape[-1])


def _rms(v):
    return lax.rsqrt(_row_mean(v * v) + EPS)


def _sigmoid(v):
    return 1.0 / (1.0 + jnp.exp(-v))


def _rms_bwd(dyv, xh, r, g):
    dxh = dyv * g
    return r * (dxh - xh * _row_mean(dxh * xh))


def norm_proj(x, g, w_t, name, dep=None):
    S, N = x.shape[0], w_t.shape[0]

    def body(x_ref, g_ref, w_ref, h_ref, p_ref):
        xf = x_ref[...]
        h = ((xf * _rms(xf)) * g_ref[...]).astype(BF16)
        h_ref[...] = h
        p_ref[...] = _dot_nt(h, w_ref[...]).astype(BF16)

    return _pallas_after(
        dep, body, x, g, w_t, name=name, grid=(S // TM,),
        in_specs=[pl.BlockSpec((TM, D_MODEL), lambda i: (i, 0)),
                  pl.BlockSpec((1, D_MODEL), lambda i: (0, 0)),
                  pl.BlockSpec((N, D_MODEL), lambda i: (0, 0))],
        out_specs=[pl.BlockSpec((TM, D_MODEL), lambda i: (i, 0)),
                   pl.BlockSpec((TM, N), lambda i: (i, 0))],
        out_shape=[jax.ShapeDtypeStruct((S, D_MODEL), BF16), jax.ShapeDtypeStruct((S, N), BF16)],
        compiler_params=_cparams(("parallel",)),
    )


def band_bias():
    qi = lax.broadcasted_iota(jnp.int32, (BLK, 2 * BLK), 0)
    kj = lax.broadcasted_iota(jnp.int32, (BLK, 2 * BLK), 1)
    diff = qi + BLK - kj
    valid = (diff >= 0) & (diff < BLK)
    return jnp.stack([jnp.where(valid & (kj >= BLK), 0.0, NEG_INF), jnp.where(valid, 0.0, NEG_INF)]).astype(F32)


def _softmax_with_sink(s, sink):
    m = jnp.maximum(_row_reduce(s, jnp.maximum, jnp.max), sink)
    p = jnp.exp(s - m)
    es = jnp.exp(sink - m)
    inv = 1.0 / (_row_sum(p) + es)
    return p * inv, es * inv


PAIR = 2 * HEAD


def _low_half():
    return lax.broadcasted_iota(jnp.int32, (1, PAIR), 1) < HEAD


def _half_sums(v, low):
    return (jnp.sum(jnp.where(low, v, 0.0), axis=-1, keepdims=True),
            jnp.sum(jnp.where(low, 0.0, v), axis=-1, keepdims=True))


def _pair_mean(v, low):
    e, o = _half_sums(v, low)
    return jnp.where(low, e, o) * (1.0 / HEAD)


def _pair_rms(v, low):
    return lax.rsqrt(_pair_mean(v * v, low) + EPS)


def _one_head_in_both_halves(pair, low):
    swapped = pltpu.roll(pair, HEAD, 1)
    return jnp.where(low, pair, swapped), jnp.where(low, swapped, pair)


def _kv_pairs(p_ref, kvp_ref):
    k = jnp.concatenate([kvp_ref[:, 0:PAIR], p_ref[:, O_K:O_K + PAIR]], axis=0).astype(F32)
    v = jnp.concatenate([kvp_ref[:, PAIR:2 * PAIR], p_ref[:, O_V:O_V + PAIR]], axis=0)
    return k, v


def mixer_fwd(proj, bias, sinks, conv_w, qg, kg, cog, aog, name, residual=None, w_o=None):
    S = proj.shape[0]
    nb = S // BLK
    project = w_o is not None

    def body(sinks_ref, p_ref, kvp_ref, bias_ref, cw_ref, qg_ref, kg_ref, cog_ref, aog_ref, *rest):
        if project:
            x_ref, wo_ref, mix_ref, y_ref, ao_ref, x1_ref, ucar = rest
        else:
            mix_ref, y_ref, ao_ref, ucar = rest
        n = pl.program_id(0)

        @pl.when(n == 0)
        def _():
            ucar[...] = jnp.zeros_like(ucar)

        bg = p_ref[:, 0:CONV_CH].astype(F32)
        u = p_ref[:, CONV_CH:2 * CONV_CH].astype(F32) * p_ref[:, 2 * CONV_CH:3 * CONV_CH].astype(F32)
        row = lax.broadcasted_iota(jnp.int32, (BLK, CONV_CH), 0)
        prev = ucar[...]
        u1 = jnp.where(row == 0, prev[7:8, :], pltpu.roll(u, 1, 0))
        u2 = jnp.where(row == 0, prev[6:7, :], jnp.where(row == 1, prev[7:8, :], pltpu.roll(u, 2, 0)))
        ucar[...] = u[BLK - 8:BLK, :]
        y = cw_ref[0:1, :] * u2 + cw_ref[1:2, :] * u1 + cw_ref[2:3, :] * u
        y_ref[...] = y.astype(BF16)
        co = bg * y
        mix_conv = ((co * _rms(co)) * cog_ref[...]).astype(BF16)
        mix_ref[:, 0:CONV_CH] = mix_conv
        if project:
            x1 = x_ref[...] + _dot(mix_conv, wo_ref[0:CONV_CH, :])

        low = _low_half()
        q_gain = qg_ref[...] * SCALE
        k, v = _kv_pairs(p_ref, kvp_ref)
        kn = ((k * _pair_rms(k, low)) * kg_ref[...]).astype(BF16)
        k_of = _one_head_in_both_halves(kn, low)
        v_of = _one_head_in_both_halves(v, low)
        scores = []
        for pair in range(N_Q // 2):
            q = p_ref[:, O_Q + PAIR * pair:O_Q + PAIR * (pair + 1)].astype(F32)
            qn = ((q * _pair_rms(q, low)) * q_gain).astype(BF16)
            for half in range(2):
                q_one = jnp.where(low if half == 0 else jnp.logical_not(low), qn, jnp.zeros_like(qn))
                scores.append(_dot_nt(q_one, k_of[pair // (GRP // 2)]) + bias_ref[0])
        probs = [_softmax_with_sink(s, sinks_ref[hq])[0].astype(BF16) for hq, s in enumerate(scores)]
        o = [_dot(pn, v_of[hq // GRP]) for hq, pn in enumerate(probs)]
        ao = jnp.concatenate([jnp.where(low, o[2 * pair], o[2 * pair + 1]) for pair in range(N_Q // 2)], axis=1)
        ao_ref[...] = ao.astype(BF16)
        mix_attn = ((ao * _rms(ao)) * aog_ref[...]).astype(BF16)
        mix_ref[:, CONV_CH:] = mix_attn
        if project:
            x1_ref[...] = x1 + _dot(mix_attn, wo_ref[CONV_CH:, :])

    small = lambda shape: pl.BlockSpec(shape, lambda n: (0, 0))
    rows = lambda w: pl.BlockSpec((BLK, w), lambda n: (n, 0))
    return pl.pallas_call(
        body, name=name, grid=(nb,),
        in_specs=[pl.BlockSpec(memory_space=pltpu.SMEM),
                  rows(IN_COLS),
                  pl.BlockSpec((BLK, 2 * N_KV * HEAD), lambda n: (jnp.maximum(n - 1, 0), O_K // (2 * N_KV * HEAD))),
                  pl.BlockSpec((1, BLK, 2 * BLK), lambda n: (jnp.minimum(n, 1), 0, 0)),
                  small((3, CONV_CH)), small((1, PAIR)), small((1, PAIR)), small((1, CONV_CH)), small((1, ATTN_W))]
        + ([rows(D_MODEL), small((D_MODEL, D_MODEL))] if project else []),
        out_specs=[rows(D_MODEL), rows(CONV_CH), rows(ATTN_W)] + ([rows(D_MODEL)] if project else []),
        out_shape=[jax.ShapeDtypeStruct((S, D_MODEL), BF16), jax.ShapeDtypeStruct((S, CONV_CH), BF16),
                   jax.ShapeDtypeStruct((S, ATTN_W), BF16)]
        + ([jax.ShapeDtypeStruct((S, D_MODEL), F32)] if project else []),
        scratch_shapes=[pltpu.VMEM((8, CONV_CH), F32)],
        compiler_params=_cparams(("arbitrary",)),
    )(sinks, proj, proj, bias, conv_w, qg, kg, cog, aog, *([residual, w_o] if project else []))


def matmul_residual(a, w, res, name, dep=None):
    S, K = a.shape
    N = w.shape[1]

    def body(a_ref, w_ref, r_ref, o_ref):
        o_ref[...] = r_ref[...] + _dot(a_ref[...], w_ref[...])

    return _pallas_after(
        dep, body, a, w, res, name=name, grid=(S // TM,),
        in_specs=[pl.BlockSpec((TM, K), lambda i: (i, 0)), pl.BlockSpec((K, N), lambda i: (0, 0)),
                  pl.BlockSpec((TM, N), lambda i: (i, 0))],
        out_specs=pl.BlockSpec((TM, N), lambda i: (i, 0)),
        out_shape=jax.ShapeDtypeStruct((S, N), F32),
        compiler_params=_cparams(("parallel",)),
    )


TF = 1408


def ffn_fwd(x1, g, wg, wu, wd, name, dep=None, tgt=None):
    S = x1.shape[0]
    ni, nj = S // TM, D_FF // TF
    with_loss = tgt is not None

    def body(x_ref, g_ref, wg_ref, wu_ref, wd_ref, *rest):
        if with_loss:
            t_ref, h2_ref, a_ref, b_ref, f_ref, o_ref, l_ref, acc, sq = rest
        else:
            h2_ref, a_ref, b_ref, f_ref, o_ref, acc = rest
        i, j = pl.program_id(0), pl.program_id(1)

        @pl.when(j == 0)
        def _():
            xf = x_ref[...]
            h2_ref[...] = ((xf * _rms(xf)) * g_ref[...]).astype(BF16)
            acc[...] = xf

        h2 = h2_ref[...]
        a = _dot_nt(h2, wg_ref[...])
        b = _dot_nt(h2, wu_ref[...])
        a_ref[...] = a.astype(BF16)
        b_ref[...] = b.astype(BF16)
        f = ((a * _sigmoid(a)) * b).astype(BF16)
        f_ref[...] = f
        acc[...] += _dot(f, wd_ref[...])

        @pl.when(j == nj - 1)
        def _():
            if not with_loss:
                o_ref[...] = acc[...]
                return
            e = acc[...] - t_ref[...]
            o_ref[...] = e * (1.0 / D_MODEL)
            col = jnp.sum(e * e, axis=0, keepdims=True)

            @pl.when(i == 0)
            def _():
                sq[...] = col

            @pl.when(i > 0)
            def _():
                sq[...] += col

            @pl.when(i == ni - 1)
            def _():
                l_ref[...] = jnp.full((1, 128), jnp.sum(sq[...]) * (0.5 / D_MODEL), F32)

    row = pl.BlockSpec((TM, D_MODEL), lambda i, j: (i, 0))
    tile = pl.BlockSpec((TM, TF), lambda i, j: (i, j))
    return _pallas_after(
        dep, body, x1, g, wg, wu, wd, *([tgt] if with_loss else []), name=name, grid=(ni, nj),
        in_specs=[row, pl.BlockSpec((1, D_MODEL), lambda i, j: (0, 0))]
        + [pl.BlockSpec((TF, D_MODEL), lambda i, j: (j, 0))] * 3 + ([row] if with_loss else []),
        out_specs=[row, tile, tile, tile, row] + ([pl.BlockSpec((1, 128), lambda i, j: (0, 0))] if with_loss else []),
        out_shape=[jax.ShapeDtypeStruct((S, D_MODEL), BF16)] + [jax.ShapeDtypeStruct((S, D_FF), BF16)] * 3
        + [jax.ShapeDtypeStruct((S, D_MODEL), F32)]
        + ([jax.ShapeDtypeStruct((1, 128), F32)] if with_loss else []),
        scratch_shapes=[pltpu.VMEM((TM, D_MODEL), F32)] + ([pltpu.VMEM((1, D_MODEL), F32)] if with_loss else []),
        compiler_params=_cparams(("arbitrary", "arbitrary") if with_loss else ("parallel", "arbitrary")),
    )


def ffn_bwd(dx2, a, b, x1, g, wg, wu, wd, name, dep=None):
    S = dx2.shape[0]
    nj = D_FF // TF

    def body(dx_ref, a_ref, b_ref, x_ref, g_ref, wg_ref, wu_ref, wd_ref, da_ref, db_ref, dx1_ref, dg_ref, dxb, acc):
        i, j = pl.program_id(0), pl.program_id(1)

        @pl.when((i == 0) & (j == 0))
        def _():
            dg_ref[...] = jnp.zeros_like(dg_ref)

        @pl.when(j == 0)
        def _():
            dxb[...] = dx_ref[...].astype(BF16)
            acc[...] = jnp.zeros_like(acc)

        df = _dot_nt(dxb[...], wd_ref[...])
        av = a_ref[...].astype(F32)
        bv = b_ref[...].astype(F32)
        sg = _sigmoid(av)
        da = ((df * bv) * (sg * (1.0 + av * (1.0 - sg)))).astype(BF16)
        db = (df * (av * sg)).astype(BF16)
        da_ref[...] = da
        db_ref[...] = db
        acc[...] += _dot(da, wg_ref[...]) + _dot(db, wu_ref[...])

        @pl.when(j == nj - 1)
        def _():
            xf = x_ref[...]
            r = _rms(xf)
            xh = xf * r
            dh = acc[...]
            dg_ref[...] += jnp.sum(dh * xh, axis=0, keepdims=True)
            dx1_ref[...] = dx_ref[...] + _rms_bwd(dh, xh, r, g_ref[...])

    return _pallas_after(
        dep, body, dx2, a, b, x1, g, wg, wu, wd, name=name, grid=(S // TM, nj),
        in_specs=[pl.BlockSpec((TM, D_MODEL), lambda i, j: (i, 0)),
                  pl.BlockSpec((TM, TF), lambda i, j: (i, j)), pl.BlockSpec((TM, TF), lambda i, j: (i, j)),
                  pl.BlockSpec((TM, D_MODEL), lambda i, j: (i, 0)), pl.BlockSpec((1, D_MODEL), lambda i, j: (0, 0))]
        + [pl.BlockSpec((TF, D_MODEL), lambda i, j: (j, 0))] * 3,
        out_specs=[pl.BlockSpec((TM, TF), lambda i, j: (i, j)), pl.BlockSpec((TM, TF), lambda i, j: (i, j)),
                   pl.BlockSpec((TM, D_MODEL), lambda i, j: (i, 0)), pl.BlockSpec((1, D_MODEL), lambda i, j: (0, 0))],
        out_shape=[jax.ShapeDtypeStruct((S, D_FF), BF16), jax.ShapeDtypeStruct((S, D_FF), BF16),
                   jax.ShapeDtypeStruct((S, D_MODEL), F32), jax.ShapeDtypeStruct((1, D_MODEL), F32)],
        scratch_shapes=[pltpu.VMEM((TM, D_MODEL), BF16), pltpu.VMEM((TM, D_MODEL), F32)],
        compiler_params=_cparams(("arbitrary", "arbitrary")),
    )


def grad_weight(lhs, rhs, tm, name, dep=None):
    S, N = rhs.shape
    M = lhs[0].shape[1]
    tk = 1024
    nk = S // tk
    n_l = n_o = len(lhs)

    def body(*refs):
        l_refs, r_ref = refs[:n_l], refs[n_l]
        o_refs, accs = refs[n_l + 1:n_l + 1 + n_o], refs[n_l + 1 + n_o:]
        k = pl.program_id(1)
        rv = r_ref[...].astype(BF16)
        for l_ref, o_ref, acc in zip(l_refs, o_refs, accs):
            @pl.when(k == 0)
            def _():
                acc[...] = jnp.zeros_like(acc)

            acc[...] += _dot_tn(l_ref[...], rv)

            @pl.when(k == nk - 1)
            def _():
                o_ref[...] = acc[...].astype(BF16)

    return _pallas_after(
        dep, body, *lhs, rhs, name=name, grid=(M // tm, nk),
        in_specs=[pl.BlockSpec((tk, tm), lambda i, k: (k, i)) for _ in lhs] + [pl.BlockSpec((tk, N), lambda i, k: (k, 0))],
        out_specs=[pl.BlockSpec((tm, N), lambda i, k: (i, 0))] * n_o,
        out_shape=[jax.ShapeDtypeStruct((M, N), BF16)] * n_o,
        scratch_shapes=[pltpu.VMEM((tm, N), F32)] * n_o,
        compiler_params=_cparams(("parallel", "arbitrary")),
    )


def proj_bwd(dproj, w, x, g, dres, name, dep=None):
    S, N = dproj.shape

    def body(dp_ref, w_ref, x_ref, g_ref, dr_ref, dx_ref, dg_ref):
        i = pl.program_id(0)

        @pl.when(i == 0)
        def _():
            dg_ref[...] = jnp.zeros_like(dg_ref)

        dh = _dot(dp_ref[...], w_ref[...])
        xf = x_ref[...]
        r = _rms(xf)
        xh = xf * r
        dg_ref[...] += jnp.sum(dh * xh, axis=0, keepdims=True)
        dx_ref[...] = dr_ref[...] + _rms_bwd(dh, xh, r, g_ref[...])

    return _pallas_after(
        dep, body, dproj, w, x, g, dres, name=name, grid=(S // TM,),
        in_specs=[pl.BlockSpec((TM, N), lambda i: (i, 0)), pl.BlockSpec((N, D_MODEL), lambda i: (0, 0)),
                  pl.BlockSpec((TM, D_MODEL), lambda i: (i, 0)), pl.BlockSpec((1, D_MODEL), lambda i: (0, 0)),
                  pl.BlockSpec((TM, D_MODEL), lambda i: (i, 0))],
        out_specs=[pl.BlockSpec((TM, D_MODEL), lambda i: (i, 0)), pl.BlockSpec((1, D_MODEL), lambda i: (0, 0))],
        out_shape=[jax.ShapeDtypeStruct((S, D_MODEL), F32), jax.ShapeDtypeStruct((1, D_MODEL), F32)],
        compiler_params=_cparams(("arbitrary",)),
    )


def mixer_bwd(proj, bias, y, ao, dx1, w_o, sinks, conv_w, qg, kg, cog, aog, name, dep=None):
    S = proj.shape[0]
    nb = S // BLK
    KV_W = 2 * N_KV * HEAD

    def body(sinks_ref, p_ref, kvp_ref, bias_ref, y_ref, ao_ref, dx_first_ref, dx_next_ref, wo_ref, cw_ref, qg_ref,
             kg_ref, cog_ref, aog_ref, dp_ref, dcw_ref, dqg_ref, dkg_ref, dsk_ref, dcog_ref, daog_ref,
             dycar, kcar, vcar, dmix_scr):
        step = pl.program_id(0)

        @pl.when(step == 0)
        def _():
            dmix_scr[...] = _dot_nt(dx_first_ref[...].astype(BF16), wo_ref[...])
            dycar[...] = jnp.zeros_like(dycar)
            kcar[...] = jnp.zeros_like(kcar)
            vcar[...] = jnp.zeros_like(vcar)
            dcw_ref[...] = jnp.zeros_like(dcw_ref)
            dqg_ref[...] = jnp.zeros_like(dqg_ref)
            dkg_ref[...] = jnp.zeros_like(dkg_ref)
            dsk_ref[...] = jnp.zeros_like(dsk_ref)
            dcog_ref[...] = jnp.zeros_like(dcog_ref)
            daog_ref[...] = jnp.zeros_like(daog_ref)

        dma = dmix_scr[:, CONV_CH:]

        aov = ao_ref[...].astype(F32)
        ra = _rms(aov)
        ah = aov * ra
        daog_ref[...] += jnp.sum(dma * ah, axis=0, keepdims=True)
        dao = _rms_bwd(dma, ah, ra, aog_ref[...])

        low = _low_half()
        high = jnp.logical_not(low)
        q_gain = qg_ref[...] * SCALE
        k, v = _kv_pairs(p_ref, kvp_ref)
        rk = _pair_rms(k, low)
        kh = k * rk
        k_of = _one_head_in_both_halves((kh * kg_ref[...]).astype(BF16), low)
        v_of = _one_head_in_both_halves(v, low)
        lane8 = lax.broadcasted_iota(jnp.int32, (1, N_Q), 1)
        dsk = jnp.zeros((1, N_Q), F32)
        dqg = jnp.zeros((1, PAIR), F32)
        dv_t = [jnp.zeros((PAIR, 2 * BLK), F32) for _ in range(N_KV)]
        dkn_t = [jnp.zeros((PAIR, 2 * BLK), F32) for _ in range(N_KV)]
        rq, qh, q_one, do_one, delta = [], [], [], [], []
        for pair in range(N_Q // 2):
            cols = slice(PAIR * pair, PAIR * (pair + 1))
            q = p_ref[:, O_Q + PAIR * pair:O_Q + PAIR * (pair + 1)].astype(F32)
            rq.append(_pair_rms(q, low))
            qh.append(q * rq[pair])
            qn = (qh[pair] * q_gain).astype(BF16)
            do = dao[:, cols]
            do_b = do.astype(BF16)
            delta += _half_sums(do * aov[:, cols], low)
            for mine in (low, high):
                q_one.append(jnp.where(mine, qn, jnp.zeros_like(qn)))
                do_one.append(jnp.where(mine, do_b, jnp.zeros_like(do_b)))
        heads = range(N_Q)
        scores = [_dot_nt(q_one[hq], k_of[hq // GRP]) + bias_ref[0] for hq in heads]
        dprobs = [_dot_nt(do_one[hq], v_of[hq // GRP]) for hq in heads]
        probs = [_softmax_with_sink(scores[hq], sinks_ref[hq]) for hq in heads]
        ds = [(probs[hq][0] * (dprobs[hq] - delta[hq])).astype(BF16) for hq in heads]
        for hq in heads:
            dsk = dsk - jnp.where(lane8 == hq, jnp.sum(probs[hq][1] * delta[hq]), 0.0)
            dv_t[hq // GRP] = dv_t[hq // GRP] + _dot_tn(do_one[hq], probs[hq][0].astype(BF16))
            dkn_t[hq // GRP] = dkn_t[hq // GRP] + _dot_tn(q_one[hq], ds[hq])
        dqn_of = [_dot(ds[hq], k_of[hq // GRP]) for hq in heads]
        for pair in range(N_Q // 2):
            dqn = jnp.where(low, dqn_of[2 * pair], dqn_of[2 * pair + 1])
            dqg = dqg + jnp.sum(dqn * qh[pair], axis=0, keepdims=True)
            dqh = dqn * q_gain
            dp_ref[:, O_Q + PAIR * pair:O_Q + PAIR * (pair + 1)] = (
                rq[pair] * (dqh - qh[pair] * _pair_mean(dqh * qh[pair], low))).astype(BF16)

        def untranspose(parts):
            return jnp.concatenate([t[:HEAD] + t[HEAD:] for t in parts], axis=0).T

        dv = untranspose(dv_t)
        dkn = untranspose(dkn_t)
        dkg_ref[...] += jnp.sum(dkn * kh, axis=0, keepdims=True)
        dkh = dkn * kg_ref[...]
        dk = rk * (dkh - kh * _pair_mean(dkh * kh, low))
        dp_ref[:, O_K:O_V] = (kcar[...] + dk[BLK:, :]).astype(BF16)
        dp_ref[:, O_V:] = (vcar[...] + dv[BLK:, :]).astype(BF16)
        kcar[...] = dk[:BLK, :]
        vcar[...] = dv[:BLK, :]
        dsk_ref[...] += dsk
        dqg_ref[...] += dqg * SCALE

        bg = p_ref[:, 0:CONV_CH].astype(F32)
        cg = p_ref[:, CONV_CH:2 * CONV_CH].astype(F32)
        hc = p_ref[:, 2 * CONV_CH:3 * CONV_CH].astype(F32)
        yv = y_ref[...].astype(F32)
        dmc = dmix_scr[:, 0:CONV_CH]
        co = bg * yv
        rc = _rms(co)
        ch = co * rc
        dcog_ref[...] += jnp.sum(dmc * ch, axis=0, keepdims=True)
        dco = _rms_bwd(dmc, ch, rc, cog_ref[...])
        dp_ref[:, 0:CONV_CH] = (dco * yv).astype(BF16)
        dy = dco * bg
        row = lax.broadcasted_iota(jnp.int32, (BLK, CONV_CH), 0)
        nxt = dycar[...]
        dy1 = jnp.where(row == BLK - 1, nxt[0:1, :], pltpu.roll(dy, BLK - 1, 0))
        dy2 = jnp.where(row == BLK - 2, nxt[0:1, :], jnp.where(row == BLK - 1, nxt[1:2, :], pltpu.roll(dy, BLK - 2, 0)))
        dycar[...] = dy[0:8, :]
        du = cw_ref[2:3, :] * dy + cw_ref[1:2, :] * dy1 + cw_ref[0:1, :] * dy2
        dp_ref[:, CONV_CH:2 * CONV_CH] = (du * hc).astype(BF16)
        dp_ref[:, 2 * CONV_CH:3 * CONV_CH] = (du * cg).astype(BF16)
        u = cg * hc
        dcw_ref[0:1, :] += jnp.sum(dy2 * u, axis=0, keepdims=True)
        dcw_ref[1:2, :] += jnp.sum(dy1 * u, axis=0, keepdims=True)
        dcw_ref[2:3, :] += jnp.sum(dy * u, axis=0, keepdims=True)

        dmix_scr[...] = _dot_nt(dx_next_ref[...].astype(BF16), wo_ref[...])

    small = lambda shape: pl.BlockSpec(shape, lambda s: (0, 0))
    blk = lambda w: pl.BlockSpec((BLK, w), lambda s: (nb - 1 - s, 0))
    return _pallas_after(
        dep, body, sinks, proj, proj, bias, y, ao, dx1, dx1, w_o, conv_w, qg, kg, cog, aog, name=name, grid=(nb,),
        in_specs=[pl.BlockSpec(memory_space=pltpu.SMEM),
                  blk(IN_COLS),
                  pl.BlockSpec((BLK, KV_W), lambda s: (jnp.maximum(nb - 2 - s, 0), O_K // KV_W)),
                  pl.BlockSpec((1, BLK, 2 * BLK), lambda s: (jnp.minimum(nb - 1 - s, 1), 0, 0)),
                  blk(CONV_CH), blk(ATTN_W),
                  pl.BlockSpec((BLK, D_MODEL), lambda s: (nb - 1, 0)),
                  pl.BlockSpec((BLK, D_MODEL), lambda s: (jnp.maximum(nb - 2 - s, 0), 0)),
                  small((D_MODEL, D_MODEL)),
                  small((3, CONV_CH)), small((1, PAIR)), small((1, PAIR)), small((1, CONV_CH)), small((1, ATTN_W))],
        out_specs=[blk(IN_COLS), small((3, CONV_CH)), small((1, PAIR)), small((1, PAIR)), small((1, N_Q)),
                   small((1, CONV_CH)), small((1, ATTN_W))],
        out_shape=[jax.ShapeDtypeStruct((S, IN_COLS), BF16), jax.ShapeDtypeStruct((3, CONV_CH), F32),
                   jax.ShapeDtypeStruct((1, PAIR), F32), jax.ShapeDtypeStruct((1, PAIR), F32),
                   jax.ShapeDtypeStruct((1, N_Q), F32), jax.ShapeDtypeStruct((1, CONV_CH), F32),
                   jax.ShapeDtypeStruct((1, ATTN_W), F32)],
        scratch_shapes=[pltpu.VMEM((8, CONV_CH), F32), pltpu.VMEM((BLK, PAIR), F32), pltpu.VMEM((BLK, PAIR), F32),
                        pltpu.VMEM((BLK, D_MODEL), F32)],
        compiler_params=_cparams(("arbitrary",)),
    )


OTHER_CHIPS = ((1, 0), (0, 1), (1, 1))


def _position():
    return lax.axis_index("x"), lax.axis_index("y"), lax.axis_index("c")


def all_gather(shards, name):
    n = len(shards)

    def body(*refs):
        x_refs, out_refs = refs[:n], refs[n:2 * n]
        send_sems, recv_sems, local_sems = refs[2 * n:]
        x, y, c = _position()
        me, sibling = (x, y, c), (x, y, 1 - c)
        chips = [(x ^ mx, y ^ my) for mx, my in OTHER_CHIPS]

        def slab(a, px, py, pc):
            return out_refs[a].at[4 * px + 2 * py + pc]

        def copy(a, k, block, to, src=None):
            return pltpu.make_async_remote_copy(
                src_ref=slab(a, *block) if src is None else src, dst_ref=slab(a, *block),
                send_sem=send_sems.at[7 * a + k], recv_sem=recv_sems.at[7 * a + k], device_id=to, device_id_type=MESH_T)

        mine = [pltpu.make_async_copy(x_refs[a], slab(a, *me), local_sems.at[a]) for a in range(n)]
        for cp in mine:
            cp.start()
        first = []
        for a in range(n):
            first.append(copy(a, 0, me, sibling, src=x_refs[a]))
            first += [copy(a, 1 + j, me, (*chip, c), src=x_refs[a]) for j, chip in enumerate(chips)]
        for cp in first:
            cp.start()
        passed = []
        for a in range(n):
            for j, chip in enumerate(chips):
                copy(a, 1 + j, (*chip, c), me).wait_recv()
                passed.append(copy(a, 4 + j, (*chip, c), sibling))
                passed[-1].start()
        for a in range(n):
            copy(a, 0, sibling, me).wait_recv()
            for j, chip in enumerate(chips):
                copy(a, 4 + j, (*chip, 1 - c), me).wait_recv()
        for cp in first + passed:
            cp.wait_send()
        for cp in mine:
            cp.wait()

    return pl.pallas_call(
        body, name=name, in_specs=[ANY] * n, out_specs=[ANY] * n,
        out_shape=[jax.ShapeDtypeStruct((N_DEV, *s.shape), s.dtype) for s in shards],
        scratch_shapes=[pltpu.SemaphoreType.DMA((7 * n,)), pltpu.SemaphoreType.DMA((7 * n,)),
                        pltpu.SemaphoreType.DMA((n,))],
    )(*shards)


class SplitCopy:
    def __init__(self, name, arrays, n_copies, plan, after=None):
        n = len(arrays)
        self.name, self.n, self.n_copies, self.plan = name, n, n_copies, plan
        extra = [] if after is None else [after]

        def body(*refs):
            in_refs = refs[:n]
            send_sems, recv_sems = refs[n + len(extra)], refs[n + len(extra) + 1]
            token = refs[2 * n + len(extra) + 2]
            for k, (src, dst, to) in enumerate(plan(_position(), in_refs)):
                pltpu.make_async_remote_copy(src_ref=src, dst_ref=dst, send_sem=send_sems.at[k],
                                             recv_sem=recv_sems.at[k], device_id=to, device_id_type=MESH_T).start()
            token[...] = jnp.zeros_like(token)

        outs = pl.pallas_call(
            body, name=name + "_start",
            out_shape=(pltpu.SemaphoreType.DMA((n_copies,)), pltpu.SemaphoreType.DMA((n_copies,)),
                       *[pltpu.HBM(a.shape, a.dtype) for a in arrays], jax.ShapeDtypeStruct((8, 128), F32)),
            in_specs=[HBM] * n + [ANY] * len(extra),
            out_specs=(SEM, SEM, *[HBM] * n, pl.BlockSpec(memory_space=pltpu.VMEM)),
            input_output_aliases={i: 2 + i for i in range(n)},
            compiler_params=pltpu.CompilerParams(has_side_effects=pltpu.SideEffectType.DATAFLOW_SIDE_EFFECTING),
        )(*[pltpu.with_memory_space_constraint(a, pltpu.HBM) for a in arrays], *extra)
        self.send_sems, self.recv_sems = outs[0], outs[1]
        self.arrays, self.token = list(outs[2:2 + n]), outs[2 + n]

    def wait(self, after):
        n, plan = self.n, self.plan

        def body(*refs):
            in_refs, send_sems, recv_sems = refs[:n], refs[n], refs[n + 1]
            for k, (src, dst, to) in enumerate(plan(_position(), in_refs)):
                cp = pltpu.make_async_remote_copy(src_ref=src, dst_ref=dst, send_sem=send_sems.at[k],
                                                  recv_sem=recv_sems.at[k], device_id=to, device_id_type=MESH_T)
                cp.wait_send()
                cp.wait_recv()

        outs = pl.pallas_call(
            body, name=self.name + "_wait",
            out_shape=tuple(pltpu.HBM(a.shape, a.dtype) for a in self.arrays),
            in_specs=[HBM] * n + [SEM, SEM, ANY], out_specs=tuple([HBM] * n),
            input_output_aliases={i: i for i in range(n)},
            compiler_params=pltpu.CompilerParams(has_side_effects=pltpu.SideEffectType.DATAFLOW_SIDE_EFFECTING),
        )(*self.arrays, self.send_sems, self.recv_sems, after)
        return list(outs)


def gather_start(shards, me, name, after=None):
    n = len(shards)
    lands = [lax.dynamic_update_slice(lax.empty((N_DEV, *s.shape), s.dtype), s[None], (me, 0, 0)) for s in shards]

    def plan(pos, refs):
        x, y, c = pos
        return [(refs[a], refs[n + a].at[4 * x + 2 * y + c], (x ^ mx, y ^ my, c))
                for a in range(n) for mx, my in OTHER_CHIPS]

    return SplitCopy(name, list(shards) + lands, 3 * n, plan, after=after)


def sibling_start(lands, name):
    n = len(lands)

    def plan(pos, refs):
        x, y, c = pos
        return [(refs[a].at[2 * q + c], refs[a].at[2 * q + c], (x, y, 1 - c)) for a in range(n) for q in range(4)]

    return SplitCopy(name, list(lands), 4 * n, plan)


def scatter_start(slabs, name):
    n = len(slabs)
    lands = [lax.empty((N_DEV - 1, *g.shape[1:]), g.dtype) for g in slabs]

    def plan(pos, refs):
        x, y, c = pos
        copies = []
        for a in range(n):
            for r in range(1, N_DEV):
                px, py, pc = x ^ ((r >> 2) & 1), y ^ ((r >> 1) & 1), c ^ (r & 1)
                copies.append((refs[a].at[4 * px + 2 * py + pc], refs[n + a].at[r - 1], (px, py, pc)))
        return copies

    return SplitCopy(name, list(slabs) + lands, (N_DEV - 1) * n, plan)


def all_reduce_small(v, name, dep=None):
    R, W = v.shape

    def body(v_ref, o_ref, recv, send_sems, recv_sems):
        x, y, c = _position()
        me = 4 * x + 2 * y + c
        copies = []
        for r in range(1, N_DEV):
            to = (x ^ ((r >> 2) & 1), y ^ ((r >> 1) & 1), c ^ (r & 1))
            copies.append(pltpu.make_async_remote_copy(
                src_ref=v_ref, dst_ref=recv.at[me], send_sem=send_sems.at[r - 1], recv_sem=recv_sems.at[r - 1],
                device_id=to, device_id_type=MESH_T))
        for cp in copies:
            cp.start()
        recv[pl.ds(me, 1)] = v_ref[...][None]
        for cp in copies:
            cp.wait()
        acc = recv[0]
        for s in range(1, N_DEV):
            acc = acc + recv[s]
        o_ref[...] = acc

    return _pallas_after(
        dep, body, v, name=name,
        in_specs=[pl.BlockSpec(memory_space=pltpu.VMEM)], out_specs=pl.BlockSpec(memory_space=pltpu.VMEM),
        out_shape=jax.ShapeDtypeStruct((R, W), F32),
        scratch_shapes=[pltpu.VMEM((N_DEV, R, W), F32), pltpu.SemaphoreType.DMA((N_DEV - 1,)),
                        pltpu.SemaphoreType.DMA((N_DEV - 1,))],
    )


def _row_tile(rows):
    if rows <= 512:
        return rows
    return max(t for t in range(8, 513, 8) if rows % t == 0)


def _adamw_update(w, g, m, v):
    mn = ADAM_B1 * m + (1.0 - ADAM_B1) * g
    vn = ADAM_B2 * v + (1.0 - ADAM_B2) * (g * g)
    m_hat = mn / (1.0 - ADAM_B1 ** ADAM_STEP)
    v_hat = vn / (1.0 - ADAM_B2 ** ADAM_STEP)
    return -ADAM_LR * (m_hat / (jnp.sqrt(v_hat) + ADAM_EPS) + ADAM_WD * w), mn, vn


def adamw(w, g, m, v, name):
    R, W = w.shape
    tr = _row_tile(R)

    def body(w_ref, g_ref, m_ref, v_ref, d_ref, mo_ref, vo_ref):
        d_ref[...], mo_ref[...], vo_ref[...] = _adamw_update(w_ref[...], g_ref[...], m_ref[...], v_ref[...])

    spec = pl.BlockSpec((tr, W), lambda i: (i, 0))
    return pl.pallas_call(
        body, name=name, grid=(R // tr,), in_specs=[spec] * 4, out_specs=[spec] * 3,
        out_shape=[jax.ShapeDtypeStruct((R, W), F32)] * 3,
        compiler_params=_cparams(("parallel",)),
    )(w, g, m, v)


def reduce_adamw(slabs, land, w, m, v, layer, me_arr, name, others=None):
    _, R, W = slabs.shape
    tr = _row_tile(R)
    n_other = 0 if others is None else 4

    def body(me_ref, s_ref, l_ref, w_ref, m_ref, v_ref, *rest):
        g_ref, d_ref, mo_ref, vo_ref = rest[n_other:]
        g = s_ref[0].astype(F32)
        for r in range(N_DEV - 1):
            g = g + l_ref[r].astype(F32)
        g_ref[0] = g
        d_ref[0], mo_ref[0], vo_ref[0] = _adamw_update(w_ref[0], g, m_ref[0], v_ref[0])

    spec = pl.BlockSpec((1, tr, W), lambda i, me: (layer, i, 0))
    return pl.pallas_call(
        body, name=name,
        grid_spec=pltpu.PrefetchScalarGridSpec(
            num_scalar_prefetch=1, grid=(R // tr,),
            in_specs=[pl.BlockSpec((1, tr, W), lambda i, me: (me[0], i, 0)),
                      pl.BlockSpec((N_DEV - 1, tr, W), lambda i, me: (0, i, 0)), spec, spec, spec] + [ANY] * n_other,
            out_specs=[spec] * 4),
        out_shape=[jax.ShapeDtypeStruct((DEPTH, R, W), F32)] * 4,
        input_output_aliases={6 + i: i for i in range(n_other)},
        compiler_params=_cparams(("parallel",)),
    )(me_arr, slabs, land, w, m, v, *([] if others is None else others))


SMALL_NAMES = ("norm1_g", "q_norm_g", "k_norm_g", "sinks", "conv_out_g", "attn_out_g", "norm2_g", "conv_w")
SMALL_SIZES = (D_MODEL, HEAD, HEAD, N_Q, CONV_CH, ATTN_W, D_MODEL, 3 * CONV_CH)
SMALL_ROWS = 80


def kernel(x, norm1_g, w_in, conv_w, q_norm_g, k_norm_g, sinks, conv_out_g, attn_out_g, w_o, norm2_g, w_gate, w_up, w_down, loss_target, m_norm1_g, m_w_in, m_conv_w, m_q_norm_g, m_k_norm_g, m_sinks, m_conv_out_g, m_attn_out_g, m_w_o, m_norm2_g, m_w_gate, m_w_up, m_w_down, v_norm1_g, v_w_in, v_conv_w, v_q_norm_g, v_k_norm_g, v_sinks, v_conv_out_g, v_attn_out_g, v_w_o, v_norm2_g, v_w_gate, v_w_up, v_w_down):
    xi, yi, ci = _position()
    me = 4 * xi + 2 * yi + ci
    me_arr = jnp.reshape(me, (1,)).astype(jnp.int32)
    xs, tgt = x[0], loss_target[0]
    bf = lambda a: a.astype(BF16)
    bias = band_bias()

    t = lambda a: jnp.swapaxes(a, 1, 2)
    shard = dict(w_in=(t(w_in), t(m_w_in), t(v_w_in)), w_o=(w_o, m_w_o, v_w_o),
                 w_gate=(t(w_gate), t(m_w_gate), t(v_w_gate)), w_up=(t(w_up), t(m_w_up), t(v_w_up)),
                 w_down=(w_down, m_w_down, v_w_down))
    wb = {n: bf(shard[n][0]) for n in shard}

    g_in0, g_o0, g_conv = all_gather([wb["w_in"][0], wb["w_o"][0], conv_w.reshape(DEPTH * 3, HEAD)], "gather_first")
    ag_ffn0 = gather_start([wb["w_gate"][0], wb["w_up"][0], wb["w_down"][0]], me, "gather_ffn0", after=g_in0)
    ag_mix1 = gather_start([wb["w_in"][1], wb["w_o"][1]], me, "gather_mix1", after=ag_ffn0.token)
    ag_ffn1 = gather_start([wb["w_gate"][1], wb["w_up"][1], wb["w_down"][1]], me, "gather_ffn1", after=ag_mix1.token)
    conv_full = g_conv.reshape(N_DEV, DEPTH, 3, HEAD).transpose(1, 2, 0, 3).reshape(DEPTH, 3, CONV_CH)
    pair_gain = lambda g: jnp.tile(g[None], (1, 2))
    small = [dict(norm1_g=norm1_g[l][None], conv_w=conv_full[l], q_norm_g=pair_gain(q_norm_g[l]),
                  k_norm_g=pair_gain(k_norm_g[l]), sinks=sinks[l], conv_out_g=conv_out_g[l][None],
                  attn_out_g=attn_out_g[l][None], norm2_g=norm2_g[l][None]) for l in range(DEPTH)]
    whole = lambda g: g.reshape(-1, D_MODEL)
    weights = [dict(w_in=whole(g_in0), w_o=whole(g_o0)), {}]

    def ffn_weights(g_gate, g_up, g_down):
        return dict(w_gate=whole(g_gate), w_up=whole(g_up), w_down=whole(g_down))

    saved = []
    xl = xs
    for l in range(DEPTH):
        sp, wl = small[l], weights[l]
        h, proj = norm_proj(xl, sp["norm1_g"], wl["w_in"], f"norm_proj{l}",
                            dep=ag_ffn1.token if l == 0 else pass_ffn1.token)
        mixer_args = (proj, bias, sp["sinks"], sp["conv_w"], sp["q_norm_g"], sp["k_norm_g"], sp["conv_out_g"],
                      sp["attn_out_g"], f"mixer_fwd{l}")
        if l == 0:
            mix, y, ao = mixer_fwd(*mixer_args)
            pass_ffn0 = sibling_start(ag_ffn0.wait(mix)[3:], "pass_ffn0")
            x1 = matmul_residual(mix, wl["w_o"], xl, "out_proj0", dep=pass_ffn0.token)
            wl.update(ffn_weights(*pass_ffn0.wait(x1)))
            pass_mix1 = sibling_start(ag_mix1.wait(x1)[2:], "pass_mix1")
            h2, a, b, f, x2 = ffn_fwd(x1, sp["norm2_g"], wl["w_gate"], wl["w_up"], wl["w_down"], "ffn_fwd0",
                                      dep=pass_mix1.token)
            g_in, g_o = pass_mix1.wait(x2)
            weights[1] = dict(w_in=whole(g_in), w_o=whole(g_o))
            pass_ffn1 = sibling_start(ag_ffn1.wait(x2)[3:], "pass_ffn1")
        else:
            mix, y, ao, x1 = mixer_fwd(*mixer_args, residual=xl, w_o=wl["w_o"])
            wl.update(ffn_weights(*pass_ffn1.wait(x1)))
            h2, a, b, f, dx, loss_row = ffn_fwd(x1, sp["norm2_g"], wl["w_gate"], wl["w_up"], wl["w_down"], "ffn_fwd1",
                                                tgt=tgt)
            x2 = None
        saved.append((xl, h, proj, mix, y, ao, x1, h2, a, b, f))
        xl = x2

    stepped = {n: None for n in shard}
    slabs = lambda d: d.reshape(N_DEV, -1, D_MODEL)
    gsmall = [None] * DEPTH

    def finish(sc, names, after, l):
        arrays = sc.wait(after)
        k = len(names)
        for i, n in enumerate(names):
            w, m, v = shard[n]
            stepped[n] = reduce_adamw(arrays[i], arrays[k + i], w, m, v, l, me_arr, f"reduce_adamw_{n}{l}",
                                      others=stepped[n])

    for l in reversed(range(DEPTH)):
        sp, wl = small[l], weights[l]
        x0, h, proj, mix, y, ao, x1, h2, a, b, f = saved[l]
        (d_wd,) = grad_weight((f,), dx, TF, f"grad_w_down{l}")
        sc_down = scatter_start([slabs(d_wd)], f"scatter_w_down{l}")
        da, db, dx1, d_g2 = ffn_bwd(dx, a, b, x1, sp["norm2_g"], wl["w_gate"], wl["w_up"], wl["w_down"],
                                    f"ffn_bwd{l}", dep=sc_down.token)
        d_wg, d_wu = grad_weight((da, db), h2, TF, f"grad_w_gate_up{l}")
        (d_wo,) = grad_weight((mix,), dx1, D_MODEL, f"grad_w_o{l}")
        sc_rest = scatter_start([slabs(d_wg), slabs(d_wu), slabs(d_wo)], f"scatter_w_gate_up_o{l}")
        dproj, d_cw, d_qg, d_kg, d_sk, d_cog, d_aog = mixer_bwd(
            proj, bias, y, ao, dx1, wl["w_o"], sp["sinks"], sp["conv_w"], sp["q_norm_g"], sp["k_norm_g"],
            sp["conv_out_g"], sp["attn_out_g"], f"mixer_bwd{l}", dep=sc_rest.token)
        finish(sc_down, ["w_down"], dproj, l)
        finish(sc_rest, ["w_gate", "w_up", "w_o"], dproj, l)
        (d_win,) = grad_weight((dproj,), h, IN_COLS // 2, f"grad_w_in{l}")
        sc_in = scatter_start([slabs(d_win)], f"scatter_w_in{l}")
        dx, d_g1 = proj_bwd(dproj, wl["w_in"], x0, sp["norm1_g"], dx1, f"proj_bwd{l}", dep=sc_in.token)
        finish(sc_in, ["w_in"], dx, l)
        both_heads = lambda d: d[:, :HEAD] + d[:, HEAD:]
        gsmall[l] = dict(norm1_g=d_g1, conv_w=d_cw, q_norm_g=both_heads(d_qg), k_norm_g=both_heads(d_kg), sinks=d_sk,
                         conv_out_g=d_cog, attn_out_g=d_aog, norm2_g=d_g2)
    grad_x = dx[None]

    flat = jnp.concatenate([loss_row[0, 0:1]] + [gsmall[l][n].reshape(-1) for l in range(DEPTH) for n in SMALL_NAMES])
    flat = jnp.pad(flat, (0, SMALL_ROWS * 128 - flat.shape[0])).reshape(SMALL_ROWS, 128)
    flat = all_reduce_small(flat, "all_reduce_small_grads").reshape(-1)
    loss = flat[0]
    gs = {n: [] for n in SMALL_NAMES}
    off = 1
    for l in range(DEPTH):
        for n, size in zip(SMALL_NAMES, SMALL_SIZES):
            gs[n].append(flat[off:off + size])
            off += size
    gs = {n: jnp.stack(v) for n, v in gs.items()}
    g_conv = lax.dynamic_slice(gs["conv_w"].reshape(DEPTH, 3, CONV_CH), (0, 0, me * HEAD), (DEPTH, 3, HEAD))

    gs["conv_w"] = g_conv
    params = dict(norm1_g=(norm1_g, m_norm1_g, v_norm1_g), conv_w=(conv_w, m_conv_w, v_conv_w),
                  q_norm_g=(q_norm_g, m_q_norm_g, v_q_norm_g), k_norm_g=(k_norm_g, m_k_norm_g, v_k_norm_g),
                  sinks=(sinks, m_sinks, v_sinks), conv_out_g=(conv_out_g, m_conv_out_g, v_conv_out_g),
                  attn_out_g=(attn_out_g, m_attn_out_g, v_attn_out_g), norm2_g=(norm2_g, m_norm2_g, v_norm2_g))
    names = ("norm1_g", "w_in", "conv_w", "q_norm_g", "k_norm_g", "sinks", "conv_out_g", "attn_out_g", "w_o",
             "norm2_g", "w_gate", "w_up", "w_down")

    out = {}
    for n in names:
        if n in shard:
            out[n] = tuple(t(r) for r in stepped[n]) if n in ("w_in", "w_gate", "w_up") else stepped[n]
        else:
            w, m, v = params[n]
            two_d = (-1, w.shape[-1])
            d, mn, vn = adamw(w.reshape(two_d), gs[n].reshape(two_d), m.reshape(two_d), v.reshape(two_d), f"adamw_{n}")
            out[n] = (gs[n].reshape(w.shape), d.reshape(w.shape), mn.reshape(w.shape), vn.reshape(w.shape))
    return (loss, grad_x, *[out[n][i] for i in range(4) for n in names])
```

```python
import functools

import jax
import jax.numpy as jnp
from jax import lax
from jax.experimental import pallas as pl
from jax.experimental.pallas import tpu as pltpu

F32 = jnp.float32
BF16 = jnp.bfloat16

D_MODEL = 1024
CONV_CH = 512
ATTN_W = 512
N_Q = 8
N_KV = 2
GRP = N_Q // N_KV
HEAD = 64
IN_COLS = 2304
D_FF = 2816
BLK = 128
O_Q = 3 * CONV_CH
O_K = O_Q + ATTN_W
O_V = O_K + N_KV * HEAD
EPS = 1e-6
NEG_INF = -1e30
SCALE = HEAD ** -0.5
N_DEV = 8
DEPTH = 2

ADAM_LR = 0.001
ADAM_B1 = 0.9
ADAM_B2 = 0.999
ADAM_EPS = 1e-08
ADAM_WD = 0.01
ADAM_STEP = 10

VMEM_LIMIT = 56 * 1024 * 1024
TM = 512
MESH_T = pl.DeviceIdType.MESH

ROWS_IN, ROWS_O, ROWS_FF = IN_COLS // N_DEV, D_MODEL // N_DEV, D_FF // N_DEV


ANY = pl.BlockSpec(memory_space=pl.ANY)
HBM = pl.BlockSpec(memory_space=pltpu.HBM)
SEM = pl.BlockSpec(memory_space=pltpu.SEMAPHORE)


def _cparams(sem):
    return pltpu.CompilerParams(dimension_semantics=sem, vmem_limit_bytes=VMEM_LIMIT)


def _pallas_after(dep, body, *args, in_specs, **kw):
    if dep is None:
        return pl.pallas_call(body, in_specs=in_specs, **kw)(*args)

    def after_dep(dep_ref, *refs):
        body(*refs)

    return pl.pallas_call(after_dep, in_specs=[ANY, *in_specs], **kw)(dep, *args)


def _dot(a, b):
    return jnp.dot(a, b, preferred_element_type=F32)


def _dot_nt(a, b):
    return lax.dot_general(a, b, (((1,), (1,)), ((), ())), preferred_element_type=F32)


def _dot_tn(a, b):
    return lax.dot_general(a, b, (((0,), (0,)), ((), ())), preferred_element_type=F32)


LANES = 128


def _row_reduce(v, op, reduce):
    w = v.shape[-1]
    if w > LANES and w % LANES == 0:
        acc = v[:, 0:LANES]
        for c in range(1, w // LANES):
            acc = op(acc, v[:, LANES * c:LANES * (c + 1)])
        v = acc
    return reduce(v, axis=-1, keepdims=True)


def _row_sum(v):
    return _row_reduce(v, jnp.add, jnp.sum)


def _row_mean(v):
    return _row_sum(v) * (1.0 / v.shape[-1])


def _rms(v):
    return lax.rsqrt(_row_mean(v * v) + EPS)


def _sigmoid(v):
    return 1.0 / (1.0 + jnp.exp(-v))


def _rms_bwd(dyv, xh, r, g):
    dxh = dyv * g
    return r * (dxh - xh * _row_mean(dxh * xh))


def norm_proj(x, g, w_t, name, dep=None):
    S, N = x.shape[0], w_t.shape[0]

    def body(x_ref, g_ref, w_ref, h_ref, p_ref):
        xf = x_ref[...]
        h = ((xf * _rms(xf)) * g_ref[...]).astype(BF16)
        h_ref[...] = h
        p_ref[...] = _dot_nt(h, w_ref[...]).astype(BF16)

    return _pallas_after(
        dep, body, x, g, w_t, name=name, grid=(S // TM,),
        in_specs=[pl.BlockSpec((TM, D_MODEL), lambda i: (i, 0)),
                  pl.BlockSpec((1, D_MODEL), lambda i: (0, 0)),
                  pl.BlockSpec((N, D_MODEL), lambda i: (0, 0))],
        out_specs=[pl.BlockSpec((TM, D_MODEL), lambda i: (i, 0)),
                   pl.BlockSpec((TM, N), lambda i: (i, 0))],
        out_shape=[jax.ShapeDtypeStruct((S, D_MODEL), BF16), jax.ShapeDtypeStruct((S, N), BF16)],
        compiler_params=_cparams(("parallel",)),
    )


def band_bias():
    qi = lax.broadcasted_iota(jnp.int32, (BLK, 2 * BLK), 0)
    kj = lax.broadcasted_iota(jnp.int32, (BLK, 2 * BLK), 1)
    diff = qi + BLK - kj
    valid = (diff >= 0) & (diff < BLK)
    return jnp.stack([jnp.where(valid & (kj >= BLK), 0.0, NEG_INF), jnp.where(valid, 0.0, NEG_INF)]).astype(F32)


def _softmax_with_sink(s, sink):
    m = jnp.maximum(_row_reduce(s, jnp.maximum, jnp.max), sink)
    p = jnp.exp(s - m)
    es = jnp.exp(sink - m)
    inv = 1.0 / (_row_sum(p) + es)
    return p * inv, es * inv


PAIR = 2 * HEAD


def _low_half():
    return lax.broadcasted_iota(jnp.int32, (1, PAIR), 1) < HEAD


def _half_sums(v, low):
    return (jnp.sum(jnp.where(low, v, 0.0), axis=-1, keepdims=True),
            jnp.sum(jnp.where(low, 0.0, v), axis=-1, keepdims=True))


def _pair_mean(v, low):
    e, o = _half_sums(v, low)
    return jnp.where(low, e, o) * (1.0 / HEAD)


def _pair_rms(v, low):
    return lax.rsqrt(_pair_mean(v * v, low) + EPS)


def _one_head_in_both_halves(pair, low):
    swapped = pltpu.roll(pair, HEAD, 1)
    return jnp.where(low, pair, swapped), jnp.where(low, swapped, pair)


SUB = 2
ROWS = SUB * BLK


def _kv_pairs(p_ref, kvp_ref):
    k = jnp.concatenate([kvp_ref[:, 0:PAIR], p_ref[:, O_K:O_K + PAIR]], axis=0).astype(F32)
    v = jnp.concatenate([kvp_ref[:, PAIR:2 * PAIR], p_ref[:, O_V:O_V + PAIR]], axis=0)
    return k, v


def _band(s):
    return slice(BLK * s, BLK * s + 2 * BLK)


def _block(s):
    return slice(BLK * s, BLK * (s + 1))


def mixer_fwd(proj, bias, sinks, conv_w, qg, kg, cog, aog, name, residual=None, w_o=None):
    S = proj.shape[0]
    steps = S // ROWS
    project = w_o is not None

    def body(sinks_ref, p_ref, kvp_ref, bias_ref, cw_ref, qg_ref, kg_ref, cog_ref, aog_ref, *rest):
        if project:
            x_ref, wo_ref, mix_ref, y_ref, ao_ref, x1_ref, ucar = rest
        else:
            mix_ref, y_ref, ao_ref, ucar = rest
        n = pl.program_id(0)

        @pl.when(n == 0)
        def _():
            ucar[...] = jnp.zeros_like(ucar)

        bg = p_ref[:, 0:CONV_CH].astype(F32)
        u = p_ref[:, CONV_CH:2 * CONV_CH].astype(F32) * p_ref[:, 2 * CONV_CH:3 * CONV_CH].astype(F32)
        row = lax.broadcasted_iota(jnp.int32, (ROWS, CONV_CH), 0)
        prev = ucar[...]
        u1 = jnp.where(row == 0, prev[7:8, :], pltpu.roll(u, 1, 0))
        u2 = jnp.where(row == 0, prev[6:7, :], jnp.where(row == 1, prev[7:8, :], pltpu.roll(u, 2, 0)))
        ucar[...] = u[ROWS - 8:ROWS, :]
        y = cw_ref[0:1, :] * u2 + cw_ref[1:2, :] * u1 + cw_ref[2:3, :] * u
        y_ref[...] = y.astype(BF16)
        co = bg * y
        mix_conv = ((co * _rms(co)) * cog_ref[...]).astype(BF16)
        mix_ref[:, 0:CONV_CH] = mix_conv
        if project:
            x1 = x_ref[...] + _dot(mix_conv, wo_ref[0:CONV_CH, :])

        low = _low_half()
        q_gain = qg_ref[...] * SCALE
        k, v = _kv_pairs(p_ref, kvp_ref)
        kn = ((k * _pair_rms(k, low)) * kg_ref[...]).astype(BF16)
        k_of = _one_head_in_both_halves(kn, low)
        v_of = _one_head_in_both_halves(v, low)
        biases = [bias_ref[jnp.minimum(n, 1)]] + [bias_ref[1]] * (SUB - 1)
        units = [(s, hq) for s in range(SUB) for hq in range(N_Q)]
        q_one = []
        for pair in range(N_Q // 2):
            q = p_ref[:, O_Q + PAIR * pair:O_Q + PAIR * (pair + 1)].astype(F32)
            qn = ((q * _pair_rms(q, low)) * q_gain).astype(BF16)
            q_one += [jnp.where(mine, qn, jnp.zeros_like(qn)) for mine in (low, jnp.logical_not(low))]
        scores = [_dot_nt(q_one[hq][_block(s)], k_of[hq // GRP][_band(s)]) + biases[s] for s, hq in units]
        probs = [_softmax_with_sink(sc, sinks_ref[hq])[0].astype(BF16) for (s, hq), sc in zip(units, scores)]
        o = {u_: _dot(pn, v_of[u_[1] // GRP][_band(u_[0])]) for u_, pn in zip(units, probs)}
        ao = jnp.concatenate(
            [jnp.concatenate([jnp.where(low, o[s, 2 * pair], o[s, 2 * pair + 1]) for pair in range(N_Q // 2)], axis=1)
             for s in range(SUB)], axis=0)
        ao_ref[...] = ao.astype(BF16)
        mix_attn = ((ao * _rms(ao)) * aog_ref[...]).astype(BF16)
        mix_ref[:, CONV_CH:] = mix_attn
        if project:
            x1_ref[...] = x1 + _dot(mix_attn, wo_ref[CONV_CH:, :])

    small = lambda shape: pl.BlockSpec(shape, lambda n: (0, 0))
    rows = lambda w: pl.BlockSpec((ROWS, w), lambda n: (n, 0))
    return pl.pallas_call(
        body, name=name, grid=(steps,),
        in_specs=[pl.BlockSpec(memory_space=pltpu.SMEM),
                  rows(IN_COLS),
                  pl.BlockSpec((BLK, 2 * PAIR), lambda n: (jnp.maximum(SUB * n - 1, 0), O_K // (2 * PAIR))),
                  pl.BlockSpec((2, BLK, 2 * BLK), lambda n: (0, 0, 0)),
                  small((3, CONV_CH)), small((1, PAIR)), small((1, PAIR)), small((1, CONV_CH)), small((1, ATTN_W))]
        + ([rows(D_MODEL), small((D_MODEL, D_MODEL))] if project else []),
        out_specs=[rows(D_MODEL), rows(CONV_CH), rows(ATTN_W)] + ([rows(D_MODEL)] if project else []),
        out_shape=[jax.ShapeDtypeStruct((S, D_MODEL), BF16), jax.ShapeDtypeStruct((S, CONV_CH), BF16),
                   jax.ShapeDtypeStruct((S, ATTN_W), BF16)]
        + ([jax.ShapeDtypeStruct((S, D_MODEL), F32)] if project else []),
        scratch_shapes=[pltpu.VMEM((8, CONV_CH), F32)],
        compiler_params=_cparams(("arbitrary",)),
    )(sinks, proj, proj, bias, conv_w, qg, kg, cog, aog, *([residual, w_o] if project else []))


def matmul_residual(a, w, res, name, dep=None):
    S, K = a.shape
    N = w.shape[1]

    def body(a_ref, w_ref, r_ref, o_ref):
        o_ref[...] = r_ref[...] + _dot(a_ref[...], w_ref[...])

    return _pallas_after(
        dep, body, a, w, res, name=name, grid=(S // TM,),
        in_specs=[pl.BlockSpec((TM, K), lambda i: (i, 0)), pl.BlockSpec((K, N), lambda i: (0, 0)),
                  pl.BlockSpec((TM, N), lambda i: (i, 0))],
        out_specs=pl.BlockSpec((TM, N), lambda i: (i, 0)),
        out_shape=jax.ShapeDtypeStruct((S, N), F32),
        compiler_params=_cparams(("parallel",)),
    )


TF = 1408


def ffn_fwd(x1, g, wg, wu, wd, name, dep=None, tgt=None):
    S = x1.shape[0]
    ni, nj = S // TM, D_FF // TF
    with_loss = tgt is not None

    def body(x_ref, g_ref, wg_ref, wu_ref, wd_ref, *rest):
        if with_loss:
            t_ref, h2_ref, a_ref, b_ref, f_ref, o_ref, l_ref, acc, sq = rest
        else:
            h2_ref, a_ref, b_ref, f_ref, o_ref, acc = rest
        i, j = pl.program_id(0), pl.program_id(1)

        @pl.when(j == 0)
        def _():
            xf = x_ref[...]
            h2_ref[...] = ((xf * _rms(xf)) * g_ref[...]).astype(BF16)
            acc[...] = xf

        h2 = h2_ref[...]
        a = _dot_nt(h2, wg_ref[...])
        b = _dot_nt(h2, wu_ref[...])
        a_ref[...] = a.astype(BF16)
        b_ref[...] = b.astype(BF16)
        f = ((a * _sigmoid(a)) * b).astype(BF16)
        f_ref[...] = f
        acc[...] += _dot(f, wd_ref[...])

        @pl.when(j == nj - 1)
        def _():
            if not with_loss:
                o_ref[...] = acc[...]
                return
            e = acc[...] - t_ref[...]
            o_ref[...] = e * (1.0 / D_MODEL)
            col = jnp.sum(e * e, axis=0, keepdims=True)

            @pl.when(i == 0)
            def _():
                sq[...] = col

            @pl.when(i > 0)
            def _():
                sq[...] += col

            @pl.when(i == ni - 1)
            def _():
                l_ref[...] = jnp.full((1, 128), jnp.sum(sq[...]) * (0.5 / D_MODEL), F32)

    row = pl.BlockSpec((TM, D_MODEL), lambda i, j: (i, 0))
    tile = pl.BlockSpec((TM, TF), lambda i, j: (i, j))
    return _pallas_after(
        dep, body, x1, g, wg, wu, wd, *([tgt] if with_loss else []), name=name, grid=(ni, nj),
        in_specs=[row, pl.BlockSpec((1, D_MODEL), lambda i, j: (0, 0))]
        + [pl.BlockSpec((TF, D_MODEL), lambda i, j: (j, 0))] * 3 + ([row] if with_loss else []),
        out_specs=[row, tile, tile, tile, row] + ([pl.BlockSpec((1, 128), lambda i, j: (0, 0))] if with_loss else []),
        out_shape=[jax.ShapeDtypeStruct((S, D_MODEL), BF16)] + [jax.ShapeDtypeStruct((S, D_FF), BF16)] * 3
        + [jax.ShapeDtypeStruct((S, D_MODEL), F32)]
        + ([jax.ShapeDtypeStruct((1, 128), F32)] if with_loss else []),
        scratch_shapes=[pltpu.VMEM((TM, D_MODEL), F32)] + ([pltpu.VMEM((1, D_MODEL), F32)] if with_loss else []),
        compiler_params=_cparams(("arbitrary", "arbitrary") if with_loss else ("parallel", "arbitrary")),
    )


def ffn_bwd(dx2, a, b, x1, g, wg, wu, wd, name, dep=None):
    S = dx2.shape[0]
    nj = D_FF // TF

    def body(dx_ref, a_ref, b_ref, x_ref, g_ref, wg_ref, wu_ref, wd_ref, da_ref, db_ref, dx1_ref, dg_ref, dxb, acc):
        i, j = pl.program_id(0), pl.program_id(1)

        @pl.when((i == 0) & (j == 0))
        def _():
            dg_ref[...] = jnp.zeros_like(dg_ref)

        @pl.when(j == 0)
        def _():
            dxb[...] = dx_ref[...].astype(BF16)
            acc[...] = jnp.zeros_like(acc)

        df = _dot_nt(dxb[...], wd_ref[...])
        av = a_ref[...].astype(F32)
        bv = b_ref[...].astype(F32)
        sg = _sigmoid(av)
        da = ((df * bv) * (sg * (1.0 + av * (1.0 - sg)))).astype(BF16)
        db = (df * (av * sg)).astype(BF16)
        da_ref[...] = da
        db_ref[...] = db
        acc[...] += _dot(da, wg_ref[...]) + _dot(db, wu_ref[...])

        @pl.when(j == nj - 1)
        def _():
            xf = x_ref[...]
            r = _rms(xf)
            xh = xf * r
            dh = acc[...]
            dg_ref[...] += jnp.sum(dh * xh, axis=0, keepdims=True)
            dx1_ref[...] = dx_ref[...] + _rms_bwd(dh, xh, r, g_ref[...])

    return _pallas_after(
        dep, body, dx2, a, b, x1, g, wg, wu, wd, name=name, grid=(S // TM, nj),
        in_specs=[pl.BlockSpec((TM, D_MODEL), lambda i, j: (i, 0)),
                  pl.BlockSpec((TM, TF), lambda i, j: (i, j)), pl.BlockSpec((TM, TF), lambda i, j: (i, j)),
                  pl.BlockSpec((TM, D_MODEL), lambda i, j: (i, 0)), pl.BlockSpec((1, D_MODEL), lambda i, j: (0, 0))]
        + [pl.BlockSpec((TF, D_MODEL), lambda i, j: (j, 0))] * 3,
        out_specs=[pl.BlockSpec((TM, TF), lambda i, j: (i, j)), pl.BlockSpec((TM, TF), lambda i, j: (i, j)),
                   pl.BlockSpec((TM, D_MODEL), lambda i, j: (i, 0)), pl.BlockSpec((1, D_MODEL), lambda i, j: (0, 0))],
        out_shape=[jax.ShapeDtypeStruct((S, D_FF), BF16), jax.ShapeDtypeStruct((S, D_FF), BF16),
                   jax.ShapeDtypeStruct((S, D_MODEL), F32), jax.ShapeDtypeStruct((1, D_MODEL), F32)],
        scratch_shapes=[pltpu.VMEM((TM, D_MODEL), BF16), pltpu.VMEM((TM, D_MODEL), F32)],
        compiler_params=_cparams(("arbitrary", "arbitrary")),
    )


def grad_weight(lhs, rhs, tm, name, dep=None):
    S, N = rhs.shape
    M = lhs[0].shape[1]
    tk = 1024
    nk = S // tk
    n_l = n_o = len(lhs)

    def body(*refs):
        l_refs, r_ref = refs[:n_l], refs[n_l]
        o_refs, accs = refs[n_l + 1:n_l + 1 + n_o], refs[n_l + 1 + n_o:]
        k = pl.program_id(1)
        rv = r_ref[...].astype(BF16)
        for l_ref, o_ref, acc in zip(l_refs, o_refs, accs):
            @pl.when(k == 0)
            def _():
                acc[...] = jnp.zeros_like(acc)

            acc[...] += _dot_tn(l_ref[...], rv)

            @pl.when(k == nk - 1)
            def _():
                o_ref[...] = acc[...].astype(BF16)

    return _pallas_after(
        dep, body, *lhs, rhs, name=name, grid=(M // tm, nk),
        in_specs=[pl.BlockSpec((tk, tm), lambda i, k: (k, i)) for _ in lhs] + [pl.BlockSpec((tk, N), lambda i, k: (k, 0))],
        out_specs=[pl.BlockSpec((tm, N), lambda i, k: (i, 0))] * n_o,
        out_shape=[jax.ShapeDtypeStruct((M, N), BF16)] * n_o,
        scratch_shapes=[pltpu.VMEM((tm, N), F32)] * n_o,
        compiler_params=_cparams(("parallel", "arbitrary")),
    )


def proj_bwd(dproj, w, x, g, dres, name, dep=None):
    S, N = dproj.shape

    def body(dp_ref, w_ref, x_ref, g_ref, dr_ref, dx_ref, dg_ref):
        i = pl.program_id(0)

        @pl.when(i == 0)
        def _():
            dg_ref[...] = jnp.zeros_like(dg_ref)

        dh = _dot(dp_ref[...], w_ref[...])
        xf = x_ref[...]
        r = _rms(xf)
        xh = xf * r
        dg_ref[...] += jnp.sum(dh * xh, axis=0, keepdims=True)
        dx_ref[...] = dr_ref[...] + _rms_bwd(dh, xh, r, g_ref[...])

    return _pallas_after(
        dep, body, dproj, w, x, g, dres, name=name, grid=(S // TM,),
        in_specs=[pl.BlockSpec((TM, N), lambda i: (i, 0)), pl.BlockSpec((N, D_MODEL), lambda i: (0, 0)),
                  pl.BlockSpec((TM, D_MODEL), lambda i: (i, 0)), pl.BlockSpec((1, D_MODEL), lambda i: (0, 0)),
                  pl.BlockSpec((TM, D_MODEL), lambda i: (i, 0))],
        out_specs=[pl.BlockSpec((TM, D_MODEL), lambda i: (i, 0)), pl.BlockSpec((1, D_MODEL), lambda i: (0, 0))],
        out_shape=[jax.ShapeDtypeStruct((S, D_MODEL), F32), jax.ShapeDtypeStruct((1, D_MODEL), F32)],
        compiler_params=_cparams(("arbitrary",)),
    )


def mixer_bwd(proj, bias, y, ao, dx1, w_o, sinks, conv_w, qg, kg, cog, aog, name, dep=None):
    S = proj.shape[0]
    steps = S // ROWS
    KV_W = 2 * PAIR

    def body(sinks_ref, p_ref, kvp_ref, bias_ref, y_ref, ao_ref, dx_first_ref, dx_next_ref, wo_ref, cw_ref, qg_ref,
             kg_ref, cog_ref, aog_ref, dp_ref, dcw_ref, dqg_ref, dkg_ref, dsk_ref, dcog_ref, daog_ref,
             dycar, kcar, vcar, dmix_scr):
        step = pl.program_id(0)

        @pl.when(step == 0)
        def _():
            dmix_scr[...] = _dot_nt(dx_first_ref[...].astype(BF16), wo_ref[...])
            dycar[...] = jnp.zeros_like(dycar)
            kcar[...] = jnp.zeros_like(kcar)
            vcar[...] = jnp.zeros_like(vcar)
            dcw_ref[...] = jnp.zeros_like(dcw_ref)
            dqg_ref[...] = jnp.zeros_like(dqg_ref)
            dkg_ref[...] = jnp.zeros_like(dkg_ref)
            dsk_ref[...] = jnp.zeros_like(dsk_ref)
            dcog_ref[...] = jnp.zeros_like(dcog_ref)
            daog_ref[...] = jnp.zeros_like(daog_ref)

        dma = dmix_scr[:, CONV_CH:]

        aov = ao_ref[...].astype(F32)
        ra = _rms(aov)
        ah = aov * ra
        daog_ref[...] += jnp.sum(dma * ah, axis=0, keepdims=True)
        dao = _rms_bwd(dma, ah, ra, aog_ref[...])

        low = _low_half()
        high = jnp.logical_not(low)
        q_gain = qg_ref[...] * SCALE
        k, v = _kv_pairs(p_ref, kvp_ref)
        rk = _pair_rms(k, low)
        kh = k * rk
        k_of = _one_head_in_both_halves((kh * kg_ref[...]).astype(BF16), low)
        v_of = _one_head_in_both_halves(v, low)
        lane8 = lax.broadcasted_iota(jnp.int32, (1, N_Q), 1)
        dsk = jnp.zeros((1, N_Q), F32)
        dqg = jnp.zeros((1, PAIR), F32)
        biases = [bias_ref[jnp.minimum(steps - 1 - step, 1)]] + [bias_ref[1]] * (SUB - 1)
        units = [(s, hq) for s in range(SUB) for hq in range(N_Q)]
        rq, qh, q_one, do_one, delta = [], [], [], [], []
        for pair in range(N_Q // 2):
            cols = slice(PAIR * pair, PAIR * (pair + 1))
            q = p_ref[:, O_Q + PAIR * pair:O_Q + PAIR * (pair + 1)].astype(F32)
            rq.append(_pair_rms(q, low))
            qh.append(q * rq[pair])
            qn = (qh[pair] * q_gain).astype(BF16)
            do = dao[:, cols]
            do_b = do.astype(BF16)
            delta += _half_sums(do * aov[:, cols], low)
            for mine in (low, high):
                q_one.append(jnp.where(mine, qn, jnp.zeros_like(qn)))
                do_one.append(jnp.where(mine, do_b, jnp.zeros_like(do_b)))
        scores = {u: _dot_nt(q_one[u[1]][_block(u[0])], k_of[u[1] // GRP][_band(u[0])]) + biases[u[0]] for u in units}
        dprobs = {u: _dot_nt(do_one[u[1]][_block(u[0])], v_of[u[1] // GRP][_band(u[0])]) for u in units}
        probs = {u: _softmax_with_sink(scores[u], sinks_ref[u[1]]) for u in units}
        ds = {u: (probs[u][0] * (dprobs[u] - delta[u[1]][_block(u[0])])).astype(BF16) for u in units}
        dv_t = {(s, h): jnp.zeros((PAIR, 2 * BLK), F32) for s in range(SUB) for h in range(N_KV)}
        dkn_t = dict(dv_t)
        for s, hq in units:
            dsk = dsk - jnp.where(lane8 == hq, jnp.sum(probs[s, hq][1] * delta[hq][_block(s)]), 0.0)
            dv_t[s, hq // GRP] = dv_t[s, hq // GRP] + _dot_tn(do_one[hq][_block(s)], probs[s, hq][0].astype(BF16))
            dkn_t[s, hq // GRP] = dkn_t[s, hq // GRP] + _dot_tn(q_one[hq][_block(s)], ds[s, hq])
        dqn_of = {u: _dot(ds[u], k_of[u[1] // GRP][_band(u[0])]) for u in units}
        for pair in range(N_Q // 2):
            dqn = jnp.concatenate([jnp.where(low, dqn_of[s, 2 * pair], dqn_of[s, 2 * pair + 1]) for s in range(SUB)],
                                  axis=0)
            dqg = dqg + jnp.sum(dqn * qh[pair], axis=0, keepdims=True)
            dqh = dqn * q_gain
            dp_ref[:, O_Q + PAIR * pair:O_Q + PAIR * (pair + 1)] = (
                rq[pair] * (dqh - qh[pair] * _pair_mean(dqh * qh[pair], low))).astype(BF16)

        def over_key_rows(parts):
            bands = [jnp.concatenate([parts[s, h][:HEAD] + parts[s, h][HEAD:] for h in range(N_KV)], axis=0).T
                     for s in range(SUB)]
            pieces = [bands[0][:BLK]]
            pieces += [bands[s - 1][BLK:] + bands[s][:BLK] for s in range(1, SUB)]
            pieces.append(bands[SUB - 1][BLK:])
            return jnp.concatenate(pieces, axis=0)

        dv = over_key_rows(dv_t)
        dkn = over_key_rows(dkn_t)
        dkg_ref[...] += jnp.sum(dkn * kh, axis=0, keepdims=True)
        dkh = dkn * kg_ref[...]
        dk = rk * (dkh - kh * _pair_mean(dkh * kh, low))
        last = slice(ROWS, ROWS + BLK)
        dp_ref[:, O_K:O_V] = jnp.concatenate([dk[BLK:ROWS], dk[last] + kcar[...]], axis=0).astype(BF16)
        dp_ref[:, O_V:] = jnp.concatenate([dv[BLK:ROWS], dv[last] + vcar[...]], axis=0).astype(BF16)
        kcar[...] = dk[:BLK, :]
        vcar[...] = dv[:BLK, :]
        dsk_ref[...] += dsk
        dqg_ref[...] += dqg * SCALE

        bg = p_ref[:, 0:CONV_CH].astype(F32)
        cg = p_ref[:, CONV_CH:2 * CONV_CH].astype(F32)
        hc = p_ref[:, 2 * CONV_CH:3 * CONV_CH].astype(F32)
        yv = y_ref[...].astype(F32)
        dmc = dmix_scr[:, 0:CONV_CH]
        co = bg * yv
        rc = _rms(co)
        ch = co * rc
        dcog_ref[...] += jnp.sum(dmc * ch, axis=0, keepdims=True)
        dco = _rms_bwd(dmc, ch, rc, cog_ref[...])
        dp_ref[:, 0:CONV_CH] = (dco * yv).astype(BF16)
        dy = dco * bg
        row = lax.broadcasted_iota(jnp.int32, (ROWS, CONV_CH), 0)
        nxt = dycar[...]
        dy1 = jnp.where(row == ROWS - 1, nxt[0:1, :], pltpu.roll(dy, ROWS - 1, 0))
        dy2 = jnp.where(row == ROWS - 2, nxt[0:1, :],
                        jnp.where(row == ROWS - 1, nxt[1:2, :], pltpu.roll(dy, ROWS - 2, 0)))
        dycar[...] = dy[0:8, :]
        du = cw_ref[2:3, :] * dy + cw_ref[1:2, :] * dy1 + cw_ref[0:1, :] * dy2
        dp_ref[:, CONV_CH:2 * CONV_CH] = (du * hc).astype(BF16)
        dp_ref[:, 2 * CONV_CH:3 * CONV_CH] = (du * cg).astype(BF16)
        u = cg * hc
        dcw_ref[0:1, :] += jnp.sum(dy2 * u, axis=0, keepdims=True)
        dcw_ref[1:2, :] += jnp.sum(dy1 * u, axis=0, keepdims=True)
        dcw_ref[2:3, :] += jnp.sum(dy * u, axis=0, keepdims=True)

        dmix_scr[...] = _dot_nt(dx_next_ref[...].astype(BF16), wo_ref[...])

    small = lambda shape: pl.BlockSpec(shape, lambda s: (0, 0))
    blk = lambda w: pl.BlockSpec((ROWS, w), lambda s: (steps - 1 - s, 0))
    return _pallas_after(
        dep, body, sinks, proj, proj, bias, y, ao, dx1, dx1, w_o, conv_w, qg, kg, cog, aog, name=name, grid=(steps,),
        in_specs=[pl.BlockSpec(memory_space=pltpu.SMEM),
                  blk(IN_COLS),
                  pl.BlockSpec((BLK, KV_W), lambda s: (jnp.maximum(SUB * (steps - 1 - s) - 1, 0), O_K // KV_W)),
                  pl.BlockSpec((2, BLK, 2 * BLK), lambda s: (0, 0, 0)),
                  blk(CONV_CH), blk(ATTN_W),
                  pl.BlockSpec((ROWS, D_MODEL), lambda s: (steps - 1, 0)),
                  pl.BlockSpec((ROWS, D_MODEL), lambda s: (jnp.maximum(steps - 2 - s, 0), 0)),
                  small((D_MODEL, D_MODEL)),
                  small((3, CONV_CH)), small((1, PAIR)), small((1, PAIR)), small((1, CONV_CH)), small((1, ATTN_W))],
        out_specs=[blk(IN_COLS), small((3, CONV_CH)), small((1, PAIR)), small((1, PAIR)), small((1, N_Q)),
                   small((1, CONV_CH)), small((1, ATTN_W))],
        out_shape=[jax.ShapeDtypeStruct((S, IN_COLS), BF16), jax.ShapeDtypeStruct((3, CONV_CH), F32),
                   jax.ShapeDtypeStruct((1, PAIR), F32), jax.ShapeDtypeStruct((1, PAIR), F32),
                   jax.ShapeDtypeStruct((1, N_Q), F32), jax.ShapeDtypeStruct((1, CONV_CH), F32),
                   jax.ShapeDtypeStruct((1, ATTN_W), F32)],
        scratch_shapes=[pltpu.VMEM((8, CONV_CH), F32), pltpu.VMEM((BLK, PAIR), F32), pltpu.VMEM((BLK, PAIR), F32),
                        pltpu.VMEM((ROWS, D_MODEL), F32)],
        compiler_params=_cparams(("arbitrary",)),
    )


OTHER_CHIPS = ((1, 0), (0, 1), (1, 1))


def _position():
    return lax.axis_index("x"), lax.axis_index("y"), lax.axis_index("c")


def all_gather(shards, name):
    n = len(shards)

    def body(*refs):
        x_refs, out_refs = refs[:n], refs[n:2 * n]
        send_sems, recv_sems, local_sems = refs[2 * n:]
        x, y, c = _position()
        me, sibling = (x, y, c), (x, y, 1 - c)
        chips = [(x ^ mx, y ^ my) for mx, my in OTHER_CHIPS]

        def slab(a, px, py, pc):
            return out_refs[a].at[4 * px + 2 * py + pc]

        def copy(a, k, block, to, src=None):
            return pltpu.make_async_remote_copy(
                src_ref=slab(a, *block) if src is None else src, dst_ref=slab(a, *block),
                send_sem=send_sems.at[7 * a + k], recv_sem=recv_sems.at[7 * a + k], device_id=to, device_id_type=MESH_T)

        mine = [pltpu.make_async_copy(x_refs[a], slab(a, *me), local_sems.at[a]) for a in range(n)]
        for cp in mine:
            cp.start()
        first = []
        for a in range(n):
            first.append(copy(a, 0, me, sibling, src=x_refs[a]))
            first += [copy(a, 1 + j, me, (*chip, c), src=x_refs[a]) for j, chip in enumerate(chips)]
        for cp in first:
            cp.start()
        passed = []
        for a in range(n):
            for j, chip in enumerate(chips):
                copy(a, 1 + j, (*chip, c), me).wait_recv()
                passed.append(copy(a, 4 + j, (*chip, c), sibling))
                passed[-1].start()
        for a in range(n):
            copy(a, 0, sibling, me).wait_recv()
            for j, chip in enumerate(chips):
                copy(a, 4 + j, (*chip, 1 - c), me).wait_recv()
        for cp in first + passed:
            cp.wait_send()
        for cp in mine:
            cp.wait()

    return pl.pallas_call(
        body, name=name, in_specs=[ANY] * n, out_specs=[ANY] * n,
        out_shape=[jax.ShapeDtypeStruct((N_DEV, *s.shape), s.dtype) for s in shards],
        scratch_shapes=[pltpu.SemaphoreType.DMA((7 * n,)), pltpu.SemaphoreType.DMA((7 * n,)),
                        pltpu.SemaphoreType.DMA((n,))],
    )(*shards)


class SplitCopy:
    def __init__(self, name, arrays, n_copies, plan, after=None):
        n = len(arrays)
        self.name, self.n, self.n_copies, self.plan = name, n, n_copies, plan
        extra = [] if after is None else [after]

        def body(*refs):
            in_refs = refs[:n]
            send_sems, recv_sems = refs[n + len(extra)], refs[n + len(extra) + 1]
            token = refs[2 * n + len(extra) + 2]
            for k, (src, dst, to) in enumerate(plan(_position(), in_refs)):
                pltpu.make_async_remote_copy(src_ref=src, dst_ref=dst, send_sem=send_sems.at[k],
                                             recv_sem=recv_sems.at[k], device_id=to, device_id_type=MESH_T).start()
            token[...] = jnp.zeros_like(token)

        outs = pl.pallas_call(
            body, name=name + "_start",
            out_shape=(pltpu.SemaphoreType.DMA((n_copies,)), pltpu.SemaphoreType.DMA((n_copies,)),
                       *[pltpu.HBM(a.shape, a.dtype) for a in arrays], jax.ShapeDtypeStruct((8, 128), F32)),
            in_specs=[HBM] * n + [ANY] * len(extra),
            out_specs=(SEM, SEM, *[HBM] * n, pl.BlockSpec(memory_space=pltpu.VMEM)),
            input_output_aliases={i: 2 + i for i in range(n)},
            compiler_params=pltpu.CompilerParams(has_side_effects=pltpu.SideEffectType.DATAFLOW_SIDE_EFFECTING),
        )(*[pltpu.with_memory_space_constraint(a, pltpu.HBM) for a in arrays], *extra)
        self.send_sems, self.recv_sems = outs[0], outs[1]
        self.arrays, self.token = list(outs[2:2 + n]), outs[2 + n]

    def wait(self, after):
        n, plan = self.n, self.plan

        def body(*refs):
            in_refs, send_sems, recv_sems = refs[:n], refs[n], refs[n + 1]
            for k, (src, dst, to) in enumerate(plan(_position(), in_refs)):
                cp = pltpu.make_async_remote_copy(src_ref=src, dst_ref=dst, send_sem=send_sems.at[k],
                                                  recv_sem=recv_sems.at[k], device_id=to, device_id_type=MESH_T)
                cp.wait_send()
                cp.wait_recv()

        outs = pl.pallas_call(
            body, name=self.name + "_wait",
            out_shape=tuple(pltpu.HBM(a.shape, a.dtype) for a in self.arrays),
            in_specs=[HBM] * n + [SEM, SEM, ANY], out_specs=tuple([HBM] * n),
            input_output_aliases={i: i for i in range(n)},
            compiler_params=pltpu.CompilerParams(has_side_effects=pltpu.SideEffectType.DATAFLOW_SIDE_EFFECTING),
        )(*self.arrays, self.send_sems, self.recv_sems, after)
        return list(outs)


def gather_start(shards, me, name, after=None):
    n = len(shards)
    lands = [lax.dynamic_update_slice(lax.empty((N_DEV, *s.shape), s.dtype), s[None], (me, 0, 0)) for s in shards]

    def plan(pos, refs):
        x, y, c = pos
        return [(refs[a], refs[n + a].at[4 * x + 2 * y + c], (x ^ mx, y ^ my, c))
                for a in range(n) for mx, my in OTHER_CHIPS]

    return SplitCopy(name, list(shards) + lands, 3 * n, plan, after=after)


def sibling_start(lands, name):
    n = len(lands)

    def plan(pos, refs):
        x, y, c = pos
        return [(refs[a].at[2 * q + c], refs[a].at[2 * q + c], (x, y, 1 - c)) for a in range(n) for q in range(4)]

    return SplitCopy(name, list(lands), 4 * n, plan)


def scatter_start(slabs, name):
    n = len(slabs)
    lands = [lax.empty((N_DEV - 1, *g.shape[1:]), g.dtype) for g in slabs]

    def plan(pos, refs):
        x, y, c = pos
        copies = []
        for a in range(n):
            for r in range(1, N_DEV):
                px, py, pc = x ^ ((r >> 2) & 1), y ^ ((r >> 1) & 1), c ^ (r & 1)
                copies.append((refs[a].at[4 * px + 2 * py + pc], refs[n + a].at[r - 1], (px, py, pc)))
        return copies

    return SplitCopy(name, list(slabs) + lands, (N_DEV - 1) * n, plan)


def all_reduce_small(v, name, dep=None):
    R, W = v.shape

    def body(v_ref, o_ref, recv, send_sems, recv_sems):
        x, y, c = _position()
        me = 4 * x + 2 * y + c
        copies = []
        for r in range(1, N_DEV):
            to = (x ^ ((r >> 2) & 1), y ^ ((r >> 1) & 1), c ^ (r & 1))
            copies.append(pltpu.make_async_remote_copy(
                src_ref=v_ref, dst_ref=recv.at[me], send_sem=send_sems.at[r - 1], recv_sem=recv_sems.at[r - 1],
                device_id=to, device_id_type=MESH_T))
        for cp in copies:
            cp.start()
        recv[pl.ds(me, 1)] = v_ref[...][None]
        for cp in copies:
            cp.wait()
        acc = recv[0]
        for s in range(1, N_DEV):
            acc = acc + recv[s]
        o_ref[...] = acc

    return _pallas_after(
        dep, body, v, name=name,
        in_specs=[pl.BlockSpec(memory_space=pltpu.VMEM)], out_specs=pl.BlockSpec(memory_space=pltpu.VMEM),
        out_shape=jax.ShapeDtypeStruct((R, W), F32),
        scratch_shapes=[pltpu.VMEM((N_DEV, R, W), F32), pltpu.SemaphoreType.DMA((N_DEV - 1,)),
                        pltpu.SemaphoreType.DMA((N_DEV - 1,))],
    )


def _row_tile(rows):
    if rows <= 512:
        return rows
    return max(t for t in range(8, 513, 8) if rows % t == 0)


def _adamw_update(w, g, m, v):
    mn = ADAM_B1 * m + (1.0 - ADAM_B1) * g
    vn = ADAM_B2 * v + (1.0 - ADAM_B2) * (g * g)
    m_hat = mn / (1.0 - ADAM_B1 ** ADAM_STEP)
    v_hat = vn / (1.0 - ADAM_B2 ** ADAM_STEP)
    return -ADAM_LR * (m_hat / (jnp.sqrt(v_hat) + ADAM_EPS) + ADAM_WD * w), mn, vn


def adamw(w, g, m, v, name):
    R, W = w.shape
    tr = _row_tile(R)

    def body(w_ref, g_ref, m_ref, v_ref, d_ref, mo_ref, vo_ref):
        d_ref[...], mo_ref[...], vo_ref[...] = _adamw_update(w_ref[...], g_ref[...], m_ref[...], v_ref[...])

    spec = pl.BlockSpec((tr, W), lambda i: (i, 0))
    return pl.pallas_call(
        body, name=name, grid=(R // tr,), in_specs=[spec] * 4, out_specs=[spec] * 3,
        out_shape=[jax.ShapeDtypeStruct((R, W), F32)] * 3,
        compiler_params=_cparams(("parallel",)),
    )(w, g, m, v)


def reduce_adamw(slabs, land, w, m, v, layer, me_arr, name, others=None):
    _, R, W = slabs.shape
    tr = _row_tile(R)
    n_other = 0 if others is None else 4

    def body(me_ref, s_ref, l_ref, w_ref, m_ref, v_ref, *rest):
        g_ref, d_ref, mo_ref, vo_ref = rest[n_other:]
        g = s_ref[0].astype(F32)
        for r in range(N_DEV - 1):
            g = g + l_ref[r].astype(F32)
        g_ref[0] = g
        d_ref[0], mo_ref[0], vo_ref[0] = _adamw_update(w_ref[0], g, m_ref[0], v_ref[0])

    spec = pl.BlockSpec((1, tr, W), lambda i, me: (layer, i, 0))
    return pl.pallas_call(
        body, name=name,
        grid_spec=pltpu.PrefetchScalarGridSpec(
            num_scalar_prefetch=1, grid=(R // tr,),
            in_specs=[pl.BlockSpec((1, tr, W), lambda i, me: (me[0], i, 0)),
                      pl.BlockSpec((N_DEV - 1, tr, W), lambda i, me: (0, i, 0)), spec, spec, spec] + [ANY] * n_other,
            out_specs=[spec] * 4),
        out_shape=[jax.ShapeDtypeStruct((DEPTH, R, W), F32)] * 4,
        input_output_aliases={6 + i: i for i in range(n_other)},
        compiler_params=_cparams(("parallel",)),
    )(me_arr, slabs, land, w, m, v, *([] if others is None else others))


SMALL_NAMES = ("norm1_g", "q_norm_g", "k_norm_g", "sinks", "conv_out_g", "attn_out_g", "norm2_g", "conv_w")
SMALL_SIZES = (D_MODEL, HEAD, HEAD, N_Q, CONV_CH, ATTN_W, D_MODEL, 3 * CONV_CH)
SMALL_ROWS = 80


def kernel(x, norm1_g, w_in, conv_w, q_norm_g, k_norm_g, sinks, conv_out_g, attn_out_g, w_o, norm2_g, w_gate, w_up, w_down, loss_target, m_norm1_g, m_w_in, m_conv_w, m_q_norm_g, m_k_norm_g, m_sinks, m_conv_out_g, m_attn_out_g, m_w_o, m_norm2_g, m_w_gate, m_w_up, m_w_down, v_norm1_g, v_w_in, v_conv_w, v_q_norm_g, v_k_norm_g, v_sinks, v_conv_out_g, v_attn_out_g, v_w_o, v_norm2_g, v_w_gate, v_w_up, v_w_down):
    xi, yi, ci = _position()
    me = 4 * xi + 2 * yi + ci
    me_arr = jnp.reshape(me, (1,)).astype(jnp.int32)
    xs, tgt = x[0], loss_target[0]
    bf = lambda a: a.astype(BF16)
    bias = band_bias()

    t = lambda a: jnp.swapaxes(a, 1, 2)
    shard = dict(w_in=(t(w_in), t(m_w_in), t(v_w_in)), w_o=(w_o, m_w_o, v_w_o),
                 w_gate=(t(w_gate), t(m_w_gate), t(v_w_gate)), w_up=(t(w_up), t(m_w_up), t(v_w_up)),
                 w_down=(w_down, m_w_down, v_w_down))
    wb = {n: bf(shard[n][0]) for n in shard}

    g_in0, g_o0, g_conv = all_gather([wb["w_in"][0], wb["w_o"][0], conv_w.reshape(DEPTH * 3, HEAD)], "gather_first")
    ag_ffn0 = gather_start([wb["w_gate"][0], wb["w_up"][0], wb["w_down"][0]], me, "gather_ffn0", after=g_in0)
    ag_mix1 = gather_start([wb["w_in"][1], wb["w_o"][1]], me, "gather_mix1", after=ag_ffn0.token)
    ag_ffn1 = gather_start([wb["w_gate"][1], wb["w_up"][1], wb["w_down"][1]], me, "gather_ffn1", after=ag_mix1.token)
    conv_full = g_conv.reshape(N_DEV, DEPTH, 3, HEAD).transpose(1, 2, 0, 3).reshape(DEPTH, 3, CONV_CH)
    pair_gain = lambda g: jnp.tile(g[None], (1, 2))
    small = [dict(norm1_g=norm1_g[l][None], conv_w=conv_full[l], q_norm_g=pair_gain(q_norm_g[l]),
                  k_norm_g=pair_gain(k_norm_g[l]), sinks=sinks[l], conv_out_g=conv_out_g[l][None],
                  attn_out_g=attn_out_g[l][None], norm2_g=norm2_g[l][None]) for l in range(DEPTH)]
    whole = lambda g: g.reshape(-1, D_MODEL)
    weights = [dict(w_in=whole(g_in0), w_o=whole(g_o0)), {}]

    def ffn_weights(g_gate, g_up, g_down):
        return dict(w_gate=whole(g_gate), w_up=whole(g_up), w_down=whole(g_down))

    saved = []
    xl = xs
    for l in range(DEPTH):
        sp, wl = small[l], weights[l]
        h, proj = norm_proj(xl, sp["norm1_g"], wl["w_in"], f"norm_proj{l}",
                            dep=ag_ffn1.token if l == 0 else pass_ffn1.token)
        mixer_args = (proj, bias, sp["sinks"], sp["conv_w"], sp["q_norm_g"], sp["k_norm_g"], sp["conv_out_g"],
                      sp["attn_out_g"], f"mixer_fwd{l}")
        if l == 0:
            mix, y, ao = mixer_fwd(*mixer_args)
            pass_ffn0 = sibling_start(ag_ffn0.wait(mix)[3:], "pass_ffn0")
            x1 = matmul_residual(mix, wl["w_o"], xl, "out_proj0", dep=pass_ffn0.token)
            wl.update(ffn_weights(*pass_ffn0.wait(x1)))
            pass_mix1 = sibling_start(ag_mix1.wait(x1)[2:], "pass_mix1")
            h2, a, b, f, x2 = ffn_fwd(x1, sp["norm2_g"], wl["w_gate"], wl["w_up"], wl["w_down"], "ffn_fwd0",
                                      dep=pass_mix1.token)
            g_in, g_o = pass_mix1.wait(x2)
            weights[1] = dict(w_in=whole(g_in), w_o=whole(g_o))
            pass_ffn1 = sibling_start(ag_ffn1.wait(x2)[3:], "pass_ffn1")
        else:
            mix, y, ao, x1 = mixer_fwd(*mixer_args, residual=xl, w_o=wl["w_o"])
            wl.update(ffn_weights(*pass_ffn1.wait(x1)))
            h2, a, b, f, dx, loss_row = ffn_fwd(x1, sp["norm2_g"], wl["w_gate"], wl["w_up"], wl["w_down"], "ffn_fwd1",
                                                tgt=tgt)
            x2 = None
        saved.append((xl, h, proj, mix, y, ao, x1, h2, a, b, f))
        xl = x2

    stepped = {n: None for n in shard}
    slabs = lambda d: d.reshape(N_DEV, -1, D_MODEL)
    gsmall = [None] * DEPTH

    def finish(sc, names, after, l):
        arrays = sc.wait(after)
        k = len(names)
        for i, n in enumerate(names):
            w, m, v = shard[n]
            stepped[n] = reduce_adamw(arrays[i], arrays[k + i], w, m, v, l, me_arr, f"reduce_adamw_{n}{l}",
                                      others=stepped[n])

    for l in reversed(range(DEPTH)):
        sp, wl = small[l], weights[l]
        x0, h, proj, mix, y, ao, x1, h2, a, b, f = saved[l]
        (d_wd,) = grad_weight((f,), dx, TF, f"grad_w_down{l}")
        sc_down = scatter_start([slabs(d_wd)], f"scatter_w_down{l}")
        da, db, dx1, d_g2 = ffn_bwd(dx, a, b, x1, sp["norm2_g"], wl["w_gate"], wl["w_up"], wl["w_down"],
                                    f"ffn_bwd{l}", dep=sc_down.token)
        d_wg, d_wu = grad_weight((da, db), h2, TF, f"grad_w_gate_up{l}")
        (d_wo,) = grad_weight((mix,), dx1, D_MODEL, f"grad_w_o{l}")
        sc_rest = scatter_start([slabs(d_wg), slabs(d_wu), slabs(d_wo)], f"scatter_w_gate_up_o{l}")
        dproj, d_cw, d_qg, d_kg, d_sk, d_cog, d_aog = mixer_bwd(
            proj, bias, y, ao, dx1, wl["w_o"], sp["sinks"], sp["conv_w"], sp["q_norm_g"], sp["k_norm_g"],
            sp["conv_out_g"], sp["attn_out_g"], f"mixer_bwd{l}", dep=sc_rest.token)
        finish(sc_down, ["w_down"], dproj, l)
        finish(sc_rest, ["w_gate", "w_up", "w_o"], dproj, l)
        (d_win,) = grad_weight((dproj,), h, IN_COLS // 2, f"grad_w_in{l}")
        sc_in = scatter_start([slabs(d_win)], f"scatter_w_in{l}")
        dx, d_g1 = proj_bwd(dproj, wl["w_in"], x0, sp["norm1_g"], dx1, f"proj_bwd{l}", dep=sc_in.token)
        finish(sc_in, ["w_in"], dx, l)
        both_heads = lambda d: d[:, :HEAD] + d[:, HEAD:]
        gsmall[l] = dict(norm1_g=d_g1, conv_w=d_cw, q_norm_g=both_heads(d_qg), k_norm_g=both_heads(d_kg), sinks=d_sk,
                         conv_out_g=d_cog, attn_out_g=d_aog, norm2_g=d_g2)
    grad_x = dx[None]

    flat = jnp.concatenate([loss_row[0, 0:1]] + [gsmall[l][n].reshape(-1) for l in range(DEPTH) for n in SMALL_NAMES])
    flat = jnp.pad(flat, (0, SMALL_ROWS * 128 - flat.shape[0])).reshape(SMALL_ROWS, 128)
    flat = all_reduce_small(flat, "all_reduce_small_grads").reshape(-1)
    loss = flat[0]
    gs = {n: [] for n in SMALL_NAMES}
    off = 1
    for l in range(DEPTH):
        for n, size in zip(SMALL_NAMES, SMALL_SIZES):
            gs[n].append(flat[off:off + size])
            off += size
    gs = {n: jnp.stack(v) for n, v in gs.items()}
    g_conv = lax.dynamic_slice(gs["conv_w"].reshape(DEPTH, 3, CONV_CH), (0, 0, me * HEAD), (DEPTH, 3, HEAD))

    gs["conv_w"] = g_conv
    params = dict(norm1_g=(norm1_g, m_norm1_g, v_norm1_g), conv_w=(conv_w, m_conv_w, v_conv_w),
                  q_norm_g=(q_norm_g, m_q_norm_g, v_q_norm_g), k_norm_g=(k_norm_g, m_k_norm_g, v_k_norm_g),
                  sinks=(sinks, m_sinks, v_sinks), conv_out_g=(conv_out_g, m_conv_out_g, v_conv_out_g),
                  attn_out_g=(attn_out_g, m_attn_out_g, v_attn_out_g), norm2_g=(norm2_g, m_norm2_g, v_norm2_g))
    names = ("norm1_g", "w_in", "conv_w", "q_norm_g", "k_norm_g", "sinks", "conv_out_g", "attn_out_g", "w_o",
             "norm2_g", "w_gate", "w_up", "w_down")

    out = {}
    for n in names:
        if n in shard:
            out[n] = tuple(t(r) for r in stepped[n]) if n in ("w_in", "w_gate", "w_up") else stepped[n]
        else:
            w, m, v = params[n]
            two_d = (-1, w.shape[-1])
            d, mn, vn = adamw(w.reshape(two_d), gs[n].reshape(two_d), m.reshape(two_d), v.reshape(two_d), f"adamw_{n}")
            out[n] = (gs[n].reshape(w.shape), d.reshape(w.shape), mn.reshape(w.shape), vn.reshape(w.shape))
    return (loss, grad_x, *[out[n][i] for i in range(4) for n in names])
```

```python
import functools

import jax
import jax.numpy as jnp
from jax import lax
from jax.experimental import pallas as pl
from jax.experimental.pallas import tpu as pltpu

F32 = jnp.float32
BF16 = jnp.bfloat16

D_MODEL = 1024
CONV_CH = 512
ATTN_W = 512
N_Q = 8
N_KV = 2
GRP = N_Q // N_KV
HEAD = 64
IN_COLS = 2304
D_FF = 2816
BLK = 128
O_Q = 3 * CONV_CH
O_K = O_Q + ATTN_W
O_V = O_K + N_KV * HEAD
EPS = 1e-6
NEG_INF = -1e30
SCALE = HEAD ** -0.5
N_DEV = 8
DEPTH = 2

ADAM_LR = 0.001
ADAM_B1 = 0.9
ADAM_B2 = 0.999
ADAM_EPS = 1e-08
ADAM_WD = 0.01
ADAM_STEP = 10

VMEM_LIMIT = 56 * 1024 * 1024
TM = 512
MESH_T = pl.DeviceIdType.MESH

ROWS_IN, ROWS_O, ROWS_FF = IN_COLS // N_DEV, D_MODEL // N_DEV, D_FF // N_DEV


ANY = pl.BlockSpec(memory_space=pl.ANY)
HBM = pl.BlockSpec(memory_space=pltpu.HBM)
SEM = pl.BlockSpec(memory_space=pltpu.SEMAPHORE)


def _cparams(sem):
    return pltpu.CompilerParams(dimension_semantics=sem, vmem_limit_bytes=VMEM_LIMIT)


def _pallas_after(dep, body, *args, in_specs, **kw):
    if dep is None:
        return pl.pallas_call(body, in_specs=in_specs, **kw)(*args)

    def after_dep(dep_ref, *refs):
        body(*refs)

    return pl.pallas_call(after_dep, in_specs=[ANY, *in_specs], **kw)(dep, *args)


def _dot(a, b):
    return jnp.dot(a, b, preferred_element_type=F32)


def _dot_nt(a, b):
    return lax.dot_general(a, b, (((1,), (1,)), ((), ())), preferred_element_type=F32)


def _dot_tn(a, b):
    return lax.dot_general(a, b, (((0,), (0,)), ((), ())), preferred_element_type=F32)


LANES = 128


def _row_reduce(v, op, reduce):
    w = v.shape[-1]
    if w > LANES and w % LANES == 0:
        acc = v[:, 0:LANES]
        for c in range(1, w // LANES):
            acc = op(acc, v[:, LANES * c:LANES * (c + 1)])
        v = acc
    return reduce(v, axis=-1, keepdims=True)


def _row_sum(v):
    return _row_reduce(v, jnp.add, jnp.sum)


def _row_mean(v):
    return _row_sum(v) * (1.0 / v.shape[-1])


def _rms(v):
    return lax.rsqrt(_row_mean(v * v) + EPS)


def _sigmoid(v):
    return 1.0 / (1.0 + jnp.exp(-v))


def _rms_bwd(dyv, xh, r, g):
    dxh = dyv * g
    return r * (dxh - xh * _row_mean(dxh * xh))


def norm_proj(x, g, w_t, name, dep=None):
    S, N = x.shape[0], w_t.shape[0]

    def body(x_ref, g_ref, w_ref, h_ref, p_ref):
        xf = x_ref[...]
        h = ((xf * _rms(xf)) * g_ref[...]).astype(BF16)
        h_ref[...] = h
        p_ref[...] = _dot_nt(h, w_ref[...]).astype(BF16)

    return _pallas_after(
        dep, body, x, g, w_t, name=name, grid=(S // TM,),
        in_specs=[pl.BlockSpec((TM, D_MODEL), lambda i: (i, 0)),
                  pl.BlockSpec((1, D_MODEL), lambda i: (0, 0)),
                  pl.BlockSpec((N, D_MODEL), lambda i: (0, 0))],
        out_specs=[pl.BlockSpec((TM, D_MODEL), lambda i: (i, 0)),
                   pl.BlockSpec((TM, N), lambda i: (i, 0))],
        out_shape=[jax.ShapeDtypeStruct((S, D_MODEL), BF16), jax.ShapeDtypeStruct((S, N), BF16)],
        compiler_params=_cparams(("parallel",)),
    )


def band_bias():
    qi = lax.broadcasted_iota(jnp.int32, (BLK, 2 * BLK), 0)
    kj = lax.broadcasted_iota(jnp.int32, (BLK, 2 * BLK), 1)
    diff = qi + BLK - kj
    valid = (diff >= 0) & (diff < BLK)
    return jnp.stack([jnp.where(valid & (kj >= BLK), 0.0, NEG_INF), jnp.where(valid, 0.0, NEG_INF)]).astype(F32)


def _softmax_with_sink(s, sink):
    m = jnp.maximum(_row_reduce(s, jnp.maximum, jnp.max), sink)
    p = jnp.exp(s - m)
    es = jnp.exp(sink - m)
    inv = 1.0 / (_row_sum(p) + es)
    return p * inv, es * inv


PAIR = 2 * HEAD


def _low_half():
    return lax.broadcasted_iota(jnp.int32, (1, PAIR), 1) < HEAD


def _half_sums(v, low):
    return (jnp.sum(jnp.where(low, v, 0.0), axis=-1, keepdims=True),
            jnp.sum(jnp.where(low, 0.0, v), axis=-1, keepdims=True))


def _pair_mean(v, low):
    e, o = _half_sums(v, low)
    return jnp.where(low, e, o) * (1.0 / HEAD)


def _pair_rms(v, low):
    return lax.rsqrt(_pair_mean(v * v, low) + EPS)


def _one_head_in_both_halves(pair, low):
    swapped = pltpu.roll(pair, HEAD, 1)
    return jnp.where(low, pair, swapped), jnp.where(low, swapped, pair)


SUB_FWD, SUB_BWD = 2, 4


def _kv_pairs(p_ref, kvp_ref):
    k = jnp.concatenate([kvp_ref[:, 0:PAIR], p_ref[:, O_K:O_K + PAIR]], axis=0).astype(F32)
    v = jnp.concatenate([kvp_ref[:, PAIR:2 * PAIR], p_ref[:, O_V:O_V + PAIR]], axis=0)
    return k, v


def _band(s):
    return slice(BLK * s, BLK * s + 2 * BLK)


def _block(s):
    return slice(BLK * s, BLK * (s + 1))


def mixer_fwd(proj, bias, sinks, conv_w, qg, kg, cog, aog, name, residual=None, w_o=None):
    S = proj.shape[0]
    SUB, ROWS = SUB_FWD, SUB_FWD * BLK
    steps = S // ROWS
    project = w_o is not None

    def body(sinks_ref, p_ref, kvp_ref, bias_ref, cw_ref, qg_ref, kg_ref, cog_ref, aog_ref, *rest):
        if project:
            x_ref, wo_ref, mix_ref, y_ref, ao_ref, x1_ref, ucar = rest
        else:
            mix_ref, y_ref, ao_ref, ucar = rest
        n = pl.program_id(0)

        @pl.when(n == 0)
        def _():
            ucar[...] = jnp.zeros_like(ucar)

        bg = p_ref[:, 0:CONV_CH].astype(F32)
        u = p_ref[:, CONV_CH:2 * CONV_CH].astype(F32) * p_ref[:, 2 * CONV_CH:3 * CONV_CH].astype(F32)
        row = lax.broadcasted_iota(jnp.int32, (ROWS, CONV_CH), 0)
        prev = ucar[...]
        u1 = jnp.where(row == 0, prev[7:8, :], pltpu.roll(u, 1, 0))
        u2 = jnp.where(row == 0, prev[6:7, :], jnp.where(row == 1, prev[7:8, :], pltpu.roll(u, 2, 0)))
        ucar[...] = u[ROWS - 8:ROWS, :]
        y = cw_ref[0:1, :] * u2 + cw_ref[1:2, :] * u1 + cw_ref[2:3, :] * u
        y_ref[...] = y.astype(BF16)
        co = bg * y
        mix_conv = ((co * _rms(co)) * cog_ref[...]).astype(BF16)
        mix_ref[:, 0:CONV_CH] = mix_conv
        if project:
            x1 = x_ref[...] + _dot(mix_conv, wo_ref[0:CONV_CH, :])

        low = _low_half()
        q_gain = qg_ref[...] * SCALE
        k, v = _kv_pairs(p_ref, kvp_ref)
        kn = ((k * _pair_rms(k, low)) * kg_ref[...]).astype(BF16)
        k_of = _one_head_in_both_halves(kn, low)
        v_of = _one_head_in_both_halves(v, low)
        biases = [bias_ref[jnp.minimum(n, 1)]] + [bias_ref[1]] * (SUB - 1)
        units = [(s, hq) for s in range(SUB) for hq in range(N_Q)]
        q_one = []
        for pair in range(N_Q // 2):
            q = p_ref[:, O_Q + PAIR * pair:O_Q + PAIR * (pair + 1)].astype(F32)
            qn = ((q * _pair_rms(q, low)) * q_gain).astype(BF16)
            q_one += [jnp.where(mine, qn, jnp.zeros_like(qn)) for mine in (low, jnp.logical_not(low))]
        scores = [_dot_nt(q_one[hq][_block(s)], k_of[hq // GRP][_band(s)]) + biases[s] for s, hq in units]
        probs = [_softmax_with_sink(sc, sinks_ref[hq])[0].astype(BF16) for (s, hq), sc in zip(units, scores)]
        o = {u_: _dot(pn, v_of[u_[1] // GRP][_band(u_[0])]) for u_, pn in zip(units, probs)}
        ao = jnp.concatenate(
            [jnp.concatenate([jnp.where(low, o[s, 2 * pair], o[s, 2 * pair + 1]) for pair in range(N_Q // 2)], axis=1)
             for s in range(SUB)], axis=0)
        ao_ref[...] = ao.astype(BF16)
        mix_attn = ((ao * _rms(ao)) * aog_ref[...]).astype(BF16)
        mix_ref[:, CONV_CH:] = mix_attn
        if project:
            x1_ref[...] = x1 + _dot(mix_attn, wo_ref[CONV_CH:, :])

    small = lambda shape: pl.BlockSpec(shape, lambda n: (0, 0))
    rows = lambda w: pl.BlockSpec((ROWS, w), lambda n: (n, 0))
    return pl.pallas_call(
        body, name=name, grid=(steps,),
        in_specs=[pl.BlockSpec(memory_space=pltpu.SMEM),
                  rows(IN_COLS),
                  pl.BlockSpec((BLK, 2 * PAIR), lambda n: (jnp.maximum(SUB * n - 1, 0), O_K // (2 * PAIR))),
                  pl.BlockSpec((2, BLK, 2 * BLK), lambda n: (0, 0, 0)),
                  small((3, CONV_CH)), small((1, PAIR)), small((1, PAIR)), small((1, CONV_CH)), small((1, ATTN_W))]
        + ([rows(D_MODEL), small((D_MODEL, D_MODEL))] if project else []),
        out_specs=[rows(D_MODEL), rows(CONV_CH), rows(ATTN_W)] + ([rows(D_MODEL)] if project else []),
        out_shape=[jax.ShapeDtypeStruct((S, D_MODEL), BF16), jax.ShapeDtypeStruct((S, CONV_CH), BF16),
                   jax.ShapeDtypeStruct((S, ATTN_W), BF16)]
        + ([jax.ShapeDtypeStruct((S, D_MODEL), F32)] if project else []),
        scratch_shapes=[pltpu.VMEM((8, CONV_CH), F32)],
        compiler_params=_cparams(("arbitrary",)),
    )(sinks, proj, proj, bias, conv_w, qg, kg, cog, aog, *([residual, w_o] if project else []))


def matmul_residual(a, w, res, name, dep=None):
    S, K = a.shape
    N = w.shape[1]

    def body(a_ref, w_ref, r_ref, o_ref):
        o_ref[...] = r_ref[...] + _dot(a_ref[...], w_ref[...])

    return _pallas_after(
        dep, body, a, w, res, name=name, grid=(S // TM,),
        in_specs=[pl.BlockSpec((TM, K), lambda i: (i, 0)), pl.BlockSpec((K, N), lambda i: (0, 0)),
                  pl.BlockSpec((TM, N), lambda i: (i, 0))],
        out_specs=pl.BlockSpec((TM, N), lambda i: (i, 0)),
        out_shape=jax.ShapeDtypeStruct((S, N), F32),
        compiler_params=_cparams(("parallel",)),
    )


TF = 1408


def ffn_fwd(x1, g, wg, wu, wd, name, dep=None, tgt=None):
    S = x1.shape[0]
    ni, nj = S // TM, D_FF // TF
    with_loss = tgt is not None

    def body(x_ref, g_ref, wg_ref, wu_ref, wd_ref, *rest):
        if with_loss:
            t_ref, h2_ref, a_ref, b_ref, f_ref, o_ref, l_ref, acc, sq = rest
        else:
            h2_ref, a_ref, b_ref, f_ref, o_ref, acc = rest
        i, j = pl.program_id(0), pl.program_id(1)

        @pl.when(j == 0)
        def _():
            xf = x_ref[...]
            h2_ref[...] = ((xf * _rms(xf)) * g_ref[...]).astype(BF16)
            acc[...] = xf

        h2 = h2_ref[...]
        a = _dot_nt(h2, wg_ref[...])
        b = _dot_nt(h2, wu_ref[...])
        a_ref[...] = a.astype(BF16)
        b_ref[...] = b.astype(BF16)
        f = ((a * _sigmoid(a)) * b).astype(BF16)
        f_ref[...] = f
        acc[...] += _dot(f, wd_ref[...])

        @pl.when(j == nj - 1)
        def _():
            if not with_loss:
                o_ref[...] = acc[...]
                return
            e = acc[...] - t_ref[...]
            o_ref[...] = e * (1.0 / D_MODEL)
            col = jnp.sum(e * e, axis=0, keepdims=True)

            @pl.when(i == 0)
            def _():
                sq[...] = col

            @pl.when(i > 0)
            def _():
                sq[...] += col

            @pl.when(i == ni - 1)
            def _():
                l_ref[...] = jnp.full((1, 128), jnp.sum(sq[...]) * (0.5 / D_MODEL), F32)

    row = pl.BlockSpec((TM, D_MODEL), lambda i, j: (i, 0))
    tile = pl.BlockSpec((TM, TF), lambda i, j: (i, j))
    return _pallas_after(
        dep, body, x1, g, wg, wu, wd, *([tgt] if with_loss else []), name=name, grid=(ni, nj),
        in_specs=[row, pl.BlockSpec((1, D_MODEL), lambda i, j: (0, 0))]
        + [pl.BlockSpec((TF, D_MODEL), lambda i, j: (j, 0))] * 3 + ([row] if with_loss else []),
        out_specs=[row, tile, tile, tile, row] + ([pl.BlockSpec((1, 128), lambda i, j: (0, 0))] if with_loss else []),
        out_shape=[jax.ShapeDtypeStruct((S, D_MODEL), BF16)] + [jax.ShapeDtypeStruct((S, D_FF), BF16)] * 3
        + [jax.ShapeDtypeStruct((S, D_MODEL), F32)]
        + ([jax.ShapeDtypeStruct((1, 128), F32)] if with_loss else []),
        scratch_shapes=[pltpu.VMEM((TM, D_MODEL), F32)] + ([pltpu.VMEM((1, D_MODEL), F32)] if with_loss else []),
        compiler_params=_cparams(("arbitrary", "arbitrary") if with_loss else ("parallel", "arbitrary")),
    )


def ffn_bwd(dx2, a, b, x1, g, wg, wu, wd, name, dep=None):
    S = dx2.shape[0]
    nj = D_FF // TF

    def body(dx_ref, a_ref, b_ref, x_ref, g_ref, wg_ref, wu_ref, wd_ref, da_ref, db_ref, dx1_ref, dg_ref, dxb, acc):
        i, j = pl.program_id(0), pl.program_id(1)

        @pl.when((i == 0) & (j == 0))
        def _():
            dg_ref[...] = jnp.zeros_like(dg_ref)

        @pl.when(j == 0)
        def _():
            dxb[...] = dx_ref[...].astype(BF16)
            acc[...] = jnp.zeros_like(acc)

        df = _dot_nt(dxb[...], wd_ref[...])
        av = a_ref[...].astype(F32)
        bv = b_ref[...].astype(F32)
        sg = _sigmoid(av)
        da = ((df * bv) * (sg * (1.0 + av * (1.0 - sg)))).astype(BF16)
        db = (df * (av * sg)).astype(BF16)
        da_ref[...] = da
        db_ref[...] = db
        acc[...] += _dot(da, wg_ref[...]) + _dot(db, wu_ref[...])

        @pl.when(j == nj - 1)
        def _():
            xf = x_ref[...]
            r = _rms(xf)
            xh = xf * r
            dh = acc[...]
            dg_ref[...] += jnp.sum(dh * xh, axis=0, keepdims=True)
            dx1_ref[...] = dx_ref[...] + _rms_bwd(dh, xh, r, g_ref[...])

    return _pallas_after(
        dep, body, dx2, a, b, x1, g, wg, wu, wd, name=name, grid=(S // TM, nj),
        in_specs=[pl.BlockSpec((TM, D_MODEL), lambda i, j: (i, 0)),
                  pl.BlockSpec((TM, TF), lambda i, j: (i, j)), pl.BlockSpec((TM, TF), lambda i, j: (i, j)),
                  pl.BlockSpec((TM, D_MODEL), lambda i, j: (i, 0)), pl.BlockSpec((1, D_MODEL), lambda i, j: (0, 0))]
        + [pl.BlockSpec((TF, D_MODEL), lambda i, j: (j, 0))] * 3,
        out_specs=[pl.BlockSpec((TM, TF), lambda i, j: (i, j)), pl.BlockSpec((TM, TF), lambda i, j: (i, j)),
                   pl.BlockSpec((TM, D_MODEL), lambda i, j: (i, 0)), pl.BlockSpec((1, D_MODEL), lambda i, j: (0, 0))],
        out_shape=[jax.ShapeDtypeStruct((S, D_FF), BF16), jax.ShapeDtypeStruct((S, D_FF), BF16),
                   jax.ShapeDtypeStruct((S, D_MODEL), F32), jax.ShapeDtypeStruct((1, D_MODEL), F32)],
        scratch_shapes=[pltpu.VMEM((TM, D_MODEL), BF16), pltpu.VMEM((TM, D_MODEL), F32)],
        compiler_params=_cparams(("arbitrary", "arbitrary")),
    )


def grad_weight(lhs, rhs, tm, name, dep=None):
    S, N = rhs.shape
    M = lhs[0].shape[1]
    tk = 1024
    nk = S // tk
    n_l = n_o = len(lhs)

    def body(*refs):
        l_refs, r_ref = refs[:n_l], refs[n_l]
        o_refs, accs = refs[n_l + 1:n_l + 1 + n_o], refs[n_l + 1 + n_o:]
        k = pl.program_id(1)
        rv = r_ref[...].astype(BF16)
        for l_ref, o_ref, acc in zip(l_refs, o_refs, accs):
            @pl.when(k == 0)
            def _():
                acc[...] = jnp.zeros_like(acc)

            acc[...] += _dot_tn(l_ref[...], rv)

            @pl.when(k == nk - 1)
            def _():
                o_ref[...] = acc[...].astype(BF16)

    return _pallas_after(
        dep, body, *lhs, rhs, name=name, grid=(M // tm, nk),
        in_specs=[pl.BlockSpec((tk, tm), lambda i, k: (k, i)) for _ in lhs] + [pl.BlockSpec((tk, N), lambda i, k: (k, 0))],
        out_specs=[pl.BlockSpec((tm, N), lambda i, k: (i, 0))] * n_o,
        out_shape=[jax.ShapeDtypeStruct((M, N), BF16)] * n_o,
        scratch_shapes=[pltpu.VMEM((tm, N), F32)] * n_o,
        compiler_params=_cparams(("parallel", "arbitrary")),
    )


def proj_bwd(dproj, w, x, g, dres, name, dep=None):
    S, N = dproj.shape

    def body(dp_ref, w_ref, x_ref, g_ref, dr_ref, dx_ref, dg_ref):
        i = pl.program_id(0)

        @pl.when(i == 0)
        def _():
            dg_ref[...] = jnp.zeros_like(dg_ref)

        dh = _dot(dp_ref[...], w_ref[...])
        xf = x_ref[...]
        r = _rms(xf)
        xh = xf * r
        dg_ref[...] += jnp.sum(dh * xh, axis=0, keepdims=True)
        dx_ref[...] = dr_ref[...] + _rms_bwd(dh, xh, r, g_ref[...])

    return _pallas_after(
        dep, body, dproj, w, x, g, dres, name=name, grid=(S // TM,),
        in_specs=[pl.BlockSpec((TM, N), lambda i: (i, 0)), pl.BlockSpec((N, D_MODEL), lambda i: (0, 0)),
                  pl.BlockSpec((TM, D_MODEL), lambda i: (i, 0)), pl.BlockSpec((1, D_MODEL), lambda i: (0, 0)),
                  pl.BlockSpec((TM, D_MODEL), lambda i: (i, 0))],
        out_specs=[pl.BlockSpec((TM, D_MODEL), lambda i: (i, 0)), pl.BlockSpec((1, D_MODEL), lambda i: (0, 0))],
        out_shape=[jax.ShapeDtypeStruct((S, D_MODEL), F32), jax.ShapeDtypeStruct((1, D_MODEL), F32)],
        compiler_params=_cparams(("arbitrary",)),
    )


def mixer_bwd(proj, bias, y, ao, dx1, w_o, sinks, conv_w, qg, kg, cog, aog, name, dep=None):
    S = proj.shape[0]
    SUB, ROWS = SUB_BWD, SUB_BWD * BLK
    steps = S // ROWS
    KV_W = 2 * PAIR

    def body(sinks_ref, p_ref, kvp_ref, bias_ref, y_ref, ao_ref, dx_first_ref, dx_next_ref, wo_ref, cw_ref, qg_ref,
             kg_ref, cog_ref, aog_ref, dp_ref, dcw_ref, dqg_ref, dkg_ref, dsk_ref, dcog_ref, daog_ref,
             dycar, kcar, vcar, dmix_scr):
        step = pl.program_id(0)

        @pl.when(step == 0)
        def _():
            dmix_scr[...] = _dot_nt(dx_first_ref[...].astype(BF16), wo_ref[...])
            dycar[...] = jnp.zeros_like(dycar)
            kcar[...] = jnp.zeros_like(kcar)
            vcar[...] = jnp.zeros_like(vcar)
            dcw_ref[...] = jnp.zeros_like(dcw_ref)
            dqg_ref[...] = jnp.zeros_like(dqg_ref)
            dkg_ref[...] = jnp.zeros_like(dkg_ref)
            dsk_ref[...] = jnp.zeros_like(dsk_ref)
            dcog_ref[...] = jnp.zeros_like(dcog_ref)
            daog_ref[...] = jnp.zeros_like(daog_ref)

        dma = dmix_scr[:, CONV_CH:]

        aov = ao_ref[...].astype(F32)
        ra = _rms(aov)
        ah = aov * ra
        daog_ref[...] += jnp.sum(dma * ah, axis=0, keepdims=True)
        dao = _rms_bwd(dma, ah, ra, aog_ref[...])

        low = _low_half()
        high = jnp.logical_not(low)
        q_gain = qg_ref[...] * SCALE
        k, v = _kv_pairs(p_ref, kvp_ref)
        rk = _pair_rms(k, low)
        kh = k * rk
        k_of = _one_head_in_both_halves((kh * kg_ref[...]).astype(BF16), low)
        v_of = _one_head_in_both_halves(v, low)
        lane8 = lax.broadcasted_iota(jnp.int32, (1, N_Q), 1)
        dsk = jnp.zeros((1, N_Q), F32)
        dqg = jnp.zeros((1, PAIR), F32)
        biases = [bias_ref[jnp.minimum(steps - 1 - step, 1)]] + [bias_ref[1]] * (SUB - 1)
        units = [(s, hq) for s in range(SUB) for hq in range(N_Q)]
        rq, qh, q_one, do_one, delta = [], [], [], [], []
        for pair in range(N_Q // 2):
            cols = slice(PAIR * pair, PAIR * (pair + 1))
            q = p_ref[:, O_Q + PAIR * pair:O_Q + PAIR * (pair + 1)].astype(F32)
            rq.append(_pair_rms(q, low))
            qh.append(q * rq[pair])
            qn = (qh[pair] * q_gain).astype(BF16)
            do = dao[:, cols]
            do_b = do.astype(BF16)
            delta += _half_sums(do * aov[:, cols], low)
            for mine in (low, high):
                q_one.append(jnp.where(mine, qn, jnp.zeros_like(qn)))
                do_one.append(jnp.where(mine, do_b, jnp.zeros_like(do_b)))
        scores = {u: _dot_nt(q_one[u[1]][_block(u[0])], k_of[u[1] // GRP][_band(u[0])]) + biases[u[0]] for u in units}
        dprobs = {u: _dot_nt(do_one[u[1]][_block(u[0])], v_of[u[1] // GRP][_band(u[0])]) for u in units}
        probs = {u: _softmax_with_sink(scores[u], sinks_ref[u[1]]) for u in units}
        ds = {u: (probs[u][0] * (dprobs[u] - delta[u[1]][_block(u[0])])).astype(BF16) for u in units}
        dv_t = {(s, h): jnp.zeros((PAIR, 2 * BLK), F32) for s in range(SUB) for h in range(N_KV)}
        dkn_t = dict(dv_t)
        for s, hq in units:
            dsk = dsk - jnp.where(lane8 == hq, jnp.sum(probs[s, hq][1] * delta[hq][_block(s)]), 0.0)
            dv_t[s, hq // GRP] = dv_t[s, hq // GRP] + _dot_tn(do_one[hq][_block(s)], probs[s, hq][0].astype(BF16))
            dkn_t[s, hq // GRP] = dkn_t[s, hq // GRP] + _dot_tn(q_one[hq][_block(s)], ds[s, hq])
        dqn_of = {u: _dot(ds[u], k_of[u[1] // GRP][_band(u[0])]) for u in units}
        for pair in range(N_Q // 2):
            dqn = jnp.concatenate([jnp.where(low, dqn_of[s, 2 * pair], dqn_of[s, 2 * pair + 1]) for s in range(SUB)],
                                  axis=0)
            dqg = dqg + jnp.sum(dqn * qh[pair], axis=0, keepdims=True)
            dqh = dqn * q_gain
            dp_ref[:, O_Q + PAIR * pair:O_Q + PAIR * (pair + 1)] = (
                rq[pair] * (dqh - qh[pair] * _pair_mean(dqh * qh[pair], low))).astype(BF16)

        def over_key_rows(parts):
            bands = [jnp.concatenate([parts[s, h][:HEAD] + parts[s, h][HEAD:] for h in range(N_KV)], axis=0).T
                     for s in range(SUB)]
            pieces = [bands[0][:BLK]]
            pieces += [bands[s - 1][BLK:] + bands[s][:BLK] for s in range(1, SUB)]
            pieces.append(bands[SUB - 1][BLK:])
            return jnp.concatenate(pieces, axis=0)

        dv = over_key_rows(dv_t)
        dkn = over_key_rows(dkn_t)
        dkg_ref[...] += jnp.sum(dkn * kh, axis=0, keepdims=True)
        dkh = dkn * kg_ref[...]
        dk = rk * (dkh - kh * _pair_mean(dkh * kh, low))
        last = slice(ROWS, ROWS + BLK)
        dp_ref[:, O_K:O_V] = jnp.concatenate([dk[BLK:ROWS], dk[last] + kcar[...]], axis=0).astype(BF16)
        dp_ref[:, O_V:] = jnp.concatenate([dv[BLK:ROWS], dv[last] + vcar[...]], axis=0).astype(BF16)
        kcar[...] = dk[:BLK, :]
        vcar[...] = dv[:BLK, :]
        dsk_ref[...] += dsk
        dqg_ref[...] += dqg * SCALE

        bg = p_ref[:, 0:CONV_CH].astype(F32)
        cg = p_ref[:, CONV_CH:2 * CONV_CH].astype(F32)
        hc = p_ref[:, 2 * CONV_CH:3 * CONV_CH].astype(F32)
        yv = y_ref[...].astype(F32)
        dmc = dmix_scr[:, 0:CONV_CH]
        co = bg * yv
        rc = _rms(co)
        ch = co * rc
        dcog_ref[...] += jnp.sum(dmc * ch, axis=0, keepdims=True)
        dco = _rms_bwd(dmc, ch, rc, cog_ref[...])
        dp_ref[:, 0:CONV_CH] = (dco * yv).astype(BF16)
        dy = dco * bg
        row = lax.broadcasted_iota(jnp.int32, (ROWS, CONV_CH), 0)
        nxt = dycar[...]
        dy1 = jnp.where(row == ROWS - 1, nxt[0:1, :], pltpu.roll(dy, ROWS - 1, 0))
        dy2 = jnp.where(row == ROWS - 2, nxt[0:1, :],
                        jnp.where(row == ROWS - 1, nxt[1:2, :], pltpu.roll(dy, ROWS - 2, 0)))
        dycar[...] = dy[0:8, :]
        du = cw_ref[2:3, :] * dy + cw_ref[1:2, :] * dy1 + cw_ref[0:1, :] * dy2
        dp_ref[:, CONV_CH:2 * CONV_CH] = (du * hc).astype(BF16)
        dp_ref[:, 2 * CONV_CH:3 * CONV_CH] = (du * cg).astype(BF16)
        u = cg * hc
        dcw_ref[0:1, :] += jnp.sum(dy2 * u, axis=0, keepdims=True)
        dcw_ref[1:2, :] += jnp.sum(dy1 * u, axis=0, keepdims=True)
        dcw_ref[2:3, :] += jnp.sum(dy * u, axis=0, keepdims=True)

        dmix_scr[...] = _dot_nt(dx_next_ref[...].astype(BF16), wo_ref[...])

    small = lambda shape: pl.BlockSpec(shape, lambda s: (0, 0))
    blk = lambda w: pl.BlockSpec((ROWS, w), lambda s: (steps - 1 - s, 0))
    return _pallas_after(
        dep, body, sinks, proj, proj, bias, y, ao, dx1, dx1, w_o, conv_w, qg, kg, cog, aog, name=name, grid=(steps,),
        in_specs=[pl.BlockSpec(memory_space=pltpu.SMEM),
                  blk(IN_COLS),
                  pl.BlockSpec((BLK, KV_W), lambda s: (jnp.maximum(SUB * (steps - 1 - s) - 1, 0), O_K // KV_W)),
                  pl.BlockSpec((2, BLK, 2 * BLK), lambda s: (0, 0, 0)),
                  blk(CONV_CH), blk(ATTN_W),
                  pl.BlockSpec((ROWS, D_MODEL), lambda s: (steps - 1, 0)),
                  pl.BlockSpec((ROWS, D_MODEL), lambda s: (jnp.maximum(steps - 2 - s, 0), 0)),
                  small((D_MODEL, D_MODEL)),
                  small((3, CONV_CH)), small((1, PAIR)), small((1, PAIR)), small((1, CONV_CH)), small((1, ATTN_W))],
        out_specs=[blk(IN_COLS), small((3, CONV_CH)), small((1, PAIR)), small((1, PAIR)), small((1, N_Q)),
                   small((1, CONV_CH)), small((1, ATTN_W))],
        out_shape=[jax.ShapeDtypeStruct((S, IN_COLS), BF16), jax.ShapeDtypeStruct((3, CONV_CH), F32),
                   jax.ShapeDtypeStruct((1, PAIR), F32), jax.ShapeDtypeStruct((1, PAIR), F32),
                   jax.ShapeDtypeStruct((1, N_Q), F32), jax.ShapeDtypeStruct((1, CONV_CH), F32),
                   jax.ShapeDtypeStruct((1, ATTN_W), F32)],
        scratch_shapes=[pltpu.VMEM((8, CONV_CH), F32), pltpu.VMEM((BLK, PAIR), F32), pltpu.VMEM((BLK, PAIR), F32),
                        pltpu.VMEM((ROWS, D_MODEL), F32)],
        compiler_params=_cparams(("arbitrary",)),
    )


OTHER_CHIPS = ((1, 0), (0, 1), (1, 1))


def _position():
    return lax.axis_index("x"), lax.axis_index("y"), lax.axis_index("c")


def all_gather(shards, name):
    n = len(shards)

    def body(*refs):
        x_refs, out_refs = refs[:n], refs[n:2 * n]
        send_sems, recv_sems, local_sems = refs[2 * n:]
        x, y, c = _position()
        me, sibling = (x, y, c), (x, y, 1 - c)
        chips = [(x ^ mx, y ^ my) for mx, my in OTHER_CHIPS]

        def slab(a, px, py, pc):
            return out_refs[a].at[4 * px + 2 * py + pc]

        def copy(a, k, block, to, src=None):
            return pltpu.make_async_remote_copy(
                src_ref=slab(a, *block) if src is None else src, dst_ref=slab(a, *block),
                send_sem=send_sems.at[7 * a + k], recv_sem=recv_sems.at[7 * a + k], device_id=to, device_id_type=MESH_T)

        mine = [pltpu.make_async_copy(x_refs[a], slab(a, *me), local_sems.at[a]) for a in range(n)]
        for cp in mine:
            cp.start()
        first = []
        for a in range(n):
            first.append(copy(a, 0, me, sibling, src=x_refs[a]))
            first += [copy(a, 1 + j, me, (*chip, c), src=x_refs[a]) for j, chip in enumerate(chips)]
        for cp in first:
            cp.start()
        passed = []
        for a in range(n):
            for j, chip in enumerate(chips):
                copy(a, 1 + j, (*chip, c), me).wait_recv()
                passed.append(copy(a, 4 + j, (*chip, c), sibling))
                passed[-1].start()
        for a in range(n):
            copy(a, 0, sibling, me).wait_recv()
            for j, chip in enumerate(chips):
                copy(a, 4 + j, (*chip, 1 - c), me).wait_recv()
        for cp in first + passed:
            cp.wait_send()
        for cp in mine:
            cp.wait()

    return pl.pallas_call(
        body, name=name, in_specs=[ANY] * n, out_specs=[ANY] * n,
        out_shape=[jax.ShapeDtypeStruct((N_DEV, *s.shape), s.dtype) for s in shards],
        scratch_shapes=[pltpu.SemaphoreType.DMA((7 * n,)), pltpu.SemaphoreType.DMA((7 * n,)),
                        pltpu.SemaphoreType.DMA((n,))],
    )(*shards)


class SplitCopy:
    def __init__(self, name, arrays, n_copies, plan, after=None):
        n = len(arrays)
        self.name, self.n, self.n_copies, self.plan = name, n, n_copies, plan
        extra = [] if after is None else [after]

        def body(*refs):
            in_refs = refs[:n]
            send_sems, recv_sems = refs[n + len(extra)], refs[n + len(extra) + 1]
            token = refs[2 * n + len(extra) + 2]
            for k, (src, dst, to) in enumerate(plan(_position(), in_refs)):
                pltpu.make_async_remote_copy(src_ref=src, dst_ref=dst, send_sem=send_sems.at[k],
                                             recv_sem=recv_sems.at[k], device_id=to, device_id_type=MESH_T).start()
            token[...] = jnp.zeros_like(token)

        outs = pl.pallas_call(
            body, name=name + "_start",
            out_shape=(pltpu.SemaphoreType.DMA((n_copies,)), pltpu.SemaphoreType.DMA((n_copies,)),
                       *[pltpu.HBM(a.shape, a.dtype) for a in arrays], jax.ShapeDtypeStruct((8, 128), F32)),
            in_specs=[HBM] * n + [ANY] * len(extra),
            out_specs=(SEM, SEM, *[HBM] * n, pl.BlockSpec(memory_space=pltpu.VMEM)),
            input_output_aliases={i: 2 + i for i in range(n)},
            compiler_params=pltpu.CompilerParams(has_side_effects=pltpu.SideEffectType.DATAFLOW_SIDE_EFFECTING),
        )(*[pltpu.with_memory_space_constraint(a, pltpu.HBM) for a in arrays], *extra)
        self.send_sems, self.recv_sems = outs[0], outs[1]
        self.arrays, self.token = list(outs[2:2 + n]), outs[2 + n]

    def wait(self, after):
        n, plan = self.n, self.plan

        def body(*refs):
            in_refs, send_sems, recv_sems = refs[:n], refs[n], refs[n + 1]
            for k, (src, dst, to) in enumerate(plan(_position(), in_refs)):
                cp = pltpu.make_async_remote_copy(src_ref=src, dst_ref=dst, send_sem=send_sems.at[k],
                                                  recv_sem=recv_sems.at[k], device_id=to, device_id_type=MESH_T)
                cp.wait_send()
                cp.wait_recv()

        outs = pl.pallas_call(
            body, name=self.name + "_wait",
            out_shape=tuple(pltpu.HBM(a.shape, a.dtype) for a in self.arrays),
            in_specs=[HBM] * n + [SEM, SEM, ANY], out_specs=tuple([HBM] * n),
            input_output_aliases={i: i for i in range(n)},
            compiler_params=pltpu.CompilerParams(has_side_effects=pltpu.SideEffectType.DATAFLOW_SIDE_EFFECTING),
        )(*self.arrays, self.send_sems, self.recv_sems, after)
        return list(outs)


def gather_start(shards, me, name, after=None):
    n = len(shards)
    lands = [lax.dynamic_update_slice(lax.empty((N_DEV, *s.shape), s.dtype), s[None], (me, 0, 0)) for s in shards]

    def plan(pos, refs):
        x, y, c = pos
        return [(refs[a], refs[n + a].at[4 * x + 2 * y + c], (x ^ mx, y ^ my, c))
                for a in range(n) for mx, my in OTHER_CHIPS]

    return SplitCopy(name, list(shards) + lands, 3 * n, plan, after=after)


def sibling_start(lands, name):
    n = len(lands)

    def plan(pos, refs):
        x, y, c = pos
        return [(refs[a].at[2 * q + c], refs[a].at[2 * q + c], (x, y, 1 - c)) for a in range(n) for q in range(4)]

    return SplitCopy(name, list(lands), 4 * n, plan)


def scatter_start(slabs, name):
    n = len(slabs)
    lands = [lax.empty((N_DEV - 1, *g.shape[1:]), g.dtype) for g in slabs]

    def plan(pos, refs):
        x, y, c = pos
        copies = []
        for a in range(n):
            for r in range(1, N_DEV):
                px, py, pc = x ^ ((r >> 2) & 1), y ^ ((r >> 1) & 1), c ^ (r & 1)
                copies.append((refs[a].at[4 * px + 2 * py + pc], refs[n + a].at[r - 1], (px, py, pc)))
        return copies

    return SplitCopy(name, list(slabs) + lands, (N_DEV - 1) * n, plan)


def all_reduce_small(v, name, dep=None):
    R, W = v.shape

    def body(v_ref, o_ref, recv, send_sems, recv_sems):
        x, y, c = _position()
        me = 4 * x + 2 * y + c
        copies = []
        for r in range(1, N_DEV):
            to = (x ^ ((r >> 2) & 1), y ^ ((r >> 1) & 1), c ^ (r & 1))
            copies.append(pltpu.make_async_remote_copy(
                src_ref=v_ref, dst_ref=recv.at[me], send_sem=send_sems.at[r - 1], recv_sem=recv_sems.at[r - 1],
                device_id=to, device_id_type=MESH_T))
        for cp in copies:
            cp.start()
        recv[pl.ds(me, 1)] = v_ref[...][None]
        for cp in copies:
            cp.wait()
        acc = recv[0]
        for s in range(1, N_DEV):
            acc = acc + recv[s]
        o_ref[...] = acc

    return _pallas_after(
        dep, body, v, name=name,
        in_specs=[pl.BlockSpec(memory_space=pltpu.VMEM)], out_specs=pl.BlockSpec(memory_space=pltpu.VMEM),
        out_shape=jax.ShapeDtypeStruct((R, W), F32),
        scratch_shapes=[pltpu.VMEM((N_DEV, R, W), F32), pltpu.SemaphoreType.DMA((N_DEV - 1,)),
                        pltpu.SemaphoreType.DMA((N_DEV - 1,))],
    )


def _row_tile(rows):
    if rows <= 512:
        return rows
    return max(t for t in range(8, 513, 8) if rows % t == 0)


def _adamw_update(w, g, m, v):
    mn = ADAM_B1 * m + (1.0 - ADAM_B1) * g
    vn = ADAM_B2 * v + (1.0 - ADAM_B2) * (g * g)
    m_hat = mn / (1.0 - ADAM_B1 ** ADAM_STEP)
    v_hat = vn / (1.0 - ADAM_B2 ** ADAM_STEP)
    return -ADAM_LR * (m_hat / (jnp.sqrt(v_hat) + ADAM_EPS) + ADAM_WD * w), mn, vn


def adamw(w, g, m, v, name):
    R, W = w.shape
    tr = _row_tile(R)

    def body(w_ref, g_ref, m_ref, v_ref, d_ref, mo_ref, vo_ref):
        d_ref[...], mo_ref[...], vo_ref[...] = _adamw_update(w_ref[...], g_ref[...], m_ref[...], v_ref[...])

    spec = pl.BlockSpec((tr, W), lambda i: (i, 0))
    return pl.pallas_call(
        body, name=name, grid=(R // tr,), in_specs=[spec] * 4, out_specs=[spec] * 3,
        out_shape=[jax.ShapeDtypeStruct((R, W), F32)] * 3,
        compiler_params=_cparams(("parallel",)),
    )(w, g, m, v)


def reduce_adamw(slabs, land, w, m, v, layer, me_arr, name, others=None):
    _, R, W = slabs.shape
    tr = _row_tile(R)
    n_other = 0 if others is None else 4

    def body(me_ref, s_ref, l_ref, w_ref, m_ref, v_ref, *rest):
        g_ref, d_ref, mo_ref, vo_ref = rest[n_other:]
        g = s_ref[0].astype(F32)
        for r in range(N_DEV - 1):
            g = g + l_ref[r].astype(F32)
        g_ref[0] = g
        d_ref[0], mo_ref[0], vo_ref[0] = _adamw_update(w_ref[0], g, m_ref[0], v_ref[0])

    spec = pl.BlockSpec((1, tr, W), lambda i, me: (layer, i, 0))
    return pl.pallas_call(
        body, name=name,
        grid_spec=pltpu.PrefetchScalarGridSpec(
            num_scalar_prefetch=1, grid=(R // tr,),
            in_specs=[pl.BlockSpec((1, tr, W), lambda i, me: (me[0], i, 0)),
                      pl.BlockSpec((N_DEV - 1, tr, W), lambda i, me: (0, i, 0)), spec, spec, spec] + [ANY] * n_other,
            out_specs=[spec] * 4),
        out_shape=[jax.ShapeDtypeStruct((DEPTH, R, W), F32)] * 4,
        input_output_aliases={6 + i: i for i in range(n_other)},
        compiler_params=_cparams(("parallel",)),
    )(me_arr, slabs, land, w, m, v, *([] if others is None else others))


SMALL_NAMES = ("norm1_g", "q_norm_g", "k_norm_g", "sinks", "conv_out_g", "attn_out_g", "norm2_g", "conv_w")
SMALL_SIZES = (D_MODEL, HEAD, HEAD, N_Q, CONV_CH, ATTN_W, D_MODEL, 3 * CONV_CH)
SMALL_ROWS = 80


def kernel(x, norm1_g, w_in, conv_w, q_norm_g, k_norm_g, sinks, conv_out_g, attn_out_g, w_o, norm2_g, w_gate, w_up, w_down, loss_target, m_norm1_g, m_w_in, m_conv_w, m_q_norm_g, m_k_norm_g, m_sinks, m_conv_out_g, m_attn_out_g, m_w_o, m_norm2_g, m_w_gate, m_w_up, m_w_down, v_norm1_g, v_w_in, v_conv_w, v_q_norm_g, v_k_norm_g, v_sinks, v_conv_out_g, v_attn_out_g, v_w_o, v_norm2_g, v_w_gate, v_w_up, v_w_down):
    xi, yi, ci = _position()
    me = 4 * xi + 2 * yi + ci
    me_arr = jnp.reshape(me, (1,)).astype(jnp.int32)
    xs, tgt = x[0], loss_target[0]
    bf = lambda a: a.astype(BF16)
    bias = band_bias()

    t = lambda a: jnp.swapaxes(a, 1, 2)
    shard = dict(w_in=(t(w_in), t(m_w_in), t(v_w_in)), w_o=(w_o, m_w_o, v_w_o),
                 w_gate=(t(w_gate), t(m_w_gate), t(v_w_gate)), w_up=(t(w_up), t(m_w_up), t(v_w_up)),
                 w_down=(w_down, m_w_down, v_w_down))
    wb = {n: bf(shard[n][0]) for n in shard}

    g_in0, g_o0, g_conv = all_gather([wb["w_in"][0], wb["w_o"][0], conv_w.reshape(DEPTH * 3, HEAD)], "gather_first")
    ag_ffn0 = gather_start([wb["w_gate"][0], wb["w_up"][0], wb["w_down"][0]], me, "gather_ffn0", after=g_in0)
    ag_mix1 = gather_start([wb["w_in"][1], wb["w_o"][1]], me, "gather_mix1", after=ag_ffn0.token)
    ag_ffn1 = gather_start([wb["w_gate"][1], wb["w_up"][1], wb["w_down"][1]], me, "gather_ffn1", after=ag_mix1.token)
    conv_full = g_conv.reshape(N_DEV, DEPTH, 3, HEAD).transpose(1, 2, 0, 3).reshape(DEPTH, 3, CONV_CH)
    pair_gain = lambda g: jnp.tile(g[None], (1, 2))
    small = [dict(norm1_g=norm1_g[l][None], conv_w=conv_full[l], q_norm_g=pair_gain(q_norm_g[l]),
                  k_norm_g=pair_gain(k_norm_g[l]), sinks=sinks[l], conv_out_g=conv_out_g[l][None],
                  attn_out_g=attn_out_g[l][None], norm2_g=norm2_g[l][None]) for l in range(DEPTH)]
    whole = lambda g: g.reshape(-1, D_MODEL)
    weights = [dict(w_in=whole(g_in0), w_o=whole(g_o0)), {}]

    def ffn_weights(g_gate, g_up, g_down):
        return dict(w_gate=whole(g_gate), w_up=whole(g_up), w_down=whole(g_down))

    saved = []
    xl = xs
    for l in range(DEPTH):
        sp, wl = small[l], weights[l]
        h, proj = norm_proj(xl, sp["norm1_g"], wl["w_in"], f"norm_proj{l}",
                            dep=ag_ffn1.token if l == 0 else pass_ffn1.token)
        mixer_args = (proj, bias, sp["sinks"], sp["conv_w"], sp["q_norm_g"], sp["k_norm_g"], sp["conv_out_g"],
                      sp["attn_out_g"], f"mixer_fwd{l}")
        if l == 0:
            mix, y, ao = mixer_fwd(*mixer_args)
            pass_ffn0 = sibling_start(ag_ffn0.wait(mix)[3:], "pass_ffn0")
            x1 = matmul_residual(mix, wl["w_o"], xl, "out_proj0", dep=pass_ffn0.token)
            wl.update(ffn_weights(*pass_ffn0.wait(x1)))
            pass_mix1 = sibling_start(ag_mix1.wait(x1)[2:], "pass_mix1")
            h2, a, b, f, x2 = ffn_fwd(x1, sp["norm2_g"], wl["w_gate"], wl["w_up"], wl["w_down"], "ffn_fwd0",
                                      dep=pass_mix1.token)
            g_in, g_o = pass_mix1.wait(x2)
            weights[1] = dict(w_in=whole(g_in), w_o=whole(g_o))
            pass_ffn1 = sibling_start(ag_ffn1.wait(x2)[3:], "pass_ffn1")
        else:
            mix, y, ao, x1 = mixer_fwd(*mixer_args, residual=xl, w_o=wl["w_o"])
            wl.update(ffn_weights(*pass_ffn1.wait(x1)))
            h2, a, b, f, dx, loss_row = ffn_fwd(x1, sp["norm2_g"], wl["w_gate"], wl["w_up"], wl["w_down"], "ffn_fwd1",
                                                tgt=tgt)
            x2 = None
        saved.append((xl, h, proj, mix, y, ao, x1, h2, a, b, f))
        xl = x2

    stepped = {n: None for n in shard}
    slabs = lambda d: d.reshape(N_DEV, -1, D_MODEL)
    gsmall = [None] * DEPTH

    def finish(sc, names, after, l):
        arrays = sc.wait(after)
        k = len(names)
        for i, n in enumerate(names):
            w, m, v = shard[n]
            stepped[n] = reduce_adamw(arrays[i], arrays[k + i], w, m, v, l, me_arr, f"reduce_adamw_{n}{l}",
                                      others=stepped[n])

    for l in reversed(range(DEPTH)):
        sp, wl = small[l], weights[l]
        x0, h, proj, mix, y, ao, x1, h2, a, b, f = saved[l]
        (d_wd,) = grad_weight((f,), dx, TF, f"grad_w_down{l}")
        sc_down = scatter_start([slabs(d_wd)], f"scatter_w_down{l}")
        da, db, dx1, d_g2 = ffn_bwd(dx, a, b, x1, sp["norm2_g"], wl["w_gate"], wl["w_up"], wl["w_down"],
                                    f"ffn_bwd{l}", dep=sc_down.token)
        d_wg, d_wu = grad_weight((da, db), h2, TF, f"grad_w_gate_up{l}")
        (d_wo,) = grad_weight((mix,), dx1, D_MODEL, f"grad_w_o{l}")
        sc_rest = scatter_start([slabs(d_wg), slabs(d_wu), slabs(d_wo)], f"scatter_w_gate_up_o{l}")
        dproj, d_cw, d_qg, d_kg, d_sk, d_cog, d_aog = mixer_bwd(
            proj, bias, y, ao, dx1, wl["w_o"], sp["sinks"], sp["conv_w"], sp["q_norm_g"], sp["k_norm_g"],
            sp["conv_out_g"], sp["attn_out_g"], f"mixer_bwd{l}", dep=sc_rest.token)
        finish(sc_down, ["w_down"], dproj, l)
        finish(sc_rest, ["w_gate", "w_up", "w_o"], dproj, l)
        (d_win,) = grad_weight((dproj,), h, IN_COLS // 2, f"grad_w_in{l}")
        sc_in = scatter_start([slabs(d_win)], f"scatter_w_in{l}")
        dx, d_g1 = proj_bwd(dproj, wl["w_in"], x0, sp["norm1_g"], dx1, f"proj_bwd{l}", dep=sc_in.token)
        finish(sc_in, ["w_in"], dx, l)
        both_heads = lambda d: d[:, :HEAD] + d[:, HEAD:]
        gsmall[l] = dict(norm1_g=d_g1, conv_w=d_cw, q_norm_g=both_heads(d_qg), k_norm_g=both_heads(d_kg), sinks=d_sk,
                         conv_out_g=d_cog, attn_out_g=d_aog, norm2_g=d_g2)
    grad_x = dx[None]

    flat = jnp.concatenate([loss_row[0, 0:1]] + [gsmall[l][n].reshape(-1) for l in range(DEPTH) for n in SMALL_NAMES])
    flat = jnp.pad(flat, (0, SMALL_ROWS * 128 - flat.shape[0])).reshape(SMALL_ROWS, 128)
    flat = all_reduce_small(flat, "all_reduce_small_grads").reshape(-1)
    loss = flat[0]
    gs = {n: [] for n in SMALL_NAMES}
    off = 1
    for l in range(DEPTH):
        for n, size in zip(SMALL_NAMES, SMALL_SIZES):
            gs[n].append(flat[off:off + size])
            off += size
    gs = {n: jnp.stack(v) for n, v in gs.items()}
    g_conv = lax.dynamic_slice(gs["conv_w"].reshape(DEPTH, 3, CONV_CH), (0, 0, me * HEAD), (DEPTH, 3, HEAD))

    gs["conv_w"] = g_conv
    params = dict(norm1_g=(norm1_g, m_norm1_g, v_norm1_g), conv_w=(conv_w, m_conv_w, v_conv_w),
                  q_norm_g=(q_norm_g, m_q_norm_g, v_q_norm_g), k_norm_g=(k_norm_g, m_k_norm_g, v_k_norm_g),
                  sinks=(sinks, m_sinks, v_sinks), conv_out_g=(conv_out_g, m_conv_out_g, v_conv_out_g),
                  attn_out_g=(attn_out_g, m_attn_out_g, v_attn_out_g), norm2_g=(norm2_g, m_norm2_g, v_norm2_g))
    names = ("norm1_g", "w_in", "conv_w", "q_norm_g", "k_norm_g", "sinks", "conv_out_g", "attn_out_g", "w_o",
             "norm2_g", "w_gate", "w_up", "w_down")

    out = {}
    for n in names:
        if n in shard:
            out[n] = tuple(t(r) for r in stepped[n]) if n in ("w_in", "w_gate", "w_up") else stepped[n]
        else:
            w, m, v = params[n]
            two_d = (-1, w.shape[-1])
            d, mn, vn = adamw(w.reshape(two_d), gs[n].reshape(two_d), m.reshape(two_d), v.reshape(two_d), f"adamw_{n}")
            out[n] = (gs[n].reshape(w.shape), d.reshape(w.shape), mn.reshape(w.shape), vn.reshape(w.shape))
    return (loss, grad_x, *[out[n][i] for i in range(4) for n in names])
```

```python
import functools

import jax
import jax.numpy as jnp
from jax import lax
from jax.experimental import pallas as pl
from jax.experimental.pallas import tpu as pltpu

F32 = jnp.float32
BF16 = jnp.bfloat16

D_MODEL = 1024
CONV_CH = 512
ATTN_W = 512
N_Q = 8
N_KV = 2
GRP = N_Q // N_KV
HEAD = 64
IN_COLS = 2304
D_FF = 2816
BLK = 128
O_Q = 3 * CONV_CH
O_K = O_Q + ATTN_W
O_V = O_K + N_KV * HEAD
EPS = 1e-6
NEG_INF = -1e30
SCALE = HEAD ** -0.5
N_DEV = 8
DEPTH = 2

ADAM_LR = 0.001
ADAM_B1 = 0.9
ADAM_B2 = 0.999
ADAM_EPS = 1e-08
ADAM_WD = 0.01
ADAM_STEP = 10

VMEM_LIMIT = 56 * 1024 * 1024
TM = 512
MESH_T = pl.DeviceIdType.MESH

ROWS_IN, ROWS_O, ROWS_FF = IN_COLS // N_DEV, D_MODEL // N_DEV, D_FF // N_DEV


ANY = pl.BlockSpec(memory_space=pl.ANY)
HBM = pl.BlockSpec(memory_space=pltpu.HBM)
SEM = pl.BlockSpec(memory_space=pltpu.SEMAPHORE)


def _cparams(sem):
    return pltpu.CompilerParams(dimension_semantics=sem, vmem_limit_bytes=VMEM_LIMIT)


def _pallas_after(dep, body, *args, in_specs, **kw):
    if dep is None:
        return pl.pallas_call(body, in_specs=in_specs, **kw)(*args)

    def after_dep(dep_ref, *refs):
        body(*refs)

    return pl.pallas_call(after_dep, in_specs=[ANY, *in_specs], **kw)(dep, *args)


def _dot(a, b):
    return jnp.dot(a, b, preferred_element_type=F32)


def _dot_nt(a, b):
    return lax.dot_general(a, b, (((1,), (1,)), ((), ())), preferred_element_type=F32)


def _dot_tn(a, b):
    return lax.dot_general(a, b, (((0,), (0,)), ((), ())), preferred_element_type=F32)


LANES = 128


def _row_reduce(v, op, reduce):
    w = v.shape[-1]
    if w > LANES and w % LANES == 0:
        acc = v[:, 0:LANES]
        for c in range(1, w // LANES):
            acc = op(acc, v[:, LANES * c:LANES * (c + 1)])
        v = acc
    return reduce(v, axis=-1, keepdims=True)


def _row_sum(v):
    return _row_reduce(v, jnp.add, jnp.sum)


def _row_mean(v):
    return _row_sum(v) * (1.0 / v.shape[-1])


def _rms(v):
    return lax.rsqrt(_row_mean(v * v) + EPS)


def _sigmoid(v):
    return 1.0 / (1.0 + jnp.exp(-v))


def _rms_bwd(dyv, xh, r, g):
    dxh = dyv * g
    return r * (dxh - xh * _row_mean(dxh * xh))


def norm_proj(x, g, w_t, name, dep=None):
    S, N = x.shape[0], w_t.shape[0]

    def body(x_ref, g_ref, w_ref, h_ref, p_ref):
        xf = x_ref[...]
        h = ((xf * _rms(xf)) * g_ref[...]).astype(BF16)
        h_ref[...] = h
        p_ref[...] = _dot_nt(h, w_ref[...]).astype(BF16)

    return _pallas_after(
        dep, body, x, g, w_t, name=name, grid=(S // TM,),
        in_specs=[pl.BlockSpec((TM, D_MODEL), lambda i: (i, 0)),
                  pl.BlockSpec((1, D_MODEL), lambda i: (0, 0)),
                  pl.BlockSpec((N, D_MODEL), lambda i: (0, 0))],
        out_specs=[pl.BlockSpec((TM, D_MODEL), lambda i: (i, 0)),
                   pl.BlockSpec((TM, N), lambda i: (i, 0))],
        out_shape=[jax.ShapeDtypeStruct((S, D_MODEL), BF16), jax.ShapeDtypeStruct((S, N), BF16)],
        compiler_params=_cparams(("parallel",)),
    )


def band_bias():
    qi = lax.broadcasted_iota(jnp.int32, (BLK, 2 * BLK), 0)
    kj = lax.broadcasted_iota(jnp.int32, (BLK, 2 * BLK), 1)
    diff = qi + BLK - kj
    valid = (diff >= 0) & (diff < BLK)
    return jnp.stack([jnp.where(valid & (kj >= BLK), 0.0, NEG_INF), jnp.where(valid, 0.0, NEG_INF)]).astype(F32)


def _softmax_with_sink(s, sink):
    m = jnp.maximum(_row_reduce(s, jnp.maximum, jnp.max), sink)
    p = jnp.exp(s - m)
    es = jnp.exp(sink - m)
    inv = 1.0 / (_row_sum(p) + es)
    return p * inv, es * inv


PAIR = 2 * HEAD


def _low_half():
    return lax.broadcasted_iota(jnp.int32, (1, PAIR), 1) < HEAD


def _half_sums(v, low):
    return (jnp.sum(jnp.where(low, v, 0.0), axis=-1, keepdims=True),
            jnp.sum(jnp.where(low, 0.0, v), axis=-1, keepdims=True))


def _pair_mean(v, low):
    e, o = _half_sums(v, low)
    return jnp.where(low, e, o) * (1.0 / HEAD)


def _pair_rms(v, low):
    return lax.rsqrt(_pair_mean(v * v, low) + EPS)


def _one_head_in_both_halves(pair, low):
    swapped = pltpu.roll(pair, HEAD, 1)
    return jnp.where(low, pair, swapped), jnp.where(low, swapped, pair)


SUB_FWD, SUB_BWD = 2, 4


def _kv_pairs(p_ref, kvp_ref):
    k = jnp.concatenate([kvp_ref[:, 0:PAIR], p_ref[:, O_K:O_K + PAIR]], axis=0).astype(F32)
    v = jnp.concatenate([kvp_ref[:, PAIR:2 * PAIR], p_ref[:, O_V:O_V + PAIR]], axis=0)
    return k, v


def _band(s):
    return slice(BLK * s, BLK * s + 2 * BLK)


def _block(s):
    return slice(BLK * s, BLK * (s + 1))


def mixer_fwd(proj, bias, sinks, conv_w, qg, kg, cog, aog, name, residual=None, w_o=None):
    S = proj.shape[0]
    SUB, ROWS = SUB_FWD, SUB_FWD * BLK
    steps = S // ROWS
    project = w_o is not None

    def body(sinks_ref, p_ref, kvp_ref, bias_ref, cw_ref, qg_ref, kg_ref, cog_ref, aog_ref, *rest):
        if project:
            x_ref, wo_ref, mix_ref, y_ref, ao_ref, x1_ref, ucar = rest
        else:
            mix_ref, y_ref, ao_ref, ucar = rest
        n = pl.program_id(0)

        @pl.when(n == 0)
        def _():
            ucar[...] = jnp.zeros_like(ucar)

        bg = p_ref[:, 0:CONV_CH].astype(F32)
        u = p_ref[:, CONV_CH:2 * CONV_CH].astype(F32) * p_ref[:, 2 * CONV_CH:3 * CONV_CH].astype(F32)
        row = lax.broadcasted_iota(jnp.int32, (ROWS, CONV_CH), 0)
        prev = ucar[...]
        u1 = jnp.where(row == 0, prev[7:8, :], pltpu.roll(u, 1, 0))
        u2 = jnp.where(row == 0, prev[6:7, :], jnp.where(row == 1, prev[7:8, :], pltpu.roll(u, 2, 0)))
        ucar[...] = u[ROWS - 8:ROWS, :]
        y = cw_ref[0:1, :] * u2 + cw_ref[1:2, :] * u1 + cw_ref[2:3, :] * u
        y_ref[...] = y.astype(BF16)
        co = bg * y
        mix_conv = ((co * _rms(co)) * cog_ref[...]).astype(BF16)
        mix_ref[:, 0:CONV_CH] = mix_conv
        if project:
            x1 = x_ref[...] + _dot(mix_conv, wo_ref[0:CONV_CH, :])

        low = _low_half()
        q_gain = qg_ref[...] * SCALE
        k, v = _kv_pairs(p_ref, kvp_ref)
        kn = ((k * _pair_rms(k, low)) * kg_ref[...]).astype(BF16)
        k_of = _one_head_in_both_halves(kn, low)
        v_of = _one_head_in_both_halves(v, low)
        biases = [bias_ref[jnp.minimum(n, 1)]] + [bias_ref[1]] * (SUB - 1)
        units = [(s, hq) for s in range(SUB) for hq in range(N_Q)]
        q_one = []
        for pair in range(N_Q // 2):
            q = p_ref[:, O_Q + PAIR * pair:O_Q + PAIR * (pair + 1)].astype(F32)
            qn = ((q * _pair_rms(q, low)) * q_gain).astype(BF16)
            q_one += [jnp.where(mine, qn, jnp.zeros_like(qn)) for mine in (low, jnp.logical_not(low))]
        scores = [_dot_nt(q_one[hq][_block(s)], k_of[hq // GRP][_band(s)]) + biases[s] for s, hq in units]
        probs = [_softmax_with_sink(sc, sinks_ref[hq])[0].astype(BF16) for (s, hq), sc in zip(units, scores)]
        o = {u_: _dot(pn, v_of[u_[1] // GRP][_band(u_[0])]) for u_, pn in zip(units, probs)}
        ao = jnp.concatenate(
            [jnp.concatenate([jnp.where(low, o[s, 2 * pair], o[s, 2 * pair + 1]) for pair in range(N_Q // 2)], axis=1)
             for s in range(SUB)], axis=0)
        ao_ref[...] = ao.astype(BF16)
        mix_attn = ((ao * _rms(ao)) * aog_ref[...]).astype(BF16)
        mix_ref[:, CONV_CH:] = mix_attn
        if project:
            x1_ref[...] = x1 + _dot(mix_attn, wo_ref[CONV_CH:, :])

    small = lambda shape: pl.BlockSpec(shape, lambda n: (0, 0))
    rows = lambda w: pl.BlockSpec((ROWS, w), lambda n: (n, 0))
    return pl.pallas_call(
        body, name=name, grid=(steps,),
        in_specs=[pl.BlockSpec(memory_space=pltpu.SMEM),
                  rows(IN_COLS),
                  pl.BlockSpec((BLK, 2 * PAIR), lambda n: (jnp.maximum(SUB * n - 1, 0), O_K // (2 * PAIR))),
                  pl.BlockSpec((2, BLK, 2 * BLK), lambda n: (0, 0, 0)),
                  small((3, CONV_CH)), small((1, PAIR)), small((1, PAIR)), small((1, CONV_CH)), small((1, ATTN_W))]
        + ([rows(D_MODEL), small((D_MODEL, D_MODEL))] if project else []),
        out_specs=[rows(D_MODEL), rows(CONV_CH), rows(ATTN_W)] + ([rows(D_MODEL)] if project else []),
        out_shape=[jax.ShapeDtypeStruct((S, D_MODEL), BF16), jax.ShapeDtypeStruct((S, CONV_CH), BF16),
                   jax.ShapeDtypeStruct((S, ATTN_W), BF16)]
        + ([jax.ShapeDtypeStruct((S, D_MODEL), F32)] if project else []),
        scratch_shapes=[pltpu.VMEM((8, CONV_CH), F32)],
        compiler_params=_cparams(("arbitrary",)),
    )(sinks, proj, proj, bias, conv_w, qg, kg, cog, aog, *([residual, w_o] if project else []))


def matmul_residual(a, w, res, name, dep=None):
    S, K = a.shape
    N = w.shape[1]

    def body(a_ref, w_ref, r_ref, o_ref):
        o_ref[...] = r_ref[...] + _dot(a_ref[...], w_ref[...])

    return _pallas_after(
        dep, body, a, w, res, name=name, grid=(S // TM,),
        in_specs=[pl.BlockSpec((TM, K), lambda i: (i, 0)), pl.BlockSpec((K, N), lambda i: (0, 0)),
                  pl.BlockSpec((TM, N), lambda i: (i, 0))],
        out_specs=pl.BlockSpec((TM, N), lambda i: (i, 0)),
        out_shape=jax.ShapeDtypeStruct((S, N), F32),
        compiler_params=_cparams(("parallel",)),
    )


TF = 1408


def ffn_fwd(x1, g, wg, wu, wd, name, dep=None, tgt=None):
    S = x1.shape[0]
    ni = S // TM
    with_loss = tgt is not None
    row = pl.BlockSpec((TM, D_MODEL), lambda i: (i, 0))
    vec = pl.BlockSpec((1, D_MODEL), lambda i: (0, 0))
    acts = [jax.ShapeDtypeStruct((S, D_FF), BF16)] * 3

    def half_weights(j):
        return [pl.BlockSpec((TF, D_MODEL), lambda i: (j, 0))] * 3

    def swiglu(h2, wg_ref, wu_ref, a_ref, b_ref, f_ref):
        a = _dot_nt(h2, wg_ref[...])
        b = _dot_nt(h2, wu_ref[...])
        a_ref[...] = a.astype(BF16)
        b_ref[...] = b.astype(BF16)
        f = ((a * _sigmoid(a)) * b).astype(BF16)
        f_ref[...] = f
        return f

    def first(x_ref, g_ref, wg_ref, wu_ref, wd_ref, h2_ref, a_ref, b_ref, f_ref, part_ref):
        xf = x_ref[...]
        h2 = ((xf * _rms(xf)) * g_ref[...]).astype(BF16)
        h2_ref[...] = h2
        part_ref[...] = xf + _dot(swiglu(h2, wg_ref, wu_ref, a_ref, b_ref, f_ref), wd_ref[...])

    h2, a, b, f, part = _pallas_after(
        dep, first, x1, g, wg, wu, wd, name=name + "_a", grid=(ni,),
        in_specs=[row, vec] + half_weights(0),
        out_specs=[row] + [pl.BlockSpec((TM, TF), lambda i: (i, 0))] * 3 + [row],
        out_shape=[jax.ShapeDtypeStruct((S, D_MODEL), BF16)] + acts + [jax.ShapeDtypeStruct((S, D_MODEL), F32)],
        compiler_params=_cparams(("parallel",)),
    )

    def second(h2_ref, part_ref, wg_ref, wu_ref, wd_ref, *rest):
        rest = rest[3:]
        if with_loss:
            t_ref, a_ref, b_ref, f_ref, o_ref, l_ref, sq = rest
        else:
            a_ref, b_ref, f_ref, o_ref = rest
        i = pl.program_id(0)
        out = part_ref[...] + _dot(swiglu(h2_ref[...], wg_ref, wu_ref, a_ref, b_ref, f_ref), wd_ref[...])
        if not with_loss:
            o_ref[...] = out
            return
        e = out - t_ref[...]
        o_ref[...] = e * (1.0 / D_MODEL)
        col = jnp.sum(e * e, axis=0, keepdims=True)

        @pl.when(i == 0)
        def _():
            sq[...] = col

        @pl.when(i > 0)
        def _():
            sq[...] += col

        @pl.when(i == ni - 1)
        def _():
            l_ref[...] = jnp.full((1, 128), jnp.sum(sq[...]) * (0.5 / D_MODEL), F32)

    results = pl.pallas_call(
        second, name=name + "_b", grid=(ni,),
        in_specs=[row, row] + half_weights(1) + [ANY] * 3 + ([row] if with_loss else []),
        out_specs=[pl.BlockSpec((TM, TF), lambda i: (i, 1))] * 3 + [row]
        + ([pl.BlockSpec((1, 128), lambda i: (0, 0))] if with_loss else []),
        out_shape=acts + [jax.ShapeDtypeStruct((S, D_MODEL), F32)]
        + ([jax.ShapeDtypeStruct((1, 128), F32)] if with_loss else []),
        input_output_aliases={5: 0, 6: 1, 7: 2},
        scratch_shapes=[pltpu.VMEM((1, D_MODEL), F32)] if with_loss else [],
        compiler_params=_cparams(("arbitrary",) if with_loss else ("parallel",)),
    )(h2, part, wg, wu, wd, a, b, f, *([tgt] if with_loss else []))
    return (h2, *results)


def ffn_bwd(dx2, a, b, x1, g, wg, wu, wd, name, dep=None):
    S = dx2.shape[0]
    ni = S // TM
    row = pl.BlockSpec((TM, D_MODEL), lambda i: (i, 0))
    vec = pl.BlockSpec((1, D_MODEL), lambda i: (0, 0))
    acts = [jax.ShapeDtypeStruct((S, D_FF), BF16)] * 2

    def half(j):
        return [pl.BlockSpec((TM, TF), lambda i: (i, j))] * 2, [pl.BlockSpec((TF, D_MODEL), lambda i: (j, 0))] * 3

    def through_half(dx_ref, a_ref, b_ref, wg_ref, wu_ref, wd_ref, da_ref, db_ref):
        df = _dot_nt(dx_ref[...].astype(BF16), wd_ref[...])
        av = a_ref[...].astype(F32)
        bv = b_ref[...].astype(F32)
        sg = _sigmoid(av)
        da = ((df * bv) * (sg * (1.0 + av * (1.0 - sg)))).astype(BF16)
        db = (df * (av * sg)).astype(BF16)
        da_ref[...] = da
        db_ref[...] = db
        return _dot(da, wg_ref[...]) + _dot(db, wu_ref[...])

    def first(dx_ref, a_ref, b_ref, wg_ref, wu_ref, wd_ref, da_ref, db_ref, part_ref):
        part_ref[...] = through_half(dx_ref, a_ref, b_ref, wg_ref, wu_ref, wd_ref, da_ref, db_ref)

    tiles, weights = half(0)
    da, db, part = _pallas_after(
        dep, first, dx2, a, b, wg, wu, wd, name=name + "_a", grid=(ni,),
        in_specs=[row] + tiles + weights, out_specs=tiles + [row],
        out_shape=acts + [jax.ShapeDtypeStruct((S, D_MODEL), F32)],
        compiler_params=_cparams(("parallel",)),
    )

    def second(dx_ref, a_ref, b_ref, wg_ref, wu_ref, wd_ref, part_ref, x_ref, g_ref, da_prev, db_prev,
               da_ref, db_ref, dx1_ref, dg_ref):
        i = pl.program_id(0)

        @pl.when(i == 0)
        def _():
            dg_ref[...] = jnp.zeros_like(dg_ref)

        dh = part_ref[...] + through_half(dx_ref, a_ref, b_ref, wg_ref, wu_ref, wd_ref, da_ref, db_ref)
        xf = x_ref[...]
        r = _rms(xf)
        xh = xf * r
        dg_ref[...] += jnp.sum(dh * xh, axis=0, keepdims=True)
        dx1_ref[...] = dx_ref[...] + _rms_bwd(dh, xh, r, g_ref[...])

    tiles, weights = half(1)
    return pl.pallas_call(
        second, name=name + "_b", grid=(ni,),
        in_specs=[row] + tiles + weights + [row, row, vec, ANY, ANY],
        out_specs=tiles + [row, vec],
        out_shape=acts + [jax.ShapeDtypeStruct((S, D_MODEL), F32), jax.ShapeDtypeStruct((1, D_MODEL), F32)],
        input_output_aliases={9: 0, 10: 1},
        compiler_params=_cparams(("arbitrary",)),
    )(dx2, a, b, wg, wu, wd, part, x1, g, da, db)


def grad_weight(lhs, rhs, tm, name, dep=None):
    S, N = rhs.shape
    M = lhs[0].shape[1]
    tk = 1024
    nk = S // tk
    n_l = n_o = len(lhs)

    def body(*refs):
        l_refs, r_ref = refs[:n_l], refs[n_l]
        o_refs, accs = refs[n_l + 1:n_l + 1 + n_o], refs[n_l + 1 + n_o:]
        k = pl.program_id(1)
        rv = r_ref[...].astype(BF16)
        for l_ref, o_ref, acc in zip(l_refs, o_refs, accs):
            @pl.when(k == 0)
            def _():
                acc[...] = jnp.zeros_like(acc)

            acc[...] += _dot_tn(l_ref[...], rv)

            @pl.when(k == nk - 1)
            def _():
                o_ref[...] = acc[...].astype(BF16)

    return _pallas_after(
        dep, body, *lhs, rhs, name=name, grid=(M // tm, nk),
        in_specs=[pl.BlockSpec((tk, tm), lambda i, k: (k, i)) for _ in lhs] + [pl.BlockSpec((tk, N), lambda i, k: (k, 0))],
        out_specs=[pl.BlockSpec((tm, N), lambda i, k: (i, 0))] * n_o,
        out_shape=[jax.ShapeDtypeStruct((M, N), BF16)] * n_o,
        scratch_shapes=[pltpu.VMEM((tm, N), F32)] * n_o,
        compiler_params=_cparams(("parallel", "arbitrary")),
    )


def proj_bwd(dproj, w, x, g, dres, name, dep=None):
    S, N = dproj.shape

    def body(dp_ref, w_ref, x_ref, g_ref, dr_ref, dx_ref, dg_ref):
        i = pl.program_id(0)

        @pl.when(i == 0)
        def _():
            dg_ref[...] = jnp.zeros_like(dg_ref)

        dh = _dot(dp_ref[...], w_ref[...])
        xf = x_ref[...]
        r = _rms(xf)
        xh = xf * r
        dg_ref[...] += jnp.sum(dh * xh, axis=0, keepdims=True)
        dx_ref[...] = dr_ref[...] + _rms_bwd(dh, xh, r, g_ref[...])

    return _pallas_after(
        dep, body, dproj, w, x, g, dres, name=name, grid=(S // TM,),
        in_specs=[pl.BlockSpec((TM, N), lambda i: (i, 0)), pl.BlockSpec((N, D_MODEL), lambda i: (0, 0)),
                  pl.BlockSpec((TM, D_MODEL), lambda i: (i, 0)), pl.BlockSpec((1, D_MODEL), lambda i: (0, 0)),
                  pl.BlockSpec((TM, D_MODEL), lambda i: (i, 0))],
        out_specs=[pl.BlockSpec((TM, D_MODEL), lambda i: (i, 0)), pl.BlockSpec((1, D_MODEL), lambda i: (0, 0))],
        out_shape=[jax.ShapeDtypeStruct((S, D_MODEL), F32), jax.ShapeDtypeStruct((1, D_MODEL), F32)],
        compiler_params=_cparams(("arbitrary",)),
    )


def mixer_bwd(proj, bias, y, ao, dx1, w_o, sinks, conv_w, qg, kg, cog, aog, name, dep=None):
    S = proj.shape[0]
    SUB, ROWS = SUB_BWD, SUB_BWD * BLK
    steps = S // ROWS
    KV_W = 2 * PAIR

    def body(sinks_ref, p_ref, kvp_ref, bias_ref, y_ref, ao_ref, dx_first_ref, dx_next_ref, wo_ref, cw_ref, qg_ref,
             kg_ref, cog_ref, aog_ref, dp_ref, dcw_ref, dqg_ref, dkg_ref, dsk_ref, dcog_ref, daog_ref,
             dycar, kcar, vcar, dmix_scr):
        step = pl.program_id(0)

        @pl.when(step == 0)
        def _():
            dmix_scr[...] = _dot_nt(dx_first_ref[...].astype(BF16), wo_ref[...])
            dycar[...] = jnp.zeros_like(dycar)
            kcar[...] = jnp.zeros_like(kcar)
            vcar[...] = jnp.zeros_like(vcar)
            dcw_ref[...] = jnp.zeros_like(dcw_ref)
            dqg_ref[...] = jnp.zeros_like(dqg_ref)
            dkg_ref[...] = jnp.zeros_like(dkg_ref)
            dsk_ref[...] = jnp.zeros_like(dsk_ref)
            dcog_ref[...] = jnp.zeros_like(dcog_ref)
            daog_ref[...] = jnp.zeros_like(daog_ref)

        dma = dmix_scr[:, CONV_CH:]

        aov = ao_ref[...].astype(F32)
        ra = _rms(aov)
        ah = aov * ra
        daog_ref[...] += jnp.sum(dma * ah, axis=0, keepdims=True)
        dao = _rms_bwd(dma, ah, ra, aog_ref[...])

        low = _low_half()
        high = jnp.logical_not(low)
        q_gain = qg_ref[...] * SCALE
        k, v = _kv_pairs(p_ref, kvp_ref)
        rk = _pair_rms(k, low)
        kh = k * rk
        k_of = _one_head_in_both_halves((kh * kg_ref[...]).astype(BF16), low)
        v_of = _one_head_in_both_halves(v, low)
        lane8 = lax.broadcasted_iota(jnp.int32, (1, N_Q), 1)
        dsk = jnp.zeros((1, N_Q), F32)
        dqg = jnp.zeros((1, PAIR), F32)
        biases = [bias_ref[jnp.minimum(steps - 1 - step, 1)]] + [bias_ref[1]] * (SUB - 1)
        units = [(s, hq) for s in range(SUB) for hq in range(N_Q)]
        rq, qh, q_one, do_one, delta = [], [], [], [], []
        for pair in range(N_Q // 2):
            cols = slice(PAIR * pair, PAIR * (pair + 1))
            q = p_ref[:, O_Q + PAIR * pair:O_Q + PAIR * (pair + 1)].astype(F32)
            rq.append(_pair_rms(q, low))
            qh.append(q * rq[pair])
            qn = (qh[pair] * q_gain).astype(BF16)
            do = dao[:, cols]
            do_b = do.astype(BF16)
            delta += _half_sums(do * aov[:, cols], low)
            for mine in (low, high):
                q_one.append(jnp.where(mine, qn, jnp.zeros_like(qn)))
                do_one.append(jnp.where(mine, do_b, jnp.zeros_like(do_b)))
        scores = {u: _dot_nt(q_one[u[1]][_block(u[0])], k_of[u[1] // GRP][_band(u[0])]) + biases[u[0]] for u in units}
        dprobs = {u: _dot_nt(do_one[u[1]][_block(u[0])], v_of[u[1] // GRP][_band(u[0])]) for u in units}
        probs = {u: _softmax_with_sink(scores[u], sinks_ref[u[1]]) for u in units}
        ds = {u: (probs[u][0] * (dprobs[u] - delta[u[1]][_block(u[0])])).astype(BF16) for u in units}
        dv_t = {(s, h): jnp.zeros((PAIR, 2 * BLK), F32) for s in range(SUB) for h in range(N_KV)}
        dkn_t = dict(dv_t)
        for s, hq in units:
            dsk = dsk - jnp.where(lane8 == hq, jnp.sum(probs[s, hq][1] * delta[hq][_block(s)]), 0.0)
            dv_t[s, hq // GRP] = dv_t[s, hq // GRP] + _dot_tn(do_one[hq][_block(s)], probs[s, hq][0].astype(BF16))
            dkn_t[s, hq // GRP] = dkn_t[s, hq // GRP] + _dot_tn(q_one[hq][_block(s)], ds[s, hq])
        dqn_of = {u: _dot(ds[u], k_of[u[1] // GRP][_band(u[0])]) for u in units}
        for pair in range(N_Q // 2):
            dqn = jnp.concatenate([jnp.where(low, dqn_of[s, 2 * pair], dqn_of[s, 2 * pair + 1]) for s in range(SUB)],
                                  axis=0)
            dqg = dqg + jnp.sum(dqn * qh[pair], axis=0, keepdims=True)
            dqh = dqn * q_gain
            dp_ref[:, O_Q + PAIR * pair:O_Q + PAIR * (pair + 1)] = (
                rq[pair] * (dqh - qh[pair] * _pair_mean(dqh * qh[pair], low))).astype(BF16)

        def over_key_rows(parts):
            bands = [jnp.concatenate([parts[s, h][:HEAD] + parts[s, h][HEAD:] for h in range(N_KV)], axis=0).T
                     for s in range(SUB)]
            pieces = [bands[0][:BLK]]
            pieces += [bands[s - 1][BLK:] + bands[s][:BLK] for s in range(1, SUB)]
            pieces.append(bands[SUB - 1][BLK:])
            return jnp.concatenate(pieces, axis=0)

        dv = over_key_rows(dv_t)
        dkn = over_key_rows(dkn_t)
        dkg_ref[...] += jnp.sum(dkn * kh, axis=0, keepdims=True)
        dkh = dkn * kg_ref[...]
        dk = rk * (dkh - kh * _pair_mean(dkh * kh, low))
        last = slice(ROWS, ROWS + BLK)
        dp_ref[:, O_K:O_V] = jnp.concatenate([dk[BLK:ROWS], dk[last] + kcar[...]], axis=0).astype(BF16)
        dp_ref[:, O_V:] = jnp.concatenate([dv[BLK:ROWS], dv[last] + vcar[...]], axis=0).astype(BF16)
        kcar[...] = dk[:BLK, :]
        vcar[...] = dv[:BLK, :]
        dsk_ref[...] += dsk
        dqg_ref[...] += dqg * SCALE

        bg = p_ref[:, 0:CONV_CH].astype(F32)
        cg = p_ref[:, CONV_CH:2 * CONV_CH].astype(F32)
        hc = p_ref[:, 2 * CONV_CH:3 * CONV_CH].astype(F32)
        yv = y_ref[...].astype(F32)
        dmc = dmix_scr[:, 0:CONV_CH]
        co = bg * yv
        rc = _rms(co)
        ch = co * rc
        dcog_ref[...] += jnp.sum(dmc * ch, axis=0, keepdims=True)
        dco = _rms_bwd(dmc, ch, rc, cog_ref[...])
        dp_ref[:, 0:CONV_CH] = (dco * yv).astype(BF16)
        dy = dco * bg
        row = lax.broadcasted_iota(jnp.int32, (ROWS, CONV_CH), 0)
        nxt = dycar[...]
        dy1 = jnp.where(row == ROWS - 1, nxt[0:1, :], pltpu.roll(dy, ROWS - 1, 0))
        dy2 = jnp.where(row == ROWS - 2, nxt[0:1, :],
                        jnp.where(row == ROWS - 1, nxt[1:2, :], pltpu.roll(dy, ROWS - 2, 0)))
        dycar[...] = dy[0:8, :]
        du = cw_ref[2:3, :] * dy + cw_ref[1:2, :] * dy1 + cw_ref[0:1, :] * dy2
        dp_ref[:, CONV_CH:2 * CONV_CH] = (du * hc).astype(BF16)
        dp_ref[:, 2 * CONV_CH:3 * CONV_CH] = (du * cg).astype(BF16)
        u = cg * hc
        dcw_ref[0:1, :] += jnp.sum(dy2 * u, axis=0, keepdims=True)
        dcw_ref[1:2, :] += jnp.sum(dy1 * u, axis=0, keepdims=True)
        dcw_ref[2:3, :] += jnp.sum(dy * u, axis=0, keepdims=True)

        dmix_scr[...] = _dot_nt(dx_next_ref[...].astype(BF16), wo_ref[...])

    small = lambda shape: pl.BlockSpec(shape, lambda s: (0, 0))
    blk = lambda w: pl.BlockSpec((ROWS, w), lambda s: (steps - 1 - s, 0))
    return _pallas_after(
        dep, body, sinks, proj, proj, bias, y, ao, dx1, dx1, w_o, conv_w, qg, kg, cog, aog, name=name, grid=(steps,),
        in_specs=[pl.BlockSpec(memory_space=pltpu.SMEM),
                  blk(IN_COLS),
                  pl.BlockSpec((BLK, KV_W), lambda s: (jnp.maximum(SUB * (steps - 1 - s) - 1, 0), O_K // KV_W)),
                  pl.BlockSpec((2, BLK, 2 * BLK), lambda s: (0, 0, 0)),
                  blk(CONV_CH), blk(ATTN_W),
                  pl.BlockSpec((ROWS, D_MODEL), lambda s: (steps - 1, 0)),
                  pl.BlockSpec((ROWS, D_MODEL), lambda s: (jnp.maximum(steps - 2 - s, 0), 0)),
                  small((D_MODEL, D_MODEL)),
                  small((3, CONV_CH)), small((1, PAIR)), small((1, PAIR)), small((1, CONV_CH)), small((1, ATTN_W))],
        out_specs=[blk(IN_COLS), small((3, CONV_CH)), small((1, PAIR)), small((1, PAIR)), small((1, N_Q)),
                   small((1, CONV_CH)), small((1, ATTN_W))],
        out_shape=[jax.ShapeDtypeStruct((S, IN_COLS), BF16), jax.ShapeDtypeStruct((3, CONV_CH), F32),
                   jax.ShapeDtypeStruct((1, PAIR), F32), jax.ShapeDtypeStruct((1, PAIR), F32),
                   jax.ShapeDtypeStruct((1, N_Q), F32), jax.ShapeDtypeStruct((1, CONV_CH), F32),
                   jax.ShapeDtypeStruct((1, ATTN_W), F32)],
        scratch_shapes=[pltpu.VMEM((8, CONV_CH), F32), pltpu.VMEM((BLK, PAIR), F32), pltpu.VMEM((BLK, PAIR), F32),
                        pltpu.VMEM((ROWS, D_MODEL), F32)],
        compiler_params=_cparams(("arbitrary",)),
    )


OTHER_CHIPS = ((1, 0), (0, 1), (1, 1))


def _position():
    return lax.axis_index("x"), lax.axis_index("y"), lax.axis_index("c")


def all_gather(shards, name):
    n = len(shards)

    def body(*refs):
        x_refs, out_refs = refs[:n], refs[n:2 * n]
        send_sems, recv_sems, local_sems = refs[2 * n:]
        x, y, c = _position()
        me, sibling = (x, y, c), (x, y, 1 - c)
        chips = [(x ^ mx, y ^ my) for mx, my in OTHER_CHIPS]

        def slab(a, px, py, pc):
            return out_refs[a].at[4 * px + 2 * py + pc]

        def copy(a, k, block, to, src=None):
            return pltpu.make_async_remote_copy(
                src_ref=slab(a, *block) if src is None else src, dst_ref=slab(a, *block),
                send_sem=send_sems.at[7 * a + k], recv_sem=recv_sems.at[7 * a + k], device_id=to, device_id_type=MESH_T)

        mine = [pltpu.make_async_copy(x_refs[a], slab(a, *me), local_sems.at[a]) for a in range(n)]
        for cp in mine:
            cp.start()
        first = []
        for a in range(n):
            first.append(copy(a, 0, me, sibling, src=x_refs[a]))
            first += [copy(a, 1 + j, me, (*chip, c), src=x_refs[a]) for j, chip in enumerate(chips)]
        for cp in first:
            cp.start()
        passed = []
        for a in range(n):
            for j, chip in enumerate(chips):
                copy(a, 1 + j, (*chip, c), me).wait_recv()
                passed.append(copy(a, 4 + j, (*chip, c), sibling))
                passed[-1].start()
        for a in range(n):
            copy(a, 0, sibling, me).wait_recv()
            for j, chip in enumerate(chips):
                copy(a, 4 + j, (*chip, 1 - c), me).wait_recv()
        for cp in first + passed:
            cp.wait_send()
        for cp in mine:
            cp.wait()

    return pl.pallas_call(
        body, name=name, in_specs=[ANY] * n, out_specs=[ANY] * n,
        out_shape=[jax.ShapeDtypeStruct((N_DEV, *s.shape), s.dtype) for s in shards],
        scratch_shapes=[pltpu.SemaphoreType.DMA((7 * n,)), pltpu.SemaphoreType.DMA((7 * n,)),
                        pltpu.SemaphoreType.DMA((n,))],
    )(*shards)


class SplitCopy:
    def __init__(self, name, arrays, n_copies, plan, after=None):
        n = len(arrays)
        self.name, self.n, self.n_copies, self.plan = name, n, n_copies, plan
        extra = [] if after is None else [after]

        def body(*refs):
            in_refs = refs[:n]
            send_sems, recv_sems = refs[n + len(extra)], refs[n + len(extra) + 1]
            token = refs[2 * n + len(extra) + 2]
            for k, (src, dst, to) in enumerate(plan(_position(), in_refs)):
                pltpu.make_async_remote_copy(src_ref=src, dst_ref=dst, send_sem=send_sems.at[k],
                                             recv_sem=recv_sems.at[k], device_id=to, device_id_type=MESH_T).start()
            token[...] = jnp.zeros_like(token)

        outs = pl.pallas_call(
            body, name=name + "_start",
            out_shape=(pltpu.SemaphoreType.DMA((n_copies,)), pltpu.SemaphoreType.DMA((n_copies,)),
                       *[pltpu.HBM(a.shape, a.dtype) for a in arrays], jax.ShapeDtypeStruct((8, 128), F32)),
            in_specs=[HBM] * n + [ANY] * len(extra),
            out_specs=(SEM, SEM, *[HBM] * n, pl.BlockSpec(memory_space=pltpu.VMEM)),
            input_output_aliases={i: 2 + i for i in range(n)},
            compiler_params=pltpu.CompilerParams(has_side_effects=pltpu.SideEffectType.DATAFLOW_SIDE_EFFECTING),
        )(*[pltpu.with_memory_space_constraint(a, pltpu.HBM) for a in arrays], *extra)
        self.send_sems, self.recv_sems = outs[0], outs[1]
        self.arrays, self.token = list(outs[2:2 + n]), outs[2 + n]

    def wait(self, after):
        n, plan = self.n, self.plan

        def body(*refs):
            in_refs, send_sems, recv_sems = refs[:n], refs[n], refs[n + 1]
            for k, (src, dst, to) in enumerate(plan(_position(), in_refs)):
                cp = pltpu.make_async_remote_copy(src_ref=src, dst_ref=dst, send_sem=send_sems.at[k],
                                                  recv_sem=recv_sems.at[k], device_id=to, device_id_type=MESH_T)
                cp.wait_send()
                cp.wait_recv()

        outs = pl.pallas_call(
            body, name=self.name + "_wait",
            out_shape=tuple(pltpu.HBM(a.shape, a.dtype) for a in self.arrays),
            in_specs=[HBM] * n + [SEM, SEM, ANY], out_specs=tuple([HBM] * n),
            input_output_aliases={i: i for i in range(n)},
            compiler_params=pltpu.CompilerParams(has_side_effects=pltpu.SideEffectType.DATAFLOW_SIDE_EFFECTING),
        )(*self.arrays, self.send_sems, self.recv_sems, after)
        return list(outs)


def gather_start(shards, me, name, after=None):
    n = len(shards)
    lands = [lax.dynamic_update_slice(lax.empty((N_DEV, *s.shape), s.dtype), s[None], (me, 0, 0)) for s in shards]

    def plan(pos, refs):
        x, y, c = pos
        return [(refs[a], refs[n + a].at[4 * x + 2 * y + c], (x ^ mx, y ^ my, c))
                for a in range(n) for mx, my in OTHER_CHIPS]

    return SplitCopy(name, list(shards) + lands, 3 * n, plan, after=after)


def sibling_start(lands, name):
    n = len(lands)

    def plan(pos, refs):
        x, y, c = pos
        return [(refs[a].at[2 * q + c], refs[a].at[2 * q + c], (x, y, 1 - c)) for a in range(n) for q in range(4)]

    return SplitCopy(name, list(lands), 4 * n, plan)


def scatter_start(slabs, name):
    n = len(slabs)
    lands = [lax.empty((N_DEV - 1, *g.shape[1:]), g.dtype) for g in slabs]

    def plan(pos, refs):
        x, y, c = pos
        copies = []
        for a in range(n):
            for r in range(1, N_DEV):
                px, py, pc = x ^ ((r >> 2) & 1), y ^ ((r >> 1) & 1), c ^ (r & 1)
                copies.append((refs[a].at[4 * px + 2 * py + pc], refs[n + a].at[r - 1], (px, py, pc)))
        return copies

    return SplitCopy(name, list(slabs) + lands, (N_DEV - 1) * n, plan)


def all_reduce_small(v, name, dep=None):
    R, W = v.shape

    def body(v_ref, o_ref, recv, send_sems, recv_sems):
        x, y, c = _position()
        me = 4 * x + 2 * y + c
        copies = []
        for r in range(1, N_DEV):
            to = (x ^ ((r >> 2) & 1), y ^ ((r >> 1) & 1), c ^ (r & 1))
            copies.append(pltpu.make_async_remote_copy(
                src_ref=v_ref, dst_ref=recv.at[me], send_sem=send_sems.at[r - 1], recv_sem=recv_sems.at[r - 1],
                device_id=to, device_id_type=MESH_T))
        for cp in copies:
            cp.start()
        recv[pl.ds(me, 1)] = v_ref[...][None]
        for cp in copies:
            cp.wait()
        acc = recv[0]
        for s in range(1, N_DEV):
            acc = acc + recv[s]
        o_ref[...] = acc

    return _pallas_after(
        dep, body, v, name=name,
        in_specs=[pl.BlockSpec(memory_space=pltpu.VMEM)], out_specs=pl.BlockSpec(memory_space=pltpu.VMEM),
        out_shape=jax.ShapeDtypeStruct((R, W), F32),
        scratch_shapes=[pltpu.VMEM((N_DEV, R, W), F32), pltpu.SemaphoreType.DMA((N_DEV - 1,)),
                        pltpu.SemaphoreType.DMA((N_DEV - 1,))],
    )


def _row_tile(rows):
    if rows <= 512:
        return rows
    return max(t for t in range(8, 513, 8) if rows % t == 0)


def _adamw_update(w, g, m, v):
    mn = ADAM_B1 * m + (1.0 - ADAM_B1) * g
    vn = ADAM_B2 * v + (1.0 - ADAM_B2) * (g * g)
    m_hat = mn / (1.0 - ADAM_B1 ** ADAM_STEP)
    v_hat = vn / (1.0 - ADAM_B2 ** ADAM_STEP)
    return -ADAM_LR * (m_hat / (jnp.sqrt(v_hat) + ADAM_EPS) + ADAM_WD * w), mn, vn


def adamw(w, g, m, v, name):
    R, W = w.shape
    tr = _row_tile(R)

    def body(w_ref, g_ref, m_ref, v_ref, d_ref, mo_ref, vo_ref):
        d_ref[...], mo_ref[...], vo_ref[...] = _adamw_update(w_ref[...], g_ref[...], m_ref[...], v_ref[...])

    spec = pl.BlockSpec((tr, W), lambda i: (i, 0))
    return pl.pallas_call(
        body, name=name, grid=(R // tr,), in_specs=[spec] * 4, out_specs=[spec] * 3,
        out_shape=[jax.ShapeDtypeStruct((R, W), F32)] * 3,
        compiler_params=_cparams(("parallel",)),
    )(w, g, m, v)


def reduce_adamw(slabs, land, w, m, v, layer, me_arr, name, others=None):
    _, R, W = slabs.shape
    tr = _row_tile(R)
    n_other = 0 if others is None else 4

    def body(me_ref, s_ref, l_ref, w_ref, m_ref, v_ref, *rest):
        g_ref, d_ref, mo_ref, vo_ref = rest[n_other:]
        g = s_ref[0].astype(F32)
        for r in range(N_DEV - 1):
            g = g + l_ref[r].astype(F32)
        g_ref[0] = g
        d_ref[0], mo_ref[0], vo_ref[0] = _adamw_update(w_ref[0], g, m_ref[0], v_ref[0])

    spec = pl.BlockSpec((1, tr, W), lambda i, me: (layer, i, 0))
    return pl.pallas_call(
        body, name=name,
        grid_spec=pltpu.PrefetchScalarGridSpec(
            num_scalar_prefetch=1, grid=(R // tr,),
            in_specs=[pl.BlockSpec((1, tr, W), lambda i, me: (me[0], i, 0)),
                      pl.BlockSpec((N_DEV - 1, tr, W), lambda i, me: (0, i, 0)), spec, spec, spec] + [ANY] * n_other,
            out_specs=[spec] * 4),
        out_shape=[jax.ShapeDtypeStruct((DEPTH, R, W), F32)] * 4,
        input_output_aliases={6 + i: i for i in range(n_other)},
        compiler_params=_cparams(("parallel",)),
    )(me_arr, slabs, land, w, m, v, *([] if others is None else others))


SMALL_NAMES = ("norm1_g", "q_norm_g", "k_norm_g", "sinks", "conv_out_g", "attn_out_g", "norm2_g", "conv_w")
SMALL_SIZES = (D_MODEL, HEAD, HEAD, N_Q, CONV_CH, ATTN_W, D_MODEL, 3 * CONV_CH)
SMALL_ROWS = 80


def kernel(x, norm1_g, w_in, conv_w, q_norm_g, k_norm_g, sinks, conv_out_g, attn_out_g, w_o, norm2_g, w_gate, w_up, w_down, loss_target, m_norm1_g, m_w_in, m_conv_w, m_q_norm_g, m_k_norm_g, m_sinks, m_conv_out_g, m_attn_out_g, m_w_o, m_norm2_g, m_w_gate, m_w_up, m_w_down, v_norm1_g, v_w_in, v_conv_w, v_q_norm_g, v_k_norm_g, v_sinks, v_conv_out_g, v_attn_out_g, v_w_o, v_norm2_g, v_w_gate, v_w_up, v_w_down):
    xi, yi, ci = _position()
    me = 4 * xi + 2 * yi + ci
    me_arr = jnp.reshape(me, (1,)).astype(jnp.int32)
    xs, tgt = x[0], loss_target[0]
    bf = lambda a: a.astype(BF16)
    bias = band_bias()

    t = lambda a: jnp.swapaxes(a, 1, 2)
    shard = dict(w_in=(t(w_in), t(m_w_in), t(v_w_in)), w_o=(w_o, m_w_o, v_w_o),
                 w_gate=(t(w_gate), t(m_w_gate), t(v_w_gate)), w_up=(t(w_up), t(m_w_up), t(v_w_up)),
                 w_down=(w_down, m_w_down, v_w_down))
    wb = {n: bf(shard[n][0]) for n in shard}

    g_in0, g_o0, g_conv = all_gather([wb["w_in"][0], wb["w_o"][0], conv_w.reshape(DEPTH * 3, HEAD)], "gather_first")
    ag_ffn0 = gather_start([wb["w_gate"][0], wb["w_up"][0], wb["w_down"][0]], me, "gather_ffn0", after=g_in0)
    ag_mix1 = gather_start([wb["w_in"][1], wb["w_o"][1]], me, "gather_mix1", after=ag_ffn0.token)
    ag_ffn1 = gather_start([wb["w_gate"][1], wb["w_up"][1], wb["w_down"][1]], me, "gather_ffn1", after=ag_mix1.token)
    conv_full = g_conv.reshape(N_DEV, DEPTH, 3, HEAD).transpose(1, 2, 0, 3).reshape(DEPTH, 3, CONV_CH)
    pair_gain = lambda g: jnp.tile(g[None], (1, 2))
    small = [dict(norm1_g=norm1_g[l][None], conv_w=conv_full[l], q_norm_g=pair_gain(q_norm_g[l]),
                  k_norm_g=pair_gain(k_norm_g[l]), sinks=sinks[l], conv_out_g=conv_out_g[l][None],
                  attn_out_g=attn_out_g[l][None], norm2_g=norm2_g[l][None]) for l in range(DEPTH)]
    whole = lambda g: g.reshape(-1, D_MODEL)
    weights = [dict(w_in=whole(g_in0), w_o=whole(g_o0)), {}]

    def ffn_weights(g_gate, g_up, g_down):
        return dict(w_gate=whole(g_gate), w_up=whole(g_up), w_down=whole(g_down))

    saved = []
    xl = xs
    for l in range(DEPTH):
        sp, wl = small[l], weights[l]
        h, proj = norm_proj(xl, sp["norm1_g"], wl["w_in"], f"norm_proj{l}",
                            dep=ag_ffn1.token if l == 0 else pass_ffn1.token)
        mixer_args = (proj, bias, sp["sinks"], sp["conv_w"], sp["q_norm_g"], sp["k_norm_g"], sp["conv_out_g"],
                      sp["attn_out_g"], f"mixer_fwd{l}")
        if l == 0:
            mix, y, ao = mixer_fwd(*mixer_args)
            pass_ffn0 = sibling_start(ag_ffn0.wait(mix)[3:], "pass_ffn0")
            x1 = matmul_residual(mix, wl["w_o"], xl, "out_proj0", dep=pass_ffn0.token)
            wl.update(ffn_weights(*pass_ffn0.wait(x1)))
            pass_mix1 = sibling_start(ag_mix1.wait(x1)[2:], "pass_mix1")
            h2, a, b, f, x2 = ffn_fwd(x1, sp["norm2_g"], wl["w_gate"], wl["w_up"], wl["w_down"], "ffn_fwd0",
                                      dep=pass_mix1.token)
            g_in, g_o = pass_mix1.wait(x2)
            weights[1] = dict(w_in=whole(g_in), w_o=whole(g_o))
            pass_ffn1 = sibling_start(ag_ffn1.wait(x2)[3:], "pass_ffn1")
        else:
            mix, y, ao, x1 = mixer_fwd(*mixer_args, residual=xl, w_o=wl["w_o"])
            wl.update(ffn_weights(*pass_ffn1.wait(x1)))
            h2, a, b, f, dx, loss_row = ffn_fwd(x1, sp["norm2_g"], wl["w_gate"], wl["w_up"], wl["w_down"], "ffn_fwd1",
                                                tgt=tgt)
            x2 = None
        saved.append((xl, h, proj, mix, y, ao, x1, h2, a, b, f))
        xl = x2

    stepped = {n: None for n in shard}
    slabs = lambda d: d.reshape(N_DEV, -1, D_MODEL)
    gsmall = [None] * DEPTH

    def finish(sc, names, after, l):
        arrays = sc.wait(after)
        k = len(names)
        for i, n in enumerate(names):
            w, m, v = shard[n]
            stepped[n] = reduce_adamw(arrays[i], arrays[k + i], w, m, v, l, me_arr, f"reduce_adamw_{n}{l}",
                                      others=stepped[n])

    for l in reversed(range(DEPTH)):
        sp, wl = small[l], weights[l]
        x0, h, proj, mix, y, ao, x1, h2, a, b, f = saved[l]
        (d_wd,) = grad_weight((f,), dx, TF, f"grad_w_down{l}")
        sc_down = scatter_start([slabs(d_wd)], f"scatter_w_down{l}")
        da, db, dx1, d_g2 = ffn_bwd(dx, a, b, x1, sp["norm2_g"], wl["w_gate"], wl["w_up"], wl["w_down"],
                                    f"ffn_bwd{l}", dep=sc_down.token)
        d_wg, d_wu = grad_weight((da, db), h2, TF, f"grad_w_gate_up{l}")
        (d_wo,) = grad_weight((mix,), dx1, D_MODEL, f"grad_w_o{l}")
        sc_rest = scatter_start([slabs(d_wg), slabs(d_wu), slabs(d_wo)], f"scatter_w_gate_up_o{l}")
        dproj, d_cw, d_qg, d_kg, d_sk, d_cog, d_aog = mixer_bwd(
            proj, bias, y, ao, dx1, wl["w_o"], sp["sinks"], sp["conv_w"], sp["q_norm_g"], sp["k_norm_g"],
            sp["conv_out_g"], sp["attn_out_g"], f"mixer_bwd{l}", dep=sc_rest.token)
        finish(sc_down, ["w_down"], dproj, l)
        finish(sc_rest, ["w_gate", "w_up", "w_o"], dproj, l)
        (d_win,) = grad_weight((dproj,), h, IN_COLS // 2, f"grad_w_in{l}")
        sc_in = scatter_start([slabs(d_win)], f"scatter_w_in{l}")
        dx, d_g1 = proj_bwd(dproj, wl["w_in"], x0, sp["norm1_g"], dx1, f"proj_bwd{l}", dep=sc_in.token)
        finish(sc_in, ["w_in"], dx, l)
        both_heads = lambda d: d[:, :HEAD] + d[:, HEAD:]
        gsmall[l] = dict(norm1_g=d_g1, conv_w=d_cw, q_norm_g=both_heads(d_qg), k_norm_g=both_heads(d_kg), sinks=d_sk,
                         conv_out_g=d_cog, attn_out_g=d_aog, norm2_g=d_g2)
    grad_x = dx[None]

    flat = jnp.concatenate([loss_row[0, 0:1]] + [gsmall[l][n].reshape(-1) for l in range(DEPTH) for n in SMALL_NAMES])
    flat = jnp.pad(flat, (0, SMALL_ROWS * 128 - flat.shape[0])).reshape(SMALL_ROWS, 128)
    flat = all_reduce_small(flat, "all_reduce_small_grads").reshape(-1)
    loss = flat[0]
    gs = {n: [] for n in SMALL_NAMES}
    off = 1
    for l in range(DEPTH):
        for n, size in zip(SMALL_NAMES, SMALL_SIZES):
            gs[n].append(flat[off:off + size])
            off += size
    gs = {n: jnp.stack(v) for n, v in gs.items()}
    g_conv = lax.dynamic_slice(gs["conv_w"].reshape(DEPTH, 3, CONV_CH), (0, 0, me * HEAD), (DEPTH, 3, HEAD))

    gs["conv_w"] = g_conv
    params = dict(norm1_g=(norm1_g, m_norm1_g, v_norm1_g), conv_w=(conv_w, m_conv_w, v_conv_w),
                  q_norm_g=(q_norm_g, m_q_norm_g, v_q_norm_g), k_norm_g=(k_norm_g, m_k_norm_g, v_k_norm_g),
                  sinks=(sinks, m_sinks, v_sinks), conv_out_g=(conv_out_g, m_conv_out_g, v_conv_out_g),
                  attn_out_g=(attn_out_g, m_attn_out_g, v_attn_out_g), norm2_g=(norm2_g, m_norm2_g, v_norm2_g))
    names = ("norm1_g", "w_in", "conv_w", "q_norm_g", "k_norm_g", "sinks", "conv_out_g", "attn_out_g", "w_o",
             "norm2_g", "w_gate", "w_up", "w_down")

    out = {}
    for n in names:
        if n in shard:
            out[n] = tuple(t(r) for r in stepped[n]) if n in ("w_in", "w_gate", "w_up") else stepped[n]
        else:
            w, m, v = params[n]
            two_d = (-1, w.shape[-1])
            d, mn, vn = adamw(w.reshape(two_d), gs[n].reshape(two_d), m.reshape(two_d), v.reshape(two_d), f"adamw_{n}")
            out[n] = (gs[n].reshape(w.shape), d.reshape(w.shape), mn.reshape(w.shape), vn.reshape(w.shape))
    return (loss, grad_x, *[out[n][i] for i in range(4) for n in names])
```

```python
import functools

import jax
import jax.numpy as jnp
from jax import lax
from jax.experimental import pallas as pl
from jax.experimental.pallas import tpu as pltpu

F32 = jnp.float32
BF16 = jnp.bfloat16

D_MODEL = 1024
CONV_CH = 512
ATTN_W = 512
N_Q = 8
N_KV = 2
GRP = N_Q // N_KV
HEAD = 64
IN_COLS = 2304
D_FF = 2816
BLK = 128
O_Q = 3 * CONV_CH
O_K = O_Q + ATTN_W
O_V = O_K + N_KV * HEAD
EPS = 1e-6
NEG_INF = -1e30
SCALE = HEAD ** -0.5
N_DEV = 8
DEPTH = 2

ADAM_LR = 0.001
ADAM_B1 = 0.9
ADAM_B2 = 0.999
ADAM_EPS = 1e-08
ADAM_WD = 0.01
ADAM_STEP = 10

VMEM_LIMIT = 56 * 1024 * 1024
TM = 512
MESH_T = pl.DeviceIdType.MESH

ROWS_IN, ROWS_O, ROWS_FF = IN_COLS // N_DEV, D_MODEL // N_DEV, D_FF // N_DEV


ANY = pl.BlockSpec(memory_space=pl.ANY)
HBM = pl.BlockSpec(memory_space=pltpu.HBM)
SEM = pl.BlockSpec(memory_space=pltpu.SEMAPHORE)


def _cparams(sem):
    return pltpu.CompilerParams(dimension_semantics=sem, vmem_limit_bytes=VMEM_LIMIT)


def _pallas_after(dep, body, *args, in_specs, **kw):
    if dep is None:
        return pl.pallas_call(body, in_specs=in_specs, **kw)(*args)

    def after_dep(dep_ref, *refs):
        body(*refs)

    return pl.pallas_call(after_dep, in_specs=[ANY, *in_specs], **kw)(dep, *args)


def _dot(a, b):
    return jnp.dot(a, b, preferred_element_type=F32)


def _dot_nt(a, b):
    return lax.dot_general(a, b, (((1,), (1,)), ((), ())), preferred_element_type=F32)


def _dot_tn(a, b):
    return lax.dot_general(a, b, (((0,), (0,)), ((), ())), preferred_element_type=F32)


LANES = 128


def _row_reduce(v, op, reduce):
    w = v.shape[-1]
    if w > LANES and w % LANES == 0:
        acc = v[:, 0:LANES]
        for c in range(1, w // LANES):
            acc = op(acc, v[:, LANES * c:LANES * (c + 1)])
        v = acc
    return reduce(v, axis=-1, keepdims=True)


def _row_sum(v):
    return _row_reduce(v, jnp.add, jnp.sum)


def _row_mean(v):
    return _row_sum(v) * (1.0 / v.shape[-1])


def _rms(v):
    return lax.rsqrt(_row_mean(v * v) + EPS)


def _sigmoid(v):
    return 1.0 / (1.0 + jnp.exp(-v))


def _rms_bwd(dyv, xh, r, g):
    dxh = dyv * g
    return r * (dxh - xh * _row_mean(dxh * xh))


def norm_proj(x, g, w_t, name, dep=None):
    S, N = x.shape[0], w_t.shape[0]

    def body(x_ref, g_ref, w_ref, h_ref, p_ref):
        xf = x_ref[...]
        h = ((xf * _rms(xf)) * g_ref[...]).astype(BF16)
        h_ref[...] = h
        p_ref[...] = _dot_nt(h, w_ref[...]).astype(BF16)

    return _pallas_after(
        dep, body, x, g, w_t, name=name, grid=(S // TM,),
        in_specs=[pl.BlockSpec((TM, D_MODEL), lambda i: (i, 0)),
                  pl.BlockSpec((1, D_MODEL), lambda i: (0, 0)),
                  pl.BlockSpec((N, D_MODEL), lambda i: (0, 0))],
        out_specs=[pl.BlockSpec((TM, D_MODEL), lambda i: (i, 0)),
                   pl.BlockSpec((TM, N), lambda i: (i, 0))],
        out_shape=[jax.ShapeDtypeStruct((S, D_MODEL), BF16), jax.ShapeDtypeStruct((S, N), BF16)],
        compiler_params=_cparams(("parallel",)),
    )


def band_bias():
    qi = lax.broadcasted_iota(jnp.int32, (BLK, 2 * BLK), 0)
    kj = lax.broadcasted_iota(jnp.int32, (BLK, 2 * BLK), 1)
    diff = qi + BLK - kj
    valid = (diff >= 0) & (diff < BLK)
    return jnp.stack([jnp.where(valid & (kj >= BLK), 0.0, NEG_INF), jnp.where(valid, 0.0, NEG_INF)]).astype(F32)


def _softmax_with_sink(s, sink):
    m = jnp.maximum(_row_reduce(s, jnp.maximum, jnp.max), sink)
    p = jnp.exp(s - m)
    es = jnp.exp(sink - m)
    inv = 1.0 / (_row_sum(p) + es)
    return p * inv, es * inv


PAIR = 2 * HEAD


def _low_half():
    return lax.broadcasted_iota(jnp.int32, (1, PAIR), 1) < HEAD


def _half_sums(v, low):
    return (jnp.sum(jnp.where(low, v, 0.0), axis=-1, keepdims=True),
            jnp.sum(jnp.where(low, 0.0, v), axis=-1, keepdims=True))


def _pair_mean(v, low):
    e, o = _half_sums(v, low)
    return jnp.where(low, e, o) * (1.0 / HEAD)


def _pair_rms(v, low):
    return lax.rsqrt(_pair_mean(v * v, low) + EPS)


def _one_head_in_both_halves(pair, low):
    swapped = pltpu.roll(pair, HEAD, 1)
    return jnp.where(low, pair, swapped), jnp.where(low, swapped, pair)


SUB_FWD, SUB_FWD_PROJECT, SUB_BWD = 2, 4, 4


def _kv_pairs(p_ref, kvp_ref):
    k = jnp.concatenate([kvp_ref[:, 0:PAIR], p_ref[:, O_K:O_K + PAIR]], axis=0).astype(F32)
    v = jnp.concatenate([kvp_ref[:, PAIR:2 * PAIR], p_ref[:, O_V:O_V + PAIR]], axis=0)
    return k, v


def _band(s):
    return slice(BLK * s, BLK * s + 2 * BLK)


def _block(s):
    return slice(BLK * s, BLK * (s + 1))


def mixer_fwd(proj, bias, sinks, conv_w, qg, kg, cog, aog, name, residual=None, w_o=None):
    S = proj.shape[0]
    project = w_o is not None
    SUB = SUB_FWD_PROJECT if project else SUB_FWD
    ROWS = SUB * BLK
    steps = S // ROWS

    def body(sinks_ref, p_ref, kvp_ref, bias_ref, cw_ref, qg_ref, kg_ref, cog_ref, aog_ref, *rest):
        if project:
            x_ref, wo_ref, mix_ref, y_ref, ao_ref, x1_ref, ucar = rest
        else:
            mix_ref, y_ref, ao_ref, ucar = rest
        n = pl.program_id(0)

        @pl.when(n == 0)
        def _():
            ucar[...] = jnp.zeros_like(ucar)

        bg = p_ref[:, 0:CONV_CH].astype(F32)
        u = p_ref[:, CONV_CH:2 * CONV_CH].astype(F32) * p_ref[:, 2 * CONV_CH:3 * CONV_CH].astype(F32)
        row = lax.broadcasted_iota(jnp.int32, (ROWS, CONV_CH), 0)
        prev = ucar[...]
        u1 = jnp.where(row == 0, prev[7:8, :], pltpu.roll(u, 1, 0))
        u2 = jnp.where(row == 0, prev[6:7, :], jnp.where(row == 1, prev[7:8, :], pltpu.roll(u, 2, 0)))
        ucar[...] = u[ROWS - 8:ROWS, :]
        y = cw_ref[0:1, :] * u2 + cw_ref[1:2, :] * u1 + cw_ref[2:3, :] * u
        y_ref[...] = y.astype(BF16)
        co = bg * y
        mix_conv = ((co * _rms(co)) * cog_ref[...]).astype(BF16)
        mix_ref[:, 0:CONV_CH] = mix_conv
        if project:
            x1 = x_ref[...] + _dot(mix_conv, wo_ref[0:CONV_CH, :])

        low = _low_half()
        q_gain = qg_ref[...] * SCALE
        k, v = _kv_pairs(p_ref, kvp_ref)
        kn = ((k * _pair_rms(k, low)) * kg_ref[...]).astype(BF16)
        k_of = _one_head_in_both_halves(kn, low)
        v_of = _one_head_in_both_halves(v, low)
        biases = [bias_ref[jnp.minimum(n, 1)]] + [bias_ref[1]] * (SUB - 1)
        units = [(s, hq) for s in range(SUB) for hq in range(N_Q)]
        q_one = []
        for pair in range(N_Q // 2):
            q = p_ref[:, O_Q + PAIR * pair:O_Q + PAIR * (pair + 1)].astype(F32)
            qn = ((q * _pair_rms(q, low)) * q_gain).astype(BF16)
            q_one += [jnp.where(mine, qn, jnp.zeros_like(qn)) for mine in (low, jnp.logical_not(low))]
        scores = [_dot_nt(q_one[hq][_block(s)], k_of[hq // GRP][_band(s)]) + biases[s] for s, hq in units]
        probs = [_softmax_with_sink(sc, sinks_ref[hq])[0].astype(BF16) for (s, hq), sc in zip(units, scores)]
        o = {u_: _dot(pn, v_of[u_[1] // GRP][_band(u_[0])]) for u_, pn in zip(units, probs)}
        ao = jnp.concatenate(
            [jnp.concatenate([jnp.where(low, o[s, 2 * pair], o[s, 2 * pair + 1]) for pair in range(N_Q // 2)], axis=1)
             for s in range(SUB)], axis=0)
        ao_ref[...] = ao.astype(BF16)
        mix_attn = ((ao * _rms(ao)) * aog_ref[...]).astype(BF16)
        mix_ref[:, CONV_CH:] = mix_attn
        if project:
            x1_ref[...] = x1 + _dot(mix_attn, wo_ref[CONV_CH:, :])

    small = lambda shape: pl.BlockSpec(shape, lambda n: (0, 0))
    rows = lambda w: pl.BlockSpec((ROWS, w), lambda n: (n, 0))
    return pl.pallas_call(
        body, name=name, grid=(steps,),
        in_specs=[pl.BlockSpec(memory_space=pltpu.SMEM),
                  rows(IN_COLS),
                  pl.BlockSpec((BLK, 2 * PAIR), lambda n: (jnp.maximum(SUB * n - 1, 0), O_K // (2 * PAIR))),
                  pl.BlockSpec((2, BLK, 2 * BLK), lambda n: (0, 0, 0)),
                  small((3, CONV_CH)), small((1, PAIR)), small((1, PAIR)), small((1, CONV_CH)), small((1, ATTN_W))]
        + ([rows(D_MODEL), small((D_MODEL, D_MODEL))] if project else []),
        out_specs=[rows(D_MODEL), rows(CONV_CH), rows(ATTN_W)] + ([rows(D_MODEL)] if project else []),
        out_shape=[jax.ShapeDtypeStruct((S, D_MODEL), BF16), jax.ShapeDtypeStruct((S, CONV_CH), BF16),
                   jax.ShapeDtypeStruct((S, ATTN_W), BF16)]
        + ([jax.ShapeDtypeStruct((S, D_MODEL), F32)] if project else []),
        scratch_shapes=[pltpu.VMEM((8, CONV_CH), F32)],
        compiler_params=_cparams(("arbitrary",)),
    )(sinks, proj, proj, bias, conv_w, qg, kg, cog, aog, *([residual, w_o] if project else []))


def matmul_residual(a, w, res, name, dep=None):
    S, K = a.shape
    N = w.shape[1]

    def body(a_ref, w_ref, r_ref, o_ref):
        o_ref[...] = r_ref[...] + _dot(a_ref[...], w_ref[...])

    return _pallas_after(
        dep, body, a, w, res, name=name, grid=(S // TM,),
        in_specs=[pl.BlockSpec((TM, K), lambda i: (i, 0)), pl.BlockSpec((K, N), lambda i: (0, 0)),
                  pl.BlockSpec((TM, N), lambda i: (i, 0))],
        out_specs=pl.BlockSpec((TM, N), lambda i: (i, 0)),
        out_shape=jax.ShapeDtypeStruct((S, N), F32),
        compiler_params=_cparams(("parallel",)),
    )


TF = 1408


def ffn_fwd(x1, g, wg, wu, wd, name, dep=None, tgt=None):
    S = x1.shape[0]
    ni = S // TM
    with_loss = tgt is not None
    row = pl.BlockSpec((TM, D_MODEL), lambda i: (i, 0))
    vec = pl.BlockSpec((1, D_MODEL), lambda i: (0, 0))
    acts = [jax.ShapeDtypeStruct((S, D_FF), BF16)] * 3

    def half_weights(j):
        return [pl.BlockSpec((TF, D_MODEL), lambda i: (j, 0))] * 3

    def swiglu(h2, wg_ref, wu_ref, a_ref, b_ref, f_ref):
        a = _dot_nt(h2, wg_ref[...])
        b = _dot_nt(h2, wu_ref[...])
        a_ref[...] = a.astype(BF16)
        b_ref[...] = b.astype(BF16)
        f = ((a * _sigmoid(a)) * b).astype(BF16)
        f_ref[...] = f
        return f

    def first(x_ref, g_ref, wg_ref, wu_ref, wd_ref, h2_ref, a_ref, b_ref, f_ref, part_ref):
        xf = x_ref[...]
        h2 = ((xf * _rms(xf)) * g_ref[...]).astype(BF16)
        h2_ref[...] = h2
        part_ref[...] = xf + _dot(swiglu(h2, wg_ref, wu_ref, a_ref, b_ref, f_ref), wd_ref[...])

    h2, a, b, f, part = _pallas_after(
        dep, first, x1, g, wg, wu, wd, name=name + "_a", grid=(ni,),
        in_specs=[row, vec] + half_weights(0),
        out_specs=[row] + [pl.BlockSpec((TM, TF), lambda i: (i, 0))] * 3 + [row],
        out_shape=[jax.ShapeDtypeStruct((S, D_MODEL), BF16)] + acts + [jax.ShapeDtypeStruct((S, D_MODEL), F32)],
        compiler_params=_cparams(("parallel",)),
    )

    def second(h2_ref, part_ref, wg_ref, wu_ref, wd_ref, *rest):
        rest = rest[3:]
        if with_loss:
            t_ref, a_ref, b_ref, f_ref, o_ref, l_ref, sq = rest
        else:
            a_ref, b_ref, f_ref, o_ref = rest
        i = pl.program_id(0)
        out = part_ref[...] + _dot(swiglu(h2_ref[...], wg_ref, wu_ref, a_ref, b_ref, f_ref), wd_ref[...])
        if not with_loss:
            o_ref[...] = out
            return
        e = out - t_ref[...]
        o_ref[...] = e * (1.0 / D_MODEL)
        col = jnp.sum(e * e, axis=0, keepdims=True)

        @pl.when(i == 0)
        def _():
            sq[...] = col

        @pl.when(i > 0)
        def _():
            sq[...] += col

        @pl.when(i == ni - 1)
        def _():
            l_ref[...] = jnp.full((1, 128), jnp.sum(sq[...]) * (0.5 / D_MODEL), F32)

    results = pl.pallas_call(
        second, name=name + "_b", grid=(ni,),
        in_specs=[row, row] + half_weights(1) + [ANY] * 3 + ([row] if with_loss else []),
        out_specs=[pl.BlockSpec((TM, TF), lambda i: (i, 1))] * 3 + [row]
        + ([pl.BlockSpec((1, 128), lambda i: (0, 0))] if with_loss else []),
        out_shape=acts + [jax.ShapeDtypeStruct((S, D_MODEL), F32)]
        + ([jax.ShapeDtypeStruct((1, 128), F32)] if with_loss else []),
        input_output_aliases={5: 0, 6: 1, 7: 2},
        scratch_shapes=[pltpu.VMEM((1, D_MODEL), F32)] if with_loss else [],
        compiler_params=_cparams(("arbitrary",) if with_loss else ("parallel",)),
    )(h2, part, wg, wu, wd, a, b, f, *([tgt] if with_loss else []))
    return (h2, *results)


def ffn_bwd(dx2, a, b, x1, g, wg, wu, wd, name, dep=None):
    S = dx2.shape[0]
    ni = S // TM
    row = pl.BlockSpec((TM, D_MODEL), lambda i: (i, 0))
    vec = pl.BlockSpec((1, D_MODEL), lambda i: (0, 0))
    acts = [jax.ShapeDtypeStruct((S, D_FF), BF16)] * 2

    def half(j):
        return [pl.BlockSpec((TM, TF), lambda i: (i, j))] * 2, [pl.BlockSpec((TF, D_MODEL), lambda i: (j, 0))] * 3

    def through_half(dx_ref, a_ref, b_ref, wg_ref, wu_ref, wd_ref, da_ref, db_ref):
        df = _dot_nt(dx_ref[...].astype(BF16), wd_ref[...])
        av = a_ref[...].astype(F32)
        bv = b_ref[...].astype(F32)
        sg = _sigmoid(av)
        da = ((df * bv) * (sg * (1.0 + av * (1.0 - sg)))).astype(BF16)
        db = (df * (av * sg)).astype(BF16)
        da_ref[...] = da
        db_ref[...] = db
        return _dot(da, wg_ref[...]) + _dot(db, wu_ref[...])

    def first(dx_ref, a_ref, b_ref, wg_ref, wu_ref, wd_ref, da_ref, db_ref, part_ref):
        part_ref[...] = through_half(dx_ref, a_ref, b_ref, wg_ref, wu_ref, wd_ref, da_ref, db_ref)

    tiles, weights = half(0)
    da, db, part = _pallas_after(
        dep, first, dx2, a, b, wg, wu, wd, name=name + "_a", grid=(ni,),
        in_specs=[row] + tiles + weights, out_specs=tiles + [row],
        out_shape=acts + [jax.ShapeDtypeStruct((S, D_MODEL), F32)],
        compiler_params=_cparams(("parallel",)),
    )

    def second(dx_ref, a_ref, b_ref, wg_ref, wu_ref, wd_ref, part_ref, x_ref, g_ref, da_prev, db_prev,
               da_ref, db_ref, dx1_ref, dg_ref):
        i = pl.program_id(0)

        @pl.when(i == 0)
        def _():
            dg_ref[...] = jnp.zeros_like(dg_ref)

        dh = part_ref[...] + through_half(dx_ref, a_ref, b_ref, wg_ref, wu_ref, wd_ref, da_ref, db_ref)
        xf = x_ref[...]
        r = _rms(xf)
        xh = xf * r
        dg_ref[...] += jnp.sum(dh * xh, axis=0, keepdims=True)
        dx1_ref[...] = dx_ref[...] + _rms_bwd(dh, xh, r, g_ref[...])

    tiles, weights = half(1)
    return pl.pallas_call(
        second, name=name + "_b", grid=(ni,),
        in_specs=[row] + tiles + weights + [row, row, vec, ANY, ANY],
        out_specs=tiles + [row, vec],
        out_shape=acts + [jax.ShapeDtypeStruct((S, D_MODEL), F32), jax.ShapeDtypeStruct((1, D_MODEL), F32)],
        input_output_aliases={9: 0, 10: 1},
        compiler_params=_cparams(("arbitrary",)),
    )(dx2, a, b, wg, wu, wd, part, x1, g, da, db)


def grad_weight(lhs, rhs, tm, name, dep=None):
    S, N = rhs.shape
    M = lhs[0].shape[1]
    tk = 1024
    nk = S // tk
    n_l = n_o = len(lhs)

    def body(*refs):
        l_refs, r_ref = refs[:n_l], refs[n_l]
        o_refs, accs = refs[n_l + 1:n_l + 1 + n_o], refs[n_l + 1 + n_o:]
        k = pl.program_id(1)
        rv = r_ref[...].astype(BF16)
        for l_ref, o_ref, acc in zip(l_refs, o_refs, accs):
            @pl.when(k == 0)
            def _():
                acc[...] = jnp.zeros_like(acc)

            acc[...] += _dot_tn(l_ref[...], rv)

            @pl.when(k == nk - 1)
            def _():
                o_ref[...] = acc[...].astype(BF16)

    return _pallas_after(
        dep, body, *lhs, rhs, name=name, grid=(M // tm, nk),
        in_specs=[pl.BlockSpec((tk, tm), lambda i, k: (k, i)) for _ in lhs] + [pl.BlockSpec((tk, N), lambda i, k: (k, 0))],
        out_specs=[pl.BlockSpec((tm, N), lambda i, k: (i, 0))] * n_o,
        out_shape=[jax.ShapeDtypeStruct((M, N), BF16)] * n_o,
        scratch_shapes=[pltpu.VMEM((tm, N), F32)] * n_o,
        compiler_params=_cparams(("parallel", "arbitrary")),
    )


def proj_bwd(dproj, w, x, g, dres, name, dep=None):
    S, N = dproj.shape

    def body(dp_ref, w_ref, x_ref, g_ref, dr_ref, dx_ref, dg_ref):
        i = pl.program_id(0)

        @pl.when(i == 0)
        def _():
            dg_ref[...] = jnp.zeros_like(dg_ref)

        dh = _dot(dp_ref[...], w_ref[...])
        xf = x_ref[...]
        r = _rms(xf)
        xh = xf * r
        dg_ref[...] += jnp.sum(dh * xh, axis=0, keepdims=True)
        dx_ref[...] = dr_ref[...] + _rms_bwd(dh, xh, r, g_ref[...])

    return _pallas_after(
        dep, body, dproj, w, x, g, dres, name=name, grid=(S // TM,),
        in_specs=[pl.BlockSpec((TM, N), lambda i: (i, 0)), pl.BlockSpec((N, D_MODEL), lambda i: (0, 0)),
                  pl.BlockSpec((TM, D_MODEL), lambda i: (i, 0)), pl.BlockSpec((1, D_MODEL), lambda i: (0, 0)),
                  pl.BlockSpec((TM, D_MODEL), lambda i: (i, 0))],
        out_specs=[pl.BlockSpec((TM, D_MODEL), lambda i: (i, 0)), pl.BlockSpec((1, D_MODEL), lambda i: (0, 0))],
        out_shape=[jax.ShapeDtypeStruct((S, D_MODEL), F32), jax.ShapeDtypeStruct((1, D_MODEL), F32)],
        compiler_params=_cparams(("arbitrary",)),
    )


def mixer_bwd(proj, bias, y, ao, dx1, w_o, sinks, conv_w, qg, kg, cog, aog, name, dep=None):
    S = proj.shape[0]
    SUB, ROWS = SUB_BWD, SUB_BWD * BLK
    steps = S // ROWS
    KV_W = 2 * PAIR

    def body(sinks_ref, p_ref, kvp_ref, bias_ref, y_ref, ao_ref, dx_first_ref, dx_next_ref, wo_ref, cw_ref, qg_ref,
             kg_ref, cog_ref, aog_ref, dp_ref, dcw_ref, dqg_ref, dkg_ref, dsk_ref, dcog_ref, daog_ref,
             dycar, kcar, vcar, dmix_scr):
        step = pl.program_id(0)

        @pl.when(step == 0)
        def _():
            dmix_scr[...] = _dot_nt(dx_first_ref[...].astype(BF16), wo_ref[...])
            dycar[...] = jnp.zeros_like(dycar)
            kcar[...] = jnp.zeros_like(kcar)
            vcar[...] = jnp.zeros_like(vcar)
            dcw_ref[...] = jnp.zeros_like(dcw_ref)
            dqg_ref[...] = jnp.zeros_like(dqg_ref)
            dkg_ref[...] = jnp.zeros_like(dkg_ref)
            dsk_ref[...] = jnp.zeros_like(dsk_ref)
            dcog_ref[...] = jnp.zeros_like(dcog_ref)
            daog_ref[...] = jnp.zeros_like(daog_ref)

        dma = dmix_scr[:, CONV_CH:]

        aov = ao_ref[...].astype(F32)
        ra = _rms(aov)
        ah = aov * ra
        daog_ref[...] += jnp.sum(dma * ah, axis=0, keepdims=True)
        dao = _rms_bwd(dma, ah, ra, aog_ref[...])

        low = _low_half()
        high = jnp.logical_not(low)
        q_gain = qg_ref[...] * SCALE
        k, v = _kv_pairs(p_ref, kvp_ref)
        rk = _pair_rms(k, low)
        kh = k * rk
        k_of = _one_head_in_both_halves((kh * kg_ref[...]).astype(BF16), low)
        v_of = _one_head_in_both_halves(v, low)
        lane8 = lax.broadcasted_iota(jnp.int32, (1, N_Q), 1)
        dsk = jnp.zeros((1, N_Q), F32)
        dqg = jnp.zeros((1, PAIR), F32)
        biases = [bias_ref[jnp.minimum(steps - 1 - step, 1)]] + [bias_ref[1]] * (SUB - 1)
        units = [(s, hq) for s in range(SUB) for hq in range(N_Q)]
        rq, qh, q_one, do_one, delta = [], [], [], [], []
        for pair in range(N_Q // 2):
            cols = slice(PAIR * pair, PAIR * (pair + 1))
            q = p_ref[:, O_Q + PAIR * pair:O_Q + PAIR * (pair + 1)].astype(F32)
            rq.append(_pair_rms(q, low))
            qh.append(q * rq[pair])
            qn = (qh[pair] * q_gain).astype(BF16)
            do = dao[:, cols]
            do_b = do.astype(BF16)
            delta += _half_sums(do * aov[:, cols], low)
            for mine in (low, high):
                q_one.append(jnp.where(mine, qn, jnp.zeros_like(qn)))
                do_one.append(jnp.where(mine, do_b, jnp.zeros_like(do_b)))
        scores = {u: _dot_nt(q_one[u[1]][_block(u[0])], k_of[u[1] // GRP][_band(u[0])]) + biases[u[0]] for u in units}
        dprobs = {u: _dot_nt(do_one[u[1]][_block(u[0])], v_of[u[1] // GRP][_band(u[0])]) for u in units}
        probs = {u: _softmax_with_sink(scores[u], sinks_ref[u[1]]) for u in units}
        ds = {u: (probs[u][0] * (dprobs[u] - delta[u[1]][_block(u[0])])).astype(BF16) for u in units}
        dv_t = {(s, h): jnp.zeros((PAIR, 2 * BLK), F32) for s in range(SUB) for h in range(N_KV)}
        dkn_t = dict(dv_t)
        for s, hq in units:
            dsk = dsk - jnp.where(lane8 == hq, jnp.sum(probs[s, hq][1] * delta[hq][_block(s)]), 0.0)
            dv_t[s, hq // GRP] = dv_t[s, hq // GRP] + _dot_tn(do_one[hq][_block(s)], probs[s, hq][0].astype(BF16))
            dkn_t[s, hq // GRP] = dkn_t[s, hq // GRP] + _dot_tn(q_one[hq][_block(s)], ds[s, hq])
        dqn_of = {u: _dot(ds[u], k_of[u[1] // GRP][_band(u[0])]) for u in units}
        for pair in range(N_Q // 2):
            dqn = jnp.concatenate([jnp.where(low, dqn_of[s, 2 * pair], dqn_of[s, 2 * pair + 1]) for s in range(SUB)],
                                  axis=0)
            dqg = dqg + jnp.sum(dqn * qh[pair], axis=0, keepdims=True)
            dqh = dqn * q_gain
            dp_ref[:, O_Q + PAIR * pair:O_Q + PAIR * (pair + 1)] = (
                rq[pair] * (dqh - qh[pair] * _pair_mean(dqh * qh[pair], low))).astype(BF16)

        def over_key_rows(parts):
            bands = [jnp.concatenate([parts[s, h][:HEAD] + parts[s, h][HEAD:] for h in range(N_KV)], axis=0).T
                     for s in range(SUB)]
            pieces = [bands[0][:BLK]]
            pieces += [bands[s - 1][BLK:] + bands[s][:BLK] for s in range(1, SUB)]
            pieces.append(bands[SUB - 1][BLK:])
            return jnp.concatenate(pieces, axis=0)

        dv = over_key_rows(dv_t)
        dkn = over_key_rows(dkn_t)
        dkg_ref[...] += jnp.sum(dkn * kh, axis=0, keepdims=True)
        dkh = dkn * kg_ref[...]
        dk = rk * (dkh - kh * _pair_mean(dkh * kh, low))
        last = slice(ROWS, ROWS + BLK)
        dp_ref[:, O_K:O_V] = jnp.concatenate([dk[BLK:ROWS], dk[last] + kcar[...]], axis=0).astype(BF16)
        dp_ref[:, O_V:] = jnp.concatenate([dv[BLK:ROWS], dv[last] + vcar[...]], axis=0).astype(BF16)
        kcar[...] = dk[:BLK, :]
        vcar[...] = dv[:BLK, :]
        dsk_ref[...] += dsk
        dqg_ref[...] += dqg * SCALE

        bg = p_ref[:, 0:CONV_CH].astype(F32)
        cg = p_ref[:, CONV_CH:2 * CONV_CH].astype(F32)
        hc = p_ref[:, 2 * CONV_CH:3 * CONV_CH].astype(F32)
        yv = y_ref[...].astype(F32)
        dmc = dmix_scr[:, 0:CONV_CH]
        co = bg * yv
        rc = _rms(co)
        ch = co * rc
        dcog_ref[...] += jnp.sum(dmc * ch, axis=0, keepdims=True)
        dco = _rms_bwd(dmc, ch, rc, cog_ref[...])
        dp_ref[:, 0:CONV_CH] = (dco * yv).astype(BF16)
        dy = dco * bg
        row = lax.broadcasted_iota(jnp.int32, (ROWS, CONV_CH), 0)
        nxt = dycar[...]
        dy1 = jnp.where(row == ROWS - 1, nxt[0:1, :], pltpu.roll(dy, ROWS - 1, 0))
        dy2 = jnp.where(row == ROWS - 2, nxt[0:1, :],
                        jnp.where(row == ROWS - 1, nxt[1:2, :], pltpu.roll(dy, ROWS - 2, 0)))
        dycar[...] = dy[0:8, :]
        du = cw_ref[2:3, :] * dy + cw_ref[1:2, :] * dy1 + cw_ref[0:1, :] * dy2
        dp_ref[:, CONV_CH:2 * CONV_CH] = (du * hc).astype(BF16)
        dp_ref[:, 2 * CONV_CH:3 * CONV_CH] = (du * cg).astype(BF16)
        u = cg * hc
        dcw_ref[0:1, :] += jnp.sum(dy2 * u, axis=0, keepdims=True)
        dcw_ref[1:2, :] += jnp.sum(dy1 * u, axis=0, keepdims=True)
        dcw_ref[2:3, :] += jnp.sum(dy * u, axis=0, keepdims=True)

        dmix_scr[...] = _dot_nt(dx_next_ref[...].astype(BF16), wo_ref[...])

    small = lambda shape: pl.BlockSpec(shape, lambda s: (0, 0))
    blk = lambda w: pl.BlockSpec((ROWS, w), lambda s: (steps - 1 - s, 0))
    return _pallas_after(
        dep, body, sinks, proj, proj, bias, y, ao, dx1, dx1, w_o, conv_w, qg, kg, cog, aog, name=name, grid=(steps,),
        in_specs=[pl.BlockSpec(memory_space=pltpu.SMEM),
                  blk(IN_COLS),
                  pl.BlockSpec((BLK, KV_W), lambda s: (jnp.maximum(SUB * (steps - 1 - s) - 1, 0), O_K // KV_W)),
                  pl.BlockSpec((2, BLK, 2 * BLK), lambda s: (0, 0, 0)),
                  blk(CONV_CH), blk(ATTN_W),
                  pl.BlockSpec((ROWS, D_MODEL), lambda s: (steps - 1, 0)),
                  pl.BlockSpec((ROWS, D_MODEL), lambda s: (jnp.maximum(steps - 2 - s, 0), 0)),
                  small((D_MODEL, D_MODEL)),
                  small((3, CONV_CH)), small((1, PAIR)), small((1, PAIR)), small((1, CONV_CH)), small((1, ATTN_W))],
        out_specs=[blk(IN_COLS), small((3, CONV_CH)), small((1, PAIR)), small((1, PAIR)), small((1, N_Q)),
                   small((1, CONV_CH)), small((1, ATTN_W))],
        out_shape=[jax.ShapeDtypeStruct((S, IN_COLS), BF16), jax.ShapeDtypeStruct((3, CONV_CH), F32),
                   jax.ShapeDtypeStruct((1, PAIR), F32), jax.ShapeDtypeStruct((1, PAIR), F32),
                   jax.ShapeDtypeStruct((1, N_Q), F32), jax.ShapeDtypeStruct((1, CONV_CH), F32),
                   jax.ShapeDtypeStruct((1, ATTN_W), F32)],
        scratch_shapes=[pltpu.VMEM((8, CONV_CH), F32), pltpu.VMEM((BLK, PAIR), F32), pltpu.VMEM((BLK, PAIR), F32),
                        pltpu.VMEM((ROWS, D_MODEL), F32)],
        compiler_params=_cparams(("arbitrary",)),
    )


OTHER_CHIPS = ((1, 0), (0, 1), (1, 1))


def _position():
    return lax.axis_index("x"), lax.axis_index("y"), lax.axis_index("c")


def all_gather(shards, name):
    n = len(shards)

    def body(*refs):
        x_refs, out_refs = refs[:n], refs[n:2 * n]
        send_sems, recv_sems, local_sems = refs[2 * n:]
        x, y, c = _position()
        me, sibling = (x, y, c), (x, y, 1 - c)
        chips = [(x ^ mx, y ^ my) for mx, my in OTHER_CHIPS]

        def slab(a, px, py, pc):
            return out_refs[a].at[4 * px + 2 * py + pc]

        def copy(a, k, block, to, src=None):
            return pltpu.make_async_remote_copy(
                src_ref=slab(a, *block) if src is None else src, dst_ref=slab(a, *block),
                send_sem=send_sems.at[7 * a + k], recv_sem=recv_sems.at[7 * a + k], device_id=to, device_id_type=MESH_T)

        mine = [pltpu.make_async_copy(x_refs[a], slab(a, *me), local_sems.at[a]) for a in range(n)]
        for cp in mine:
            cp.start()
        first = []
        for a in range(n):
            first.append(copy(a, 0, me, sibling, src=x_refs[a]))
            first += [copy(a, 1 + j, me, (*chip, c), src=x_refs[a]) for j, chip in enumerate(chips)]
        for cp in first:
            cp.start()
        passed = []
        for a in range(n):
            for j, chip in enumerate(chips):
                copy(a, 1 + j, (*chip, c), me).wait_recv()
                passed.append(copy(a, 4 + j, (*chip, c), sibling))
                passed[-1].start()
        for a in range(n):
            copy(a, 0, sibling, me).wait_recv()
            for j, chip in enumerate(chips):
                copy(a, 4 + j, (*chip, 1 - c), me).wait_recv()
        for cp in first + passed:
            cp.wait_send()
        for cp in mine:
            cp.wait()

    return pl.pallas_call(
        body, name=name, in_specs=[ANY] * n, out_specs=[ANY] * n,
        out_shape=[jax.ShapeDtypeStruct((N_DEV, *s.shape), s.dtype) for s in shards],
        scratch_shapes=[pltpu.SemaphoreType.DMA((7 * n,)), pltpu.SemaphoreType.DMA((7 * n,)),
                        pltpu.SemaphoreType.DMA((n,))],
    )(*shards)


class SplitCopy:
    def __init__(self, name, arrays, n_copies, plan, after=None):
        n = len(arrays)
        self.name, self.n, self.n_copies, self.plan = name, n, n_copies, plan
        extra = [] if after is None else [after]

        def body(*refs):
            in_refs = refs[:n]
            send_sems, recv_sems = refs[n + len(extra)], refs[n + len(extra) + 1]
            token = refs[2 * n + len(extra) + 2]
            for k, (src, dst, to) in enumerate(plan(_position(), in_refs)):
                pltpu.make_async_remote_copy(src_ref=src, dst_ref=dst, send_sem=send_sems.at[k],
                                             recv_sem=recv_sems.at[k], device_id=to, device_id_type=MESH_T).start()
            token[...] = jnp.zeros_like(token)

        outs = pl.pallas_call(
            body, name=name + "_start",
            out_shape=(pltpu.SemaphoreType.DMA((n_copies,)), pltpu.SemaphoreType.DMA((n_copies,)),
                       *[pltpu.HBM(a.shape, a.dtype) for a in arrays], jax.ShapeDtypeStruct((8, 128), F32)),
            in_specs=[HBM] * n + [ANY] * len(extra),
            out_specs=(SEM, SEM, *[HBM] * n, pl.BlockSpec(memory_space=pltpu.VMEM)),
            input_output_aliases={i: 2 + i for i in range(n)},
            compiler_params=pltpu.CompilerParams(has_side_effects=pltpu.SideEffectType.DATAFLOW_SIDE_EFFECTING),
        )(*[pltpu.with_memory_space_constraint(a, pltpu.HBM) for a in arrays], *extra)
        self.send_sems, self.recv_sems = outs[0], outs[1]
        self.arrays, self.token = list(outs[2:2 + n]), outs[2 + n]

    def wait(self, after):
        n, plan = self.n, self.plan

        def body(*refs):
            in_refs, send_sems, recv_sems = refs[:n], refs[n], refs[n + 1]
            for k, (src, dst, to) in enumerate(plan(_position(), in_refs)):
                cp = pltpu.make_async_remote_copy(src_ref=src, dst_ref=dst, send_sem=send_sems.at[k],
                                                  recv_sem=recv_sems.at[k], device_id=to, device_id_type=MESH_T)
                cp.wait_send()
                cp.wait_recv()

        outs = pl.pallas_call(
            body, name=self.name + "_wait",
            out_shape=tuple(pltpu.HBM(a.shape, a.dtype) for a in self.arrays),
            in_specs=[HBM] * n + [SEM, SEM, ANY], out_specs=tuple([HBM] * n),
            input_output_aliases={i: i for i in range(n)},
            compiler_params=pltpu.CompilerParams(has_side_effects=pltpu.SideEffectType.DATAFLOW_SIDE_EFFECTING),
        )(*self.arrays, self.send_sems, self.recv_sems, after)
        return list(outs)


def gather_start(shards, me, name, after=None):
    n = len(shards)
    lands = [lax.dynamic_update_slice(lax.empty((N_DEV, *s.shape), s.dtype), s[None], (me, 0, 0)) for s in shards]

    def plan(pos, refs):
        x, y, c = pos
        return [(refs[a], refs[n + a].at[4 * x + 2 * y + c], (x ^ mx, y ^ my, c))
                for a in range(n) for mx, my in OTHER_CHIPS]

    return SplitCopy(name, list(shards) + lands, 3 * n, plan, after=after)


def sibling_start(lands, name):
    n = len(lands)

    def plan(pos, refs):
        x, y, c = pos
        return [(refs[a].at[2 * q + c], refs[a].at[2 * q + c], (x, y, 1 - c)) for a in range(n) for q in range(4)]

    return SplitCopy(name, list(lands), 4 * n, plan)


def scatter_start(slabs, name):
    n = len(slabs)
    lands = [lax.empty((N_DEV - 1, *g.shape[1:]), g.dtype) for g in slabs]

    def plan(pos, refs):
        x, y, c = pos
        copies = []
        for a in range(n):
            for r in range(1, N_DEV):
                px, py, pc = x ^ ((r >> 2) & 1), y ^ ((r >> 1) & 1), c ^ (r & 1)
                copies.append((refs[a].at[4 * px + 2 * py + pc], refs[n + a].at[r - 1], (px, py, pc)))
        return copies

    return SplitCopy(name, list(slabs) + lands, (N_DEV - 1) * n, plan)


def all_reduce_small(v, name, dep=None):
    R, W = v.shape

    def body(v_ref, o_ref, recv, send_sems, recv_sems):
        x, y, c = _position()
        me = 4 * x + 2 * y + c
        copies = []
        for r in range(1, N_DEV):
            to = (x ^ ((r >> 2) & 1), y ^ ((r >> 1) & 1), c ^ (r & 1))
            copies.append(pltpu.make_async_remote_copy(
                src_ref=v_ref, dst_ref=recv.at[me], send_sem=send_sems.at[r - 1], recv_sem=recv_sems.at[r - 1],
                device_id=to, device_id_type=MESH_T))
        for cp in copies:
            cp.start()
        recv[pl.ds(me, 1)] = v_ref[...][None]
        for cp in copies:
            cp.wait()
        acc = recv[0]
        for s in range(1, N_DEV):
            acc = acc + recv[s]
        o_ref[...] = acc

    return _pallas_after(
        dep, body, v, name=name,
        in_specs=[pl.BlockSpec(memory_space=pltpu.VMEM)], out_specs=pl.BlockSpec(memory_space=pltpu.VMEM),
        out_shape=jax.ShapeDtypeStruct((R, W), F32),
        scratch_shapes=[pltpu.VMEM((N_DEV, R, W), F32), pltpu.SemaphoreType.DMA((N_DEV - 1,)),
                        pltpu.SemaphoreType.DMA((N_DEV - 1,))],
    )


def _row_tile(rows):
    if rows <= 512:
        return rows
    return max(t for t in range(8, 513, 8) if rows % t == 0)


def _adamw_update(w, g, m, v):
    mn = ADAM_B1 * m + (1.0 - ADAM_B1) * g
    vn = ADAM_B2 * v + (1.0 - ADAM_B2) * (g * g)
    m_hat = mn / (1.0 - ADAM_B1 ** ADAM_STEP)
    v_hat = vn / (1.0 - ADAM_B2 ** ADAM_STEP)
    return -ADAM_LR * (m_hat / (jnp.sqrt(v_hat) + ADAM_EPS) + ADAM_WD * w), mn, vn


def adamw(w, g, m, v, name):
    R, W = w.shape
    tr = _row_tile(R)

    def body(w_ref, g_ref, m_ref, v_ref, d_ref, mo_ref, vo_ref):
        d_ref[...], mo_ref[...], vo_ref[...] = _adamw_update(w_ref[...], g_ref[...], m_ref[...], v_ref[...])

    spec = pl.BlockSpec((tr, W), lambda i: (i, 0))
    return pl.pallas_call(
        body, name=name, grid=(R // tr,), in_specs=[spec] * 4, out_specs=[spec] * 3,
        out_shape=[jax.ShapeDtypeStruct((R, W), F32)] * 3,
        compiler_params=_cparams(("parallel",)),
    )(w, g, m, v)


def reduce_adamw(slabs, land, w, m, v, layer, me_arr, name, others=None):
    _, R, W = slabs.shape
    tr = _row_tile(R)
    n_other = 0 if others is None else 4

    def body(me_ref, s_ref, l_ref, w_ref, m_ref, v_ref, *rest):
        g_ref, d_ref, mo_ref, vo_ref = rest[n_other:]
        g = s_ref[0].astype(F32)
        for r in range(N_DEV - 1):
            g = g + l_ref[r].astype(F32)
        g_ref[0] = g
        d_ref[0], mo_ref[0], vo_ref[0] = _adamw_update(w_ref[0], g, m_ref[0], v_ref[0])

    spec = pl.BlockSpec((1, tr, W), lambda i, me: (layer, i, 0))
    return pl.pallas_call(
        body, name=name,
        grid_spec=pltpu.PrefetchScalarGridSpec(
            num_scalar_prefetch=1, grid=(R // tr,),
            in_specs=[pl.BlockSpec((1, tr, W), lambda i, me: (me[0], i, 0)),
                      pl.BlockSpec((N_DEV - 1, tr, W), lambda i, me: (0, i, 0)), spec, spec, spec] + [ANY] * n_other,
            out_specs=[spec] * 4),
        out_shape=[jax.ShapeDtypeStruct((DEPTH, R, W), F32)] * 4,
        input_output_aliases={6 + i: i for i in range(n_other)},
        compiler_params=_cparams(("parallel",)),
    )(me_arr, slabs, land, w, m, v, *([] if others is None else others))


SMALL_NAMES = ("norm1_g", "q_norm_g", "k_norm_g", "sinks", "conv_out_g", "attn_out_g", "norm2_g", "conv_w")
SMALL_SIZES = (D_MODEL, HEAD, HEAD, N_Q, CONV_CH, ATTN_W, D_MODEL, 3 * CONV_CH)
SMALL_ROWS = 80


def kernel(x, norm1_g, w_in, conv_w, q_norm_g, k_norm_g, sinks, conv_out_g, attn_out_g, w_o, norm2_g, w_gate, w_up, w_down, loss_target, m_norm1_g, m_w_in, m_conv_w, m_q_norm_g, m_k_norm_g, m_sinks, m_conv_out_g, m_attn_out_g, m_w_o, m_norm2_g, m_w_gate, m_w_up, m_w_down, v_norm1_g, v_w_in, v_conv_w, v_q_norm_g, v_k_norm_g, v_sinks, v_conv_out_g, v_attn_out_g, v_w_o, v_norm2_g, v_w_gate, v_w_up, v_w_down):
    xi, yi, ci = _position()
    me = 4 * xi + 2 * yi + ci
    me_arr = jnp.reshape(me, (1,)).astype(jnp.int32)
    xs, tgt = x[0], loss_target[0]
    bf = lambda a: a.astype(BF16)
    bias = band_bias()

    t = lambda a: jnp.swapaxes(a, 1, 2)
    shard = dict(w_in=(t(w_in), t(m_w_in), t(v_w_in)), w_o=(w_o, m_w_o, v_w_o),
                 w_gate=(t(w_gate), t(m_w_gate), t(v_w_gate)), w_up=(t(w_up), t(m_w_up), t(v_w_up)),
                 w_down=(w_down, m_w_down, v_w_down))
    wb = {n: bf(shard[n][0]) for n in shard}

    g_in0, g_o0, g_conv = all_gather([wb["w_in"][0], wb["w_o"][0], conv_w.reshape(DEPTH * 3, HEAD)], "gather_first")
    ag_ffn0 = gather_start([wb["w_gate"][0], wb["w_up"][0], wb["w_down"][0]], me, "gather_ffn0", after=g_in0)
    ag_mix1 = gather_start([wb["w_in"][1], wb["w_o"][1]], me, "gather_mix1", after=ag_ffn0.token)
    ag_ffn1 = gather_start([wb["w_gate"][1], wb["w_up"][1], wb["w_down"][1]], me, "gather_ffn1", after=ag_mix1.token)
    conv_full = g_conv.reshape(N_DEV, DEPTH, 3, HEAD).transpose(1, 2, 0, 3).reshape(DEPTH, 3, CONV_CH)
    pair_gain = lambda g: jnp.tile(g[None], (1, 2))
    small = [dict(norm1_g=norm1_g[l][None], conv_w=conv_full[l], q_norm_g=pair_gain(q_norm_g[l]),
                  k_norm_g=pair_gain(k_norm_g[l]), sinks=sinks[l], conv_out_g=conv_out_g[l][None],
                  attn_out_g=attn_out_g[l][None], norm2_g=norm2_g[l][None]) for l in range(DEPTH)]
    whole = lambda g: g.reshape(-1, D_MODEL)
    weights = [dict(w_in=whole(g_in0), w_o=whole(g_o0)), {}]

    def ffn_weights(g_gate, g_up, g_down):
        return dict(w_gate=whole(g_gate), w_up=whole(g_up), w_down=whole(g_down))

    saved = []
    xl = xs
    for l in range(DEPTH):
        sp, wl = small[l], weights[l]
        h, proj = norm_proj(xl, sp["norm1_g"], wl["w_in"], f"norm_proj{l}",
                            dep=ag_ffn1.token if l == 0 else pass_ffn1.token)
        mixer_args = (proj, bias, sp["sinks"], sp["conv_w"], sp["q_norm_g"], sp["k_norm_g"], sp["conv_out_g"],
                      sp["attn_out_g"], f"mixer_fwd{l}")
        if l == 0:
            mix, y, ao = mixer_fwd(*mixer_args)
            pass_ffn0 = sibling_start(ag_ffn0.wait(mix)[3:], "pass_ffn0")
            x1 = matmul_residual(mix, wl["w_o"], xl, "out_proj0", dep=pass_ffn0.token)
            wl.update(ffn_weights(*pass_ffn0.wait(x1)))
            pass_mix1 = sibling_start(ag_mix1.wait(x1)[2:], "pass_mix1")
            h2, a, b, f, x2 = ffn_fwd(x1, sp["norm2_g"], wl["w_gate"], wl["w_up"], wl["w_down"], "ffn_fwd0",
                                      dep=pass_mix1.token)
            g_in, g_o = pass_mix1.wait(x2)
            weights[1] = dict(w_in=whole(g_in), w_o=whole(g_o))
            pass_ffn1 = sibling_start(ag_ffn1.wait(x2)[3:], "pass_ffn1")
        else:
            mix, y, ao, x1 = mixer_fwd(*mixer_args, residual=xl, w_o=wl["w_o"])
            wl.update(ffn_weights(*pass_ffn1.wait(x1)))
            h2, a, b, f, dx, loss_row = ffn_fwd(x1, sp["norm2_g"], wl["w_gate"], wl["w_up"], wl["w_down"], "ffn_fwd1",
                                                tgt=tgt)
            x2 = None
        saved.append((xl, h, proj, mix, y, ao, x1, h2, a, b, f))
        xl = x2

    stepped = {n: None for n in shard}
    slabs = lambda d: d.reshape(N_DEV, -1, D_MODEL)
    gsmall = [None] * DEPTH

    def finish(sc, names, after, l):
        arrays = sc.wait(after)
        k = len(names)
        for i, n in enumerate(names):
            w, m, v = shard[n]
            stepped[n] = reduce_adamw(arrays[i], arrays[k + i], w, m, v, l, me_arr, f"reduce_adamw_{n}{l}",
                                      others=stepped[n])

    for l in reversed(range(DEPTH)):
        sp, wl = small[l], weights[l]
        x0, h, proj, mix, y, ao, x1, h2, a, b, f = saved[l]
        (d_wd,) = grad_weight((f,), dx, TF, f"grad_w_down{l}")
        sc_down = scatter_start([slabs(d_wd)], f"scatter_w_down{l}")
        da, db, dx1, d_g2 = ffn_bwd(dx, a, b, x1, sp["norm2_g"], wl["w_gate"], wl["w_up"], wl["w_down"],
                                    f"ffn_bwd{l}", dep=sc_down.token)
        d_wg, d_wu = grad_weight((da, db), h2, TF, f"grad_w_gate_up{l}")
        (d_wo,) = grad_weight((mix,), dx1, D_MODEL, f"grad_w_o{l}")
        sc_rest = scatter_start([slabs(d_wg), slabs(d_wu), slabs(d_wo)], f"scatter_w_gate_up_o{l}")
        dproj, d_cw, d_qg, d_kg, d_sk, d_cog, d_aog = mixer_bwd(
            proj, bias, y, ao, dx1, wl["w_o"], sp["sinks"], sp["conv_w"], sp["q_norm_g"], sp["k_norm_g"],
            sp["conv_out_g"], sp["attn_out_g"], f"mixer_bwd{l}", dep=sc_rest.token)
        finish(sc_down, ["w_down"], dproj, l)
        finish(sc_rest, ["w_gate", "w_up", "w_o"], dproj, l)
        (d_win,) = grad_weight((dproj,), h, IN_COLS // 2, f"grad_w_in{l}")
        sc_in = scatter_start([slabs(d_win)], f"scatter_w_in{l}")
        dx, d_g1 = proj_bwd(dproj, wl["w_in"], x0, sp["norm1_g"], dx1, f"proj_bwd{l}", dep=sc_in.token)
        finish(sc_in, ["w_in"], dx, l)
        both_heads = lambda d: d[:, :HEAD] + d[:, HEAD:]
        gsmall[l] = dict(norm1_g=d_g1, conv_w=d_cw, q_norm_g=both_heads(d_qg), k_norm_g=both_heads(d_kg), sinks=d_sk,
                         conv_out_g=d_cog, attn_out_g=d_aog, norm2_g=d_g2)
    grad_x = dx[None]

    flat = jnp.concatenate([loss_row[0, 0:1]] + [gsmall[l][n].reshape(-1) for l in range(DEPTH) for n in SMALL_NAMES])
    flat = jnp.pad(flat, (0, SMALL_ROWS * 128 - flat.shape[0])).reshape(SMALL_ROWS, 128)
    flat = all_reduce_small(flat, "all_reduce_small_grads").reshape(-1)
    loss = flat[0]
    gs = {n: [] for n in SMALL_NAMES}
    off = 1
    for l in range(DEPTH):
        for n, size in zip(SMALL_NAMES, SMALL_SIZES):
            gs[n].append(flat[off:off + size])
            off += size
    gs = {n: jnp.stack(v) for n, v in gs.items()}
    g_conv = lax.dynamic_slice(gs["conv_w"].reshape(DEPTH, 3, CONV_CH), (0, 0, me * HEAD), (DEPTH, 3, HEAD))

    gs["conv_w"] = g_conv
    params = dict(norm1_g=(norm1_g, m_norm1_g, v_norm1_g), conv_w=(conv_w, m_conv_w, v_conv_w),
                  q_norm_g=(q_norm_g, m_q_norm_g, v_q_norm_g), k_norm_g=(k_norm_g, m_k_norm_g, v_k_norm_g),
                  sinks=(sinks, m_sinks, v_sinks), conv_out_g=(conv_out_g, m_conv_out_g, v_conv_out_g),
                  attn_out_g=(attn_out_g, m_attn_out_g, v_attn_out_g), norm2_g=(norm2_g, m_norm2_g, v_norm2_g))
    names = ("norm1_g", "w_in", "conv_w", "q_norm_g", "k_norm_g", "sinks", "conv_out_g", "attn_out_g", "w_o",
             "norm2_g", "w_gate", "w_up", "w_down")

    out = {}
    for n in names:
        if n in shard:
            out[n] = tuple(t(r) for r in stepped[n]) if n in ("w_in", "w_gate", "w_up") else stepped[n]
        else:
            w, m, v = params[n]
            two_d = (-1, w.shape[-1])
            d, mn, vn = adamw(w.reshape(two_d), gs[n].reshape(two_d), m.reshape(two_d), v.reshape(two_d), f"adamw_{n}")
            out[n] = (gs[n].reshape(w.shape), d.reshape(w.shape), mn.reshape(w.shape), vn.reshape(w.shape))
    return (loss, grad_x, *[out[n][i] for i in range(4) for n in names])
```

```python
import functools

import jax
import jax.numpy as jnp
from jax import lax
from jax.experimental import pallas as pl
from jax.experimental.pallas import tpu as pltpu

F32 = jnp.float32
BF16 = jnp.bfloat16

D_MODEL = 1024
CONV_CH = 512
ATTN_W = 512
N_Q = 8
N_KV = 2
GRP = N_Q // N_KV
HEAD = 64
IN_COLS = 2304
D_FF = 2816
BLK = 128
O_Q = 3 * CONV_CH
O_K = O_Q + ATTN_W
O_V = O_K + N_KV * HEAD
EPS = 1e-6
NEG_INF = -1e30
SCALE = HEAD ** -0.5
N_DEV = 8
DEPTH = 2

ADAM_LR = 0.001
ADAM_B1 = 0.9
ADAM_B2 = 0.999
ADAM_EPS = 1e-08
ADAM_WD = 0.01
ADAM_STEP = 10

VMEM_LIMIT = 56 * 1024 * 1024
TM = 512
MESH_T = pl.DeviceIdType.MESH

ROWS_IN, ROWS_O, ROWS_FF = IN_COLS // N_DEV, D_MODEL // N_DEV, D_FF // N_DEV


ANY = pl.BlockSpec(memory_space=pl.ANY)
HBM = pl.BlockSpec(memory_space=pltpu.HBM)
SEM = pl.BlockSpec(memory_space=pltpu.SEMAPHORE)


def _cparams(sem):
    return pltpu.CompilerParams(dimension_semantics=sem, vmem_limit_bytes=VMEM_LIMIT)


def _pallas_after(dep, body, *args, in_specs, **kw):
    if dep is None:
        return pl.pallas_call(body, in_specs=in_specs, **kw)(*args)

    def after_dep(dep_ref, *refs):
        body(*refs)

    return pl.pallas_call(after_dep, in_specs=[ANY, *in_specs], **kw)(dep, *args)


def _dot(a, b):
    return jnp.dot(a, b, preferred_element_type=F32)


def _dot_nt(a, b):
    return lax.dot_general(a, b, (((1,), (1,)), ((), ())), preferred_element_type=F32)


def _dot_tn(a, b):
    return lax.dot_general(a, b, (((0,), (0,)), ((), ())), preferred_element_type=F32)


LANES = 128


def _row_reduce(v, op, reduce):
    w = v.shape[-1]
    if w > LANES and w % LANES == 0:
        acc = v[:, 0:LANES]
        for c in range(1, w // LANES):
            acc = op(acc, v[:, LANES * c:LANES * (c + 1)])
        v = acc
    return reduce(v, axis=-1, keepdims=True)


def _row_sum(v):
    return _row_reduce(v, jnp.add, jnp.sum)


def _row_mean(v):
    return _row_sum(v) * (1.0 / v.shape[-1])


def _rms(v):
    return lax.rsqrt(_row_mean(v * v) + EPS)


def _sigmoid(v):
    return 1.0 / (1.0 + jnp.exp(-v))


def _rms_bwd(dyv, xh, r, g):
    dxh = dyv * g
    return r * (dxh - xh * _row_mean(dxh * xh))


def norm_proj(x, g, w_t, name, dep=None):
    S, N = x.shape[0], w_t.shape[0]

    def body(x_ref, g_ref, w_ref, h_ref, p_ref):
        xf = x_ref[...]
        h = ((xf * _rms(xf)) * g_ref[...]).astype(BF16)
        h_ref[...] = h
        p_ref[...] = _dot_nt(h, w_ref[...]).astype(BF16)

    return _pallas_after(
        dep, body, x, g, w_t, name=name, grid=(S // TM,),
        in_specs=[pl.BlockSpec((TM, D_MODEL), lambda i: (i, 0)),
                  pl.BlockSpec((1, D_MODEL), lambda i: (0, 0)),
                  pl.BlockSpec((N, D_MODEL), lambda i: (0, 0))],
        out_specs=[pl.BlockSpec((TM, D_MODEL), lambda i: (i, 0)),
                   pl.BlockSpec((TM, N), lambda i: (i, 0))],
        out_shape=[jax.ShapeDtypeStruct((S, D_MODEL), BF16), jax.ShapeDtypeStruct((S, N), BF16)],
        compiler_params=_cparams(("parallel",)),
    )


def band_bias():
    qi = lax.broadcasted_iota(jnp.int32, (BLK, 2 * BLK), 0)
    kj = lax.broadcasted_iota(jnp.int32, (BLK, 2 * BLK), 1)
    diff = qi + BLK - kj
    valid = (diff >= 0) & (diff < BLK)
    return jnp.stack([jnp.where(valid & (kj >= BLK), 0.0, NEG_INF), jnp.where(valid, 0.0, NEG_INF)]).astype(F32)


def _softmax_with_sink(s, sink):
    m = jnp.maximum(_row_reduce(s, jnp.maximum, jnp.max), sink)
    p = jnp.exp(s - m)
    es = jnp.exp(sink - m)
    inv = 1.0 / (_row_sum(p) + es)
    return p * inv, es * inv


PAIR = 2 * HEAD


def _low_half():
    return lax.broadcasted_iota(jnp.int32, (1, PAIR), 1) < HEAD


def _half_sums(v, low):
    return (jnp.sum(jnp.where(low, v, 0.0), axis=-1, keepdims=True),
            jnp.sum(jnp.where(low, 0.0, v), axis=-1, keepdims=True))


def _pair_mean(v, low):
    e, o = _half_sums(v, low)
    return jnp.where(low, e, o) * (1.0 / HEAD)


def _pair_rms(v, low):
    return lax.rsqrt(_pair_mean(v * v, low) + EPS)


def _one_head_in_both_halves(pair, low):
    swapped = pltpu.roll(pair, HEAD, 1)
    return jnp.where(low, pair, swapped), jnp.where(low, swapped, pair)


SUB_FWD, SUB_FWD_PROJECT, SUB_BWD = 2, 4, 4


def _kv_pairs(p_ref, kvp_ref):
    k = jnp.concatenate([kvp_ref[:, 0:PAIR], p_ref[:, O_K:O_K + PAIR]], axis=0).astype(F32)
    v = jnp.concatenate([kvp_ref[:, PAIR:2 * PAIR], p_ref[:, O_V:O_V + PAIR]], axis=0)
    return k, v


def _band(s):
    return slice(BLK * s, BLK * s + 2 * BLK)


def _block(s):
    return slice(BLK * s, BLK * (s + 1))


def mixer_fwd(proj, bias, sinks, conv_w, qg, kg, cog, aog, name, residual=None, w_o=None):
    S = proj.shape[0]
    project = w_o is not None
    SUB = SUB_FWD_PROJECT if project else SUB_FWD
    ROWS = SUB * BLK
    steps = S // ROWS

    def body(sinks_ref, p_ref, kvp_ref, bias_ref, cw_ref, qg_ref, kg_ref, cog_ref, aog_ref, *rest):
        if project:
            x_ref, wo_ref, mix_ref, y_ref, ao_ref, x1_ref, ucar = rest
        else:
            mix_ref, y_ref, ao_ref, ucar = rest
        n = pl.program_id(0)

        @pl.when(n == 0)
        def _():
            ucar[...] = jnp.zeros_like(ucar)

        bg = p_ref[:, 0:CONV_CH].astype(F32)
        u = p_ref[:, CONV_CH:2 * CONV_CH].astype(F32) * p_ref[:, 2 * CONV_CH:3 * CONV_CH].astype(F32)
        row = lax.broadcasted_iota(jnp.int32, (ROWS, CONV_CH), 0)
        prev = ucar[...]
        u1 = jnp.where(row == 0, prev[7:8, :], pltpu.roll(u, 1, 0))
        u2 = jnp.where(row == 0, prev[6:7, :], jnp.where(row == 1, prev[7:8, :], pltpu.roll(u, 2, 0)))
        ucar[...] = u[ROWS - 8:ROWS, :]
        y = cw_ref[0:1, :] * u2 + cw_ref[1:2, :] * u1 + cw_ref[2:3, :] * u
        y_ref[...] = y.astype(BF16)
        co = bg * y
        mix_conv = ((co * _rms(co)) * cog_ref[...]).astype(BF16)
        mix_ref[:, 0:CONV_CH] = mix_conv
        if project:
            x1 = x_ref[...] + _dot(mix_conv, wo_ref[0:CONV_CH, :])

        low = _low_half()
        q_gain = qg_ref[...] * SCALE
        k, v = _kv_pairs(p_ref, kvp_ref)
        kn = ((k * _pair_rms(k, low)) * kg_ref[...]).astype(BF16)
        k_of = _one_head_in_both_halves(kn, low)
        v_of = _one_head_in_both_halves(v, low)
        biases = [bias_ref[jnp.minimum(n, 1)]] + [bias_ref[1]] * (SUB - 1)
        units = [(s, hq) for s in range(SUB) for hq in range(N_Q)]
        q_one = []
        for pair in range(N_Q // 2):
            q = p_ref[:, O_Q + PAIR * pair:O_Q + PAIR * (pair + 1)].astype(F32)
            qn = ((q * _pair_rms(q, low)) * q_gain).astype(BF16)
            q_one += [jnp.where(mine, qn, jnp.zeros_like(qn)) for mine in (low, jnp.logical_not(low))]
        scores = [_dot_nt(q_one[hq][_block(s)], k_of[hq // GRP][_band(s)]) + biases[s] for s, hq in units]
        probs = [_softmax_with_sink(sc, sinks_ref[hq])[0].astype(BF16) for (s, hq), sc in zip(units, scores)]
        o = {u_: _dot(pn, v_of[u_[1] // GRP][_band(u_[0])]) for u_, pn in zip(units, probs)}
        ao = jnp.concatenate(
            [jnp.concatenate([jnp.where(low, o[s, 2 * pair], o[s, 2 * pair + 1]) for pair in range(N_Q // 2)], axis=1)
             for s in range(SUB)], axis=0)
        ao_ref[...] = ao.astype(BF16)
        mix_attn = ((ao * _rms(ao)) * aog_ref[...]).astype(BF16)
        mix_ref[:, CONV_CH:] = mix_attn
        if project:
            x1_ref[...] = x1 + _dot(mix_attn, wo_ref[CONV_CH:, :])

    small = lambda shape: pl.BlockSpec(shape, lambda n: (0, 0))
    rows = lambda w: pl.BlockSpec((ROWS, w), lambda n: (n, 0))
    return pl.pallas_call(
        body, name=name, grid=(steps,),
        in_specs=[pl.BlockSpec(memory_space=pltpu.SMEM),
                  rows(IN_COLS),
                  pl.BlockSpec((BLK, 2 * PAIR), lambda n: (jnp.maximum(SUB * n - 1, 0), O_K // (2 * PAIR))),
                  pl.BlockSpec((2, BLK, 2 * BLK), lambda n: (0, 0, 0)),
                  small((3, CONV_CH)), small((1, PAIR)), small((1, PAIR)), small((1, CONV_CH)), small((1, ATTN_W))]
        + ([rows(D_MODEL), small((D_MODEL, D_MODEL))] if project else []),
        out_specs=[rows(D_MODEL), rows(CONV_CH), rows(ATTN_W)] + ([rows(D_MODEL)] if project else []),
        out_shape=[jax.ShapeDtypeStruct((S, D_MODEL), BF16), jax.ShapeDtypeStruct((S, CONV_CH), BF16),
                   jax.ShapeDtypeStruct((S, ATTN_W), BF16)]
        + ([jax.ShapeDtypeStruct((S, D_MODEL), F32)] if project else []),
        scratch_shapes=[pltpu.VMEM((8, CONV_CH), F32)],
        compiler_params=_cparams(("arbitrary",)),
    )(sinks, proj, proj, bias, conv_w, qg, kg, cog, aog, *([residual, w_o] if project else []))


def matmul_residual(a, w, res, name, dep=None):
    S, K = a.shape
    N = w.shape[1]

    def body(a_ref, w_ref, r_ref, o_ref):
        o_ref[...] = r_ref[...] + _dot(a_ref[...], w_ref[...])

    return _pallas_after(
        dep, body, a, w, res, name=name, grid=(S // TM,),
        in_specs=[pl.BlockSpec((TM, K), lambda i: (i, 0)), pl.BlockSpec((K, N), lambda i: (0, 0)),
                  pl.BlockSpec((TM, N), lambda i: (i, 0))],
        out_specs=pl.BlockSpec((TM, N), lambda i: (i, 0)),
        out_shape=jax.ShapeDtypeStruct((S, N), F32),
        compiler_params=_cparams(("parallel",)),
    )


TF = 1408


def ffn_fwd(x1, g, wg, wu, wd, name, dep=None, tgt=None):
    S = x1.shape[0]
    ni = S // TM
    with_loss = tgt is not None
    row = pl.BlockSpec((TM, D_MODEL), lambda i: (i, 0))
    vec = pl.BlockSpec((1, D_MODEL), lambda i: (0, 0))
    acts = [jax.ShapeDtypeStruct((S, D_FF), BF16)] * 3

    def half_weights(j):
        return [pl.BlockSpec((TF, D_MODEL), lambda i: (j, 0))] * 3

    def swiglu(h2, wg_ref, wu_ref, a_ref, b_ref, f_ref):
        a = _dot_nt(h2, wg_ref[...])
        b = _dot_nt(h2, wu_ref[...])
        a_ref[...] = a.astype(BF16)
        b_ref[...] = b.astype(BF16)
        f = ((a * _sigmoid(a)) * b).astype(BF16)
        f_ref[...] = f
        return f

    def first(x_ref, g_ref, wg_ref, wu_ref, wd_ref, h2_ref, a_ref, b_ref, f_ref, part_ref):
        xf = x_ref[...]
        h2 = ((xf * _rms(xf)) * g_ref[...]).astype(BF16)
        h2_ref[...] = h2
        part_ref[...] = xf + _dot(swiglu(h2, wg_ref, wu_ref, a_ref, b_ref, f_ref), wd_ref[...])

    h2, a, b, f, part = _pallas_after(
        dep, first, x1, g, wg, wu, wd, name=name + "_a", grid=(ni,),
        in_specs=[row, vec] + half_weights(0),
        out_specs=[row] + [pl.BlockSpec((TM, TF), lambda i: (i, 0))] * 3 + [row],
        out_shape=[jax.ShapeDtypeStruct((S, D_MODEL), BF16)] + acts + [jax.ShapeDtypeStruct((S, D_MODEL), F32)],
        compiler_params=_cparams(("parallel",)),
    )

    def second(h2_ref, part_ref, wg_ref, wu_ref, wd_ref, *rest):
        rest = rest[3:]
        if with_loss:
            t_ref, a_ref, b_ref, f_ref, o_ref, l_ref, sq = rest
        else:
            a_ref, b_ref, f_ref, o_ref = rest
        i = pl.program_id(0)
        out = part_ref[...] + _dot(swiglu(h2_ref[...], wg_ref, wu_ref, a_ref, b_ref, f_ref), wd_ref[...])
        if not with_loss:
            o_ref[...] = out
            return
        e = out - t_ref[...]
        o_ref[...] = e * (1.0 / D_MODEL)
        col = jnp.sum(e * e, axis=0, keepdims=True)

        @pl.when(i == 0)
        def _():
            sq[...] = col

        @pl.when(i > 0)
        def _():
            sq[...] += col

        @pl.when(i == ni - 1)
        def _():
            l_ref[...] = jnp.full((1, 128), jnp.sum(sq[...]) * (0.5 / D_MODEL), F32)

    results = pl.pallas_call(
        second, name=name + "_b", grid=(ni,),
        in_specs=[row, row] + half_weights(1) + [ANY] * 3 + ([row] if with_loss else []),
        out_specs=[pl.BlockSpec((TM, TF), lambda i: (i, 1))] * 3 + [row]
        + ([pl.BlockSpec((1, 128), lambda i: (0, 0))] if with_loss else []),
        out_shape=acts + [jax.ShapeDtypeStruct((S, D_MODEL), F32)]
        + ([jax.ShapeDtypeStruct((1, 128), F32)] if with_loss else []),
        input_output_aliases={5: 0, 6: 1, 7: 2},
        scratch_shapes=[pltpu.VMEM((1, D_MODEL), F32)] if with_loss else [],
        compiler_params=_cparams(("arbitrary",) if with_loss else ("parallel",)),
    )(h2, part, wg, wu, wd, a, b, f, *([tgt] if with_loss else []))
    return (h2, *results)


def ffn_bwd(dx2, a, b, x1, g, wg, wu, wd, name, dep=None):
    S = dx2.shape[0]
    ni = S // TM
    row = pl.BlockSpec((TM, D_MODEL), lambda i: (i, 0))
    vec = pl.BlockSpec((1, D_MODEL), lambda i: (0, 0))
    acts = [jax.ShapeDtypeStruct((S, D_FF), BF16)] * 2

    def half(j):
        return [pl.BlockSpec((TM, TF), lambda i: (i, j))] * 2, [pl.BlockSpec((TF, D_MODEL), lambda i: (j, 0))] * 3

    def through_half(dx_ref, a_ref, b_ref, wg_ref, wu_ref, wd_ref, da_ref, db_ref):
        df = _dot_nt(dx_ref[...].astype(BF16), wd_ref[...])
        av = a_ref[...].astype(F32)
        bv = b_ref[...].astype(F32)
        sg = _sigmoid(av)
        da = ((df * bv) * (sg * (1.0 + av * (1.0 - sg)))).astype(BF16)
        db = (df * (av * sg)).astype(BF16)
        da_ref[...] = da
        db_ref[...] = db
        return _dot(da, wg_ref[...]) + _dot(db, wu_ref[...])

    def first(dx_ref, a_ref, b_ref, wg_ref, wu_ref, wd_ref, da_ref, db_ref, part_ref):
        part_ref[...] = through_half(dx_ref, a_ref, b_ref, wg_ref, wu_ref, wd_ref, da_ref, db_ref)

    tiles, weights = half(0)
    da, db, part = _pallas_after(
        dep, first, dx2, a, b, wg, wu, wd, name=name + "_a", grid=(ni,),
        in_specs=[row] + tiles + weights, out_specs=tiles + [row],
        out_shape=acts + [jax.ShapeDtypeStruct((S, D_MODEL), F32)],
        compiler_params=_cparams(("parallel",)),
    )

    def second(dx_ref, a_ref, b_ref, wg_ref, wu_ref, wd_ref, part_ref, x_ref, g_ref, da_prev, db_prev,
               da_ref, db_ref, dx1_ref, dg_ref):
        i = pl.program_id(0)

        @pl.when(i == 0)
        def _():
            dg_ref[...] = jnp.zeros_like(dg_ref)

        dh = part_ref[...] + through_half(dx_ref, a_ref, b_ref, wg_ref, wu_ref, wd_ref, da_ref, db_ref)
        xf = x_ref[...]
        r = _rms(xf)
        xh = xf * r
        dg_ref[...] += jnp.sum(dh * xh, axis=0, keepdims=True)
        dx1_ref[...] = dx_ref[...] + _rms_bwd(dh, xh, r, g_ref[...])

    tiles, weights = half(1)
    return pl.pallas_call(
        second, name=name + "_b", grid=(ni,),
        in_specs=[row] + tiles + weights + [row, row, vec, ANY, ANY],
        out_specs=tiles + [row, vec],
        out_shape=acts + [jax.ShapeDtypeStruct((S, D_MODEL), F32), jax.ShapeDtypeStruct((1, D_MODEL), F32)],
        input_output_aliases={9: 0, 10: 1},
        compiler_params=_cparams(("arbitrary",)),
    )(dx2, a, b, wg, wu, wd, part, x1, g, da, db)


def grad_weight(lhs, rhs, tm, name, dep=None, tk=1024):
    S, N = rhs.shape
    M = lhs[0].shape[1]
    tk = min(tk, S)
    nk = S // tk
    n_l = n_o = len(lhs)

    def body(*refs):
        l_refs, r_ref = refs[:n_l], refs[n_l]
        o_refs, accs = refs[n_l + 1:n_l + 1 + n_o], refs[n_l + 1 + n_o:]
        k = pl.program_id(1)
        rv = r_ref[...].astype(BF16)
        for l_ref, o_ref, acc in zip(l_refs, o_refs, accs):
            @pl.when(k == 0)
            def _():
                acc[...] = jnp.zeros_like(acc)

            acc[...] += _dot_tn(l_ref[...], rv)

            @pl.when(k == nk - 1)
            def _():
                o_ref[...] = acc[...].astype(BF16)

    return _pallas_after(
        dep, body, *lhs, rhs, name=name, grid=(M // tm, nk),
        in_specs=[pl.BlockSpec((tk, tm), lambda i, k: (k, i)) for _ in lhs] + [pl.BlockSpec((tk, N), lambda i, k: (k, 0))],
        out_specs=[pl.BlockSpec((tm, N), lambda i, k: (i, 0))] * n_o,
        out_shape=[jax.ShapeDtypeStruct((M, N), BF16)] * n_o,
        scratch_shapes=[pltpu.VMEM((tm, N), F32)] * n_o,
        compiler_params=_cparams(("parallel", "arbitrary")),
    )


def proj_bwd(dproj, w, x, g, dres, name, dep=None):
    S, N = dproj.shape

    def body(dp_ref, w_ref, x_ref, g_ref, dr_ref, dx_ref, dg_ref):
        i = pl.program_id(0)

        @pl.when(i == 0)
        def _():
            dg_ref[...] = jnp.zeros_like(dg_ref)

        dh = _dot(dp_ref[...], w_ref[...])
        xf = x_ref[...]
        r = _rms(xf)
        xh = xf * r
        dg_ref[...] += jnp.sum(dh * xh, axis=0, keepdims=True)
        dx_ref[...] = dr_ref[...] + _rms_bwd(dh, xh, r, g_ref[...])

    return _pallas_after(
        dep, body, dproj, w, x, g, dres, name=name, grid=(S // TM,),
        in_specs=[pl.BlockSpec((TM, N), lambda i: (i, 0)), pl.BlockSpec((N, D_MODEL), lambda i: (0, 0)),
                  pl.BlockSpec((TM, D_MODEL), lambda i: (i, 0)), pl.BlockSpec((1, D_MODEL), lambda i: (0, 0)),
                  pl.BlockSpec((TM, D_MODEL), lambda i: (i, 0))],
        out_specs=[pl.BlockSpec((TM, D_MODEL), lambda i: (i, 0)), pl.BlockSpec((1, D_MODEL), lambda i: (0, 0))],
        out_shape=[jax.ShapeDtypeStruct((S, D_MODEL), F32), jax.ShapeDtypeStruct((1, D_MODEL), F32)],
        compiler_params=_cparams(("arbitrary",)),
    )


def mixer_bwd(proj, bias, y, ao, dx1, w_o, sinks, conv_w, qg, kg, cog, aog, name, dep=None):
    S = proj.shape[0]
    SUB, ROWS = SUB_BWD, SUB_BWD * BLK
    steps = S // ROWS
    KV_W = 2 * PAIR

    def body(sinks_ref, p_ref, kvp_ref, bias_ref, y_ref, ao_ref, dx_first_ref, dx_next_ref, wo_ref, cw_ref, qg_ref,
             kg_ref, cog_ref, aog_ref, dp_ref, dcw_ref, dqg_ref, dkg_ref, dsk_ref, dcog_ref, daog_ref,
             dycar, kcar, vcar, dmix_scr):
        step = pl.program_id(0)

        @pl.when(step == 0)
        def _():
            dmix_scr[...] = _dot_nt(dx_first_ref[...].astype(BF16), wo_ref[...])
            dycar[...] = jnp.zeros_like(dycar)
            kcar[...] = jnp.zeros_like(kcar)
            vcar[...] = jnp.zeros_like(vcar)
            dcw_ref[...] = jnp.zeros_like(dcw_ref)
            dqg_ref[...] = jnp.zeros_like(dqg_ref)
            dkg_ref[...] = jnp.zeros_like(dkg_ref)
            dsk_ref[...] = jnp.zeros_like(dsk_ref)
            dcog_ref[...] = jnp.zeros_like(dcog_ref)
            daog_ref[...] = jnp.zeros_like(daog_ref)

        dma = dmix_scr[:, CONV_CH:]

        aov = ao_ref[...].astype(F32)
        ra = _rms(aov)
        ah = aov * ra
        daog_ref[...] += jnp.sum(dma * ah, axis=0, keepdims=True)
        dao = _rms_bwd(dma, ah, ra, aog_ref[...])

        low = _low_half()
        high = jnp.logical_not(low)
        q_gain = qg_ref[...] * SCALE
        k, v = _kv_pairs(p_ref, kvp_ref)
        rk = _pair_rms(k, low)
        kh = k * rk
        k_of = _one_head_in_both_halves((kh * kg_ref[...]).astype(BF16), low)
        v_of = _one_head_in_both_halves(v, low)
        lane8 = lax.broadcasted_iota(jnp.int32, (1, N_Q), 1)
        dsk = jnp.zeros((1, N_Q), F32)
        dqg = jnp.zeros((1, PAIR), F32)
        biases = [bias_ref[jnp.minimum(steps - 1 - step, 1)]] + [bias_ref[1]] * (SUB - 1)
        units = [(s, hq) for s in range(SUB) for hq in range(N_Q)]
        rq, qh, q_one, do_one, delta = [], [], [], [], []
        for pair in range(N_Q // 2):
            cols = slice(PAIR * pair, PAIR * (pair + 1))
            q = p_ref[:, O_Q + PAIR * pair:O_Q + PAIR * (pair + 1)].astype(F32)
            rq.append(_pair_rms(q, low))
            qh.append(q * rq[pair])
            qn = (qh[pair] * q_gain).astype(BF16)
            do = dao[:, cols]
            do_b = do.astype(BF16)
            delta += _half_sums(do * aov[:, cols], low)
            for mine in (low, high):
                q_one.append(jnp.where(mine, qn, jnp.zeros_like(qn)))
                do_one.append(jnp.where(mine, do_b, jnp.zeros_like(do_b)))
        scores = {u: _dot_nt(q_one[u[1]][_block(u[0])], k_of[u[1] // GRP][_band(u[0])]) + biases[u[0]] for u in units}
        dprobs = {u: _dot_nt(do_one[u[1]][_block(u[0])], v_of[u[1] // GRP][_band(u[0])]) for u in units}
        probs = {u: _softmax_with_sink(scores[u], sinks_ref[u[1]]) for u in units}
        ds = {u: (probs[u][0] * (dprobs[u] - delta[u[1]][_block(u[0])])).astype(BF16) for u in units}
        dv_t = {(s, h): jnp.zeros((PAIR, 2 * BLK), F32) for s in range(SUB) for h in range(N_KV)}
        dkn_t = dict(dv_t)
        for s, hq in units:
            dsk = dsk - jnp.where(lane8 == hq, jnp.sum(probs[s, hq][1] * delta[hq][_block(s)]), 0.0)
            dv_t[s, hq // GRP] = dv_t[s, hq // GRP] + _dot_tn(do_one[hq][_block(s)], probs[s, hq][0].astype(BF16))
            dkn_t[s, hq // GRP] = dkn_t[s, hq // GRP] + _dot_tn(q_one[hq][_block(s)], ds[s, hq])
        dqn_of = {u: _dot(ds[u], k_of[u[1] // GRP][_band(u[0])]) for u in units}
        for pair in range(N_Q // 2):
            dqn = jnp.concatenate([jnp.where(low, dqn_of[s, 2 * pair], dqn_of[s, 2 * pair + 1]) for s in range(SUB)],
                                  axis=0)
            dqg = dqg + jnp.sum(dqn * qh[pair], axis=0, keepdims=True)
            dqh = dqn * q_gain
            dp_ref[:, O_Q + PAIR * pair:O_Q + PAIR * (pair + 1)] = (
                rq[pair] * (dqh - qh[pair] * _pair_mean(dqh * qh[pair], low))).astype(BF16)

        def over_key_rows(parts):
            bands = [jnp.concatenate([parts[s, h][:HEAD] + parts[s, h][HEAD:] for h in range(N_KV)], axis=0).T
                     for s in range(SUB)]
            pieces = [bands[0][:BLK]]
            pieces += [bands[s - 1][BLK:] + bands[s][:BLK] for s in range(1, SUB)]
            pieces.append(bands[SUB - 1][BLK:])
            return jnp.concatenate(pieces, axis=0)

        dv = over_key_rows(dv_t)
        dkn = over_key_rows(dkn_t)
        dkg_ref[...] += jnp.sum(dkn * kh, axis=0, keepdims=True)
        dkh = dkn * kg_ref[...]
        dk = rk * (dkh - kh * _pair_mean(dkh * kh, low))
        last = slice(ROWS, ROWS + BLK)
        dp_ref[:, O_K:O_V] = jnp.concatenate([dk[BLK:ROWS], dk[last] + kcar[...]], axis=0).astype(BF16)
        dp_ref[:, O_V:] = jnp.concatenate([dv[BLK:ROWS], dv[last] + vcar[...]], axis=0).astype(BF16)
        kcar[...] = dk[:BLK, :]
        vcar[...] = dv[:BLK, :]
        dsk_ref[...] += dsk
        dqg_ref[...] += dqg * SCALE

        bg = p_ref[:, 0:CONV_CH].astype(F32)
        cg = p_ref[:, CONV_CH:2 * CONV_CH].astype(F32)
        hc = p_ref[:, 2 * CONV_CH:3 * CONV_CH].astype(F32)
        yv = y_ref[...].astype(F32)
        dmc = dmix_scr[:, 0:CONV_CH]
        co = bg * yv
        rc = _rms(co)
        ch = co * rc
        dcog_ref[...] += jnp.sum(dmc * ch, axis=0, keepdims=True)
        dco = _rms_bwd(dmc, ch, rc, cog_ref[...])
        dp_ref[:, 0:CONV_CH] = (dco * yv).astype(BF16)
        dy = dco * bg
        row = lax.broadcasted_iota(jnp.int32, (ROWS, CONV_CH), 0)
        nxt = dycar[...]
        dy1 = jnp.where(row == ROWS - 1, nxt[0:1, :], pltpu.roll(dy, ROWS - 1, 0))
        dy2 = jnp.where(row == ROWS - 2, nxt[0:1, :],
                        jnp.where(row == ROWS - 1, nxt[1:2, :], pltpu.roll(dy, ROWS - 2, 0)))
        dycar[...] = dy[0:8, :]
        du = cw_ref[2:3, :] * dy + cw_ref[1:2, :] * dy1 + cw_ref[0:1, :] * dy2
        dp_ref[:, CONV_CH:2 * CONV_CH] = (du * hc).astype(BF16)
        dp_ref[:, 2 * CONV_CH:3 * CONV_CH] = (du * cg).astype(BF16)
        u = cg * hc
        dcw_ref[0:1, :] += jnp.sum(dy2 * u, axis=0, keepdims=True)
        dcw_ref[1:2, :] += jnp.sum(dy1 * u, axis=0, keepdims=True)
        dcw_ref[2:3, :] += jnp.sum(dy * u, axis=0, keepdims=True)

        dmix_scr[...] = _dot_nt(dx_next_ref[...].astype(BF16), wo_ref[...])

    small = lambda shape: pl.BlockSpec(shape, lambda s: (0, 0))
    blk = lambda w: pl.BlockSpec((ROWS, w), lambda s: (steps - 1 - s, 0))
    return _pallas_after(
        dep, body, sinks, proj, proj, bias, y, ao, dx1, dx1, w_o, conv_w, qg, kg, cog, aog, name=name, grid=(steps,),
        in_specs=[pl.BlockSpec(memory_space=pltpu.SMEM),
                  blk(IN_COLS),
                  pl.BlockSpec((BLK, KV_W), lambda s: (jnp.maximum(SUB * (steps - 1 - s) - 1, 0), O_K // KV_W)),
                  pl.BlockSpec((2, BLK, 2 * BLK), lambda s: (0, 0, 0)),
                  blk(CONV_CH), blk(ATTN_W),
                  pl.BlockSpec((ROWS, D_MODEL), lambda s: (steps - 1, 0)),
                  pl.BlockSpec((ROWS, D_MODEL), lambda s: (jnp.maximum(steps - 2 - s, 0), 0)),
                  small((D_MODEL, D_MODEL)),
                  small((3, CONV_CH)), small((1, PAIR)), small((1, PAIR)), small((1, CONV_CH)), small((1, ATTN_W))],
        out_specs=[blk(IN_COLS), small((3, CONV_CH)), small((1, PAIR)), small((1, PAIR)), small((1, N_Q)),
                   small((1, CONV_CH)), small((1, ATTN_W))],
        out_shape=[jax.ShapeDtypeStruct((S, IN_COLS), BF16), jax.ShapeDtypeStruct((3, CONV_CH), F32),
                   jax.ShapeDtypeStruct((1, PAIR), F32), jax.ShapeDtypeStruct((1, PAIR), F32),
                   jax.ShapeDtypeStruct((1, N_Q), F32), jax.ShapeDtypeStruct((1, CONV_CH), F32),
                   jax.ShapeDtypeStruct((1, ATTN_W), F32)],
        scratch_shapes=[pltpu.VMEM((8, CONV_CH), F32), pltpu.VMEM((BLK, PAIR), F32), pltpu.VMEM((BLK, PAIR), F32),
                        pltpu.VMEM((ROWS, D_MODEL), F32)],
        compiler_params=_cparams(("arbitrary",)),
    )


OTHER_CHIPS = ((1, 0), (0, 1), (1, 1))


def _position():
    return lax.axis_index("x"), lax.axis_index("y"), lax.axis_index("c")


def all_gather(shards, name):
    n = len(shards)

    def body(*refs):
        x_refs, out_refs = refs[:n], refs[n:2 * n]
        send_sems, recv_sems, local_sems = refs[2 * n:]
        x, y, c = _position()
        me, sibling = (x, y, c), (x, y, 1 - c)
        chips = [(x ^ mx, y ^ my) for mx, my in OTHER_CHIPS]

        def slab(a, px, py, pc):
            return out_refs[a].at[4 * px + 2 * py + pc]

        def copy(a, k, block, to, src=None):
            return pltpu.make_async_remote_copy(
                src_ref=slab(a, *block) if src is None else src, dst_ref=slab(a, *block),
                send_sem=send_sems.at[7 * a + k], recv_sem=recv_sems.at[7 * a + k], device_id=to, device_id_type=MESH_T)

        mine = [pltpu.make_async_copy(x_refs[a], slab(a, *me), local_sems.at[a]) for a in range(n)]
        for cp in mine:
            cp.start()
        first = []
        for a in range(n):
            first.append(copy(a, 0, me, sibling, src=x_refs[a]))
            first += [copy(a, 1 + j, me, (*chip, c), src=x_refs[a]) for j, chip in enumerate(chips)]
        for cp in first:
            cp.start()
        passed = []
        for a in range(n):
            for j, chip in enumerate(chips):
                copy(a, 1 + j, (*chip, c), me).wait_recv()
                passed.append(copy(a, 4 + j, (*chip, c), sibling))
                passed[-1].start()
        for a in range(n):
            copy(a, 0, sibling, me).wait_recv()
            for j, chip in enumerate(chips):
                copy(a, 4 + j, (*chip, 1 - c), me).wait_recv()
        for cp in first + passed:
            cp.wait_send()
        for cp in mine:
            cp.wait()

    return pl.pallas_call(
        body, name=name, in_specs=[ANY] * n, out_specs=[ANY] * n,
        out_shape=[jax.ShapeDtypeStruct((N_DEV, *s.shape), s.dtype) for s in shards],
        scratch_shapes=[pltpu.SemaphoreType.DMA((7 * n,)), pltpu.SemaphoreType.DMA((7 * n,)),
                        pltpu.SemaphoreType.DMA((n,))],
    )(*shards)


class SplitCopy:
    def __init__(self, name, arrays, n_copies, plan, after=None):
        n = len(arrays)
        self.name, self.n, self.n_copies, self.plan = name, n, n_copies, plan
        extra = [] if after is None else [after]

        def body(*refs):
            in_refs = refs[:n]
            send_sems, recv_sems = refs[n + len(extra)], refs[n + len(extra) + 1]
            token = refs[2 * n + len(extra) + 2]
            for k, (src, dst, to) in enumerate(plan(_position(), in_refs)):
                pltpu.make_async_remote_copy(src_ref=src, dst_ref=dst, send_sem=send_sems.at[k],
                                             recv_sem=recv_sems.at[k], device_id=to, device_id_type=MESH_T).start()
            token[...] = jnp.zeros_like(token)

        outs = pl.pallas_call(
            body, name=name + "_start",
            out_shape=(pltpu.SemaphoreType.DMA((n_copies,)), pltpu.SemaphoreType.DMA((n_copies,)),
                       *[pltpu.HBM(a.shape, a.dtype) for a in arrays], jax.ShapeDtypeStruct((8, 128), F32)),
            in_specs=[HBM] * n + [ANY] * len(extra),
            out_specs=(SEM, SEM, *[HBM] * n, pl.BlockSpec(memory_space=pltpu.VMEM)),
            input_output_aliases={i: 2 + i for i in range(n)},
            compiler_params=pltpu.CompilerParams(has_side_effects=pltpu.SideEffectType.DATAFLOW_SIDE_EFFECTING),
        )(*[pltpu.with_memory_space_constraint(a, pltpu.HBM) for a in arrays], *extra)
        self.send_sems, self.recv_sems = outs[0], outs[1]
        self.arrays, self.token = list(outs[2:2 + n]), outs[2 + n]

    def wait(self, after):
        n, plan = self.n, self.plan

        def body(*refs):
            in_refs, send_sems, recv_sems = refs[:n], refs[n], refs[n + 1]
            for k, (src, dst, to) in enumerate(plan(_position(), in_refs)):
                cp = pltpu.make_async_remote_copy(src_ref=src, dst_ref=dst, send_sem=send_sems.at[k],
                                                  recv_sem=recv_sems.at[k], device_id=to, device_id_type=MESH_T)
                cp.wait_send()
                cp.wait_recv()

        outs = pl.pallas_call(
            body, name=self.name + "_wait",
            out_shape=tuple(pltpu.HBM(a.shape, a.dtype) for a in self.arrays),
            in_specs=[HBM] * n + [SEM, SEM, ANY], out_specs=tuple([HBM] * n),
            input_output_aliases={i: i for i in range(n)},
            compiler_params=pltpu.CompilerParams(has_side_effects=pltpu.SideEffectType.DATAFLOW_SIDE_EFFECTING),
        )(*self.arrays, self.send_sems, self.recv_sems, after)
        return list(outs)


def gather_start(shards, me, name, after=None):
    n = len(shards)
    lands = [lax.dynamic_update_slice(lax.empty((N_DEV, *s.shape), s.dtype), s[None], (me, 0, 0)) for s in shards]

    def plan(pos, refs):
        x, y, c = pos
        return [(refs[a], refs[n + a].at[4 * x + 2 * y + c], (x ^ mx, y ^ my, c))
                for a in range(n) for mx, my in OTHER_CHIPS]

    return SplitCopy(name, list(shards) + lands, 3 * n, plan, after=after)


def sibling_start(lands, name):
    n = len(lands)

    def plan(pos, refs):
        x, y, c = pos
        return [(refs[a].at[2 * q + c], refs[a].at[2 * q + c], (x, y, 1 - c)) for a in range(n) for q in range(4)]

    return SplitCopy(name, list(lands), 4 * n, plan)


def scatter_start(slabs, name):
    n = len(slabs)
    lands = [lax.empty((N_DEV - 1, *g.shape[1:]), g.dtype) for g in slabs]

    def plan(pos, refs):
        x, y, c = pos
        copies = []
        for a in range(n):
            for r in range(1, N_DEV):
                px, py, pc = x ^ ((r >> 2) & 1), y ^ ((r >> 1) & 1), c ^ (r & 1)
                copies.append((refs[a].at[4 * px + 2 * py + pc], refs[n + a].at[r - 1], (px, py, pc)))
        return copies

    return SplitCopy(name, list(slabs) + lands, (N_DEV - 1) * n, plan)


def all_reduce_small(v, name, dep=None):
    R, W = v.shape

    def body(v_ref, o_ref, recv, send_sems, recv_sems):
        x, y, c = _position()
        me = 4 * x + 2 * y + c
        copies = []
        for r in range(1, N_DEV):
            to = (x ^ ((r >> 2) & 1), y ^ ((r >> 1) & 1), c ^ (r & 1))
            copies.append(pltpu.make_async_remote_copy(
                src_ref=v_ref, dst_ref=recv.at[me], send_sem=send_sems.at[r - 1], recv_sem=recv_sems.at[r - 1],
                device_id=to, device_id_type=MESH_T))
        for cp in copies:
            cp.start()
        recv[pl.ds(me, 1)] = v_ref[...][None]
        for cp in copies:
            cp.wait()
        acc = recv[0]
        for s in range(1, N_DEV):
            acc = acc + recv[s]
        o_ref[...] = acc

    return _pallas_after(
        dep, body, v, name=name,
        in_specs=[pl.BlockSpec(memory_space=pltpu.VMEM)], out_specs=pl.BlockSpec(memory_space=pltpu.VMEM),
        out_shape=jax.ShapeDtypeStruct((R, W), F32),
        scratch_shapes=[pltpu.VMEM((N_DEV, R, W), F32), pltpu.SemaphoreType.DMA((N_DEV - 1,)),
                        pltpu.SemaphoreType.DMA((N_DEV - 1,))],
    )


def _row_tile(rows):
    if rows <= 512:
        return rows
    return max(t for t in range(8, 513, 8) if rows % t == 0)


def _adamw_update(w, g, m, v):
    mn = ADAM_B1 * m + (1.0 - ADAM_B1) * g
    vn = ADAM_B2 * v + (1.0 - ADAM_B2) * (g * g)
    m_hat = mn / (1.0 - ADAM_B1 ** ADAM_STEP)
    v_hat = vn / (1.0 - ADAM_B2 ** ADAM_STEP)
    return -ADAM_LR * (m_hat / (jnp.sqrt(v_hat) + ADAM_EPS) + ADAM_WD * w), mn, vn


def adamw(w, g, m, v, name):
    R, W = w.shape
    tr = _row_tile(R)

    def body(w_ref, g_ref, m_ref, v_ref, d_ref, mo_ref, vo_ref):
        d_ref[...], mo_ref[...], vo_ref[...] = _adamw_update(w_ref[...], g_ref[...], m_ref[...], v_ref[...])

    spec = pl.BlockSpec((tr, W), lambda i: (i, 0))
    return pl.pallas_call(
        body, name=name, grid=(R // tr,), in_specs=[spec] * 4, out_specs=[spec] * 3,
        out_shape=[jax.ShapeDtypeStruct((R, W), F32)] * 3,
        compiler_params=_cparams(("parallel",)),
    )(w, g, m, v)


def reduce_adamw(slabs, land, w, m, v, layer, me_arr, name, others=None):
    _, R, W = slabs.shape
    tr = _row_tile(R)
    n_other = 0 if others is None else 4

    def body(me_ref, s_ref, l_ref, w_ref, m_ref, v_ref, *rest):
        g_ref, d_ref, mo_ref, vo_ref = rest[n_other:]
        g = s_ref[0].astype(F32)
        for r in range(N_DEV - 1):
            g = g + l_ref[r].astype(F32)
        g_ref[0] = g
        d_ref[0], mo_ref[0], vo_ref[0] = _adamw_update(w_ref[0], g, m_ref[0], v_ref[0])

    spec = pl.BlockSpec((1, tr, W), lambda i, me: (layer, i, 0))
    return pl.pallas_call(
        body, name=name,
        grid_spec=pltpu.PrefetchScalarGridSpec(
            num_scalar_prefetch=1, grid=(R // tr,),
            in_specs=[pl.BlockSpec((1, tr, W), lambda i, me: (me[0], i, 0)),
                      pl.BlockSpec((N_DEV - 1, tr, W), lambda i, me: (0, i, 0)), spec, spec, spec] + [ANY] * n_other,
            out_specs=[spec] * 4),
        out_shape=[jax.ShapeDtypeStruct((DEPTH, R, W), F32)] * 4,
        input_output_aliases={6 + i: i for i in range(n_other)},
        compiler_params=_cparams(("parallel",)),
    )(me_arr, slabs, land, w, m, v, *([] if others is None else others))


SMALL_NAMES = ("norm1_g", "q_norm_g", "k_norm_g", "sinks", "conv_out_g", "attn_out_g", "norm2_g", "conv_w")
SMALL_SIZES = (D_MODEL, HEAD, HEAD, N_Q, CONV_CH, ATTN_W, D_MODEL, 3 * CONV_CH)
SMALL_ROWS = 80


def kernel(x, norm1_g, w_in, conv_w, q_norm_g, k_norm_g, sinks, conv_out_g, attn_out_g, w_o, norm2_g, w_gate, w_up, w_down, loss_target, m_norm1_g, m_w_in, m_conv_w, m_q_norm_g, m_k_norm_g, m_sinks, m_conv_out_g, m_attn_out_g, m_w_o, m_norm2_g, m_w_gate, m_w_up, m_w_down, v_norm1_g, v_w_in, v_conv_w, v_q_norm_g, v_k_norm_g, v_sinks, v_conv_out_g, v_attn_out_g, v_w_o, v_norm2_g, v_w_gate, v_w_up, v_w_down):
    xi, yi, ci = _position()
    me = 4 * xi + 2 * yi + ci
    me_arr = jnp.reshape(me, (1,)).astype(jnp.int32)
    xs, tgt = x[0], loss_target[0]
    bf = lambda a: a.astype(BF16)
    bias = band_bias()

    t = lambda a: jnp.swapaxes(a, 1, 2)
    shard = dict(w_in=(t(w_in), t(m_w_in), t(v_w_in)), w_o=(w_o, m_w_o, v_w_o),
                 w_gate=(t(w_gate), t(m_w_gate), t(v_w_gate)), w_up=(t(w_up), t(m_w_up), t(v_w_up)),
                 w_down=(w_down, m_w_down, v_w_down))
    wb = {n: bf(shard[n][0]) for n in shard}

    g_in0, g_o0, g_conv = all_gather([wb["w_in"][0], wb["w_o"][0], conv_w.reshape(DEPTH * 3, HEAD)], "gather_first")
    ag_ffn0 = gather_start([wb["w_gate"][0], wb["w_up"][0], wb["w_down"][0]], me, "gather_ffn0", after=g_in0)
    ag_mix1 = gather_start([wb["w_in"][1], wb["w_o"][1]], me, "gather_mix1", after=ag_ffn0.token)
    ag_ffn1 = gather_start([wb["w_gate"][1], wb["w_up"][1], wb["w_down"][1]], me, "gather_ffn1", after=ag_mix1.token)
    conv_full = g_conv.reshape(N_DEV, DEPTH, 3, HEAD).transpose(1, 2, 0, 3).reshape(DEPTH, 3, CONV_CH)
    pair_gain = lambda g: jnp.tile(g[None], (1, 2))
    small = [dict(norm1_g=norm1_g[l][None], conv_w=conv_full[l], q_norm_g=pair_gain(q_norm_g[l]),
                  k_norm_g=pair_gain(k_norm_g[l]), sinks=sinks[l], conv_out_g=conv_out_g[l][None],
                  attn_out_g=attn_out_g[l][None], norm2_g=norm2_g[l][None]) for l in range(DEPTH)]
    whole = lambda g: g.reshape(-1, D_MODEL)
    weights = [dict(w_in=whole(g_in0), w_o=whole(g_o0)), {}]

    def ffn_weights(g_gate, g_up, g_down):
        return dict(w_gate=whole(g_gate), w_up=whole(g_up), w_down=whole(g_down))

    saved = []
    xl = xs
    for l in range(DEPTH):
        sp, wl = small[l], weights[l]
        h, proj = norm_proj(xl, sp["norm1_g"], wl["w_in"], f"norm_proj{l}",
                            dep=ag_ffn1.token if l == 0 else pass_ffn1.token)
        mixer_args = (proj, bias, sp["sinks"], sp["conv_w"], sp["q_norm_g"], sp["k_norm_g"], sp["conv_out_g"],
                      sp["attn_out_g"], f"mixer_fwd{l}")
        if l == 0:
            mix, y, ao = mixer_fwd(*mixer_args)
            pass_ffn0 = sibling_start(ag_ffn0.wait(mix)[3:], "pass_ffn0")
            x1 = matmul_residual(mix, wl["w_o"], xl, "out_proj0", dep=pass_ffn0.token)
            wl.update(ffn_weights(*pass_ffn0.wait(x1)))
            pass_mix1 = sibling_start(ag_mix1.wait(x1)[2:], "pass_mix1")
            h2, a, b, f, x2 = ffn_fwd(x1, sp["norm2_g"], wl["w_gate"], wl["w_up"], wl["w_down"], "ffn_fwd0",
                                      dep=pass_mix1.token)
            g_in, g_o = pass_mix1.wait(x2)
            weights[1] = dict(w_in=whole(g_in), w_o=whole(g_o))
            pass_ffn1 = sibling_start(ag_ffn1.wait(x2)[3:], "pass_ffn1")
        else:
            mix, y, ao, x1 = mixer_fwd(*mixer_args, residual=xl, w_o=wl["w_o"])
            wl.update(ffn_weights(*pass_ffn1.wait(x1)))
            h2, a, b, f, dx, loss_row = ffn_fwd(x1, sp["norm2_g"], wl["w_gate"], wl["w_up"], wl["w_down"], "ffn_fwd1",
                                                tgt=tgt)
            x2 = None
        saved.append((xl, h, proj, mix, y, ao, x1, h2, a, b, f))
        xl = x2

    stepped = {n: None for n in shard}
    slabs = lambda d: d.reshape(N_DEV, -1, D_MODEL)
    gsmall = [None] * DEPTH

    def finish(sc, names, after, l):
        arrays = sc.wait(after)
        k = len(names)
        for i, n in enumerate(names):
            w, m, v = shard[n]
            stepped[n] = reduce_adamw(arrays[i], arrays[k + i], w, m, v, l, me_arr, f"reduce_adamw_{n}{l}",
                                      others=stepped[n])

    for l in reversed(range(DEPTH)):
        sp, wl = small[l], weights[l]
        x0, h, proj, mix, y, ao, x1, h2, a, b, f = saved[l]
        (d_wd,) = grad_weight((f,), dx, TF, f"grad_w_down{l}", tk=2048)
        sc_down = scatter_start([slabs(d_wd)], f"scatter_w_down{l}")
        da, db, dx1, d_g2 = ffn_bwd(dx, a, b, x1, sp["norm2_g"], wl["w_gate"], wl["w_up"], wl["w_down"],
                                    f"ffn_bwd{l}", dep=sc_down.token)
        d_wg, d_wu = grad_weight((da, db), h2, TF, f"grad_w_gate_up{l}")
        (d_wo,) = grad_weight((mix,), dx1, D_MODEL, f"grad_w_o{l}", tk=2048)
        sc_rest = scatter_start([slabs(d_wg), slabs(d_wu), slabs(d_wo)], f"scatter_w_gate_up_o{l}")
        dproj, d_cw, d_qg, d_kg, d_sk, d_cog, d_aog = mixer_bwd(
            proj, bias, y, ao, dx1, wl["w_o"], sp["sinks"], sp["conv_w"], sp["q_norm_g"], sp["k_norm_g"],
            sp["conv_out_g"], sp["attn_out_g"], f"mixer_bwd{l}", dep=sc_rest.token)
        finish(sc_down, ["w_down"], dproj, l)
        finish(sc_rest, ["w_gate", "w_up", "w_o"], dproj, l)
        (d_win,) = grad_weight((dproj,), h, IN_COLS // 2, f"grad_w_in{l}", tk=2048)
        sc_in = scatter_start([slabs(d_win)], f"scatter_w_in{l}")
        dx, d_g1 = proj_bwd(dproj, wl["w_in"], x0, sp["norm1_g"], dx1, f"proj_bwd{l}", dep=sc_in.token)
        finish(sc_in, ["w_in"], dx, l)
        both_heads = lambda d: d[:, :HEAD] + d[:, HEAD:]
        gsmall[l] = dict(norm1_g=d_g1, conv_w=d_cw, q_norm_g=both_heads(d_qg), k_norm_g=both_heads(d_kg), sinks=d_sk,
                         conv_out_g=d_cog, attn_out_g=d_aog, norm2_g=d_g2)
    grad_x = dx[None]

    flat = jnp.concatenate([loss_row[0, 0:1]] + [gsmall[l][n].reshape(-1) for l in range(DEPTH) for n in SMALL_NAMES])
    flat = jnp.pad(flat, (0, SMALL_ROWS * 128 - flat.shape[0])).reshape(SMALL_ROWS, 128)
    flat = all_reduce_small(flat, "all_reduce_small_grads").reshape(-1)
    loss = flat[0]
    gs = {n: [] for n in SMALL_NAMES}
    off = 1
    for l in range(DEPTH):
        for n, size in zip(SMALL_NAMES, SMALL_SIZES):
            gs[n].append(flat[off:off + size])
            off += size
    gs = {n: jnp.stack(v) for n, v in gs.items()}
    g_conv = lax.dynamic_slice(gs["conv_w"].reshape(DEPTH, 3, CONV_CH), (0, 0, me * HEAD), (DEPTH, 3, HEAD))

    gs["conv_w"] = g_conv
    params = dict(norm1_g=(norm1_g, m_norm1_g, v_norm1_g), conv_w=(conv_w, m_conv_w, v_conv_w),
                  q_norm_g=(q_norm_g, m_q_norm_g, v_q_norm_g), k_norm_g=(k_norm_g, m_k_norm_g, v_k_norm_g),
                  sinks=(sinks, m_sinks, v_sinks), conv_out_g=(conv_out_g, m_conv_out_g, v_conv_out_g),
                  attn_out_g=(attn_out_g, m_attn_out_g, v_attn_out_g), norm2_g=(norm2_g, m_norm2_g, v_norm2_g))
    names = ("norm1_g", "w_in", "conv_w", "q_norm_g", "k_norm_g", "sinks", "conv_out_g", "attn_out_g", "w_o",
             "norm2_g", "w_gate", "w_up", "w_down")

    out = {}
    for n in names:
        if n in shard:
            out[n] = tuple(t(r) for r in stepped[n]) if n in ("w_in", "w_gate", "w_up") else stepped[n]
        else:
            w, m, v = params[n]
            two_d = (-1, w.shape[-1])
            d, mn, vn = adamw(w.reshape(two_d), gs[n].reshape(two_d), m.reshape(two_d), v.reshape(two_d), f"adamw_{n}")
            out[n] = (gs[n].reshape(w.shape), d.reshape(w.shape), mn.reshape(w.shape), vn.reshape(w.shape))
    return (loss, grad_x, *[out[n][i] for i in range(4) for n in names])
```

```python
import jax
import jax.numpy as jnp
from jax import lax
from jax.experimental import pallas as pl
from jax.experimental.pallas import tpu as pltpu

F32 = jnp.float32
BF16 = jnp.bfloat16

D_MODEL = 1024
CONV_CH = 512
ATTN_W = 512
N_Q = 8
N_KV = 2
GRP = N_Q // N_KV
HEAD = 64
IN_COLS = 2304
D_FF = 2816
BLK = 128
O_Q = 3 * CONV_CH
O_K = O_Q + ATTN_W
O_V = O_K + N_KV * HEAD
EPS = 1e-6
NEG_INF = -1e30
SCALE = HEAD ** -0.5
N_DEV = 8
DEPTH = 2

ADAM_LR = 0.001
ADAM_B1 = 0.9
ADAM_B2 = 0.999
ADAM_EPS = 1e-08
ADAM_WD = 0.01
ADAM_STEP = 10

VMEM_LIMIT = 56 * 1024 * 1024
TM = 512
MESH_T = pl.DeviceIdType.MESH

ANY = pl.BlockSpec(memory_space=pl.ANY)
HBM = pl.BlockSpec(memory_space=pltpu.HBM)
SEM = pl.BlockSpec(memory_space=pltpu.SEMAPHORE)


def _cparams(sem):
    return pltpu.CompilerParams(dimension_semantics=sem, vmem_limit_bytes=VMEM_LIMIT)


def _pallas_after(dep, body, *args, in_specs, **kw):
    if dep is None:
        return pl.pallas_call(body, in_specs=in_specs, **kw)(*args)

    def after_dep(dep_ref, *refs):
        body(*refs)

    return pl.pallas_call(after_dep, in_specs=[ANY, *in_specs], **kw)(dep, *args)


def _dot(a, b):
    return jnp.dot(a, b, preferred_element_type=F32)


def _dot_nt(a, b):
    return lax.dot_general(a, b, (((1,), (1,)), ((), ())), preferred_element_type=F32)


def _dot_tn(a, b):
    return lax.dot_general(a, b, (((0,), (0,)), ((), ())), preferred_element_type=F32)


def _row_sum(v):
    return jnp.sum(v, axis=-1, keepdims=True)


def _row_mean(v):
    return _row_sum(v) * (1.0 / v.shape[-1])


def _rms(v):
    return lax.rsqrt(_row_mean(v * v) + EPS)


def _sigmoid(v):
    return 1.0 / (1.0 + jnp.exp(-v))


def _rms_bwd(dyv, xh, r, g):
    dxh = dyv * g
    return r * (dxh - xh * _row_mean(dxh * xh))


def norm_proj(x, g, w_t, name, dep=None):
    S, N = x.shape[0], w_t.shape[0]

    def body(x_ref, g_ref, w_ref, h_ref, p_ref):
        xf = x_ref[...]
        h = ((xf * _rms(xf)) * g_ref[...]).astype(BF16)
        h_ref[...] = h
        p_ref[...] = _dot_nt(h, w_ref[...]).astype(BF16)

    return _pallas_after(
        dep, body, x, g, w_t, name=name, grid=(S // TM,),
        in_specs=[pl.BlockSpec((TM, D_MODEL), lambda i: (i, 0)),
                  pl.BlockSpec((1, D_MODEL), lambda i: (0, 0)),
                  pl.BlockSpec((N, D_MODEL), lambda i: (0, 0))],
        out_specs=[pl.BlockSpec((TM, D_MODEL), lambda i: (i, 0)),
                   pl.BlockSpec((TM, N), lambda i: (i, 0))],
        out_shape=[jax.ShapeDtypeStruct((S, D_MODEL), BF16), jax.ShapeDtypeStruct((S, N), BF16)],
        compiler_params=_cparams(("parallel",)),
    )


def band_bias():
    qi = lax.broadcasted_iota(jnp.int32, (BLK, 2 * BLK), 0)
    kj = lax.broadcasted_iota(jnp.int32, (BLK, 2 * BLK), 1)
    diff = qi + BLK - kj
    valid = (diff >= 0) & (diff < BLK)
    return jnp.stack([jnp.where(valid & (kj >= BLK), 0.0, NEG_INF), jnp.where(valid, 0.0, NEG_INF)]).astype(F32)


def _softmax_with_sink(s, sink):
    m = jnp.maximum(jnp.max(s, axis=-1, keepdims=True), sink)
    p = jnp.exp(s - m)
    es = jnp.exp(sink - m)
    inv = 1.0 / (_row_sum(p) + es)
    return p * inv, es * inv


PAIR = 2 * HEAD


def _low_half():
    return lax.broadcasted_iota(jnp.int32, (1, PAIR), 1) < HEAD


def _half_sums(v, low):
    return (jnp.sum(jnp.where(low, v, 0.0), axis=-1, keepdims=True),
            jnp.sum(jnp.where(low, 0.0, v), axis=-1, keepdims=True))


def _pair_mean(v, low):
    e, o = _half_sums(v, low)
    return jnp.where(low, e, o) * (1.0 / HEAD)


def _pair_rms(v, low):
    return lax.rsqrt(_pair_mean(v * v, low) + EPS)


def _one_head_in_both_halves(pair, low):
    swapped = pltpu.roll(pair, HEAD, 1)
    return jnp.where(low, pair, swapped), jnp.where(low, swapped, pair)


SUB_FWD, SUB_FWD_PROJECT, SUB_BWD = 2, 4, 4


def _kv_pairs(p_ref, kvp_ref):
    k = jnp.concatenate([kvp_ref[:, 0:PAIR], p_ref[:, O_K:O_K + PAIR]], axis=0).astype(F32)
    v = jnp.concatenate([kvp_ref[:, PAIR:2 * PAIR], p_ref[:, O_V:O_V + PAIR]], axis=0)
    return k, v


def _band(s):
    return slice(BLK * s, BLK * s + 2 * BLK)


def _block(s):
    return slice(BLK * s, BLK * (s + 1))


def mixer_fwd(proj, bias, sinks, conv_w, qg, kg, cog, aog, name, residual=None, w_o=None):
    S = proj.shape[0]
    project = w_o is not None
    SUB = SUB_FWD_PROJECT if project else SUB_FWD
    ROWS = SUB * BLK
    steps = S // ROWS

    def body(sinks_ref, p_ref, kvp_ref, bias_ref, cw_ref, qg_ref, kg_ref, cog_ref, aog_ref, *rest):
        if project:
            x_ref, wo_ref, mix_ref, y_ref, ao_ref, x1_ref, ucar = rest
        else:
            mix_ref, y_ref, ao_ref, ucar = rest
        n = pl.program_id(0)

        @pl.when(n == 0)
        def _():
            ucar[...] = jnp.zeros_like(ucar)

        bg = p_ref[:, 0:CONV_CH].astype(F32)
        u = p_ref[:, CONV_CH:2 * CONV_CH].astype(F32) * p_ref[:, 2 * CONV_CH:3 * CONV_CH].astype(F32)
        row = lax.broadcasted_iota(jnp.int32, (ROWS, CONV_CH), 0)
        prev = ucar[...]
        u1 = jnp.where(row == 0, prev[7:8, :], pltpu.roll(u, 1, 0))
        u2 = jnp.where(row == 0, prev[6:7, :], jnp.where(row == 1, prev[7:8, :], pltpu.roll(u, 2, 0)))
        ucar[...] = u[ROWS - 8:ROWS, :]
        y = cw_ref[0:1, :] * u2 + cw_ref[1:2, :] * u1 + cw_ref[2:3, :] * u
        y_ref[...] = y.astype(BF16)
        co = bg * y
        mix_conv = ((co * _rms(co)) * cog_ref[...]).astype(BF16)
        mix_ref[:, 0:CONV_CH] = mix_conv
        if project:
            x1 = x_ref[...] + _dot(mix_conv, wo_ref[0:CONV_CH, :])

        low = _low_half()
        q_gain = qg_ref[...] * SCALE
        k, v = _kv_pairs(p_ref, kvp_ref)
        kn = ((k * _pair_rms(k, low)) * kg_ref[...]).astype(BF16)
        k_of = _one_head_in_both_halves(kn, low)
        v_of = _one_head_in_both_halves(v, low)
        biases = [bias_ref[jnp.minimum(n, 1)]] + [bias_ref[1]] * (SUB - 1)
        units = [(s, hq) for s in range(SUB) for hq in range(N_Q)]
        q_one = []
        for pair in range(N_Q // 2):
            q = p_ref[:, O_Q + PAIR * pair:O_Q + PAIR * (pair + 1)].astype(F32)
            qn = ((q * _pair_rms(q, low)) * q_gain).astype(BF16)
            q_one += [jnp.where(mine, qn, jnp.zeros_like(qn)) for mine in (low, jnp.logical_not(low))]
        scores = [_dot_nt(q_one[hq][_block(s)], k_of[hq // GRP][_band(s)]) + biases[s] for s, hq in units]
        probs = [_softmax_with_sink(sc, sinks_ref[hq])[0].astype(BF16) for (s, hq), sc in zip(units, scores)]
        o = {u_: _dot(pn, v_of[u_[1] // GRP][_band(u_[0])]) for u_, pn in zip(units, probs)}
        ao = jnp.concatenate(
            [jnp.concatenate([jnp.where(low, o[s, 2 * pair], o[s, 2 * pair + 1]) for pair in range(N_Q // 2)], axis=1)
             for s in range(SUB)], axis=0)
        ao_ref[...] = ao.astype(BF16)
        mix_attn = ((ao * _rms(ao)) * aog_ref[...]).astype(BF16)
        mix_ref[:, CONV_CH:] = mix_attn
        if project:
            x1_ref[...] = x1 + _dot(mix_attn, wo_ref[CONV_CH:, :])

    small = lambda shape: pl.BlockSpec(shape, lambda n: (0, 0))
    rows = lambda w: pl.BlockSpec((ROWS, w), lambda n: (n, 0))
    return pl.pallas_call(
        body, name=name, grid=(steps,),
        in_specs=[pl.BlockSpec(memory_space=pltpu.SMEM),
                  rows(IN_COLS),
                  pl.BlockSpec((BLK, 2 * PAIR), lambda n: (jnp.maximum(SUB * n - 1, 0), O_K // (2 * PAIR))),
                  pl.BlockSpec((2, BLK, 2 * BLK), lambda n: (0, 0, 0)),
                  small((3, CONV_CH)), small((1, PAIR)), small((1, PAIR)), small((1, CONV_CH)), small((1, ATTN_W))]
        + ([rows(D_MODEL), small((D_MODEL, D_MODEL))] if project else []),
        out_specs=[rows(D_MODEL), rows(CONV_CH), rows(ATTN_W)] + ([rows(D_MODEL)] if project else []),
        out_shape=[jax.ShapeDtypeStruct((S, D_MODEL), BF16), jax.ShapeDtypeStruct((S, CONV_CH), BF16),
                   jax.ShapeDtypeStruct((S, ATTN_W), BF16)]
        + ([jax.ShapeDtypeStruct((S, D_MODEL), F32)] if project else []),
        scratch_shapes=[pltpu.VMEM((8, CONV_CH), F32)],
        compiler_params=_cparams(("arbitrary",)),
    )(sinks, proj, proj, bias, conv_w, qg, kg, cog, aog, *([residual, w_o] if project else []))


def matmul_residual(a, w, res, name, dep=None):
    S, K = a.shape
    N = w.shape[1]

    def body(a_ref, w_ref, r_ref, o_ref):
        o_ref[...] = r_ref[...] + _dot(a_ref[...], w_ref[...])

    return _pallas_after(
        dep, body, a, w, res, name=name, grid=(S // TM,),
        in_specs=[pl.BlockSpec((TM, K), lambda i: (i, 0)), pl.BlockSpec((K, N), lambda i: (0, 0)),
                  pl.BlockSpec((TM, N), lambda i: (i, 0))],
        out_specs=pl.BlockSpec((TM, N), lambda i: (i, 0)),
        out_shape=jax.ShapeDtypeStruct((S, N), F32),
        compiler_params=_cparams(("parallel",)),
    )


TF = 1408


def ffn_fwd(x1, g, wg, wu, wd, name, dep=None, tgt=None):
    S = x1.shape[0]
    ni = S // TM
    with_loss = tgt is not None
    row = pl.BlockSpec((TM, D_MODEL), lambda i: (i, 0))
    vec = pl.BlockSpec((1, D_MODEL), lambda i: (0, 0))
    acts = [jax.ShapeDtypeStruct((S, D_FF), BF16)] * 3

    def half_weights(j):
        return [pl.BlockSpec((TF, D_MODEL), lambda i: (j, 0))] * 3

    def swiglu(h2, wg_ref, wu_ref, a_ref, b_ref, f_ref):
        a = _dot_nt(h2, wg_ref[...])
        b = _dot_nt(h2, wu_ref[...])
        a_ref[...] = a.astype(BF16)
        b_ref[...] = b.astype(BF16)
        f = ((a * _sigmoid(a)) * b).astype(BF16)
        f_ref[...] = f
        return f

    def first(x_ref, g_ref, wg_ref, wu_ref, wd_ref, h2_ref, a_ref, b_ref, f_ref, part_ref):
        xf = x_ref[...]
        h2 = ((xf * _rms(xf)) * g_ref[...]).astype(BF16)
        h2_ref[...] = h2
        part_ref[...] = xf + _dot(swiglu(h2, wg_ref, wu_ref, a_ref, b_ref, f_ref), wd_ref[...])

    h2, a, b, f, part = _pallas_after(
        dep, first, x1, g, wg, wu, wd, name=name + "_a", grid=(ni,),
        in_specs=[row, vec] + half_weights(0),
        out_specs=[row] + [pl.BlockSpec((TM, TF), lambda i: (i, 0))] * 3 + [row],
        out_shape=[jax.ShapeDtypeStruct((S, D_MODEL), BF16)] + acts + [jax.ShapeDtypeStruct((S, D_MODEL), F32)],
        compiler_params=_cparams(("parallel",)),
    )

    def second(h2_ref, part_ref, wg_ref, wu_ref, wd_ref, *rest):
        rest = rest[3:]
        if with_loss:
            t_ref, a_ref, b_ref, f_ref, o_ref, l_ref, sq = rest
        else:
            a_ref, b_ref, f_ref, o_ref = rest
        i = pl.program_id(0)
        out = part_ref[...] + _dot(swiglu(h2_ref[...], wg_ref, wu_ref, a_ref, b_ref, f_ref), wd_ref[...])
        if not with_loss:
            o_ref[...] = out
            return
        e = out - t_ref[...]
        o_ref[...] = e * (1.0 / D_MODEL)
        col = jnp.sum(e * e, axis=0, keepdims=True)

        @pl.when(i == 0)
        def _():
            sq[...] = col

        @pl.when(i > 0)
        def _():
            sq[...] += col

        @pl.when(i == ni - 1)
        def _():
            l_ref[...] = jnp.full((1, 128), jnp.sum(sq[...]) * (0.5 / D_MODEL), F32)

    results = pl.pallas_call(
        second, name=name + "_b", grid=(ni,),
        in_specs=[row, row] + half_weights(1) + [ANY] * 3 + ([row] if with_loss else []),
        out_specs=[pl.BlockSpec((TM, TF), lambda i: (i, 1))] * 3 + [row]
        + ([pl.BlockSpec((1, 128), lambda i: (0, 0))] if with_loss else []),
        out_shape=acts + [jax.ShapeDtypeStruct((S, D_MODEL), F32)]
        + ([jax.ShapeDtypeStruct((1, 128), F32)] if with_loss else []),
        input_output_aliases={5: 0, 6: 1, 7: 2},
        scratch_shapes=[pltpu.VMEM((1, D_MODEL), F32)] if with_loss else [],
        compiler_params=_cparams(("arbitrary",) if with_loss else ("parallel",)),
    )(h2, part, wg, wu, wd, a, b, f, *([tgt] if with_loss else []))
    return (h2, *results)


def ffn_bwd(dx2, a, b, x1, g, wg, wu, wd, name, dep=None):
    S = dx2.shape[0]
    ni = S // TM
    row = pl.BlockSpec((TM, D_MODEL), lambda i: (i, 0))
    vec = pl.BlockSpec((1, D_MODEL), lambda i: (0, 0))
    acts = [jax.ShapeDtypeStruct((S, D_FF), BF16)] * 2

    def half(j):
        return [pl.BlockSpec((TM, TF), lambda i: (i, j))] * 2, [pl.BlockSpec((TF, D_MODEL), lambda i: (j, 0))] * 3

    def through_half(dx_ref, a_ref, b_ref, wg_ref, wu_ref, wd_ref, da_ref, db_ref):
        df = _dot_nt(dx_ref[...].astype(BF16), wd_ref[...])
        av = a_ref[...].astype(F32)
        bv = b_ref[...].astype(F32)
        sg = _sigmoid(av)
        da = ((df * bv) * (sg * (1.0 + av * (1.0 - sg)))).astype(BF16)
        db = (df * (av * sg)).astype(BF16)
        da_ref[...] = da
        db_ref[...] = db
        return _dot(da, wg_ref[...]) + _dot(db, wu_ref[...])

    def first(dx_ref, a_ref, b_ref, wg_ref, wu_ref, wd_ref, da_ref, db_ref, part_ref):
        part_ref[...] = through_half(dx_ref, a_ref, b_ref, wg_ref, wu_ref, wd_ref, da_ref, db_ref)

    tiles, weights = half(0)
    da, db, part = _pallas_after(
        dep, first, dx2, a, b, wg, wu, wd, name=name + "_a", grid=(ni,),
        in_specs=[row] + tiles + weights, out_specs=tiles + [row],
        out_shape=acts + [jax.ShapeDtypeStruct((S, D_MODEL), F32)],
        compiler_params=_cparams(("parallel",)),
    )

    def second(dx_ref, a_ref, b_ref, wg_ref, wu_ref, wd_ref, part_ref, x_ref, g_ref, da_prev, db_prev,
               da_ref, db_ref, dx1_ref, dg_ref):
        i = pl.program_id(0)

        @pl.when(i == 0)
        def _():
            dg_ref[...] = jnp.zeros_like(dg_ref)

        dh = part_ref[...] + through_half(dx_ref, a_ref, b_ref, wg_ref, wu_ref, wd_ref, da_ref, db_ref)
        xf = x_ref[...]
        r = _rms(xf)
        xh = xf * r
        dg_ref[...] += jnp.sum(dh * xh, axis=0, keepdims=True)
        dx1_ref[...] = dx_ref[...] + _rms_bwd(dh, xh, r, g_ref[...])

    tiles, weights = half(1)
    return pl.pallas_call(
        second, name=name + "_b", grid=(ni,),
        in_specs=[row] + tiles + weights + [row, row, vec, ANY, ANY],
        out_specs=tiles + [row, vec],
        out_shape=acts + [jax.ShapeDtypeStruct((S, D_MODEL), F32), jax.ShapeDtypeStruct((1, D_MODEL), F32)],
        input_output_aliases={9: 0, 10: 1},
        compiler_params=_cparams(("arbitrary",)),
    )(dx2, a, b, wg, wu, wd, part, x1, g, da, db)


def grad_weight(lhs, rhs, tm, name, tk=1024):
    S, N = rhs.shape
    M = lhs[0].shape[1]
    tk = min(tk, S)
    nk = S // tk
    n_l = len(lhs)

    def body(*refs):
        l_refs, r_ref = refs[:n_l], refs[n_l]
        o_refs, accs = refs[n_l + 1:2 * n_l + 1], refs[2 * n_l + 1:]
        k = pl.program_id(1)
        rv = r_ref[...].astype(BF16)
        for l_ref, o_ref, acc in zip(l_refs, o_refs, accs):
            @pl.when(k == 0)
            def _():
                acc[...] = jnp.zeros_like(acc)

            acc[...] += _dot_tn(l_ref[...], rv)

            @pl.when(k == nk - 1)
            def _():
                o_ref[...] = acc[...].astype(BF16)

    return pl.pallas_call(
        body, name=name, grid=(M // tm, nk),
        in_specs=[pl.BlockSpec((tk, tm), lambda i, k: (k, i))] * n_l + [pl.BlockSpec((tk, N), lambda i, k: (k, 0))],
        out_specs=[pl.BlockSpec((tm, N), lambda i, k: (i, 0))] * n_l,
        out_shape=[jax.ShapeDtypeStruct((M, N), BF16)] * n_l,
        scratch_shapes=[pltpu.VMEM((tm, N), F32)] * n_l,
        compiler_params=_cparams(("parallel", "arbitrary")),
    )(*lhs, rhs)


def proj_bwd(dproj, w, x, g, dres, name, dep=None):
    S, N = dproj.shape

    def body(dp_ref, w_ref, x_ref, g_ref, dr_ref, dx_ref, dg_ref):
        i = pl.program_id(0)

        @pl.when(i == 0)
        def _():
            dg_ref[...] = jnp.zeros_like(dg_ref)

        dh = _dot(dp_ref[...], w_ref[...])
        xf = x_ref[...]
        r = _rms(xf)
        xh = xf * r
        dg_ref[...] += jnp.sum(dh * xh, axis=0, keepdims=True)
        dx_ref[...] = dr_ref[...] + _rms_bwd(dh, xh, r, g_ref[...])

    return _pallas_after(
        dep, body, dproj, w, x, g, dres, name=name, grid=(S // TM,),
        in_specs=[pl.BlockSpec((TM, N), lambda i: (i, 0)), pl.BlockSpec((N, D_MODEL), lambda i: (0, 0)),
                  pl.BlockSpec((TM, D_MODEL), lambda i: (i, 0)), pl.BlockSpec((1, D_MODEL), lambda i: (0, 0)),
                  pl.BlockSpec((TM, D_MODEL), lambda i: (i, 0))],
        out_specs=[pl.BlockSpec((TM, D_MODEL), lambda i: (i, 0)), pl.BlockSpec((1, D_MODEL), lambda i: (0, 0))],
        out_shape=[jax.ShapeDtypeStruct((S, D_MODEL), F32), jax.ShapeDtypeStruct((1, D_MODEL), F32)],
        compiler_params=_cparams(("arbitrary",)),
    )


def mixer_bwd(proj, bias, y, ao, dx1, w_o, sinks, conv_w, qg, kg, cog, aog, name, dep=None):
    S = proj.shape[0]
    SUB, ROWS = SUB_BWD, SUB_BWD * BLK
    steps = S // ROWS
    KV_W = 2 * PAIR

    def body(sinks_ref, p_ref, kvp_ref, bias_ref, y_ref, ao_ref, dx_first_ref, dx_next_ref, wo_ref, cw_ref, qg_ref,
             kg_ref, cog_ref, aog_ref, dp_ref, dcw_ref, dqg_ref, dkg_ref, dsk_ref, dcog_ref, daog_ref,
             dycar, kcar, vcar, dmix_scr):
        step = pl.program_id(0)

        @pl.when(step == 0)
        def _():
            dmix_scr[...] = _dot_nt(dx_first_ref[...].astype(BF16), wo_ref[...])
            dycar[...] = jnp.zeros_like(dycar)
            kcar[...] = jnp.zeros_like(kcar)
            vcar[...] = jnp.zeros_like(vcar)
            dcw_ref[...] = jnp.zeros_like(dcw_ref)
            dqg_ref[...] = jnp.zeros_like(dqg_ref)
            dkg_ref[...] = jnp.zeros_like(dkg_ref)
            dsk_ref[...] = jnp.zeros_like(dsk_ref)
            dcog_ref[...] = jnp.zeros_like(dcog_ref)
            daog_ref[...] = jnp.zeros_like(daog_ref)

        dma = dmix_scr[:, CONV_CH:]

        aov = ao_ref[...].astype(F32)
        ra = _rms(aov)
        ah = aov * ra
        daog_ref[...] += jnp.sum(dma * ah, axis=0, keepdims=True)
        dao = _rms_bwd(dma, ah, ra, aog_ref[...])

        low = _low_half()
        high = jnp.logical_not(low)
        q_gain = qg_ref[...] * SCALE
        k, v = _kv_pairs(p_ref, kvp_ref)
        rk = _pair_rms(k, low)
        kh = k * rk
        k_of = _one_head_in_both_halves((kh * kg_ref[...]).astype(BF16), low)
        v_of = _one_head_in_both_halves(v, low)
        lane8 = lax.broadcasted_iota(jnp.int32, (1, N_Q), 1)
        dsk = jnp.zeros((1, N_Q), F32)
        dqg = jnp.zeros((1, PAIR), F32)
        biases = [bias_ref[jnp.minimum(steps - 1 - step, 1)]] + [bias_ref[1]] * (SUB - 1)
        units = [(s, hq) for s in range(SUB) for hq in range(N_Q)]
        rq, qh, q_one, do_one, delta = [], [], [], [], []
        for pair in range(N_Q // 2):
            cols = slice(PAIR * pair, PAIR * (pair + 1))
            q = p_ref[:, O_Q + PAIR * pair:O_Q + PAIR * (pair + 1)].astype(F32)
            rq.append(_pair_rms(q, low))
            qh.append(q * rq[pair])
            qn = (qh[pair] * q_gain).astype(BF16)
            do = dao[:, cols]
            do_b = do.astype(BF16)
            delta += _half_sums(do * aov[:, cols], low)
            for mine in (low, high):
                q_one.append(jnp.where(mine, qn, jnp.zeros_like(qn)))
                do_one.append(jnp.where(mine, do_b, jnp.zeros_like(do_b)))
        scores = {u: _dot_nt(q_one[u[1]][_block(u[0])], k_of[u[1] // GRP][_band(u[0])]) + biases[u[0]] for u in units}
        dprobs = {u: _dot_nt(do_one[u[1]][_block(u[0])], v_of[u[1] // GRP][_band(u[0])]) for u in units}
        probs = {u: _softmax_with_sink(scores[u], sinks_ref[u[1]]) for u in units}
        ds = {u: (probs[u][0] * (dprobs[u] - delta[u[1]][_block(u[0])])).astype(BF16) for u in units}
        dv_t = {(s, h): jnp.zeros((PAIR, 2 * BLK), F32) for s in range(SUB) for h in range(N_KV)}
        dkn_t = dict(dv_t)
        for s, hq in units:
            dsk = dsk - jnp.where(lane8 == hq, jnp.sum(probs[s, hq][1] * delta[hq][_block(s)]), 0.0)
            dv_t[s, hq // GRP] = dv_t[s, hq // GRP] + _dot_tn(do_one[hq][_block(s)], probs[s, hq][0].astype(BF16))
            dkn_t[s, hq // GRP] = dkn_t[s, hq // GRP] + _dot_tn(q_one[hq][_block(s)], ds[s, hq])
        dqn_of = {u: _dot(ds[u], k_of[u[1] // GRP][_band(u[0])]) for u in units}
        for pair in range(N_Q // 2):
            dqn = jnp.concatenate([jnp.where(low, dqn_of[s, 2 * pair], dqn_of[s, 2 * pair + 1]) for s in range(SUB)],
                                  axis=0)
            dqg = dqg + jnp.sum(dqn * qh[pair], axis=0, keepdims=True)
            dqh = dqn * q_gain
            dp_ref[:, O_Q + PAIR * pair:O_Q + PAIR * (pair + 1)] = (
                rq[pair] * (dqh - qh[pair] * _pair_mean(dqh * qh[pair], low))).astype(BF16)

        def over_key_rows(parts):
            bands = [jnp.concatenate([parts[s, h][:HEAD] + parts[s, h][HEAD:] for h in range(N_KV)], axis=0).T
                     for s in range(SUB)]
            pieces = [bands[0][:BLK]]
            pieces += [bands[s - 1][BLK:] + bands[s][:BLK] for s in range(1, SUB)]
            pieces.append(bands[SUB - 1][BLK:])
            return jnp.concatenate(pieces, axis=0)

        dv = over_key_rows(dv_t)
        dkn = over_key_rows(dkn_t)
        dkg_ref[...] += jnp.sum(dkn * kh, axis=0, keepdims=True)
        dkh = dkn * kg_ref[...]
        dk = rk * (dkh - kh * _pair_mean(dkh * kh, low))
        last = slice(ROWS, ROWS + BLK)
        dp_ref[:, O_K:O_V] = jnp.concatenate([dk[BLK:ROWS], dk[last] + kcar[...]], axis=0).astype(BF16)
        dp_ref[:, O_V:] = jnp.concatenate([dv[BLK:ROWS], dv[last] + vcar[...]], axis=0).astype(BF16)
        kcar[...] = dk[:BLK, :]
        vcar[...] = dv[:BLK, :]
        dsk_ref[...] += dsk
        dqg_ref[...] += dqg * SCALE

        bg = p_ref[:, 0:CONV_CH].astype(F32)
        cg = p_ref[:, CONV_CH:2 * CONV_CH].astype(F32)
        hc = p_ref[:, 2 * CONV_CH:3 * CONV_CH].astype(F32)
        yv = y_ref[...].astype(F32)
        dmc = dmix_scr[:, 0:CONV_CH]
        co = bg * yv
        rc = _rms(co)
        ch = co * rc
        dcog_ref[...] += jnp.sum(dmc * ch, axis=0, keepdims=True)
        dco = _rms_bwd(dmc, ch, rc, cog_ref[...])
        dp_ref[:, 0:CONV_CH] = (dco * yv).astype(BF16)
        dy = dco * bg
        row = lax.broadcasted_iota(jnp.int32, (ROWS, CONV_CH), 0)
        nxt = dycar[...]
        dy1 = jnp.where(row == ROWS - 1, nxt[0:1, :], pltpu.roll(dy, ROWS - 1, 0))
        dy2 = jnp.where(row == ROWS - 2, nxt[0:1, :],
                        jnp.where(row == ROWS - 1, nxt[1:2, :], pltpu.roll(dy, ROWS - 2, 0)))
        dycar[...] = dy[0:8, :]
        du = cw_ref[2:3, :] * dy + cw_ref[1:2, :] * dy1 + cw_ref[0:1, :] * dy2
        dp_ref[:, CONV_CH:2 * CONV_CH] = (du * hc).astype(BF16)
        dp_ref[:, 2 * CONV_CH:3 * CONV_CH] = (du * cg).astype(BF16)
        u = cg * hc
        dcw_ref[0:1, :] += jnp.sum(dy2 * u, axis=0, keepdims=True)
        dcw_ref[1:2, :] += jnp.sum(dy1 * u, axis=0, keepdims=True)
        dcw_ref[2:3, :] += jnp.sum(dy * u, axis=0, keepdims=True)

        dmix_scr[...] = _dot_nt(dx_next_ref[...].astype(BF16), wo_ref[...])

    small = lambda shape: pl.BlockSpec(shape, lambda s: (0, 0))
    blk = lambda w: pl.BlockSpec((ROWS, w), lambda s: (steps - 1 - s, 0))
    return _pallas_after(
        dep, body, sinks, proj, proj, bias, y, ao, dx1, dx1, w_o, conv_w, qg, kg, cog, aog, name=name, grid=(steps,),
        in_specs=[pl.BlockSpec(memory_space=pltpu.SMEM),
                  blk(IN_COLS),
                  pl.BlockSpec((BLK, KV_W), lambda s: (jnp.maximum(SUB * (steps - 1 - s) - 1, 0), O_K // KV_W)),
                  pl.BlockSpec((2, BLK, 2 * BLK), lambda s: (0, 0, 0)),
                  blk(CONV_CH), blk(ATTN_W),
                  pl.BlockSpec((ROWS, D_MODEL), lambda s: (steps - 1, 0)),
                  pl.BlockSpec((ROWS, D_MODEL), lambda s: (jnp.maximum(steps - 2 - s, 0), 0)),
                  small((D_MODEL, D_MODEL)),
                  small((3, CONV_CH)), small((1, PAIR)), small((1, PAIR)), small((1, CONV_CH)), small((1, ATTN_W))],
        out_specs=[blk(IN_COLS), small((3, CONV_CH)), small((1, PAIR)), small((1, PAIR)), small((1, N_Q)),
                   small((1, CONV_CH)), small((1, ATTN_W))],
        out_shape=[jax.ShapeDtypeStruct((S, IN_COLS), BF16), jax.ShapeDtypeStruct((3, CONV_CH), F32),
                   jax.ShapeDtypeStruct((1, PAIR), F32), jax.ShapeDtypeStruct((1, PAIR), F32),
                   jax.ShapeDtypeStruct((1, N_Q), F32), jax.ShapeDtypeStruct((1, CONV_CH), F32),
                   jax.ShapeDtypeStruct((1, ATTN_W), F32)],
        scratch_shapes=[pltpu.VMEM((8, CONV_CH), F32), pltpu.VMEM((BLK, PAIR), F32), pltpu.VMEM((BLK, PAIR), F32),
                        pltpu.VMEM((ROWS, D_MODEL), F32)],
        compiler_params=_cparams(("arbitrary",)),
    )


OTHER_CHIPS = ((1, 0), (0, 1), (1, 1))


def _position():
    return lax.axis_index("x"), lax.axis_index("y"), lax.axis_index("c")


def all_gather(shards, name):
    n = len(shards)

    def body(*refs):
        x_refs, out_refs = refs[:n], refs[n:2 * n]
        send_sems, recv_sems, local_sems = refs[2 * n:]
        x, y, c = _position()
        me, sibling = (x, y, c), (x, y, 1 - c)
        chips = [(x ^ mx, y ^ my) for mx, my in OTHER_CHIPS]

        def slab(a, px, py, pc):
            return out_refs[a].at[4 * px + 2 * py + pc]

        def copy(a, k, block, to, src=None):
            return pltpu.make_async_remote_copy(
                src_ref=slab(a, *block) if src is None else src, dst_ref=slab(a, *block),
                send_sem=send_sems.at[7 * a + k], recv_sem=recv_sems.at[7 * a + k], device_id=to, device_id_type=MESH_T)

        mine = [pltpu.make_async_copy(x_refs[a], slab(a, *me), local_sems.at[a]) for a in range(n)]
        for cp in mine:
            cp.start()
        first = []
        for a in range(n):
            first.append(copy(a, 0, me, sibling, src=x_refs[a]))
            first += [copy(a, 1 + j, me, (*chip, c), src=x_refs[a]) for j, chip in enumerate(chips)]
        for cp in first:
            cp.start()
        passed = []
        for a in range(n):
            for j, chip in enumerate(chips):
                copy(a, 1 + j, (*chip, c), me).wait_recv()
                passed.append(copy(a, 4 + j, (*chip, c), sibling))
                passed[-1].start()
        for a in range(n):
            copy(a, 0, sibling, me).wait_recv()
            for j, chip in enumerate(chips):
                copy(a, 4 + j, (*chip, 1 - c), me).wait_recv()
        for cp in first + passed:
            cp.wait_send()
        for cp in mine:
            cp.wait()

    return pl.pallas_call(
        body, name=name, in_specs=[ANY] * n, out_specs=[ANY] * n,
        out_shape=[jax.ShapeDtypeStruct((N_DEV, *s.shape), s.dtype) for s in shards],
        scratch_shapes=[pltpu.SemaphoreType.DMA((7 * n,)), pltpu.SemaphoreType.DMA((7 * n,)),
                        pltpu.SemaphoreType.DMA((n,))],
    )(*shards)


class SplitCopy:
    def __init__(self, name, arrays, n_copies, plan, after=None):
        n = len(arrays)
        self.name, self.n, self.n_copies, self.plan = name, n, n_copies, plan
        extra = [] if after is None else [after]

        def body(*refs):
            in_refs = refs[:n]
            send_sems, recv_sems = refs[n + len(extra)], refs[n + len(extra) + 1]
            token = refs[2 * n + len(extra) + 2]
            for k, (src, dst, to) in enumerate(plan(_position(), in_refs)):
                pltpu.make_async_remote_copy(src_ref=src, dst_ref=dst, send_sem=send_sems.at[k],
                                             recv_sem=recv_sems.at[k], device_id=to, device_id_type=MESH_T).start()
            token[...] = jnp.zeros_like(token)

        outs = pl.pallas_call(
            body, name=name + "_start",
            out_shape=(pltpu.SemaphoreType.DMA((n_copies,)), pltpu.SemaphoreType.DMA((n_copies,)),
                       *[pltpu.HBM(a.shape, a.dtype) for a in arrays], jax.ShapeDtypeStruct((8, 128), F32)),
            in_specs=[HBM] * n + [ANY] * len(extra),
            out_specs=(SEM, SEM, *[HBM] * n, pl.BlockSpec(memory_space=pltpu.VMEM)),
            input_output_aliases={i: 2 + i for i in range(n)},
            compiler_params=pltpu.CompilerParams(has_side_effects=pltpu.SideEffectType.DATAFLOW_SIDE_EFFECTING),
        )(*[pltpu.with_memory_space_constraint(a, pltpu.HBM) for a in arrays], *extra)
        self.send_sems, self.recv_sems = outs[0], outs[1]
        self.arrays, self.token = list(outs[2:2 + n]), outs[2 + n]

    def wait(self, after):
        n, plan = self.n, self.plan

        def body(*refs):
            in_refs, send_sems, recv_sems = refs[:n], refs[n], refs[n + 1]
            for k, (src, dst, to) in enumerate(plan(_position(), in_refs)):
                cp = pltpu.make_async_remote_copy(src_ref=src, dst_ref=dst, send_sem=send_sems.at[k],
                                                  recv_sem=recv_sems.at[k], device_id=to, device_id_type=MESH_T)
                cp.wait_send()
                cp.wait_recv()

        outs = pl.pallas_call(
            body, name=self.name + "_wait",
            out_shape=tuple(pltpu.HBM(a.shape, a.dtype) for a in self.arrays),
            in_specs=[HBM] * n + [SEM, SEM, ANY], out_specs=tuple([HBM] * n),
            input_output_aliases={i: i for i in range(n)},
            compiler_params=pltpu.CompilerParams(has_side_effects=pltpu.SideEffectType.DATAFLOW_SIDE_EFFECTING),
        )(*self.arrays, self.send_sems, self.recv_sems, after)
        return list(outs)


def gather_start(shards, me, name, after=None):
    n = len(shards)
    lands = [lax.dynamic_update_slice(lax.empty((N_DEV, *s.shape), s.dtype), s[None], (me, 0, 0)) for s in shards]

    def plan(pos, refs):
        x, y, c = pos
        return [(refs[a], refs[n + a].at[4 * x + 2 * y + c], (x ^ mx, y ^ my, c))
                for a in range(n) for mx, my in OTHER_CHIPS]

    return SplitCopy(name, list(shards) + lands, 3 * n, plan, after=after)


def sibling_start(lands, name):
    n = len(lands)

    def plan(pos, refs):
        x, y, c = pos
        return [(refs[a].at[2 * q + c], refs[a].at[2 * q + c], (x, y, 1 - c)) for a in range(n) for q in range(4)]

    return SplitCopy(name, list(lands), 4 * n, plan)


def scatter_start(slabs, name):
    n = len(slabs)
    lands = [lax.empty((N_DEV - 1, *g.shape[1:]), g.dtype) for g in slabs]

    def plan(pos, refs):
        x, y, c = pos
        copies = []
        for a in range(n):
            for r in range(1, N_DEV):
                px, py, pc = x ^ ((r >> 2) & 1), y ^ ((r >> 1) & 1), c ^ (r & 1)
                copies.append((refs[a].at[4 * px + 2 * py + pc], refs[n + a].at[r - 1], (px, py, pc)))
        return copies

    return SplitCopy(name, list(slabs) + lands, (N_DEV - 1) * n, plan)


def all_reduce_small(v, name):
    R, W = v.shape

    def body(v_ref, o_ref, recv, send_sems, recv_sems):
        x, y, c = _position()
        me = 4 * x + 2 * y + c
        copies = []
        for r in range(1, N_DEV):
            to = (x ^ ((r >> 2) & 1), y ^ ((r >> 1) & 1), c ^ (r & 1))
            copies.append(pltpu.make_async_remote_copy(
                src_ref=v_ref, dst_ref=recv.at[me], send_sem=send_sems.at[r - 1], recv_sem=recv_sems.at[r - 1],
                device_id=to, device_id_type=MESH_T))
        for cp in copies:
            cp.start()
        recv[pl.ds(me, 1)] = v_ref[...][None]
        for cp in copies:
            cp.wait()
        acc = recv[0]
        for s in range(1, N_DEV):
            acc = acc + recv[s]
        o_ref[...] = acc

    return pl.pallas_call(
        body, name=name,
        in_specs=[pl.BlockSpec(memory_space=pltpu.VMEM)], out_specs=pl.BlockSpec(memory_space=pltpu.VMEM),
        out_shape=jax.ShapeDtypeStruct((R, W), F32),
        scratch_shapes=[pltpu.VMEM((N_DEV, R, W), F32), pltpu.SemaphoreType.DMA((N_DEV - 1,)),
                        pltpu.SemaphoreType.DMA((N_DEV - 1,))],
    )(v)


def _row_tile(rows):
    if rows <= 512:
        return rows
    return max(t for t in range(8, 513, 8) if rows % t == 0)


def _adamw_update(w, g, m, v):
    mn = ADAM_B1 * m + (1.0 - ADAM_B1) * g
    vn = ADAM_B2 * v + (1.0 - ADAM_B2) * (g * g)
    m_hat = mn / (1.0 - ADAM_B1 ** ADAM_STEP)
    v_hat = vn / (1.0 - ADAM_B2 ** ADAM_STEP)
    return -ADAM_LR * (m_hat / (jnp.sqrt(v_hat) + ADAM_EPS) + ADAM_WD * w), mn, vn


def adamw(w, g, m, v, name):
    R, W = w.shape
    tr = _row_tile(R)

    def body(w_ref, g_ref, m_ref, v_ref, d_ref, mo_ref, vo_ref):
        d_ref[...], mo_ref[...], vo_ref[...] = _adamw_update(w_ref[...], g_ref[...], m_ref[...], v_ref[...])

    spec = pl.BlockSpec((tr, W), lambda i: (i, 0))
    return pl.pallas_call(
        body, name=name, grid=(R // tr,), in_specs=[spec] * 4, out_specs=[spec] * 3,
        out_shape=[jax.ShapeDtypeStruct((R, W), F32)] * 3,
        compiler_params=_cparams(("parallel",)),
    )(w, g, m, v)


def reduce_adamw(slabs, land, w, m, v, layer, me_arr, name, others=None):
    _, R, W = slabs.shape
    tr = _row_tile(R)
    n_other = 0 if others is None else 4

    def body(me_ref, s_ref, l_ref, w_ref, m_ref, v_ref, *rest):
        g_ref, d_ref, mo_ref, vo_ref = rest[n_other:]
        g = s_ref[0].astype(F32)
        for r in range(N_DEV - 1):
            g = g + l_ref[r].astype(F32)
        g_ref[0] = g
        d_ref[0], mo_ref[0], vo_ref[0] = _adamw_update(w_ref[0], g, m_ref[0], v_ref[0])

    spec = pl.BlockSpec((1, tr, W), lambda i, me: (layer, i, 0))
    return pl.pallas_call(
        body, name=name,
        grid_spec=pltpu.PrefetchScalarGridSpec(
            num_scalar_prefetch=1, grid=(R // tr,),
            in_specs=[pl.BlockSpec((1, tr, W), lambda i, me: (me[0], i, 0)),
                      pl.BlockSpec((N_DEV - 1, tr, W), lambda i, me: (0, i, 0)), spec, spec, spec] + [ANY] * n_other,
            out_specs=[spec] * 4),
        out_shape=[jax.ShapeDtypeStruct((DEPTH, R, W), F32)] * 4,
        input_output_aliases={6 + i: i for i in range(n_other)},
        compiler_params=_cparams(("parallel",)),
    )(me_arr, slabs, land, w, m, v, *([] if others is None else others))


SMALL_NAMES = ("norm1_g", "q_norm_g", "k_norm_g", "sinks", "conv_out_g", "attn_out_g", "norm2_g", "conv_w")
SMALL_SIZES = (D_MODEL, HEAD, HEAD, N_Q, CONV_CH, ATTN_W, D_MODEL, 3 * CONV_CH)
SMALL_ROWS = 80


def kernel(x, norm1_g, w_in, conv_w, q_norm_g, k_norm_g, sinks, conv_out_g, attn_out_g, w_o, norm2_g, w_gate, w_up, w_down, loss_target, m_norm1_g, m_w_in, m_conv_w, m_q_norm_g, m_k_norm_g, m_sinks, m_conv_out_g, m_attn_out_g, m_w_o, m_norm2_g, m_w_gate, m_w_up, m_w_down, v_norm1_g, v_w_in, v_conv_w, v_q_norm_g, v_k_norm_g, v_sinks, v_conv_out_g, v_attn_out_g, v_w_o, v_norm2_g, v_w_gate, v_w_up, v_w_down):
    xi, yi, ci = _position()
    me = 4 * xi + 2 * yi + ci
    me_arr = jnp.reshape(me, (1,)).astype(jnp.int32)
    xs, tgt = x[0], loss_target[0]
    bf = lambda a: a.astype(BF16)
    bias = band_bias()

    t = lambda a: jnp.swapaxes(a, 1, 2)
    shard = dict(w_in=(t(w_in), t(m_w_in), t(v_w_in)), w_o=(w_o, m_w_o, v_w_o),
                 w_gate=(t(w_gate), t(m_w_gate), t(v_w_gate)), w_up=(t(w_up), t(m_w_up), t(v_w_up)),
                 w_down=(w_down, m_w_down, v_w_down))
    wb = {n: bf(shard[n][0]) for n in shard}

    g_in0, g_o0, g_conv = all_gather([wb["w_in"][0], wb["w_o"][0], conv_w.reshape(DEPTH * 3, HEAD)], "gather_first")
    ag_ffn0 = gather_start([wb["w_gate"][0], wb["w_up"][0], wb["w_down"][0]], me, "gather_ffn0", after=g_in0)
    ag_mix1 = gather_start([wb["w_in"][1], wb["w_o"][1]], me, "gather_mix1", after=ag_ffn0.token)
    ag_ffn1 = gather_start([wb["w_gate"][1], wb["w_up"][1], wb["w_down"][1]], me, "gather_ffn1", after=ag_mix1.token)
    conv_full = g_conv.reshape(N_DEV, DEPTH, 3, HEAD).transpose(1, 2, 0, 3).reshape(DEPTH, 3, CONV_CH)
    pair_gain = lambda g: jnp.tile(g[None], (1, 2))
    small = [dict(norm1_g=norm1_g[l][None], conv_w=conv_full[l], q_norm_g=pair_gain(q_norm_g[l]),
                  k_norm_g=pair_gain(k_norm_g[l]), sinks=sinks[l], conv_out_g=conv_out_g[l][None],
                  attn_out_g=attn_out_g[l][None], norm2_g=norm2_g[l][None]) for l in range(DEPTH)]
    whole = lambda g: g.reshape(-1, D_MODEL)
    weights = [dict(w_in=whole(g_in0), w_o=whole(g_o0)), {}]

    def ffn_weights(g_gate, g_up, g_down):
        return dict(w_gate=whole(g_gate), w_up=whole(g_up), w_down=whole(g_down))

    saved = []
    xl = xs
    for l in range(DEPTH):
        sp, wl = small[l], weights[l]
        h, proj = norm_proj(xl, sp["norm1_g"], wl["w_in"], f"norm_proj{l}",
                            dep=ag_ffn1.token if l == 0 else pass_ffn1.token)
        mixer_args = (proj, bias, sp["sinks"], sp["conv_w"], sp["q_norm_g"], sp["k_norm_g"], sp["conv_out_g"],
                      sp["attn_out_g"], f"mixer_fwd{l}")
        if l == 0:
            mix, y, ao = mixer_fwd(*mixer_args)
            pass_ffn0 = sibling_start(ag_ffn0.wait(mix)[3:], "pass_ffn0")
            x1 = matmul_residual(mix, wl["w_o"], xl, "out_proj0", dep=pass_ffn0.token)
            wl.update(ffn_weights(*pass_ffn0.wait(x1)))
            pass_mix1 = sibling_start(ag_mix1.wait(x1)[2:], "pass_mix1")
            h2, a, b, f, x2 = ffn_fwd(x1, sp["norm2_g"], wl["w_gate"], wl["w_up"], wl["w_down"], "ffn_fwd0",
                                      dep=pass_mix1.token)
            g_in, g_o = pass_mix1.wait(x2)
            weights[1] = dict(w_in=whole(g_in), w_o=whole(g_o))
            pass_ffn1 = sibling_start(ag_ffn1.wait(x2)[3:], "pass_ffn1")
        else:
            mix, y, ao, x1 = mixer_fwd(*mixer_args, residual=xl, w_o=wl["w_o"])
            wl.update(ffn_weights(*pass_ffn1.wait(x1)))
            h2, a, b, f, dx, loss_row = ffn_fwd(x1, sp["norm2_g"], wl["w_gate"], wl["w_up"], wl["w_down"], "ffn_fwd1",
                                                tgt=tgt)
            x2 = None
        saved.append((xl, h, proj, mix, y, ao, x1, h2, a, b, f))
        xl = x2

    stepped = {n: None for n in shard}
    slabs = lambda d: d.reshape(N_DEV, -1, D_MODEL)
    gsmall = [None] * DEPTH

    def finish(sc, names, after, l):
        arrays = sc.wait(after)
        k = len(names)
        for i, n in enumerate(names):
            w, m, v = shard[n]
            stepped[n] = reduce_adamw(arrays[i], arrays[k + i], w, m, v, l, me_arr, f"reduce_adamw_{n}{l}",
                                      others=stepped[n])

    for l in reversed(range(DEPTH)):
        sp, wl = small[l], weights[l]
        x0, h, proj, mix, y, ao, x1, h2, a, b, f = saved[l]
        (d_wd,) = grad_weight((f,), dx, TF, f"grad_w_down{l}", tk=2048)
        sc_down = scatter_start([slabs(d_wd)], f"scatter_w_down{l}")
        da, db, dx1, d_g2 = ffn_bwd(dx, a, b, x1, sp["norm2_g"], wl["w_gate"], wl["w_up"], wl["w_down"],
                                    f"ffn_bwd{l}", dep=sc_down.token)
        d_wg, d_wu = grad_weight((da, db), h2, TF, f"grad_w_gate_up{l}")
        (d_wo,) = grad_weight((mix,), dx1, D_MODEL, f"grad_w_o{l}", tk=2048)
        sc_rest = scatter_start([slabs(d_wg), slabs(d_wu), slabs(d_wo)], f"scatter_w_gate_up_o{l}")
        dproj, d_cw, d_qg, d_kg, d_sk, d_cog, d_aog = mixer_bwd(
            proj, bias, y, ao, dx1, wl["w_o"], sp["sinks"], sp["conv_w"], sp["q_norm_g"], sp["k_norm_g"],
            sp["conv_out_g"], sp["attn_out_g"], f"mixer_bwd{l}", dep=sc_rest.token)
        finish(sc_down, ["w_down"], dproj, l)
        finish(sc_rest, ["w_gate", "w_up", "w_o"], dproj, l)
        (d_win,) = grad_weight((dproj,), h, IN_COLS // 2, f"grad_w_in{l}", tk=2048)
        sc_in = scatter_start([slabs(d_win)], f"scatter_w_in{l}")
        dx, d_g1 = proj_bwd(dproj, wl["w_in"], x0, sp["norm1_g"], dx1, f"proj_bwd{l}", dep=sc_in.token)
        finish(sc_in, ["w_in"], dx, l)
        both_heads = lambda d: d[:, :HEAD] + d[:, HEAD:]
        gsmall[l] = dict(norm1_g=d_g1, conv_w=d_cw, q_norm_g=both_heads(d_qg), k_norm_g=both_heads(d_kg), sinks=d_sk,
                         conv_out_g=d_cog, attn_out_g=d_aog, norm2_g=d_g2)
    grad_x = dx[None]

    flat = jnp.concatenate([loss_row[0, 0:1]] + [gsmall[l][n].reshape(-1) for l in range(DEPTH) for n in SMALL_NAMES])
    flat = jnp.pad(flat, (0, SMALL_ROWS * 128 - flat.shape[0])).reshape(SMALL_ROWS, 128)
    flat = all_reduce_small(flat, "all_reduce_small_grads").reshape(-1)
    loss = flat[0]
    gs = {n: [] for n in SMALL_NAMES}
    off = 1
    for l in range(DEPTH):
        for n, size in zip(SMALL_NAMES, SMALL_SIZES):
            gs[n].append(flat[off:off + size])
            off += size
    gs = {n: jnp.stack(v) for n, v in gs.items()}
    g_conv = lax.dynamic_slice(gs["conv_w"].reshape(DEPTH, 3, CONV_CH), (0, 0, me * HEAD), (DEPTH, 3, HEAD))

    gs["conv_w"] = g_conv
    params = dict(norm1_g=(norm1_g, m_norm1_g, v_norm1_g), conv_w=(conv_w, m_conv_w, v_conv_w),
                  q_norm_g=(q_norm_g, m_q_norm_g, v_q_norm_g), k_norm_g=(k_norm_g, m_k_norm_g, v_k_norm_g),
                  sinks=(sinks, m_sinks, v_sinks), conv_out_g=(conv_out_g, m_conv_out_g, v_conv_out_g),
                  attn_out_g=(attn_out_g, m_attn_out_g, v_attn_out_g), norm2_g=(norm2_g, m_norm2_g, v_norm2_g))
    names = ("norm1_g", "w_in", "conv_w", "q_norm_g", "k_norm_g", "sinks", "conv_out_g", "attn_out_g", "w_o",
             "norm2_g", "w_gate", "w_up", "w_down")

    out = {}
    for n in names:
        if n in shard:
            out[n] = tuple(t(r) for r in stepped[n]) if n in ("w_in", "w_gate", "w_up") else stepped[n]
        else:
            w, m, v = params[n]
            two_d = (-1, w.shape[-1])
            d, mn, vn = adamw(w.reshape(two_d), gs[n].reshape(two_d), m.reshape(two_d), v.reshape(two_d), f"adamw_{n}")
            out[n] = (gs[n].reshape(w.shape), d.reshape(w.shape), mn.reshape(w.shape), vn.reshape(w.shape))
    return (loss, grad_x, *[out[n][i] for i in range(4) for n in names])
```

```python
import jax
import jax.numpy as jnp
from jax import lax
from jax.experimental import pallas as pl
from jax.experimental.pallas import tpu as pltpu

F32 = jnp.float32
BF16 = jnp.bfloat16

D_MODEL = 1024
CONV_CH = 512
ATTN_W = 512
N_Q = 8
N_KV = 2
GRP = N_Q // N_KV
HEAD = 64
IN_COLS = 2304
D_FF = 2816
BLK = 128
O_Q = 3 * CONV_CH
O_K = O_Q + ATTN_W
O_V = O_K + N_KV * HEAD
EPS = 1e-6
NEG_INF = -1e30
SCALE = HEAD ** -0.5
N_DEV = 8
DEPTH = 2

ADAM_LR = 0.001
ADAM_B1 = 0.9
ADAM_B2 = 0.999
ADAM_EPS = 1e-08
ADAM_WD = 0.01
ADAM_STEP = 10

VMEM_LIMIT = 56 * 1024 * 1024
TM = 512
MESH_T = pl.DeviceIdType.MESH

ANY = pl.BlockSpec(memory_space=pl.ANY)
HBM = pl.BlockSpec(memory_space=pltpu.HBM)
SEM = pl.BlockSpec(memory_space=pltpu.SEMAPHORE)


def _cparams(sem):
    return pltpu.CompilerParams(dimension_semantics=sem, vmem_limit_bytes=VMEM_LIMIT)


def _pallas_after(dep, body, *args, in_specs, **kw):
    if dep is None:
        return pl.pallas_call(body, in_specs=in_specs, **kw)(*args)

    def after_dep(dep_ref, *refs):
        body(*refs)

    return pl.pallas_call(after_dep, in_specs=[ANY, *in_specs], **kw)(dep, *args)


def _dot(a, b):
    return jnp.dot(a, b, preferred_element_type=F32)


def _dot_nt(a, b):
    return lax.dot_general(a, b, (((1,), (1,)), ((), ())), preferred_element_type=F32)


def _dot_tn(a, b):
    return lax.dot_general(a, b, (((0,), (0,)), ((), ())), preferred_element_type=F32)


def _row_sum(v):
    return jnp.sum(v, axis=-1, keepdims=True)


def _row_mean(v):
    return _row_sum(v) * (1.0 / v.shape[-1])


def _rms(v):
    return lax.rsqrt(_row_mean(v * v) + EPS)


def _sigmoid(v):
    return 1.0 / (1.0 + jnp.exp(-v))


def _rms_bwd(dyv, xh, r, g):
    dxh = dyv * g
    return r * (dxh - xh * _row_mean(dxh * xh))


def norm_proj(x, g, w_t, name, dep=None):
    S, N = x.shape[0], w_t.shape[0]
    tm = 2 * TM

    def body(x_ref, g_ref, w_ref, h_ref, p_ref):
        xf = x_ref[...]
        h = ((xf * _rms(xf)) * g_ref[...]).astype(BF16)
        h_ref[...] = h
        p_ref[...] = _dot_nt(h, w_ref[...]).astype(BF16)

    return _pallas_after(
        dep, body, x, g, w_t, name=name, grid=(S // tm,),
        in_specs=[pl.BlockSpec((tm, D_MODEL), lambda i: (i, 0)),
                  pl.BlockSpec((1, D_MODEL), lambda i: (0, 0)),
                  pl.BlockSpec((N, D_MODEL), lambda i: (0, 0))],
        out_specs=[pl.BlockSpec((tm, D_MODEL), lambda i: (i, 0)),
                   pl.BlockSpec((tm, N), lambda i: (i, 0))],
        out_shape=[jax.ShapeDtypeStruct((S, D_MODEL), BF16), jax.ShapeDtypeStruct((S, N), BF16)],
        compiler_params=_cparams(("parallel",)),
    )


def band_bias():
    qi = lax.broadcasted_iota(jnp.int32, (BLK, 2 * BLK), 0)
    kj = lax.broadcasted_iota(jnp.int32, (BLK, 2 * BLK), 1)
    diff = qi + BLK - kj
    valid = (diff >= 0) & (diff < BLK)
    return jnp.stack([jnp.where(valid & (kj >= BLK), 0.0, NEG_INF), jnp.where(valid, 0.0, NEG_INF)]).astype(F32)


def _softmax_with_sink(s, sink):
    m = jnp.maximum(jnp.max(s, axis=-1, keepdims=True), sink)
    p = jnp.exp(s - m)
    es = jnp.exp(sink - m)
    inv = 1.0 / (_row_sum(p) + es)
    return p * inv, es * inv


PAIR = 2 * HEAD


def _low_half():
    return lax.broadcasted_iota(jnp.int32, (1, PAIR), 1) < HEAD


def _half_sums(v, low):
    return (jnp.sum(jnp.where(low, v, 0.0), axis=-1, keepdims=True),
            jnp.sum(jnp.where(low, 0.0, v), axis=-1, keepdims=True))


def _pair_mean(v, low):
    e, o = _half_sums(v, low)
    return jnp.where(low, e, o) * (1.0 / HEAD)


def _pair_rms(v, low):
    return lax.rsqrt(_pair_mean(v * v, low) + EPS)


def _one_head_in_both_halves(pair, low):
    swapped = pltpu.roll(pair, HEAD, 1)
    return jnp.where(low, pair, swapped), jnp.where(low, swapped, pair)


SUB_FWD, SUB_FWD_PROJECT, SUB_BWD = 2, 4, 4


def _kv_pairs(p_ref, kvp_ref):
    k = jnp.concatenate([kvp_ref[:, 0:PAIR], p_ref[:, O_K:O_K + PAIR]], axis=0).astype(F32)
    v = jnp.concatenate([kvp_ref[:, PAIR:2 * PAIR], p_ref[:, O_V:O_V + PAIR]], axis=0)
    return k, v


def _band(s):
    return slice(BLK * s, BLK * s + 2 * BLK)


def _block(s):
    return slice(BLK * s, BLK * (s + 1))


def mixer_fwd(proj, bias, sinks, conv_w, qg, kg, cog, aog, name, residual=None, w_o=None):
    S = proj.shape[0]
    project = w_o is not None
    SUB = SUB_FWD_PROJECT if project else SUB_FWD
    ROWS = SUB * BLK
    steps = S // ROWS

    def body(sinks_ref, p_ref, kvp_ref, bias_ref, cw_ref, qg_ref, kg_ref, cog_ref, aog_ref, *rest):
        if project:
            x_ref, wo_ref, mix_ref, y_ref, ao_ref, x1_ref, ucar = rest
        else:
            mix_ref, y_ref, ao_ref, ucar = rest
        n = pl.program_id(0)

        @pl.when(n == 0)
        def _():
            ucar[...] = jnp.zeros_like(ucar)

        bg = p_ref[:, 0:CONV_CH].astype(F32)
        u = p_ref[:, CONV_CH:2 * CONV_CH].astype(F32) * p_ref[:, 2 * CONV_CH:3 * CONV_CH].astype(F32)
        row = lax.broadcasted_iota(jnp.int32, (ROWS, CONV_CH), 0)
        prev = ucar[...]
        u1 = jnp.where(row == 0, prev[7:8, :], pltpu.roll(u, 1, 0))
        u2 = jnp.where(row == 0, prev[6:7, :], jnp.where(row == 1, prev[7:8, :], pltpu.roll(u, 2, 0)))
        ucar[...] = u[ROWS - 8:ROWS, :]
        y = cw_ref[0:1, :] * u2 + cw_ref[1:2, :] * u1 + cw_ref[2:3, :] * u
        y_ref[...] = y.astype(BF16)
        co = bg * y
        mix_conv = ((co * _rms(co)) * cog_ref[...]).astype(BF16)
        mix_ref[:, 0:CONV_CH] = mix_conv
        if project:
            x1 = x_ref[...] + _dot(mix_conv, wo_ref[0:CONV_CH, :])

        low = _low_half()
        q_gain = qg_ref[...] * SCALE
        k, v = _kv_pairs(p_ref, kvp_ref)
        kn = ((k * _pair_rms(k, low)) * kg_ref[...]).astype(BF16)
        k_of = _one_head_in_both_halves(kn, low)
        v_of = _one_head_in_both_halves(v, low)
        biases = [bias_ref[jnp.minimum(n, 1)]] + [bias_ref[1]] * (SUB - 1)
        units = [(s, hq) for s in range(SUB) for hq in range(N_Q)]
        q_one = []
        for pair in range(N_Q // 2):
            q = p_ref[:, O_Q + PAIR * pair:O_Q + PAIR * (pair + 1)].astype(F32)
            qn = ((q * _pair_rms(q, low)) * q_gain).astype(BF16)
            q_one += [jnp.where(mine, qn, jnp.zeros_like(qn)) for mine in (low, jnp.logical_not(low))]
        scores = [_dot_nt(q_one[hq][_block(s)], k_of[hq // GRP][_band(s)]) + biases[s] for s, hq in units]
        probs = [_softmax_with_sink(sc, sinks_ref[hq])[0].astype(BF16) for (s, hq), sc in zip(units, scores)]
        o = {u_: _dot(pn, v_of[u_[1] // GRP][_band(u_[0])]) for u_, pn in zip(units, probs)}
        ao = jnp.concatenate(
            [jnp.concatenate([jnp.where(low, o[s, 2 * pair], o[s, 2 * pair + 1]) for pair in range(N_Q // 2)], axis=1)
             for s in range(SUB)], axis=0)
        ao_ref[...] = ao.astype(BF16)
        mix_attn = ((ao * _rms(ao)) * aog_ref[...]).astype(BF16)
        mix_ref[:, CONV_CH:] = mix_attn
        if project:
            x1_ref[...] = x1 + _dot(mix_attn, wo_ref[CONV_CH:, :])

    small = lambda shape: pl.BlockSpec(shape, lambda n: (0, 0))
    rows = lambda w: pl.BlockSpec((ROWS, w), lambda n: (n, 0))
    return pl.pallas_call(
        body, name=name, grid=(steps,),
        in_specs=[pl.BlockSpec(memory_space=pltpu.SMEM),
                  rows(IN_COLS),
                  pl.BlockSpec((BLK, 2 * PAIR), lambda n: (jnp.maximum(SUB * n - 1, 0), O_K // (2 * PAIR))),
                  pl.BlockSpec((2, BLK, 2 * BLK), lambda n: (0, 0, 0)),
                  small((3, CONV_CH)), small((1, PAIR)), small((1, PAIR)), small((1, CONV_CH)), small((1, ATTN_W))]
        + ([rows(D_MODEL), small((D_MODEL, D_MODEL))] if project else []),
        out_specs=[rows(D_MODEL), rows(CONV_CH), rows(ATTN_W)] + ([rows(D_MODEL)] if project else []),
        out_shape=[jax.ShapeDtypeStruct((S, D_MODEL), BF16), jax.ShapeDtypeStruct((S, CONV_CH), BF16),
                   jax.ShapeDtypeStruct((S, ATTN_W), BF16)]
        + ([jax.ShapeDtypeStruct((S, D_MODEL), F32)] if project else []),
        scratch_shapes=[pltpu.VMEM((8, CONV_CH), F32)],
        compiler_params=_cparams(("arbitrary",)),
    )(sinks, proj, proj, bias, conv_w, qg, kg, cog, aog, *([residual, w_o] if project else []))


def matmul_residual(a, w, res, name, dep=None):
    S, K = a.shape
    N = w.shape[1]

    def body(a_ref, w_ref, r_ref, o_ref):
        o_ref[...] = r_ref[...] + _dot(a_ref[...], w_ref[...])

    return _pallas_after(
        dep, body, a, w, res, name=name, grid=(S // TM,),
        in_specs=[pl.BlockSpec((TM, K), lambda i: (i, 0)), pl.BlockSpec((K, N), lambda i: (0, 0)),
                  pl.BlockSpec((TM, N), lambda i: (i, 0))],
        out_specs=pl.BlockSpec((TM, N), lambda i: (i, 0)),
        out_shape=jax.ShapeDtypeStruct((S, N), F32),
        compiler_params=_cparams(("parallel",)),
    )


TF = 1408


def ffn_fwd(x1, g, wg, wu, wd, name, dep=None, tgt=None):
    S = x1.shape[0]
    ni = S // TM
    with_loss = tgt is not None
    row = pl.BlockSpec((TM, D_MODEL), lambda i: (i, 0))
    vec = pl.BlockSpec((1, D_MODEL), lambda i: (0, 0))
    acts = [jax.ShapeDtypeStruct((S, D_FF), BF16)] * 3

    def half_weights(j):
        return [pl.BlockSpec((TF, D_MODEL), lambda i: (j, 0))] * 3

    def swiglu(h2, wg_ref, wu_ref, a_ref, b_ref, f_ref):
        a = _dot_nt(h2, wg_ref[...])
        b = _dot_nt(h2, wu_ref[...])
        a_ref[...] = a.astype(BF16)
        b_ref[...] = b.astype(BF16)
        f = ((a * _sigmoid(a)) * b).astype(BF16)
        f_ref[...] = f
        return f

    def first(x_ref, g_ref, wg_ref, wu_ref, wd_ref, h2_ref, a_ref, b_ref, f_ref, part_ref):
        xf = x_ref[...]
        h2 = ((xf * _rms(xf)) * g_ref[...]).astype(BF16)
        h2_ref[...] = h2
        part_ref[...] = xf + _dot(swiglu(h2, wg_ref, wu_ref, a_ref, b_ref, f_ref), wd_ref[...])

    h2, a, b, f, part = _pallas_after(
        dep, first, x1, g, wg, wu, wd, name=name + "_a", grid=(ni,),
        in_specs=[row, vec] + half_weights(0),
        out_specs=[row] + [pl.BlockSpec((TM, TF), lambda i: (i, 0))] * 3 + [row],
        out_shape=[jax.ShapeDtypeStruct((S, D_MODEL), BF16)] + acts + [jax.ShapeDtypeStruct((S, D_MODEL), F32)],
        compiler_params=_cparams(("parallel",)),
    )

    def second(h2_ref, part_ref, wg_ref, wu_ref, wd_ref, *rest):
        rest = rest[3:]
        if with_loss:
            t_ref, a_ref, b_ref, f_ref, o_ref, l_ref, sq = rest
        else:
            a_ref, b_ref, f_ref, o_ref = rest
        i = pl.program_id(0)
        out = part_ref[...] + _dot(swiglu(h2_ref[...], wg_ref, wu_ref, a_ref, b_ref, f_ref), wd_ref[...])
        if not with_loss:
            o_ref[...] = out
            return
        e = out - t_ref[...]
        o_ref[...] = e * (1.0 / D_MODEL)
        col = jnp.sum(e * e, axis=0, keepdims=True)

        @pl.when(i == 0)
        def _():
            sq[...] = col

        @pl.when(i > 0)
        def _():
            sq[...] += col

        @pl.when(i == ni - 1)
        def _():
            l_ref[...] = jnp.full((1, 128), jnp.sum(sq[...]) * (0.5 / D_MODEL), F32)

    results = pl.pallas_call(
        second, name=name + "_b", grid=(ni,),
        in_specs=[row, row] + half_weights(1) + [ANY] * 3 + ([row] if with_loss else []),
        out_specs=[pl.BlockSpec((TM, TF), lambda i: (i, 1))] * 3 + [row]
        + ([pl.BlockSpec((1, 128), lambda i: (0, 0))] if with_loss else []),
        out_shape=acts + [jax.ShapeDtypeStruct((S, D_MODEL), F32)]
        + ([jax.ShapeDtypeStruct((1, 128), F32)] if with_loss else []),
        input_output_aliases={5: 0, 6: 1, 7: 2},
        scratch_shapes=[pltpu.VMEM((1, D_MODEL), F32)] if with_loss else [],
        compiler_params=_cparams(("arbitrary",) if with_loss else ("parallel",)),
    )(h2, part, wg, wu, wd, a, b, f, *([tgt] if with_loss else []))
    return (h2, *results)


def ffn_bwd(dx2, a, b, x1, g, wg, wu, wd, name, dep=None):
    S = dx2.shape[0]
    ni = S // TM
    row = pl.BlockSpec((TM, D_MODEL), lambda i: (i, 0))
    vec = pl.BlockSpec((1, D_MODEL), lambda i: (0, 0))
    acts = [jax.ShapeDtypeStruct((S, D_FF), BF16)] * 2

    def half(j):
        return [pl.BlockSpec((TM, TF), lambda i: (i, j))] * 2, [pl.BlockSpec((TF, D_MODEL), lambda i: (j, 0))] * 3

    def through_half(dx_ref, a_ref, b_ref, wg_ref, wu_ref, wd_ref, da_ref, db_ref):
        df = _dot_nt(dx_ref[...].astype(BF16), wd_ref[...])
        av = a_ref[...].astype(F32)
        bv = b_ref[...].astype(F32)
        sg = _sigmoid(av)
        da = ((df * bv) * (sg * (1.0 + av * (1.0 - sg)))).astype(BF16)
        db = (df * (av * sg)).astype(BF16)
        da_ref[...] = da
        db_ref[...] = db
        return _dot(da, wg_ref[...]) + _dot(db, wu_ref[...])

    def first(dx_ref, a_ref, b_ref, wg_ref, wu_ref, wd_ref, da_ref, db_ref, part_ref):
        part_ref[...] = through_half(dx_ref, a_ref, b_ref, wg_ref, wu_ref, wd_ref, da_ref, db_ref)

    tiles, weights = half(0)
    da, db, part = _pallas_after(
        dep, first, dx2, a, b, wg, wu, wd, name=name + "_a", grid=(ni,),
        in_specs=[row] + tiles + weights, out_specs=tiles + [row],
        out_shape=acts + [jax.ShapeDtypeStruct((S, D_MODEL), F32)],
        compiler_params=_cparams(("parallel",)),
    )

    def second(dx_ref, a_ref, b_ref, wg_ref, wu_ref, wd_ref, part_ref, x_ref, g_ref, da_prev, db_prev,
               da_ref, db_ref, dx1_ref, dg_ref):
        i = pl.program_id(0)

        @pl.when(i == 0)
        def _():
            dg_ref[...] = jnp.zeros_like(dg_ref)

        dh = part_ref[...] + through_half(dx_ref, a_ref, b_ref, wg_ref, wu_ref, wd_ref, da_ref, db_ref)
        xf = x_ref[...]
        r = _rms(xf)
        xh = xf * r
        dg_ref[...] += jnp.sum(dh * xh, axis=0, keepdims=True)
        dx1_ref[...] = dx_ref[...] + _rms_bwd(dh, xh, r, g_ref[...])

    tiles, weights = half(1)
    return pl.pallas_call(
        second, name=name + "_b", grid=(ni,),
        in_specs=[row] + tiles + weights + [row, row, vec, ANY, ANY],
        out_specs=tiles + [row, vec],
        out_shape=acts + [jax.ShapeDtypeStruct((S, D_MODEL), F32), jax.ShapeDtypeStruct((1, D_MODEL), F32)],
        input_output_aliases={9: 0, 10: 1},
        compiler_params=_cparams(("arbitrary",)),
    )(dx2, a, b, wg, wu, wd, part, x1, g, da, db)


def grad_weight(lhs, rhs, tm, name, tk=1024):
    S, N = rhs.shape
    M = lhs[0].shape[1]
    tk = min(tk, S)
    nk = S // tk
    n_l = len(lhs)

    def body(*refs):
        l_refs, r_ref = refs[:n_l], refs[n_l]
        o_refs, accs = refs[n_l + 1:2 * n_l + 1], refs[2 * n_l + 1:]
        k = pl.program_id(1)
        rv = r_ref[...].astype(BF16)
        for l_ref, o_ref, acc in zip(l_refs, o_refs, accs):
            @pl.when(k == 0)
            def _():
                acc[...] = jnp.zeros_like(acc)

            acc[...] += _dot_tn(l_ref[...], rv)

            @pl.when(k == nk - 1)
            def _():
                o_ref[...] = acc[...].astype(BF16)

    return pl.pallas_call(
        body, name=name, grid=(M // tm, nk),
        in_specs=[pl.BlockSpec((tk, tm), lambda i, k: (k, i))] * n_l + [pl.BlockSpec((tk, N), lambda i, k: (k, 0))],
        out_specs=[pl.BlockSpec((tm, N), lambda i, k: (i, 0))] * n_l,
        out_shape=[jax.ShapeDtypeStruct((M, N), BF16)] * n_l,
        scratch_shapes=[pltpu.VMEM((tm, N), F32)] * n_l,
        compiler_params=_cparams(("parallel", "arbitrary")),
    )(*lhs, rhs)


def proj_bwd(dproj, w, x, g, dres, name, dep=None):
    S, N = dproj.shape

    def body(dp_ref, w_ref, x_ref, g_ref, dr_ref, dx_ref, dg_ref):
        i = pl.program_id(0)

        @pl.when(i == 0)
        def _():
            dg_ref[...] = jnp.zeros_like(dg_ref)

        dh = _dot(dp_ref[...], w_ref[...])
        xf = x_ref[...]
        r = _rms(xf)
        xh = xf * r
        dg_ref[...] += jnp.sum(dh * xh, axis=0, keepdims=True)
        dx_ref[...] = dr_ref[...] + _rms_bwd(dh, xh, r, g_ref[...])

    return _pallas_after(
        dep, body, dproj, w, x, g, dres, name=name, grid=(S // TM,),
        in_specs=[pl.BlockSpec((TM, N), lambda i: (i, 0)), pl.BlockSpec((N, D_MODEL), lambda i: (0, 0)),
                  pl.BlockSpec((TM, D_MODEL), lambda i: (i, 0)), pl.BlockSpec((1, D_MODEL), lambda i: (0, 0)),
                  pl.BlockSpec((TM, D_MODEL), lambda i: (i, 0))],
        out_specs=[pl.BlockSpec((TM, D_MODEL), lambda i: (i, 0)), pl.BlockSpec((1, D_MODEL), lambda i: (0, 0))],
        out_shape=[jax.ShapeDtypeStruct((S, D_MODEL), F32), jax.ShapeDtypeStruct((1, D_MODEL), F32)],
        compiler_params=_cparams(("arbitrary",)),
    )


def mixer_bwd(proj, bias, y, ao, dx1, w_o, sinks, conv_w, qg, kg, cog, aog, name, dep=None):
    S = proj.shape[0]
    SUB, ROWS = SUB_BWD, SUB_BWD * BLK
    steps = S // ROWS
    KV_W = 2 * PAIR

    def body(sinks_ref, p_ref, kvp_ref, bias_ref, y_ref, ao_ref, dx_first_ref, dx_next_ref, wo_ref, cw_ref, qg_ref,
             kg_ref, cog_ref, aog_ref, dp_ref, dcw_ref, dqg_ref, dkg_ref, dsk_ref, dcog_ref, daog_ref,
             dycar, kcar, vcar, dmix_scr):
        step = pl.program_id(0)

        @pl.when(step == 0)
        def _():
            dmix_scr[...] = _dot_nt(dx_first_ref[...].astype(BF16), wo_ref[...])
            dycar[...] = jnp.zeros_like(dycar)
            kcar[...] = jnp.zeros_like(kcar)
            vcar[...] = jnp.zeros_like(vcar)
            dcw_ref[...] = jnp.zeros_like(dcw_ref)
            dqg_ref[...] = jnp.zeros_like(dqg_ref)
            dkg_ref[...] = jnp.zeros_like(dkg_ref)
            dsk_ref[...] = jnp.zeros_like(dsk_ref)
            dcog_ref[...] = jnp.zeros_like(dcog_ref)
            daog_ref[...] = jnp.zeros_like(daog_ref)

        dma = dmix_scr[:, CONV_CH:]

        aov = ao_ref[...].astype(F32)
        ra = _rms(aov)
        ah = aov * ra
        daog_ref[...] += jnp.sum(dma * ah, axis=0, keepdims=True)
        dao = _rms_bwd(dma, ah, ra, aog_ref[...])

        low = _low_half()
        high = jnp.logical_not(low)
        q_gain = qg_ref[...] * SCALE
        k, v = _kv_pairs(p_ref, kvp_ref)
        rk = _pair_rms(k, low)
        kh = k * rk
        k_of = _one_head_in_both_halves((kh * kg_ref[...]).astype(BF16), low)
        v_of = _one_head_in_both_halves(v, low)
        lane8 = lax.broadcasted_iota(jnp.int32, (1, N_Q), 1)
        dsk = jnp.zeros((1, N_Q), F32)
        dqg = jnp.zeros((1, PAIR), F32)
        biases = [bias_ref[jnp.minimum(steps - 1 - step, 1)]] + [bias_ref[1]] * (SUB - 1)
        units = [(s, hq) for s in range(SUB) for hq in range(N_Q)]
        rq, qh, q_one, do_one, delta = [], [], [], [], []
        for pair in range(N_Q // 2):
            cols = slice(PAIR * pair, PAIR * (pair + 1))
            q = p_ref[:, O_Q + PAIR * pair:O_Q + PAIR * (pair + 1)].astype(F32)
            rq.append(_pair_rms(q, low))
            qh.append(q * rq[pair])
            qn = (qh[pair] * q_gain).astype(BF16)
            do = dao[:, cols]
            do_b = do.astype(BF16)
            delta += _half_sums(do * aov[:, cols], low)
            for mine in (low, high):
                q_one.append(jnp.where(mine, qn, jnp.zeros_like(qn)))
                do_one.append(jnp.where(mine, do_b, jnp.zeros_like(do_b)))
        scores = {u: _dot_nt(q_one[u[1]][_block(u[0])], k_of[u[1] // GRP][_band(u[0])]) + biases[u[0]] for u in units}
        dprobs = {u: _dot_nt(do_one[u[1]][_block(u[0])], v_of[u[1] // GRP][_band(u[0])]) for u in units}
        probs = {u: _softmax_with_sink(scores[u], sinks_ref[u[1]]) for u in units}
        ds = {u: (probs[u][0] * (dprobs[u] - delta[u[1]][_block(u[0])])).astype(BF16) for u in units}
        dv_t = {(s, h): jnp.zeros((PAIR, 2 * BLK), F32) for s in range(SUB) for h in range(N_KV)}
        dkn_t = dict(dv_t)
        for s, hq in units:
            dsk = dsk - jnp.where(lane8 == hq, jnp.sum(probs[s, hq][1] * delta[hq][_block(s)]), 0.0)
            dv_t[s, hq // GRP] = dv_t[s, hq // GRP] + _dot_tn(do_one[hq][_block(s)], probs[s, hq][0].astype(BF16))
            dkn_t[s, hq // GRP] = dkn_t[s, hq // GRP] + _dot_tn(q_one[hq][_block(s)], ds[s, hq])
        dqn_of = {u: _dot(ds[u], k_of[u[1] // GRP][_band(u[0])]) for u in units}
        for pair in range(N_Q // 2):
            dqn = jnp.concatenate([jnp.where(low, dqn_of[s, 2 * pair], dqn_of[s, 2 * pair + 1]) for s in range(SUB)],
                                  axis=0)
            dqg = dqg + jnp.sum(dqn * qh[pair], axis=0, keepdims=True)
            dqh = dqn * q_gain
            dp_ref[:, O_Q + PAIR * pair:O_Q + PAIR * (pair + 1)] = (
                rq[pair] * (dqh - qh[pair] * _pair_mean(dqh * qh[pair], low))).astype(BF16)

        def over_key_rows(parts):
            bands = [jnp.concatenate([parts[s, h][:HEAD] + parts[s, h][HEAD:] for h in range(N_KV)], axis=0).T
                     for s in range(SUB)]
            pieces = [bands[0][:BLK]]
            pieces += [bands[s - 1][BLK:] + bands[s][:BLK] for s in range(1, SUB)]
            pieces.append(bands[SUB - 1][BLK:])
            return jnp.concatenate(pieces, axis=0)

        dv = over_key_rows(dv_t)
        dkn = over_key_rows(dkn_t)
        dkg_ref[...] += jnp.sum(dkn * kh, axis=0, keepdims=True)
        dkh = dkn * kg_ref[...]
        dk = rk * (dkh - kh * _pair_mean(dkh * kh, low))
        last = slice(ROWS, ROWS + BLK)
        dp_ref[:, O_K:O_V] = jnp.concatenate([dk[BLK:ROWS], dk[last] + kcar[...]], axis=0).astype(BF16)
        dp_ref[:, O_V:] = jnp.concatenate([dv[BLK:ROWS], dv[last] + vcar[...]], axis=0).astype(BF16)
        kcar[...] = dk[:BLK, :]
        vcar[...] = dv[:BLK, :]
        dsk_ref[...] += dsk
        dqg_ref[...] += dqg * SCALE

        bg = p_ref[:, 0:CONV_CH].astype(F32)
        cg = p_ref[:, CONV_CH:2 * CONV_CH].astype(F32)
        hc = p_ref[:, 2 * CONV_CH:3 * CONV_CH].astype(F32)
        yv = y_ref[...].astype(F32)
        dmc = dmix_scr[:, 0:CONV_CH]
        co = bg * yv
        rc = _rms(co)
        ch = co * rc
        dcog_ref[...] += jnp.sum(dmc * ch, axis=0, keepdims=True)
        dco = _rms_bwd(dmc, ch, rc, cog_ref[...])
        dp_ref[:, 0:CONV_CH] = (dco * yv).astype(BF16)
        dy = dco * bg
        row = lax.broadcasted_iota(jnp.int32, (ROWS, CONV_CH), 0)
        nxt = dycar[...]
        dy1 = jnp.where(row == ROWS - 1, nxt[0:1, :], pltpu.roll(dy, ROWS - 1, 0))
        dy2 = jnp.where(row == ROWS - 2, nxt[0:1, :],
                        jnp.where(row == ROWS - 1, nxt[1:2, :], pltpu.roll(dy, ROWS - 2, 0)))
        dycar[...] = dy[0:8, :]
        du = cw_ref[2:3, :] * dy + cw_ref[1:2, :] * dy1 + cw_ref[0:1, :] * dy2
        dp_ref[:, CONV_CH:2 * CONV_CH] = (du * hc).astype(BF16)
        dp_ref[:, 2 * CONV_CH:3 * CONV_CH] = (du * cg).astype(BF16)
        u = cg * hc
        dcw_ref[0:1, :] += jnp.sum(dy2 * u, axis=0, keepdims=True)
        dcw_ref[1:2, :] += jnp.sum(dy1 * u, axis=0, keepdims=True)
        dcw_ref[2:3, :] += jnp.sum(dy * u, axis=0, keepdims=True)

        dmix_scr[...] = _dot_nt(dx_next_ref[...].astype(BF16), wo_ref[...])

    small = lambda shape: pl.BlockSpec(shape, lambda s: (0, 0))
    blk = lambda w: pl.BlockSpec((ROWS, w), lambda s: (steps - 1 - s, 0))
    return _pallas_after(
        dep, body, sinks, proj, proj, bias, y, ao, dx1, dx1, w_o, conv_w, qg, kg, cog, aog, name=name, grid=(steps,),
        in_specs=[pl.BlockSpec(memory_space=pltpu.SMEM),
                  blk(IN_COLS),
                  pl.BlockSpec((BLK, KV_W), lambda s: (jnp.maximum(SUB * (steps - 1 - s) - 1, 0), O_K // KV_W)),
                  pl.BlockSpec((2, BLK, 2 * BLK), lambda s: (0, 0, 0)),
                  blk(CONV_CH), blk(ATTN_W),
                  pl.BlockSpec((ROWS, D_MODEL), lambda s: (steps - 1, 0)),
                  pl.BlockSpec((ROWS, D_MODEL), lambda s: (jnp.maximum(steps - 2 - s, 0), 0)),
                  small((D_MODEL, D_MODEL)),
                  small((3, CONV_CH)), small((1, PAIR)), small((1, PAIR)), small((1, CONV_CH)), small((1, ATTN_W))],
        out_specs=[blk(IN_COLS), small((3, CONV_CH)), small((1, PAIR)), small((1, PAIR)), small((1, N_Q)),
                   small((1, CONV_CH)), small((1, ATTN_W))],
        out_shape=[jax.ShapeDtypeStruct((S, IN_COLS), BF16), jax.ShapeDtypeStruct((3, CONV_CH), F32),
                   jax.ShapeDtypeStruct((1, PAIR), F32), jax.ShapeDtypeStruct((1, PAIR), F32),
                   jax.ShapeDtypeStruct((1, N_Q), F32), jax.ShapeDtypeStruct((1, CONV_CH), F32),
                   jax.ShapeDtypeStruct((1, ATTN_W), F32)],
        scratch_shapes=[pltpu.VMEM((8, CONV_CH), F32), pltpu.VMEM((BLK, PAIR), F32), pltpu.VMEM((BLK, PAIR), F32),
                        pltpu.VMEM((ROWS, D_MODEL), F32)],
        compiler_params=_cparams(("arbitrary",)),
    )


OTHER_CHIPS = ((1, 0), (0, 1), (1, 1))


def _position():
    return lax.axis_index("x"), lax.axis_index("y"), lax.axis_index("c")


def all_gather(shards, name):
    n = len(shards)

    def body(*refs):
        x_refs, out_refs = refs[:n], refs[n:2 * n]
        send_sems, recv_sems, local_sems = refs[2 * n:]
        x, y, c = _position()
        me, sibling = (x, y, c), (x, y, 1 - c)
        chips = [(x ^ mx, y ^ my) for mx, my in OTHER_CHIPS]

        def slab(a, px, py, pc):
            return out_refs[a].at[4 * px + 2 * py + pc]

        def copy(a, k, block, to, src=None):
            return pltpu.make_async_remote_copy(
                src_ref=slab(a, *block) if src is None else src, dst_ref=slab(a, *block),
                send_sem=send_sems.at[7 * a + k], recv_sem=recv_sems.at[7 * a + k], device_id=to, device_id_type=MESH_T)

        mine = [pltpu.make_async_copy(x_refs[a], slab(a, *me), local_sems.at[a]) for a in range(n)]
        for cp in mine:
            cp.start()
        first = []
        for a in range(n):
            first.append(copy(a, 0, me, sibling, src=x_refs[a]))
            first += [copy(a, 1 + j, me, (*chip, c), src=x_refs[a]) for j, chip in enumerate(chips)]
        for cp in first:
            cp.start()
        passed = []
        for a in range(n):
            for j, chip in enumerate(chips):
                copy(a, 1 + j, (*chip, c), me).wait_recv()
                passed.append(copy(a, 4 + j, (*chip, c), sibling))
                passed[-1].start()
        for a in range(n):
            copy(a, 0, sibling, me).wait_recv()
            for j, chip in enumerate(chips):
                copy(a, 4 + j, (*chip, 1 - c), me).wait_recv()
        for cp in first + passed:
            cp.wait_send()
        for cp in mine:
            cp.wait()

    return pl.pallas_call(
        body, name=name, in_specs=[ANY] * n, out_specs=[ANY] * n,
        out_shape=[jax.ShapeDtypeStruct((N_DEV, *s.shape), s.dtype) for s in shards],
        scratch_shapes=[pltpu.SemaphoreType.DMA((7 * n,)), pltpu.SemaphoreType.DMA((7 * n,)),
                        pltpu.SemaphoreType.DMA((n,))],
    )(*shards)


class SplitCopy:
    def __init__(self, name, arrays, n_copies, plan, after=None):
        n = len(arrays)
        self.name, self.n, self.n_copies, self.plan = name, n, n_copies, plan
        extra = [] if after is None else [after]

        def body(*refs):
            in_refs = refs[:n]
            send_sems, recv_sems = refs[n + len(extra)], refs[n + len(extra) + 1]
            token = refs[2 * n + len(extra) + 2]
            for k, (src, dst, to) in enumerate(plan(_position(), in_refs)):
                pltpu.make_async_remote_copy(src_ref=src, dst_ref=dst, send_sem=send_sems.at[k],
                                             recv_sem=recv_sems.at[k], device_id=to, device_id_type=MESH_T).start()
            token[...] = jnp.zeros_like(token)

        outs = pl.pallas_call(
            body, name=name + "_start",
            out_shape=(pltpu.SemaphoreType.DMA((n_copies,)), pltpu.SemaphoreType.DMA((n_copies,)),
                       *[pltpu.HBM(a.shape, a.dtype) for a in arrays], jax.ShapeDtypeStruct((8, 128), F32)),
            in_specs=[HBM] * n + [ANY] * len(extra),
            out_specs=(SEM, SEM, *[HBM] * n, pl.BlockSpec(memory_space=pltpu.VMEM)),
            input_output_aliases={i: 2 + i for i in range(n)},
            compiler_params=pltpu.CompilerParams(has_side_effects=pltpu.SideEffectType.DATAFLOW_SIDE_EFFECTING),
        )(*[pltpu.with_memory_space_constraint(a, pltpu.HBM) for a in arrays], *extra)
        self.send_sems, self.recv_sems = outs[0], outs[1]
        self.arrays, self.token = list(outs[2:2 + n]), outs[2 + n]

    def wait(self, after):
        n, plan = self.n, self.plan

        def body(*refs):
            in_refs, send_sems, recv_sems = refs[:n], refs[n], refs[n + 1]
            for k, (src, dst, to) in enumerate(plan(_position(), in_refs)):
                cp = pltpu.make_async_remote_copy(src_ref=src, dst_ref=dst, send_sem=send_sems.at[k],
                                                  recv_sem=recv_sems.at[k], device_id=to, device_id_type=MESH_T)
                cp.wait_send()
                cp.wait_recv()

        outs = pl.pallas_call(
            body, name=self.name + "_wait",
            out_shape=tuple(pltpu.HBM(a.shape, a.dtype) for a in self.arrays),
            in_specs=[HBM] * n + [SEM, SEM, ANY], out_specs=tuple([HBM] * n),
            input_output_aliases={i: i for i in range(n)},
            compiler_params=pltpu.CompilerParams(has_side_effects=pltpu.SideEffectType.DATAFLOW_SIDE_EFFECTING),
        )(*self.arrays, self.send_sems, self.recv_sems, after)
        return list(outs)


def gather_start(shards, me, name, after=None):
    n = len(shards)
    lands = [lax.dynamic_update_slice(lax.empty((N_DEV, *s.shape), s.dtype), s[None], (me, 0, 0)) for s in shards]

    def plan(pos, refs):
        x, y, c = pos
        return [(refs[a], refs[n + a].at[4 * x + 2 * y + c], (x ^ mx, y ^ my, c))
                for a in range(n) for mx, my in OTHER_CHIPS]

    return SplitCopy(name, list(shards) + lands, 3 * n, plan, after=after)


def sibling_start(lands, name):
    n = len(lands)

    def plan(pos, refs):
        x, y, c = pos
        return [(refs[a].at[2 * q + c], refs[a].at[2 * q + c], (x, y, 1 - c)) for a in range(n) for q in range(4)]

    return SplitCopy(name, list(lands), 4 * n, plan)


def scatter_start(slabs, name):
    n = len(slabs)
    lands = [lax.empty((N_DEV - 1, *g.shape[1:]), g.dtype) for g in slabs]

    def plan(pos, refs):
        x, y, c = pos
        copies = []
        for a in range(n):
            for r in range(1, N_DEV):
                px, py, pc = x ^ ((r >> 2) & 1), y ^ ((r >> 1) & 1), c ^ (r & 1)
                copies.append((refs[a].at[4 * px + 2 * py + pc], refs[n + a].at[r - 1], (px, py, pc)))
        return copies

    return SplitCopy(name, list(slabs) + lands, (N_DEV - 1) * n, plan)


def all_reduce_small(v, name):
    R, W = v.shape

    def body(v_ref, o_ref, recv, send_sems, recv_sems):
        x, y, c = _position()
        me = 4 * x + 2 * y + c
        copies = []
        for r in range(1, N_DEV):
            to = (x ^ ((r >> 2) & 1), y ^ ((r >> 1) & 1), c ^ (r & 1))
            copies.append(pltpu.make_async_remote_copy(
                src_ref=v_ref, dst_ref=recv.at[me], send_sem=send_sems.at[r - 1], recv_sem=recv_sems.at[r - 1],
                device_id=to, device_id_type=MESH_T))
        for cp in copies:
            cp.start()
        recv[pl.ds(me, 1)] = v_ref[...][None]
        for cp in copies:
            cp.wait()
        acc = recv[0]
        for s in range(1, N_DEV):
            acc = acc + recv[s]
        o_ref[...] = acc

    return pl.pallas_call(
        body, name=name,
        in_specs=[pl.BlockSpec(memory_space=pltpu.VMEM)], out_specs=pl.BlockSpec(memory_space=pltpu.VMEM),
        out_shape=jax.ShapeDtypeStruct((R, W), F32),
        scratch_shapes=[pltpu.VMEM((N_DEV, R, W), F32), pltpu.SemaphoreType.DMA((N_DEV - 1,)),
                        pltpu.SemaphoreType.DMA((N_DEV - 1,))],
    )(v)


def _row_tile(rows):
    if rows <= 512:
        return rows
    return max(t for t in range(8, 513, 8) if rows % t == 0)


def _adamw_update(w, g, m, v):
    mn = ADAM_B1 * m + (1.0 - ADAM_B1) * g
    vn = ADAM_B2 * v + (1.0 - ADAM_B2) * (g * g)
    m_hat = mn / (1.0 - ADAM_B1 ** ADAM_STEP)
    v_hat = vn / (1.0 - ADAM_B2 ** ADAM_STEP)
    return -ADAM_LR * (m_hat / (jnp.sqrt(v_hat) + ADAM_EPS) + ADAM_WD * w), mn, vn


def adamw(w, g, m, v, name):
    R, W = w.shape
    tr = _row_tile(R)

    def body(w_ref, g_ref, m_ref, v_ref, d_ref, mo_ref, vo_ref):
        d_ref[...], mo_ref[...], vo_ref[...] = _adamw_update(w_ref[...], g_ref[...], m_ref[...], v_ref[...])

    spec = pl.BlockSpec((tr, W), lambda i: (i, 0))
    return pl.pallas_call(
        body, name=name, grid=(R // tr,), in_specs=[spec] * 4, out_specs=[spec] * 3,
        out_shape=[jax.ShapeDtypeStruct((R, W), F32)] * 3,
        compiler_params=_cparams(("parallel",)),
    )(w, g, m, v)


def reduce_adamw(slabs, land, w, m, v, layer, me_arr, name, others=None):
    _, R, W = slabs.shape
    tr = _row_tile(R)
    n_other = 0 if others is None else 4

    def body(me_ref, s_ref, l_ref, w_ref, m_ref, v_ref, *rest):
        g_ref, d_ref, mo_ref, vo_ref = rest[n_other:]
        g = s_ref[0].astype(F32)
        for r in range(N_DEV - 1):
            g = g + l_ref[r].astype(F32)
        g_ref[0] = g
        d_ref[0], mo_ref[0], vo_ref[0] = _adamw_update(w_ref[0], g, m_ref[0], v_ref[0])

    spec = pl.BlockSpec((1, tr, W), lambda i, me: (layer, i, 0))
    return pl.pallas_call(
        body, name=name,
        grid_spec=pltpu.PrefetchScalarGridSpec(
            num_scalar_prefetch=1, grid=(R // tr,),
            in_specs=[pl.BlockSpec((1, tr, W), lambda i, me: (me[0], i, 0)),
                      pl.BlockSpec((N_DEV - 1, tr, W), lambda i, me: (0, i, 0)), spec, spec, spec] + [ANY] * n_other,
            out_specs=[spec] * 4),
        out_shape=[jax.ShapeDtypeStruct((DEPTH, R, W), F32)] * 4,
        input_output_aliases={6 + i: i for i in range(n_other)},
        compiler_params=_cparams(("parallel",)),
    )(me_arr, slabs, land, w, m, v, *([] if others is None else others))


SMALL_NAMES = ("norm1_g", "q_norm_g", "k_norm_g", "sinks", "conv_out_g", "attn_out_g", "norm2_g", "conv_w")
SMALL_SIZES = (D_MODEL, HEAD, HEAD, N_Q, CONV_CH, ATTN_W, D_MODEL, 3 * CONV_CH)
SMALL_ROWS = 80


def kernel(x, norm1_g, w_in, conv_w, q_norm_g, k_norm_g, sinks, conv_out_g, attn_out_g, w_o, norm2_g, w_gate, w_up, w_down, loss_target, m_norm1_g, m_w_in, m_conv_w, m_q_norm_g, m_k_norm_g, m_sinks, m_conv_out_g, m_attn_out_g, m_w_o, m_norm2_g, m_w_gate, m_w_up, m_w_down, v_norm1_g, v_w_in, v_conv_w, v_q_norm_g, v_k_norm_g, v_sinks, v_conv_out_g, v_attn_out_g, v_w_o, v_norm2_g, v_w_gate, v_w_up, v_w_down):
    xi, yi, ci = _position()
    me = 4 * xi + 2 * yi + ci
    me_arr = jnp.reshape(me, (1,)).astype(jnp.int32)
    xs, tgt = x[0], loss_target[0]
    bf = lambda a: a.astype(BF16)
    bias = band_bias()

    t = lambda a: jnp.swapaxes(a, 1, 2)
    shard = dict(w_in=(t(w_in), t(m_w_in), t(v_w_in)), w_o=(w_o, m_w_o, v_w_o),
                 w_gate=(t(w_gate), t(m_w_gate), t(v_w_gate)), w_up=(t(w_up), t(m_w_up), t(v_w_up)),
                 w_down=(w_down, m_w_down, v_w_down))
    wb = {n: bf(shard[n][0]) for n in shard}

    g_in0, g_o0, g_conv = all_gather([wb["w_in"][0], wb["w_o"][0], conv_w.reshape(DEPTH * 3, HEAD)], "gather_first")
    ag_ffn0 = gather_start([wb["w_gate"][0], wb["w_up"][0], wb["w_down"][0]], me, "gather_ffn0", after=g_in0)
    ag_mix1 = gather_start([wb["w_in"][1], wb["w_o"][1]], me, "gather_mix1", after=ag_ffn0.token)
    ag_ffn1 = gather_start([wb["w_gate"][1], wb["w_up"][1], wb["w_down"][1]], me, "gather_ffn1", after=ag_mix1.token)
    conv_full = g_conv.reshape(N_DEV, DEPTH, 3, HEAD).transpose(1, 2, 0, 3).reshape(DEPTH, 3, CONV_CH)
    pair_gain = lambda g: jnp.tile(g[None], (1, 2))
    small = [dict(norm1_g=norm1_g[l][None], conv_w=conv_full[l], q_norm_g=pair_gain(q_norm_g[l]),
                  k_norm_g=pair_gain(k_norm_g[l]), sinks=sinks[l], conv_out_g=conv_out_g[l][None],
                  attn_out_g=attn_out_g[l][None], norm2_g=norm2_g[l][None]) for l in range(DEPTH)]
    whole = lambda g: g.reshape(-1, D_MODEL)
    weights = [dict(w_in=whole(g_in0), w_o=whole(g_o0)), {}]

    def ffn_weights(g_gate, g_up, g_down):
        return dict(w_gate=whole(g_gate), w_up=whole(g_up), w_down=whole(g_down))

    saved = []
    xl = xs
    for l in range(DEPTH):
        sp, wl = small[l], weights[l]
        h, proj = norm_proj(xl, sp["norm1_g"], wl["w_in"], f"norm_proj{l}",
                            dep=ag_ffn1.token if l == 0 else pass_ffn1.token)
        mixer_args = (proj, bias, sp["sinks"], sp["conv_w"], sp["q_norm_g"], sp["k_norm_g"], sp["conv_out_g"],
                      sp["attn_out_g"], f"mixer_fwd{l}")
        if l == 0:
            mix, y, ao = mixer_fwd(*mixer_args)
            pass_ffn0 = sibling_start(ag_ffn0.wait(mix)[3:], "pass_ffn0")
            x1 = matmul_residual(mix, wl["w_o"], xl, "out_proj0", dep=pass_ffn0.token)
            wl.update(ffn_weights(*pass_ffn0.wait(x1)))
            pass_mix1 = sibling_start(ag_mix1.wait(x1)[2:], "pass_mix1")
            h2, a, b, f, x2 = ffn_fwd(x1, sp["norm2_g"], wl["w_gate"], wl["w_up"], wl["w_down"], "ffn_fwd0",
                                      dep=pass_mix1.token)
            g_in, g_o = pass_mix1.wait(x2)
            weights[1] = dict(w_in=whole(g_in), w_o=whole(g_o))
            pass_ffn1 = sibling_start(ag_ffn1.wait(x2)[3:], "pass_ffn1")
        else:
            mix, y, ao, x1 = mixer_fwd(*mixer_args, residual=xl, w_o=wl["w_o"])
            wl.update(ffn_weights(*pass_ffn1.wait(x1)))
            h2, a, b, f, dx, loss_row = ffn_fwd(x1, sp["norm2_g"], wl["w_gate"], wl["w_up"], wl["w_down"], "ffn_fwd1",
                                                tgt=tgt)
            x2 = None
        saved.append((xl, h, proj, mix, y, ao, x1, h2, a, b, f))
        xl = x2

    stepped = {n: None for n in shard}
    slabs = lambda d: d.reshape(N_DEV, -1, D_MODEL)
    gsmall = [None] * DEPTH

    def finish(sc, names, after, l):
        arrays = sc.wait(after)
        k = len(names)
        for i, n in enumerate(names):
            w, m, v = shard[n]
            stepped[n] = reduce_adamw(arrays[i], arrays[k + i], w, m, v, l, me_arr, f"reduce_adamw_{n}{l}",
                                      others=stepped[n])

    for l in reversed(range(DEPTH)):
        sp, wl = small[l], weights[l]
        x0, h, proj, mix, y, ao, x1, h2, a, b, f = saved[l]
        (d_wd,) = grad_weight((f,), dx, TF, f"grad_w_down{l}", tk=2048)
        sc_down = scatter_start([slabs(d_wd)], f"scatter_w_down{l}")
        da, db, dx1, d_g2 = ffn_bwd(dx, a, b, x1, sp["norm2_g"], wl["w_gate"], wl["w_up"], wl["w_down"],
                                    f"ffn_bwd{l}", dep=sc_down.token)
        (d_wg,) = grad_weight((da,), h2, TF, f"grad_w_gate{l}", tk=2048)
        (d_wu,) = grad_weight((db,), h2, TF, f"grad_w_up{l}", tk=2048)
        (d_wo,) = grad_weight((mix,), dx1, D_MODEL, f"grad_w_o{l}", tk=2048)
        sc_rest = scatter_start([slabs(d_wg), slabs(d_wu), slabs(d_wo)], f"scatter_w_gate_up_o{l}")
        dproj, d_cw, d_qg, d_kg, d_sk, d_cog, d_aog = mixer_bwd(
            proj, bias, y, ao, dx1, wl["w_o"], sp["sinks"], sp["conv_w"], sp["q_norm_g"], sp["k_norm_g"],
            sp["conv_out_g"], sp["attn_out_g"], f"mixer_bwd{l}", dep=sc_rest.token)
        finish(sc_down, ["w_down"], dproj, l)
        finish(sc_rest, ["w_gate", "w_up", "w_o"], dproj, l)
        (d_win,) = grad_weight((dproj,), h, IN_COLS // 2, f"grad_w_in{l}", tk=2048)
        sc_in = scatter_start([slabs(d_win)], f"scatter_w_in{l}")
        dx, d_g1 = proj_bwd(dproj, wl["w_in"], x0, sp["norm1_g"], dx1, f"proj_bwd{l}", dep=sc_in.token)
        finish(sc_in, ["w_in"], dx, l)
        both_heads = lambda d: d[:, :HEAD] + d[:, HEAD:]
        gsmall[l] = dict(norm1_g=d_g1, conv_w=d_cw, q_norm_g=both_heads(d_qg), k_norm_g=both_heads(d_kg), sinks=d_sk,
                         conv_out_g=d_cog, attn_out_g=d_aog, norm2_g=d_g2)
    grad_x = dx[None]

    flat = jnp.concatenate([loss_row[0, 0:1]] + [gsmall[l][n].reshape(-1) for l in range(DEPTH) for n in SMALL_NAMES])
    flat = jnp.pad(flat, (0, SMALL_ROWS * 128 - flat.shape[0])).reshape(SMALL_ROWS, 128)
    flat = all_reduce_small(flat, "all_reduce_small_grads").reshape(-1)
    loss = flat[0]
    gs = {n: [] for n in SMALL_NAMES}
    off = 1
    for l in range(DEPTH):
        for n, size in zip(SMALL_NAMES, SMALL_SIZES):
            gs[n].append(flat[off:off + size])
            off += size
    gs = {n: jnp.stack(v) for n, v in gs.items()}
    g_conv = lax.dynamic_slice(gs["conv_w"].reshape(DEPTH, 3, CONV_CH), (0, 0, me * HEAD), (DEPTH, 3, HEAD))

    gs["conv_w"] = g_conv
    params = dict(norm1_g=(norm1_g, m_norm1_g, v_norm1_g), conv_w=(conv_w, m_conv_w, v_conv_w),
                  q_norm_g=(q_norm_g, m_q_norm_g, v_q_norm_g), k_norm_g=(k_norm_g, m_k_norm_g, v_k_norm_g),
                  sinks=(sinks, m_sinks, v_sinks), conv_out_g=(conv_out_g, m_conv_out_g, v_conv_out_g),
                  attn_out_g=(attn_out_g, m_attn_out_g, v_attn_out_g), norm2_g=(norm2_g, m_norm2_g, v_norm2_g))
    names = ("norm1_g", "w_in", "conv_w", "q_norm_g", "k_norm_g", "sinks", "conv_out_g", "attn_out_g", "w_o",
             "norm2_g", "w_gate", "w_up", "w_down")

    out = {}
    for n in names:
        if n in shard:
            out[n] = tuple(t(r) for r in stepped[n]) if n in ("w_in", "w_gate", "w_up") else stepped[n]
        else:
            w, m, v = params[n]
            two_d = (-1, w.shape[-1])
            d, mn, vn = adamw(w.reshape(two_d), gs[n].reshape(two_d), m.reshape(two_d), v.reshape(two_d), f"adamw_{n}")
            out[n] = (gs[n].reshape(w.shape), d.reshape(w.shape), mn.reshape(w.shape), vn.reshape(w.shape))
    return (loss, grad_x, *[out[n][i] for i in range(4) for n in names])
```

```python
import jax
import jax.numpy as jnp
from jax import lax
from jax.experimental import pallas as pl
from jax.experimental.pallas import tpu as pltpu

F32 = jnp.float32
BF16 = jnp.bfloat16

D_MODEL = 1024
CONV_CH = 512
ATTN_W = 512
N_Q = 8
N_KV = 2
GRP = N_Q // N_KV
HEAD = 64
IN_COLS = 2304
D_FF = 2816
BLK = 128
O_Q = 3 * CONV_CH
O_K = O_Q + ATTN_W
O_V = O_K + N_KV * HEAD
EPS = 1e-6
NEG_INF = -1e30
SCALE = HEAD ** -0.5
N_DEV = 8
DEPTH = 2

ADAM_LR = 0.001
ADAM_B1 = 0.9
ADAM_B2 = 0.999
ADAM_EPS = 1e-08
ADAM_WD = 0.01
ADAM_STEP = 10

VMEM_LIMIT = 56 * 1024 * 1024
TM = 512
MESH_T = pl.DeviceIdType.MESH

ANY = pl.BlockSpec(memory_space=pl.ANY)
HBM = pl.BlockSpec(memory_space=pltpu.HBM)
SEM = pl.BlockSpec(memory_space=pltpu.SEMAPHORE)


def _cparams(sem):
    return pltpu.CompilerParams(dimension_semantics=sem, vmem_limit_bytes=VMEM_LIMIT)


def _pallas_after(dep, body, *args, in_specs, **kw):
    if dep is None:
        return pl.pallas_call(body, in_specs=in_specs, **kw)(*args)

    def after_dep(dep_ref, *refs):
        body(*refs)

    return pl.pallas_call(after_dep, in_specs=[ANY, *in_specs], **kw)(dep, *args)


def _dot(a, b):
    return jnp.dot(a, b, preferred_element_type=F32)


def _dot_nt(a, b):
    return lax.dot_general(a, b, (((1,), (1,)), ((), ())), preferred_element_type=F32)


def _dot_tn(a, b):
    return lax.dot_general(a, b, (((0,), (0,)), ((), ())), preferred_element_type=F32)


LANES = 128


def _row_reduce(v, op, reduce):
    w = v.shape[-1]
    if w > LANES and w % LANES == 0:
        acc = v[:, 0:LANES]
        for c in range(1, w // LANES):
            acc = op(acc, v[:, LANES * c:LANES * (c + 1)])
        v = acc
    return reduce(v, axis=-1, keepdims=True)


def _row_sum(v):
    return _row_reduce(v, jnp.add, jnp.sum)


def _row_mean(v):
    return _row_sum(v) * (1.0 / v.shape[-1])


def _rms(v):
    return lax.rsqrt(_row_mean(v * v) + EPS)


def _sigmoid(v):
    return 1.0 / (1.0 + jnp.exp(-v))


def _rms_bwd(dyv, xh, r, g):
    dxh = dyv * g
    return r * (dxh - xh * _row_mean(dxh * xh))


def norm_proj(x, g, w_t, name, dep=None):
    S, N = x.shape[0], w_t.shape[0]
    tm = 2 * TM

    def body(x_ref, g_ref, w_ref, h_ref, p_ref):
        xf = x_ref[...]
        h = ((xf * _rms(xf)) * g_ref[...]).astype(BF16)
        h_ref[...] = h
        p_ref[...] = _dot_nt(h, w_ref[...]).astype(BF16)

    return _pallas_after(
        dep, body, x, g, w_t, name=name, grid=(S // tm,),
        in_specs=[pl.BlockSpec((tm, D_MODEL), lambda i: (i, 0)),
                  pl.BlockSpec((1, D_MODEL), lambda i: (0, 0)),
                  pl.BlockSpec((N, D_MODEL), lambda i: (0, 0))],
        out_specs=[pl.BlockSpec((tm, D_MODEL), lambda i: (i, 0)),
                   pl.BlockSpec((tm, N), lambda i: (i, 0))],
        out_shape=[jax.ShapeDtypeStruct((S, D_MODEL), BF16), jax.ShapeDtypeStruct((S, N), BF16)],
        compiler_params=_cparams(("parallel",)),
    )


def band_bias():
    qi = lax.broadcasted_iota(jnp.int32, (BLK, 2 * BLK), 0)
    kj = lax.broadcasted_iota(jnp.int32, (BLK, 2 * BLK), 1)
    diff = qi + BLK - kj
    valid = (diff >= 0) & (diff < BLK)
    return jnp.stack([jnp.where(valid & (kj >= BLK), 0.0, NEG_INF), jnp.where(valid, 0.0, NEG_INF)]).astype(F32)


def _softmax_with_sink(s, sink):
    m = jnp.maximum(_row_reduce(s, jnp.maximum, jnp.max), sink)
    p = jnp.exp(s - m)
    es = jnp.exp(sink - m)
    inv = 1.0 / (_row_sum(p) + es)
    return p * inv, es * inv


PAIR = 2 * HEAD


def _low_half():
    return lax.broadcasted_iota(jnp.int32, (1, PAIR), 1) < HEAD


def _half_sums(v, low):
    return (jnp.sum(jnp.where(low, v, 0.0), axis=-1, keepdims=True),
            jnp.sum(jnp.where(low, 0.0, v), axis=-1, keepdims=True))


def _pair_mean(v, low):
    e, o = _half_sums(v, low)
    return jnp.where(low, e, o) * (1.0 / HEAD)


def _pair_rms(v, low):
    return lax.rsqrt(_pair_mean(v * v, low) + EPS)


def _one_head_in_both_halves(pair, low):
    swapped = pltpu.roll(pair, HEAD, 1)
    return jnp.where(low, pair, swapped), jnp.where(low, swapped, pair)


SUB_FWD, SUB_FWD_PROJECT, SUB_BWD = 2, 4, 4


def _kv_pairs(p_ref, kvp_ref):
    k = jnp.concatenate([kvp_ref[:, 0:PAIR], p_ref[:, O_K:O_K + PAIR]], axis=0).astype(F32)
    v = jnp.concatenate([kvp_ref[:, PAIR:2 * PAIR], p_ref[:, O_V:O_V + PAIR]], axis=0)
    return k, v


def _band(s):
    return slice(BLK * s, BLK * s + 2 * BLK)


def _block(s):
    return slice(BLK * s, BLK * (s + 1))


def mixer_fwd(proj, bias, sinks, conv_w, qg, kg, cog, aog, name, residual=None, w_o=None):
    S = proj.shape[0]
    project = w_o is not None
    SUB = SUB_FWD_PROJECT if project else SUB_FWD
    ROWS = SUB * BLK
    steps = S // ROWS

    def body(sinks_ref, p_ref, kvp_ref, bias_ref, cw_ref, qg_ref, kg_ref, cog_ref, aog_ref, *rest):
        if project:
            x_ref, wo_ref, mix_ref, y_ref, ao_ref, x1_ref, ucar = rest
        else:
            mix_ref, y_ref, ao_ref, ucar = rest
        n = pl.program_id(0)

        @pl.when(n == 0)
        def _():
            ucar[...] = jnp.zeros_like(ucar)

        bg = p_ref[:, 0:CONV_CH].astype(F32)
        u = p_ref[:, CONV_CH:2 * CONV_CH].astype(F32) * p_ref[:, 2 * CONV_CH:3 * CONV_CH].astype(F32)
        row = lax.broadcasted_iota(jnp.int32, (ROWS, CONV_CH), 0)
        prev = ucar[...]
        u1 = jnp.where(row == 0, prev[7:8, :], pltpu.roll(u, 1, 0))
        u2 = jnp.where(row == 0, prev[6:7, :], jnp.where(row == 1, prev[7:8, :], pltpu.roll(u, 2, 0)))
        ucar[...] = u[ROWS - 8:ROWS, :]
        y = cw_ref[0:1, :] * u2 + cw_ref[1:2, :] * u1 + cw_ref[2:3, :] * u
        y_ref[...] = y.astype(BF16)
        co = bg * y
        mix_conv = ((co * _rms(co)) * cog_ref[...]).astype(BF16)
        mix_ref[:, 0:CONV_CH] = mix_conv
        if project:
            x1 = x_ref[...] + _dot(mix_conv, wo_ref[0:CONV_CH, :])

        low = _low_half()
        q_gain = qg_ref[...] * SCALE
        k, v = _kv_pairs(p_ref, kvp_ref)
        kn = ((k * _pair_rms(k, low)) * kg_ref[...]).astype(BF16)
        k_of = _one_head_in_both_halves(kn, low)
        v_of = _one_head_in_both_halves(v, low)
        biases = [bias_ref[jnp.minimum(n, 1)]] + [bias_ref[1]] * (SUB - 1)
        units = [(s, hq) for s in range(SUB) for hq in range(N_Q)]
        q_one = []
        for pair in range(N_Q // 2):
            q = p_ref[:, O_Q + PAIR * pair:O_Q + PAIR * (pair + 1)].astype(F32)
            qn = ((q * _pair_rms(q, low)) * q_gain).astype(BF16)
            q_one += [jnp.where(mine, qn, jnp.zeros_like(qn)) for mine in (low, jnp.logical_not(low))]
        scores = [_dot_nt(q_one[hq][_block(s)], k_of[hq // GRP][_band(s)]) + biases[s] for s, hq in units]
        probs = [_softmax_with_sink(sc, sinks_ref[hq])[0].astype(BF16) for (s, hq), sc in zip(units, scores)]
        o = {u_: _dot(pn, v_of[u_[1] // GRP][_band(u_[0])]) for u_, pn in zip(units, probs)}
        ao = jnp.concatenate(
            [jnp.concatenate([jnp.where(low, o[s, 2 * pair], o[s, 2 * pair + 1]) for pair in range(N_Q // 2)], axis=1)
             for s in range(SUB)], axis=0)
        ao_ref[...] = ao.astype(BF16)
        mix_attn = ((ao * _rms(ao)) * aog_ref[...]).astype(BF16)
        mix_ref[:, CONV_CH:] = mix_attn
        if project:
            x1_ref[...] = x1 + _dot(mix_attn, wo_ref[CONV_CH:, :])

    small = lambda shape: pl.BlockSpec(shape, lambda n: (0, 0))
    rows = lambda w: pl.BlockSpec((ROWS, w), lambda n: (n, 0))
    return pl.pallas_call(
        body, name=name, grid=(steps,),
        in_specs=[pl.BlockSpec(memory_space=pltpu.SMEM),
                  rows(IN_COLS),
                  pl.BlockSpec((BLK, 2 * PAIR), lambda n: (jnp.maximum(SUB * n - 1, 0), O_K // (2 * PAIR))),
                  pl.BlockSpec((2, BLK, 2 * BLK), lambda n: (0, 0, 0)),
                  small((3, CONV_CH)), small((1, PAIR)), small((1, PAIR)), small((1, CONV_CH)), small((1, ATTN_W))]
        + ([rows(D_MODEL), small((D_MODEL, D_MODEL))] if project else []),
        out_specs=[rows(D_MODEL), rows(CONV_CH), rows(ATTN_W)] + ([rows(D_MODEL)] if project else []),
        out_shape=[jax.ShapeDtypeStruct((S, D_MODEL), BF16), jax.ShapeDtypeStruct((S, CONV_CH), BF16),
                   jax.ShapeDtypeStruct((S, ATTN_W), BF16)]
        + ([jax.ShapeDtypeStruct((S, D_MODEL), F32)] if project else []),
        scratch_shapes=[pltpu.VMEM((8, CONV_CH), F32)],
        compiler_params=_cparams(("arbitrary",)),
    )(sinks, proj, proj, bias, conv_w, qg, kg, cog, aog, *([residual, w_o] if project else []))


def matmul_residual(a, w, res, name, dep=None):
    S, K = a.shape
    N = w.shape[1]

    def body(a_ref, w_ref, r_ref, o_ref):
        o_ref[...] = r_ref[...] + _dot(a_ref[...], w_ref[...])

    return _pallas_after(
        dep, body, a, w, res, name=name, grid=(S // TM,),
        in_specs=[pl.BlockSpec((TM, K), lambda i: (i, 0)), pl.BlockSpec((K, N), lambda i: (0, 0)),
                  pl.BlockSpec((TM, N), lambda i: (i, 0))],
        out_specs=pl.BlockSpec((TM, N), lambda i: (i, 0)),
        out_shape=jax.ShapeDtypeStruct((S, N), F32),
        compiler_params=_cparams(("parallel",)),
    )


TF = 1408


def ffn_fwd(x1, g, wg, wu, wd, name, dep=None, tgt=None):
    S = x1.shape[0]
    ni = S // TM
    with_loss = tgt is not None
    row = pl.BlockSpec((TM, D_MODEL), lambda i: (i, 0))
    vec = pl.BlockSpec((1, D_MODEL), lambda i: (0, 0))
    acts = [jax.ShapeDtypeStruct((S, D_FF), BF16)] * 3

    def half_weights(j):
        return [pl.BlockSpec((TF, D_MODEL), lambda i: (j, 0))] * 3

    def swiglu(h2, wg_ref, wu_ref, a_ref, b_ref, f_ref):
        a = _dot_nt(h2, wg_ref[...])
        b = _dot_nt(h2, wu_ref[...])
        a_ref[...] = a.astype(BF16)
        b_ref[...] = b.astype(BF16)
        f = ((a * _sigmoid(a)) * b).astype(BF16)
        f_ref[...] = f
        return f

    def first(x_ref, g_ref, wg_ref, wu_ref, wd_ref, h2_ref, a_ref, b_ref, f_ref, part_ref):
        xf = x_ref[...]
        h2 = ((xf * _rms(xf)) * g_ref[...]).astype(BF16)
        h2_ref[...] = h2
        part_ref[...] = xf + _dot(swiglu(h2, wg_ref, wu_ref, a_ref, b_ref, f_ref), wd_ref[...])

    h2, a, b, f, part = _pallas_after(
        dep, first, x1, g, wg, wu, wd, name=name + "_a", grid=(ni,),
        in_specs=[row, vec] + half_weights(0),
        out_specs=[row] + [pl.BlockSpec((TM, TF), lambda i: (i, 0))] * 3 + [row],
        out_shape=[jax.ShapeDtypeStruct((S, D_MODEL), BF16)] + acts + [jax.ShapeDtypeStruct((S, D_MODEL), F32)],
        compiler_params=_cparams(("parallel",)),
    )

    def second(h2_ref, part_ref, wg_ref, wu_ref, wd_ref, *rest):
        rest = rest[3:]
        if with_loss:
            t_ref, a_ref, b_ref, f_ref, o_ref, l_ref, sq = rest
        else:
            a_ref, b_ref, f_ref, o_ref = rest
        i = pl.program_id(0)
        out = part_ref[...] + _dot(swiglu(h2_ref[...], wg_ref, wu_ref, a_ref, b_ref, f_ref), wd_ref[...])
        if not with_loss:
            o_ref[...] = out
            return
        e = out - t_ref[...]
        o_ref[...] = e * (1.0 / D_MODEL)
        col = jnp.sum(e * e, axis=0, keepdims=True)

        @pl.when(i == 0)
        def _():
            sq[...] = col

        @pl.when(i > 0)
        def _():
            sq[...] += col

        @pl.when(i == ni - 1)
        def _():
            l_ref[...] = jnp.full((1, 128), jnp.sum(sq[...]) * (0.5 / D_MODEL), F32)

    results = pl.pallas_call(
        second, name=name + "_b", grid=(ni,),
        in_specs=[row, row] + half_weights(1) + [ANY] * 3 + ([row] if with_loss else []),
        out_specs=[pl.BlockSpec((TM, TF), lambda i: (i, 1))] * 3 + [row]
        + ([pl.BlockSpec((1, 128), lambda i: (0, 0))] if with_loss else []),
        out_shape=acts + [jax.ShapeDtypeStruct((S, D_MODEL), F32)]
        + ([jax.ShapeDtypeStruct((1, 128), F32)] if with_loss else []),
        input_output_aliases={5: 0, 6: 1, 7: 2},
        scratch_shapes=[pltpu.VMEM((1, D_MODEL), F32)] if with_loss else [],
        compiler_params=_cparams(("arbitrary",) if with_loss else ("parallel",)),
    )(h2, part, wg, wu, wd, a, b, f, *([tgt] if with_loss else []))
    return (h2, *results)


def ffn_bwd(dx2, a, b, x1, g, wg, wu, wd, name, dep=None):
    S = dx2.shape[0]
    ni = S // TM
    row = pl.BlockSpec((TM, D_MODEL), lambda i: (i, 0))
    vec = pl.BlockSpec((1, D_MODEL), lambda i: (0, 0))
    acts = [jax.ShapeDtypeStruct((S, D_FF), BF16)] * 2

    def half(j):
        return [pl.BlockSpec((TM, TF), lambda i: (i, j))] * 2, [pl.BlockSpec((TF, D_MODEL), lambda i: (j, 0))] * 3

    def through_half(dx_ref, a_ref, b_ref, wg_ref, wu_ref, wd_ref, da_ref, db_ref):
        df = _dot_nt(dx_ref[...].astype(BF16), wd_ref[...])
        av = a_ref[...].astype(F32)
        bv = b_ref[...].astype(F32)
        sg = _sigmoid(av)
        da = ((df * bv) * (sg * (1.0 + av * (1.0 - sg)))).astype(BF16)
        db = (df * (av * sg)).astype(BF16)
        da_ref[...] = da
        db_ref[...] = db
        return _dot(da, wg_ref[...]) + _dot(db, wu_ref[...])

    def first(dx_ref, a_ref, b_ref, wg_ref, wu_ref, wd_ref, da_ref, db_ref, part_ref):
        part_ref[...] = through_half(dx_ref, a_ref, b_ref, wg_ref, wu_ref, wd_ref, da_ref, db_ref)

    tiles, weights = half(0)
    da, db, part = _pallas_after(
        dep, first, dx2, a, b, wg, wu, wd, name=name + "_a", grid=(ni,),
        in_specs=[row] + tiles + weights, out_specs=tiles + [row],
        out_shape=acts + [jax.ShapeDtypeStruct((S, D_MODEL), F32)],
        compiler_params=_cparams(("parallel",)),
    )

    def second(dx_ref, a_ref, b_ref, wg_ref, wu_ref, wd_ref, part_ref, x_ref, g_ref, da_prev, db_prev,
               da_ref, db_ref, dx1_ref, dg_ref):
        i = pl.program_id(0)

        @pl.when(i == 0)
        def _():
            dg_ref[...] = jnp.zeros_like(dg_ref)

        dh = part_ref[...] + through_half(dx_ref, a_ref, b_ref, wg_ref, wu_ref, wd_ref, da_ref, db_ref)
        xf = x_ref[...]
        r = _rms(xf)
        xh = xf * r
        dg_ref[...] += jnp.sum(dh * xh, axis=0, keepdims=True)
        dx1_ref[...] = dx_ref[...] + _rms_bwd(dh, xh, r, g_ref[...])

    tiles, weights = half(1)
    return pl.pallas_call(
        second, name=name + "_b", grid=(ni,),
        in_specs=[row] + tiles + weights + [row, row, vec, ANY, ANY],
        out_specs=tiles + [row, vec],
        out_shape=acts + [jax.ShapeDtypeStruct((S, D_MODEL), F32), jax.ShapeDtypeStruct((1, D_MODEL), F32)],
        input_output_aliases={9: 0, 10: 1},
        compiler_params=_cparams(("arbitrary",)),
    )(dx2, a, b, wg, wu, wd, part, x1, g, da, db)


def grad_weight(lhs, rhs, tm, name, tk=1024):
    S, N = rhs.shape
    M = lhs[0].shape[1]
    tk = min(tk, S)
    nk = S // tk
    n_l = len(lhs)

    def body(*refs):
        l_refs, r_ref = refs[:n_l], refs[n_l]
        o_refs, accs = refs[n_l + 1:2 * n_l + 1], refs[2 * n_l + 1:]
        k = pl.program_id(1)
        rv = r_ref[...].astype(BF16)
        for l_ref, o_ref, acc in zip(l_refs, o_refs, accs):
            @pl.when(k == 0)
            def _():
                acc[...] = jnp.zeros_like(acc)

            acc[...] += _dot_tn(l_ref[...], rv)

            @pl.when(k == nk - 1)
            def _():
                o_ref[...] = acc[...].astype(BF16)

    return pl.pallas_call(
        body, name=name, grid=(M // tm, nk),
        in_specs=[pl.BlockSpec((tk, tm), lambda i, k: (k, i))] * n_l + [pl.BlockSpec((tk, N), lambda i, k: (k, 0))],
        out_specs=[pl.BlockSpec((tm, N), lambda i, k: (i, 0))] * n_l,
        out_shape=[jax.ShapeDtypeStruct((M, N), BF16)] * n_l,
        scratch_shapes=[pltpu.VMEM((tm, N), F32)] * n_l,
        compiler_params=_cparams(("parallel", "arbitrary")),
    )(*lhs, rhs)


def proj_bwd(dproj, w, x, g, dres, name, dep=None):
    S, N = dproj.shape

    def body(dp_ref, w_ref, x_ref, g_ref, dr_ref, dx_ref, dg_ref):
        i = pl.program_id(0)

        @pl.when(i == 0)
        def _():
            dg_ref[...] = jnp.zeros_like(dg_ref)

        dh = _dot(dp_ref[...], w_ref[...])
        xf = x_ref[...]
        r = _rms(xf)
        xh = xf * r
        dg_ref[...] += jnp.sum(dh * xh, axis=0, keepdims=True)
        dx_ref[...] = dr_ref[...] + _rms_bwd(dh, xh, r, g_ref[...])

    return _pallas_after(
        dep, body, dproj, w, x, g, dres, name=name, grid=(S // TM,),
        in_specs=[pl.BlockSpec((TM, N), lambda i: (i, 0)), pl.BlockSpec((N, D_MODEL), lambda i: (0, 0)),
                  pl.BlockSpec((TM, D_MODEL), lambda i: (i, 0)), pl.BlockSpec((1, D_MODEL), lambda i: (0, 0)),
                  pl.BlockSpec((TM, D_MODEL), lambda i: (i, 0))],
        out_specs=[pl.BlockSpec((TM, D_MODEL), lambda i: (i, 0)), pl.BlockSpec((1, D_MODEL), lambda i: (0, 0))],
        out_shape=[jax.ShapeDtypeStruct((S, D_MODEL), F32), jax.ShapeDtypeStruct((1, D_MODEL), F32)],
        compiler_params=_cparams(("arbitrary",)),
    )


def mixer_bwd(proj, bias, y, ao, dx1, w_o, sinks, conv_w, qg, kg, cog, aog, name, dep=None):
    S = proj.shape[0]
    SUB, ROWS = SUB_BWD, SUB_BWD * BLK
    steps = S // ROWS
    KV_W = 2 * PAIR

    def body(sinks_ref, p_ref, kvp_ref, bias_ref, y_ref, ao_ref, dx_first_ref, dx_next_ref, wo_ref, cw_ref, qg_ref,
             kg_ref, cog_ref, aog_ref, dp_ref, dcw_ref, dqg_ref, dkg_ref, dsk_ref, dcog_ref, daog_ref,
             dycar, kcar, vcar, dmix_scr):
        step = pl.program_id(0)

        @pl.when(step == 0)
        def _():
            dmix_scr[...] = _dot_nt(dx_first_ref[...].astype(BF16), wo_ref[...])
            dycar[...] = jnp.zeros_like(dycar)
            kcar[...] = jnp.zeros_like(kcar)
            vcar[...] = jnp.zeros_like(vcar)
            dcw_ref[...] = jnp.zeros_like(dcw_ref)
            dqg_ref[...] = jnp.zeros_like(dqg_ref)
            dkg_ref[...] = jnp.zeros_like(dkg_ref)
            dsk_ref[...] = jnp.zeros_like(dsk_ref)
            dcog_ref[...] = jnp.zeros_like(dcog_ref)
            daog_ref[...] = jnp.zeros_like(daog_ref)

        dma = dmix_scr[:, CONV_CH:]

        aov = ao_ref[...].astype(F32)
        ra = _rms(aov)
        ah = aov * ra
        daog_ref[...] += jnp.sum(dma * ah, axis=0, keepdims=True)
        dao = _rms_bwd(dma, ah, ra, aog_ref[...])

        low = _low_half()
        high = jnp.logical_not(low)
        q_gain = qg_ref[...] * SCALE
        k, v = _kv_pairs(p_ref, kvp_ref)
        rk = _pair_rms(k, low)
        kh = k * rk
        k_of = _one_head_in_both_halves((kh * kg_ref[...]).astype(BF16), low)
        v_of = _one_head_in_both_halves(v, low)
        lane8 = lax.broadcasted_iota(jnp.int32, (1, N_Q), 1)
        dsk = jnp.zeros((1, N_Q), F32)
        dqg = jnp.zeros((1, PAIR), F32)
        biases = [bias_ref[jnp.minimum(steps - 1 - step, 1)]] + [bias_ref[1]] * (SUB - 1)
        units = [(s, hq) for s in range(SUB) for hq in range(N_Q)]
        rq, qh, q_one, do_one, delta = [], [], [], [], []
        for pair in range(N_Q // 2):
            cols = slice(PAIR * pair, PAIR * (pair + 1))
            q = p_ref[:, O_Q + PAIR * pair:O_Q + PAIR * (pair + 1)].astype(F32)
            rq.append(_pair_rms(q, low))
            qh.append(q * rq[pair])
            qn = (qh[pair] * q_gain).astype(BF16)
            do = dao[:, cols]
            do_b = do.astype(BF16)
            delta += _half_sums(do * aov[:, cols], low)
            for mine in (low, high):
                q_one.append(jnp.where(mine, qn, jnp.zeros_like(qn)))
                do_one.append(jnp.where(mine, do_b, jnp.zeros_like(do_b)))
        scores = {u: _dot_nt(q_one[u[1]][_block(u[0])], k_of[u[1] // GRP][_band(u[0])]) + biases[u[0]] for u in units}
        dprobs = {u: _dot_nt(do_one[u[1]][_block(u[0])], v_of[u[1] // GRP][_band(u[0])]) for u in units}
        probs = {u: _softmax_with_sink(scores[u], sinks_ref[u[1]]) for u in units}
        ds = {u: (probs[u][0] * (dprobs[u] - delta[u[1]][_block(u[0])])).astype(BF16) for u in units}
        dv_t = {(s, h): jnp.zeros((PAIR, 2 * BLK), F32) for s in range(SUB) for h in range(N_KV)}
        dkn_t = dict(dv_t)
        for s, hq in units:
            dsk = dsk - jnp.where(lane8 == hq, jnp.sum(probs[s, hq][1] * delta[hq][_block(s)]), 0.0)
            dv_t[s, hq // GRP] = dv_t[s, hq // GRP] + _dot_tn(do_one[hq][_block(s)], probs[s, hq][0].astype(BF16))
            dkn_t[s, hq // GRP] = dkn_t[s, hq // GRP] + _dot_tn(q_one[hq][_block(s)], ds[s, hq])
        dqn_of = {u: _dot(ds[u], k_of[u[1] // GRP][_band(u[0])]) for u in units}
        for pair in range(N_Q // 2):
            dqn = jnp.concatenate([jnp.where(low, dqn_of[s, 2 * pair], dqn_of[s, 2 * pair + 1]) for s in range(SUB)],
                                  axis=0)
            dqg = dqg + jnp.sum(dqn * qh[pair], axis=0, keepdims=True)
            dqh = dqn * q_gain
            dp_ref[:, O_Q + PAIR * pair:O_Q + PAIR * (pair + 1)] = (
                rq[pair] * (dqh - qh[pair] * _pair_mean(dqh * qh[pair], low))).astype(BF16)

        def over_key_rows(parts):
            bands = [jnp.concatenate([parts[s, h][:HEAD] + parts[s, h][HEAD:] for h in range(N_KV)], axis=0).T
                     for s in range(SUB)]
            pieces = [bands[0][:BLK]]
            pieces += [bands[s - 1][BLK:] + bands[s][:BLK] for s in range(1, SUB)]
            pieces.append(bands[SUB - 1][BLK:])
            return jnp.concatenate(pieces, axis=0)

        dv = over_key_rows(dv_t)
        dkn = over_key_rows(dkn_t)
        dkg_ref[...] += jnp.sum(dkn * kh, axis=0, keepdims=True)
        dkh = dkn * kg_ref[...]
        dk = rk * (dkh - kh * _pair_mean(dkh * kh, low))
        last = slice(ROWS, ROWS + BLK)
        dp_ref[:, O_K:O_V] = jnp.concatenate([dk[BLK:ROWS], dk[last] + kcar[...]], axis=0).astype(BF16)
        dp_ref[:, O_V:] = jnp.concatenate([dv[BLK:ROWS], dv[last] + vcar[...]], axis=0).astype(BF16)
        kcar[...] = dk[:BLK, :]
        vcar[...] = dv[:BLK, :]
        dsk_ref[...] += dsk
        dqg_ref[...] += dqg * SCALE

        bg = p_ref[:, 0:CONV_CH].astype(F32)
        cg = p_ref[:, CONV_CH:2 * CONV_CH].astype(F32)
        hc = p_ref[:, 2 * CONV_CH:3 * CONV_CH].astype(F32)
        yv = y_ref[...].astype(F32)
        dmc = dmix_scr[:, 0:CONV_CH]
        co = bg * yv
        rc = _rms(co)
        ch = co * rc
        dcog_ref[...] += jnp.sum(dmc * ch, axis=0, keepdims=True)
        dco = _rms_bwd(dmc, ch, rc, cog_ref[...])
        dp_ref[:, 0:CONV_CH] = (dco * yv).astype(BF16)
        dy = dco * bg
        row = lax.broadcasted_iota(jnp.int32, (ROWS, CONV_CH), 0)
        nxt = dycar[...]
        dy1 = jnp.where(row == ROWS - 1, nxt[0:1, :], pltpu.roll(dy, ROWS - 1, 0))
        dy2 = jnp.where(row == ROWS - 2, nxt[0:1, :],
                        jnp.where(row == ROWS - 1, nxt[1:2, :], pltpu.roll(dy, ROWS - 2, 0)))
        dycar[...] = dy[0:8, :]
        du = cw_ref[2:3, :] * dy + cw_ref[1:2, :] * dy1 + cw_ref[0:1, :] * dy2
        dp_ref[:, CONV_CH:2 * CONV_CH] = (du * hc).astype(BF16)
        dp_ref[:, 2 * CONV_CH:3 * CONV_CH] = (du * cg).astype(BF16)
        u = cg * hc
        dcw_ref[0:1, :] += jnp.sum(dy2 * u, axis=0, keepdims=True)
        dcw_ref[1:2, :] += jnp.sum(dy1 * u, axis=0, keepdims=True)
        dcw_ref[2:3, :] += jnp.sum(dy * u, axis=0, keepdims=True)

        dmix_scr[...] = _dot_nt(dx_next_ref[...].astype(BF16), wo_ref[...])

    small = lambda shape: pl.BlockSpec(shape, lambda s: (0, 0))
    blk = lambda w: pl.BlockSpec((ROWS, w), lambda s: (steps - 1 - s, 0))
    return _pallas_after(
        dep, body, sinks, proj, proj, bias, y, ao, dx1, dx1, w_o, conv_w, qg, kg, cog, aog, name=name, grid=(steps,),
        in_specs=[pl.BlockSpec(memory_space=pltpu.SMEM),
                  blk(IN_COLS),
                  pl.BlockSpec((BLK, KV_W), lambda s: (jnp.maximum(SUB * (steps - 1 - s) - 1, 0), O_K // KV_W)),
                  pl.BlockSpec((2, BLK, 2 * BLK), lambda s: (0, 0, 0)),
                  blk(CONV_CH), blk(ATTN_W),
                  pl.BlockSpec((ROWS, D_MODEL), lambda s: (steps - 1, 0)),
                  pl.BlockSpec((ROWS, D_MODEL), lambda s: (jnp.maximum(steps - 2 - s, 0), 0)),
                  small((D_MODEL, D_MODEL)),
                  small((3, CONV_CH)), small((1, PAIR)), small((1, PAIR)), small((1, CONV_CH)), small((1, ATTN_W))],
        out_specs=[blk(IN_COLS), small((3, CONV_CH)), small((1, PAIR)), small((1, PAIR)), small((1, N_Q)),
                   small((1, CONV_CH)), small((1, ATTN_W))],
        out_shape=[jax.ShapeDtypeStruct((S, IN_COLS), BF16), jax.ShapeDtypeStruct((3, CONV_CH), F32),
                   jax.ShapeDtypeStruct((1, PAIR), F32), jax.ShapeDtypeStruct((1, PAIR), F32),
                   jax.ShapeDtypeStruct((1, N_Q), F32), jax.ShapeDtypeStruct((1, CONV_CH), F32),
                   jax.ShapeDtypeStruct((1, ATTN_W), F32)],
        scratch_shapes=[pltpu.VMEM((8, CONV_CH), F32), pltpu.VMEM((BLK, PAIR), F32), pltpu.VMEM((BLK, PAIR), F32),
                        pltpu.VMEM((ROWS, D_MODEL), F32)],
        compiler_params=_cparams(("arbitrary",)),
    )


OTHER_CHIPS = ((1, 0), (0, 1), (1, 1))


def _position():
    return lax.axis_index("x"), lax.axis_index("y"), lax.axis_index("c")


def all_gather(shards, name):
    n = len(shards)

    def body(*refs):
        x_refs, out_refs = refs[:n], refs[n:2 * n]
        send_sems, recv_sems, local_sems = refs[2 * n:]
        x, y, c = _position()
        me, sibling = (x, y, c), (x, y, 1 - c)
        chips = [(x ^ mx, y ^ my) for mx, my in OTHER_CHIPS]

        def slab(a, px, py, pc):
            return out_refs[a].at[4 * px + 2 * py + pc]

        def copy(a, k, block, to, src=None):
            return pltpu.make_async_remote_copy(
                src_ref=slab(a, *block) if src is None else src, dst_ref=slab(a, *block),
                send_sem=send_sems.at[7 * a + k], recv_sem=recv_sems.at[7 * a + k], device_id=to, device_id_type=MESH_T)

        mine = [pltpu.make_async_copy(x_refs[a], slab(a, *me), local_sems.at[a]) for a in range(n)]
        for cp in mine:
            cp.start()
        first = []
        for a in range(n):
            first.append(copy(a, 0, me, sibling, src=x_refs[a]))
            first += [copy(a, 1 + j, me, (*chip, c), src=x_refs[a]) for j, chip in enumerate(chips)]
        for cp in first:
            cp.start()
        passed = []
        for a in range(n):
            for j, chip in enumerate(chips):
                copy(a, 1 + j, (*chip, c), me).wait_recv()
                passed.append(copy(a, 4 + j, (*chip, c), sibling))
                passed[-1].start()
        for a in range(n):
            copy(a, 0, sibling, me).wait_recv()
            for j, chip in enumerate(chips):
                copy(a, 4 + j, (*chip, 1 - c), me).wait_recv()
        for cp in first + passed:
            cp.wait_send()
        for cp in mine:
            cp.wait()

    return pl.pallas_call(
        body, name=name, in_specs=[ANY] * n, out_specs=[ANY] * n,
        out_shape=[jax.ShapeDtypeStruct((N_DEV, *s.shape), s.dtype) for s in shards],
        scratch_shapes=[pltpu.SemaphoreType.DMA((7 * n,)), pltpu.SemaphoreType.DMA((7 * n,)),
                        pltpu.SemaphoreType.DMA((n,))],
    )(*shards)


class SplitCopy:
    def __init__(self, name, arrays, n_copies, plan, after=None):
        n = len(arrays)
        self.name, self.n, self.n_copies, self.plan = name, n, n_copies, plan
        extra = [] if after is None else [after]

        def body(*refs):
            in_refs = refs[:n]
            send_sems, recv_sems = refs[n + len(extra)], refs[n + len(extra) + 1]
            token = refs[2 * n + len(extra) + 2]
            for k, (src, dst, to) in enumerate(plan(_position(), in_refs)):
                pltpu.make_async_remote_copy(src_ref=src, dst_ref=dst, send_sem=send_sems.at[k],
                                             recv_sem=recv_sems.at[k], device_id=to, device_id_type=MESH_T).start()
            token[...] = jnp.zeros_like(token)

        outs = pl.pallas_call(
            body, name=name + "_start",
            out_shape=(pltpu.SemaphoreType.DMA((n_copies,)), pltpu.SemaphoreType.DMA((n_copies,)),
                       *[pltpu.HBM(a.shape, a.dtype) for a in arrays], jax.ShapeDtypeStruct((8, 128), F32)),
            in_specs=[HBM] * n + [ANY] * len(extra),
            out_specs=(SEM, SEM, *[HBM] * n, pl.BlockSpec(memory_space=pltpu.VMEM)),
            input_output_aliases={i: 2 + i for i in range(n)},
            compiler_params=pltpu.CompilerParams(has_side_effects=pltpu.SideEffectType.DATAFLOW_SIDE_EFFECTING),
        )(*[pltpu.with_memory_space_constraint(a, pltpu.HBM) for a in arrays], *extra)
        self.send_sems, self.recv_sems = outs[0], outs[1]
        self.arrays, self.token = list(outs[2:2 + n]), outs[2 + n]

    def wait(self, after):
        n, plan = self.n, self.plan

        def body(*refs):
            in_refs, send_sems, recv_sems = refs[:n], refs[n], refs[n + 1]
            for k, (src, dst, to) in enumerate(plan(_position(), in_refs)):
                cp = pltpu.make_async_remote_copy(src_ref=src, dst_ref=dst, send_sem=send_sems.at[k],
                                                  recv_sem=recv_sems.at[k], device_id=to, device_id_type=MESH_T)
                cp.wait_send()
                cp.wait_recv()

        outs = pl.pallas_call(
            body, name=self.name + "_wait",
            out_shape=tuple(pltpu.HBM(a.shape, a.dtype) for a in self.arrays),
            in_specs=[HBM] * n + [SEM, SEM, ANY], out_specs=tuple([HBM] * n),
            input_output_aliases={i: i for i in range(n)},
            compiler_params=pltpu.CompilerParams(has_side_effects=pltpu.SideEffectType.DATAFLOW_SIDE_EFFECTING),
        )(*self.arrays, self.send_sems, self.recv_sems, after)
        return list(outs)


def gather_start(shards, me, name, after=None):
    n = len(shards)
    lands = [lax.dynamic_update_slice(lax.empty((N_DEV, *s.shape), s.dtype), s[None], (me, 0, 0)) for s in shards]

    def plan(pos, refs):
        x, y, c = pos
        return [(refs[a], refs[n + a].at[4 * x + 2 * y + c], (x ^ mx, y ^ my, c))
                for a in range(n) for mx, my in OTHER_CHIPS]

    return SplitCopy(name, list(shards) + lands, 3 * n, plan, after=after)


def sibling_start(lands, name):
    n = len(lands)

    def plan(pos, refs):
        x, y, c = pos
        return [(refs[a].at[2 * q + c], refs[a].at[2 * q + c], (x, y, 1 - c)) for a in range(n) for q in range(4)]

    return SplitCopy(name, list(lands), 4 * n, plan)


def scatter_start(slabs, name):
    n = len(slabs)
    lands = [lax.empty((N_DEV - 1, *g.shape[1:]), g.dtype) for g in slabs]

    def plan(pos, refs):
        x, y, c = pos
        copies = []
        for a in range(n):
            for r in range(1, N_DEV):
                px, py, pc = x ^ ((r >> 2) & 1), y ^ ((r >> 1) & 1), c ^ (r & 1)
                copies.append((refs[a].at[4 * px + 2 * py + pc], refs[n + a].at[r - 1], (px, py, pc)))
        return copies

    return SplitCopy(name, list(slabs) + lands, (N_DEV - 1) * n, plan)


def all_reduce_small(v, name):
    R, W = v.shape

    def body(v_ref, o_ref, recv, send_sems, recv_sems):
        x, y, c = _position()
        me = 4 * x + 2 * y + c
        copies = []
        for r in range(1, N_DEV):
            to = (x ^ ((r >> 2) & 1), y ^ ((r >> 1) & 1), c ^ (r & 1))
            copies.append(pltpu.make_async_remote_copy(
                src_ref=v_ref, dst_ref=recv.at[me], send_sem=send_sems.at[r - 1], recv_sem=recv_sems.at[r - 1],
                device_id=to, device_id_type=MESH_T))
        for cp in copies:
            cp.start()
        recv[pl.ds(me, 1)] = v_ref[...][None]
        for cp in copies:
            cp.wait()
        acc = recv[0]
        for s in range(1, N_DEV):
            acc = acc + recv[s]
        o_ref[...] = acc

    return pl.pallas_call(
        body, name=name,
        in_specs=[pl.BlockSpec(memory_space=pltpu.VMEM)], out_specs=pl.BlockSpec(memory_space=pltpu.VMEM),
        out_shape=jax.ShapeDtypeStruct((R, W), F32),
        scratch_shapes=[pltpu.VMEM((N_DEV, R, W), F32), pltpu.SemaphoreType.DMA((N_DEV - 1,)),
                        pltpu.SemaphoreType.DMA((N_DEV - 1,))],
    )(v)


def _row_tile(rows):
    if rows <= 512:
        return rows
    return max(t for t in range(8, 513, 8) if rows % t == 0)


def _adamw_update(w, g, m, v):
    mn = ADAM_B1 * m + (1.0 - ADAM_B1) * g
    vn = ADAM_B2 * v + (1.0 - ADAM_B2) * (g * g)
    m_hat = mn / (1.0 - ADAM_B1 ** ADAM_STEP)
    v_hat = vn / (1.0 - ADAM_B2 ** ADAM_STEP)
    return -ADAM_LR * (m_hat / (jnp.sqrt(v_hat) + ADAM_EPS) + ADAM_WD * w), mn, vn


def adamw(w, g, m, v, name):
    R, W = w.shape
    tr = _row_tile(R)

    def body(w_ref, g_ref, m_ref, v_ref, d_ref, mo_ref, vo_ref):
        d_ref[...], mo_ref[...], vo_ref[...] = _adamw_update(w_ref[...], g_ref[...], m_ref[...], v_ref[...])

    spec = pl.BlockSpec((tr, W), lambda i: (i, 0))
    return pl.pallas_call(
        body, name=name, grid=(R // tr,), in_specs=[spec] * 4, out_specs=[spec] * 3,
        out_shape=[jax.ShapeDtypeStruct((R, W), F32)] * 3,
        compiler_params=_cparams(("parallel",)),
    )(w, g, m, v)


def reduce_adamw(slabs, land, w, m, v, layer, me_arr, name, others=None):
    _, R, W = slabs.shape
    tr = _row_tile(R)
    n_other = 0 if others is None else 4

    def body(me_ref, s_ref, l_ref, w_ref, m_ref, v_ref, *rest):
        g_ref, d_ref, mo_ref, vo_ref = rest[n_other:]
        g = s_ref[0].astype(F32)
        for r in range(N_DEV - 1):
            g = g + l_ref[r].astype(F32)
        g_ref[0] = g
        d_ref[0], mo_ref[0], vo_ref[0] = _adamw_update(w_ref[0], g, m_ref[0], v_ref[0])

    spec = pl.BlockSpec((1, tr, W), lambda i, me: (layer, i, 0))
    return pl.pallas_call(
        body, name=name,
        grid_spec=pltpu.PrefetchScalarGridSpec(
            num_scalar_prefetch=1, grid=(R // tr,),
            in_specs=[pl.BlockSpec((1, tr, W), lambda i, me: (me[0], i, 0)),
                      pl.BlockSpec((N_DEV - 1, tr, W), lambda i, me: (0, i, 0)), spec, spec, spec] + [ANY] * n_other,
            out_specs=[spec] * 4),
        out_shape=[jax.ShapeDtypeStruct((DEPTH, R, W), F32)] * 4,
        input_output_aliases={6 + i: i for i in range(n_other)},
        compiler_params=_cparams(("parallel",)),
    )(me_arr, slabs, land, w, m, v, *([] if others is None else others))


SMALL_NAMES = ("norm1_g", "q_norm_g", "k_norm_g", "sinks", "conv_out_g", "attn_out_g", "norm2_g", "conv_w")
SMALL_SIZES = (D_MODEL, HEAD, HEAD, N_Q, CONV_CH, ATTN_W, D_MODEL, 3 * CONV_CH)
SMALL_ROWS = 80


def kernel(x, norm1_g, w_in, conv_w, q_norm_g, k_norm_g, sinks, conv_out_g, attn_out_g, w_o, norm2_g, w_gate, w_up, w_down, loss_target, m_norm1_g, m_w_in, m_conv_w, m_q_norm_g, m_k_norm_g, m_sinks, m_conv_out_g, m_attn_out_g, m_w_o, m_norm2_g, m_w_gate, m_w_up, m_w_down, v_norm1_g, v_w_in, v_conv_w, v_q_norm_g, v_k_norm_g, v_sinks, v_conv_out_g, v_attn_out_g, v_w_o, v_norm2_g, v_w_gate, v_w_up, v_w_down):
    xi, yi, ci = _position()
    me = 4 * xi + 2 * yi + ci
    me_arr = jnp.reshape(me, (1,)).astype(jnp.int32)
    xs, tgt = x[0], loss_target[0]
    bf = lambda a: a.astype(BF16)
    bias = band_bias()

    t = lambda a: jnp.swapaxes(a, 1, 2)
    shard = dict(w_in=(t(w_in), t(m_w_in), t(v_w_in)), w_o=(w_o, m_w_o, v_w_o),
                 w_gate=(t(w_gate), t(m_w_gate), t(v_w_gate)), w_up=(t(w_up), t(m_w_up), t(v_w_up)),
                 w_down=(w_down, m_w_down, v_w_down))
    wb = {n: bf(shard[n][0]) for n in shard}

    g_in0, g_o0, g_conv = all_gather([wb["w_in"][0], wb["w_o"][0], conv_w.reshape(DEPTH * 3, HEAD)], "gather_first")
    ag_ffn0 = gather_start([wb["w_gate"][0], wb["w_up"][0], wb["w_down"][0]], me, "gather_ffn0", after=g_in0)
    ag_mix1 = gather_start([wb["w_in"][1], wb["w_o"][1]], me, "gather_mix1", after=ag_ffn0.token)
    ag_ffn1 = gather_start([wb["w_gate"][1], wb["w_up"][1], wb["w_down"][1]], me, "gather_ffn1", after=ag_mix1.token)
    conv_full = g_conv.reshape(N_DEV, DEPTH, 3, HEAD).transpose(1, 2, 0, 3).reshape(DEPTH, 3, CONV_CH)
    pair_gain = lambda g: jnp.tile(g[None], (1, 2))
    small = [dict(norm1_g=norm1_g[l][None], conv_w=conv_full[l], q_norm_g=pair_gain(q_norm_g[l]),
                  k_norm_g=pair_gain(k_norm_g[l]), sinks=sinks[l], conv_out_g=conv_out_g[l][None],
                  attn_out_g=attn_out_g[l][None], norm2_g=norm2_g[l][None]) for l in range(DEPTH)]
    whole = lambda g: g.reshape(-1, D_MODEL)
    weights = [dict(w_in=whole(g_in0), w_o=whole(g_o0)), {}]

    def ffn_weights(g_gate, g_up, g_down):
        return dict(w_gate=whole(g_gate), w_up=whole(g_up), w_down=whole(g_down))

    saved = []
    xl = xs
    for l in range(DEPTH):
        sp, wl = small[l], weights[l]
        h, proj = norm_proj(xl, sp["norm1_g"], wl["w_in"], f"norm_proj{l}",
                            dep=ag_ffn1.token if l == 0 else pass_ffn1.token)
        mixer_args = (proj, bias, sp["sinks"], sp["conv_w"], sp["q_norm_g"], sp["k_norm_g"], sp["conv_out_g"],
                      sp["attn_out_g"], f"mixer_fwd{l}")
        if l == 0:
            mix, y, ao = mixer_fwd(*mixer_args)
            pass_ffn0 = sibling_start(ag_ffn0.wait(mix)[3:], "pass_ffn0")
            x1 = matmul_residual(mix, wl["w_o"], xl, "out_proj0", dep=pass_ffn0.token)
            wl.update(ffn_weights(*pass_ffn0.wait(x1)))
            pass_mix1 = sibling_start(ag_mix1.wait(x1)[2:], "pass_mix1")
            h2, a, b, f, x2 = ffn_fwd(x1, sp["norm2_g"], wl["w_gate"], wl["w_up"], wl["w_down"], "ffn_fwd0",
                                      dep=pass_mix1.token)
            g_in, g_o = pass_mix1.wait(x2)
            weights[1] = dict(w_in=whole(g_in), w_o=whole(g_o))
            pass_ffn1 = sibling_start(ag_ffn1.wait(x2)[3:], "pass_ffn1")
        else:
            mix, y, ao, x1 = mixer_fwd(*mixer_args, residual=xl, w_o=wl["w_o"])
            wl.update(ffn_weights(*pass_ffn1.wait(x1)))
            h2, a, b, f, dx, loss_row = ffn_fwd(x1, sp["norm2_g"], wl["w_gate"], wl["w_up"], wl["w_down"], "ffn_fwd1",
                                                tgt=tgt)
            x2 = None
        saved.append((xl, h, proj, mix, y, ao, x1, h2, a, b, f))
        xl = x2

    stepped = {n: None for n in shard}
    slabs = lambda d: d.reshape(N_DEV, -1, D_MODEL)
    gsmall = [None] * DEPTH

    def finish(sc, names, after, l):
        arrays = sc.wait(after)
        k = len(names)
        for i, n in enumerate(names):
            w, m, v = shard[n]
            stepped[n] = reduce_adamw(arrays[i], arrays[k + i], w, m, v, l, me_arr, f"reduce_adamw_{n}{l}",
                                      others=stepped[n])

    for l in reversed(range(DEPTH)):
        sp, wl = small[l], weights[l]
        x0, h, proj, mix, y, ao, x1, h2, a, b, f = saved[l]
        (d_wd,) = grad_weight((f,), dx, TF, f"grad_w_down{l}", tk=2048)
        sc_down = scatter_start([slabs(d_wd)], f"scatter_w_down{l}")
        da, db, dx1, d_g2 = ffn_bwd(dx, a, b, x1, sp["norm2_g"], wl["w_gate"], wl["w_up"], wl["w_down"],
                                    f"ffn_bwd{l}", dep=sc_down.token)
        d_wg, d_wu = grad_weight((da, db), h2, TF, f"grad_w_gate_up{l}")
        (d_wo,) = grad_weight((mix,), dx1, D_MODEL, f"grad_w_o{l}", tk=2048)
        sc_rest = scatter_start([slabs(d_wg), slabs(d_wu), slabs(d_wo)], f"scatter_w_gate_up_o{l}")
        dproj, d_cw, d_qg, d_kg, d_sk, d_cog, d_aog = mixer_bwd(
            proj, bias, y, ao, dx1, wl["w_o"], sp["sinks"], sp["conv_w"], sp["q_norm_g"], sp["k_norm_g"],
            sp["conv_out_g"], sp["attn_out_g"], f"mixer_bwd{l}", dep=sc_rest.token)
        finish(sc_down, ["w_down"], dproj, l)
        finish(sc_rest, ["w_gate", "w_up", "w_o"], dproj, l)
        (d_win,) = grad_weight((dproj,), h, IN_COLS // 2, f"grad_w_in{l}", tk=2048)
        sc_in = scatter_start([slabs(d_win)], f"scatter_w_in{l}")
        dx, d_g1 = proj_bwd(dproj, wl["w_in"], x0, sp["norm1_g"], dx1, f"proj_bwd{l}", dep=sc_in.token)
        finish(sc_in, ["w_in"], dx, l)
        both_heads = lambda d: d[:, :HEAD] + d[:, HEAD:]
        gsmall[l] = dict(norm1_g=d_g1, conv_w=d_cw, q_norm_g=both_heads(d_qg), k_norm_g=both_heads(d_kg), sinks=d_sk,
                         conv_out_g=d_cog, attn_out_g=d_aog, norm2_g=d_g2)
    grad_x = dx[None]

    flat = jnp.concatenate([loss_row[0, 0:1]] + [gsmall[l][n].reshape(-1) for l in range(DEPTH) for n in SMALL_NAMES])
    flat = jnp.pad(flat, (0, SMALL_ROWS * 128 - flat.shape[0])).reshape(SMALL_ROWS, 128)
    flat = all_reduce_small(flat, "all_reduce_small_grads").reshape(-1)
    loss = flat[0]
    gs = {n: [] for n in SMALL_NAMES}
    off = 1
    for l in range(DEPTH):
        for n, size in zip(SMALL_NAMES, SMALL_SIZES):
            gs[n].append(flat[off:off + size])
            off += size
    gs = {n: jnp.stack(v) for n, v in gs.items()}
    g_conv = lax.dynamic_slice(gs["conv_w"].reshape(DEPTH, 3, CONV_CH), (0, 0, me * HEAD), (DEPTH, 3, HEAD))

    gs["conv_w"] = g_conv
    params = dict(norm1_g=(norm1_g, m_norm1_g, v_norm1_g), conv_w=(conv_w, m_conv_w, v_conv_w),
                  q_norm_g=(q_norm_g, m_q_norm_g, v_q_norm_g), k_norm_g=(k_norm_g, m_k_norm_g, v_k_norm_g),
                  sinks=(sinks, m_sinks, v_sinks), conv_out_g=(conv_out_g, m_conv_out_g, v_conv_out_g),
                  attn_out_g=(attn_out_g, m_attn_out_g, v_attn_out_g), norm2_g=(norm2_g, m_norm2_g, v_norm2_g))
    names = ("norm1_g", "w_in", "conv_w", "q_norm_g", "k_norm_g", "sinks", "conv_out_g", "attn_out_g", "w_o",
             "norm2_g", "w_gate", "w_up", "w_down")

    out = {}
    for n in names:
        if n in shard:
            out[n] = tuple(t(r) for r in stepped[n]) if n in ("w_in", "w_gate", "w_up") else stepped[n]
        else:
            w, m, v = params[n]
            two_d = (-1, w.shape[-1])
            d, mn, vn = adamw(w.reshape(two_d), gs[n].reshape(two_d), m.reshape(two_d), v.reshape(two_d), f"adamw_{n}")
            out[n] = (gs[n].reshape(w.shape), d.reshape(w.shape), mn.reshape(w.shape), vn.reshape(w.shape))
    return (loss, grad_x, *[out[n][i] for i in range(4) for n in names])
```

```python
import jax
import jax.numpy as jnp
from jax import lax
from jax.experimental import pallas as pl
from jax.experimental.pallas import tpu as pltpu

F32 = jnp.float32
BF16 = jnp.bfloat16

D_MODEL = 1024
CONV_CH = 512
ATTN_W = 512
N_Q = 8
N_KV = 2
GRP = N_Q // N_KV
HEAD = 64
IN_COLS = 2304
D_FF = 2816
BLK = 128
O_Q = 3 * CONV_CH
O_K = O_Q + ATTN_W
O_V = O_K + N_KV * HEAD
EPS = 1e-6
NEG_INF = -1e30
SCALE = HEAD ** -0.5
N_DEV = 8
DEPTH = 2

ADAM_LR = 0.001
ADAM_B1 = 0.9
ADAM_B2 = 0.999
ADAM_EPS = 1e-08
ADAM_WD = 0.01
ADAM_STEP = 10

VMEM_LIMIT = 56 * 1024 * 1024
TM = 512
MESH_T = pl.DeviceIdType.MESH

ANY = pl.BlockSpec(memory_space=pl.ANY)
HBM = pl.BlockSpec(memory_space=pltpu.HBM)
SEM = pl.BlockSpec(memory_space=pltpu.SEMAPHORE)


def _cparams(sem):
    return pltpu.CompilerParams(dimension_semantics=sem, vmem_limit_bytes=VMEM_LIMIT)


def _pallas_after(dep, body, *args, in_specs, **kw):
    if dep is None:
        return pl.pallas_call(body, in_specs=in_specs, **kw)(*args)

    def after_dep(dep_ref, *refs):
        body(*refs)

    return pl.pallas_call(after_dep, in_specs=[ANY, *in_specs], **kw)(dep, *args)


def _dot(a, b):
    return jnp.dot(a, b, preferred_element_type=F32)


def _dot_nt(a, b):
    return lax.dot_general(a, b, (((1,), (1,)), ((), ())), preferred_element_type=F32)


def _dot_tn(a, b):
    return lax.dot_general(a, b, (((0,), (0,)), ((), ())), preferred_element_type=F32)


LANES = 128


def _row_reduce(v, op, reduce):
    w = v.shape[-1]
    if w > LANES and w % LANES == 0:
        acc = v[:, 0:LANES]
        for c in range(1, w // LANES):
            acc = op(acc, v[:, LANES * c:LANES * (c + 1)])
        v = acc
    return reduce(v, axis=-1, keepdims=True)


def _row_sum(v):
    return _row_reduce(v, jnp.add, jnp.sum)


def _row_mean(v):
    return _row_sum(v) * (1.0 / v.shape[-1])


def _rms(v):
    return lax.rsqrt(_row_mean(v * v) + EPS)


def _sigmoid(v):
    return 1.0 / (1.0 + jnp.exp(-v))


def _rms_bwd(dyv, xh, r, g):
    dxh = dyv * g
    return r * (dxh - xh * _row_mean(dxh * xh))


def norm_proj(x, g, w_t, name, dep=None):
    S, N = x.shape[0], w_t.shape[0]
    tm = 2 * TM

    def body(x_ref, g_ref, w_ref, h_ref, p_ref):
        xf = x_ref[...]
        h = ((xf * _rms(xf)) * g_ref[...]).astype(BF16)
        h_ref[...] = h
        p_ref[...] = _dot_nt(h, w_ref[...]).astype(BF16)

    return _pallas_after(
        dep, body, x, g, w_t, name=name, grid=(S // tm,),
        in_specs=[pl.BlockSpec((tm, D_MODEL), lambda i: (i, 0)),
                  pl.BlockSpec((1, D_MODEL), lambda i: (0, 0)),
                  pl.BlockSpec((N, D_MODEL), lambda i: (0, 0))],
        out_specs=[pl.BlockSpec((tm, D_MODEL), lambda i: (i, 0)),
                   pl.BlockSpec((tm, N), lambda i: (i, 0))],
        out_shape=[jax.ShapeDtypeStruct((S, D_MODEL), BF16), jax.ShapeDtypeStruct((S, N), BF16)],
        compiler_params=_cparams(("parallel",)),
    )


def band_bias():
    qi = lax.broadcasted_iota(jnp.int32, (BLK, 2 * BLK), 0)
    kj = lax.broadcasted_iota(jnp.int32, (BLK, 2 * BLK), 1)
    diff = qi + BLK - kj
    valid = (diff >= 0) & (diff < BLK)
    return jnp.stack([jnp.where(valid & (kj >= BLK), 0.0, NEG_INF), jnp.where(valid, 0.0, NEG_INF)]).astype(F32)


def _softmax_with_sink(s, sink):
    m = jnp.maximum(_row_reduce(s, jnp.maximum, jnp.max), sink)
    p = jnp.exp(s - m)
    es = jnp.exp(sink - m)
    inv = 1.0 / (_row_sum(p) + es)
    return p * inv, es * inv


PAIR = 2 * HEAD


def _low_half():
    return lax.broadcasted_iota(jnp.int32, (1, PAIR), 1) < HEAD


def _half_sums(v, low):
    return (jnp.sum(jnp.where(low, v, 0.0), axis=-1, keepdims=True),
            jnp.sum(jnp.where(low, 0.0, v), axis=-1, keepdims=True))


def _pair_mean(v, low):
    e, o = _half_sums(v, low)
    return jnp.where(low, e, o) * (1.0 / HEAD)


def _pair_rms(v, low):
    return lax.rsqrt(_pair_mean(v * v, low) + EPS)


def _one_head_in_both_halves(pair, low):
    swapped = pltpu.roll(pair, HEAD, 1)
    return jnp.where(low, pair, swapped), jnp.where(low, swapped, pair)


SUB_FWD, SUB_FWD_PROJECT, SUB_BWD = 2, 4, 4


def _kv_pairs(p_ref, kvp_ref):
    k = jnp.concatenate([kvp_ref[:, 0:PAIR], p_ref[:, O_K:O_K + PAIR]], axis=0).astype(F32)
    v = jnp.concatenate([kvp_ref[:, PAIR:2 * PAIR], p_ref[:, O_V:O_V + PAIR]], axis=0)
    return k, v


def _band(s):
    return slice(BLK * s, BLK * s + 2 * BLK)


def _block(s):
    return slice(BLK * s, BLK * (s + 1))


def mixer_fwd(proj, bias, sinks, conv_w, qg, kg, cog, aog, name, residual=None, w_o=None):
    S = proj.shape[0]
    project = w_o is not None
    SUB = SUB_FWD_PROJECT if project else SUB_FWD
    ROWS = SUB * BLK
    steps = S // ROWS

    def body(sinks_ref, p_ref, kvp_ref, bias_ref, cw_ref, qg_ref, kg_ref, cog_ref, aog_ref, *rest):
        if project:
            x_ref, wo_ref, mix_ref, y_ref, ao_ref, x1_ref, ucar = rest
        else:
            mix_ref, y_ref, ao_ref, ucar = rest
        n = pl.program_id(0)

        @pl.when(n == 0)
        def _():
            ucar[...] = jnp.zeros_like(ucar)

        bg = p_ref[:, 0:CONV_CH].astype(F32)
        u = p_ref[:, CONV_CH:2 * CONV_CH].astype(F32) * p_ref[:, 2 * CONV_CH:3 * CONV_CH].astype(F32)
        row = lax.broadcasted_iota(jnp.int32, (ROWS, CONV_CH), 0)
        prev = ucar[...]
        u1 = jnp.where(row == 0, prev[7:8, :], pltpu.roll(u, 1, 0))
        u2 = jnp.where(row == 0, prev[6:7, :], jnp.where(row == 1, prev[7:8, :], pltpu.roll(u, 2, 0)))
        ucar[...] = u[ROWS - 8:ROWS, :]
        y = cw_ref[0:1, :] * u2 + cw_ref[1:2, :] * u1 + cw_ref[2:3, :] * u
        y_ref[...] = y.astype(BF16)
        co = bg * y
        mix_conv = ((co * _rms(co)) * cog_ref[...]).astype(BF16)
        mix_ref[:, 0:CONV_CH] = mix_conv
        if project:
            x1 = x_ref[...] + _dot(mix_conv, wo_ref[0:CONV_CH, :])

        low = _low_half()
        q_gain = qg_ref[...] * SCALE
        k, v = _kv_pairs(p_ref, kvp_ref)
        kn = ((k * _pair_rms(k, low)) * kg_ref[...]).astype(BF16)
        k_of = _one_head_in_both_halves(kn, low)
        v_of = _one_head_in_both_halves(v, low)
        biases = [bias_ref[jnp.minimum(n, 1)]] + [bias_ref[1]] * (SUB - 1)
        units = [(s, hq) for s in range(SUB) for hq in range(N_Q)]
        q_one = []
        for pair in range(N_Q // 2):
            q = p_ref[:, O_Q + PAIR * pair:O_Q + PAIR * (pair + 1)].astype(F32)
            qn = ((q * _pair_rms(q, low)) * q_gain).astype(BF16)
            q_one += [jnp.where(mine, qn, jnp.zeros_like(qn)) for mine in (low, jnp.logical_not(low))]
        scores = [_dot_nt(q_one[hq][_block(s)], k_of[hq // GRP][_band(s)]) + biases[s] for s, hq in units]
        probs = [_softmax_with_sink(sc, sinks_ref[hq])[0].astype(BF16) for (s, hq), sc in zip(units, scores)]
        o = {u_: _dot(pn, v_of[u_[1] // GRP][_band(u_[0])]) for u_, pn in zip(units, probs)}
        ao = jnp.concatenate(
            [jnp.concatenate([jnp.where(low, o[s, 2 * pair], o[s, 2 * pair + 1]) for pair in range(N_Q // 2)], axis=1)
             for s in range(SUB)], axis=0)
        ao_ref[...] = ao.astype(BF16)
        mix_attn = ((ao * _rms(ao)) * aog_ref[...]).astype(BF16)
        mix_ref[:, CONV_CH:] = mix_attn
        if project:
            x1_ref[...] = x1 + _dot(mix_attn, wo_ref[CONV_CH:, :])

    small = lambda shape: pl.BlockSpec(shape, lambda n: (0, 0))
    rows = lambda w: pl.BlockSpec((ROWS, w), lambda n: (n, 0))
    return pl.pallas_call(
        body, name=name, grid=(steps,),
        in_specs=[pl.BlockSpec(memory_space=pltpu.SMEM),
                  rows(IN_COLS),
                  pl.BlockSpec((BLK, 2 * PAIR), lambda n: (jnp.maximum(SUB * n - 1, 0), O_K // (2 * PAIR))),
                  pl.BlockSpec((2, BLK, 2 * BLK), lambda n: (0, 0, 0)),
                  small((3, CONV_CH)), small((1, PAIR)), small((1, PAIR)), small((1, CONV_CH)), small((1, ATTN_W))]
        + ([rows(D_MODEL), small((D_MODEL, D_MODEL))] if project else []),
        out_specs=[rows(D_MODEL), rows(CONV_CH), rows(ATTN_W)] + ([rows(D_MODEL)] if project else []),
        out_shape=[jax.ShapeDtypeStruct((S, D_MODEL), BF16), jax.ShapeDtypeStruct((S, CONV_CH), BF16),
                   jax.ShapeDtypeStruct((S, ATTN_W), BF16)]
        + ([jax.ShapeDtypeStruct((S, D_MODEL), F32)] if project else []),
        scratch_shapes=[pltpu.VMEM((8, CONV_CH), F32)],
        compiler_params=_cparams(("arbitrary",)),
    )(sinks, proj, proj, bias, conv_w, qg, kg, cog, aog, *([residual, w_o] if project else []))


def matmul_residual(a, w, res, name, dep=None):
    S, K = a.shape
    N = w.shape[1]

    def body(a_ref, w_ref, r_ref, o_ref):
        o_ref[...] = r_ref[...] + _dot(a_ref[...], w_ref[...])

    return _pallas_after(
        dep, body, a, w, res, name=name, grid=(S // TM,),
        in_specs=[pl.BlockSpec((TM, K), lambda i: (i, 0)), pl.BlockSpec((K, N), lambda i: (0, 0)),
                  pl.BlockSpec((TM, N), lambda i: (i, 0))],
        out_specs=pl.BlockSpec((TM, N), lambda i: (i, 0)),
        out_shape=jax.ShapeDtypeStruct((S, N), F32),
        compiler_params=_cparams(("parallel",)),
    )


TF = 1408


def ffn_fwd(x1, g, wg, wu, wd, name, dep=None, tgt=None):
    S = x1.shape[0]
    ni = S // TM
    with_loss = tgt is not None
    row = pl.BlockSpec((TM, D_MODEL), lambda i: (i, 0))
    vec = pl.BlockSpec((1, D_MODEL), lambda i: (0, 0))
    acts = [jax.ShapeDtypeStruct((S, D_FF), BF16)] * 3

    def half_weights(j):
        return [pl.BlockSpec((TF, D_MODEL), lambda i: (j, 0))] * 3

    def swiglu(h2, wg_ref, wu_ref, a_ref, b_ref, f_ref):
        a = _dot_nt(h2, wg_ref[...])
        b = _dot_nt(h2, wu_ref[...])
        a_ref[...] = a.astype(BF16)
        b_ref[...] = b.astype(BF16)
        f = ((a * _sigmoid(a)) * b).astype(BF16)
        f_ref[...] = f
        return f

    def first(x_ref, g_ref, wg_ref, wu_ref, wd_ref, h2_ref, a_ref, b_ref, f_ref, part_ref):
        xf = x_ref[...]
        h2 = ((xf * _rms(xf)) * g_ref[...]).astype(BF16)
        h2_ref[...] = h2
        part_ref[...] = xf + _dot(swiglu(h2, wg_ref, wu_ref, a_ref, b_ref, f_ref), wd_ref[...])

    h2, a, b, f, part = _pallas_after(
        dep, first, x1, g, wg, wu, wd, name=name + "_a", grid=(ni,),
        in_specs=[row, vec] + half_weights(0),
        out_specs=[row] + [pl.BlockSpec((TM, TF), lambda i: (i, 0))] * 3 + [row],
        out_shape=[jax.ShapeDtypeStruct((S, D_MODEL), BF16)] + acts + [jax.ShapeDtypeStruct((S, D_MODEL), F32)],
        compiler_params=_cparams(("parallel",)),
    )

    def second(h2_ref, part_ref, wg_ref, wu_ref, wd_ref, *rest):
        rest = rest[3:]
        if with_loss:
            t_ref, a_ref, b_ref, f_ref, o_ref, l_ref, sq = rest
        else:
            a_ref, b_ref, f_ref, o_ref = rest
        i = pl.program_id(0)
        out = part_ref[...] + _dot(swiglu(h2_ref[...], wg_ref, wu_ref, a_ref, b_ref, f_ref), wd_ref[...])
        if not with_loss:
            o_ref[...] = out
            return
        e = out - t_ref[...]
        o_ref[...] = e * (1.0 / D_MODEL)
        col = jnp.sum(e * e, axis=0, keepdims=True)

        @pl.when(i == 0)
        def _():
            sq[...] = col

        @pl.when(i > 0)
        def _():
            sq[...] += col

        @pl.when(i == ni - 1)
        def _():
            l_ref[...] = jnp.full((1, 128), jnp.sum(sq[...]) * (0.5 / D_MODEL), F32)

    results = pl.pallas_call(
        second, name=name + "_b", grid=(ni,),
        in_specs=[row, row] + half_weights(1) + [ANY] * 3 + ([row] if with_loss else []),
        out_specs=[pl.BlockSpec((TM, TF), lambda i: (i, 1))] * 3 + [row]
        + ([pl.BlockSpec((1, 128), lambda i: (0, 0))] if with_loss else []),
        out_shape=acts + [jax.ShapeDtypeStruct((S, D_MODEL), F32)]
        + ([jax.ShapeDtypeStruct((1, 128), F32)] if with_loss else []),
        input_output_aliases={5: 0, 6: 1, 7: 2},
        scratch_shapes=[pltpu.VMEM((1, D_MODEL), F32)] if with_loss else [],
        compiler_params=_cparams(("arbitrary",) if with_loss else ("parallel",)),
    )(h2, part, wg, wu, wd, a, b, f, *([tgt] if with_loss else []))
    return (h2, *results)


def ffn_bwd(dx2, a, b, x1, g, wg, wu, wd, name, dep=None):
    S = dx2.shape[0]
    ni = S // TM
    row = pl.BlockSpec((TM, D_MODEL), lambda i: (i, 0))
    vec = pl.BlockSpec((1, D_MODEL), lambda i: (0, 0))
    acts = [jax.ShapeDtypeStruct((S, D_FF), BF16)] * 2

    def half(j):
        return [pl.BlockSpec((TM, TF), lambda i: (i, j))] * 2, [pl.BlockSpec((TF, D_MODEL), lambda i: (j, 0))] * 3

    def through_half(dx_ref, a_ref, b_ref, wg_ref, wu_ref, wd_ref, da_ref, db_ref):
        df = _dot_nt(dx_ref[...].astype(BF16), wd_ref[...])
        av = a_ref[...].astype(F32)
        bv = b_ref[...].astype(F32)
        sg = _sigmoid(av)
        da = ((df * bv) * (sg * (1.0 + av * (1.0 - sg)))).astype(BF16)
        db = (df * (av * sg)).astype(BF16)
        da_ref[...] = da
        db_ref[...] = db
        return _dot(da, wg_ref[...]) + _dot(db, wu_ref[...])

    def first(dx_ref, a_ref, b_ref, wg_ref, wu_ref, wd_ref, da_ref, db_ref, part_ref):
        part_ref[...] = through_half(dx_ref, a_ref, b_ref, wg_ref, wu_ref, wd_ref, da_ref, db_ref)

    tiles, weights = half(0)
    da, db, part = _pallas_after(
        dep, first, dx2, a, b, wg, wu, wd, name=name + "_a", grid=(ni,),
        in_specs=[row] + tiles + weights, out_specs=tiles + [row],
        out_shape=acts + [jax.ShapeDtypeStruct((S, D_MODEL), F32)],
        compiler_params=_cparams(("parallel",)),
    )

    def second(dx_ref, a_ref, b_ref, wg_ref, wu_ref, wd_ref, part_ref, x_ref, g_ref, da_prev, db_prev,
               da_ref, db_ref, dx1_ref, dg_ref):
        i = pl.program_id(0)

        @pl.when(i == 0)
        def _():
            dg_ref[...] = jnp.zeros_like(dg_ref)

        dh = part_ref[...] + through_half(dx_ref, a_ref, b_ref, wg_ref, wu_ref, wd_ref, da_ref, db_ref)
        xf = x_ref[...]
        r = _rms(xf)
        xh = xf * r
        dg_ref[...] += jnp.sum(dh * xh, axis=0, keepdims=True)
        dx1_ref[...] = dx_ref[...] + _rms_bwd(dh, xh, r, g_ref[...])

    tiles, weights = half(1)
    return pl.pallas_call(
        second, name=name + "_b", grid=(ni,),
        in_specs=[row] + tiles + weights + [row, row, vec, ANY, ANY],
        out_specs=tiles + [row, vec],
        out_shape=acts + [jax.ShapeDtypeStruct((S, D_MODEL), F32), jax.ShapeDtypeStruct((1, D_MODEL), F32)],
        input_output_aliases={9: 0, 10: 1},
        compiler_params=_cparams(("arbitrary",)),
    )(dx2, a, b, wg, wu, wd, part, x1, g, da, db)


def grad_weight(lhs, rhs, tm, name, tk=1024):
    S, N = rhs.shape
    M = lhs[0].shape[1]
    tk = min(tk, S)
    nk = S // tk
    n_l = len(lhs)

    def body(*refs):
        l_refs, r_ref = refs[:n_l], refs[n_l]
        o_refs, accs = refs[n_l + 1:2 * n_l + 1], refs[2 * n_l + 1:]
        k = pl.program_id(1)
        rv = r_ref[...].astype(BF16)
        for l_ref, o_ref, acc in zip(l_refs, o_refs, accs):
            @pl.when(k == 0)
            def _():
                acc[...] = jnp.zeros_like(acc)

            acc[...] += _dot_tn(l_ref[...], rv)

            @pl.when(k == nk - 1)
            def _():
                o_ref[...] = acc[...].astype(BF16)

    return pl.pallas_call(
        body, name=name, grid=(M // tm, nk),
        in_specs=[pl.BlockSpec((tk, tm), lambda i, k: (k, i))] * n_l + [pl.BlockSpec((tk, N), lambda i, k: (k, 0))],
        out_specs=[pl.BlockSpec((tm, N), lambda i, k: (i, 0))] * n_l,
        out_shape=[jax.ShapeDtypeStruct((M, N), BF16)] * n_l,
        scratch_shapes=[pltpu.VMEM((tm, N), F32)] * n_l,
        compiler_params=_cparams(("parallel", "arbitrary")),
    )(*lhs, rhs)


def proj_bwd(dproj, w, x, g, dres, name, dep=None):
    S, N = dproj.shape

    def body(dp_ref, w_ref, x_ref, g_ref, dr_ref, dx_ref, dg_ref):
        i = pl.program_id(0)

        @pl.when(i == 0)
        def _():
            dg_ref[...] = jnp.zeros_like(dg_ref)

        dh = _dot(dp_ref[...], w_ref[...])
        xf = x_ref[...]
        r = _rms(xf)
        xh = xf * r
        dg_ref[...] += jnp.sum(dh * xh, axis=0, keepdims=True)
        dx_ref[...] = dr_ref[...] + _rms_bwd(dh, xh, r, g_ref[...])

    return _pallas_after(
        dep, body, dproj, w, x, g, dres, name=name, grid=(S // TM,),
        in_specs=[pl.BlockSpec((TM, N), lambda i: (i, 0)), pl.BlockSpec((N, D_MODEL), lambda i: (0, 0)),
                  pl.BlockSpec((TM, D_MODEL), lambda i: (i, 0)), pl.BlockSpec((1, D_MODEL), lambda i: (0, 0)),
                  pl.BlockSpec((TM, D_MODEL), lambda i: (i, 0))],
        out_specs=[pl.BlockSpec((TM, D_MODEL), lambda i: (i, 0)), pl.BlockSpec((1, D_MODEL), lambda i: (0, 0))],
        out_shape=[jax.ShapeDtypeStruct((S, D_MODEL), F32), jax.ShapeDtypeStruct((1, D_MODEL), F32)],
        compiler_params=_cparams(("arbitrary",)),
    )


def mixer_bwd(proj, bias, y, ao, dx1, w_o, sinks, conv_w, qg, kg, cog, aog, name, dep=None):
    S = proj.shape[0]
    SUB, ROWS = SUB_BWD, SUB_BWD * BLK
    steps = S // ROWS
    KV_W = 2 * PAIR

    def body(sinks_ref, p_ref, kvp_ref, bias_ref, y_ref, ao_ref, dx_first_ref, dx_next_ref, wo_ref, cw_ref, qg_ref,
             kg_ref, cog_ref, aog_ref, dp_ref, dcw_ref, dqg_ref, dkg_ref, dsk_ref, dcog_ref, daog_ref,
             dycar, kcar, vcar, dmix_scr):
        step = pl.program_id(0)

        @pl.when(step == 0)
        def _():
            dmix_scr[...] = _dot_nt(dx_first_ref[...].astype(BF16), wo_ref[...])
            dycar[...] = jnp.zeros_like(dycar)
            kcar[...] = jnp.zeros_like(kcar)
            vcar[...] = jnp.zeros_like(vcar)
            dcw_ref[...] = jnp.zeros_like(dcw_ref)
            dqg_ref[...] = jnp.zeros_like(dqg_ref)
            dkg_ref[...] = jnp.zeros_like(dkg_ref)
            dsk_ref[...] = jnp.zeros_like(dsk_ref)
            dcog_ref[...] = jnp.zeros_like(dcog_ref)
            daog_ref[...] = jnp.zeros_like(daog_ref)

        dma = dmix_scr[:, CONV_CH:]

        aov = ao_ref[...].astype(F32)
        ra = _rms(aov)
        ah = aov * ra
        daog_ref[...] += jnp.sum(dma * ah, axis=0, keepdims=True)
        dao = _rms_bwd(dma, ah, ra, aog_ref[...])

        low = _low_half()
        high = jnp.logical_not(low)
        q_gain = qg_ref[...] * SCALE
        k, v = _kv_pairs(p_ref, kvp_ref)
        rk = _pair_rms(k, low)
        kh = k * rk
        k_of = _one_head_in_both_halves((kh * kg_ref[...]).astype(BF16), low)
        v_of = _one_head_in_both_halves(v, low)
        lane8 = lax.broadcasted_iota(jnp.int32, (1, N_Q), 1)
        dsk = jnp.zeros((1, N_Q), F32)
        dqg = jnp.zeros((1, PAIR), F32)
        biases = [bias_ref[jnp.minimum(steps - 1 - step, 1)]] + [bias_ref[1]] * (SUB - 1)
        units = [(s, hq) for s in range(SUB) for hq in range(N_Q)]
        rq, qh, q_one, do_one, delta = [], [], [], [], []
        for pair in range(N_Q // 2):
            cols = slice(PAIR * pair, PAIR * (pair + 1))
            q = p_ref[:, O_Q + PAIR * pair:O_Q + PAIR * (pair + 1)].astype(F32)
            rq.append(_pair_rms(q, low))
            qh.append(q * rq[pair])
            qn = (qh[pair] * q_gain).astype(BF16)
            do = dao[:, cols]
            do_b = do.astype(BF16)
            delta += _half_sums(do * aov[:, cols], low)
            for mine in (low, high):
                q_one.append(jnp.where(mine, qn, jnp.zeros_like(qn)))
                do_one.append(jnp.where(mine, do_b, jnp.zeros_like(do_b)))
        scores = {u: _dot_nt(q_one[u[1]][_block(u[0])], k_of[u[1] // GRP][_band(u[0])]) + biases[u[0]] for u in units}
        dprobs = {u: _dot_nt(do_one[u[1]][_block(u[0])], v_of[u[1] // GRP][_band(u[0])]) for u in units}
        probs = {u: _softmax_with_sink(scores[u], sinks_ref[u[1]]) for u in units}
        ds = {u: (probs[u][0] * (dprobs[u] - delta[u[1]][_block(u[0])])).astype(BF16) for u in units}
        dv_t = {(s, h): jnp.zeros((PAIR, 2 * BLK), F32) for s in range(SUB) for h in range(N_KV)}
        dkn_t = dict(dv_t)
        for s, hq in units:
            dsk = dsk - jnp.where(lane8 == hq, jnp.sum(probs[s, hq][1] * delta[hq][_block(s)]), 0.0)
            dv_t[s, hq // GRP] = dv_t[s, hq // GRP] + _dot_tn(do_one[hq][_block(s)], probs[s, hq][0].astype(BF16))
            dkn_t[s, hq // GRP] = dkn_t[s, hq // GRP] + _dot_tn(q_one[hq][_block(s)], ds[s, hq])
        dqn_of = {u: _dot(ds[u], k_of[u[1] // GRP][_band(u[0])]) for u in units}
        for pair in range(N_Q // 2):
            dqn = jnp.concatenate([jnp.where(low, dqn_of[s, 2 * pair], dqn_of[s, 2 * pair + 1]) for s in range(SUB)],
                                  axis=0)
            dqg = dqg + jnp.sum(dqn * qh[pair], axis=0, keepdims=True)
            dqh = dqn * q_gain
            dp_ref[:, O_Q + PAIR * pair:O_Q + PAIR * (pair + 1)] = (
                rq[pair] * (dqh - qh[pair] * _pair_mean(dqh * qh[pair], low))).astype(BF16)

        def over_key_rows(parts):
            bands = [jnp.concatenate([parts[s, h][:HEAD] + parts[s, h][HEAD:] for h in range(N_KV)], axis=0).T
                     for s in range(SUB)]
            pieces = [bands[0][:BLK]]
            pieces += [bands[s - 1][BLK:] + bands[s][:BLK] for s in range(1, SUB)]
            pieces.append(bands[SUB - 1][BLK:])
            return jnp.concatenate(pieces, axis=0)

        dv = over_key_rows(dv_t)
        dkn = over_key_rows(dkn_t)
        dkg_ref[...] += jnp.sum(dkn * kh, axis=0, keepdims=True)
        dkh = dkn * kg_ref[...]
        dk = rk * (dkh - kh * _pair_mean(dkh * kh, low))
        last = slice(ROWS, ROWS + BLK)
        dp_ref[:, O_K:O_V] = jnp.concatenate([dk[BLK:ROWS], dk[last] + kcar[...]], axis=0).astype(BF16)
        dp_ref[:, O_V:] = jnp.concatenate([dv[BLK:ROWS], dv[last] + vcar[...]], axis=0).astype(BF16)
        kcar[...] = dk[:BLK, :]
        vcar[...] = dv[:BLK, :]
        dsk_ref[...] += dsk
        dqg_ref[...] += dqg * SCALE

        bg = p_ref[:, 0:CONV_CH].astype(F32)
        cg = p_ref[:, CONV_CH:2 * CONV_CH].astype(F32)
        hc = p_ref[:, 2 * CONV_CH:3 * CONV_CH].astype(F32)
        yv = y_ref[...].astype(F32)
        dmc = dmix_scr[:, 0:CONV_CH]
        co = bg * yv
        rc = _rms(co)
        ch = co * rc
        dcog_ref[...] += jnp.sum(dmc * ch, axis=0, keepdims=True)
        dco = _rms_bwd(dmc, ch, rc, cog_ref[...])
        dp_ref[:, 0:CONV_CH] = (dco * yv).astype(BF16)
        dy = dco * bg
        row = lax.broadcasted_iota(jnp.int32, (ROWS, CONV_CH), 0)
        nxt = dycar[...]
        dy1 = jnp.where(row == ROWS - 1, nxt[0:1, :], pltpu.roll(dy, ROWS - 1, 0))
        dy2 = jnp.where(row == ROWS - 2, nxt[0:1, :],
                        jnp.where(row == ROWS - 1, nxt[1:2, :], pltpu.roll(dy, ROWS - 2, 0)))
        dycar[...] = dy[0:8, :]
        du = cw_ref[2:3, :] * dy + cw_ref[1:2, :] * dy1 + cw_ref[0:1, :] * dy2
        dp_ref[:, CONV_CH:2 * CONV_CH] = (du * hc).astype(BF16)
        dp_ref[:, 2 * CONV_CH:3 * CONV_CH] = (du * cg).astype(BF16)
        u = cg * hc
        dcw_ref[0:1, :] += jnp.sum(dy2 * u, axis=0, keepdims=True)
        dcw_ref[1:2, :] += jnp.sum(dy1 * u, axis=0, keepdims=True)
        dcw_ref[2:3, :] += jnp.sum(dy * u, axis=0, keepdims=True)

        dmix_scr[...] = _dot_nt(dx_next_ref[...].astype(BF16), wo_ref[...])

    small = lambda shape: pl.BlockSpec(shape, lambda s: (0, 0))
    blk = lambda w: pl.BlockSpec((ROWS, w), lambda s: (steps - 1 - s, 0))
    return _pallas_after(
        dep, body, sinks, proj, proj, bias, y, ao, dx1, dx1, w_o, conv_w, qg, kg, cog, aog, name=name, grid=(steps,),
        in_specs=[pl.BlockSpec(memory_space=pltpu.SMEM),
                  blk(IN_COLS),
                  pl.BlockSpec((BLK, KV_W), lambda s: (jnp.maximum(SUB * (steps - 1 - s) - 1, 0), O_K // KV_W)),
                  pl.BlockSpec((2, BLK, 2 * BLK), lambda s: (0, 0, 0)),
                  blk(CONV_CH), blk(ATTN_W),
                  pl.BlockSpec((ROWS, D_MODEL), lambda s: (steps - 1, 0)),
                  pl.BlockSpec((ROWS, D_MODEL), lambda s: (jnp.maximum(steps - 2 - s, 0), 0)),
                  small((D_MODEL, D_MODEL)),
                  small((3, CONV_CH)), small((1, PAIR)), small((1, PAIR)), small((1, CONV_CH)), small((1, ATTN_W))],
        out_specs=[blk(IN_COLS), small((3, CONV_CH)), small((1, PAIR)), small((1, PAIR)), small((1, N_Q)),
                   small((1, CONV_CH)), small((1, ATTN_W))],
        out_shape=[jax.ShapeDtypeStruct((S, IN_COLS), BF16), jax.ShapeDtypeStruct((3, CONV_CH), F32),
                   jax.ShapeDtypeStruct((1, PAIR), F32), jax.ShapeDtypeStruct((1, PAIR), F32),
                   jax.ShapeDtypeStruct((1, N_Q), F32), jax.ShapeDtypeStruct((1, CONV_CH), F32),
                   jax.ShapeDtypeStruct((1, ATTN_W), F32)],
        scratch_shapes=[pltpu.VMEM((8, CONV_CH), F32), pltpu.VMEM((BLK, PAIR), F32), pltpu.VMEM((BLK, PAIR), F32),
                        pltpu.VMEM((ROWS, D_MODEL), F32)],
        compiler_params=_cparams(("arbitrary",)),
    )


OTHER_CHIPS = ((1, 0), (0, 1), (1, 1))


def _position():
    return lax.axis_index("x"), lax.axis_index("y"), lax.axis_index("c")


def all_gather(shards, name):
    n = len(shards)

    def body(*refs):
        x_refs, out_refs = refs[:n], refs[n:2 * n]
        send_sems, recv_sems, local_sems = refs[2 * n:]
        x, y, c = _position()
        me, sibling = (x, y, c), (x, y, 1 - c)
        chips = [(x ^ mx, y ^ my) for mx, my in OTHER_CHIPS]

        def slab(a, px, py, pc):
            return out_refs[a].at[4 * px + 2 * py + pc]

        def copy(a, k, block, to, src=None):
            return pltpu.make_async_remote_copy(
                src_ref=slab(a, *block) if src is None else src, dst_ref=slab(a, *block),
                send_sem=send_sems.at[7 * a + k], recv_sem=recv_sems.at[7 * a + k], device_id=to, device_id_type=MESH_T)

        mine = [pltpu.make_async_copy(x_refs[a], slab(a, *me), local_sems.at[a]) for a in range(n)]
        for cp in mine:
            cp.start()
        first = []
        for a in range(n):
            first.append(copy(a, 0, me, sibling, src=x_refs[a]))
            first += [copy(a, 1 + j, me, (*chip, c), src=x_refs[a]) for j, chip in enumerate(chips)]
        for cp in first:
            cp.start()
        passed = []
        for a in range(n):
            for j, chip in enumerate(chips):
                copy(a, 1 + j, (*chip, c), me).wait_recv()
                passed.append(copy(a, 4 + j, (*chip, c), sibling))
                passed[-1].start()
        for a in range(n):
            copy(a, 0, sibling, me).wait_recv()
            for j, chip in enumerate(chips):
                copy(a, 4 + j, (*chip, 1 - c), me).wait_recv()
        for cp in first + passed:
            cp.wait_send()
        for cp in mine:
            cp.wait()

    return pl.pallas_call(
        body, name=name, in_specs=[ANY] * n, out_specs=[ANY] * n,
        out_shape=[jax.ShapeDtypeStruct((N_DEV, *s.shape), s.dtype) for s in shards],
        scratch_shapes=[pltpu.SemaphoreType.DMA((7 * n,)), pltpu.SemaphoreType.DMA((7 * n,)),
                        pltpu.SemaphoreType.DMA((n,))],
    )(*shards)


class SplitCopy:
    def __init__(self, name, arrays, n_copies, plan, after=None):
        n = len(arrays)
        self.name, self.n, self.n_copies, self.plan = name, n, n_copies, plan
        extra = [] if after is None else [after]

        def body(*refs):
            in_refs = refs[:n]
            send_sems, recv_sems = refs[n + len(extra)], refs[n + len(extra) + 1]
            token = refs[2 * n + len(extra) + 2]
            for k, (src, dst, to) in enumerate(plan(_position(), in_refs)):
                pltpu.make_async_remote_copy(src_ref=src, dst_ref=dst, send_sem=send_sems.at[k],
                                             recv_sem=recv_sems.at[k], device_id=to, device_id_type=MESH_T).start()
            token[...] = jnp.zeros_like(token)

        outs = pl.pallas_call(
            body, name=name + "_start",
            out_shape=(pltpu.SemaphoreType.DMA((n_copies,)), pltpu.SemaphoreType.DMA((n_copies,)),
                       *[pltpu.HBM(a.shape, a.dtype) for a in arrays], jax.ShapeDtypeStruct((8, 128), F32)),
            in_specs=[HBM] * n + [ANY] * len(extra),
            out_specs=(SEM, SEM, *[HBM] * n, pl.BlockSpec(memory_space=pltpu.VMEM)),
            input_output_aliases={i: 2 + i for i in range(n)},
            compiler_params=pltpu.CompilerParams(has_side_effects=pltpu.SideEffectType.DATAFLOW_SIDE_EFFECTING),
        )(*[pltpu.with_memory_space_constraint(a, pltpu.HBM) for a in arrays], *extra)
        self.send_sems, self.recv_sems = outs[0], outs[1]
        self.arrays, self.token = list(outs[2:2 + n]), outs[2 + n]

    def wait(self, after):
        n, plan = self.n, self.plan

        def body(*refs):
            in_refs, send_sems, recv_sems = refs[:n], refs[n], refs[n + 1]
            for k, (src, dst, to) in enumerate(plan(_position(), in_refs)):
                cp = pltpu.make_async_remote_copy(src_ref=src, dst_ref=dst, send_sem=send_sems.at[k],
                                                  recv_sem=recv_sems.at[k], device_id=to, device_id_type=MESH_T)
                cp.wait_send()
                cp.wait_recv()

        outs = pl.pallas_call(
            body, name=self.name + "_wait",
            out_shape=tuple(pltpu.HBM(a.shape, a.dtype) for a in self.arrays),
            in_specs=[HBM] * n + [SEM, SEM, ANY], out_specs=tuple([HBM] * n),
            input_output_aliases={i: i for i in range(n)},
            compiler_params=pltpu.CompilerParams(has_side_effects=pltpu.SideEffectType.DATAFLOW_SIDE_EFFECTING),
        )(*self.arrays, self.send_sems, self.recv_sems, after)
        return list(outs)


def gather_start(shards, me, name, after=None):
    n = len(shards)
    lands = [lax.dynamic_update_slice(lax.empty((N_DEV, *s.shape), s.dtype), s[None], (me, 0, 0)) for s in shards]

    def plan(pos, refs):
        x, y, c = pos
        return [(refs[a], refs[n + a].at[4 * x + 2 * y + c], (x ^ mx, y ^ my, c))
                for a in range(n) for mx, my in OTHER_CHIPS]

    return SplitCopy(name, list(shards) + lands, 3 * n, plan, after=after)


def sibling_start(lands, name):
    n = len(lands)

    def plan(pos, refs):
        x, y, c = pos
        return [(refs[a].at[2 * q + c], refs[a].at[2 * q + c], (x, y, 1 - c)) for a in range(n) for q in range(4)]

    return SplitCopy(name, list(lands), 4 * n, plan)


def scatter_start(slabs, name):
    n = len(slabs)
    lands = [lax.empty((N_DEV - 1, *g.shape[1:]), g.dtype) for g in slabs]

    def plan(pos, refs):
        x, y, c = pos
        copies = []
        for a in range(n):
            for r in range(1, N_DEV):
                px, py, pc = x ^ ((r >> 2) & 1), y ^ ((r >> 1) & 1), c ^ (r & 1)
                copies.append((refs[a].at[4 * px + 2 * py + pc], refs[n + a].at[r - 1], (px, py, pc)))
        return copies

    return SplitCopy(name, list(slabs) + lands, (N_DEV - 1) * n, plan)


def all_reduce_small(v, name):
    R, W = v.shape

    def body(v_ref, o_ref, recv, send_sems, recv_sems):
        x, y, c = _position()
        me = 4 * x + 2 * y + c
        copies = []
        for r in range(1, N_DEV):
            to = (x ^ ((r >> 2) & 1), y ^ ((r >> 1) & 1), c ^ (r & 1))
            copies.append(pltpu.make_async_remote_copy(
                src_ref=v_ref, dst_ref=recv.at[me], send_sem=send_sems.at[r - 1], recv_sem=recv_sems.at[r - 1],
                device_id=to, device_id_type=MESH_T))
        for cp in copies:
            cp.start()
        recv[pl.ds(me, 1)] = v_ref[...][None]
        for cp in copies:
            cp.wait()
        acc = recv[0]
        for s in range(1, N_DEV):
            acc = acc + recv[s]
        o_ref[...] = acc

    return pl.pallas_call(
        body, name=name,
        in_specs=[pl.BlockSpec(memory_space=pltpu.VMEM)], out_specs=pl.BlockSpec(memory_space=pltpu.VMEM),
        out_shape=jax.ShapeDtypeStruct((R, W), F32),
        scratch_shapes=[pltpu.VMEM((N_DEV, R, W), F32), pltpu.SemaphoreType.DMA((N_DEV - 1,)),
                        pltpu.SemaphoreType.DMA((N_DEV - 1,))],
    )(v)


def _row_tile(rows):
    if rows <= 512:
        return rows
    return max(t for t in range(8, 513, 8) if rows % t == 0)


def _adamw_update(w, g, m, v):
    mn = ADAM_B1 * m + (1.0 - ADAM_B1) * g
    vn = ADAM_B2 * v + (1.0 - ADAM_B2) * (g * g)
    m_hat = mn / (1.0 - ADAM_B1 ** ADAM_STEP)
    v_hat = vn / (1.0 - ADAM_B2 ** ADAM_STEP)
    return -ADAM_LR * (m_hat / (jnp.sqrt(v_hat) + ADAM_EPS) + ADAM_WD * w), mn, vn


def adamw(w, g, m, v, name):
    R, W = w.shape
    tr = _row_tile(R)

    def body(w_ref, g_ref, m_ref, v_ref, d_ref, mo_ref, vo_ref):
        d_ref[...], mo_ref[...], vo_ref[...] = _adamw_update(w_ref[...], g_ref[...], m_ref[...], v_ref[...])

    spec = pl.BlockSpec((tr, W), lambda i: (i, 0))
    return pl.pallas_call(
        body, name=name, grid=(R // tr,), in_specs=[spec] * 4, out_specs=[spec] * 3,
        out_shape=[jax.ShapeDtypeStruct((R, W), F32)] * 3,
        compiler_params=_cparams(("parallel",)),
    )(w, g, m, v)


def reduce_adamw(slabs, land, w, m, v, layer, me_arr, name, others=None):
    _, R, W = slabs.shape
    tr = max(t for t in range(16, R // 2 + 1, 16) if R % t == 0)
    n_other = 0 if others is None else 4

    def body(me_ref, s_ref, l_ref, w_ref, m_ref, v_ref, *rest):
        g_ref, d_ref, mo_ref, vo_ref = rest[n_other:]
        g = s_ref[0].astype(F32)
        for r in range(N_DEV - 1):
            g = g + l_ref[r].astype(F32)
        g_ref[0] = g
        d_ref[0], mo_ref[0], vo_ref[0] = _adamw_update(w_ref[0], g, m_ref[0], v_ref[0])

    spec = pl.BlockSpec((1, tr, W), lambda i, me: (layer, i, 0))
    return pl.pallas_call(
        body, name=name,
        grid_spec=pltpu.PrefetchScalarGridSpec(
            num_scalar_prefetch=1, grid=(R // tr,),
            in_specs=[pl.BlockSpec((1, tr, W), lambda i, me: (me[0], i, 0)),
                      pl.BlockSpec((N_DEV - 1, tr, W), lambda i, me: (0, i, 0)), spec, spec, spec] + [ANY] * n_other,
            out_specs=[spec] * 4),
        out_shape=[jax.ShapeDtypeStruct((DEPTH, R, W), F32)] * 4,
        input_output_aliases={6 + i: i for i in range(n_other)},
        compiler_params=_cparams(("parallel",)),
    )(me_arr, slabs, land, w, m, v, *([] if others is None else others))


SMALL_NAMES = ("norm1_g", "q_norm_g", "k_norm_g", "sinks", "conv_out_g", "attn_out_g", "norm2_g", "conv_w")
SMALL_SIZES = (D_MODEL, HEAD, HEAD, N_Q, CONV_CH, ATTN_W, D_MODEL, 3 * CONV_CH)
SMALL_ROWS = 80


def kernel(x, norm1_g, w_in, conv_w, q_norm_g, k_norm_g, sinks, conv_out_g, attn_out_g, w_o, norm2_g, w_gate, w_up, w_down, loss_target, m_norm1_g, m_w_in, m_conv_w, m_q_norm_g, m_k_norm_g, m_sinks, m_conv_out_g, m_attn_out_g, m_w_o, m_norm2_g, m_w_gate, m_w_up, m_w_down, v_norm1_g, v_w_in, v_conv_w, v_q_norm_g, v_k_norm_g, v_sinks, v_conv_out_g, v_attn_out_g, v_w_o, v_norm2_g, v_w_gate, v_w_up, v_w_down):
    xi, yi, ci = _position()
    me = 4 * xi + 2 * yi + ci
    me_arr = jnp.reshape(me, (1,)).astype(jnp.int32)
    xs, tgt = x[0], loss_target[0]
    bf = lambda a: a.astype(BF16)
    bias = band_bias()

    t = lambda a: jnp.swapaxes(a, 1, 2)
    shard = dict(w_in=(t(w_in), t(m_w_in), t(v_w_in)), w_o=(w_o, m_w_o, v_w_o),
                 w_gate=(t(w_gate), t(m_w_gate), t(v_w_gate)), w_up=(t(w_up), t(m_w_up), t(v_w_up)),
                 w_down=(w_down, m_w_down, v_w_down))
    wb = {n: bf(shard[n][0]) for n in shard}

    g_in0, g_o0, g_conv = all_gather([wb["w_in"][0], wb["w_o"][0], conv_w.reshape(DEPTH * 3, HEAD)], "gather_first")
    ag_ffn0 = gather_start([wb["w_gate"][0], wb["w_up"][0], wb["w_down"][0]], me, "gather_ffn0", after=g_in0)
    ag_mix1 = gather_start([wb["w_in"][1], wb["w_o"][1]], me, "gather_mix1", after=ag_ffn0.token)
    ag_ffn1 = gather_start([wb["w_gate"][1], wb["w_up"][1], wb["w_down"][1]], me, "gather_ffn1", after=ag_mix1.token)
    conv_full = g_conv.reshape(N_DEV, DEPTH, 3, HEAD).transpose(1, 2, 0, 3).reshape(DEPTH, 3, CONV_CH)
    pair_gain = lambda g: jnp.tile(g[None], (1, 2))
    small = [dict(norm1_g=norm1_g[l][None], conv_w=conv_full[l], q_norm_g=pair_gain(q_norm_g[l]),
                  k_norm_g=pair_gain(k_norm_g[l]), sinks=sinks[l], conv_out_g=conv_out_g[l][None],
                  attn_out_g=attn_out_g[l][None], norm2_g=norm2_g[l][None]) for l in range(DEPTH)]
    whole = lambda g: g.reshape(-1, D_MODEL)
    weights = [dict(w_in=whole(g_in0), w_o=whole(g_o0)), {}]

    def ffn_weights(g_gate, g_up, g_down):
        return dict(w_gate=whole(g_gate), w_up=whole(g_up), w_down=whole(g_down))

    saved = []
    xl = xs
    for l in range(DEPTH):
        sp, wl = small[l], weights[l]
        h, proj = norm_proj(xl, sp["norm1_g"], wl["w_in"], f"norm_proj{l}",
                            dep=ag_ffn1.token if l == 0 else pass_ffn1.token)
        mixer_args = (proj, bias, sp["sinks"], sp["conv_w"], sp["q_norm_g"], sp["k_norm_g"], sp["conv_out_g"],
                      sp["attn_out_g"], f"mixer_fwd{l}")
        if l == 0:
            mix, y, ao = mixer_fwd(*mixer_args)
            pass_ffn0 = sibling_start(ag_ffn0.wait(mix)[3:], "pass_ffn0")
            x1 = matmul_residual(mix, wl["w_o"], xl, "out_proj0", dep=pass_ffn0.token)
            wl.update(ffn_weights(*pass_ffn0.wait(x1)))
            pass_mix1 = sibling_start(ag_mix1.wait(x1)[2:], "pass_mix1")
            h2, a, b, f, x2 = ffn_fwd(x1, sp["norm2_g"], wl["w_gate"], wl["w_up"], wl["w_down"], "ffn_fwd0",
                                      dep=pass_mix1.token)
            g_in, g_o = pass_mix1.wait(x2)
            weights[1] = dict(w_in=whole(g_in), w_o=whole(g_o))
            pass_ffn1 = sibling_start(ag_ffn1.wait(x2)[3:], "pass_ffn1")
        else:
            mix, y, ao, x1 = mixer_fwd(*mixer_args, residual=xl, w_o=wl["w_o"])
            wl.update(ffn_weights(*pass_ffn1.wait(x1)))
            h2, a, b, f, dx, loss_row = ffn_fwd(x1, sp["norm2_g"], wl["w_gate"], wl["w_up"], wl["w_down"], "ffn_fwd1",
                                                tgt=tgt)
            x2 = None
        saved.append((xl, h, proj, mix, y, ao, x1, h2, a, b, f))
        xl = x2

    stepped = {n: None for n in shard}
    slabs = lambda d: d.reshape(N_DEV, -1, D_MODEL)
    gsmall = [None] * DEPTH

    def finish(sc, names, after, l):
        arrays = sc.wait(after)
        k = len(names)
        for i, n in enumerate(names):
            w, m, v = shard[n]
            stepped[n] = reduce_adamw(arrays[i], arrays[k + i], w, m, v, l, me_arr, f"reduce_adamw_{n}{l}",
                                      others=stepped[n])

    for l in reversed(range(DEPTH)):
        sp, wl = small[l], weights[l]
        x0, h, proj, mix, y, ao, x1, h2, a, b, f = saved[l]
        (d_wd,) = grad_weight((f,), dx, TF, f"grad_w_down{l}", tk=2048)
        sc_down = scatter_start([slabs(d_wd)], f"scatter_w_down{l}")
        da, db, dx1, d_g2 = ffn_bwd(dx, a, b, x1, sp["norm2_g"], wl["w_gate"], wl["w_up"], wl["w_down"],
                                    f"ffn_bwd{l}", dep=sc_down.token)
        d_wg, d_wu = grad_weight((da, db), h2, TF, f"grad_w_gate_up{l}")
        (d_wo,) = grad_weight((mix,), dx1, D_MODEL, f"grad_w_o{l}", tk=2048)
        sc_rest = scatter_start([slabs(d_wg), slabs(d_wu), slabs(d_wo)], f"scatter_w_gate_up_o{l}")
        dproj, d_cw, d_qg, d_kg, d_sk, d_cog, d_aog = mixer_bwd(
            proj, bias, y, ao, dx1, wl["w_o"], sp["sinks"], sp["conv_w"], sp["q_norm_g"], sp["k_norm_g"],
            sp["conv_out_g"], sp["attn_out_g"], f"mixer_bwd{l}", dep=sc_rest.token)
        finish(sc_down, ["w_down"], dproj, l)
        finish(sc_rest, ["w_gate", "w_up", "w_o"], dproj, l)
        (d_win,) = grad_weight((dproj,), h, IN_COLS // 2, f"grad_w_in{l}", tk=2048)
        sc_in = scatter_start([slabs(d_win)], f"scatter_w_in{l}")
        dx, d_g1 = proj_bwd(dproj, wl["w_in"], x0, sp["norm1_g"], dx1, f"proj_bwd{l}", dep=sc_in.token)
        finish(sc_in, ["w_in"], dx, l)
        both_heads = lambda d: d[:, :HEAD] + d[:, HEAD:]
        gsmall[l] = dict(norm1_g=d_g1, conv_w=d_cw, q_norm_g=both_heads(d_qg), k_norm_g=both_heads(d_kg), sinks=d_sk,
                         conv_out_g=d_cog, attn_out_g=d_aog, norm2_g=d_g2)
    grad_x = dx[None]

    flat = jnp.concatenate([loss_row[0, 0:1]] + [gsmall[l][n].reshape(-1) for l in range(DEPTH) for n in SMALL_NAMES])
    flat = jnp.pad(flat, (0, SMALL_ROWS * 128 - flat.shape[0])).reshape(SMALL_ROWS, 128)
    flat = all_reduce_small(flat, "all_reduce_small_grads").reshape(-1)
    loss = flat[0]
    gs = {n: [] for n in SMALL_NAMES}
    off = 1
    for l in range(DEPTH):
        for n, size in zip(SMALL_NAMES, SMALL_SIZES):
            gs[n].append(flat[off:off + size])
            off += size
    gs = {n: jnp.stack(v) for n, v in gs.items()}
    g_conv = lax.dynamic_slice(gs["conv_w"].reshape(DEPTH, 3, CONV_CH), (0, 0, me * HEAD), (DEPTH, 3, HEAD))

    gs["conv_w"] = g_conv
    params = dict(norm1_g=(norm1_g, m_norm1_g, v_norm1_g), conv_w=(conv_w, m_conv_w, v_conv_w),
                  q_norm_g=(q_norm_g, m_q_norm_g, v_q_norm_g), k_norm_g=(k_norm_g, m_k_norm_g, v_k_norm_g),
                  sinks=(sinks, m_sinks, v_sinks), conv_out_g=(conv_out_g, m_conv_out_g, v_conv_out_g),
                  attn_out_g=(attn_out_g, m_attn_out_g, v_attn_out_g), norm2_g=(norm2_g, m_norm2_g, v_norm2_g))
    names = ("norm1_g", "w_in", "conv_w", "q_norm_g", "k_norm_g", "sinks", "conv_out_g", "attn_out_g", "w_o",
             "norm2_g", "w_gate", "w_up", "w_down")

    out = {}
    for n in names:
        if n in shard:
            out[n] = tuple(t(r) for r in stepped[n]) if n in ("w_in", "w_gate", "w_up") else stepped[n]
        else:
            w, m, v = params[n]
            two_d = (-1, w.shape[-1])
            d, mn, vn = adamw(w.reshape(two_d), gs[n].reshape(two_d), m.reshape(two_d), v.reshape(two_d), f"adamw_{n}")
            out[n] = (gs[n].reshape(w.shape), d.reshape(w.shape), mn.reshape(w.shape), vn.reshape(w.shape))
    return (loss, grad_x, *[out[n][i] for i in range(4) for n in names])
```

```python
import jax
import jax.numpy as jnp
from jax import lax
from jax.experimental import pallas as pl
from jax.experimental.pallas import tpu as pltpu

F32 = jnp.float32
BF16 = jnp.bfloat16

D_MODEL = 1024
CONV_CH = 512
ATTN_W = 512
N_Q = 8
N_KV = 2
GRP = N_Q // N_KV
HEAD = 64
IN_COLS = 2304
D_FF = 2816
BLK = 128
O_Q = 3 * CONV_CH
O_K = O_Q + ATTN_W
O_V = O_K + N_KV * HEAD
EPS = 1e-6
NEG_INF = -1e30
SCALE = HEAD ** -0.5
N_DEV = 8
DEPTH = 2

ADAM_LR = 0.001
ADAM_B1 = 0.9
ADAM_B2 = 0.999
ADAM_EPS = 1e-08
ADAM_WD = 0.01
ADAM_STEP = 10

VMEM_LIMIT = 56 * 1024 * 1024
TM = 512
MESH_T = pl.DeviceIdType.MESH

ANY = pl.BlockSpec(memory_space=pl.ANY)
HBM = pl.BlockSpec(memory_space=pltpu.HBM)
SEM = pl.BlockSpec(memory_space=pltpu.SEMAPHORE)


def _cparams(sem):
    return pltpu.CompilerParams(dimension_semantics=sem, vmem_limit_bytes=VMEM_LIMIT)


def _pallas_after(dep, body, *args, in_specs, **kw):
    if dep is None:
        return pl.pallas_call(body, in_specs=in_specs, **kw)(*args)

    def after_dep(dep_ref, *refs):
        body(*refs)

    return pl.pallas_call(after_dep, in_specs=[ANY, *in_specs], **kw)(dep, *args)


def _dot(a, b):
    return jnp.dot(a, b, preferred_element_type=F32)


def _dot_nt(a, b):
    return lax.dot_general(a, b, (((1,), (1,)), ((), ())), preferred_element_type=F32)


def _dot_tn(a, b):
    return lax.dot_general(a, b, (((0,), (0,)), ((), ())), preferred_element_type=F32)


LANES = 128


def _row_reduce(v, op, reduce):
    w = v.shape[-1]
    if w > LANES and w % LANES == 0:
        acc = v[:, 0:LANES]
        for c in range(1, w // LANES):
            acc = op(acc, v[:, LANES * c:LANES * (c + 1)])
        v = acc
    return reduce(v, axis=-1, keepdims=True)


def _row_sum(v):
    return _row_reduce(v, jnp.add, jnp.sum)


def _row_mean(v):
    return _row_sum(v) * (1.0 / v.shape[-1])


def _rms(v):
    return lax.rsqrt(_row_mean(v * v) + EPS)


def _sigmoid(v):
    return 1.0 / (1.0 + jnp.exp(-v))


def _rms_bwd(dyv, xh, r, g):
    dxh = dyv * g
    return r * (dxh - xh * _row_mean(dxh * xh))


def norm_proj(x, g, w_t, name, dep=None):
    S, N = x.shape[0], w_t.shape[0]
    tm = 2 * TM

    def body(x_ref, g_ref, w_ref, h_ref, p_ref):
        xf = x_ref[...]
        h = ((xf * _rms(xf)) * g_ref[...]).astype(BF16)
        h_ref[...] = h
        p_ref[...] = _dot_nt(h, w_ref[...]).astype(BF16)

    return _pallas_after(
        dep, body, x, g, w_t, name=name, grid=(S // tm,),
        in_specs=[pl.BlockSpec((tm, D_MODEL), lambda i: (i, 0)),
                  pl.BlockSpec((1, D_MODEL), lambda i: (0, 0)),
                  pl.BlockSpec((N, D_MODEL), lambda i: (0, 0))],
        out_specs=[pl.BlockSpec((tm, D_MODEL), lambda i: (i, 0)),
                   pl.BlockSpec((tm, N), lambda i: (i, 0))],
        out_shape=[jax.ShapeDtypeStruct((S, D_MODEL), BF16), jax.ShapeDtypeStruct((S, N), BF16)],
        compiler_params=_cparams(("parallel",)),
    )


def band_bias():
    qi = lax.broadcasted_iota(jnp.int32, (BLK, 2 * BLK), 0)
    kj = lax.broadcasted_iota(jnp.int32, (BLK, 2 * BLK), 1)
    diff = qi + BLK - kj
    valid = (diff >= 0) & (diff < BLK)
    return jnp.stack([jnp.where(valid & (kj >= BLK), 0.0, NEG_INF), jnp.where(valid, 0.0, NEG_INF)]).astype(F32)


def _softmax_with_sink(s, sink):
    m = jnp.maximum(_row_reduce(s, jnp.maximum, jnp.max), sink)
    p = jnp.exp(s - m)
    es = jnp.exp(sink - m)
    inv = 1.0 / (_row_sum(p) + es)
    return p * inv, es * inv


PAIR = 2 * HEAD


def _low_half():
    return lax.broadcasted_iota(jnp.int32, (1, PAIR), 1) < HEAD


def _half_sums(v, low):
    return (jnp.sum(jnp.where(low, v, 0.0), axis=-1, keepdims=True),
            jnp.sum(jnp.where(low, 0.0, v), axis=-1, keepdims=True))


def _pair_mean(v, low):
    e, o = _half_sums(v, low)
    return jnp.where(low, e, o) * (1.0 / HEAD)


def _pair_rms(v, low):
    return lax.rsqrt(_pair_mean(v * v, low) + EPS)


def _one_head_in_both_halves(pair, low):
    swapped = pltpu.roll(pair, HEAD, 1)
    return jnp.where(low, pair, swapped), jnp.where(low, swapped, pair)


SUB_FWD, SUB_FWD_PROJECT, SUB_BWD = 2, 8, 4


def _kv_pairs(p_ref, kvp_ref):
    k = jnp.concatenate([kvp_ref[:, 0:PAIR], p_ref[:, O_K:O_K + PAIR]], axis=0).astype(F32)
    v = jnp.concatenate([kvp_ref[:, PAIR:2 * PAIR], p_ref[:, O_V:O_V + PAIR]], axis=0)
    return k, v


def _band(s):
    return slice(BLK * s, BLK * s + 2 * BLK)


def _block(s):
    return slice(BLK * s, BLK * (s + 1))


def mixer_fwd(proj, bias, sinks, conv_w, qg, kg, cog, aog, name, residual=None, w_o=None):
    S = proj.shape[0]
    project = w_o is not None
    SUB = SUB_FWD_PROJECT if project else SUB_FWD
    ROWS = SUB * BLK
    steps = S // ROWS

    def body(sinks_ref, p_ref, kvp_ref, bias_ref, cw_ref, qg_ref, kg_ref, cog_ref, aog_ref, *rest):
        if project:
            x_ref, wo_ref, mix_ref, y_ref, ao_ref, x1_ref, ucar = rest
        else:
            mix_ref, y_ref, ao_ref, ucar = rest
        n = pl.program_id(0)

        @pl.when(n == 0)
        def _():
            ucar[...] = jnp.zeros_like(ucar)

        bg = p_ref[:, 0:CONV_CH].astype(F32)
        u = p_ref[:, CONV_CH:2 * CONV_CH].astype(F32) * p_ref[:, 2 * CONV_CH:3 * CONV_CH].astype(F32)
        row = lax.broadcasted_iota(jnp.int32, (ROWS, CONV_CH), 0)
        prev = ucar[...]
        u1 = jnp.where(row == 0, prev[7:8, :], pltpu.roll(u, 1, 0))
        u2 = jnp.where(row == 0, prev[6:7, :], jnp.where(row == 1, prev[7:8, :], pltpu.roll(u, 2, 0)))
        ucar[...] = u[ROWS - 8:ROWS, :]
        y = cw_ref[0:1, :] * u2 + cw_ref[1:2, :] * u1 + cw_ref[2:3, :] * u
        y_ref[...] = y.astype(BF16)
        co = bg * y
        mix_conv = ((co * _rms(co)) * cog_ref[...]).astype(BF16)
        mix_ref[:, 0:CONV_CH] = mix_conv
        if project:
            x1 = x_ref[...] + _dot(mix_conv, wo_ref[0:CONV_CH, :])

        low = _low_half()
        q_gain = qg_ref[...] * SCALE
        k, v = _kv_pairs(p_ref, kvp_ref)
        kn = ((k * _pair_rms(k, low)) * kg_ref[...]).astype(BF16)
        k_of = _one_head_in_both_halves(kn, low)
        v_of = _one_head_in_both_halves(v, low)
        biases = [bias_ref[jnp.minimum(n, 1)]] + [bias_ref[1]] * (SUB - 1)
        units = [(s, hq) for s in range(SUB) for hq in range(N_Q)]
        q_one = []
        for pair in range(N_Q // 2):
            q = p_ref[:, O_Q + PAIR * pair:O_Q + PAIR * (pair + 1)].astype(F32)
            qn = ((q * _pair_rms(q, low)) * q_gain).astype(BF16)
            q_one += [jnp.where(mine, qn, jnp.zeros_like(qn)) for mine in (low, jnp.logical_not(low))]
        scores = [_dot_nt(q_one[hq][_block(s)], k_of[hq // GRP][_band(s)]) + biases[s] for s, hq in units]
        probs = [_softmax_with_sink(sc, sinks_ref[hq])[0].astype(BF16) for (s, hq), sc in zip(units, scores)]
        o = {u_: _dot(pn, v_of[u_[1] // GRP][_band(u_[0])]) for u_, pn in zip(units, probs)}
        ao = jnp.concatenate(
            [jnp.concatenate([jnp.where(low, o[s, 2 * pair], o[s, 2 * pair + 1]) for pair in range(N_Q // 2)], axis=1)
             for s in range(SUB)], axis=0)
        ao_ref[...] = ao.astype(BF16)
        mix_attn = ((ao * _rms(ao)) * aog_ref[...]).astype(BF16)
        mix_ref[:, CONV_CH:] = mix_attn
        if project:
            x1_ref[...] = x1 + _dot(mix_attn, wo_ref[CONV_CH:, :])

    small = lambda shape: pl.BlockSpec(shape, lambda n: (0, 0))
    rows = lambda w: pl.BlockSpec((ROWS, w), lambda n: (n, 0))
    return pl.pallas_call(
        body, name=name, grid=(steps,),
        in_specs=[pl.BlockSpec(memory_space=pltpu.SMEM),
                  rows(IN_COLS),
                  pl.BlockSpec((BLK, 2 * PAIR), lambda n: (jnp.maximum(SUB * n - 1, 0), O_K // (2 * PAIR))),
                  pl.BlockSpec((2, BLK, 2 * BLK), lambda n: (0, 0, 0)),
                  small((3, CONV_CH)), small((1, PAIR)), small((1, PAIR)), small((1, CONV_CH)), small((1, ATTN_W))]
        + ([rows(D_MODEL), small((D_MODEL, D_MODEL))] if project else []),
        out_specs=[rows(D_MODEL), rows(CONV_CH), rows(ATTN_W)] + ([rows(D_MODEL)] if project else []),
        out_shape=[jax.ShapeDtypeStruct((S, D_MODEL), BF16), jax.ShapeDtypeStruct((S, CONV_CH), BF16),
                   jax.ShapeDtypeStruct((S, ATTN_W), BF16)]
        + ([jax.ShapeDtypeStruct((S, D_MODEL), F32)] if project else []),
        scratch_shapes=[pltpu.VMEM((8, CONV_CH), F32)],
        compiler_params=_cparams(("arbitrary",)),
    )(sinks, proj, proj, bias, conv_w, qg, kg, cog, aog, *([residual, w_o] if project else []))


def matmul_residual(a, w, res, name, dep=None):
    S, K = a.shape
    N = w.shape[1]

    def body(a_ref, w_ref, r_ref, o_ref):
        o_ref[...] = r_ref[...] + _dot(a_ref[...], w_ref[...])

    return _pallas_after(
        dep, body, a, w, res, name=name, grid=(S // TM,),
        in_specs=[pl.BlockSpec((TM, K), lambda i: (i, 0)), pl.BlockSpec((K, N), lambda i: (0, 0)),
                  pl.BlockSpec((TM, N), lambda i: (i, 0))],
        out_specs=pl.BlockSpec((TM, N), lambda i: (i, 0)),
        out_shape=jax.ShapeDtypeStruct((S, N), F32),
        compiler_params=_cparams(("parallel",)),
    )


TF = 1408


def ffn_fwd(x1, g, wg, wu, wd, name, dep=None, tgt=None):
    S = x1.shape[0]
    ni = S // TM
    with_loss = tgt is not None
    row = pl.BlockSpec((TM, D_MODEL), lambda i: (i, 0))
    vec = pl.BlockSpec((1, D_MODEL), lambda i: (0, 0))
    acts = [jax.ShapeDtypeStruct((S, D_FF), BF16)] * 3

    def half_weights(j):
        return [pl.BlockSpec((TF, D_MODEL), lambda i: (j, 0))] * 3

    def swiglu(h2, wg_ref, wu_ref, a_ref, b_ref, f_ref):
        a = _dot_nt(h2, wg_ref[...])
        b = _dot_nt(h2, wu_ref[...])
        a_ref[...] = a.astype(BF16)
        b_ref[...] = b.astype(BF16)
        f = ((a * _sigmoid(a)) * b).astype(BF16)
        f_ref[...] = f
        return f

    def first(x_ref, g_ref, wg_ref, wu_ref, wd_ref, h2_ref, a_ref, b_ref, f_ref, part_ref):
        xf = x_ref[...]
        h2 = ((xf * _rms(xf)) * g_ref[...]).astype(BF16)
        h2_ref[...] = h2
        part_ref[...] = xf + _dot(swiglu(h2, wg_ref, wu_ref, a_ref, b_ref, f_ref), wd_ref[...])

    h2, a, b, f, part = _pallas_after(
        dep, first, x1, g, wg, wu, wd, name=name + "_a", grid=(ni,),
        in_specs=[row, vec] + half_weights(0),
        out_specs=[row] + [pl.BlockSpec((TM, TF), lambda i: (i, 0))] * 3 + [row],
        out_shape=[jax.ShapeDtypeStruct((S, D_MODEL), BF16)] + acts + [jax.ShapeDtypeStruct((S, D_MODEL), F32)],
        compiler_params=_cparams(("parallel",)),
    )

    def second(h2_ref, part_ref, wg_ref, wu_ref, wd_ref, *rest):
        rest = rest[3:]
        if with_loss:
            t_ref, a_ref, b_ref, f_ref, o_ref, l_ref, sq = rest
        else:
            a_ref, b_ref, f_ref, o_ref = rest
        i = pl.program_id(0)
        out = part_ref[...] + _dot(swiglu(h2_ref[...], wg_ref, wu_ref, a_ref, b_ref, f_ref), wd_ref[...])
        if not with_loss:
            o_ref[...] = out
            return
        e = out - t_ref[...]
        o_ref[...] = e * (1.0 / D_MODEL)
        col = jnp.sum(e * e, axis=0, keepdims=True)

        @pl.when(i == 0)
        def _():
            sq[...] = col

        @pl.when(i > 0)
        def _():
            sq[...] += col

        @pl.when(i == ni - 1)
        def _():
            l_ref[...] = jnp.full((1, 128), jnp.sum(sq[...]) * (0.5 / D_MODEL), F32)

    results = pl.pallas_call(
        second, name=name + "_b", grid=(ni,),
        in_specs=[row, row] + half_weights(1) + [ANY] * 3 + ([row] if with_loss else []),
        out_specs=[pl.BlockSpec((TM, TF), lambda i: (i, 1))] * 3 + [row]
        + ([pl.BlockSpec((1, 128), lambda i: (0, 0))] if with_loss else []),
        out_shape=acts + [jax.ShapeDtypeStruct((S, D_MODEL), F32)]
        + ([jax.ShapeDtypeStruct((1, 128), F32)] if with_loss else []),
        input_output_aliases={5: 0, 6: 1, 7: 2},
        scratch_shapes=[pltpu.VMEM((1, D_MODEL), F32)] if with_loss else [],
        compiler_params=_cparams(("arbitrary",) if with_loss else ("parallel",)),
    )(h2, part, wg, wu, wd, a, b, f, *([tgt] if with_loss else []))
    return (h2, *results)


def ffn_bwd(dx2, a, b, x1, g, wg, wu, wd, name, dep=None):
    S = dx2.shape[0]
    ni = S // TM
    row = pl.BlockSpec((TM, D_MODEL), lambda i: (i, 0))
    vec = pl.BlockSpec((1, D_MODEL), lambda i: (0, 0))
    acts = [jax.ShapeDtypeStruct((S, D_FF), BF16)] * 2

    def half(j):
        return [pl.BlockSpec((TM, TF), lambda i: (i, j))] * 2, [pl.BlockSpec((TF, D_MODEL), lambda i: (j, 0))] * 3

    def through_half(dx_ref, a_ref, b_ref, wg_ref, wu_ref, wd_ref, da_ref, db_ref):
        df = _dot_nt(dx_ref[...].astype(BF16), wd_ref[...])
        av = a_ref[...].astype(F32)
        bv = b_ref[...].astype(F32)
        sg = _sigmoid(av)
        da = ((df * bv) * (sg * (1.0 + av * (1.0 - sg)))).astype(BF16)
        db = (df * (av * sg)).astype(BF16)
        da_ref[...] = da
        db_ref[...] = db
        return _dot(da, wg_ref[...]) + _dot(db, wu_ref[...])

    def first(dx_ref, a_ref, b_ref, wg_ref, wu_ref, wd_ref, da_ref, db_ref, part_ref):
        part_ref[...] = through_half(dx_ref, a_ref, b_ref, wg_ref, wu_ref, wd_ref, da_ref, db_ref)

    tiles, weights = half(0)
    da, db, part = _pallas_after(
        dep, first, dx2, a, b, wg, wu, wd, name=name + "_a", grid=(ni,),
        in_specs=[row] + tiles + weights, out_specs=tiles + [row],
        out_shape=acts + [jax.ShapeDtypeStruct((S, D_MODEL), F32)],
        compiler_params=_cparams(("parallel",)),
    )

    def second(dx_ref, a_ref, b_ref, wg_ref, wu_ref, wd_ref, part_ref, x_ref, g_ref, da_prev, db_prev,
               da_ref, db_ref, dx1_ref, dg_ref):
        i = pl.program_id(0)

        @pl.when(i == 0)
        def _():
            dg_ref[...] = jnp.zeros_like(dg_ref)

        dh = part_ref[...] + through_half(dx_ref, a_ref, b_ref, wg_ref, wu_ref, wd_ref, da_ref, db_ref)
        xf = x_ref[...]
        r = _rms(xf)
        xh = xf * r
        dg_ref[...] += jnp.sum(dh * xh, axis=0, keepdims=True)
        dx1_ref[...] = dx_ref[...] + _rms_bwd(dh, xh, r, g_ref[...])

    tiles, weights = half(1)
    return pl.pallas_call(
        second, name=name + "_b", grid=(ni,),
        in_specs=[row] + tiles + weights + [row, row, vec, ANY, ANY],
        out_specs=tiles + [row, vec],
        out_shape=acts + [jax.ShapeDtypeStruct((S, D_MODEL), F32), jax.ShapeDtypeStruct((1, D_MODEL), F32)],
        input_output_aliases={9: 0, 10: 1},
        compiler_params=_cparams(("arbitrary",)),
    )(dx2, a, b, wg, wu, wd, part, x1, g, da, db)


def grad_weight(lhs, rhs, tm, name, tk=1024):
    S, N = rhs.shape
    M = lhs[0].shape[1]
    tk = min(tk, S)
    nk = S // tk
    n_l = len(lhs)

    def body(*refs):
        l_refs, r_ref = refs[:n_l], refs[n_l]
        o_refs, accs = refs[n_l + 1:2 * n_l + 1], refs[2 * n_l + 1:]
        k = pl.program_id(1)
        rv = r_ref[...].astype(BF16)
        for l_ref, o_ref, acc in zip(l_refs, o_refs, accs):
            @pl.when(k == 0)
            def _():
                acc[...] = jnp.zeros_like(acc)

            acc[...] += _dot_tn(l_ref[...], rv)

            @pl.when(k == nk - 1)
            def _():
                o_ref[...] = acc[...].astype(BF16)

    return pl.pallas_call(
        body, name=name, grid=(M // tm, nk),
        in_specs=[pl.BlockSpec((tk, tm), lambda i, k: (k, i))] * n_l + [pl.BlockSpec((tk, N), lambda i, k: (k, 0))],
        out_specs=[pl.BlockSpec((tm, N), lambda i, k: (i, 0))] * n_l,
        out_shape=[jax.ShapeDtypeStruct((M, N), BF16)] * n_l,
        scratch_shapes=[pltpu.VMEM((tm, N), F32)] * n_l,
        compiler_params=_cparams(("parallel", "arbitrary")),
    )(*lhs, rhs)


def proj_bwd(dproj, w, x, g, dres, name, dep=None):
    S, N = dproj.shape

    def body(dp_ref, w_ref, x_ref, g_ref, dr_ref, dx_ref, dg_ref):
        i = pl.program_id(0)

        @pl.when(i == 0)
        def _():
            dg_ref[...] = jnp.zeros_like(dg_ref)

        dh = _dot(dp_ref[...], w_ref[...])
        xf = x_ref[...]
        r = _rms(xf)
        xh = xf * r
        dg_ref[...] += jnp.sum(dh * xh, axis=0, keepdims=True)
        dx_ref[...] = dr_ref[...] + _rms_bwd(dh, xh, r, g_ref[...])

    return _pallas_after(
        dep, body, dproj, w, x, g, dres, name=name, grid=(S // TM,),
        in_specs=[pl.BlockSpec((TM, N), lambda i: (i, 0)), pl.BlockSpec((N, D_MODEL), lambda i: (0, 0)),
                  pl.BlockSpec((TM, D_MODEL), lambda i: (i, 0)), pl.BlockSpec((1, D_MODEL), lambda i: (0, 0)),
                  pl.BlockSpec((TM, D_MODEL), lambda i: (i, 0))],
        out_specs=[pl.BlockSpec((TM, D_MODEL), lambda i: (i, 0)), pl.BlockSpec((1, D_MODEL), lambda i: (0, 0))],
        out_shape=[jax.ShapeDtypeStruct((S, D_MODEL), F32), jax.ShapeDtypeStruct((1, D_MODEL), F32)],
        compiler_params=_cparams(("arbitrary",)),
    )


def mixer_bwd(proj, bias, y, ao, dx1, w_o, sinks, conv_w, qg, kg, cog, aog, name, dep=None):
    S = proj.shape[0]
    SUB, ROWS = SUB_BWD, SUB_BWD * BLK
    steps = S // ROWS
    KV_W = 2 * PAIR

    def body(sinks_ref, p_ref, kvp_ref, bias_ref, y_ref, ao_ref, dx_first_ref, dx_next_ref, wo_ref, cw_ref, qg_ref,
             kg_ref, cog_ref, aog_ref, dp_ref, dcw_ref, dqg_ref, dkg_ref, dsk_ref, dcog_ref, daog_ref,
             dycar, kcar, vcar, dmix_scr):
        step = pl.program_id(0)

        @pl.when(step == 0)
        def _():
            dmix_scr[...] = _dot_nt(dx_first_ref[...].astype(BF16), wo_ref[...])
            dycar[...] = jnp.zeros_like(dycar)
            kcar[...] = jnp.zeros_like(kcar)
            vcar[...] = jnp.zeros_like(vcar)
            dcw_ref[...] = jnp.zeros_like(dcw_ref)
            dqg_ref[...] = jnp.zeros_like(dqg_ref)
            dkg_ref[...] = jnp.zeros_like(dkg_ref)
            dsk_ref[...] = jnp.zeros_like(dsk_ref)
            dcog_ref[...] = jnp.zeros_like(dcog_ref)
            daog_ref[...] = jnp.zeros_like(daog_ref)

        dma = dmix_scr[:, CONV_CH:]

        aov = ao_ref[...].astype(F32)
        ra = _rms(aov)
        ah = aov * ra
        daog_ref[...] += jnp.sum(dma * ah, axis=0, keepdims=True)
        dao = _rms_bwd(dma, ah, ra, aog_ref[...])

        low = _low_half()
        high = jnp.logical_not(low)
        q_gain = qg_ref[...] * SCALE
        k, v = _kv_pairs(p_ref, kvp_ref)
        rk = _pair_rms(k, low)
        kh = k * rk
        k_of = _one_head_in_both_halves((kh * kg_ref[...]).astype(BF16), low)
        v_of = _one_head_in_both_halves(v, low)
        lane8 = lax.broadcasted_iota(jnp.int32, (1, N_Q), 1)
        dsk = jnp.zeros((1, N_Q), F32)
        dqg = jnp.zeros((1, PAIR), F32)
        biases = [bias_ref[jnp.minimum(steps - 1 - step, 1)]] + [bias_ref[1]] * (SUB - 1)
        units = [(s, hq) for s in range(SUB) for hq in range(N_Q)]
        rq, qh, q_one, do_one, delta = [], [], [], [], []
        for pair in range(N_Q // 2):
            cols = slice(PAIR * pair, PAIR * (pair + 1))
            q = p_ref[:, O_Q + PAIR * pair:O_Q + PAIR * (pair + 1)].astype(F32)
            rq.append(_pair_rms(q, low))
            qh.append(q * rq[pair])
            qn = (qh[pair] * q_gain).astype(BF16)
            do = dao[:, cols]
            do_b = do.astype(BF16)
            delta += _half_sums(do * aov[:, cols], low)
            for mine in (low, high):
                q_one.append(jnp.where(mine, qn, jnp.zeros_like(qn)))
                do_one.append(jnp.where(mine, do_b, jnp.zeros_like(do_b)))
        scores = {u: _dot_nt(q_one[u[1]][_block(u[0])], k_of[u[1] // GRP][_band(u[0])]) + biases[u[0]] for u in units}
        dprobs = {u: _dot_nt(do_one[u[1]][_block(u[0])], v_of[u[1] // GRP][_band(u[0])]) for u in units}
        probs = {u: _softmax_with_sink(scores[u], sinks_ref[u[1]]) for u in units}
        ds = {u: (probs[u][0] * (dprobs[u] - delta[u[1]][_block(u[0])])).astype(BF16) for u in units}
        dv_t = {(s, h): jnp.zeros((PAIR, 2 * BLK), F32) for s in range(SUB) for h in range(N_KV)}
        dkn_t = dict(dv_t)
        for s, hq in units:
            dsk = dsk - jnp.where(lane8 == hq, jnp.sum(probs[s, hq][1] * delta[hq][_block(s)]), 0.0)
            dv_t[s, hq // GRP] = dv_t[s, hq // GRP] + _dot_tn(do_one[hq][_block(s)], probs[s, hq][0].astype(BF16))
            dkn_t[s, hq // GRP] = dkn_t[s, hq // GRP] + _dot_tn(q_one[hq][_block(s)], ds[s, hq])
        dqn_of = {u: _dot(ds[u], k_of[u[1] // GRP][_band(u[0])]) for u in units}
        for pair in range(N_Q // 2):
            dqn = jnp.concatenate([jnp.where(low, dqn_of[s, 2 * pair], dqn_of[s, 2 * pair + 1]) for s in range(SUB)],
                                  axis=0)
            dqg = dqg + jnp.sum(dqn * qh[pair], axis=0, keepdims=True)
            dqh = dqn * q_gain
            dp_ref[:, O_Q + PAIR * pair:O_Q + PAIR * (pair + 1)] = (
                rq[pair] * (dqh - qh[pair] * _pair_mean(dqh * qh[pair], low))).astype(BF16)

        def over_key_rows(parts):
            bands = [jnp.concatenate([parts[s, h][:HEAD] + parts[s, h][HEAD:] for h in range(N_KV)], axis=0).T
                     for s in range(SUB)]
            pieces = [bands[0][:BLK]]
            pieces += [bands[s - 1][BLK:] + bands[s][:BLK] for s in range(1, SUB)]
            pieces.append(bands[SUB - 1][BLK:])
            return jnp.concatenate(pieces, axis=0)

        dv = over_key_rows(dv_t)
        dkn = over_key_rows(dkn_t)
        dkg_ref[...] += jnp.sum(dkn * kh, axis=0, keepdims=True)
        dkh = dkn * kg_ref[...]
        dk = rk * (dkh - kh * _pair_mean(dkh * kh, low))
        last = slice(ROWS, ROWS + BLK)
        dp_ref[:, O_K:O_V] = jnp.concatenate([dk[BLK:ROWS], dk[last] + kcar[...]], axis=0).astype(BF16)
        dp_ref[:, O_V:] = jnp.concatenate([dv[BLK:ROWS], dv[last] + vcar[...]], axis=0).astype(BF16)
        kcar[...] = dk[:BLK, :]
        vcar[...] = dv[:BLK, :]
        dsk_ref[...] += dsk
        dqg_ref[...] += dqg * SCALE

        bg = p_ref[:, 0:CONV_CH].astype(F32)
        cg = p_ref[:, CONV_CH:2 * CONV_CH].astype(F32)
        hc = p_ref[:, 2 * CONV_CH:3 * CONV_CH].astype(F32)
        yv = y_ref[...].astype(F32)
        dmc = dmix_scr[:, 0:CONV_CH]
        co = bg * yv
        rc = _rms(co)
        ch = co * rc
        dcog_ref[...] += jnp.sum(dmc * ch, axis=0, keepdims=True)
        dco = _rms_bwd(dmc, ch, rc, cog_ref[...])
        dp_ref[:, 0:CONV_CH] = (dco * yv).astype(BF16)
        dy = dco * bg
        row = lax.broadcasted_iota(jnp.int32, (ROWS, CONV_CH), 0)
        nxt = dycar[...]
        dy1 = jnp.where(row == ROWS - 1, nxt[0:1, :], pltpu.roll(dy, ROWS - 1, 0))
        dy2 = jnp.where(row == ROWS - 2, nxt[0:1, :],
                        jnp.where(row == ROWS - 1, nxt[1:2, :], pltpu.roll(dy, ROWS - 2, 0)))
        dycar[...] = dy[0:8, :]
        du = cw_ref[2:3, :] * dy + cw_ref[1:2, :] * dy1 + cw_ref[0:1, :] * dy2
        dp_ref[:, CONV_CH:2 * CONV_CH] = (du * hc).astype(BF16)
        dp_ref[:, 2 * CONV_CH:3 * CONV_CH] = (du * cg).astype(BF16)
        u = cg * hc
        dcw_ref[0:1, :] += jnp.sum(dy2 * u, axis=0, keepdims=True)
        dcw_ref[1:2, :] += jnp.sum(dy1 * u, axis=0, keepdims=True)
        dcw_ref[2:3, :] += jnp.sum(dy * u, axis=0, keepdims=True)

        dmix_scr[...] = _dot_nt(dx_next_ref[...].astype(BF16), wo_ref[...])

    small = lambda shape: pl.BlockSpec(shape, lambda s: (0, 0))
    blk = lambda w: pl.BlockSpec((ROWS, w), lambda s: (steps - 1 - s, 0))
    return _pallas_after(
        dep, body, sinks, proj, proj, bias, y, ao, dx1, dx1, w_o, conv_w, qg, kg, cog, aog, name=name, grid=(steps,),
        in_specs=[pl.BlockSpec(memory_space=pltpu.SMEM),
                  blk(IN_COLS),
                  pl.BlockSpec((BLK, KV_W), lambda s: (jnp.maximum(SUB * (steps - 1 - s) - 1, 0), O_K // KV_W)),
                  pl.BlockSpec((2, BLK, 2 * BLK), lambda s: (0, 0, 0)),
                  blk(CONV_CH), blk(ATTN_W),
                  pl.BlockSpec((ROWS, D_MODEL), lambda s: (steps - 1, 0)),
                  pl.BlockSpec((ROWS, D_MODEL), lambda s: (jnp.maximum(steps - 2 - s, 0), 0)),
                  small((D_MODEL, D_MODEL)),
                  small((3, CONV_CH)), small((1, PAIR)), small((1, PAIR)), small((1, CONV_CH)), small((1, ATTN_W))],
        out_specs=[blk(IN_COLS), small((3, CONV_CH)), small((1, PAIR)), small((1, PAIR)), small((1, N_Q)),
                   small((1, CONV_CH)), small((1, ATTN_W))],
        out_shape=[jax.ShapeDtypeStruct((S, IN_COLS), BF16), jax.ShapeDtypeStruct((3, CONV_CH), F32),
                   jax.ShapeDtypeStruct((1, PAIR), F32), jax.ShapeDtypeStruct((1, PAIR), F32),
                   jax.ShapeDtypeStruct((1, N_Q), F32), jax.ShapeDtypeStruct((1, CONV_CH), F32),
                   jax.ShapeDtypeStruct((1, ATTN_W), F32)],
        scratch_shapes=[pltpu.VMEM((8, CONV_CH), F32), pltpu.VMEM((BLK, PAIR), F32), pltpu.VMEM((BLK, PAIR), F32),
                        pltpu.VMEM((ROWS, D_MODEL), F32)],
        compiler_params=_cparams(("arbitrary",)),
    )


OTHER_CHIPS = ((1, 0), (0, 1), (1, 1))


def _position():
    return lax.axis_index("x"), lax.axis_index("y"), lax.axis_index("c")


def all_gather(shards, name):
    n = len(shards)

    def body(*refs):
        x_refs, out_refs = refs[:n], refs[n:2 * n]
        send_sems, recv_sems, local_sems = refs[2 * n:]
        x, y, c = _position()
        me, sibling = (x, y, c), (x, y, 1 - c)
        chips = [(x ^ mx, y ^ my) for mx, my in OTHER_CHIPS]

        def slab(a, px, py, pc):
            return out_refs[a].at[4 * px + 2 * py + pc]

        def copy(a, k, block, to, src=None):
            return pltpu.make_async_remote_copy(
                src_ref=slab(a, *block) if src is None else src, dst_ref=slab(a, *block),
                send_sem=send_sems.at[7 * a + k], recv_sem=recv_sems.at[7 * a + k], device_id=to, device_id_type=MESH_T)

        mine = [pltpu.make_async_copy(x_refs[a], slab(a, *me), local_sems.at[a]) for a in range(n)]
        for cp in mine:
            cp.start()
        first = []
        for a in range(n):
            first.append(copy(a, 0, me, sibling, src=x_refs[a]))
            first += [copy(a, 1 + j, me, (*chip, c), src=x_refs[a]) for j, chip in enumerate(chips)]
        for cp in first:
            cp.start()
        passed = []
        for a in range(n):
            for j, chip in enumerate(chips):
                copy(a, 1 + j, (*chip, c), me).wait_recv()
                passed.append(copy(a, 4 + j, (*chip, c), sibling))
                passed[-1].start()
        for a in range(n):
            copy(a, 0, sibling, me).wait_recv()
            for j, chip in enumerate(chips):
                copy(a, 4 + j, (*chip, 1 - c), me).wait_recv()
        for cp in first + passed:
            cp.wait_send()
        for cp in mine:
            cp.wait()

    return pl.pallas_call(
        body, name=name, in_specs=[ANY] * n, out_specs=[ANY] * n,
        out_shape=[jax.ShapeDtypeStruct((N_DEV, *s.shape), s.dtype) for s in shards],
        scratch_shapes=[pltpu.SemaphoreType.DMA((7 * n,)), pltpu.SemaphoreType.DMA((7 * n,)),
                        pltpu.SemaphoreType.DMA((n,))],
    )(*shards)


class SplitCopy:
    def __init__(self, name, arrays, n_copies, plan, after=None):
        n = len(arrays)
        self.name, self.n, self.n_copies, self.plan = name, n, n_copies, plan
        extra = [] if after is None else [after]

        def body(*refs):
            in_refs = refs[:n]
            send_sems, recv_sems = refs[n + len(extra)], refs[n + len(extra) + 1]
            token = refs[2 * n + len(extra) + 2]
            for k, (src, dst, to) in enumerate(plan(_position(), in_refs)):
                pltpu.make_async_remote_copy(src_ref=src, dst_ref=dst, send_sem=send_sems.at[k],
                                             recv_sem=recv_sems.at[k], device_id=to, device_id_type=MESH_T).start()
            token[...] = jnp.zeros_like(token)

        outs = pl.pallas_call(
            body, name=name + "_start",
            out_shape=(pltpu.SemaphoreType.DMA((n_copies,)), pltpu.SemaphoreType.DMA((n_copies,)),
                       *[pltpu.HBM(a.shape, a.dtype) for a in arrays], jax.ShapeDtypeStruct((8, 128), F32)),
            in_specs=[HBM] * n + [ANY] * len(extra),
            out_specs=(SEM, SEM, *[HBM] * n, pl.BlockSpec(memory_space=pltpu.VMEM)),
            input_output_aliases={i: 2 + i for i in range(n)},
            compiler_params=pltpu.CompilerParams(has_side_effects=pltpu.SideEffectType.DATAFLOW_SIDE_EFFECTING),
        )(*[pltpu.with_memory_space_constraint(a, pltpu.HBM) for a in arrays], *extra)
        self.send_sems, self.recv_sems = outs[0], outs[1]
        self.arrays, self.token = list(outs[2:2 + n]), outs[2 + n]

    def wait(self, after):
        n, plan = self.n, self.plan

        def body(*refs):
            in_refs, send_sems, recv_sems = refs[:n], refs[n], refs[n + 1]
            for k, (src, dst, to) in enumerate(plan(_position(), in_refs)):
                cp = pltpu.make_async_remote_copy(src_ref=src, dst_ref=dst, send_sem=send_sems.at[k],
                                                  recv_sem=recv_sems.at[k], device_id=to, device_id_type=MESH_T)
                cp.wait_send()
                cp.wait_recv()

        outs = pl.pallas_call(
            body, name=self.name + "_wait",
            out_shape=tuple(pltpu.HBM(a.shape, a.dtype) for a in self.arrays),
            in_specs=[HBM] * n + [SEM, SEM, ANY], out_specs=tuple([HBM] * n),
            input_output_aliases={i: i for i in range(n)},
            compiler_params=pltpu.CompilerParams(has_side_effects=pltpu.SideEffectType.DATAFLOW_SIDE_EFFECTING),
        )(*self.arrays, self.send_sems, self.recv_sems, after)
        return list(outs)


def gather_start(shards, me, name, after=None):
    n = len(shards)
    lands = [lax.dynamic_update_slice(lax.empty((N_DEV, *s.shape), s.dtype), s[None], (me, 0, 0)) for s in shards]

    def plan(pos, refs):
        x, y, c = pos
        return [(refs[a], refs[n + a].at[4 * x + 2 * y + c], (x ^ mx, y ^ my, c))
                for a in range(n) for mx, my in OTHER_CHIPS]

    return SplitCopy(name, list(shards) + lands, 3 * n, plan, after=after)


def sibling_start(lands, name):
    n = len(lands)

    def plan(pos, refs):
        x, y, c = pos
        return [(refs[a].at[2 * q + c], refs[a].at[2 * q + c], (x, y, 1 - c)) for a in range(n) for q in range(4)]

    return SplitCopy(name, list(lands), 4 * n, plan)


def scatter_start(slabs, name):
    n = len(slabs)
    lands = [lax.empty((N_DEV - 1, *g.shape[1:]), g.dtype) for g in slabs]

    def plan(pos, refs):
        x, y, c = pos
        copies = []
        for a in range(n):
            for r in range(1, N_DEV):
                px, py, pc = x ^ ((r >> 2) & 1), y ^ ((r >> 1) & 1), c ^ (r & 1)
                copies.append((refs[a].at[4 * px + 2 * py + pc], refs[n + a].at[r - 1], (px, py, pc)))
        return copies

    return SplitCopy(name, list(slabs) + lands, (N_DEV - 1) * n, plan)


def all_reduce_small(v, name):
    R, W = v.shape

    def body(v_ref, o_ref, recv, send_sems, recv_sems):
        x, y, c = _position()
        me = 4 * x + 2 * y + c
        copies = []
        for r in range(1, N_DEV):
            to = (x ^ ((r >> 2) & 1), y ^ ((r >> 1) & 1), c ^ (r & 1))
            copies.append(pltpu.make_async_remote_copy(
                src_ref=v_ref, dst_ref=recv.at[me], send_sem=send_sems.at[r - 1], recv_sem=recv_sems.at[r - 1],
                device_id=to, device_id_type=MESH_T))
        for cp in copies:
            cp.start()
        recv[pl.ds(me, 1)] = v_ref[...][None]
        for cp in copies:
            cp.wait()
        acc = recv[0]
        for s in range(1, N_DEV):
            acc = acc + recv[s]
        o_ref[...] = acc

    return pl.pallas_call(
        body, name=name,
        in_specs=[pl.BlockSpec(memory_space=pltpu.VMEM)], out_specs=pl.BlockSpec(memory_space=pltpu.VMEM),
        out_shape=jax.ShapeDtypeStruct((R, W), F32),
        scratch_shapes=[pltpu.VMEM((N_DEV, R, W), F32), pltpu.SemaphoreType.DMA((N_DEV - 1,)),
                        pltpu.SemaphoreType.DMA((N_DEV - 1,))],
    )(v)


def _row_tile(rows):
    if rows <= 512:
        return rows
    return max(t for t in range(8, 513, 8) if rows % t == 0)


def _adamw_update(w, g, m, v):
    mn = ADAM_B1 * m + (1.0 - ADAM_B1) * g
    vn = ADAM_B2 * v + (1.0 - ADAM_B2) * (g * g)
    m_hat = mn / (1.0 - ADAM_B1 ** ADAM_STEP)
    v_hat = vn / (1.0 - ADAM_B2 ** ADAM_STEP)
    return -ADAM_LR * (m_hat / (jnp.sqrt(v_hat) + ADAM_EPS) + ADAM_WD * w), mn, vn


def adamw(w, g, m, v, name):
    R, W = w.shape
    tr = _row_tile(R)

    def body(w_ref, g_ref, m_ref, v_ref, d_ref, mo_ref, vo_ref):
        d_ref[...], mo_ref[...], vo_ref[...] = _adamw_update(w_ref[...], g_ref[...], m_ref[...], v_ref[...])

    spec = pl.BlockSpec((tr, W), lambda i: (i, 0))
    return pl.pallas_call(
        body, name=name, grid=(R // tr,), in_specs=[spec] * 4, out_specs=[spec] * 3,
        out_shape=[jax.ShapeDtypeStruct((R, W), F32)] * 3,
        compiler_params=_cparams(("parallel",)),
    )(w, g, m, v)


def reduce_adamw(slabs, land, w, m, v, layer, me_arr, name, others=None):
    _, R, W = slabs.shape
    tr = max(t for t in range(16, R // 2 + 1, 16) if R % t == 0)
    n_other = 0 if others is None else 4

    def body(me_ref, s_ref, l_ref, w_ref, m_ref, v_ref, *rest):
        g_ref, d_ref, mo_ref, vo_ref = rest[n_other:]
        g = s_ref[0].astype(F32)
        for r in range(N_DEV - 1):
            g = g + l_ref[r].astype(F32)
        g_ref[0] = g
        d_ref[0], mo_ref[0], vo_ref[0] = _adamw_update(w_ref[0], g, m_ref[0], v_ref[0])

    spec = pl.BlockSpec((1, tr, W), lambda i, me: (layer, i, 0))
    return pl.pallas_call(
        body, name=name,
        grid_spec=pltpu.PrefetchScalarGridSpec(
            num_scalar_prefetch=1, grid=(R // tr,),
            in_specs=[pl.BlockSpec((1, tr, W), lambda i, me: (me[0], i, 0)),
                      pl.BlockSpec((N_DEV - 1, tr, W), lambda i, me: (0, i, 0)), spec, spec, spec] + [ANY] * n_other,
            out_specs=[spec] * 4),
        out_shape=[jax.ShapeDtypeStruct((DEPTH, R, W), F32)] * 4,
        input_output_aliases={6 + i: i for i in range(n_other)},
        compiler_params=_cparams(("parallel",)),
    )(me_arr, slabs, land, w, m, v, *([] if others is None else others))


SMALL_NAMES = ("norm1_g", "q_norm_g", "k_norm_g", "sinks", "conv_out_g", "attn_out_g", "norm2_g", "conv_w")
SMALL_SIZES = (D_MODEL, HEAD, HEAD, N_Q, CONV_CH, ATTN_W, D_MODEL, 3 * CONV_CH)
SMALL_ROWS = 80


def kernel(x, norm1_g, w_in, conv_w, q_norm_g, k_norm_g, sinks, conv_out_g, attn_out_g, w_o, norm2_g, w_gate, w_up, w_down, loss_target, m_norm1_g, m_w_in, m_conv_w, m_q_norm_g, m_k_norm_g, m_sinks, m_conv_out_g, m_attn_out_g, m_w_o, m_norm2_g, m_w_gate, m_w_up, m_w_down, v_norm1_g, v_w_in, v_conv_w, v_q_norm_g, v_k_norm_g, v_sinks, v_conv_out_g, v_attn_out_g, v_w_o, v_norm2_g, v_w_gate, v_w_up, v_w_down):
    xi, yi, ci = _position()
    me = 4 * xi + 2 * yi + ci
    me_arr = jnp.reshape(me, (1,)).astype(jnp.int32)
    xs, tgt = x[0], loss_target[0]
    bf = lambda a: a.astype(BF16)
    bias = band_bias()

    t = lambda a: jnp.swapaxes(a, 1, 2)
    shard = dict(w_in=(t(w_in), t(m_w_in), t(v_w_in)), w_o=(w_o, m_w_o, v_w_o),
                 w_gate=(t(w_gate), t(m_w_gate), t(v_w_gate)), w_up=(t(w_up), t(m_w_up), t(v_w_up)),
                 w_down=(w_down, m_w_down, v_w_down))
    wb = {n: bf(shard[n][0]) for n in shard}

    g_in0, g_o0, g_conv = all_gather([wb["w_in"][0], wb["w_o"][0], conv_w.reshape(DEPTH * 3, HEAD)], "gather_first")
    ag_ffn0 = gather_start([wb["w_gate"][0], wb["w_up"][0], wb["w_down"][0]], me, "gather_ffn0", after=g_in0)
    ag_mix1 = gather_start([wb["w_in"][1], wb["w_o"][1]], me, "gather_mix1", after=ag_ffn0.token)
    ag_ffn1 = gather_start([wb["w_gate"][1], wb["w_up"][1], wb["w_down"][1]], me, "gather_ffn1", after=ag_mix1.token)
    conv_full = g_conv.reshape(N_DEV, DEPTH, 3, HEAD).transpose(1, 2, 0, 3).reshape(DEPTH, 3, CONV_CH)
    pair_gain = lambda g: jnp.tile(g[None], (1, 2))
    small = [dict(norm1_g=norm1_g[l][None], conv_w=conv_full[l], q_norm_g=pair_gain(q_norm_g[l]),
                  k_norm_g=pair_gain(k_norm_g[l]), sinks=sinks[l], conv_out_g=conv_out_g[l][None],
                  attn_out_g=attn_out_g[l][None], norm2_g=norm2_g[l][None]) for l in range(DEPTH)]
    whole = lambda g: g.reshape(-1, D_MODEL)
    weights = [dict(w_in=whole(g_in0), w_o=whole(g_o0)), {}]

    def ffn_weights(g_gate, g_up, g_down):
        return dict(w_gate=whole(g_gate), w_up=whole(g_up), w_down=whole(g_down))

    saved = []
    xl = xs
    for l in range(DEPTH):
        sp, wl = small[l], weights[l]
        h, proj = norm_proj(xl, sp["norm1_g"], wl["w_in"], f"norm_proj{l}",
                            dep=ag_ffn1.token if l == 0 else pass_ffn1.token)
        mixer_args = (proj, bias, sp["sinks"], sp["conv_w"], sp["q_norm_g"], sp["k_norm_g"], sp["conv_out_g"],
                      sp["attn_out_g"], f"mixer_fwd{l}")
        if l == 0:
            mix, y, ao = mixer_fwd(*mixer_args)
            pass_ffn0 = sibling_start(ag_ffn0.wait(mix)[3:], "pass_ffn0")
            x1 = matmul_residual(mix, wl["w_o"], xl, "out_proj0", dep=pass_ffn0.token)
            wl.update(ffn_weights(*pass_ffn0.wait(x1)))
            pass_mix1 = sibling_start(ag_mix1.wait(x1)[2:], "pass_mix1")
            h2, a, b, f, x2 = ffn_fwd(x1, sp["norm2_g"], wl["w_gate"], wl["w_up"], wl["w_down"], "ffn_fwd0",
                                      dep=pass_mix1.token)
            g_in, g_o = pass_mix1.wait(x2)
            weights[1] = dict(w_in=whole(g_in), w_o=whole(g_o))
            pass_ffn1 = sibling_start(ag_ffn1.wait(x2)[3:], "pass_ffn1")
        else:
            mix, y, ao, x1 = mixer_fwd(*mixer_args, residual=xl, w_o=wl["w_o"])
            wl.update(ffn_weights(*pass_ffn1.wait(x1)))
            h2, a, b, f, dx, loss_row = ffn_fwd(x1, sp["norm2_g"], wl["w_gate"], wl["w_up"], wl["w_down"], "ffn_fwd1",
                                                tgt=tgt)
            x2 = None
        saved.append((xl, h, proj, mix, y, ao, x1, h2, a, b, f))
        xl = x2

    stepped = {n: None for n in shard}
    slabs = lambda d: d.reshape(N_DEV, -1, D_MODEL)
    gsmall = [None] * DEPTH

    def finish(sc, names, after, l):
        arrays = sc.wait(after)
        k = len(names)
        for i, n in enumerate(names):
            w, m, v = shard[n]
            stepped[n] = reduce_adamw(arrays[i], arrays[k + i], w, m, v, l, me_arr, f"reduce_adamw_{n}{l}",
                                      others=stepped[n])

    for l in reversed(range(DEPTH)):
        sp, wl = small[l], weights[l]
        x0, h, proj, mix, y, ao, x1, h2, a, b, f = saved[l]
        (d_wd,) = grad_weight((f,), dx, TF, f"grad_w_down{l}", tk=2048)
        sc_down = scatter_start([slabs(d_wd)], f"scatter_w_down{l}")
        da, db, dx1, d_g2 = ffn_bwd(dx, a, b, x1, sp["norm2_g"], wl["w_gate"], wl["w_up"], wl["w_down"],
                                    f"ffn_bwd{l}", dep=sc_down.token)
        d_wg, d_wu = grad_weight((da, db), h2, TF, f"grad_w_gate_up{l}")
        (d_wo,) = grad_weight((mix,), dx1, D_MODEL, f"grad_w_o{l}", tk=2048)
        sc_rest = scatter_start([slabs(d_wg), slabs(d_wu), slabs(d_wo)], f"scatter_w_gate_up_o{l}")
        dproj, d_cw, d_qg, d_kg, d_sk, d_cog, d_aog = mixer_bwd(
            proj, bias, y, ao, dx1, wl["w_o"], sp["sinks"], sp["conv_w"], sp["q_norm_g"], sp["k_norm_g"],
            sp["conv_out_g"], sp["attn_out_g"], f"mixer_bwd{l}", dep=sc_rest.token)
        finish(sc_down, ["w_down"], dproj, l)
        finish(sc_rest, ["w_gate", "w_up", "w_o"], dproj, l)
        (d_win,) = grad_weight((dproj,), h, IN_COLS // 2, f"grad_w_in{l}", tk=2048)
        sc_in = scatter_start([slabs(d_win)], f"scatter_w_in{l}")
        dx, d_g1 = proj_bwd(dproj, wl["w_in"], x0, sp["norm1_g"], dx1, f"proj_bwd{l}", dep=sc_in.token)
        finish(sc_in, ["w_in"], dx, l)
        both_heads = lambda d: d[:, :HEAD] + d[:, HEAD:]
        gsmall[l] = dict(norm1_g=d_g1, conv_w=d_cw, q_norm_g=both_heads(d_qg), k_norm_g=both_heads(d_kg), sinks=d_sk,
                         conv_out_g=d_cog, attn_out_g=d_aog, norm2_g=d_g2)
    grad_x = dx[None]

    flat = jnp.concatenate([loss_row[0, 0:1]] + [gsmall[l][n].reshape(-1) for l in range(DEPTH) for n in SMALL_NAMES])
    flat = jnp.pad(flat, (0, SMALL_ROWS * 128 - flat.shape[0])).reshape(SMALL_ROWS, 128)
    flat = all_reduce_small(flat, "all_reduce_small_grads").reshape(-1)
    loss = flat[0]
    gs = {n: [] for n in SMALL_NAMES}
    off = 1
    for l in range(DEPTH):
        for n, size in zip(SMALL_NAMES, SMALL_SIZES):
            gs[n].append(flat[off:off + size])
            off += size
    gs = {n: jnp.stack(v) for n, v in gs.items()}
    g_conv = lax.dynamic_slice(gs["conv_w"].reshape(DEPTH, 3, CONV_CH), (0, 0, me * HEAD), (DEPTH, 3, HEAD))

    gs["conv_w"] = g_conv
    params = dict(norm1_g=(norm1_g, m_norm1_g, v_norm1_g), conv_w=(conv_w, m_conv_w, v_conv_w),
                  q_norm_g=(q_norm_g, m_q_norm_g, v_q_norm_g), k_norm_g=(k_norm_g, m_k_norm_g, v_k_norm_g),
                  sinks=(sinks, m_sinks, v_sinks), conv_out_g=(conv_out_g, m_conv_out_g, v_conv_out_g),
                  attn_out_g=(attn_out_g, m_attn_out_g, v_attn_out_g), norm2_g=(norm2_g, m_norm2_g, v_norm2_g))
    names = ("norm1_g", "w_in", "conv_w", "q_norm_g", "k_norm_g", "sinks", "conv_out_g", "attn_out_g", "w_o",
             "norm2_g", "w_gate", "w_up", "w_down")

    out = {}
    for n in names:
        if n in shard:
            out[n] = tuple(t(r) for r in stepped[n]) if n in ("w_in", "w_gate", "w_up") else stepped[n]
        else:
            w, m, v = params[n]
            two_d = (-1, w.shape[-1])
            d, mn, vn = adamw(w.reshape(two_d), gs[n].reshape(two_d), m.reshape(two_d), v.reshape(two_d), f"adamw_{n}")
            out[n] = (gs[n].reshape(w.shape), d.reshape(w.shape), mn.reshape(w.shape), vn.reshape(w.shape))
    return (loss, grad_x, *[out[n][i] for i in range(4) for n in names])
```

```python
import jax
import jax.numpy as jnp
from jax import lax
from jax.experimental import pallas as pl
from jax.experimental.pallas import tpu as pltpu

F32 = jnp.float32
BF16 = jnp.bfloat16

D_MODEL = 1024
CONV_CH = 512
ATTN_W = 512
N_Q = 8
N_KV = 2
GRP = N_Q // N_KV
HEAD = 64
IN_COLS = 2304
D_FF = 2816
BLK = 128
O_Q = 3 * CONV_CH
O_K = O_Q + ATTN_W
O_V = O_K + N_KV * HEAD
EPS = 1e-6
NEG_INF = -1e30
SCALE = HEAD ** -0.5
N_DEV = 8
DEPTH = 2

ADAM_LR = 0.001
ADAM_B1 = 0.9
ADAM_B2 = 0.999
ADAM_EPS = 1e-08
ADAM_WD = 0.01
ADAM_STEP = 10

VMEM_LIMIT = 56 * 1024 * 1024
TM = 512
MESH_T = pl.DeviceIdType.MESH

ANY = pl.BlockSpec(memory_space=pl.ANY)
HBM = pl.BlockSpec(memory_space=pltpu.HBM)
SEM = pl.BlockSpec(memory_space=pltpu.SEMAPHORE)


def _cparams(sem):
    return pltpu.CompilerParams(dimension_semantics=sem, vmem_limit_bytes=VMEM_LIMIT)


def _pallas_after(dep, body, *args, in_specs, **kw):
    if dep is None:
        return pl.pallas_call(body, in_specs=in_specs, **kw)(*args)

    def after_dep(dep_ref, *refs):
        body(*refs)

    return pl.pallas_call(after_dep, in_specs=[ANY, *in_specs], **kw)(dep, *args)


def _dot(a, b):
    return jnp.dot(a, b, preferred_element_type=F32)


def _dot_nt(a, b):
    return lax.dot_general(a, b, (((1,), (1,)), ((), ())), preferred_element_type=F32)


def _dot_tn(a, b):
    return lax.dot_general(a, b, (((0,), (0,)), ((), ())), preferred_element_type=F32)


LANES = 128


def _row_reduce(v, op, reduce):
    w = v.shape[-1]
    if w > LANES and w % LANES == 0:
        acc = v[:, 0:LANES]
        for c in range(1, w // LANES):
            acc = op(acc, v[:, LANES * c:LANES * (c + 1)])
        v = acc
    return reduce(v, axis=-1, keepdims=True)


def _row_sum(v):
    return _row_reduce(v, jnp.add, jnp.sum)


def _row_mean(v):
    return _row_sum(v) * (1.0 / v.shape[-1])


def _rms(v):
    return lax.rsqrt(_row_mean(v * v) + EPS)


def _sigmoid(v):
    return 1.0 / (1.0 + jnp.exp(-v))


def _rms_bwd(dyv, xh, r, g):
    dxh = dyv * g
    return r * (dxh - xh * _row_mean(dxh * xh))


def norm_proj(x, g, w_t, name, dep=None):
    S, N = x.shape[0], w_t.shape[0]
    tm = 2 * TM

    def body(x_ref, g_ref, w_ref, h_ref, p_ref):
        xf = x_ref[...]
        h = ((xf * _rms(xf)) * g_ref[...]).astype(BF16)
        h_ref[...] = h
        p_ref[...] = _dot_nt(h, w_ref[...]).astype(BF16)

    return _pallas_after(
        dep, body, x, g, w_t, name=name, grid=(S // tm,),
        in_specs=[pl.BlockSpec((tm, D_MODEL), lambda i: (i, 0)),
                  pl.BlockSpec((1, D_MODEL), lambda i: (0, 0)),
                  pl.BlockSpec((N, D_MODEL), lambda i: (0, 0))],
        out_specs=[pl.BlockSpec((tm, D_MODEL), lambda i: (i, 0)),
                   pl.BlockSpec((tm, N), lambda i: (i, 0))],
        out_shape=[jax.ShapeDtypeStruct((S, D_MODEL), BF16), jax.ShapeDtypeStruct((S, N), BF16)],
        compiler_params=_cparams(("parallel",)),
    )


def band_bias():
    qi = lax.broadcasted_iota(jnp.int32, (BLK, 2 * BLK), 0)
    kj = lax.broadcasted_iota(jnp.int32, (BLK, 2 * BLK), 1)
    diff = qi + BLK - kj
    valid = (diff >= 0) & (diff < BLK)
    return jnp.stack([jnp.where(valid & (kj >= BLK), 0.0, NEG_INF), jnp.where(valid, 0.0, NEG_INF)]).astype(F32)


def _softmax_with_sink(s, sink):
    m = jnp.maximum(_row_reduce(s, jnp.maximum, jnp.max), sink)
    p = jnp.exp(s - m)
    es = jnp.exp(sink - m)
    inv = 1.0 / (_row_sum(p) + es)
    return p * inv, es * inv


PAIR = 2 * HEAD


def _low_half():
    return lax.broadcasted_iota(jnp.int32, (1, PAIR), 1) < HEAD


def _half_sums(v, low):
    return (jnp.sum(jnp.where(low, v, 0.0), axis=-1, keepdims=True),
            jnp.sum(jnp.where(low, 0.0, v), axis=-1, keepdims=True))


def _pair_mean(v, low):
    e, o = _half_sums(v, low)
    return jnp.where(low, e, o) * (1.0 / HEAD)


def _pair_rms(v, low):
    return lax.rsqrt(_pair_mean(v * v, low) + EPS)


def _one_head_in_both_halves(pair, low):
    swapped = pltpu.roll(pair, HEAD, 1)
    return jnp.where(low, pair, swapped), jnp.where(low, swapped, pair)


SUB_FWD, SUB_FWD_PROJECT, SUB_BWD = 2, 8, 4


def _kv_pairs(p_ref, kvp_ref):
    k = jnp.concatenate([kvp_ref[:, 0:PAIR], p_ref[:, O_K:O_K + PAIR]], axis=0).astype(F32)
    v = jnp.concatenate([kvp_ref[:, PAIR:2 * PAIR], p_ref[:, O_V:O_V + PAIR]], axis=0)
    return k, v


def _band(s):
    return slice(BLK * s, BLK * s + 2 * BLK)


def _block(s):
    return slice(BLK * s, BLK * (s + 1))


def mixer_fwd(proj, bias, sinks, conv_w, qg, kg, cog, aog, name, residual=None, w_o=None):
    S = proj.shape[0]
    project = w_o is not None
    SUB = SUB_FWD_PROJECT if project else SUB_FWD
    ROWS = SUB * BLK
    steps = S // ROWS

    def body(sinks_ref, p_ref, kvp_ref, bias_ref, cw_ref, qg_ref, kg_ref, cog_ref, aog_ref, *rest):
        if project:
            x_ref, wo_ref, mix_ref, y_ref, ao_ref, x1_ref, ucar = rest
        else:
            mix_ref, y_ref, ao_ref, ucar = rest
        n = pl.program_id(0)

        @pl.when(n == 0)
        def _():
            ucar[...] = jnp.zeros_like(ucar)

        bg = p_ref[:, 0:CONV_CH].astype(F32)
        u = p_ref[:, CONV_CH:2 * CONV_CH].astype(F32) * p_ref[:, 2 * CONV_CH:3 * CONV_CH].astype(F32)
        row = lax.broadcasted_iota(jnp.int32, (ROWS, CONV_CH), 0)
        prev = ucar[...]
        u1 = jnp.where(row == 0, prev[7:8, :], pltpu.roll(u, 1, 0))
        u2 = jnp.where(row == 0, prev[6:7, :], jnp.where(row == 1, prev[7:8, :], pltpu.roll(u, 2, 0)))
        ucar[...] = u[ROWS - 8:ROWS, :]
        y = cw_ref[0:1, :] * u2 + cw_ref[1:2, :] * u1 + cw_ref[2:3, :] * u
        y_ref[...] = y.astype(BF16)
        co = bg * y
        mix_conv = ((co * _rms(co)) * cog_ref[...]).astype(BF16)
        mix_ref[:, 0:CONV_CH] = mix_conv
        if project:
            x1 = x_ref[...] + _dot(mix_conv, wo_ref[0:CONV_CH, :])

        low = _low_half()
        q_gain = qg_ref[...] * SCALE
        k, v = _kv_pairs(p_ref, kvp_ref)
        kn = ((k * _pair_rms(k, low)) * kg_ref[...]).astype(BF16)
        k_of = _one_head_in_both_halves(kn, low)
        v_of = _one_head_in_both_halves(v, low)
        biases = [bias_ref[jnp.minimum(n, 1)]] + [bias_ref[1]] * (SUB - 1)
        units = [(s, hq) for s in range(SUB) for hq in range(N_Q)]
        q_one = []
        for pair in range(N_Q // 2):
            q = p_ref[:, O_Q + PAIR * pair:O_Q + PAIR * (pair + 1)].astype(F32)
            qn = ((q * _pair_rms(q, low)) * q_gain).astype(BF16)
            q_one += [jnp.where(mine, qn, jnp.zeros_like(qn)) for mine in (low, jnp.logical_not(low))]
        scores = [_dot_nt(q_one[hq][_block(s)], k_of[hq // GRP][_band(s)]) + biases[s] for s, hq in units]
        probs = [_softmax_with_sink(sc, sinks_ref[hq])[0].astype(BF16) for (s, hq), sc in zip(units, scores)]
        o = {u_: _dot(pn, v_of[u_[1] // GRP][_band(u_[0])]) for u_, pn in zip(units, probs)}
        ao = jnp.concatenate(
            [jnp.concatenate([jnp.where(low, o[s, 2 * pair], o[s, 2 * pair + 1]) for pair in range(N_Q // 2)], axis=1)
             for s in range(SUB)], axis=0)
        ao_ref[...] = ao.astype(BF16)
        mix_attn = ((ao * _rms(ao)) * aog_ref[...]).astype(BF16)
        mix_ref[:, CONV_CH:] = mix_attn
        if project:
            x1_ref[...] = x1 + _dot(mix_attn, wo_ref[CONV_CH:, :])

    small = lambda shape: pl.BlockSpec(shape, lambda n: (0, 0))
    rows = lambda w: pl.BlockSpec((ROWS, w), lambda n: (n, 0))
    return pl.pallas_call(
        body, name=name, grid=(steps,),
        in_specs=[pl.BlockSpec(memory_space=pltpu.SMEM),
                  rows(IN_COLS),
                  pl.BlockSpec((BLK, 2 * PAIR), lambda n: (jnp.maximum(SUB * n - 1, 0), O_K // (2 * PAIR))),
                  pl.BlockSpec((2, BLK, 2 * BLK), lambda n: (0, 0, 0)),
                  small((3, CONV_CH)), small((1, PAIR)), small((1, PAIR)), small((1, CONV_CH)), small((1, ATTN_W))]
        + ([rows(D_MODEL), small((D_MODEL, D_MODEL))] if project else []),
        out_specs=[rows(D_MODEL), rows(CONV_CH), rows(ATTN_W)] + ([rows(D_MODEL)] if project else []),
        out_shape=[jax.ShapeDtypeStruct((S, D_MODEL), BF16), jax.ShapeDtypeStruct((S, CONV_CH), BF16),
                   jax.ShapeDtypeStruct((S, ATTN_W), BF16)]
        + ([jax.ShapeDtypeStruct((S, D_MODEL), F32)] if project else []),
        scratch_shapes=[pltpu.VMEM((8, CONV_CH), F32)],
        compiler_params=_cparams(("arbitrary",)),
    )(sinks, proj, proj, bias, conv_w, qg, kg, cog, aog, *([residual, w_o] if project else []))


def matmul_residual(a, w, res, name, dep=None):
    S, K = a.shape
    N = w.shape[1]

    def body(a_ref, w_ref, r_ref, o_ref):
        o_ref[...] = r_ref[...] + _dot(a_ref[...], w_ref[...])

    return _pallas_after(
        dep, body, a, w, res, name=name, grid=(S // TM,),
        in_specs=[pl.BlockSpec((TM, K), lambda i: (i, 0)), pl.BlockSpec((K, N), lambda i: (0, 0)),
                  pl.BlockSpec((TM, N), lambda i: (i, 0))],
        out_specs=pl.BlockSpec((TM, N), lambda i: (i, 0)),
        out_shape=jax.ShapeDtypeStruct((S, N), F32),
        compiler_params=_cparams(("parallel",)),
    )


TF = 1408


def ffn_fwd(x1, g, wg, wu, wd, name, dep=None, tgt=None):
    S = x1.shape[0]
    ni = S // TM
    with_loss = tgt is not None
    row = pl.BlockSpec((TM, D_MODEL), lambda i: (i, 0))
    vec = pl.BlockSpec((1, D_MODEL), lambda i: (0, 0))
    acts = [jax.ShapeDtypeStruct((S, D_FF), BF16)] * 3

    def half_weights(j):
        return [pl.BlockSpec((TF, D_MODEL), lambda i: (j, 0))] * 3

    def swiglu(h2, wg_ref, wu_ref, a_ref, b_ref, f_ref):
        a = _dot_nt(h2, wg_ref[...])
        b = _dot_nt(h2, wu_ref[...])
        a_ref[...] = a.astype(BF16)
        b_ref[...] = b.astype(BF16)
        f = ((a * _sigmoid(a)) * b).astype(BF16)
        f_ref[...] = f
        return f

    def first(x_ref, g_ref, wg_ref, wu_ref, wd_ref, h2_ref, a_ref, b_ref, f_ref, part_ref):
        xf = x_ref[...]
        h2 = ((xf * _rms(xf)) * g_ref[...]).astype(BF16)
        h2_ref[...] = h2
        part_ref[...] = xf + _dot(swiglu(h2, wg_ref, wu_ref, a_ref, b_ref, f_ref), wd_ref[...])

    h2, a, b, f, part = _pallas_after(
        dep, first, x1, g, wg, wu, wd, name=name + "_a", grid=(ni,),
        in_specs=[row, vec] + half_weights(0),
        out_specs=[row] + [pl.BlockSpec((TM, TF), lambda i: (i, 0))] * 3 + [row],
        out_shape=[jax.ShapeDtypeStruct((S, D_MODEL), BF16)] + acts + [jax.ShapeDtypeStruct((S, D_MODEL), F32)],
        compiler_params=_cparams(("parallel",)),
    )

    def second(h2_ref, part_ref, wg_ref, wu_ref, wd_ref, *rest):
        rest = rest[3:]
        if with_loss:
            t_ref, a_ref, b_ref, f_ref, o_ref, l_ref, sq = rest
        else:
            a_ref, b_ref, f_ref, o_ref = rest
        i = pl.program_id(0)
        out = part_ref[...] + _dot(swiglu(h2_ref[...], wg_ref, wu_ref, a_ref, b_ref, f_ref), wd_ref[...])
        if not with_loss:
            o_ref[...] = out
            return
        e = out - t_ref[...]
        o_ref[...] = e * (1.0 / D_MODEL)
        col = jnp.sum(e * e, axis=0, keepdims=True)

        @pl.when(i == 0)
        def _():
            sq[...] = col

        @pl.when(i > 0)
        def _():
            sq[...] += col

        @pl.when(i == ni - 1)
        def _():
            l_ref[...] = jnp.full((1, 128), jnp.sum(sq[...]) * (0.5 / D_MODEL), F32)

    results = pl.pallas_call(
        second, name=name + "_b", grid=(ni,),
        in_specs=[row, row] + half_weights(1) + [ANY] * 3 + ([row] if with_loss else []),
        out_specs=[pl.BlockSpec((TM, TF), lambda i: (i, 1))] * 3 + [row]
        + ([pl.BlockSpec((1, 128), lambda i: (0, 0))] if with_loss else []),
        out_shape=acts + [jax.ShapeDtypeStruct((S, D_MODEL), F32)]
        + ([jax.ShapeDtypeStruct((1, 128), F32)] if with_loss else []),
        input_output_aliases={5: 0, 6: 1, 7: 2},
        scratch_shapes=[pltpu.VMEM((1, D_MODEL), F32)] if with_loss else [],
        compiler_params=_cparams(("arbitrary",) if with_loss else ("parallel",)),
    )(h2, part, wg, wu, wd, a, b, f, *([tgt] if with_loss else []))
    return (h2, *results)


def ffn_bwd(dx2, a, b, x1, g, wg, wu, wd, name, dep=None):
    S = dx2.shape[0]
    ni = S // TM
    row = pl.BlockSpec((TM, D_MODEL), lambda i: (i, 0))
    vec = pl.BlockSpec((1, D_MODEL), lambda i: (0, 0))
    acts = [jax.ShapeDtypeStruct((S, D_FF), BF16)] * 2

    def half(j):
        return [pl.BlockSpec((TM, TF), lambda i: (i, j))] * 2, [pl.BlockSpec((TF, D_MODEL), lambda i: (j, 0))] * 3

    def through_half(dx_ref, a_ref, b_ref, wg_ref, wu_ref, wd_ref, da_ref, db_ref):
        df = _dot_nt(dx_ref[...].astype(BF16), wd_ref[...])
        av = a_ref[...].astype(F32)
        bv = b_ref[...].astype(F32)
        sg = _sigmoid(av)
        da = ((df * bv) * (sg * (1.0 + av * (1.0 - sg)))).astype(BF16)
        db = (df * (av * sg)).astype(BF16)
        da_ref[...] = da
        db_ref[...] = db
        return _dot(da, wg_ref[...]) + _dot(db, wu_ref[...])

    def first(dx_ref, a_ref, b_ref, wg_ref, wu_ref, wd_ref, da_ref, db_ref, part_ref):
        part_ref[...] = through_half(dx_ref, a_ref, b_ref, wg_ref, wu_ref, wd_ref, da_ref, db_ref)

    tiles, weights = half(0)
    da, db, part = _pallas_after(
        dep, first, dx2, a, b, wg, wu, wd, name=name + "_a", grid=(ni,),
        in_specs=[row] + tiles + weights, out_specs=tiles + [row],
        out_shape=acts + [jax.ShapeDtypeStruct((S, D_MODEL), F32)],
        compiler_params=_cparams(("parallel",)),
    )

    def second(dx_ref, a_ref, b_ref, wg_ref, wu_ref, wd_ref, part_ref, x_ref, g_ref, da_prev, db_prev,
               da_ref, db_ref, dx1_ref, dg_ref):
        i = pl.program_id(0)

        @pl.when(i == 0)
        def _():
            dg_ref[...] = jnp.zeros_like(dg_ref)

        dh = part_ref[...] + through_half(dx_ref, a_ref, b_ref, wg_ref, wu_ref, wd_ref, da_ref, db_ref)
        xf = x_ref[...]
        r = _rms(xf)
        xh = xf * r
        dg_ref[...] += jnp.sum(dh * xh, axis=0, keepdims=True)
        dx1_ref[...] = dx_ref[...] + _rms_bwd(dh, xh, r, g_ref[...])

    tiles, weights = half(1)
    return pl.pallas_call(
        second, name=name + "_b", grid=(ni,),
        in_specs=[row] + tiles + weights + [row, row, vec, ANY, ANY],
        out_specs=tiles + [row, vec],
        out_shape=acts + [jax.ShapeDtypeStruct((S, D_MODEL), F32), jax.ShapeDtypeStruct((1, D_MODEL), F32)],
        input_output_aliases={9: 0, 10: 1},
        compiler_params=_cparams(("arbitrary",)),
    )(dx2, a, b, wg, wu, wd, part, x1, g, da, db)


def grad_weight(lhs, rhs, tm, name, tk=1024):
    S, N = rhs.shape
    M = lhs[0].shape[1]
    tk = min(tk, S)
    nk = S // tk
    n_l = len(lhs)

    def body(*refs):
        l_refs, r_ref = refs[:n_l], refs[n_l]
        o_refs, accs = refs[n_l + 1:2 * n_l + 1], refs[2 * n_l + 1:]
        k = pl.program_id(1)
        rv = r_ref[...].astype(BF16)
        for l_ref, o_ref, acc in zip(l_refs, o_refs, accs):
            @pl.when(k == 0)
            def _():
                acc[...] = jnp.zeros_like(acc)

            acc[...] += _dot_tn(l_ref[...], rv)

            @pl.when(k == nk - 1)
            def _():
                o_ref[...] = acc[...].astype(BF16)

    return pl.pallas_call(
        body, name=name, grid=(M // tm, nk),
        in_specs=[pl.BlockSpec((tk, tm), lambda i, k: (k, i))] * n_l + [pl.BlockSpec((tk, N), lambda i, k: (k, 0))],
        out_specs=[pl.BlockSpec((tm, N), lambda i, k: (i, 0))] * n_l,
        out_shape=[jax.ShapeDtypeStruct((M, N), BF16)] * n_l,
        scratch_shapes=[pltpu.VMEM((tm, N), F32)] * n_l,
        compiler_params=_cparams(("parallel", "arbitrary")),
    )(*lhs, rhs)


def proj_bwd(dproj, w, x, g, dres, name, dep=None):
    S, N = dproj.shape

    def body(dp_ref, w_ref, x_ref, g_ref, dr_ref, dx_ref, dg_ref):
        i = pl.program_id(0)

        @pl.when(i == 0)
        def _():
            dg_ref[...] = jnp.zeros_like(dg_ref)

        dh = _dot(dp_ref[...], w_ref[...])
        xf = x_ref[...]
        r = _rms(xf)
        xh = xf * r
        dg_ref[...] += jnp.sum(dh * xh, axis=0, keepdims=True)
        dx_ref[...] = dr_ref[...] + _rms_bwd(dh, xh, r, g_ref[...])

    return _pallas_after(
        dep, body, dproj, w, x, g, dres, name=name, grid=(S // TM,),
        in_specs=[pl.BlockSpec((TM, N), lambda i: (i, 0)), pl.BlockSpec((N, D_MODEL), lambda i: (0, 0)),
                  pl.BlockSpec((TM, D_MODEL), lambda i: (i, 0)), pl.BlockSpec((1, D_MODEL), lambda i: (0, 0)),
                  pl.BlockSpec((TM, D_MODEL), lambda i: (i, 0))],
        out_specs=[pl.BlockSpec((TM, D_MODEL), lambda i: (i, 0)), pl.BlockSpec((1, D_MODEL), lambda i: (0, 0))],
        out_shape=[jax.ShapeDtypeStruct((S, D_MODEL), F32), jax.ShapeDtypeStruct((1, D_MODEL), F32)],
        compiler_params=_cparams(("arbitrary",)),
    )


def mixer_bwd(proj, bias, y, ao, dx1, w_o, sinks, conv_w, qg, kg, cog, aog, name, dep=None):
    S = proj.shape[0]
    SUB, ROWS = SUB_BWD, SUB_BWD * BLK
    steps = S // ROWS
    KV_W = 2 * PAIR

    def body(sinks_ref, p_ref, kvp_ref, bias_ref, y_ref, ao_ref, dx_first_ref, dx_next_ref, wo_ref, cw_ref, qg_ref,
             kg_ref, cog_ref, aog_ref, dp_ref, dcw_ref, dqg_ref, dkg_ref, dsk_ref, dcog_ref, daog_ref,
             dycar, kcar, vcar, dmix_scr):
        step = pl.program_id(0)

        @pl.when(step == 0)
        def _():
            dmix_scr[...] = _dot_nt(dx_first_ref[...].astype(BF16), wo_ref[...])
            dycar[...] = jnp.zeros_like(dycar)
            kcar[...] = jnp.zeros_like(kcar)
            vcar[...] = jnp.zeros_like(vcar)
            dcw_ref[...] = jnp.zeros_like(dcw_ref)
            dqg_ref[...] = jnp.zeros_like(dqg_ref)
            dkg_ref[...] = jnp.zeros_like(dkg_ref)
            dsk_ref[...] = jnp.zeros_like(dsk_ref)
            dcog_ref[...] = jnp.zeros_like(dcog_ref)
            daog_ref[...] = jnp.zeros_like(daog_ref)

        dma = dmix_scr[:, CONV_CH:]

        aov = ao_ref[...].astype(F32)
        ra = _rms(aov)
        ah = aov * ra
        daog_ref[...] += jnp.sum(dma * ah, axis=0, keepdims=True)
        dao = _rms_bwd(dma, ah, ra, aog_ref[...])

        low = _low_half()
        high = jnp.logical_not(low)
        q_gain = qg_ref[...] * SCALE
        k, v = _kv_pairs(p_ref, kvp_ref)
        rk = _pair_rms(k, low)
        kh = k * rk
        k_of = _one_head_in_both_halves((kh * kg_ref[...]).astype(BF16), low)
        v_of = _one_head_in_both_halves(v, low)
        lane8 = lax.broadcasted_iota(jnp.int32, (1, N_Q), 1)
        dsk = jnp.zeros((1, N_Q), F32)
        dqg = jnp.zeros((1, PAIR), F32)
        biases = [bias_ref[jnp.minimum(steps - 1 - step, 1)]] + [bias_ref[1]] * (SUB - 1)
        units = [(s, hq) for s in range(SUB) for hq in range(N_Q)]
        rq, qh, q_one, do_one, delta = [], [], [], [], []
        for pair in range(N_Q // 2):
            cols = slice(PAIR * pair, PAIR * (pair + 1))
            q = p_ref[:, O_Q + PAIR * pair:O_Q + PAIR * (pair + 1)].astype(F32)
            rq.append(_pair_rms(q, low))
            qh.append(q * rq[pair])
            qn = (qh[pair] * q_gain).astype(BF16)
            do = dao[:, cols]
            do_b = do.astype(BF16)
            delta += _half_sums(do * aov[:, cols], low)
            for mine in (low, high):
                q_one.append(jnp.where(mine, qn, jnp.zeros_like(qn)))
                do_one.append(jnp.where(mine, do_b, jnp.zeros_like(do_b)))
        scores = {u: _dot_nt(q_one[u[1]][_block(u[0])], k_of[u[1] // GRP][_band(u[0])]) + biases[u[0]] for u in units}
        dprobs = {u: _dot_nt(do_one[u[1]][_block(u[0])], v_of[u[1] // GRP][_band(u[0])]) for u in units}
        probs = {u: _softmax_with_sink(scores[u], sinks_ref[u[1]]) for u in units}
        ds = {u: (probs[u][0] * (dprobs[u] - delta[u[1]][_block(u[0])])).astype(BF16) for u in units}
        dv_t = {(s, h): jnp.zeros((PAIR, 2 * BLK), F32) for s in range(SUB) for h in range(N_KV)}
        dkn_t = dict(dv_t)
        for s, hq in units:
            dsk = dsk - jnp.where(lane8 == hq, jnp.sum(probs[s, hq][1] * delta[hq][_block(s)]), 0.0)
            dv_t[s, hq // GRP] = dv_t[s, hq // GRP] + _dot_tn(do_one[hq][_block(s)], probs[s, hq][0].astype(BF16))
            dkn_t[s, hq // GRP] = dkn_t[s, hq // GRP] + _dot_tn(q_one[hq][_block(s)], ds[s, hq])
        dqn_of = {u: _dot(ds[u], k_of[u[1] // GRP][_band(u[0])]) for u in units}
        for pair in range(N_Q // 2):
            dqn = jnp.concatenate([jnp.where(low, dqn_of[s, 2 * pair], dqn_of[s, 2 * pair + 1]) for s in range(SUB)],
                                  axis=0)
            dqg = dqg + jnp.sum(dqn * qh[pair], axis=0, keepdims=True)
            dqh = dqn * q_gain
            dp_ref[:, O_Q + PAIR * pair:O_Q + PAIR * (pair + 1)] = (
                rq[pair] * (dqh - qh[pair] * _pair_mean(dqh * qh[pair], low))).astype(BF16)

        def over_key_rows(parts):
            bands = [jnp.concatenate([parts[s, h][:HEAD] + parts[s, h][HEAD:] for h in range(N_KV)], axis=0).T
                     for s in range(SUB)]
            pieces = [bands[0][:BLK]]
            pieces += [bands[s - 1][BLK:] + bands[s][:BLK] for s in range(1, SUB)]
            pieces.append(bands[SUB - 1][BLK:])
            return jnp.concatenate(pieces, axis=0)

        dv = over_key_rows(dv_t)
        dkn = over_key_rows(dkn_t)
        dkg_ref[...] += jnp.sum(dkn * kh, axis=0, keepdims=True)
        dkh = dkn * kg_ref[...]
        dk = rk * (dkh - kh * _pair_mean(dkh * kh, low))
        last = slice(ROWS, ROWS + BLK)
        dp_ref[:, O_K:O_V] = jnp.concatenate([dk[BLK:ROWS], dk[last] + kcar[...]], axis=0).astype(BF16)
        dp_ref[:, O_V:] = jnp.concatenate([dv[BLK:ROWS], dv[last] + vcar[...]], axis=0).astype(BF16)
        kcar[...] = dk[:BLK, :]
        vcar[...] = dv[:BLK, :]
        dsk_ref[...] += dsk
        dqg_ref[...] += dqg * SCALE

        bg = p_ref[:, 0:CONV_CH].astype(F32)
        cg = p_ref[:, CONV_CH:2 * CONV_CH].astype(F32)
        hc = p_ref[:, 2 * CONV_CH:3 * CONV_CH].astype(F32)
        yv = y_ref[...].astype(F32)
        dmc = dmix_scr[:, 0:CONV_CH]
        co = bg * yv
        rc = _rms(co)
        ch = co * rc
        dcog_ref[...] += jnp.sum(dmc * ch, axis=0, keepdims=True)
        dco = _rms_bwd(dmc, ch, rc, cog_ref[...])
        dp_ref[:, 0:CONV_CH] = (dco * yv).astype(BF16)
        dy = dco * bg
        row = lax.broadcasted_iota(jnp.int32, (ROWS, CONV_CH), 0)
        nxt = dycar[...]
        dy1 = jnp.where(row == ROWS - 1, nxt[0:1, :], pltpu.roll(dy, ROWS - 1, 0))
        dy2 = jnp.where(row == ROWS - 2, nxt[0:1, :],
                        jnp.where(row == ROWS - 1, nxt[1:2, :], pltpu.roll(dy, ROWS - 2, 0)))
        dycar[...] = dy[0:8, :]
        du = cw_ref[2:3, :] * dy + cw_ref[1:2, :] * dy1 + cw_ref[0:1, :] * dy2
        dp_ref[:, CONV_CH:2 * CONV_CH] = (du * hc).astype(BF16)
        dp_ref[:, 2 * CONV_CH:3 * CONV_CH] = (du * cg).astype(BF16)
        u = cg * hc
        dcw_ref[0:1, :] += jnp.sum(dy2 * u, axis=0, keepdims=True)
        dcw_ref[1:2, :] += jnp.sum(dy1 * u, axis=0, keepdims=True)
        dcw_ref[2:3, :] += jnp.sum(dy * u, axis=0, keepdims=True)

        dmix_scr[...] = _dot_nt(dx_next_ref[...].astype(BF16), wo_ref[...])

    small = lambda shape: pl.BlockSpec(shape, lambda s: (0, 0))
    blk = lambda w: pl.BlockSpec((ROWS, w), lambda s: (steps - 1 - s, 0))
    return _pallas_after(
        dep, body, sinks, proj, proj, bias, y, ao, dx1, dx1, w_o, conv_w, qg, kg, cog, aog, name=name, grid=(steps,),
        in_specs=[pl.BlockSpec(memory_space=pltpu.SMEM),
                  blk(IN_COLS),
                  pl.BlockSpec((BLK, KV_W), lambda s: (jnp.maximum(SUB * (steps - 1 - s) - 1, 0), O_K // KV_W)),
                  pl.BlockSpec((2, BLK, 2 * BLK), lambda s: (0, 0, 0)),
                  blk(CONV_CH), blk(ATTN_W),
                  pl.BlockSpec((ROWS, D_MODEL), lambda s: (steps - 1, 0)),
                  pl.BlockSpec((ROWS, D_MODEL), lambda s: (jnp.maximum(steps - 2 - s, 0), 0)),
                  small((D_MODEL, D_MODEL)),
                  small((3, CONV_CH)), small((1, PAIR)), small((1, PAIR)), small((1, CONV_CH)), small((1, ATTN_W))],
        out_specs=[blk(IN_COLS), small((3, CONV_CH)), small((1, PAIR)), small((1, PAIR)), small((1, N_Q)),
                   small((1, CONV_CH)), small((1, ATTN_W))],
        out_shape=[jax.ShapeDtypeStruct((S, IN_COLS), BF16), jax.ShapeDtypeStruct((3, CONV_CH), F32),
                   jax.ShapeDtypeStruct((1, PAIR), F32), jax.ShapeDtypeStruct((1, PAIR), F32),
                   jax.ShapeDtypeStruct((1, N_Q), F32), jax.ShapeDtypeStruct((1, CONV_CH), F32),
                   jax.ShapeDtypeStruct((1, ATTN_W), F32)],
        scratch_shapes=[pltpu.VMEM((8, CONV_CH), F32), pltpu.VMEM((BLK, PAIR), F32), pltpu.VMEM((BLK, PAIR), F32),
                        pltpu.VMEM((ROWS, D_MODEL), F32)],
        compiler_params=_cparams(("arbitrary",)),
    )


OTHER_CHIPS = ((1, 0), (0, 1), (1, 1))


def _position():
    return lax.axis_index("x"), lax.axis_index("y"), lax.axis_index("c")


def all_gather(shards, name):
    n = len(shards)

    def body(*refs):
        x_refs, out_refs = refs[:n], refs[n:2 * n]
        send_sems, recv_sems, local_sems = refs[2 * n:]
        x, y, c = _position()
        me, sibling = (x, y, c), (x, y, 1 - c)
        chips = [(x ^ mx, y ^ my) for mx, my in OTHER_CHIPS]

        def slab(a, px, py, pc):
            return out_refs[a].at[4 * px + 2 * py + pc]

        def copy(a, k, block, to, src=None):
            return pltpu.make_async_remote_copy(
                src_ref=slab(a, *block) if src is None else src, dst_ref=slab(a, *block),
                send_sem=send_sems.at[7 * a + k], recv_sem=recv_sems.at[7 * a + k], device_id=to, device_id_type=MESH_T)

        mine = [pltpu.make_async_copy(x_refs[a], slab(a, *me), local_sems.at[a]) for a in range(n)]
        for cp in mine:
            cp.start()
        first = []
        for a in range(n):
            first.append(copy(a, 0, me, sibling, src=x_refs[a]))
            first += [copy(a, 1 + j, me, (*chip, c), src=x_refs[a]) for j, chip in enumerate(chips)]
        for cp in first:
            cp.start()
        passed = []
        for a in range(n):
            for j, chip in enumerate(chips):
                copy(a, 1 + j, (*chip, c), me).wait_recv()
                passed.append(copy(a, 4 + j, (*chip, c), sibling))
                passed[-1].start()
        for a in range(n):
            copy(a, 0, sibling, me).wait_recv()
            for j, chip in enumerate(chips):
                copy(a, 4 + j, (*chip, 1 - c), me).wait_recv()
        for cp in first + passed:
            cp.wait_send()
        for cp in mine:
            cp.wait()

    return pl.pallas_call(
        body, name=name, in_specs=[ANY] * n, out_specs=[ANY] * n,
        out_shape=[jax.ShapeDtypeStruct((N_DEV, *s.shape), s.dtype) for s in shards],
        scratch_shapes=[pltpu.SemaphoreType.DMA((7 * n,)), pltpu.SemaphoreType.DMA((7 * n,)),
                        pltpu.SemaphoreType.DMA((n,))],
    )(*shards)


class SplitCopy:
    def __init__(self, name, arrays, n_copies, plan, after=None):
        n = len(arrays)
        self.name, self.n, self.n_copies, self.plan = name, n, n_copies, plan
        extra = [] if after is None else [after]

        def body(*refs):
            in_refs = refs[:n]
            send_sems, recv_sems = refs[n + len(extra)], refs[n + len(extra) + 1]
            token = refs[2 * n + len(extra) + 2]
            for k, (src, dst, to) in enumerate(plan(_position(), in_refs)):
                pltpu.make_async_remote_copy(src_ref=src, dst_ref=dst, send_sem=send_sems.at[k],
                                             recv_sem=recv_sems.at[k], device_id=to, device_id_type=MESH_T).start()
            token[...] = jnp.zeros_like(token)

        outs = pl.pallas_call(
            body, name=name + "_start",
            out_shape=(pltpu.SemaphoreType.DMA((n_copies,)), pltpu.SemaphoreType.DMA((n_copies,)),
                       *[pltpu.HBM(a.shape, a.dtype) for a in arrays], jax.ShapeDtypeStruct((8, 128), F32)),
            in_specs=[HBM] * n + [ANY] * len(extra),
            out_specs=(SEM, SEM, *[HBM] * n, pl.BlockSpec(memory_space=pltpu.VMEM)),
            input_output_aliases={i: 2 + i for i in range(n)},
            compiler_params=pltpu.CompilerParams(has_side_effects=pltpu.SideEffectType.DATAFLOW_SIDE_EFFECTING),
        )(*[pltpu.with_memory_space_constraint(a, pltpu.HBM) for a in arrays], *extra)
        self.send_sems, self.recv_sems = outs[0], outs[1]
        self.arrays, self.token = list(outs[2:2 + n]), outs[2 + n]

    def wait(self, after):
        n, plan = self.n, self.plan

        def body(*refs):
            in_refs, send_sems, recv_sems = refs[:n], refs[n], refs[n + 1]
            for k, (src, dst, to) in enumerate(plan(_position(), in_refs)):
                cp = pltpu.make_async_remote_copy(src_ref=src, dst_ref=dst, send_sem=send_sems.at[k],
                                                  recv_sem=recv_sems.at[k], device_id=to, device_id_type=MESH_T)
                cp.wait_send()
                cp.wait_recv()

        outs = pl.pallas_call(
            body, name=self.name + "_wait",
            out_shape=tuple(pltpu.HBM(a.shape, a.dtype) for a in self.arrays),
            in_specs=[HBM] * n + [SEM, SEM, ANY], out_specs=tuple([HBM] * n),
            input_output_aliases={i: i for i in range(n)},
            compiler_params=pltpu.CompilerParams(has_side_effects=pltpu.SideEffectType.DATAFLOW_SIDE_EFFECTING),
        )(*self.arrays, self.send_sems, self.recv_sems, after)
        return list(outs)


def gather_start(shards, me, name, after=None):
    n = len(shards)
    lands = [lax.dynamic_update_slice(lax.empty((N_DEV, *s.shape), s.dtype), s[None], (me, 0, 0)) for s in shards]

    def plan(pos, refs):
        x, y, c = pos
        return [(refs[a], refs[n + a].at[4 * x + 2 * y + c], (x ^ mx, y ^ my, c))
                for a in range(n) for mx, my in OTHER_CHIPS]

    return SplitCopy(name, list(shards) + lands, 3 * n, plan, after=after)


def sibling_start(lands, name):
    n = len(lands)

    def plan(pos, refs):
        x, y, c = pos
        return [(refs[a].at[2 * q + c], refs[a].at[2 * q + c], (x, y, 1 - c)) for a in range(n) for q in range(4)]

    return SplitCopy(name, list(lands), 4 * n, plan)


def scatter_start(slabs, name):
    n = len(slabs)
    lands = [lax.empty((N_DEV - 1, *g.shape[1:]), g.dtype) for g in slabs]

    def plan(pos, refs):
        x, y, c = pos
        copies = []
        for a in range(n):
            for r in range(1, N_DEV):
                px, py, pc = x ^ ((r >> 2) & 1), y ^ ((r >> 1) & 1), c ^ (r & 1)
                copies.append((refs[a].at[4 * px + 2 * py + pc], refs[n + a].at[r - 1], (px, py, pc)))
        return copies

    return SplitCopy(name, list(slabs) + lands, (N_DEV - 1) * n, plan)


def all_reduce_small(v, name):
    R, W = v.shape

    def body(v_ref, o_ref, recv, send_sems, recv_sems):
        x, y, c = _position()
        me = 4 * x + 2 * y + c
        copies = []
        for r in range(1, N_DEV):
            to = (x ^ ((r >> 2) & 1), y ^ ((r >> 1) & 1), c ^ (r & 1))
            copies.append(pltpu.make_async_remote_copy(
                src_ref=v_ref, dst_ref=recv.at[me], send_sem=send_sems.at[r - 1], recv_sem=recv_sems.at[r - 1],
                device_id=to, device_id_type=MESH_T))
        for cp in copies:
            cp.start()
        recv[pl.ds(me, 1)] = v_ref[...][None]
        for cp in copies:
            cp.wait()
        acc = recv[0]
        for s in range(1, N_DEV):
            acc = acc + recv[s]
        o_ref[...] = acc

    return pl.pallas_call(
        body, name=name,
        in_specs=[pl.BlockSpec(memory_space=pltpu.VMEM)], out_specs=pl.BlockSpec(memory_space=pltpu.VMEM),
        out_shape=jax.ShapeDtypeStruct((R, W), F32),
        scratch_shapes=[pltpu.VMEM((N_DEV, R, W), F32), pltpu.SemaphoreType.DMA((N_DEV - 1,)),
                        pltpu.SemaphoreType.DMA((N_DEV - 1,))],
    )(v)


def _row_tile(rows):
    if rows <= 512:
        return rows
    return max(t for t in range(8, 513, 8) if rows % t == 0)


def _adamw_update(w, g, m, v):
    mn = ADAM_B1 * m + (1.0 - ADAM_B1) * g
    vn = ADAM_B2 * v + (1.0 - ADAM_B2) * (g * g)
    m_hat = mn / (1.0 - ADAM_B1 ** ADAM_STEP)
    v_hat = vn / (1.0 - ADAM_B2 ** ADAM_STEP)
    return -ADAM_LR * (m_hat / (jnp.sqrt(v_hat) + ADAM_EPS) + ADAM_WD * w), mn, vn


def adamw(w, g, m, v, name):
    R, W = w.shape
    tr = _row_tile(R)

    def body(w_ref, g_ref, m_ref, v_ref, d_ref, mo_ref, vo_ref):
        d_ref[...], mo_ref[...], vo_ref[...] = _adamw_update(w_ref[...], g_ref[...], m_ref[...], v_ref[...])

    spec = pl.BlockSpec((tr, W), lambda i: (i, 0))
    return pl.pallas_call(
        body, name=name, grid=(R // tr,), in_specs=[spec] * 4, out_specs=[spec] * 3,
        out_shape=[jax.ShapeDtypeStruct((R, W), F32)] * 3,
        compiler_params=_cparams(("parallel",)),
    )(w, g, m, v)


def reduce_adamw(slabs, land, w, m, v, layer, me_arr, name, others=None):
    _, R, W = slabs.shape
    tr = max(t for t in range(16, R // 2 + 1, 16) if R % t == 0)
    n_other = 0 if others is None else 4

    def body(me_ref, s_ref, l_ref, w_ref, m_ref, v_ref, *rest):
        g_ref, d_ref, mo_ref, vo_ref = rest[n_other:]
        g = s_ref[0].astype(F32)
        for r in range(N_DEV - 1):
            g = g + l_ref[r].astype(F32)
        g_ref[0] = g
        d_ref[0], mo_ref[0], vo_ref[0] = _adamw_update(w_ref[0], g, m_ref[0], v_ref[0])

    spec = pl.BlockSpec((1, tr, W), lambda i, me: (layer, i, 0))
    return pl.pallas_call(
        body, name=name,
        grid_spec=pltpu.PrefetchScalarGridSpec(
            num_scalar_prefetch=1, grid=(R // tr,),
            in_specs=[pl.BlockSpec((1, tr, W), lambda i, me: (me[0], i, 0)),
                      pl.BlockSpec((N_DEV - 1, tr, W), lambda i, me: (0, i, 0)), spec, spec, spec] + [ANY] * n_other,
            out_specs=[spec] * 4),
        out_shape=[jax.ShapeDtypeStruct((DEPTH, R, W), F32)] * 4,
        input_output_aliases={6 + i: i for i in range(n_other)},
        compiler_params=_cparams(("parallel",)),
    )(me_arr, slabs, land, w, m, v, *([] if others is None else others))


SMALL_NAMES = ("norm1_g", "q_norm_g", "k_norm_g", "sinks", "conv_out_g", "attn_out_g", "norm2_g", "conv_w")
SMALL_SIZES = (D_MODEL, HEAD, HEAD, N_Q, CONV_CH, ATTN_W, D_MODEL, 3 * CONV_CH)
SMALL_ROWS = 80


def kernel(x, norm1_g, w_in, conv_w, q_norm_g, k_norm_g, sinks, conv_out_g, attn_out_g, w_o, norm2_g, w_gate, w_up, w_down, loss_target, m_norm1_g, m_w_in, m_conv_w, m_q_norm_g, m_k_norm_g, m_sinks, m_conv_out_g, m_attn_out_g, m_w_o, m_norm2_g, m_w_gate, m_w_up, m_w_down, v_norm1_g, v_w_in, v_conv_w, v_q_norm_g, v_k_norm_g, v_sinks, v_conv_out_g, v_attn_out_g, v_w_o, v_norm2_g, v_w_gate, v_w_up, v_w_down):
    xi, yi, ci = _position()
    me = 4 * xi + 2 * yi + ci
    me_arr = jnp.reshape(me, (1,)).astype(jnp.int32)
    xs, tgt = x[0], loss_target[0]
    bf = lambda a: a.astype(BF16)
    bias = band_bias()

    t = lambda a: jnp.swapaxes(a, 1, 2)
    shard = dict(w_in=(t(w_in), t(m_w_in), t(v_w_in)), w_o=(w_o, m_w_o, v_w_o),
                 w_gate=(t(w_gate), t(m_w_gate), t(v_w_gate)), w_up=(t(w_up), t(m_w_up), t(v_w_up)),
                 w_down=(w_down, m_w_down, v_w_down))
    wb = {n: bf(shard[n][0]) for n in shard}

    g_in0, g_o0, g_conv = all_gather([wb["w_in"][0], wb["w_o"][0], conv_w.reshape(DEPTH * 3, HEAD)], "gather_first")
    ag_ffn0 = gather_start([wb["w_gate"][0], wb["w_up"][0], wb["w_down"][0]], me, "gather_ffn0", after=g_in0)
    ag_mix1 = gather_start([wb["w_in"][1], wb["w_o"][1]], me, "gather_mix1", after=ag_ffn0.token)
    ag_ffn1 = gather_start([wb["w_gate"][1], wb["w_up"][1], wb["w_down"][1]], me, "gather_ffn1", after=ag_mix1.token)
    conv_full = g_conv.reshape(N_DEV, DEPTH, 3, HEAD).transpose(1, 2, 0, 3).reshape(DEPTH, 3, CONV_CH)
    pair_gain = lambda g: jnp.tile(g[None], (1, 2))
    small = [dict(norm1_g=norm1_g[l][None], conv_w=conv_full[l], q_norm_g=pair_gain(q_norm_g[l]),
                  k_norm_g=pair_gain(k_norm_g[l]), sinks=sinks[l], conv_out_g=conv_out_g[l][None],
                  attn_out_g=attn_out_g[l][None], norm2_g=norm2_g[l][None]) for l in range(DEPTH)]
    whole = lambda g: g.reshape(-1, D_MODEL)
    weights = [dict(w_in=whole(g_in0), w_o=whole(g_o0)), {}]

    def ffn_weights(g_gate, g_up, g_down):
        return dict(w_gate=whole(g_gate), w_up=whole(g_up), w_down=whole(g_down))

    saved = []
    xl = xs
    for l in range(DEPTH):
        sp, wl = small[l], weights[l]
        h, proj = norm_proj(xl, sp["norm1_g"], wl["w_in"], f"norm_proj{l}",
                            dep=ag_ffn1.token if l == 0 else pass_ffn1.token)
        mix, y, ao, x1 = mixer_fwd(proj, bias, sp["sinks"], sp["conv_w"], sp["q_norm_g"], sp["k_norm_g"],
                                   sp["conv_out_g"], sp["attn_out_g"], f"mixer_fwd{l}", residual=xl, w_o=wl["w_o"])
        if l == 0:
            pass_ffn0 = sibling_start(ag_ffn0.wait(x1)[3:], "pass_ffn0")
            wl.update(ffn_weights(*pass_ffn0.wait(pass_ffn0.token)))
            pass_mix1 = sibling_start(ag_mix1.wait(x1)[2:], "pass_mix1")
            h2, a, b, f, x2 = ffn_fwd(x1, sp["norm2_g"], wl["w_gate"], wl["w_up"], wl["w_down"], "ffn_fwd0",
                                      dep=pass_mix1.token)
            g_in, g_o = pass_mix1.wait(x2)
            weights[1] = dict(w_in=whole(g_in), w_o=whole(g_o))
            pass_ffn1 = sibling_start(ag_ffn1.wait(x2)[3:], "pass_ffn1")
        else:
            wl.update(ffn_weights(*pass_ffn1.wait(x1)))
            h2, a, b, f, dx, loss_row = ffn_fwd(x1, sp["norm2_g"], wl["w_gate"], wl["w_up"], wl["w_down"], "ffn_fwd1",
                                                tgt=tgt)
            x2 = None
        saved.append((xl, h, proj, mix, y, ao, x1, h2, a, b, f))
        xl = x2

    stepped = {n: None for n in shard}
    slabs = lambda d: d.reshape(N_DEV, -1, D_MODEL)
    gsmall = [None] * DEPTH

    def finish(sc, names, after, l):
        arrays = sc.wait(after)
        k = len(names)
        for i, n in enumerate(names):
            w, m, v = shard[n]
            stepped[n] = reduce_adamw(arrays[i], arrays[k + i], w, m, v, l, me_arr, f"reduce_adamw_{n}{l}",
                                      others=stepped[n])

    for l in reversed(range(DEPTH)):
        sp, wl = small[l], weights[l]
        x0, h, proj, mix, y, ao, x1, h2, a, b, f = saved[l]
        (d_wd,) = grad_weight((f,), dx, TF, f"grad_w_down{l}", tk=2048)
        sc_down = scatter_start([slabs(d_wd)], f"scatter_w_down{l}")
        da, db, dx1, d_g2 = ffn_bwd(dx, a, b, x1, sp["norm2_g"], wl["w_gate"], wl["w_up"], wl["w_down"],
                                    f"ffn_bwd{l}", dep=sc_down.token)
        d_wg, d_wu = grad_weight((da, db), h2, TF, f"grad_w_gate_up{l}")
        (d_wo,) = grad_weight((mix,), dx1, D_MODEL, f"grad_w_o{l}", tk=2048)
        sc_rest = scatter_start([slabs(d_wg), slabs(d_wu), slabs(d_wo)], f"scatter_w_gate_up_o{l}")
        dproj, d_cw, d_qg, d_kg, d_sk, d_cog, d_aog = mixer_bwd(
            proj, bias, y, ao, dx1, wl["w_o"], sp["sinks"], sp["conv_w"], sp["q_norm_g"], sp["k_norm_g"],
            sp["conv_out_g"], sp["attn_out_g"], f"mixer_bwd{l}", dep=sc_rest.token)
        finish(sc_down, ["w_down"], dproj, l)
        finish(sc_rest, ["w_gate", "w_up", "w_o"], dproj, l)
        (d_win,) = grad_weight((dproj,), h, IN_COLS // 2, f"grad_w_in{l}", tk=2048)
        sc_in = scatter_start([slabs(d_win)], f"scatter_w_in{l}")
        dx, d_g1 = proj_bwd(dproj, wl["w_in"], x0, sp["norm1_g"], dx1, f"proj_bwd{l}", dep=sc_in.token)
        finish(sc_in, ["w_in"], dx, l)
        both_heads = lambda d: d[:, :HEAD] + d[:, HEAD:]
        gsmall[l] = dict(norm1_g=d_g1, conv_w=d_cw, q_norm_g=both_heads(d_qg), k_norm_g=both_heads(d_kg), sinks=d_sk,
                         conv_out_g=d_cog, attn_out_g=d_aog, norm2_g=d_g2)
    grad_x = dx[None]

    flat = jnp.concatenate([loss_row[0, 0:1]] + [gsmall[l][n].reshape(-1) for l in range(DEPTH) for n in SMALL_NAMES])
    flat = jnp.pad(flat, (0, SMALL_ROWS * 128 - flat.shape[0])).reshape(SMALL_ROWS, 128)
    flat = all_reduce_small(flat, "all_reduce_small_grads").reshape(-1)
    loss = flat[0]
    gs = {n: [] for n in SMALL_NAMES}
    off = 1
    for l in range(DEPTH):
        for n, size in zip(SMALL_NAMES, SMALL_SIZES):
            gs[n].append(flat[off:off + size])
            off += size
    gs = {n: jnp.stack(v) for n, v in gs.items()}
    g_conv = lax.dynamic_slice(gs["conv_w"].reshape(DEPTH, 3, CONV_CH), (0, 0, me * HEAD), (DEPTH, 3, HEAD))

    gs["conv_w"] = g_conv
    params = dict(norm1_g=(norm1_g, m_norm1_g, v_norm1_g), conv_w=(conv_w, m_conv_w, v_conv_w),
                  q_norm_g=(q_norm_g, m_q_norm_g, v_q_norm_g), k_norm_g=(k_norm_g, m_k_norm_g, v_k_norm_g),
                  sinks=(sinks, m_sinks, v_sinks), conv_out_g=(conv_out_g, m_conv_out_g, v_conv_out_g),
                  attn_out_g=(attn_out_g, m_attn_out_g, v_attn_out_g), norm2_g=(norm2_g, m_norm2_g, v_norm2_g))
    names = ("norm1_g", "w_in", "conv_w", "q_norm_g", "k_norm_g", "sinks", "conv_out_g", "attn_out_g", "w_o",
             "norm2_g", "w_gate", "w_up", "w_down")

    out = {}
    for n in names:
        if n in shard:
            out[n] = tuple(t(r) for r in stepped[n]) if n in ("w_in", "w_gate", "w_up") else stepped[n]
        else:
            w, m, v = params[n]
            two_d = (-1, w.shape[-1])
            d, mn, vn = adamw(w.reshape(two_d), gs[n].reshape(two_d), m.reshape(two_d), v.reshape(two_d), f"adamw_{n}")
            out[n] = (gs[n].reshape(w.shape), d.reshape(w.shape), mn.reshape(w.shape), vn.reshape(w.shape))
    return (loss, grad_x, *[out[n][i] for i in range(4) for n in names])
```

```python
import jax
import jax.numpy as jnp
from jax import lax
from jax.experimental import pallas as pl
from jax.experimental.pallas import tpu as pltpu

F32 = jnp.float32
BF16 = jnp.bfloat16

D_MODEL = 1024
CONV_CH = 512
ATTN_W = 512
N_Q = 8
N_KV = 2
GRP = N_Q // N_KV
HEAD = 64
IN_COLS = 2304
D_FF = 2816
BLK = 128
O_Q = 3 * CONV_CH
O_K = O_Q + ATTN_W
O_V = O_K + N_KV * HEAD
EPS = 1e-6
NEG_INF = -1e30
SCALE = HEAD ** -0.5
N_DEV = 8
DEPTH = 2

ADAM_LR = 0.001
ADAM_B1 = 0.9
ADAM_B2 = 0.999
ADAM_EPS = 1e-08
ADAM_WD = 0.01
ADAM_STEP = 10

VMEM_LIMIT = 56 * 1024 * 1024
TM = 512
MESH_T = pl.DeviceIdType.MESH

ANY = pl.BlockSpec(memory_space=pl.ANY)
HBM = pl.BlockSpec(memory_space=pltpu.HBM)
SEM = pl.BlockSpec(memory_space=pltpu.SEMAPHORE)


def _cparams(sem):
    return pltpu.CompilerParams(dimension_semantics=sem, vmem_limit_bytes=VMEM_LIMIT)


def _pallas_after(dep, body, *args, in_specs, **kw):
    if dep is None:
        return pl.pallas_call(body, in_specs=in_specs, **kw)(*args)

    def after_dep(dep_ref, *refs):
        body(*refs)

    return pl.pallas_call(after_dep, in_specs=[ANY, *in_specs], **kw)(dep, *args)


def _dot(a, b):
    return jnp.dot(a, b, preferred_element_type=F32)


def _dot_nt(a, b):
    return lax.dot_general(a, b, (((1,), (1,)), ((), ())), preferred_element_type=F32)


def _dot_tn(a, b):
    return lax.dot_general(a, b, (((0,), (0,)), ((), ())), preferred_element_type=F32)


LANES = 128


def _row_reduce(v, op, reduce):
    w = v.shape[-1]
    if w > LANES and w % LANES == 0:
        acc = v[:, 0:LANES]
        for c in range(1, w // LANES):
            acc = op(acc, v[:, LANES * c:LANES * (c + 1)])
        v = acc
    return reduce(v, axis=-1, keepdims=True)


def _row_sum(v):
    return _row_reduce(v, jnp.add, jnp.sum)


def _row_mean(v):
    return _row_sum(v) * (1.0 / v.shape[-1])


def _rms(v):
    return lax.rsqrt(_row_mean(v * v) + EPS)


def _sigmoid(v):
    return 1.0 / (1.0 + jnp.exp(-v))


def _rms_bwd(dyv, xh, r, g):
    dxh = dyv * g
    return r * (dxh - xh * _row_mean(dxh * xh))


def norm_proj(x, g, w_t, name, dep=None):
    S, N = x.shape[0], w_t.shape[0]
    tm = 2 * TM

    def body(x_ref, g_ref, w_ref, h_ref, p_ref):
        xf = x_ref[...]
        h = ((xf * _rms(xf)) * g_ref[...]).astype(BF16)
        h_ref[...] = h
        p_ref[...] = _dot_nt(h, w_ref[...]).astype(BF16)

    return _pallas_after(
        dep, body, x, g, w_t, name=name, grid=(S // tm,),
        in_specs=[pl.BlockSpec((tm, D_MODEL), lambda i: (i, 0)),
                  pl.BlockSpec((1, D_MODEL), lambda i: (0, 0)),
                  pl.BlockSpec((N, D_MODEL), lambda i: (0, 0))],
        out_specs=[pl.BlockSpec((tm, D_MODEL), lambda i: (i, 0)),
                   pl.BlockSpec((tm, N), lambda i: (i, 0))],
        out_shape=[jax.ShapeDtypeStruct((S, D_MODEL), BF16), jax.ShapeDtypeStruct((S, N), BF16)],
        compiler_params=_cparams(("parallel",)),
    )


def band_bias():
    qi = lax.broadcasted_iota(jnp.int32, (BLK, 2 * BLK), 0)
    kj = lax.broadcasted_iota(jnp.int32, (BLK, 2 * BLK), 1)
    diff = qi + BLK - kj
    valid = (diff >= 0) & (diff < BLK)
    return jnp.stack([jnp.where(valid & (kj >= BLK), 0.0, NEG_INF), jnp.where(valid, 0.0, NEG_INF)]).astype(F32)


def _softmax_with_sink(s, sink):
    m = jnp.maximum(_row_reduce(s, jnp.maximum, jnp.max), sink)
    p = jnp.exp(s - m)
    es = jnp.exp(sink - m)
    inv = 1.0 / (_row_sum(p) + es)
    return p * inv, es * inv


PAIR = 2 * HEAD


def _low_half():
    return lax.broadcasted_iota(jnp.int32, (1, PAIR), 1) < HEAD


def _half_sums(v, low):
    return (jnp.sum(jnp.where(low, v, 0.0), axis=-1, keepdims=True),
            jnp.sum(jnp.where(low, 0.0, v), axis=-1, keepdims=True))


def _pair_mean(v, low):
    e, o = _half_sums(v, low)
    return jnp.where(low, e, o) * (1.0 / HEAD)


def _pair_rms(v, low):
    return lax.rsqrt(_pair_mean(v * v, low) + EPS)


def _one_head_in_both_halves(pair, low):
    swapped = pltpu.roll(pair, HEAD, 1)
    return jnp.where(low, pair, swapped), jnp.where(low, swapped, pair)


SUB_FWD, SUB_FWD_PROJECT, SUB_BWD = 2, 8, 4


def _kv_pairs(p_ref, kvp_ref):
    k = jnp.concatenate([kvp_ref[:, 0:PAIR], p_ref[:, O_K:O_K + PAIR]], axis=0).astype(F32)
    v = jnp.concatenate([kvp_ref[:, PAIR:2 * PAIR], p_ref[:, O_V:O_V + PAIR]], axis=0)
    return k, v


def _band(s):
    return slice(BLK * s, BLK * s + 2 * BLK)


def _block(s):
    return slice(BLK * s, BLK * (s + 1))


def mixer_fwd(proj, bias, sinks, conv_w, qg, kg, cog, aog, name, residual=None, w_o=None):
    S = proj.shape[0]
    project = w_o is not None
    SUB = SUB_FWD_PROJECT if project else SUB_FWD
    ROWS = SUB * BLK
    steps = S // ROWS

    def body(sinks_ref, p_ref, kvp_ref, bias_ref, cw_ref, qg_ref, kg_ref, cog_ref, aog_ref, *rest):
        if project:
            x_ref, wo_ref, mix_ref, y_ref, ao_ref, x1_ref, ucar = rest
        else:
            mix_ref, y_ref, ao_ref, ucar = rest
        n = pl.program_id(0)

        @pl.when(n == 0)
        def _():
            ucar[...] = jnp.zeros_like(ucar)

        bg = p_ref[:, 0:CONV_CH].astype(F32)
        u = p_ref[:, CONV_CH:2 * CONV_CH].astype(F32) * p_ref[:, 2 * CONV_CH:3 * CONV_CH].astype(F32)
        row = lax.broadcasted_iota(jnp.int32, (ROWS, CONV_CH), 0)
        prev = ucar[...]
        u1 = jnp.where(row == 0, prev[7:8, :], pltpu.roll(u, 1, 0))
        u2 = jnp.where(row == 0, prev[6:7, :], jnp.where(row == 1, prev[7:8, :], pltpu.roll(u, 2, 0)))
        ucar[...] = u[ROWS - 8:ROWS, :]
        y = cw_ref[0:1, :] * u2 + cw_ref[1:2, :] * u1 + cw_ref[2:3, :] * u
        y_ref[...] = y.astype(BF16)
        co = bg * y
        mix_conv = ((co * _rms(co)) * cog_ref[...]).astype(BF16)
        mix_ref[:, 0:CONV_CH] = mix_conv
        if project:
            x1 = x_ref[...] + _dot(mix_conv, wo_ref[0:CONV_CH, :])

        low = _low_half()
        q_gain = qg_ref[...] * SCALE
        k, v = _kv_pairs(p_ref, kvp_ref)
        kn = ((k * _pair_rms(k, low)) * kg_ref[...]).astype(BF16)
        k_of = _one_head_in_both_halves(kn, low)
        v_of = _one_head_in_both_halves(v, low)
        biases = [bias_ref[jnp.minimum(n, 1)]] + [bias_ref[1]] * (SUB - 1)
        units = [(s, hq) for s in range(SUB) for hq in range(N_Q)]
        q_one = []
        for pair in range(N_Q // 2):
            q = p_ref[:, O_Q + PAIR * pair:O_Q + PAIR * (pair + 1)].astype(F32)
            qn = ((q * _pair_rms(q, low)) * q_gain).astype(BF16)
            q_one += [jnp.where(mine, qn, jnp.zeros_like(qn)) for mine in (low, jnp.logical_not(low))]
        scores = [_dot_nt(q_one[hq][_block(s)], k_of[hq // GRP][_band(s)]) + biases[s] for s, hq in units]
        probs = [_softmax_with_sink(sc, sinks_ref[hq])[0].astype(BF16) for (s, hq), sc in zip(units, scores)]
        o = {u_: _dot(pn, v_of[u_[1] // GRP][_band(u_[0])]) for u_, pn in zip(units, probs)}
        ao = jnp.concatenate(
            [jnp.concatenate([jnp.where(low, o[s, 2 * pair], o[s, 2 * pair + 1]) for pair in range(N_Q // 2)], axis=1)
             for s in range(SUB)], axis=0)
        ao_ref[...] = ao.astype(BF16)
        mix_attn = ((ao * _rms(ao)) * aog_ref[...]).astype(BF16)
        mix_ref[:, CONV_CH:] = mix_attn
        if project:
            x1_ref[...] = x1 + _dot(mix_attn, wo_ref[CONV_CH:, :])

    small = lambda shape: pl.BlockSpec(shape, lambda n: (0, 0))
    rows = lambda w: pl.BlockSpec((ROWS, w), lambda n: (n, 0))
    return pl.pallas_call(
        body, name=name, grid=(steps,),
        in_specs=[pl.BlockSpec(memory_space=pltpu.SMEM),
                  rows(IN_COLS),
                  pl.BlockSpec((BLK, 2 * PAIR), lambda n: (jnp.maximum(SUB * n - 1, 0), O_K // (2 * PAIR))),
                  pl.BlockSpec((2, BLK, 2 * BLK), lambda n: (0, 0, 0)),
                  small((3, CONV_CH)), small((1, PAIR)), small((1, PAIR)), small((1, CONV_CH)), small((1, ATTN_W))]
        + ([rows(D_MODEL), small((D_MODEL, D_MODEL))] if project else []),
        out_specs=[rows(D_MODEL), rows(CONV_CH), rows(ATTN_W)] + ([rows(D_MODEL)] if project else []),
        out_shape=[jax.ShapeDtypeStruct((S, D_MODEL), BF16), jax.ShapeDtypeStruct((S, CONV_CH), BF16),
                   jax.ShapeDtypeStruct((S, ATTN_W), BF16)]
        + ([jax.ShapeDtypeStruct((S, D_MODEL), F32)] if project else []),
        scratch_shapes=[pltpu.VMEM((8, CONV_CH), F32)],
        compiler_params=_cparams(("arbitrary",)),
    )(sinks, proj, proj, bias, conv_w, qg, kg, cog, aog, *([residual, w_o] if project else []))


def matmul_residual(a, w, res, name, dep=None):
    S, K = a.shape
    N = w.shape[1]

    def body(a_ref, w_ref, r_ref, o_ref):
        o_ref[...] = r_ref[...] + _dot(a_ref[...], w_ref[...])

    return _pallas_after(
        dep, body, a, w, res, name=name, grid=(S // TM,),
        in_specs=[pl.BlockSpec((TM, K), lambda i: (i, 0)), pl.BlockSpec((K, N), lambda i: (0, 0)),
                  pl.BlockSpec((TM, N), lambda i: (i, 0))],
        out_specs=pl.BlockSpec((TM, N), lambda i: (i, 0)),
        out_shape=jax.ShapeDtypeStruct((S, N), F32),
        compiler_params=_cparams(("parallel",)),
    )


TF = 1408


def ffn_fwd(x1, g, wg, wu, wd, name, dep=None, tgt=None):
    S = x1.shape[0]
    ni = S // TM
    with_loss = tgt is not None
    row = pl.BlockSpec((TM, D_MODEL), lambda i: (i, 0))
    vec = pl.BlockSpec((1, D_MODEL), lambda i: (0, 0))
    acts = [jax.ShapeDtypeStruct((S, D_FF), BF16)] * 3

    def half_weights(j):
        return [pl.BlockSpec((TF, D_MODEL), lambda i: (j, 0))] * 3

    def swiglu(h2, wg_ref, wu_ref, a_ref, b_ref, f_ref):
        a = _dot_nt(h2, wg_ref[...])
        b = _dot_nt(h2, wu_ref[...])
        a_ref[...] = a.astype(BF16)
        b_ref[...] = b.astype(BF16)
        f = ((a * _sigmoid(a)) * b).astype(BF16)
        f_ref[...] = f
        return f

    def first(x_ref, g_ref, wg_ref, wu_ref, wd_ref, h2_ref, a_ref, b_ref, f_ref, part_ref):
        xf = x_ref[...]
        h2 = ((xf * _rms(xf)) * g_ref[...]).astype(BF16)
        h2_ref[...] = h2
        part_ref[...] = xf + _dot(swiglu(h2, wg_ref, wu_ref, a_ref, b_ref, f_ref), wd_ref[...])

    h2, a, b, f, part = _pallas_after(
        dep, first, x1, g, wg, wu, wd, name=name + "_a", grid=(ni,),
        in_specs=[row, vec] + half_weights(0),
        out_specs=[row] + [pl.BlockSpec((TM, TF), lambda i: (i, 0))] * 3 + [row],
        out_shape=[jax.ShapeDtypeStruct((S, D_MODEL), BF16)] + acts + [jax.ShapeDtypeStruct((S, D_MODEL), F32)],
        compiler_params=_cparams(("parallel",)),
    )

    def second(h2_ref, part_ref, wg_ref, wu_ref, wd_ref, *rest):
        rest = rest[3:]
        if with_loss:
            t_ref, a_ref, b_ref, f_ref, o_ref, l_ref, sq = rest
        else:
            a_ref, b_ref, f_ref, o_ref = rest
        i = pl.program_id(0)
        out = part_ref[...] + _dot(swiglu(h2_ref[...], wg_ref, wu_ref, a_ref, b_ref, f_ref), wd_ref[...])
        if not with_loss:
            o_ref[...] = out
            return
        e = out - t_ref[...]
        o_ref[...] = e * (1.0 / D_MODEL)
        col = jnp.sum(e * e, axis=0, keepdims=True)

        @pl.when(i == 0)
        def _():
            sq[...] = col

        @pl.when(i > 0)
        def _():
            sq[...] += col

        @pl.when(i == ni - 1)
        def _():
            l_ref[...] = jnp.full((1, 128), jnp.sum(sq[...]) * (0.5 / D_MODEL), F32)

    results = pl.pallas_call(
        second, name=name + "_b", grid=(ni,),
        in_specs=[row, row] + half_weights(1) + [ANY] * 3 + ([row] if with_loss else []),
        out_specs=[pl.BlockSpec((TM, TF), lambda i: (i, 1))] * 3 + [row]
        + ([pl.BlockSpec((1, 128), lambda i: (0, 0))] if with_loss else []),
        out_shape=acts + [jax.ShapeDtypeStruct((S, D_MODEL), F32)]
        + ([jax.ShapeDtypeStruct((1, 128), F32)] if with_loss else []),
        input_output_aliases={5: 0, 6: 1, 7: 2},
        scratch_shapes=[pltpu.VMEM((1, D_MODEL), F32)] if with_loss else [],
        compiler_params=_cparams(("arbitrary",) if with_loss else ("parallel",)),
    )(h2, part, wg, wu, wd, a, b, f, *([tgt] if with_loss else []))
    return (h2, *results)


def ffn_bwd(dx2, a, b, x1, g, wg, wu, wd, name, dep=None):
    S = dx2.shape[0]
    ni = S // TM
    row = pl.BlockSpec((TM, D_MODEL), lambda i: (i, 0))
    vec = pl.BlockSpec((1, D_MODEL), lambda i: (0, 0))
    acts = [jax.ShapeDtypeStruct((S, D_FF), BF16)] * 2

    def half(j):
        return [pl.BlockSpec((TM, TF), lambda i: (i, j))] * 2, [pl.BlockSpec((TF, D_MODEL), lambda i: (j, 0))] * 3

    def through_half(dx_ref, a_ref, b_ref, wg_ref, wu_ref, wd_ref, da_ref, db_ref):
        df = _dot_nt(dx_ref[...].astype(BF16), wd_ref[...])
        av = a_ref[...].astype(F32)
        bv = b_ref[...].astype(F32)
        sg = _sigmoid(av)
        da = ((df * bv) * (sg * (1.0 + av * (1.0 - sg)))).astype(BF16)
        db = (df * (av * sg)).astype(BF16)
        da_ref[...] = da
        db_ref[...] = db
        return _dot(da, wg_ref[...]) + _dot(db, wu_ref[...])

    def first(dx_ref, a_ref, b_ref, wg_ref, wu_ref, wd_ref, da_ref, db_ref, part_ref):
        part_ref[...] = through_half(dx_ref, a_ref, b_ref, wg_ref, wu_ref, wd_ref, da_ref, db_ref)

    tiles, weights = half(0)
    da, db, part = _pallas_after(
        dep, first, dx2, a, b, wg, wu, wd, name=name + "_a", grid=(ni,),
        in_specs=[row] + tiles + weights, out_specs=tiles + [row],
        out_shape=acts + [jax.ShapeDtypeStruct((S, D_MODEL), F32)],
        compiler_params=_cparams(("parallel",)),
    )

    def second(dx_ref, a_ref, b_ref, wg_ref, wu_ref, wd_ref, part_ref, x_ref, g_ref, da_prev, db_prev,
               da_ref, db_ref, dx1_ref, dg_ref):
        i = pl.program_id(0)

        @pl.when(i == 0)
        def _():
            dg_ref[...] = jnp.zeros_like(dg_ref)

        dh = part_ref[...] + through_half(dx_ref, a_ref, b_ref, wg_ref, wu_ref, wd_ref, da_ref, db_ref)
        xf = x_ref[...]
        r = _rms(xf)
        xh = xf * r
        dg_ref[...] += jnp.sum(dh * xh, axis=0, keepdims=True)
        dx1_ref[...] = dx_ref[...] + _rms_bwd(dh, xh, r, g_ref[...])

    tiles, weights = half(1)
    return pl.pallas_call(
        second, name=name + "_b", grid=(ni,),
        in_specs=[row] + tiles + weights + [row, row, vec, ANY, ANY],
        out_specs=tiles + [row, vec],
        out_shape=acts + [jax.ShapeDtypeStruct((S, D_MODEL), F32), jax.ShapeDtypeStruct((1, D_MODEL), F32)],
        input_output_aliases={9: 0, 10: 1},
        compiler_params=_cparams(("arbitrary",)),
    )(dx2, a, b, wg, wu, wd, part, x1, g, da, db)


def grad_weight(lhs, rhs, tm, name, tk=1024):
    S, N = rhs.shape
    M = lhs[0].shape[1]
    tk = min(tk, S)
    nk = S // tk
    n_l = len(lhs)

    def body(*refs):
        l_refs, r_ref = refs[:n_l], refs[n_l]
        o_refs, accs = refs[n_l + 1:2 * n_l + 1], refs[2 * n_l + 1:]
        k = pl.program_id(1)
        rv = r_ref[...].astype(BF16)
        for l_ref, o_ref, acc in zip(l_refs, o_refs, accs):
            @pl.when(k == 0)
            def _():
                acc[...] = jnp.zeros_like(acc)

            acc[...] += _dot_tn(l_ref[...], rv)

            @pl.when(k == nk - 1)
            def _():
                o_ref[...] = acc[...].astype(BF16)

    return pl.pallas_call(
        body, name=name, grid=(M // tm, nk),
        in_specs=[pl.BlockSpec((tk, tm), lambda i, k: (k, i))] * n_l + [pl.BlockSpec((tk, N), lambda i, k: (k, 0))],
        out_specs=[pl.BlockSpec((tm, N), lambda i, k: (i, 0))] * n_l,
        out_shape=[jax.ShapeDtypeStruct((M, N), BF16)] * n_l,
        scratch_shapes=[pltpu.VMEM((tm, N), F32)] * n_l,
        compiler_params=_cparams(("parallel", "arbitrary")),
    )(*lhs, rhs)


def proj_bwd(dproj, w, x, g, dres, name, dep=None):
    S, N = dproj.shape

    def body(dp_ref, w_ref, x_ref, g_ref, dr_ref, dx_ref, dg_ref):
        i = pl.program_id(0)

        @pl.when(i == 0)
        def _():
            dg_ref[...] = jnp.zeros_like(dg_ref)

        dh = _dot(dp_ref[...], w_ref[...])
        xf = x_ref[...]
        r = _rms(xf)
        xh = xf * r
        dg_ref[...] += jnp.sum(dh * xh, axis=0, keepdims=True)
        dx_ref[...] = dr_ref[...] + _rms_bwd(dh, xh, r, g_ref[...])

    tm = 2 * TM
    return _pallas_after(
        dep, body, dproj, w, x, g, dres, name=name, grid=(S // tm,),
        in_specs=[pl.BlockSpec((tm, N), lambda i: (i, 0)),
                  pl.BlockSpec((N, D_MODEL), lambda i: (0, 0), pipeline_mode=pl.Buffered(1)),
                  pl.BlockSpec((tm, D_MODEL), lambda i: (i, 0)), pl.BlockSpec((1, D_MODEL), lambda i: (0, 0)),
                  pl.BlockSpec((tm, D_MODEL), lambda i: (i, 0))],
        out_specs=[pl.BlockSpec((tm, D_MODEL), lambda i: (i, 0)), pl.BlockSpec((1, D_MODEL), lambda i: (0, 0))],
        out_shape=[jax.ShapeDtypeStruct((S, D_MODEL), F32), jax.ShapeDtypeStruct((1, D_MODEL), F32)],
        compiler_params=_cparams(("arbitrary",)),
    )


def mixer_bwd(proj, bias, y, ao, dx1, w_o, sinks, conv_w, qg, kg, cog, aog, name, dep=None):
    S = proj.shape[0]
    SUB, ROWS = SUB_BWD, SUB_BWD * BLK
    steps = S // ROWS
    KV_W = 2 * PAIR

    def body(sinks_ref, p_ref, kvp_ref, bias_ref, y_ref, ao_ref, dx_first_ref, dx_next_ref, wo_ref, cw_ref, qg_ref,
             kg_ref, cog_ref, aog_ref, dp_ref, dcw_ref, dqg_ref, dkg_ref, dsk_ref, dcog_ref, daog_ref,
             dycar, kcar, vcar, dmix_scr):
        step = pl.program_id(0)

        @pl.when(step == 0)
        def _():
            dmix_scr[...] = _dot_nt(dx_first_ref[...].astype(BF16), wo_ref[...])
            dycar[...] = jnp.zeros_like(dycar)
            kcar[...] = jnp.zeros_like(kcar)
            vcar[...] = jnp.zeros_like(vcar)
            dcw_ref[...] = jnp.zeros_like(dcw_ref)
            dqg_ref[...] = jnp.zeros_like(dqg_ref)
            dkg_ref[...] = jnp.zeros_like(dkg_ref)
            dsk_ref[...] = jnp.zeros_like(dsk_ref)
            dcog_ref[...] = jnp.zeros_like(dcog_ref)
            daog_ref[...] = jnp.zeros_like(daog_ref)

        dma = dmix_scr[:, CONV_CH:]

        aov = ao_ref[...].astype(F32)
        ra = _rms(aov)
        ah = aov * ra
        daog_ref[...] += jnp.sum(dma * ah, axis=0, keepdims=True)
        dao = _rms_bwd(dma, ah, ra, aog_ref[...])

        low = _low_half()
        high = jnp.logical_not(low)
        q_gain = qg_ref[...] * SCALE
        k, v = _kv_pairs(p_ref, kvp_ref)
        rk = _pair_rms(k, low)
        kh = k * rk
        k_of = _one_head_in_both_halves((kh * kg_ref[...]).astype(BF16), low)
        v_of = _one_head_in_both_halves(v, low)
        lane8 = lax.broadcasted_iota(jnp.int32, (1, N_Q), 1)
        dsk = jnp.zeros((1, N_Q), F32)
        dqg = jnp.zeros((1, PAIR), F32)
        biases = [bias_ref[jnp.minimum(steps - 1 - step, 1)]] + [bias_ref[1]] * (SUB - 1)
        units = [(s, hq) for s in range(SUB) for hq in range(N_Q)]
        rq, qh, q_one, do_one, delta = [], [], [], [], []
        for pair in range(N_Q // 2):
            cols = slice(PAIR * pair, PAIR * (pair + 1))
            q = p_ref[:, O_Q + PAIR * pair:O_Q + PAIR * (pair + 1)].astype(F32)
            rq.append(_pair_rms(q, low))
            qh.append(q * rq[pair])
            qn = (qh[pair] * q_gain).astype(BF16)
            do = dao[:, cols]
            do_b = do.astype(BF16)
            delta += _half_sums(do * aov[:, cols], low)
            for mine in (low, high):
                q_one.append(jnp.where(mine, qn, jnp.zeros_like(qn)))
                do_one.append(jnp.where(mine, do_b, jnp.zeros_like(do_b)))
        scores = {u: _dot_nt(q_one[u[1]][_block(u[0])], k_of[u[1] // GRP][_band(u[0])]) + biases[u[0]] for u in units}
        dprobs = {u: _dot_nt(do_one[u[1]][_block(u[0])], v_of[u[1] // GRP][_band(u[0])]) for u in units}
        probs = {u: _softmax_with_sink(scores[u], sinks_ref[u[1]]) for u in units}
        ds = {u: (probs[u][0] * (dprobs[u] - delta[u[1]][_block(u[0])])).astype(BF16) for u in units}
        dv_t = {(s, h): jnp.zeros((PAIR, 2 * BLK), F32) for s in range(SUB) for h in range(N_KV)}
        dkn_t = dict(dv_t)
        for s, hq in units:
            dsk = dsk - jnp.where(lane8 == hq, jnp.sum(probs[s, hq][1] * delta[hq][_block(s)]), 0.0)
            dv_t[s, hq // GRP] = dv_t[s, hq // GRP] + _dot_tn(do_one[hq][_block(s)], probs[s, hq][0].astype(BF16))
            dkn_t[s, hq // GRP] = dkn_t[s, hq // GRP] + _dot_tn(q_one[hq][_block(s)], ds[s, hq])
        dqn_of = {u: _dot(ds[u], k_of[u[1] // GRP][_band(u[0])]) for u in units}
        for pair in range(N_Q // 2):
            dqn = jnp.concatenate([jnp.where(low, dqn_of[s, 2 * pair], dqn_of[s, 2 * pair + 1]) for s in range(SUB)],
                                  axis=0)
            dqg = dqg + jnp.sum(dqn * qh[pair], axis=0, keepdims=True)
            dqh = dqn * q_gain
            dp_ref[:, O_Q + PAIR * pair:O_Q + PAIR * (pair + 1)] = (
                rq[pair] * (dqh - qh[pair] * _pair_mean(dqh * qh[pair], low))).astype(BF16)

        def over_key_rows(parts):
            bands = [jnp.concatenate([parts[s, h][:HEAD] + parts[s, h][HEAD:] for h in range(N_KV)], axis=0).T
                     for s in range(SUB)]
            pieces = [bands[0][:BLK]]
            pieces += [bands[s - 1][BLK:] + bands[s][:BLK] for s in range(1, SUB)]
            pieces.append(bands[SUB - 1][BLK:])
            return jnp.concatenate(pieces, axis=0)

        dv = over_key_rows(dv_t)
        dkn = over_key_rows(dkn_t)
        dkg_ref[...] += jnp.sum(dkn * kh, axis=0, keepdims=True)
        dkh = dkn * kg_ref[...]
        dk = rk * (dkh - kh * _pair_mean(dkh * kh, low))
        last = slice(ROWS, ROWS + BLK)
        dp_ref[:, O_K:O_V] = jnp.concatenate([dk[BLK:ROWS], dk[last] + kcar[...]], axis=0).astype(BF16)
        dp_ref[:, O_V:] = jnp.concatenate([dv[BLK:ROWS], dv[last] + vcar[...]], axis=0).astype(BF16)
        kcar[...] = dk[:BLK, :]
        vcar[...] = dv[:BLK, :]
        dsk_ref[...] += dsk
        dqg_ref[...] += dqg * SCALE

        bg = p_ref[:, 0:CONV_CH].astype(F32)
        cg = p_ref[:, CONV_CH:2 * CONV_CH].astype(F32)
        hc = p_ref[:, 2 * CONV_CH:3 * CONV_CH].astype(F32)
        yv = y_ref[...].astype(F32)
        dmc = dmix_scr[:, 0:CONV_CH]
        co = bg * yv
        rc = _rms(co)
        ch = co * rc
        dcog_ref[...] += jnp.sum(dmc * ch, axis=0, keepdims=True)
        dco = _rms_bwd(dmc, ch, rc, cog_ref[...])
        dp_ref[:, 0:CONV_CH] = (dco * yv).astype(BF16)
        dy = dco * bg
        row = lax.broadcasted_iota(jnp.int32, (ROWS, CONV_CH), 0)
        nxt = dycar[...]
        dy1 = jnp.where(row == ROWS - 1, nxt[0:1, :], pltpu.roll(dy, ROWS - 1, 0))
        dy2 = jnp.where(row == ROWS - 2, nxt[0:1, :],
                        jnp.where(row == ROWS - 1, nxt[1:2, :], pltpu.roll(dy, ROWS - 2, 0)))
        dycar[...] = dy[0:8, :]
        du = cw_ref[2:3, :] * dy + cw_ref[1:2, :] * dy1 + cw_ref[0:1, :] * dy2
        dp_ref[:, CONV_CH:2 * CONV_CH] = (du * hc).astype(BF16)
        dp_ref[:, 2 * CONV_CH:3 * CONV_CH] = (du * cg).astype(BF16)
        u = cg * hc
        dcw_ref[0:1, :] += jnp.sum(dy2 * u, axis=0, keepdims=True)
        dcw_ref[1:2, :] += jnp.sum(dy1 * u, axis=0, keepdims=True)
        dcw_ref[2:3, :] += jnp.sum(dy * u, axis=0, keepdims=True)

        dmix_scr[...] = _dot_nt(dx_next_ref[...].astype(BF16), wo_ref[...])

    small = lambda shape: pl.BlockSpec(shape, lambda s: (0, 0))
    blk = lambda w: pl.BlockSpec((ROWS, w), lambda s: (steps - 1 - s, 0))
    return _pallas_after(
        dep, body, sinks, proj, proj, bias, y, ao, dx1, dx1, w_o, conv_w, qg, kg, cog, aog, name=name, grid=(steps,),
        in_specs=[pl.BlockSpec(memory_space=pltpu.SMEM),
                  blk(IN_COLS),
                  pl.BlockSpec((BLK, KV_W), lambda s: (jnp.maximum(SUB * (steps - 1 - s) - 1, 0), O_K // KV_W)),
                  pl.BlockSpec((2, BLK, 2 * BLK), lambda s: (0, 0, 0)),
                  blk(CONV_CH), blk(ATTN_W),
                  pl.BlockSpec((ROWS, D_MODEL), lambda s: (steps - 1, 0)),
                  pl.BlockSpec((ROWS, D_MODEL), lambda s: (jnp.maximum(steps - 2 - s, 0), 0)),
                  small((D_MODEL, D_MODEL)),
                  small((3, CONV_CH)), small((1, PAIR)), small((1, PAIR)), small((1, CONV_CH)), small((1, ATTN_W))],
        out_specs=[blk(IN_COLS), small((3, CONV_CH)), small((1, PAIR)), small((1, PAIR)), small((1, N_Q)),
                   small((1, CONV_CH)), small((1, ATTN_W))],
        out_shape=[jax.ShapeDtypeStruct((S, IN_COLS), BF16), jax.ShapeDtypeStruct((3, CONV_CH), F32),
                   jax.ShapeDtypeStruct((1, PAIR), F32), jax.ShapeDtypeStruct((1, PAIR), F32),
                   jax.ShapeDtypeStruct((1, N_Q), F32), jax.ShapeDtypeStruct((1, CONV_CH), F32),
                   jax.ShapeDtypeStruct((1, ATTN_W), F32)],
        scratch_shapes=[pltpu.VMEM((8, CONV_CH), F32), pltpu.VMEM((BLK, PAIR), F32), pltpu.VMEM((BLK, PAIR), F32),
                        pltpu.VMEM((ROWS, D_MODEL), F32)],
        compiler_params=_cparams(("arbitrary",)),
    )


OTHER_CHIPS = ((1, 0), (0, 1), (1, 1))


def _position():
    return lax.axis_index("x"), lax.axis_index("y"), lax.axis_index("c")


def all_gather(shards, name):
    n = len(shards)

    def body(*refs):
        x_refs, out_refs = refs[:n], refs[n:2 * n]
        send_sems, recv_sems, local_sems = refs[2 * n:]
        x, y, c = _position()
        me, sibling = (x, y, c), (x, y, 1 - c)
        chips = [(x ^ mx, y ^ my) for mx, my in OTHER_CHIPS]

        def slab(a, px, py, pc):
            return out_refs[a].at[4 * px + 2 * py + pc]

        def copy(a, k, block, to, src=None):
            return pltpu.make_async_remote_copy(
                src_ref=slab(a, *block) if src is None else src, dst_ref=slab(a, *block),
                send_sem=send_sems.at[7 * a + k], recv_sem=recv_sems.at[7 * a + k], device_id=to, device_id_type=MESH_T)

        mine = [pltpu.make_async_copy(x_refs[a], slab(a, *me), local_sems.at[a]) for a in range(n)]
        for cp in mine:
            cp.start()
        first = []
        for a in range(n):
            first.append(copy(a, 0, me, sibling, src=x_refs[a]))
            first += [copy(a, 1 + j, me, (*chip, c), src=x_refs[a]) for j, chip in enumerate(chips)]
        for cp in first:
            cp.start()
        passed = []
        for a in range(n):
            for j, chip in enumerate(chips):
                copy(a, 1 + j, (*chip, c), me).wait_recv()
                passed.append(copy(a, 4 + j, (*chip, c), sibling))
                passed[-1].start()
        for a in range(n):
            copy(a, 0, sibling, me).wait_recv()
            for j, chip in enumerate(chips):
                copy(a, 4 + j, (*chip, 1 - c), me).wait_recv()
        for cp in first + passed:
            cp.wait_send()
        for cp in mine:
            cp.wait()

    return pl.pallas_call(
        body, name=name, in_specs=[ANY] * n, out_specs=[ANY] * n,
        out_shape=[jax.ShapeDtypeStruct((N_DEV, *s.shape), s.dtype) for s in shards],
        scratch_shapes=[pltpu.SemaphoreType.DMA((7 * n,)), pltpu.SemaphoreType.DMA((7 * n,)),
                        pltpu.SemaphoreType.DMA((n,))],
    )(*shards)


class SplitCopy:
    def __init__(self, name, arrays, n_copies, plan, after=None):
        n = len(arrays)
        self.name, self.n, self.n_copies, self.plan = name, n, n_copies, plan
        extra = [] if after is None else [after]

        def body(*refs):
            in_refs = refs[:n]
            send_sems, recv_sems = refs[n + len(extra)], refs[n + len(extra) + 1]
            token = refs[2 * n + len(extra) + 2]
            for k, (src, dst, to) in enumerate(plan(_position(), in_refs)):
                pltpu.make_async_remote_copy(src_ref=src, dst_ref=dst, send_sem=send_sems.at[k],
                                             recv_sem=recv_sems.at[k], device_id=to, device_id_type=MESH_T).start()
            token[...] = jnp.zeros_like(token)

        outs = pl.pallas_call(
            body, name=name + "_start",
            out_shape=(pltpu.SemaphoreType.DMA((n_copies,)), pltpu.SemaphoreType.DMA((n_copies,)),
                       *[pltpu.HBM(a.shape, a.dtype) for a in arrays], jax.ShapeDtypeStruct((8, 128), F32)),
            in_specs=[HBM] * n + [ANY] * len(extra),
            out_specs=(SEM, SEM, *[HBM] * n, pl.BlockSpec(memory_space=pltpu.VMEM)),
            input_output_aliases={i: 2 + i for i in range(n)},
            compiler_params=pltpu.CompilerParams(has_side_effects=pltpu.SideEffectType.DATAFLOW_SIDE_EFFECTING),
        )(*[pltpu.with_memory_space_constraint(a, pltpu.HBM) for a in arrays], *extra)
        self.send_sems, self.recv_sems = outs[0], outs[1]
        self.arrays, self.token = list(outs[2:2 + n]), outs[2 + n]

    def wait(self, after):
        n, plan = self.n, self.plan

        def body(*refs):
            in_refs, send_sems, recv_sems = refs[:n], refs[n], refs[n + 1]
            for k, (src, dst, to) in enumerate(plan(_position(), in_refs)):
                cp = pltpu.make_async_remote_copy(src_ref=src, dst_ref=dst, send_sem=send_sems.at[k],
                                                  recv_sem=recv_sems.at[k], device_id=to, device_id_type=MESH_T)
                cp.wait_send()
                cp.wait_recv()

        outs = pl.pallas_call(
            body, name=self.name + "_wait",
            out_shape=tuple(pltpu.HBM(a.shape, a.dtype) for a in self.arrays),
            in_specs=[HBM] * n + [SEM, SEM, ANY], out_specs=tuple([HBM] * n),
            input_output_aliases={i: i for i in range(n)},
            compiler_params=pltpu.CompilerParams(has_side_effects=pltpu.SideEffectType.DATAFLOW_SIDE_EFFECTING),
        )(*self.arrays, self.send_sems, self.recv_sems, after)
        return list(outs)


def gather_start(shards, me, name, after=None):
    n = len(shards)
    lands = [lax.dynamic_update_slice(lax.empty((N_DEV, *s.shape), s.dtype), s[None], (me, 0, 0)) for s in shards]

    def plan(pos, refs):
        x, y, c = pos
        return [(refs[a], refs[n + a].at[4 * x + 2 * y + c], (x ^ mx, y ^ my, c))
                for a in range(n) for mx, my in OTHER_CHIPS]

    return SplitCopy(name, list(shards) + lands, 3 * n, plan, after=after)


def sibling_start(lands, name):
    n = len(lands)

    def plan(pos, refs):
        x, y, c = pos
        return [(refs[a].at[2 * q + c], refs[a].at[2 * q + c], (x, y, 1 - c)) for a in range(n) for q in range(4)]

    return SplitCopy(name, list(lands), 4 * n, plan)


def scatter_start(slabs, name):
    n = len(slabs)
    lands = [lax.empty((N_DEV - 1, *g.shape[1:]), g.dtype) for g in slabs]

    def plan(pos, refs):
        x, y, c = pos
        copies = []
        for a in range(n):
            for r in range(1, N_DEV):
                px, py, pc = x ^ ((r >> 2) & 1), y ^ ((r >> 1) & 1), c ^ (r & 1)
                copies.append((refs[a].at[4 * px + 2 * py + pc], refs[n + a].at[r - 1], (px, py, pc)))
        return copies

    return SplitCopy(name, list(slabs) + lands, (N_DEV - 1) * n, plan)


def all_reduce_small(v, name):
    R, W = v.shape

    def body(v_ref, o_ref, recv, send_sems, recv_sems):
        x, y, c = _position()
        me = 4 * x + 2 * y + c
        copies = []
        for r in range(1, N_DEV):
            to = (x ^ ((r >> 2) & 1), y ^ ((r >> 1) & 1), c ^ (r & 1))
            copies.append(pltpu.make_async_remote_copy(
                src_ref=v_ref, dst_ref=recv.at[me], send_sem=send_sems.at[r - 1], recv_sem=recv_sems.at[r - 1],
                device_id=to, device_id_type=MESH_T))
        for cp in copies:
            cp.start()
        recv[pl.ds(me, 1)] = v_ref[...][None]
        for cp in copies:
            cp.wait()
        acc = recv[0]
        for s in range(1, N_DEV):
            acc = acc + recv[s]
        o_ref[...] = acc

    return pl.pallas_call(
        body, name=name,
        in_specs=[pl.BlockSpec(memory_space=pltpu.VMEM)], out_specs=pl.BlockSpec(memory_space=pltpu.VMEM),
        out_shape=jax.ShapeDtypeStruct((R, W), F32),
        scratch_shapes=[pltpu.VMEM((N_DEV, R, W), F32), pltpu.SemaphoreType.DMA((N_DEV - 1,)),
                        pltpu.SemaphoreType.DMA((N_DEV - 1,))],
    )(v)


def _row_tile(rows):
    if rows <= 512:
        return rows
    return max(t for t in range(8, 513, 8) if rows % t == 0)


def _adamw_update(w, g, m, v):
    mn = ADAM_B1 * m + (1.0 - ADAM_B1) * g
    vn = ADAM_B2 * v + (1.0 - ADAM_B2) * (g * g)
    m_hat = mn / (1.0 - ADAM_B1 ** ADAM_STEP)
    v_hat = vn / (1.0 - ADAM_B2 ** ADAM_STEP)
    return -ADAM_LR * (m_hat / (jnp.sqrt(v_hat) + ADAM_EPS) + ADAM_WD * w), mn, vn


def adamw(w, g, m, v, name):
    R, W = w.shape
    tr = _row_tile(R)

    def body(w_ref, g_ref, m_ref, v_ref, d_ref, mo_ref, vo_ref):
        d_ref[...], mo_ref[...], vo_ref[...] = _adamw_update(w_ref[...], g_ref[...], m_ref[...], v_ref[...])

    spec = pl.BlockSpec((tr, W), lambda i: (i, 0))
    return pl.pallas_call(
        body, name=name, grid=(R // tr,), in_specs=[spec] * 4, out_specs=[spec] * 3,
        out_shape=[jax.ShapeDtypeStruct((R, W), F32)] * 3,
        compiler_params=_cparams(("parallel",)),
    )(w, g, m, v)


def reduce_adamw(slabs, land, w, m, v, layer, me_arr, name, others=None):
    _, R, W = slabs.shape
    tr = max(t for t in range(16, R // 2 + 1, 16) if R % t == 0)
    n_other = 0 if others is None else 4

    def body(me_ref, s_ref, l_ref, w_ref, m_ref, v_ref, *rest):
        g_ref, d_ref, mo_ref, vo_ref = rest[n_other:]
        g = s_ref[0].astype(F32)
        for r in range(N_DEV - 1):
            g = g + l_ref[r].astype(F32)
        g_ref[0] = g
        d_ref[0], mo_ref[0], vo_ref[0] = _adamw_update(w_ref[0], g, m_ref[0], v_ref[0])

    spec = pl.BlockSpec((1, tr, W), lambda i, me: (layer, i, 0))
    return pl.pallas_call(
        body, name=name,
        grid_spec=pltpu.PrefetchScalarGridSpec(
            num_scalar_prefetch=1, grid=(R // tr,),
            in_specs=[pl.BlockSpec((1, tr, W), lambda i, me: (me[0], i, 0)),
                      pl.BlockSpec((N_DEV - 1, tr, W), lambda i, me: (0, i, 0)), spec, spec, spec] + [ANY] * n_other,
            out_specs=[spec] * 4),
        out_shape=[jax.ShapeDtypeStruct((DEPTH, R, W), F32)] * 4,
        input_output_aliases={6 + i: i for i in range(n_other)},
        compiler_params=_cparams(("parallel",)),
    )(me_arr, slabs, land, w, m, v, *([] if others is None else others))


SMALL_NAMES = ("norm1_g", "q_norm_g", "k_norm_g", "sinks", "conv_out_g", "attn_out_g", "norm2_g", "conv_w")
SMALL_SIZES = (D_MODEL, HEAD, HEAD, N_Q, CONV_CH, ATTN_W, D_MODEL, 3 * CONV_CH)
SMALL_ROWS = 80


def kernel(x, norm1_g, w_in, conv_w, q_norm_g, k_norm_g, sinks, conv_out_g, attn_out_g, w_o, norm2_g, w_gate, w_up, w_down, loss_target, m_norm1_g, m_w_in, m_conv_w, m_q_norm_g, m_k_norm_g, m_sinks, m_conv_out_g, m_attn_out_g, m_w_o, m_norm2_g, m_w_gate, m_w_up, m_w_down, v_norm1_g, v_w_in, v_conv_w, v_q_norm_g, v_k_norm_g, v_sinks, v_conv_out_g, v_attn_out_g, v_w_o, v_norm2_g, v_w_gate, v_w_up, v_w_down):
    xi, yi, ci = _position()
    me = 4 * xi + 2 * yi + ci
    me_arr = jnp.reshape(me, (1,)).astype(jnp.int32)
    xs, tgt = x[0], loss_target[0]
    bf = lambda a: a.astype(BF16)
    bias = band_bias()

    t = lambda a: jnp.swapaxes(a, 1, 2)
    shard = dict(w_in=(t(w_in), t(m_w_in), t(v_w_in)), w_o=(w_o, m_w_o, v_w_o),
                 w_gate=(t(w_gate), t(m_w_gate), t(v_w_gate)), w_up=(t(w_up), t(m_w_up), t(v_w_up)),
                 w_down=(w_down, m_w_down, v_w_down))
    wb = {n: bf(shard[n][0]) for n in shard}

    g_in0, g_o0, g_conv = all_gather([wb["w_in"][0], wb["w_o"][0], conv_w.reshape(DEPTH * 3, HEAD)], "gather_first")
    ag_ffn0 = gather_start([wb["w_gate"][0], wb["w_up"][0], wb["w_down"][0]], me, "gather_ffn0", after=g_in0)
    ag_mix1 = gather_start([wb["w_in"][1], wb["w_o"][1]], me, "gather_mix1", after=ag_ffn0.token)
    ag_ffn1 = gather_start([wb["w_gate"][1], wb["w_up"][1], wb["w_down"][1]], me, "gather_ffn1", after=ag_mix1.token)
    conv_full = g_conv.reshape(N_DEV, DEPTH, 3, HEAD).transpose(1, 2, 0, 3).reshape(DEPTH, 3, CONV_CH)
    pair_gain = lambda g: jnp.tile(g[None], (1, 2))
    small = [dict(norm1_g=norm1_g[l][None], conv_w=conv_full[l], q_norm_g=pair_gain(q_norm_g[l]),
                  k_norm_g=pair_gain(k_norm_g[l]), sinks=sinks[l], conv_out_g=conv_out_g[l][None],
                  attn_out_g=attn_out_g[l][None], norm2_g=norm2_g[l][None]) for l in range(DEPTH)]
    whole = lambda g: g.reshape(-1, D_MODEL)
    weights = [dict(w_in=whole(g_in0), w_o=whole(g_o0)), {}]

    def ffn_weights(g_gate, g_up, g_down):
        return dict(w_gate=whole(g_gate), w_up=whole(g_up), w_down=whole(g_down))

    saved = []
    xl = xs
    for l in range(DEPTH):
        sp, wl = small[l], weights[l]
        h, proj = norm_proj(xl, sp["norm1_g"], wl["w_in"], f"norm_proj{l}",
                            dep=ag_ffn1.token if l == 0 else pass_ffn1.token)
        mix, y, ao, x1 = mixer_fwd(proj, bias, sp["sinks"], sp["conv_w"], sp["q_norm_g"], sp["k_norm_g"],
                                   sp["conv_out_g"], sp["attn_out_g"], f"mixer_fwd{l}", residual=xl, w_o=wl["w_o"])
        if l == 0:
            pass_ffn0 = sibling_start(ag_ffn0.wait(x1)[3:], "pass_ffn0")
            wl.update(ffn_weights(*pass_ffn0.wait(pass_ffn0.token)))
            pass_mix1 = sibling_start(ag_mix1.wait(x1)[2:], "pass_mix1")
            h2, a, b, f, x2 = ffn_fwd(x1, sp["norm2_g"], wl["w_gate"], wl["w_up"], wl["w_down"], "ffn_fwd0",
                                      dep=pass_mix1.token)
            g_in, g_o = pass_mix1.wait(x2)
            weights[1] = dict(w_in=whole(g_in), w_o=whole(g_o))
            pass_ffn1 = sibling_start(ag_ffn1.wait(x2)[3:], "pass_ffn1")
        else:
            wl.update(ffn_weights(*pass_ffn1.wait(x1)))
            h2, a, b, f, dx, loss_row = ffn_fwd(x1, sp["norm2_g"], wl["w_gate"], wl["w_up"], wl["w_down"], "ffn_fwd1",
                                                tgt=tgt)
            x2 = None
        saved.append((xl, h, proj, mix, y, ao, x1, h2, a, b, f))
        xl = x2

    stepped = {n: None for n in shard}
    slabs = lambda d: d.reshape(N_DEV, -1, D_MODEL)
    gsmall = [None] * DEPTH

    def finish(sc, names, after, l):
        arrays = sc.wait(after)
        k = len(names)
        for i, n in enumerate(names):
            w, m, v = shard[n]
            stepped[n] = reduce_adamw(arrays[i], arrays[k + i], w, m, v, l, me_arr, f"reduce_adamw_{n}{l}",
                                      others=stepped[n])

    for l in reversed(range(DEPTH)):
        sp, wl = small[l], weights[l]
        x0, h, proj, mix, y, ao, x1, h2, a, b, f = saved[l]
        (d_wd,) = grad_weight((f,), dx, TF, f"grad_w_down{l}", tk=2048)
        sc_down = scatter_start([slabs(d_wd)], f"scatter_w_down{l}")
        da, db, dx1, d_g2 = ffn_bwd(dx, a, b, x1, sp["norm2_g"], wl["w_gate"], wl["w_up"], wl["w_down"],
                                    f"ffn_bwd{l}", dep=sc_down.token)
        d_wg, d_wu = grad_weight((da, db), h2, TF, f"grad_w_gate_up{l}")
        (d_wo,) = grad_weight((mix,), dx1, D_MODEL, f"grad_w_o{l}", tk=2048)
        sc_rest = scatter_start([slabs(d_wg), slabs(d_wu), slabs(d_wo)], f"scatter_w_gate_up_o{l}")
        dproj, d_cw, d_qg, d_kg, d_sk, d_cog, d_aog = mixer_bwd(
            proj, bias, y, ao, dx1, wl["w_o"], sp["sinks"], sp["conv_w"], sp["q_norm_g"], sp["k_norm_g"],
            sp["conv_out_g"], sp["attn_out_g"], f"mixer_bwd{l}", dep=sc_rest.token)
        finish(sc_down, ["w_down"], dproj, l)
        finish(sc_rest, ["w_gate", "w_up", "w_o"], dproj, l)
        (d_win,) = grad_weight((dproj,), h, IN_COLS // 2, f"grad_w_in{l}", tk=2048)
        sc_in = scatter_start([slabs(d_win)], f"scatter_w_in{l}")
        dx, d_g1 = proj_bwd(dproj, wl["w_in"], x0, sp["norm1_g"], dx1, f"proj_bwd{l}", dep=sc_in.token)
        finish(sc_in, ["w_in"], dx, l)
        both_heads = lambda d: d[:, :HEAD] + d[:, HEAD:]
        gsmall[l] = dict(norm1_g=d_g1, conv_w=d_cw, q_norm_g=both_heads(d_qg), k_norm_g=both_heads(d_kg), sinks=d_sk,
                         conv_out_g=d_cog, attn_out_g=d_aog, norm2_g=d_g2)
    grad_x = dx[None]

    flat = jnp.concatenate([loss_row[0, 0:1]] + [gsmall[l][n].reshape(-1) for l in range(DEPTH) for n in SMALL_NAMES])
    flat = jnp.pad(flat, (0, SMALL_ROWS * 128 - flat.shape[0])).reshape(SMALL_ROWS, 128)
    flat = all_reduce_small(flat, "all_reduce_small_grads").reshape(-1)
    loss = flat[0]
    gs = {n: [] for n in SMALL_NAMES}
    off = 1
    for l in range(DEPTH):
        for n, size in zip(SMALL_NAMES, SMALL_SIZES):
            gs[n].append(flat[off:off + size])
            off += size
    gs = {n: jnp.stack(v) for n, v in gs.items()}
    g_conv = lax.dynamic_slice(gs["conv_w"].reshape(DEPTH, 3, CONV_CH), (0, 0, me * HEAD), (DEPTH, 3, HEAD))

    gs["conv_w"] = g_conv
    params = dict(norm1_g=(norm1_g, m_norm1_g, v_norm1_g), conv_w=(conv_w, m_conv_w, v_conv_w),
                  q_norm_g=(q_norm_g, m_q_norm_g, v_q_norm_g), k_norm_g=(k_norm_g, m_k_norm_g, v_k_norm_g),
                  sinks=(sinks, m_sinks, v_sinks), conv_out_g=(conv_out_g, m_conv_out_g, v_conv_out_g),
                  attn_out_g=(attn_out_g, m_attn_out_g, v_attn_out_g), norm2_g=(norm2_g, m_norm2_g, v_norm2_g))
    names = ("norm1_g", "w_in", "conv_w", "q_norm_g", "k_norm_g", "sinks", "conv_out_g", "attn_out_g", "w_o",
             "norm2_g", "w_gate", "w_up", "w_down")

    out = {}
    for n in names:
        if n in shard:
            out[n] = tuple(t(r) for r in stepped[n]) if n in ("w_in", "w_gate", "w_up") else stepped[n]
        else:
            w, m, v = params[n]
            two_d = (-1, w.shape[-1])
            d, mn, vn = adamw(w.reshape(two_d), gs[n].reshape(two_d), m.reshape(two_d), v.reshape(two_d), f"adamw_{n}")
            out[n] = (gs[n].reshape(w.shape), d.reshape(w.shape), mn.reshape(w.shape), vn.reshape(w.shape))
    return (loss, grad_x, *[out[n][i] for i in range(4) for n in names])
```
